```python
import jax
import jax.numpy as jnp
from jax import lax
import numpy as np

D_MODEL = 1024
BATCH = 8
SEQ = 4096
DEPTH = 2

GRID_W = 64
CTX_LEN = 256
N_ATTN_LAYERS = (DEPTH + 1) // 2
N_RWKV_LAYERS = DEPTH // 2
NORM_EPS = 1e-6
ROPE_THETA = 10000.0
Q_BLOCK = 128
NEG_INF = -1e30
N_MODS = 6

MLA_HEADS = 4
MLA_Q_RANK = 384
MLA_KV_RANK = 256
MLA_NOPE = 128
MLA_ROPE = 64
MLA_V = 128

SWA_HEADS = 8
SWA_KV_HEADS = 2
SWA_GROUP = SWA_HEADS // SWA_KV_HEADS
SWA_HEAD_DIM = 64
WINDOW = 128

IN_SPLITS = (MLA_Q_RANK,
             MLA_Q_RANK + MLA_KV_RANK,
             MLA_Q_RANK + MLA_KV_RANK + MLA_ROPE,
             MLA_Q_RANK + MLA_KV_RANK + MLA_ROPE + SWA_HEADS * SWA_HEAD_DIM,
             MLA_Q_RANK + MLA_KV_RANK + MLA_ROPE + (SWA_HEADS + SWA_KV_HEADS) * SWA_HEAD_DIM)
IN_WIDTH = MLA_Q_RANK + MLA_KV_RANK + MLA_ROPE + (SWA_HEADS + 2 * SWA_KV_HEADS) * SWA_HEAD_DIM
MIX_WIDTH = MLA_HEADS * MLA_V + SWA_HEADS * SWA_HEAD_DIM

RWKV_HEAD = 64
RWKV_HEADS = D_MODEL // RWKV_HEAD
DECAY_LORA = 64
ICLR_LORA = 64
GATE_LORA = 128
GN_EPS = 64e-5

N_EXPERTS = 64
TOP_K = 6
N_GROUPS = 8
TOPK_GROUPS = 4
D_EXPERT = 256
D_SHARED = 256
ROUTED_SCALE = 2.5
EXPERT_CHUNK = 8

kernel_name = 'hybrid_mla_swa_rwkv7_moe_prefix_dit'


def rms_norm(x, g):
    xf = x.astype(jnp.float32)
    y = xf * lax.rsqrt(jnp.mean(xf * xf, axis=-1, keepdims=True) + NORM_EPS)
    return (y * g.astype(jnp.float32)).astype(x.dtype)


def modulate(n, shift, scale):
    return n * (1.0 + scale) + shift


def axial_rope_tables(n_lat, n_ctx, dim):
    rows = n_lat // GRID_W
    nf = dim // 4
    inv = ROPE_THETA ** (-jnp.arange(nf, dtype=jnp.float32) / nf)
    row = jnp.repeat(jnp.arange(rows, dtype=jnp.float32), GRID_W)
    col = jnp.tile(jnp.arange(GRID_W, dtype=jnp.float32), rows)
    ar = row[:, None] * inv
    ac = col[:, None] * inv
    ang = jnp.concatenate([ar, ar, ac, ac], axis=-1)
    cos = jnp.concatenate([jnp.ones((n_ctx, dim), jnp.float32), jnp.cos(ang)], axis=0)
    sin = jnp.concatenate([jnp.zeros((n_ctx, dim), jnp.float32), jnp.sin(ang)], axis=0)
    return cos, sin


def rotate_half_axial(x):
    lead = x.shape[:-1]
    d = x.shape[-1]
    xr = x.reshape(*lead, 2, 2, d // 4)
    rot = jnp.stack([-xr[..., 1, :], xr[..., 0, :]], axis=-2)
    return rot.reshape(*lead, d)


def apply_rope(x, cos, sin):
    cos = cos.astype(x.dtype)[:, None, :]
    sin = sin.astype(x.dtype)[:, None, :]
    return x * cos + rotate_half_axial(x) * sin


def softmax_attend(q, k, v, scale):
    s = jnp.einsum('bqhd,bkhd->bhqk', q, k).astype(jnp.float32) * scale
    p = jax.nn.softmax(s, axis=-1).astype(v.dtype)
    return jnp.einsum('bhqk,bkhd->bqhd', p, v)


def dense_attend_blocks(q_lat, k_all, v_all, scale):
    B, S = q_lat.shape[:2]
    nb = S // Q_BLOCK
    qb = jnp.moveaxis(q_lat.reshape(B, nb, Q_BLOCK, *q_lat.shape[2:]), 1, 0)
    out = lax.map(lambda qj: softmax_attend(qj, k_all, v_all, scale), qb)
    return jnp.moveaxis(out, 0, 1).reshape(B, S, *out.shape[3:])


def sink_softmax(s, sink):
    sk = sink.astype(jnp.float32)[None, :, :, None, None]
    m = jnp.maximum(jnp.max(s, axis=-1, keepdims=True), sk)
    p = jnp.exp(s - m)
    return p / (jnp.sum(p, axis=-1, keepdims=True) + jnp.exp(sk - m))


def ctx_sink_attend(q, k, v, sink, scale):
    s = jnp.einsum('bqhgd,bkhd->bhgqk', q, k).astype(jnp.float32) * scale
    p = sink_softmax(s, sink).astype(v.dtype)
    return jnp.einsum('bhgqk,bkhd->bqhgd', p, v)


def window_sink_attend(q_lat, k_ctx, v_ctx, k_lat, v_lat, sink, scale):
    B, S = q_lat.shape[:2]
    nb = S // Q_BLOCK
    band = Q_BLOCK + 2 * WINDOW
    pad = ((0, 0), (WINDOW, WINDOW), (0, 0), (0, 0))
    kp = jnp.pad(k_lat, pad)
    vp = jnp.pad(v_lat, pad)
    qb = jnp.moveaxis(q_lat.reshape(B, nb, Q_BLOCK, *q_lat.shape[2:]), 1, 0)

    def block(args):
        j, qj = args
        start = j * Q_BLOCK
        kb = lax.dynamic_slice_in_dim(kp, start, band, axis=1)
        vb = lax.dynamic_slice_in_dim(vp, start, band, axis=1)
        qpos = start + jnp.arange(Q_BLOCK)
        kpos = start - WINDOW + jnp.arange(band)
        valid = (jnp.abs(qpos[:, None] - kpos[None, :]) <= WINDOW) & (kpos >= 0)[None, :] & (kpos < S)[None, :]
        s_band = jnp.einsum('bqhgd,bkhd->bhgqk', qj, kb).astype(jnp.float32) * scale
        s_band = jnp.where(valid, s_band, NEG_INF)
        s_ctx = jnp.einsum('bqhgd,bkhd->bhgqk', qj, k_ctx).astype(jnp.float32) * scale
        p = sink_softmax(jnp.concatenate([s_ctx, s_band], axis=-1), sink).astype(v_lat.dtype)
        return jnp.einsum('bhgqk,bkhd->bqhgd', p, jnp.concatenate([v_ctx, vb], axis=1))

    out = lax.map(block, (jnp.arange(nb), qb))
    return jnp.moveaxis(out, 0, 1).reshape(B, S, *q_lat.shape[2:])


def attn_mixer(n_ctx, n_lat, rope_mla, rope_swa, w_in, q_norm, kv_norm, w_uq, w_ukv, sinks, w_o, with_ctx):
    B, Lc, _ = n_ctx.shape
    S = n_lat.shape[1]
    L = Lc + S
    u = jnp.concatenate([n_ctx, n_lat], axis=1) @ w_in
    cq, ckv, kr, qs, ks, vs = jnp.split(u, IN_SPLITS, axis=-1)
    cos, sin = rope_mla
    q = (rms_norm(cq, q_norm) @ w_uq).reshape(B, L, MLA_HEADS, MLA_NOPE + MLA_ROPE)
    kv = (rms_norm(ckv, kv_norm) @ w_ukv).reshape(B, L, MLA_HEADS, MLA_NOPE + MLA_V)
    q = jnp.concatenate([q[..., :MLA_NOPE], apply_rope(q[..., MLA_NOPE:], cos, sin)], axis=-1)
    k_rope = apply_rope(kr[:, :, None, :], cos, sin)
    k = jnp.concatenate([kv[..., :MLA_NOPE], jnp.broadcast_to(k_rope, (B, L, MLA_HEADS, MLA_ROPE))], axis=-1)
    v = kv[..., MLA_NOPE:]
    scale_a = (MLA_NOPE + MLA_ROPE) ** -0.5
    a_lat = dense_attend_blocks(q[:, Lc:], k, v, scale_a)
    cos, sin = rope_swa
    qs = apply_rope(qs.reshape(B, L, SWA_HEADS, SWA_HEAD_DIM), cos, sin)
    qs = qs.reshape(B, L, SWA_KV_HEADS, SWA_GROUP, SWA_HEAD_DIM)
    ks = apply_rope(ks.reshape(B, L, SWA_KV_HEADS, SWA_HEAD_DIM), cos, sin)
    vs = vs.reshape(B, L, SWA_KV_HEADS, SWA_HEAD_DIM)
    sink = sinks.reshape(SWA_KV_HEADS, SWA_GROUP)
    scale_b = SWA_HEAD_DIM ** -0.5
    b_lat = window_sink_attend(qs[:, Lc:], ks[:, :Lc], vs[:, :Lc], ks[:, Lc:], vs[:, Lc:], sink, scale_b)
    o_lat = jnp.concatenate([a_lat.reshape(B, S, -1), b_lat.reshape(B, S, -1)], axis=-1) @ w_o
    o_ctx = None
    if with_ctx:
        a_ctx = softmax_attend(q[:, :Lc], k[:, :Lc], v[:, :Lc], scale_a)
        b_ctx = ctx_sink_attend(qs[:, :Lc], ks[:, :Lc], vs[:, :Lc], sink, scale_b)
        o_ctx = jnp.concatenate([a_ctx.reshape(B, Lc, -1), b_ctx.reshape(B, Lc, -1)], axis=-1) @ w_o
    return o_ctx, o_lat


def token_shift_bi(x):
    half = x.shape[-1] // 2
    prev = jnp.pad(x[:, :-1, :half], ((0, 0), (1, 0), (0, 0)))
    nxt = jnp.pad(x[:, 1:, half:], ((0, 0), (0, 1), (0, 0)))
    return jnp.concatenate([prev, nxt], axis=-1)


def rwkv_features(n, mu, w_r, w_k, w_v, g1, g2):
    xx = token_shift_bi(n) - n
    xr, xw, xk, xv, xa, xg = [n + xx * mu[i] for i in range(6)]
    r = xr @ w_r
    k = xk @ w_k
    v = xv @ w_v
    g = jax.nn.sigmoid(xg @ g1) @ g2
    return r, k, v, g, xw, xa


def wkv7_scan(state0, r, decay, k, v, kk, a, reverse):
    def step(S, inp):
        r_t, w_t, k_t, v_t, kk_t, a_t = inp
        sa = jnp.einsum('bhvk,bhk->bhv', S, kk_t)
        S = S * w_t[:, :, None, :] - sa[..., None] * (kk_t * a_t)[:, :, None, :] + v_t[..., None] * k_t[:, :, None, :]
        return S, jnp.einsum('bhvk,bhk->bhv', S, r_t)

    xs = tuple(jnp.moveaxis(t, 1, 0) for t in (r, decay, k, v, kk, a))
    S, ys = lax.scan(step, state0, xs, reverse=reverse)
    return S, jnp.moveaxis(ys, 0, 1)


def wkv_direction(state0, feats, w0, w1, w2, a0, a1, a2, k_k, k_a, r_k, reverse):
    r, k, v, _, xw, xa = feats
    B, T, _ = k.shape

    def heads(t):
        return t.astype(jnp.float32).reshape(B, T, RWKV_HEADS, RWKV_HEAD)

    w_log = -jax.nn.softplus(-(w0 + jnp.tanh(xw @ w1) @ w2).astype(jnp.float32)) - 0.5
    decay = jnp.exp(-jnp.exp(w_log))
    a = jax.nn.sigmoid((a0 + (xa @ a1) @ a2).astype(jnp.float32))
    kk = heads(k * k_k)
    kk = kk / jnp.maximum(jnp.sqrt(jnp.sum(kk * kk, axis=-1, keepdims=True)), 1e-12)
    k_mod = k.astype(jnp.float32) * (1.0 + (a - 1.0) * k_a.astype(jnp.float32))
    r_h, k_h, v_h, a_h = heads(r), heads(k_mod), heads(v), heads(a)
    state, y = wkv7_scan(state0, r_h, heads(decay), k_h, v_h, kk, a_h, reverse)
    bonus = jnp.sum(r_h * k_h * r_k.astype(jnp.float32), axis=-1, keepdims=True) * v_h
    return state, y, bonus


def rwkv_readout(y, bonus, g, ln_w, ln_b, w_o, dtype):
    B, T = y.shape[:2]
    mu = jnp.mean(y, axis=-1, keepdims=True)
    var = jnp.mean(jnp.square(y - mu), axis=-1, keepdims=True)
    yn = ((y - mu) * lax.rsqrt(var + GN_EPS)).reshape(B, T, D_MODEL)
    yn = yn * ln_w.astype(jnp.float32) + ln_b.astype(jnp.float32)
    out = (yn + bonus.reshape(B, T, D_MODEL)) * g.astype(jnp.float32)
    return out.astype(dtype) @ w_o


def rwkv_mixer(n_ctx, n_lat, mu, w_r, w_k, w_v, w_o, g1, g2, w0, w1, w2, a0, a1, a2, k_k, k_a, r_k,
               ln_w, ln_b, with_ctx):
    B = n_ctx.shape[0]
    feats_c = rwkv_features(n_ctx, mu, w_r, w_k, w_v, g1, g2)
    feats_l = rwkv_features(n_lat, mu, w_r, w_k, w_v, g1, g2)
    y_c = b_c = y_l = b_l = 0.0
    for d, reverse in ((0, False), (1, True)):
        dparams = (w0[d], w1[d], w2[d], a0[d], a1[d], a2[d], k_k, k_a, r_k)
        state0 = jnp.zeros((B, RWKV_HEADS, RWKV_HEAD, RWKV_HEAD), jnp.float32)
        s_ctx, yc, bc = wkv_direction(state0, feats_c, *dparams, reverse)
        _, yl, bl = wkv_direction(s_ctx, feats_l, *dparams, reverse)
        y_l = y_l + yl
        b_l = b_l + bl
        y_c = y_c + yc
        b_c = b_c + bc
    o_lat = rwkv_readout(y_l, b_l, feats_l[3], ln_w, ln_b, w_o, n_lat.dtype)
    o_ctx = rwkv_readout(y_c, b_c, feats_c[3], ln_w, ln_b, w_o, n_ctx.dtype) if with_ctx else None
    return o_ctx, o_lat


def moe_ffn(h, w_router, e_bias, w_gate, w_up, w_down, ws_gate, ws_up, ws_down):
    B, L, D = h.shape
    t = h.reshape(B * L, D)
    scores = jax.nn.sigmoid((t @ w_router).astype(jnp.float32))
    choice = scores + e_bias.astype(jnp.float32)
    grp = choice.reshape(-1, N_GROUPS, N_EXPERTS // N_GROUPS)
    grp_score = jnp.sum(lax.top_k(grp, 2)[0], axis=-1)
    gidx = lax.top_k(grp_score, TOPK_GROUPS)[1]
    gmask = jnp.max(jax.nn.one_hot(gidx, N_GROUPS, dtype=jnp.float32), axis=1) > 0
    emask = jnp.repeat(gmask, N_EXPERTS // N_GROUPS, axis=1)
    eidx = lax.top_k(jnp.where(emask, choice, -jnp.inf), TOP_K)[1]
    wsel = jnp.take_along_axis(scores, eidx, axis=1)
    wsel = ROUTED_SCALE * wsel / jnp.sum(wsel, axis=-1, keepdims=True)
    gates = jnp.einsum('tk,tke->te', wsel, jax.nn.one_hot(eidx, N_EXPERTS, dtype=jnp.float32)).astype(t.dtype)
    out = (jax.nn.silu(t @ ws_gate) * (t @ ws_up)) @ ws_down
    for e0 in range(0, N_EXPERTS, EXPERT_CHUNK):
        sl = slice(e0, e0 + EXPERT_CHUNK)
        hg = jnp.einsum('td,edf->tef', t, w_gate[sl])
        hu = jnp.einsum('td,edf->tef', t, w_up[sl])
        act = jax.nn.silu(hg) * hu * gates[:, sl, None]
        out = out + jnp.einsum('tef,efd->td', act, w_down[sl])
    return out.reshape(B, L, D)


def setup_inputs(seed: int = 0) -> dict:
    key = jax.random.key(seed)
    keys = iter(jax.random.split(key, 64))

    def nrm(shape, scale):
        return jax.random.normal(next(keys), shape, jnp.float32) * scale

    def gain(shape):
        return 1.0 + nrm(shape, 0.02)

    def unif(shape, lo, hi):
        return jax.random.uniform(next(keys), shape, jnp.float32, lo, hi)

    D, NA, NR = D_MODEL, N_ATTN_LAYERS, N_RWKV_LAYERS
    return {
        'x': nrm((BATCH, SEQ, D), 1.0),
        'c': nrm((BATCH, D), 1.0),
        'ctx': nrm((BATCH, CTX_LEN, D), 1.0),
        'c_ctx': nrm((D,), 1.0),
        'ada_w': nrm((DEPTH, D, N_MODS * D), 0.5 * D ** -0.5),
        'ada_b': nrm((DEPTH, N_MODS * D), 0.01),
        'norm_mix': gain((DEPTH, D)),
        'norm_ffn': gain((DEPTH, D)),
        'norm_final': gain((D,)),
        'attn_w_in': nrm((NA, D, IN_WIDTH), D ** -0.5),
        'attn_q_norm': gain((NA, MLA_Q_RANK)),
        'attn_kv_norm': gain((NA, MLA_KV_RANK)),
        'attn_w_uq': nrm((NA, MLA_Q_RANK, MLA_HEADS * (MLA_NOPE + MLA_ROPE)), MLA_Q_RANK ** -0.5),
        'attn_w_ukv': nrm((NA, MLA_KV_RANK, MLA_HEADS * (MLA_NOPE + MLA_V)), MLA_KV_RANK ** -0.5),
        'attn_sinks': nrm((NA, SWA_HEADS), 1.0),
        'attn_w_o': nrm((NA, MIX_WIDTH, D), MIX_WIDTH ** -0.5),
        'rwkv_mu': unif((NR, 6, D), 0.0, 1.0),
        'rwkv_w_r': nrm((NR, D, D), D ** -0.5),
        'rwkv_w_k': nrm((NR, D, D), D ** -0.5),
        'rwkv_w_v': nrm((NR, D, D), D ** -0.5),
        'rwkv_w_o': nrm((NR, D, D), D ** -0.5),
        'rwkv_g1': nrm((NR, D, GATE_LORA), D ** -0.5),
        'rwkv_g2': nrm((NR, GATE_LORA, D), GATE_LORA ** -0.5),
        'rwkv_w0': unif((NR, 2, D), -4.0, 1.0),
        'rwkv_w1': nrm((NR, 2, D, DECAY_LORA), D ** -0.5),
        'rwkv_w2': nrm((NR, 2, DECAY_LORA, D), DECAY_LORA ** -0.5),
        'rwkv_a0': nrm((NR, 2, D), 0.5),
        'rwkv_a1': nrm((NR, 2, D, ICLR_LORA), D ** -0.5),
        'rwkv_a2': nrm((NR, 2, ICLR_LORA, D), ICLR_LORA ** -0.5),
        'rwkv_k_k': 0.85 + nrm((NR, D), 0.05),
        'rwkv_k_a': 1.0 + nrm((NR, D), 0.05),
        'rwkv_r_k': nrm((NR, RWKV_HEADS, RWKV_HEAD), 0.1),
        'rwkv_ln_w': gain((NR, D)),
        'rwkv_ln_b': nrm((NR, D), 0.01),
        'moe_router': nrm((DEPTH, D, N_EXPERTS), D ** -0.5),
        'moe_bias': nrm((DEPTH, N_EXPERTS), 0.01),
        'moe_w_gate': nrm((DEPTH, N_EXPERTS, D, D_EXPERT), D ** -0.5),
        'moe_w_up': nrm((DEPTH, N_EXPERTS, D, D_EXPERT), D ** -0.5),
        'moe_w_down': nrm((DEPTH, N_EXPERTS, D_EXPERT, D), D_EXPERT ** -0.5),
        'moe_ws_gate': nrm((DEPTH, D, D_SHARED), D ** -0.5),
        'moe_ws_up': nrm((DEPTH, D, D_SHARED), D ** -0.5),
        'moe_ws_down': nrm((DEPTH, D_SHARED, D), D_SHARED ** -0.5),
    }


def reference(x, c, ctx, c_ctx, ada_w, ada_b, norm_mix, norm_ffn, norm_final,
              attn_w_in, attn_q_norm, attn_kv_norm, attn_w_uq, attn_w_ukv, attn_sinks, attn_w_o,
              rwkv_mu, rwkv_w_r, rwkv_w_k, rwkv_w_v, rwkv_w_o, rwkv_g1, rwkv_g2,
              rwkv_w0, rwkv_w1, rwkv_w2, rwkv_a0, rwkv_a1, rwkv_a2, rwkv_k_k, rwkv_k_a, rwkv_r_k,
              rwkv_ln_w, rwkv_ln_b,
              moe_router, moe_bias, moe_w_gate, moe_w_up, moe_w_down,
              moe_ws_gate, moe_ws_up, moe_ws_down):
    S = x.shape[1]
    Lc = ctx.shape[1]
    rope_mla = axial_rope_tables(S, Lc, MLA_ROPE)
    rope_swa = axial_rope_tables(S, Lc, SWA_HEAD_DIM)
    h_ctx = ctx
    for l in range(DEPTH):
        with_ctx = l < DEPTH - 1
        i = l // 2
        mods = jnp.split((jax.nn.silu(c) @ ada_w[l] + ada_b[l])[:, None, :], N_MODS, axis=-1)
        cmods = jnp.split(jax.nn.silu(c_ctx) @ ada_w[l] + ada_b[l], N_MODS, axis=-1)
        n_lat = modulate(rms_norm(x, norm_mix[l]), mods[0], mods[1])
        n_ctx = modulate(rms_norm(h_ctx, norm_mix[l]), cmods[0], cmods[1])
        if l % 2 == 0:
            o_ctx, o_lat = attn_mixer(n_ctx, n_lat, rope_mla, rope_swa, attn_w_in[i], attn_q_norm[i],
                                      attn_kv_norm[i], attn_w_uq[i], attn_w_ukv[i], attn_sinks[i],
                                      attn_w_o[i], with_ctx)
        else:
            o_ctx, o_lat = rwkv_mixer(n_ctx, n_lat, rwkv_mu[i], rwkv_w_r[i], rwkv_w_k[i], rwkv_w_v[i],
                                      rwkv_w_o[i], rwkv_g1[i], rwkv_g2[i], rwkv_w0[i], rwkv_w1[i],
                                      rwkv_w2[i], rwkv_a0[i], rwkv_a1[i], rwkv_a2[i], rwkv_k_k[i],
                                      rwkv_k_a[i], rwkv_r_k[i], rwkv_ln_w[i], rwkv_ln_b[i], with_ctx)
        x = x + mods[2] * o_lat
        moe_args = (moe_router[l], moe_bias[l], moe_w_gate[l], moe_w_up[l], moe_w_down[l],
                    moe_ws_gate[l], moe_ws_up[l], moe_ws_down[l])
        n_lat = modulate(rms_norm(x, norm_ffn[l]), mods[3], mods[4])
        if with_ctx:
            h_ctx = h_ctx + cmods[2] * o_ctx
            n_ctx = modulate(rms_norm(h_ctx, norm_ffn[l]), cmods[3], cmods[4])
            f = moe_ffn(jnp.concatenate([n_ctx, n_lat], axis=1), *moe_args)
            h_ctx = h_ctx + cmods[5] * f[:, :Lc]
            x = x + mods[5] * f[:, Lc:]
        else:
            x = x + mods[5] * moe_ffn(n_lat, *moe_args)
    return rms_norm(x, norm_final)
```

```python
import functools

import jax
import jax.numpy as jnp
from jax import lax
from jax.experimental import pallas as pl
from jax.experimental.pallas import tpu as pltpu

F32 = jnp.float32
BF16 = jnp.bfloat16
HIGHEST = lax.Precision.HIGHEST

GRID_W = 64
NORM_EPS = 1e-6
ROPE_THETA = 10000.0
NEG_INF = -1e30
N_MODS = 6

MLA_HEADS = 4
MLA_Q_RANK = 384
MLA_KV_RANK = 256
MLA_NOPE = 128
MLA_ROPE = 64
MLA_V = 128

SWA_HEADS = 8
SWA_KV_HEADS = 2
SWA_GROUP = SWA_HEADS // SWA_KV_HEADS
SWA_HEAD_DIM = 64
WINDOW = 128

RWKV_HEAD = 64
DECAY_LORA = 64
ICLR_LORA = 64
GATE_LORA = 128
GN_EPS = 64e-5

N_EXPERTS = 64
TOP_K = 6
N_GROUPS = 8
TOPK_GROUPS = 4
GROUP_SIZE = N_EXPERTS // N_GROUPS
ROUTED_SCALE = 2.5
GATE_W = 128

V7X_LANES = 128
V7X_MXU_DIM = 256
V7X_VMEM_BYTES = 64 * 1024 * 1024

TOKEN_TILE = 256
MLA_Q_TILE = 256
SWA_Q_TILE = 128
SWA_BAND = SWA_Q_TILE + 2 * WINDOW
WKV_CHUNK = 64
WKV_PAIR = 2 * RWKV_HEAD


def _vmem_limit(mib):
    return min(mib * 1024 * 1024, V7X_VMEM_BYTES - 4 * 1024 * 1024)


def _dot(a, b):
    return jnp.dot(a.astype(BF16), b.astype(BF16), preferred_element_type=F32)


def _dot_nt(a, b):
    return lax.dot_general(a.astype(BF16), b.astype(BF16), (((1,), (1,)), ((), ())),
                           preferred_element_type=F32)


def _dot_tn(a, b):
    return lax.dot_general(a.astype(BF16), b.astype(BF16), (((0,), (0,)), ((), ())),
                           preferred_element_type=F32)


def _sigmoid(x):
    return 1.0 / (1.0 + jnp.exp(-x))


def _silu(x):
    return x * _sigmoid(x)


def _rms(x, g):
    return x * lax.rsqrt(jnp.mean(x * x, axis=-1, keepdims=True) + NORM_EPS) * g


def _norm_mod(x, g, shift, scale):
    return _rms(x, g) * (1.0 + scale) + shift


def _split_dot(x, w):
    hi = x.astype(BF16)
    lo = (x - hi.astype(F32)).astype(BF16)
    return (jnp.dot(hi, w, preferred_element_type=F32) + jnp.dot(lo, w, preferred_element_type=F32))


def _head_sum(x, bd):
    w = bd.shape[0]
    parts = [_split_dot(x[:, c * w:(c + 1) * w], bd) for c in range(x.shape[1] // w)]
    return jnp.concatenate(parts, axis=1)


def _ada_kernel(c_ref, w_ref, b_ref, o_ref):
    s = _silu(c_ref[...])
    o_ref[...] = jnp.dot(s, w_ref[...], precision=HIGHEST, preferred_element_type=F32) + b_ref[...]


def _ada_mods(cc, w, b):
    rows, d = cc.shape
    n = w.shape[1]
    return pl.pallas_call(
        _ada_kernel,
        out_shape=jax.ShapeDtypeStruct((rows, n), F32),
        grid=(n // d,),
        in_specs=[pl.BlockSpec((rows, d), lambda i: (0, 0)),
                  pl.BlockSpec((d, d), lambda i: (0, i)),
                  pl.BlockSpec((1, d), lambda i: (0, i))],
        out_specs=pl.BlockSpec((rows, d), lambda i: (0, i)),
        compiler_params=pltpu.CompilerParams(dimension_semantics=("parallel",),
                                             vmem_limit_bytes=_vmem_limit(32)),
        name="ada_mods",
    )(cc, w, b.reshape(1, n))


def _rope128(x, cos, sin, first_half):
    rot = jnp.where(first_half, -pltpu.roll(x, V7X_LANES - 16, axis=1), pltpu.roll(x, 16, axis=1))
    return x * cos + rot * sin


_C_CQ = 0
_C_CKV = _C_CQ + MLA_Q_RANK
_C_QS = _C_CKV + MLA_KV_RANK
_C_KS = _C_QS + SWA_HEADS * SWA_HEAD_DIM
_C_VS = _C_KS + SWA_KV_HEADS * V7X_MXU_DIM
_C_KR = _C_VS + SWA_KV_HEADS * V7X_MXU_DIM
_C_END = _C_KR + V7X_LANES
_SWA_W = SWA_KV_HEADS * V7X_MXU_DIM
_MLA_QK_W = MLA_HEADS * V7X_MXU_DIM


def _attn_proj_kernel(h_ref, mods_ref, g_ref, win_ref, qn_ref, kvn_ref, wuq_ref, wukv_ref, cos_ref, sin_ref,
                      q_ref, k_ref, v_ref, qs_ref, ks_ref, vs_ref):
    m = mods_ref[0, 0]
    n = _norm_mod(h_ref[0], g_ref[...], m[0:1], m[1:2])
    u = _dot(n, win_ref[...])
    cos = cos_ref[...]
    sin = sin_ref[...]
    lane = lax.broadcasted_iota(jnp.int32, (1, V7X_LANES), 1)
    first_half = (lane % 32) < 16

    def rope(x):
        return _rope128(x, cos, sin, first_half)

    scale_a = (MLA_NOPE + MLA_ROPE) ** -0.5
    scale_b = SWA_HEAD_DIM ** -0.5
    q = _dot(_rms(u[:, _C_CQ:_C_CKV], qn_ref[...]), wuq_ref[...])
    kv = _dot(_rms(u[:, _C_CKV:_C_QS], kvn_ref[...]), wukv_ref[...])
    kr = rope(u[:, _C_KR:_C_END]).astype(BF16)
    for h in range(MLA_HEADS):
        o = h * V7X_MXU_DIM
        q_ref[0, :, o:o + V7X_LANES] = (q[:, o:o + V7X_LANES] * scale_a).astype(BF16)
        q_ref[0, :, o + V7X_LANES:o + V7X_MXU_DIM] = (rope(q[:, o + V7X_LANES:o + V7X_MXU_DIM]) * scale_a).astype(BF16)
        k_ref[0, :, o:o + V7X_LANES] = kv[:, h * MLA_NOPE:(h + 1) * MLA_NOPE].astype(BF16)
        k_ref[0, :, o + V7X_LANES:o + V7X_MXU_DIM] = kr
    v_ref[0] = kv[:, MLA_HEADS * MLA_NOPE:].astype(BF16)
    for c in range((_C_KS - _C_QS) // V7X_LANES):
        o = c * V7X_LANES
        qs_ref[0, :, o:o + V7X_LANES] = (rope(u[:, _C_QS + o:_C_QS + o + V7X_LANES]) * scale_b).astype(BF16)
    for c in range(_SWA_W // V7X_LANES):
        o = c * V7X_LANES
        ks_ref[0, :, o:o + V7X_LANES] = rope(u[:, _C_KS + o:_C_KS + o + V7X_LANES]).astype(BF16)
    vs_ref[0] = u[:, _C_VS:_C_KR].astype(BF16)


def _attn_proj(h, mods, g, win, qn, kvn, wuq, wukv, cos, sin, nct):
    b, l, d = h.shape
    tl = TOKEN_TILE
    tok = lambda w: pl.BlockSpec((1, tl, w), lambda i, j: (i, j, 0))
    full = lambda a: pl.BlockSpec(a.shape, lambda i, j: (0,) * a.ndim)
    outs = [(_MLA_QK_W, BF16), (_MLA_QK_W, BF16), (MLA_HEADS * MLA_V, BF16),
            (SWA_HEADS * SWA_HEAD_DIM, BF16), (_SWA_W, BF16), (_SWA_W, BF16)]
    return pl.pallas_call(
        _attn_proj_kernel,
        out_shape=[jax.ShapeDtypeStruct((b, l, w), t) for w, t in outs],
        grid=(b, l // tl),
        in_specs=[tok(d),
                  pl.BlockSpec((1, 1, N_MODS, d), lambda i, j: (i, jnp.where(j < nct, 0, 1), 0, 0)),
                  full(g), full(win), full(qn), full(kvn), full(wuq), full(wukv),
                  pl.BlockSpec((tl, V7X_LANES), lambda i, j: (j, 0)),
                  pl.BlockSpec((tl, V7X_LANES), lambda i, j: (j, 0))],
        out_specs=[tok(w) for w, _ in outs],
        compiler_params=pltpu.CompilerParams(dimension_semantics=("parallel", "parallel"),
                                             vmem_limit_bytes=_vmem_limit(48)),
        name="attn_proj",
    )(h, mods, g, win, qn, kvn, wuq, wukv, cos, sin)


def _mla_kernel(q_ref, k_ref, v_ref, o_ref, *, nct_q, lc):
    j = pl.program_id(2)
    s = _dot_nt(q_ref[0], k_ref[0])
    col = lax.broadcasted_iota(jnp.int32, s.shape, 1)
    limit = jnp.where(j < nct_q, lc, s.shape[1])
    s = jnp.where(col < limit, s, NEG_INF)
    mx = jnp.max(s, axis=-1, keepdims=True)
    p = jnp.exp(s - mx)
    den = jnp.sum(p, axis=-1, keepdims=True)
    o_ref[0] = (_dot(p, v_ref[0]) / den).astype(o_ref.dtype)


def _mla_attention(q, k, v, lc, q_tile0):
    b, l, _ = q.shape
    tq = MLA_Q_TILE
    return pl.pallas_call(
        functools.partial(_mla_kernel, nct_q=lc // tq - q_tile0, lc=lc),
        out_shape=jax.ShapeDtypeStruct((b, l, MLA_HEADS * MLA_V), BF16),
        grid=(b, MLA_HEADS, l // tq - q_tile0),
        in_specs=[pl.BlockSpec((1, tq, V7X_MXU_DIM), lambda i, h, j: (i, j + q_tile0, h)),
                  pl.BlockSpec((1, l, V7X_MXU_DIM), lambda i, h, j: (i, 0, h)),
                  pl.BlockSpec((1, l, MLA_V), lambda i, h, j: (i, 0, h))],
        out_specs=pl.BlockSpec((1, tq, MLA_V), lambda i, h, j: (i, j + q_tile0, h)),
        compiler_params=pltpu.CompilerParams(dimension_semantics=("parallel", "parallel", "parallel"),
                                             vmem_limit_bytes=_vmem_limit(48)),
        name="mla_attention",
    )(q, k, v)


def _swa_kernel(sink_ref, q_ref, k_ref, v_ref, o_ref, *, lc, q_tile0):
    tq = SWA_Q_TILE
    l = k_ref.shape[1]
    r0 = (pl.program_id(1) + q_tile0) * tq
    start = pl.multiple_of(jnp.clip(r0 - WINDOW, lc, l - SWA_BAND), tq)
    rows = SWA_GROUP * tq
    row = lax.broadcasted_iota(jnp.int32, (rows, 1), 0)
    qpos = jnp.where(r0 >= lc, r0, -l) + row % tq
    kpos = start + lax.broadcasted_iota(jnp.int32, (1, SWA_BAND), 1)
    valid = jnp.abs(qpos - kpos) <= WINDOW
    lane = lax.broadcasted_iota(jnp.int32, (1, V7X_MXU_DIM), 1)
    for g in range(SWA_KV_HEADS):
        sl = slice(g * V7X_MXU_DIM, (g + 1) * V7X_MXU_DIM)
        qg = q_ref[0, :, sl]
        zero = jnp.zeros_like(qg)
        head = [(lane // SWA_HEAD_DIM) == hh for hh in range(SWA_GROUP)]
        qstack = jnp.concatenate([jnp.where(head[hh], qg, zero) for hh in range(SWA_GROUP)], axis=0)
        sc = _dot_nt(qstack, k_ref[0, 0:lc, sl])
        sb = jnp.where(valid, _dot_nt(qstack, k_ref[0, pl.ds(start, SWA_BAND), sl]), NEG_INF)
        sk = jnp.zeros((rows, 1), F32)
        for hh in range(SWA_GROUP):
            sk = jnp.where(row // tq == hh, sink_ref[g * SWA_GROUP + hh], sk)
        mx = jnp.maximum(jnp.maximum(jnp.max(sc, axis=-1, keepdims=True), jnp.max(sb, axis=-1, keepdims=True)), sk)
        pc = jnp.exp(sc - mx)
        pb = jnp.exp(sb - mx)
        den = jnp.sum(pc, axis=-1, keepdims=True) + jnp.sum(pb, axis=-1, keepdims=True) + jnp.exp(sk - mx)
        ostack = (_dot(pc, v_ref[0, 0:lc, sl]) + _dot(pb, v_ref[0, pl.ds(start, SWA_BAND), sl])) / den
        o = jnp.zeros((tq, V7X_MXU_DIM), F32)
        for hh in range(SWA_GROUP):
            o = o + jnp.where(head[hh], ostack[hh * tq:(hh + 1) * tq], 0.0)
        o_ref[0, :, sl] = o.astype(o_ref.dtype)


def _swa_attention(sinks, q, k, v, lc, q_tile0):
    b, l, _ = q.shape
    tq = SWA_Q_TILE
    return pl.pallas_call(
        functools.partial(_swa_kernel, lc=lc, q_tile0=q_tile0),
        out_shape=jax.ShapeDtypeStruct((b, l, SWA_HEADS * SWA_HEAD_DIM), BF16),
        grid=(b, l // tq - q_tile0),
        in_specs=[pl.BlockSpec(memory_space=pltpu.SMEM),
                  pl.BlockSpec((1, tq, SWA_HEADS * SWA_HEAD_DIM), lambda i, j: (i, j + q_tile0, 0)),
                  pl.BlockSpec((1, l, _SWA_W), lambda i, j: (i, 0, 0)),
                  pl.BlockSpec((1, l, _SWA_W), lambda i, j: (i, 0, 0))],
        out_specs=pl.BlockSpec((1, tq, SWA_HEADS * SWA_HEAD_DIM), lambda i, j: (i, j + q_tile0, 0)),
        compiler_params=pltpu.CompilerParams(dimension_semantics=("parallel", "parallel"),
                                             vmem_limit_bytes=_vmem_limit(48)),
        name="swa_attention",
    )(sinks, q, k, v)


def _route(n2, wrt, bias):
    logits = lax.dot_general(wrt, n2, (((1,), (1,)), ((), ())), precision=HIGHEST, preferred_element_type=F32)
    rows = logits.shape[1]
    shape3 = (N_GROUPS, GROUP_SIZE, rows)
    scores = _sigmoid(logits[0:N_EXPERTS]).reshape(shape3)
    choice = scores + bias
    ji = lax.broadcasted_iota(jnp.int32, shape3, 1).astype(F32)
    gi = lax.broadcasted_iota(jnp.int32, shape3, 0).astype(F32)
    ei = gi * GROUP_SIZE + ji

    def max2(a):
        return jnp.max(jnp.max(a, axis=1, keepdims=True), axis=0, keepdims=True)

    def min2(a):
        return jnp.min(jnp.min(a, axis=1, keepdims=True), axis=0, keepdims=True)

    m1 = jnp.max(choice, axis=1, keepdims=True)
    first = jnp.min(jnp.where(choice == m1, ji, float(GROUP_SIZE)), axis=1, keepdims=True)
    m2 = jnp.max(jnp.where(ji == first, -jnp.inf, choice), axis=1, keepdims=True)
    gs = m1 + m2
    gidx = lax.broadcasted_iota(jnp.int32, gs.shape, 0).astype(F32)
    gsel = jnp.zeros_like(gs)
    for _ in range(TOPK_GROUPS):
        mx = jnp.max(gs, axis=0, keepdims=True)
        pick = gidx == jnp.min(jnp.where(gs == mx, gidx, float(N_GROUPS)), axis=0, keepdims=True)
        gsel = jnp.where(pick, 1.0, gsel)
        gs = jnp.where(pick, -jnp.inf, gs)
    cand = jnp.where(gsel > 0.0, choice, -jnp.inf)
    esel = jnp.zeros_like(cand)
    for _ in range(TOP_K):
        mx = max2(cand)
        pick = ei == min2(jnp.where(cand == mx, ei, float(N_EXPERTS)))
        esel = jnp.where(pick, 1.0, esel)
        cand = jnp.where(pick, -jnp.inf, cand)
    w = scores * esel
    w = w * (ROUTED_SCALE / jnp.sum(jnp.sum(w, axis=1, keepdims=True), axis=0, keepdims=True))
    w = w.reshape(N_EXPERTS, rows)
    return jnp.concatenate([w, jnp.zeros((logits.shape[0] - N_EXPERTS, rows), F32)], axis=0)


def _mixer_tail(o, h_ref, m, gffn_ref, wrt_ref, bias_ref, hn_ref, n2_ref, gates_ref):
    hn = h_ref[0] + m[2:3] * o
    hn_ref[0] = hn
    n2 = _norm_mod(hn, gffn_ref[...], m[3:4], m[4:5])
    n2_ref[0] = n2.astype(BF16)
    gates_ref[0] = _route(n2, wrt_ref[...], bias_ref[...]).T


def _attn_out_kernel(a_ref, b_ref, h_ref, mods_ref, wo_ref, gffn_ref, wrt_ref, bias_ref, hn_ref, n2_ref, gates_ref):
    wa = MLA_HEADS * MLA_V
    o = _dot(a_ref[0], wo_ref[0:wa, :]) + _dot(b_ref[0], wo_ref[wa:, :])
    _mixer_tail(o, h_ref, mods_ref[0, 0], gffn_ref, wrt_ref, bias_ref, hn_ref, n2_ref, gates_ref)


def _tail_out_shapes(b, l, d):
    return [jax.ShapeDtypeStruct((b, l, d), F32), jax.ShapeDtypeStruct((b, l, d), BF16),
            jax.ShapeDtypeStruct((b, l, GATE_W), F32)]


def _attn_out(a, bm, h, mods, wo, gffn, wrt, bias, nct):
    b, l, d = h.shape
    tl = TOKEN_TILE
    tok = lambda w: pl.BlockSpec((1, tl, w), lambda i, j: (i, j, 0))
    full = lambda x: pl.BlockSpec(x.shape, lambda i, j: (0,) * x.ndim)
    return pl.pallas_call(
        _attn_out_kernel,
        out_shape=_tail_out_shapes(b, l, d),
        grid=(b, l // tl),
        in_specs=[tok(a.shape[2]), tok(bm.shape[2]), tok(d),
                  pl.BlockSpec((1, 1, N_MODS, d), lambda i, j: (i, jnp.where(j < nct, 0, 1), 0, 0)),
                  full(wo), full(gffn), full(wrt), full(bias)],
        out_specs=[tok(d), tok(d), tok(GATE_W)],
        compiler_params=pltpu.CompilerParams(dimension_semantics=("parallel", "parallel"),
                                             vmem_limit_bytes=_vmem_limit(40)),
        name="attn_out",
    )(a, bm, h, mods, wo, gffn, wrt, bias)


def _moe_kernel(x_ref, g_ref, wg_ref, wu_ref, wd_ref, sg_ref, su_ref, sd_ref, h_ref, mods_ref, gfin_ref,
                o_ref, acc_ref, *, tiles_per_sample, lc, final_norm):
    e = pl.program_id(1)
    x = x_ref[...]

    @pl.when(e == 0)
    def _():
        hs = _silu(_dot(x, sg_ref[...])) * _dot(x, su_ref[...])
        acc_ref[...] = _dot(hs, sd_ref[...])

    lane = lax.broadcasted_iota(jnp.int32, (1, GATE_W), 1)
    gcol = jnp.sum(jnp.where(lane == e, g_ref[...], 0.0), axis=1, keepdims=True)
    act = _silu(_dot(x, wg_ref[0])) * _dot(x, wu_ref[0]) * gcol
    acc_ref[...] += _dot(act, wd_ref[0])

    @pl.when(e == pl.num_programs(1) - 1)
    def _():
        tm = x.shape[0]
        pos = (pl.program_id(0) % tiles_per_sample) * tm + lax.broadcasted_iota(jnp.int32, (tm, 1), 0)
        gate = jnp.where(pos < lc, mods_ref[0, 0, N_MODS - 1:N_MODS, :], mods_ref[0, 1, N_MODS - 1:N_MODS, :])
        y = h_ref[...] + gate * acc_ref[...]
        if final_norm:
            y = _rms(y, gfin_ref[...])
        o_ref[...] = y


def _moe(x, gates, wg, wu, wd, sg, su, sd, h, mods, gfin, lc, tm, final_norm):
    t, d = x.shape
    b = mods.shape[0]
    tps = (t // b) // tm
    ne, _, f = wg.shape
    full = lambda a: pl.BlockSpec(a.shape, lambda i, e: (0,) * a.ndim)
    return pl.pallas_call(
        functools.partial(_moe_kernel, tiles_per_sample=tps, lc=lc, final_norm=final_norm),
        out_shape=jax.ShapeDtypeStruct((t, d), F32),
        grid=(t // tm, ne),
        in_specs=[pl.BlockSpec((tm, d), lambda i, e: (i, 0)),
                  pl.BlockSpec((tm, GATE_W), lambda i, e: (i, 0)),
                  pl.BlockSpec((1, d, f), lambda i, e: (e, 0, 0)),
                  pl.BlockSpec((1, d, f), lambda i, e: (e, 0, 0)),
                  pl.BlockSpec((1, f, d), lambda i, e: (e, 0, 0)),
                  full(sg), full(su), full(sd),
                  pl.BlockSpec((tm, d), lambda i, e: (i, 0)),
                  pl.BlockSpec((1, 2, N_MODS, d), lambda i, e: (i // tps, 0, 0, 0)),
                  full(gfin)],
        out_specs=pl.BlockSpec((tm, d), lambda i, e: (i, 0)),
        scratch_shapes=[pltpu.VMEM((tm, d), F32)],
        compiler_params=pltpu.CompilerParams(dimension_semantics=("parallel", "arbitrary"),
                                             vmem_limit_bytes=_vmem_limit(56)),
        name="moe_ffn",
    )(x, gates, wg, wu, wd, sg, su, sd, h, mods, gfin)


def _rwkv_proj_kernel(h_ref, hp_ref, hx_ref, mods_ref, g_ref, mu_ref, wr_ref, wk_ref, wv_ref, g1_ref, g2_ref,
                      w1_ref, w2_ref, a1_ref, a2_ref, w0_ref, a0_ref, kk_ref, ka_ref, rk_ref, bd_ref,
                      r_out, v_out, kk_out, g_out, km_out, b_out, lw_out, bonus_out, *, nct):
    j = pl.program_id(1)
    nt = pl.num_programs(1)
    m = mods_ref[0, 0]
    g = g_ref[...]
    n = _norm_mod(h_ref[0], g, m[0:1], m[1:2])
    tl, d = n.shape
    seg_first = (j == 0) | (j == nct)
    seg_last = (j == nct - 1) | (j == nt - 1)
    n_prev = _norm_mod(hp_ref[0], g, m[0:1], m[1:2])[7:8] * jnp.where(seg_first, 0.0, 1.0)
    n_next = _norm_mod(hx_ref[0], g, m[0:1], m[1:2])[0:1] * jnp.where(seg_last, 0.0, 1.0)
    row = lax.broadcasted_iota(jnp.int32, (tl, 1), 0)
    prev = jnp.where(row == 0, n_prev, pltpu.roll(n, 1, axis=0))
    nxt = jnp.where(row == tl - 1, n_next, pltpu.roll(n, tl - 1, axis=0))
    lane = lax.broadcasted_iota(jnp.int32, (1, d), 1)
    xx = jnp.where(lane < d // 2, prev, nxt) - n
    mu = mu_ref[...]
    xr, xw, xk, xv, xa, xg = [n + xx * mu[i:i + 1] for i in range(6)]
    r = _dot(xr, wr_ref[...])
    k = _dot(xk, wk_ref[...])
    v = _dot(xv, wv_ref[...])
    g_out[0] = _dot(_sigmoid(_dot(xg, g1_ref[...])), g2_ref[...]).astype(g_out.dtype)
    tw = jnp.tanh(_dot(xw, w1_ref[...]))
    ta = _dot(xa, a1_ref[...])
    bd = bd_ref[...]
    kk = k * kk_ref[...]
    kk = kk / jnp.maximum(jnp.sqrt(_head_sum(kk * kk, bd)), 1e-12)
    r_out[0] = r.astype(r_out.dtype)
    v_out[0] = v.astype(v_out.dtype)
    kk_out[0] = kk.astype(kk_out.dtype)
    bonus = jnp.zeros_like(v)
    for dr in range(2):
        zw = w0_ref[dr:dr + 1, :] + _dot(tw, w2_ref[dr])
        lw_out[dr, 0] = -jnp.exp(-0.5) * _sigmoid(zw)
        a = _sigmoid(a0_ref[dr:dr + 1, :] + _dot(ta, a2_ref[dr]))
        km = k * (1.0 + (a - 1.0) * ka_ref[...])
        km_out[dr, 0] = km.astype(km_out.dtype)
        b_out[dr, 0] = (kk * a).astype(b_out.dtype)
        bonus = bonus + _head_sum(r * km * rk_ref[...], bd) * v
    bonus_out[0] = bonus


def _rwkv_proj(h, mods, g, mu, wr, wk, wv, g1, g2, w1, w2, a1, a2, w0, a0, kk, ka, rk, bd, nct):
    b, l, d = h.shape
    tl = TOKEN_TILE
    nb8 = l // 8
    tok = pl.BlockSpec((1, tl, d), lambda i, j: (i, j, 0))
    tok2 = pl.BlockSpec((2, 1, tl, d), lambda i, j: (0, i, j, 0))
    full = lambda x: pl.BlockSpec(x.shape, lambda i, j: (0,) * x.ndim)
    sds = jax.ShapeDtypeStruct
    return pl.pallas_call(
        functools.partial(_rwkv_proj_kernel, nct=nct),
        out_shape=[sds((b, l, d), BF16), sds((b, l, d), BF16), sds((b, l, d), BF16), sds((b, l, d), BF16),
                   sds((2, b, l, d), BF16), sds((2, b, l, d), BF16), sds((2, b, l, d), F32), sds((b, l, d), F32)],
        grid=(b, l // tl),
        in_specs=[tok,
                  pl.BlockSpec((1, 8, d), lambda i, j: (i, jnp.maximum(j * (tl // 8) - 1, 0), 0)),
                  pl.BlockSpec((1, 8, d), lambda i, j: (i, jnp.minimum((j + 1) * (tl // 8), nb8 - 1), 0)),
                  pl.BlockSpec((1, 1, N_MODS, d), lambda i, j: (i, jnp.where(j < nct, 0, 1), 0, 0)),
                  full(g), full(mu), full(wr), full(wk), full(wv), full(g1), full(g2), full(w1), full(w2),
                  full(a1), full(a2), full(w0), full(a0), full(kk), full(ka), full(rk), full(bd)],
        out_specs=[tok, tok, tok, tok, tok2, tok2, tok2, tok],
        compiler_params=pltpu.CompilerParams(dimension_semantics=("parallel", "parallel"),
                                             vmem_limit_bytes=_vmem_limit(56)),
        name="rwkv_proj",
    )(h, h, h, mods, g, mu, wr, wk, wv, g1, g2, w1, w2, a1, a2, w0, a0, kk, ka, rk, bd)


def _wkv_kernel(r_ref, v_ref, kk_ref, km_ref, b_ref, lw_ref, y_ref, st_ref):
    c = WKV_CHUNK
    w = WKV_PAIR
    rev = pl.program_id(0)
    sign = 1 - 2 * rev

    @pl.when(pl.program_id(2) == 0)
    def _():
        st_ref[...] = jnp.zeros_like(st_ref)

    ti = lax.broadcasted_iota(jnp.int32, (c, c), 0)
    si = lax.broadcasted_iota(jnp.int32, (c, c), 1)
    tri = jnp.where((si - ti) * sign <= 0, 1.0, 0.0).astype(F32)
    lw = lw_ref[0, 0]
    l_incl = jnp.dot(tri, lw, precision=HIGHEST, preferred_element_type=F32)
    l_tot = jnp.sum(lw, axis=0, keepdims=True)
    mid = 0.5 * l_tot
    e_pos = jnp.exp(l_incl - mid)
    e_neg = jnp.exp(mid - l_incl)
    e_mid = jnp.exp(mid)
    rt = r_ref[0].astype(F32) * e_pos
    kt = kk_ref[0].astype(F32) * jnp.exp(l_incl - lw - mid)
    kh = km_ref[0, 0].astype(F32) * e_neg
    bh = b_ref[0, 0].astype(F32) * e_neg
    v = v_ref[0]

    ri = lax.broadcasted_iota(jnp.int32, (w, w), 0)
    ci = lax.broadcasted_iota(jnp.int32, (w, w), 1)
    same = (ri // c) == (ci // c)
    dlt = (ci % c - ri % c) * sign
    strict = same & (dlt < 0)
    incl = same & (dlt <= 0)
    eye = jnp.where(ri == ci, 1.0, 0.0).astype(F32)
    lane = lax.broadcasted_iota(jnp.int32, (1, w), 1)
    h0 = lane < RWKV_HEAD

    def rows2(x):
        return jnp.concatenate([jnp.where(h0, x, 0.0), jnp.where(h0, 0.0, x)], axis=0)

    def fold(x):
        return x[:c] + x[c:]

    for p in range(st_ref.shape[0]):
        sl = slice(p * w, (p + 1) * w)
        rt_p, kt_p, kh_p, bh_p = rt[:, sl], kt[:, sl], kh[:, sl], bh[:, sl]
        em = e_mid[:, sl]
        x = jnp.concatenate([rows2(kt_p), rows2(rt_p)], axis=0)
        gk = _dot_nt(x, jnp.concatenate([kh_p, kh_p], axis=0))
        gb = _dot_nt(x, jnp.concatenate([bh_p, bh_p], axis=0))
        a_kk = jnp.where(strict, gk[:w], 0.0)
        nmat = jnp.where(strict, gb[:w], 0.0)
        a_rk = jnp.where(incl, gk[w:], 0.0)
        a_rb = jnp.where(incl, gb[w:], 0.0)
        tinv = eye - nmat
        npow = nmat
        for _ in range(c.bit_length() - 2):
            npow = _dot(npow, npow)
            tinv = tinv + _dot(tinv, npow)
        v_rows = rows2(v[:, sl].astype(F32))
        st = st_ref[p]
        rhs = rows2(_dot_nt(kt_p * em, st)) + _dot(a_kk, v_rows)
        u_rows = _dot(tinv, rhs)
        y = _dot_nt(rt_p * em, st) + fold(_dot(a_rk, v_rows) - _dot(a_rb, u_rows))
        y_ref[0, 0, :, sl] = y
        upd = _dot_tn(jnp.concatenate([v[:, sl].astype(F32), -fold(u_rows)], axis=0),
                      jnp.concatenate([kh_p * em, bh_p * em], axis=0))
        st_ref[p] = st * (em * em) + jnp.where(same, upd, 0.0)


def _wkv(r, v, kk, km, bv, lw, lc):
    b, l, d = r.shape
    c = WKV_CHUNK
    ncc = lc // c
    nlc = (l - lc) // c

    def chunk(dr, i):
        return jnp.where(dr == 0, i, jnp.where(i < ncc, ncc - 1 - i, nlc + 2 * ncc - 1 - i))

    shared = pl.BlockSpec((1, c, d), lambda dr, bi, i: (bi, chunk(dr, i), 0))
    per_dir = pl.BlockSpec((1, 1, c, d), lambda dr, bi, i: (dr, bi, chunk(dr, i), 0))
    return pl.pallas_call(
        _wkv_kernel,
        out_shape=jax.ShapeDtypeStruct((2, b, l, d), F32),
        grid=(2, b, l // c),
        in_specs=[shared, shared, shared, per_dir, per_dir, per_dir],
        out_specs=per_dir,
        scratch_shapes=[pltpu.VMEM((d // WKV_PAIR, WKV_PAIR, WKV_PAIR), F32)],
        compiler_params=pltpu.CompilerParams(dimension_semantics=("parallel", "parallel", "arbitrary"),
                                             vmem_limit_bytes=_vmem_limit(32)),
        name="wkv7_chunked",
    )(r, v, kk, km, bv, lw)


def _rwkv_out_kernel(y_ref, bonus_ref, g_ref, lnw_ref, lnb_ref, wo_ref, bd_ref, h_ref, mods_ref, gffn_ref,
                     wrt_ref, bias_ref, hn_ref, n2_ref, gates_ref):
    y = y_ref[0, 0] + y_ref[1, 0]
    bd = bd_ref[...]
    mean = _head_sum(y, bd) * (1.0 / RWKV_HEAD)
    yc = y - mean
    var = _head_sum(yc * yc, bd) * (1.0 / RWKV_HEAD)
    yn = yc * lax.rsqrt(var + GN_EPS) * lnw_ref[...] + lnb_ref[...]
    out = (yn + bonus_ref[0]) * g_ref[0].astype(F32)
    _mixer_tail(_dot(out, wo_ref[...]), h_ref, mods_ref[0, 0], gffn_ref, wrt_ref, bias_ref, hn_ref, n2_ref, gates_ref)


def _rwkv_out(y, bonus, g, lnw, lnb, wo, bd, h, mods, gffn, wrt, bias, nct):
    b, l, d = h.shape
    tl = TOKEN_TILE
    tok = lambda w: pl.BlockSpec((1, tl, w), lambda i, j: (i, j, 0))
    full = lambda x: pl.BlockSpec(x.shape, lambda i, j: (0,) * x.ndim)
    return pl.pallas_call(
        _rwkv_out_kernel,
        out_shape=_tail_out_shapes(b, l, d),
        grid=(b, l // tl),
        in_specs=[pl.BlockSpec((2, 1, tl, d), lambda i, j: (0, i, j, 0)), tok(d), tok(d),
                  full(lnw), full(lnb), full(wo), full(bd), tok(d),
                  pl.BlockSpec((1, 1, N_MODS, d), lambda i, j: (i, jnp.where(j < nct, 0, 1), 0, 0)),
                  full(gffn), full(wrt), full(bias)],
        out_specs=[tok(d), tok(d), tok(GATE_W)],
        compiler_params=pltpu.CompilerParams(dimension_semantics=("parallel", "parallel"),
                                             vmem_limit_bytes=_vmem_limit(40)),
        name="rwkv_out",
    )(y, bonus, g, lnw, lnb, wo, bd, h, mods, gffn, wrt, bias)


def _rope_table(n_lat, n_ctx):
    dim = SWA_HEAD_DIM
    nf = dim // 4
    inv = ROPE_THETA ** (-jnp.arange(nf, dtype=F32) / nf)
    row = jnp.repeat(jnp.arange(n_lat // GRID_W, dtype=F32), GRID_W)
    col = jnp.tile(jnp.arange(GRID_W, dtype=F32), n_lat // GRID_W)
    ar = row[:, None] * inv
    ac = col[:, None] * inv
    ang = jnp.concatenate([ar, ar, ac, ac], axis=-1)
    cos = jnp.concatenate([jnp.ones((n_ctx, dim), F32), jnp.cos(ang)], axis=0)
    sin = jnp.concatenate([jnp.zeros((n_ctx, dim), F32), jnp.sin(ang)], axis=0)
    return jnp.tile(cos, (1, 2)), jnp.tile(sin, (1, 2))


def _layout_attn_weights(w_in, w_uq, w_ukv):
    d = w_in.shape[0]
    s0 = MLA_Q_RANK
    s1 = s0 + MLA_KV_RANK
    s2 = s1 + MLA_ROPE
    s3 = s2 + SWA_HEADS * SWA_HEAD_DIM
    s4 = s3 + SWA_KV_HEADS * SWA_HEAD_DIM
    rep = lambda w: jnp.concatenate(
        [jnp.tile(w[:, g * SWA_HEAD_DIM:(g + 1) * SWA_HEAD_DIM], (1, SWA_GROUP)) for g in range(SWA_KV_HEADS)], axis=1)
    win = jnp.concatenate([w_in[:, :s1], w_in[:, s2:s3], rep(w_in[:, s3:s4]), rep(w_in[:, s4:]),
                           w_in[:, s1:s2], jnp.zeros((d, V7X_LANES - MLA_ROPE), w_in.dtype)], axis=1)
    qh = MLA_NOPE + MLA_ROPE
    pad = jnp.zeros((w_uq.shape[0], V7X_MXU_DIM - qh), w_uq.dtype)
    wuq = jnp.concatenate([jnp.concatenate([w_uq[:, h * qh:(h + 1) * qh], pad], axis=1) for h in range(MLA_HEADS)], axis=1)
    kvh = MLA_NOPE + MLA_V
    wukv = jnp.concatenate([w_ukv[:, h * kvh:h * kvh + MLA_NOPE] for h in range(MLA_HEADS)]
                           + [w_ukv[:, h * kvh + MLA_NOPE:(h + 1) * kvh] for h in range(MLA_HEADS)], axis=1)
    return win.astype(BF16), wuq.astype(BF16), wukv.astype(BF16)


def _lora_pair(w_down, w_up):
    rank = w_down.shape[2]
    down = jnp.concatenate([w_down[0], w_down[1]], axis=1)
    z = jnp.zeros((rank, w_up.shape[2]), w_up.dtype)
    up = jnp.stack([jnp.concatenate([w_up[0], z], axis=0), jnp.concatenate([z, w_up[1]], axis=0)], axis=0)
    return down.astype(BF16), up.astype(BF16)


def _head_block_diag():
    i = jnp.arange(V7X_MXU_DIM) // RWKV_HEAD
    return (i[:, None] == i[None, :]).astype(BF16)


def kernel(x, c, ctx, c_ctx, ada_w, ada_b, norm_mix, norm_ffn, norm_final, attn_w_in, attn_q_norm, attn_kv_norm, attn_w_uq, attn_w_ukv, attn_sinks, attn_w_o, rwkv_mu, rwkv_w_r, rwkv_w_k, rwkv_w_v, rwkv_w_o, rwkv_g1, rwkv_g2, rwkv_w0, rwkv_w1, rwkv_w2, rwkv_a0, rwkv_a1, rwkv_a2, rwkv_k_k, rwkv_k_a, rwkv_r_k, rwkv_ln_w, rwkv_ln_b, moe_router, moe_bias, moe_w_gate, moe_w_up, moe_w_down, moe_ws_gate, moe_ws_up, moe_ws_down):
    bsz, s, d = x.shape
    lc = ctx.shape[1]
    l = lc + s
    depth = ada_w.shape[0]
    nct = lc // TOKEN_TILE
    assert lc % TOKEN_TILE == 0 and s % TOKEN_TILE == 0 and s >= SWA_BAND and lc % WKV_CHUNK == 0
    assert d % V7X_MXU_DIM == 0 and WKV_CHUNK * 2 == V7X_LANES
    moe_tile = l // 4
    assert l % 4 == 0 and moe_tile % 16 == 0

    h = jnp.concatenate([ctx, x], axis=1)
    cos, sin = _rope_table(s, lc)
    bd = _head_block_diag()
    rows = -(-(bsz + 1) // 8) * 8
    cc = jnp.concatenate([c, c_ctx[None, :], jnp.zeros((rows - bsz - 1, d), F32)], axis=0)
    row2 = lambda a: a.reshape(1, -1)

    for li in range(depth):
        with_ctx = li < depth - 1
        i = li // 2
        ada = _ada_mods(cc, ada_w[li], ada_b[li])
        mods = jnp.stack([jnp.broadcast_to(ada[bsz].reshape(1, N_MODS, d), (bsz, N_MODS, d)),
                          ada[:bsz].reshape(bsz, N_MODS, d)], axis=1)
        wrt = jnp.concatenate([moe_router[li].T, jnp.zeros((GATE_W - N_EXPERTS, d), F32)], axis=0)
        bias = moe_bias[li].reshape(N_GROUPS, GROUP_SIZE, 1)
        if li % 2 == 0:
            win, wuq, wukv = _layout_attn_weights(attn_w_in[i], attn_w_uq[i], attn_w_ukv[i])
            q, k, v, qs, ks, vs = _attn_proj(h, mods, row2(norm_mix[li]), win, row2(attn_q_norm[i]),
                                             row2(attn_kv_norm[i]), wuq, wukv, cos, sin, nct)
            a = _mla_attention(q, k, v, lc, 0 if with_ctx else lc // MLA_Q_TILE)
            bm = _swa_attention(attn_sinks[i], qs, ks, vs, lc, 0 if with_ctx else lc // SWA_Q_TILE)
            h, n2, gates = _attn_out(a, bm, h, mods, attn_w_o[i].astype(BF16), row2(norm_ffn[li]), wrt, bias, nct)
        else:
            w1, w2 = _lora_pair(rwkv_w1[i], rwkv_w2[i])
            a1, a2 = _lora_pair(rwkv_a1[i], rwkv_a2[i])
            r, v, kk, g, km, bv, lw, bonus = _rwkv_proj(
                h, mods, row2(norm_mix[li]), rwkv_mu[i], rwkv_w_r[i].astype(BF16), rwkv_w_k[i].astype(BF16),
                rwkv_w_v[i].astype(BF16), rwkv_g1[i].astype(BF16), rwkv_g2[i].astype(BF16), w1, w2, a1, a2,
                rwkv_w0[i], rwkv_a0[i], row2(rwkv_k_k[i]), row2(rwkv_k_a[i]), row2(rwkv_r_k[i]), bd, nct)
            y = _wkv(r, v, kk, km, bv, lw, lc)
            h, n2, gates = _rwkv_out(y, bonus, g, row2(rwkv_ln_w[i]), row2(rwkv_ln_b[i]), rwkv_w_o[i].astype(BF16),
                                     bd, h, mods, row2(norm_ffn[li]), wrt, bias, nct)
        h = _moe(n2.reshape(bsz * l, d), gates.reshape(bsz * l, GATE_W), moe_w_gate[li].astype(BF16),
                 moe_w_up[li].astype(BF16), moe_w_down[li].astype(BF16), moe_ws_gate[li].astype(BF16),
                 moe_ws_up[li].astype(BF16), moe_ws_down[li].astype(BF16), h.reshape(bsz * l, d), mods,
                 row2(norm_final), lc, moe_tile, li == depth - 1).reshape(bsz, l, d)
    return h[:, lc:]
```

```python
import functools

import jax
import jax.numpy as jnp
from jax import lax
from jax.experimental import pallas as pl
from jax.experimental.pallas import tpu as pltpu

F32 = jnp.float32
BF16 = jnp.bfloat16
HIGHEST = lax.Precision.HIGHEST

GRID_W = 64
NORM_EPS = 1e-6
ROPE_THETA = 10000.0
NEG_INF = -1e30
N_MODS = 6

MLA_HEADS = 4
MLA_Q_RANK = 384
MLA_KV_RANK = 256
MLA_NOPE = 128
MLA_ROPE = 64
MLA_V = 128

SWA_HEADS = 8
SWA_KV_HEADS = 2
SWA_GROUP = SWA_HEADS // SWA_KV_HEADS
SWA_HEAD_DIM = 64
WINDOW = 128

RWKV_HEAD = 64
DECAY_LORA = 64
ICLR_LORA = 64
GATE_LORA = 128
GN_EPS = 64e-5

N_EXPERTS = 64
TOP_K = 6
N_GROUPS = 8
TOPK_GROUPS = 4
GROUP_SIZE = N_EXPERTS // N_GROUPS
ROUTED_SCALE = 2.5
GATE_W = 128

V7X_LANES = 128
V7X_MXU_DIM = 256
V7X_VMEM_BYTES = 64 * 1024 * 1024

TOKEN_TILE = 256
MLA_Q_TILE = 256
SWA_Q_TILE = 128
SWA_BAND = SWA_Q_TILE + 2 * WINDOW
WKV_CHUNK = 64
WKV_PAIR = 2 * RWKV_HEAD


def _vmem_limit(mib):
    return min(mib * 1024 * 1024, V7X_VMEM_BYTES - 4 * 1024 * 1024)


def _dot(a, b):
    return jnp.dot(a.astype(BF16), b.astype(BF16), preferred_element_type=F32)


def _dot_nt(a, b):
    return lax.dot_general(a.astype(BF16), b.astype(BF16), (((1,), (1,)), ((), ())),
                           preferred_element_type=F32)


def _dot_tn(a, b):
    return lax.dot_general(a.astype(BF16), b.astype(BF16), (((0,), (0,)), ((), ())),
                           preferred_element_type=F32)


def _sigmoid(x):
    return 1.0 / (1.0 + jnp.exp(-x))


def _silu(x):
    return x * _sigmoid(x)


def _rms(x, g):
    return x * lax.rsqrt(jnp.mean(x * x, axis=-1, keepdims=True) + NORM_EPS) * g


def _norm_mod(x, g, shift, scale):
    return _rms(x, g) * (1.0 + scale) + shift


def _split_dot(x, w):
    hi = x.astype(BF16)
    lo = (x - hi.astype(F32)).astype(BF16)
    return (jnp.dot(hi, w, preferred_element_type=F32) + jnp.dot(lo, w, preferred_element_type=F32))


def _head_sum(x, bd):
    w = bd.shape[0]
    parts = [_split_dot(x[:, c * w:(c + 1) * w], bd) for c in range(x.shape[1] // w)]
    return jnp.concatenate(parts, axis=1)


def _ada_kernel(c_ref, w_ref, b_ref, o_ref):
    s = _silu(c_ref[...])
    o_ref[...] = jnp.dot(s, w_ref[...], precision=HIGHEST, preferred_element_type=F32) + b_ref[...]


def _ada_mods(cc, w, b):
    rows, d = cc.shape
    n = w.shape[1]
    return pl.pallas_call(
        _ada_kernel,
        out_shape=jax.ShapeDtypeStruct((rows, n), F32),
        grid=(n // d,),
        in_specs=[pl.BlockSpec((rows, d), lambda i: (0, 0)),
                  pl.BlockSpec((d, d), lambda i: (0, i)),
                  pl.BlockSpec((1, d), lambda i: (0, i))],
        out_specs=pl.BlockSpec((rows, d), lambda i: (0, i)),
        compiler_params=pltpu.CompilerParams(dimension_semantics=("parallel",),
                                             vmem_limit_bytes=_vmem_limit(32)),
        name="ada_mods",
    )(cc, w, b.reshape(1, n))


def _rope128(x, cos, sin, first_half):
    rot = jnp.where(first_half, -pltpu.roll(x, V7X_LANES - 16, axis=1), pltpu.roll(x, 16, axis=1))
    return x * cos + rot * sin


_C_CQ = 0
_C_CKV = _C_CQ + MLA_Q_RANK
_C_QS = _C_CKV + MLA_KV_RANK
_C_KS = _C_QS + SWA_HEADS * SWA_HEAD_DIM
_C_VS = _C_KS + SWA_KV_HEADS * V7X_MXU_DIM
_C_KR = _C_VS + SWA_KV_HEADS * V7X_MXU_DIM
_C_END = _C_KR + V7X_LANES
_SWA_W = SWA_KV_HEADS * V7X_MXU_DIM
_MLA_QK_W = MLA_HEADS * V7X_MXU_DIM


def _attn_proj_kernel(h_ref, mods_ref, g_ref, win_ref, qn_ref, kvn_ref, wuq_ref, wukv_ref, cos_ref, sin_ref,
                      q_ref, k_ref, v_ref, qs_ref, ks_ref, vs_ref):
    m = mods_ref[0, 0]
    n = _norm_mod(h_ref[0], g_ref[...], m[0:1], m[1:2])
    u = _dot(n, win_ref[...])
    cos = cos_ref[...]
    sin = sin_ref[...]
    lane = lax.broadcasted_iota(jnp.int32, (1, V7X_LANES), 1)
    first_half = (lane % 32) < 16

    def rope(x):
        return _rope128(x, cos, sin, first_half)

    scale_a = (MLA_NOPE + MLA_ROPE) ** -0.5
    scale_b = SWA_HEAD_DIM ** -0.5
    q = _dot(_rms(u[:, _C_CQ:_C_CKV], qn_ref[...]), wuq_ref[...])
    kv = _dot(_rms(u[:, _C_CKV:_C_QS], kvn_ref[...]), wukv_ref[...])
    kr = rope(u[:, _C_KR:_C_END]).astype(BF16)
    for h in range(MLA_HEADS):
        o = h * V7X_MXU_DIM
        q_ref[0, :, o:o + V7X_LANES] = (q[:, o:o + V7X_LANES] * scale_a).astype(BF16)
        q_ref[0, :, o + V7X_LANES:o + V7X_MXU_DIM] = (rope(q[:, o + V7X_LANES:o + V7X_MXU_DIM]) * scale_a).astype(BF16)
        k_ref[0, :, o:o + V7X_LANES] = kv[:, h * MLA_NOPE:(h + 1) * MLA_NOPE].astype(BF16)
        k_ref[0, :, o + V7X_LANES:o + V7X_MXU_DIM] = kr
    v_ref[0] = kv[:, MLA_HEADS * MLA_NOPE:].astype(BF16)
    for c in range((_C_KS - _C_QS) // V7X_LANES):
        o = c * V7X_LANES
        qs_ref[0, :, o:o + V7X_LANES] = (rope(u[:, _C_QS + o:_C_QS + o + V7X_LANES]) * scale_b).astype(BF16)
    for c in range(_SWA_W // V7X_LANES):
        o = c * V7X_LANES
        ks_ref[0, :, o:o + V7X_LANES] = rope(u[:, _C_KS + o:_C_KS + o + V7X_LANES]).astype(BF16)
    vs_ref[0] = u[:, _C_VS:_C_KR].astype(BF16)


def _attn_proj(h, mods, g, win, qn, kvn, wuq, wukv, cos, sin, nct):
    b, l, d = h.shape
    tl = TOKEN_TILE
    tok = lambda w: pl.BlockSpec((1, tl, w), lambda i, j: (i, j, 0))
    full = lambda a: pl.BlockSpec(a.shape, lambda i, j: (0,) * a.ndim)
    outs = [(_MLA_QK_W, BF16), (_MLA_QK_W, BF16), (MLA_HEADS * MLA_V, BF16),
            (SWA_HEADS * SWA_HEAD_DIM, BF16), (_SWA_W, BF16), (_SWA_W, BF16)]
    return pl.pallas_call(
        _attn_proj_kernel,
        out_shape=[jax.ShapeDtypeStruct((b, l, w), t) for w, t in outs],
        grid=(b, l // tl),
        in_specs=[tok(d),
                  pl.BlockSpec((1, 1, N_MODS, d), lambda i, j: (i, jnp.where(j < nct, 0, 1), 0, 0)),
                  full(g), full(win), full(qn), full(kvn), full(wuq), full(wukv),
                  pl.BlockSpec((tl, V7X_LANES), lambda i, j: (j, 0)),
                  pl.BlockSpec((tl, V7X_LANES), lambda i, j: (j, 0))],
        out_specs=[tok(w) for w, _ in outs],
        compiler_params=pltpu.CompilerParams(dimension_semantics=("parallel", "parallel"),
                                             vmem_limit_bytes=_vmem_limit(48)),
        name="attn_proj",
    )(h, mods, g, win, qn, kvn, wuq, wukv, cos, sin)


def _mla_kernel(q_ref, k_ref, v_ref, o_ref, *, nct_q, lc):
    j = pl.program_id(2)
    s = _dot_nt(q_ref[0], k_ref[0])
    col = lax.broadcasted_iota(jnp.int32, s.shape, 1)
    limit = jnp.where(j < nct_q, lc, s.shape[1])
    s = jnp.where(col < limit, s, NEG_INF)
    mx = jnp.max(s, axis=-1, keepdims=True)
    p = jnp.exp(s - mx)
    den = jnp.sum(p, axis=-1, keepdims=True)
    o_ref[0] = (_dot(p, v_ref[0]) / den).astype(o_ref.dtype)


def _mla_attention(q, k, v, lc, q_tile0):
    b, l, _ = q.shape
    tq = MLA_Q_TILE
    return pl.pallas_call(
        functools.partial(_mla_kernel, nct_q=lc // tq - q_tile0, lc=lc),
        out_shape=jax.ShapeDtypeStruct((b, l, MLA_HEADS * MLA_V), BF16),
        grid=(b, MLA_HEADS, l // tq - q_tile0),
        in_specs=[pl.BlockSpec((1, tq, V7X_MXU_DIM), lambda i, h, j: (i, j + q_tile0, h)),
                  pl.BlockSpec((1, l, V7X_MXU_DIM), lambda i, h, j: (i, 0, h)),
                  pl.BlockSpec((1, l, MLA_V), lambda i, h, j: (i, 0, h))],
        out_specs=pl.BlockSpec((1, tq, MLA_V), lambda i, h, j: (i, j + q_tile0, h)),
        compiler_params=pltpu.CompilerParams(dimension_semantics=("parallel", "parallel", "parallel"),
                                             vmem_limit_bytes=_vmem_limit(48)),
        name="mla_attention",
    )(q, k, v)


def _swa_kernel(sink_ref, q_ref, k_ref, v_ref, o_ref, *, lc, q_tile0):
    tq = SWA_Q_TILE
    l = k_ref.shape[1]
    r0 = (pl.program_id(1) + q_tile0) * tq
    start = pl.multiple_of(jnp.clip(r0 - WINDOW, lc, l - SWA_BAND), tq)
    rows = SWA_GROUP * tq
    row = lax.broadcasted_iota(jnp.int32, (rows, 1), 0)
    qpos = jnp.where(r0 >= lc, r0, -l) + row % tq
    kpos = start + lax.broadcasted_iota(jnp.int32, (1, SWA_BAND), 1)
    valid = jnp.abs(qpos - kpos) <= WINDOW
    lane = lax.broadcasted_iota(jnp.int32, (1, V7X_MXU_DIM), 1)
    for g in range(SWA_KV_HEADS):
        sl = slice(g * V7X_MXU_DIM, (g + 1) * V7X_MXU_DIM)
        qg = q_ref[0, :, sl]
        zero = jnp.zeros_like(qg)
        head = [(lane // SWA_HEAD_DIM) == hh for hh in range(SWA_GROUP)]
        qstack = jnp.concatenate([jnp.where(head[hh], qg, zero) for hh in range(SWA_GROUP)], axis=0)
        sc = _dot_nt(qstack, k_ref[0, 0:lc, sl])
        sb = jnp.where(valid, _dot_nt(qstack, k_ref[0, pl.ds(start, SWA_BAND), sl]), NEG_INF)
        sk = jnp.zeros((rows, 1), F32)
        for hh in range(SWA_GROUP):
            sk = jnp.where(row // tq == hh, sink_ref[g * SWA_GROUP + hh], sk)
        mx = jnp.maximum(jnp.maximum(jnp.max(sc, axis=-1, keepdims=True), jnp.max(sb, axis=-1, keepdims=True)), sk)
        pc = jnp.exp(sc - mx)
        pb = jnp.exp(sb - mx)
        den = jnp.sum(pc, axis=-1, keepdims=True) + jnp.sum(pb, axis=-1, keepdims=True) + jnp.exp(sk - mx)
        ostack = (_dot(pc, v_ref[0, 0:lc, sl]) + _dot(pb, v_ref[0, pl.ds(start, SWA_BAND), sl])) / den
        o = jnp.zeros((tq, V7X_MXU_DIM), F32)
        for hh in range(SWA_GROUP):
            o = o + jnp.where(head[hh], ostack[hh * tq:(hh + 1) * tq], 0.0)
        o_ref[0, :, sl] = o.astype(o_ref.dtype)


def _swa_attention(sinks, q, k, v, lc, q_tile0):
    b, l, _ = q.shape
    tq = SWA_Q_TILE
    return pl.pallas_call(
        functools.partial(_swa_kernel, lc=lc, q_tile0=q_tile0),
        out_shape=jax.ShapeDtypeStruct((b, l, SWA_HEADS * SWA_HEAD_DIM), BF16),
        grid=(b, l // tq - q_tile0),
        in_specs=[pl.BlockSpec(memory_space=pltpu.SMEM),
                  pl.BlockSpec((1, tq, SWA_HEADS * SWA_HEAD_DIM), lambda i, j: (i, j + q_tile0, 0)),
                  pl.BlockSpec((1, l, _SWA_W), lambda i, j: (i, 0, 0)),
                  pl.BlockSpec((1, l, _SWA_W), lambda i, j: (i, 0, 0))],
        out_specs=pl.BlockSpec((1, tq, SWA_HEADS * SWA_HEAD_DIM), lambda i, j: (i, j + q_tile0, 0)),
        compiler_params=pltpu.CompilerParams(dimension_semantics=("parallel", "parallel"),
                                             vmem_limit_bytes=_vmem_limit(48)),
        name="swa_attention",
    )(sinks, q, k, v)


def _route(n2, wrt, bias):
    logits = lax.dot_general(wrt, n2, (((1,), (1,)), ((), ())), precision=HIGHEST, preferred_element_type=F32)
    rows = logits.shape[1]
    shape3 = (N_GROUPS, GROUP_SIZE, rows)
    scores = _sigmoid(logits[0:N_EXPERTS]).reshape(shape3)
    choice = scores + bias
    ji = lax.broadcasted_iota(jnp.int32, shape3, 1).astype(F32)
    gi = lax.broadcasted_iota(jnp.int32, shape3, 0).astype(F32)
    ei = gi * GROUP_SIZE + ji

    def max2(a):
        return jnp.max(jnp.max(a, axis=1, keepdims=True), axis=0, keepdims=True)

    def min2(a):
        return jnp.min(jnp.min(a, axis=1, keepdims=True), axis=0, keepdims=True)

    m1 = jnp.max(choice, axis=1, keepdims=True)
    first = jnp.min(jnp.where(choice == m1, ji, float(GROUP_SIZE)), axis=1, keepdims=True)
    m2 = jnp.max(jnp.where(ji == first, -jnp.inf, choice), axis=1, keepdims=True)
    gs = m1 + m2
    gidx = lax.broadcasted_iota(jnp.int32, gs.shape, 0).astype(F32)
    gsel = jnp.zeros_like(gs)
    for _ in range(TOPK_GROUPS):
        mx = jnp.max(gs, axis=0, keepdims=True)
        pick = gidx == jnp.min(jnp.where(gs == mx, gidx, float(N_GROUPS)), axis=0, keepdims=True)
        gsel = jnp.where(pick, 1.0, gsel)
        gs = jnp.where(pick, -jnp.inf, gs)
    cand = jnp.where(gsel > 0.0, choice, -jnp.inf)
    esel = jnp.zeros_like(cand)
    for _ in range(TOP_K):
        mx = max2(cand)
        pick = ei == min2(jnp.where(cand == mx, ei, float(N_EXPERTS)))
        esel = jnp.where(pick, 1.0, esel)
        cand = jnp.where(pick, -jnp.inf, cand)
    w = scores * esel
    w = w * (ROUTED_SCALE / jnp.sum(jnp.sum(w, axis=1, keepdims=True), axis=0, keepdims=True))
    w = w.reshape(N_EXPERTS, rows)
    return jnp.concatenate([w, jnp.zeros((logits.shape[0] - N_EXPERTS, rows), F32)], axis=0)


def _mixer_tail(o, h_ref, m, gffn_ref, wrt_ref, bias_ref, hn_ref, n2_ref, gates_ref):
    hn = h_ref[0] + m[2:3] * o
    hn_ref[0] = hn
    n2 = _norm_mod(hn, gffn_ref[...], m[3:4], m[4:5])
    n2_ref[0] = n2.astype(BF16)
    gates_ref[0] = _route(n2, wrt_ref[...], bias_ref[...]).T


def _attn_out_kernel(a_ref, b_ref, h_ref, mods_ref, wo_ref, gffn_ref, wrt_ref, bias_ref, hn_ref, n2_ref, gates_ref):
    wa = MLA_HEADS * MLA_V
    o = _dot(a_ref[0], wo_ref[0:wa, :]) + _dot(b_ref[0], wo_ref[wa:, :])
    _mixer_tail(o, h_ref, mods_ref[0, 0], gffn_ref, wrt_ref, bias_ref, hn_ref, n2_ref, gates_ref)


def _tail_out_shapes(b, l, d):
    return [jax.ShapeDtypeStruct((b, l, d), F32), jax.ShapeDtypeStruct((b, l, d), BF16),
            jax.ShapeDtypeStruct((b, l, GATE_W), F32)]


def _attn_out(a, bm, h, mods, wo, gffn, wrt, bias, nct):
    b, l, d = h.shape
    tl = TOKEN_TILE
    tok = lambda w: pl.BlockSpec((1, tl, w), lambda i, j: (i, j, 0))
    full = lambda x: pl.BlockSpec(x.shape, lambda i, j: (0,) * x.ndim)
    return pl.pallas_call(
        _attn_out_kernel,
        out_shape=_tail_out_shapes(b, l, d),
        grid=(b, l // tl),
        in_specs=[tok(a.shape[2]), tok(bm.shape[2]), tok(d),
                  pl.BlockSpec((1, 1, N_MODS, d), lambda i, j: (i, jnp.where(j < nct, 0, 1), 0, 0)),
                  full(wo), full(gffn), full(wrt), full(bias)],
        out_specs=[tok(d), tok(d), tok(GATE_W)],
        compiler_params=pltpu.CompilerParams(dimension_semantics=("parallel", "parallel"),
                                             vmem_limit_bytes=_vmem_limit(40)),
        name="attn_out",
    )(a, bm, h, mods, wo, gffn, wrt, bias)


def _moe_kernel(x_ref, g_ref, wg_ref, wu_ref, wd_ref, sg_ref, su_ref, sd_ref, h_ref, mods_ref, gfin_ref,
                o_ref, acc_ref, *, tiles_per_sample, lc, final_norm):
    e = pl.program_id(1)
    x = x_ref[...]

    @pl.when(e == 0)
    def _():
        hs = _silu(_dot(x, sg_ref[...])) * _dot(x, su_ref[...])
        acc_ref[...] = _dot(hs, sd_ref[...])

    lane = lax.broadcasted_iota(jnp.int32, (1, GATE_W), 1)
    gcol = jnp.sum(jnp.where(lane == e, g_ref[...], 0.0), axis=1, keepdims=True)
    act = _silu(_dot(x, wg_ref[0])) * _dot(x, wu_ref[0]) * gcol
    acc_ref[...] += _dot(act, wd_ref[0])

    @pl.when(e == pl.num_programs(1) - 1)
    def _():
        tm = x.shape[0]
        pos = (pl.program_id(0) % tiles_per_sample) * tm + lax.broadcasted_iota(jnp.int32, (tm, 1), 0)
        gate = jnp.where(pos < lc, mods_ref[0, 0, N_MODS - 1:N_MODS, :], mods_ref[0, 1, N_MODS - 1:N_MODS, :])
        y = h_ref[...] + gate * acc_ref[...]
        if final_norm:
            y = _rms(y, gfin_ref[...])
        o_ref[...] = y


def _moe(x, gates, wg, wu, wd, sg, su, sd, h, mods, gfin, lc, tm, final_norm):
    t, d = x.shape
    b = mods.shape[0]
    tps = (t // b) // tm
    ne, _, f = wg.shape
    full = lambda a: pl.BlockSpec(a.shape, lambda i, e: (0,) * a.ndim)
    return pl.pallas_call(
        functools.partial(_moe_kernel, tiles_per_sample=tps, lc=lc, final_norm=final_norm),
        out_shape=jax.ShapeDtypeStruct((t, d), F32),
        grid=(t // tm, ne),
        in_specs=[pl.BlockSpec((tm, d), lambda i, e: (i, 0)),
                  pl.BlockSpec((tm, GATE_W), lambda i, e: (i, 0)),
                  pl.BlockSpec((1, d, f), lambda i, e: (e, 0, 0)),
                  pl.BlockSpec((1, d, f), lambda i, e: (e, 0, 0)),
                  pl.BlockSpec((1, f, d), lambda i, e: (e, 0, 0)),
                  full(sg), full(su), full(sd),
                  pl.BlockSpec((tm, d), lambda i, e: (i, 0)),
                  pl.BlockSpec((1, 2, N_MODS, d), lambda i, e: (i // tps, 0, 0, 0)),
                  full(gfin)],
        out_specs=pl.BlockSpec((tm, d), lambda i, e: (i, 0)),
        scratch_shapes=[pltpu.VMEM((tm, d), F32)],
        compiler_params=pltpu.CompilerParams(dimension_semantics=("parallel", "arbitrary"),
                                             vmem_limit_bytes=_vmem_limit(56)),
        name="moe_ffn",
    )(x, gates, wg, wu, wd, sg, su, sd, h, mods, gfin)


def _rwkv_proj_kernel(h_ref, hp_ref, hx_ref, mods_ref, g_ref, mu_ref, wr_ref, wk_ref, wv_ref, g1_ref, g2_ref,
                      w1_ref, w2_ref, a1_ref, a2_ref, w0_ref, a0_ref, kk_ref, ka_ref, rk_ref, bd_ref,
                      r_out, v_out, kk_out, g_out, km_out, b_out, lw_out, bonus_out, *, nct):
    j = pl.program_id(1)
    nt = pl.num_programs(1)
    m = mods_ref[0, 0]
    g = g_ref[...]
    n = _norm_mod(h_ref[0], g, m[0:1], m[1:2])
    tl, d = n.shape
    seg_first = (j == 0) | (j == nct)
    seg_last = (j == nct - 1) | (j == nt - 1)
    n_prev = _norm_mod(hp_ref[0], g, m[0:1], m[1:2])[7:8] * jnp.where(seg_first, 0.0, 1.0)
    n_next = _norm_mod(hx_ref[0], g, m[0:1], m[1:2])[0:1] * jnp.where(seg_last, 0.0, 1.0)
    row = lax.broadcasted_iota(jnp.int32, (tl, 1), 0)
    prev = jnp.where(row == 0, n_prev, pltpu.roll(n, 1, axis=0))
    nxt = jnp.where(row == tl - 1, n_next, pltpu.roll(n, tl - 1, axis=0))
    lane = lax.broadcasted_iota(jnp.int32, (1, d), 1)
    xx = jnp.where(lane < d // 2, prev, nxt) - n
    mu = mu_ref[...]
    xr, xw, xk, xv, xa, xg = [n + xx * mu[i:i + 1] for i in range(6)]
    r = _dot(xr, wr_ref[...])
    k = _dot(xk, wk_ref[...])
    v = _dot(xv, wv_ref[...])
    g_out[0] = _dot(_sigmoid(_dot(xg, g1_ref[...])), g2_ref[...]).astype(g_out.dtype)
    tw = jnp.tanh(_dot(xw, w1_ref[...]))
    ta = _dot(xa, a1_ref[...])
    bd = bd_ref[...]
    kk = k * kk_ref[...]
    kk = kk / jnp.maximum(jnp.sqrt(_head_sum(kk * kk, bd)), 1e-12)
    r_out[0] = r.astype(r_out.dtype)
    v_out[0] = v.astype(v_out.dtype)
    kk_out[0] = kk.astype(kk_out.dtype)
    bonus = jnp.zeros_like(v)
    for dr in range(2):
        zw = w0_ref[dr:dr + 1, :] + _dot(tw, w2_ref[dr])
        lw_out[dr, 0] = -jnp.exp(-0.5) * _sigmoid(zw)
        a = _sigmoid(a0_ref[dr:dr + 1, :] + _dot(ta, a2_ref[dr]))
        km = k * (1.0 + (a - 1.0) * ka_ref[...])
        km_out[dr, 0] = km.astype(km_out.dtype)
        b_out[dr, 0] = (kk * a).astype(b_out.dtype)
        bonus = bonus + _head_sum(r * km * rk_ref[...], bd) * v
    bonus_out[0] = bonus


def _rwkv_proj(h, mods, g, mu, wr, wk, wv, g1, g2, w1, w2, a1, a2, w0, a0, kk, ka, rk, bd, nct):
    b, l, d = h.shape
    tl = TOKEN_TILE
    nb8 = l // 8
    tok = pl.BlockSpec((1, tl, d), lambda i, j: (i, j, 0))
    tok2 = pl.BlockSpec((2, 1, tl, d), lambda i, j: (0, i, j, 0))
    full = lambda x: pl.BlockSpec(x.shape, lambda i, j: (0,) * x.ndim)
    sds = jax.ShapeDtypeStruct
    return pl.pallas_call(
        functools.partial(_rwkv_proj_kernel, nct=nct),
        out_shape=[sds((b, l, d), BF16), sds((b, l, d), BF16), sds((b, l, d), BF16), sds((b, l, d), BF16),
                   sds((2, b, l, d), BF16), sds((2, b, l, d), BF16), sds((2, b, l, d), F32), sds((b, l, d), F32)],
        grid=(b, l // tl),
        in_specs=[tok,
                  pl.BlockSpec((1, 8, d), lambda i, j: (i, jnp.maximum(j * (tl // 8) - 1, 0), 0)),
                  pl.BlockSpec((1, 8, d), lambda i, j: (i, jnp.minimum((j + 1) * (tl // 8), nb8 - 1), 0)),
                  pl.BlockSpec((1, 1, N_MODS, d), lambda i, j: (i, jnp.where(j < nct, 0, 1), 0, 0)),
                  full(g), full(mu), full(wr), full(wk), full(wv), full(g1), full(g2), full(w1), full(w2),
                  full(a1), full(a2), full(w0), full(a0), full(kk), full(ka), full(rk), full(bd)],
        out_specs=[tok, tok, tok, tok, tok2, tok2, tok2, tok],
        compiler_params=pltpu.CompilerParams(dimension_semantics=("parallel", "parallel"),
                                             vmem_limit_bytes=_vmem_limit(56)),
        name="rwkv_proj",
    )(h, h, h, mods, g, mu, wr, wk, wv, g1, g2, w1, w2, a1, a2, w0, a0, kk, ka, rk, bd)


def _wkv_kernel(r_ref, v_ref, kk_ref, km_ref, b_ref, lw_ref, y_ref, st_ref):
    c = WKV_CHUNK
    w = WKV_PAIR
    rev = pl.program_id(0)
    sign = 1 - 2 * rev

    @pl.when(pl.program_id(2) == 0)
    def _():
        st_ref[...] = jnp.zeros_like(st_ref)

    ti = lax.broadcasted_iota(jnp.int32, (c, c), 0)
    si = lax.broadcasted_iota(jnp.int32, (c, c), 1)
    tri = jnp.where((si - ti) * sign <= 0, 1.0, 0.0).astype(F32)
    lw = lw_ref[0, 0]
    l_incl = jnp.dot(tri, lw, precision=HIGHEST, preferred_element_type=F32)
    l_tot = jnp.sum(lw, axis=0, keepdims=True)
    mid = 0.5 * l_tot
    e_pos = jnp.exp(l_incl - mid)
    e_neg = jnp.exp(mid - l_incl)
    e_mid = jnp.exp(mid)
    rt = r_ref[0].astype(F32) * e_pos
    kt = kk_ref[0].astype(F32) * jnp.exp(l_incl - lw - mid)
    kh = km_ref[0, 0].astype(F32) * e_neg
    bh = b_ref[0, 0].astype(F32) * e_neg
    v = v_ref[0]

    ri = lax.broadcasted_iota(jnp.int32, (w, w), 0)
    ci = lax.broadcasted_iota(jnp.int32, (w, w), 1)
    same = (ri // c) == (ci // c)
    dlt = (ci % c - ri % c) * sign
    strict = same & (dlt < 0)
    incl = same & (dlt <= 0)
    eye = jnp.where(ri == ci, 1.0, 0.0).astype(F32)
    lane = lax.broadcasted_iota(jnp.int32, (1, w), 1)
    h0 = lane < RWKV_HEAD

    def rows2(x):
        return jnp.concatenate([jnp.where(h0, x, 0.0), jnp.where(h0, 0.0, x)], axis=0)

    def fold(x):
        return x[:c] + x[c:]

    pairs = range(st_ref.shape[0])
    sls = [slice(p * w, (p + 1) * w) for p in pairs]
    em = [e_mid[:, s] for s in sls]
    g = [_dot_nt(jnp.concatenate([rows2(kt[:, s]), rows2(rt[:, s])], axis=0),
                 jnp.concatenate([kh[:, s], kh[:, s], bh[:, s], bh[:, s]], axis=0)) for s in sls]
    a_kk = [jnp.where(strict, x[:w, :w], 0.0) for x in g]
    nmat = [jnp.where(strict, x[:w, w:], 0.0) for x in g]
    a_rk = [jnp.where(incl, x[w:, :w], 0.0) for x in g]
    a_rb = [jnp.where(incl, x[w:, w:], 0.0) for x in g]
    v32 = [v[:, s].astype(F32) for s in sls]
    v_rows = [rows2(x) for x in v32]
    r_pre = [_dot(a_kk[p], v_rows[p]) for p in pairs]
    tinv = [eye - x for x in nmat]
    npow = nmat
    for _ in range(c.bit_length() - 2):
        npow = [_dot(x, x) for x in npow]
        tinv = [tinv[p] + _dot(tinv[p], npow[p]) for p in pairs]
    sol = [_dot(tinv[p], jnp.concatenate([r_pre[p], rows2(kt[:, sls[p]] * em[p])], axis=1)) for p in pairs]
    u_rows = [x[:, :w] for x in sol]
    kq_rows = [x[:, w:] for x in sol]
    y_pre = [fold(_dot(jnp.concatenate([a_rk[p], -a_rb[p]], axis=1),
                       jnp.concatenate([v_rows[p], u_rows[p]], axis=0))) for p in pairs]
    r_eff = [rt[:, sls[p]] * em[p] - fold(_dot(a_rb[p], kq_rows[p])) for p in pairs]
    bbar = [bh[:, sls[p]] * em[p] for p in pairs]
    kbar = [kh[:, sls[p]] * em[p] for p in pairs]
    eye_dec = [eye * (em[p] * em[p]) for p in pairs]
    mmat = [eye_dec[p] - jnp.where(same, _dot_tn(fold(kq_rows[p]), bbar[p]), 0.0) for p in pairs]
    s_pre = [jnp.where(same, _dot_tn(jnp.concatenate([v32[p], -fold(u_rows[p])], axis=0),
                                     jnp.concatenate([kbar[p], bbar[p]], axis=0)), 0.0) for p in pairs]
    st = [st_ref[p] for p in pairs]
    for p in pairs:
        y_ref[0, 0, :, sls[p]] = _dot_nt(r_eff[p], st[p]) + y_pre[p]
    for p in pairs:
        hi = st[p].astype(BF16)
        lo = (st[p] - hi.astype(F32)).astype(BF16)
        mb = mmat[p].astype(BF16)
        st_ref[p] = (jnp.dot(hi, mb, preferred_element_type=F32) + jnp.dot(lo, mb, preferred_element_type=F32)
                     + s_pre[p])


def _wkv(r, v, kk, km, bv, lw, lc):
    b, l, d = r.shape
    c = WKV_CHUNK
    ncc = lc // c
    nlc = (l - lc) // c

    def chunk(dr, i):
        return jnp.where(dr == 0, i, jnp.where(i < ncc, ncc - 1 - i, nlc + 2 * ncc - 1 - i))

    shared = pl.BlockSpec((1, c, d), lambda dr, bi, i: (bi, chunk(dr, i), 0))
    per_dir = pl.BlockSpec((1, 1, c, d), lambda dr, bi, i: (dr, bi, chunk(dr, i), 0))
    return pl.pallas_call(
        _wkv_kernel,
        out_shape=jax.ShapeDtypeStruct((2, b, l, d), F32),
        grid=(2, b, l // c),
        in_specs=[shared, shared, shared, per_dir, per_dir, per_dir],
        out_specs=per_dir,
        scratch_shapes=[pltpu.VMEM((d // WKV_PAIR, WKV_PAIR, WKV_PAIR), F32)],
        compiler_params=pltpu.CompilerParams(dimension_semantics=("parallel", "parallel", "arbitrary"),
                                             vmem_limit_bytes=_vmem_limit(32)),
        name="wkv7_chunked",
    )(r, v, kk, km, bv, lw)


def _rwkv_out_kernel(y_ref, bonus_ref, g_ref, lnw_ref, lnb_ref, wo_ref, bd_ref, h_ref, mods_ref, gffn_ref,
                     wrt_ref, bias_ref, hn_ref, n2_ref, gates_ref):
    y = y_ref[0, 0] + y_ref[1, 0]
    bd = bd_ref[...]
    mean = _head_sum(y, bd) * (1.0 / RWKV_HEAD)
    yc = y - mean
    var = _head_sum(yc * yc, bd) * (1.0 / RWKV_HEAD)
    yn = yc * lax.rsqrt(var + GN_EPS) * lnw_ref[...] + lnb_ref[...]
    out = (yn + bonus_ref[0]) * g_ref[0].astype(F32)
    _mixer_tail(_dot(out, wo_ref[...]), h_ref, mods_ref[0, 0], gffn_ref, wrt_ref, bias_ref, hn_ref, n2_ref, gates_ref)


def _rwkv_out(y, bonus, g, lnw, lnb, wo, bd, h, mods, gffn, wrt, bias, nct):
    b, l, d = h.shape
    tl = TOKEN_TILE
    tok = lambda w: pl.BlockSpec((1, tl, w), lambda i, j: (i, j, 0))
    full = lambda x: pl.BlockSpec(x.shape, lambda i, j: (0,) * x.ndim)
    return pl.pallas_call(
        _rwkv_out_kernel,
        out_shape=_tail_out_shapes(b, l, d),
        grid=(b, l // tl),
        in_specs=[pl.BlockSpec((2, 1, tl, d), lambda i, j: (0, i, j, 0)), tok(d), tok(d),
                  full(lnw), full(lnb), full(wo), full(bd), tok(d),
                  pl.BlockSpec((1, 1, N_MODS, d), lambda i, j: (i, jnp.where(j < nct, 0, 1), 0, 0)),
                  full(gffn), full(wrt), full(bias)],
        out_specs=[tok(d), tok(d), tok(GATE_W)],
        compiler_params=pltpu.CompilerParams(dimension_semantics=("parallel", "parallel"),
                                             vmem_limit_bytes=_vmem_limit(40)),
        name="rwkv_out",
    )(y, bonus, g, lnw, lnb, wo, bd, h, mods, gffn, wrt, bias)


def _rope_table(n_lat, n_ctx):
    dim = SWA_HEAD_DIM
    nf = dim // 4
    inv = ROPE_THETA ** (-jnp.arange(nf, dtype=F32) / nf)
    row = jnp.repeat(jnp.arange(n_lat // GRID_W, dtype=F32), GRID_W)
    col = jnp.tile(jnp.arange(GRID_W, dtype=F32), n_lat // GRID_W)
    ar = row[:, None] * inv
    ac = col[:, None] * inv
    ang = jnp.concatenate([ar, ar, ac, ac], axis=-1)
    cos = jnp.concatenate([jnp.ones((n_ctx, dim), F32), jnp.cos(ang)], axis=0)
    sin = jnp.concatenate([jnp.zeros((n_ctx, dim), F32), jnp.sin(ang)], axis=0)
    return jnp.tile(cos, (1, 2)), jnp.tile(sin, (1, 2))


def _layout_attn_weights(w_in, w_uq, w_ukv):
    d = w_in.shape[0]
    s0 = MLA_Q_RANK
    s1 = s0 + MLA_KV_RANK
    s2 = s1 + MLA_ROPE
    s3 = s2 + SWA_HEADS * SWA_HEAD_DIM
    s4 = s3 + SWA_KV_HEADS * SWA_HEAD_DIM
    rep = lambda w: jnp.concatenate(
        [jnp.tile(w[:, g * SWA_HEAD_DIM:(g + 1) * SWA_HEAD_DIM], (1, SWA_GROUP)) for g in range(SWA_KV_HEADS)], axis=1)
    win = jnp.concatenate([w_in[:, :s1], w_in[:, s2:s3], rep(w_in[:, s3:s4]), rep(w_in[:, s4:]),
                           w_in[:, s1:s2], jnp.zeros((d, V7X_LANES - MLA_ROPE), w_in.dtype)], axis=1)
    qh = MLA_NOPE + MLA_ROPE
    pad = jnp.zeros((w_uq.shape[0], V7X_MXU_DIM - qh), w_uq.dtype)
    wuq = jnp.concatenate([jnp.concatenate([w_uq[:, h * qh:(h + 1) * qh], pad], axis=1) for h in range(MLA_HEADS)], axis=1)
    kvh = MLA_NOPE + MLA_V
    wukv = jnp.concatenate([w_ukv[:, h * kvh:h * kvh + MLA_NOPE] for h in range(MLA_HEADS)]
                           + [w_ukv[:, h * kvh + MLA_NOPE:(h + 1) * kvh] for h in range(MLA_HEADS)], axis=1)
    return win.astype(BF16), wuq.astype(BF16), wukv.astype(BF16)


def _lora_pair(w_down, w_up):
    rank = w_down.shape[2]
    down = jnp.concatenate([w_down[0], w_down[1]], axis=1)
    z = jnp.zeros((rank, w_up.shape[2]), w_up.dtype)
    up = jnp.stack([jnp.concatenate([w_up[0], z], axis=0), jnp.concatenate([z, w_up[1]], axis=0)], axis=0)
    return down.astype(BF16), up.astype(BF16)


def _head_block_diag():
    i = jnp.arange(V7X_MXU_DIM) // RWKV_HEAD
    return (i[:, None] == i[None, :]).astype(BF16)


def kernel(x, c, ctx, c_ctx, ada_w, ada_b, norm_mix, norm_ffn, norm_final, attn_w_in, attn_q_norm, attn_kv_norm, attn_w_uq, attn_w_ukv, attn_sinks, attn_w_o, rwkv_mu, rwkv_w_r, rwkv_w_k, rwkv_w_v, rwkv_w_o, rwkv_g1, rwkv_g2, rwkv_w0, rwkv_w1, rwkv_w2, rwkv_a0, rwkv_a1, rwkv_a2, rwkv_k_k, rwkv_k_a, rwkv_r_k, rwkv_ln_w, rwkv_ln_b, moe_router, moe_bias, moe_w_gate, moe_w_up, moe_w_down, moe_ws_gate, moe_ws_up, moe_ws_down):
    bsz, s, d = x.shape
    lc = ctx.shape[1]
    l = lc + s
    depth = ada_w.shape[0]
    nct = lc // TOKEN_TILE
    assert lc % TOKEN_TILE == 0 and s % TOKEN_TILE == 0 and s >= SWA_BAND and lc % WKV_CHUNK == 0
    assert d % V7X_MXU_DIM == 0 and WKV_CHUNK * 2 == V7X_LANES
    moe_tile = l // 4
    assert l % 4 == 0 and moe_tile % 16 == 0

    h = jnp.concatenate([ctx, x], axis=1)
    cos, sin = _rope_table(s, lc)
    bd = _head_block_diag()
    rows = -(-(bsz + 1) // 8) * 8
    cc = jnp.concatenate([c, c_ctx[None, :], jnp.zeros((rows - bsz - 1, d), F32)], axis=0)
    row2 = lambda a: a.reshape(1, -1)

    for li in range(depth):
        with_ctx = li < depth - 1
        i = li // 2
        ada = _ada_mods(cc, ada_w[li], ada_b[li])
        mods = jnp.stack([jnp.broadcast_to(ada[bsz].reshape(1, N_MODS, d), (bsz, N_MODS, d)),
                          ada[:bsz].reshape(bsz, N_MODS, d)], axis=1)
        wrt = jnp.concatenate([moe_router[li].T, jnp.zeros((GATE_W - N_EXPERTS, d), F32)], axis=0)
        bias = moe_bias[li].reshape(N_GROUPS, GROUP_SIZE, 1)
        if li % 2 == 0:
            win, wuq, wukv = _layout_attn_weights(attn_w_in[i], attn_w_uq[i], attn_w_ukv[i])
            q, k, v, qs, ks, vs = _attn_proj(h, mods, row2(norm_mix[li]), win, row2(attn_q_norm[i]),
                                             row2(attn_kv_norm[i]), wuq, wukv, cos, sin, nct)
            a = _mla_attention(q, k, v, lc, 0 if with_ctx else lc // MLA_Q_TILE)
            bm = _swa_attention(attn_sinks[i], qs, ks, vs, lc, 0 if with_ctx else lc // SWA_Q_TILE)
            h, n2, gates = _attn_out(a, bm, h, mods, attn_w_o[i].astype(BF16), row2(norm_ffn[li]), wrt, bias, nct)
        else:
            w1, w2 = _lora_pair(rwkv_w1[i], rwkv_w2[i])
            a1, a2 = _lora_pair(rwkv_a1[i], rwkv_a2[i])
            r, v, kk, g, km, bv, lw, bonus = _rwkv_proj(
                h, mods, row2(norm_mix[li]), rwkv_mu[i], rwkv_w_r[i].astype(BF16), rwkv_w_k[i].astype(BF16),
                rwkv_w_v[i].astype(BF16), rwkv_g1[i].astype(BF16), rwkv_g2[i].astype(BF16), w1, w2, a1, a2,
                rwkv_w0[i], rwkv_a0[i], row2(rwkv_k_k[i]), row2(rwkv_k_a[i]), row2(rwkv_r_k[i]), bd, nct)
            y = _wkv(r, v, kk, km, bv, lw, lc)
            h, n2, gates = _rwkv_out(y, bonus, g, row2(rwkv_ln_w[i]), row2(rwkv_ln_b[i]), rwkv_w_o[i].astype(BF16),
                                     bd, h, mods, row2(norm_ffn[li]), wrt, bias, nct)
        h = _moe(n2.reshape(bsz * l, d), gates.reshape(bsz * l, GATE_W), moe_w_gate[li].astype(BF16),
                 moe_w_up[li].astype(BF16), moe_w_down[li].astype(BF16), moe_ws_gate[li].astype(BF16),
                 moe_ws_up[li].astype(BF16), moe_ws_down[li].astype(BF16), h.reshape(bsz * l, d), mods,
                 row2(norm_final), lc, moe_tile, li == depth - 1).reshape(bsz, l, d)
    return h[:, lc:]
```

```python
import functools

import jax
import jax.numpy as jnp
from jax import lax
from jax.experimental import pallas as pl
from jax.experimental.pallas import tpu as pltpu
from jax.experimental.pallas import tpu_sc as plsc

F32 = jnp.float32
BF16 = jnp.bfloat16
HIGHEST = lax.Precision.HIGHEST

GRID_W = 64
NORM_EPS = 1e-6
ROPE_THETA = 10000.0
NEG_INF = -1e30
N_MODS = 6

MLA_HEADS = 4
MLA_Q_RANK = 384
MLA_KV_RANK = 256
MLA_NOPE = 128
MLA_ROPE = 64
MLA_V = 128

SWA_HEADS = 8
SWA_KV_HEADS = 2
SWA_GROUP = SWA_HEADS // SWA_KV_HEADS
SWA_HEAD_DIM = 64
WINDOW = 128

RWKV_HEAD = 64
DECAY_LORA = 64
ICLR_LORA = 64
GATE_LORA = 128
GN_EPS = 64e-5

N_EXPERTS = 64
TOP_K = 6
N_GROUPS = 8
TOPK_GROUPS = 4
GROUP_SIZE = N_EXPERTS // N_GROUPS
ROUTED_SCALE = 2.5
GATE_W = 128

V7X_LANES = 128
V7X_MXU_DIM = 256
V7X_VMEM_BYTES = 64 * 1024 * 1024
V7X_SC_CORES = 2
V7X_SC_SUBCORES = 16
V7X_SC_WORKERS = V7X_SC_CORES * V7X_SC_SUBCORES

TOKEN_TILE = 256
MLA_Q_TILE = 256
SWA_Q_TILE = 128
SWA_BAND = SWA_Q_TILE + 2 * WINDOW
WKV_CHUNK = 64
WKV_PAIR = 2 * RWKV_HEAD
MOE_ROW_TILE = 512
SC_MAX_CHUNK = 64


def _vmem_limit(mib):
    return min(mib * 1024 * 1024, V7X_VMEM_BYTES - 4 * 1024 * 1024)


def _dot(a, b):
    return jnp.dot(a.astype(BF16), b.astype(BF16), preferred_element_type=F32)


def _dot_nt(a, b):
    return lax.dot_general(a.astype(BF16), b.astype(BF16), (((1,), (1,)), ((), ())),
                           preferred_element_type=F32)


def _dot_tn(a, b):
    return lax.dot_general(a.astype(BF16), b.astype(BF16), (((0,), (0,)), ((), ())),
                           preferred_element_type=F32)


def _sigmoid(x):
    return 1.0 / (1.0 + jnp.exp(-x))


def _silu(x):
    return x * _sigmoid(x)


def _rms(x, g):
    return x * lax.rsqrt(jnp.mean(x * x, axis=-1, keepdims=True) + NORM_EPS) * g


def _norm_mod(x, g, shift, scale):
    return _rms(x, g) * (1.0 + scale) + shift


def _split_dot(x, w):
    hi = x.astype(BF16)
    lo = (x - hi.astype(F32)).astype(BF16)
    return (jnp.dot(hi, w, preferred_element_type=F32) + jnp.dot(lo, w, preferred_element_type=F32))


def _head_sum(x, bd):
    w = bd.shape[0]
    parts = [_split_dot(x[:, c * w:(c + 1) * w], bd) for c in range(x.shape[1] // w)]
    return jnp.concatenate(parts, axis=1)


def _ada_kernel(c_ref, w_ref, b_ref, o_ref):
    s = _silu(c_ref[...])
    o_ref[...] = jnp.dot(s, w_ref[...], precision=HIGHEST, preferred_element_type=F32) + b_ref[...]


def _ada_mods(cc, w, b):
    rows, d = cc.shape
    n = w.shape[1]
    return pl.pallas_call(
        _ada_kernel,
        out_shape=jax.ShapeDtypeStruct((rows, n), F32),
        grid=(n // d,),
        in_specs=[pl.BlockSpec((rows, d), lambda i: (0, 0)),
                  pl.BlockSpec((d, d), lambda i: (0, i)),
                  pl.BlockSpec((1, d), lambda i: (0, i))],
        out_specs=pl.BlockSpec((rows, d), lambda i: (0, i)),
        compiler_params=pltpu.CompilerParams(dimension_semantics=("parallel",),
                                             vmem_limit_bytes=_vmem_limit(32)),
        name="ada_mods",
    )(cc, w, b.reshape(1, n))


def _rope128(x, cos, sin, first_half):
    rot = jnp.where(first_half, -pltpu.roll(x, V7X_LANES - 16, axis=1), pltpu.roll(x, 16, axis=1))
    return x * cos + rot * sin


_C_CQ = 0
_C_CKV = _C_CQ + MLA_Q_RANK
_C_QS = _C_CKV + MLA_KV_RANK
_C_KS = _C_QS + SWA_HEADS * SWA_HEAD_DIM
_C_VS = _C_KS + SWA_KV_HEADS * V7X_MXU_DIM
_C_KR = _C_VS + SWA_KV_HEADS * V7X_MXU_DIM
_C_END = _C_KR + V7X_LANES
_SWA_W = SWA_KV_HEADS * V7X_MXU_DIM
_MLA_QK_W = MLA_HEADS * V7X_MXU_DIM


def _attn_proj_kernel(h_ref, mods_ref, g_ref, win_ref, qn_ref, kvn_ref, wuq_ref, wukv_ref, cos_ref, sin_ref,
                      q_ref, k_ref, v_ref, qs_ref, ks_ref, vs_ref):
    m = mods_ref[0, 0]
    n = _norm_mod(h_ref[0], g_ref[...], m[0:1], m[1:2])
    u = _dot(n, win_ref[...])
    cos = cos_ref[...]
    sin = sin_ref[...]
    lane = lax.broadcasted_iota(jnp.int32, (1, V7X_LANES), 1)
    first_half = (lane % 32) < 16

    def rope(x):
        return _rope128(x, cos, sin, first_half)

    scale_a = (MLA_NOPE + MLA_ROPE) ** -0.5
    scale_b = SWA_HEAD_DIM ** -0.5
    q = _dot(_rms(u[:, _C_CQ:_C_CKV], qn_ref[...]), wuq_ref[...])
    kv = _dot(_rms(u[:, _C_CKV:_C_QS], kvn_ref[...]), wukv_ref[...])
    kr = rope(u[:, _C_KR:_C_END]).astype(BF16)
    for h in range(MLA_HEADS):
        o = h * V7X_MXU_DIM
        q_ref[0, :, o:o + V7X_LANES] = (q[:, o:o + V7X_LANES] * scale_a).astype(BF16)
        q_ref[0, :, o + V7X_LANES:o + V7X_MXU_DIM] = (rope(q[:, o + V7X_LANES:o + V7X_MXU_DIM]) * scale_a).astype(BF16)
        k_ref[0, :, o:o + V7X_LANES] = kv[:, h * MLA_NOPE:(h + 1) * MLA_NOPE].astype(BF16)
        k_ref[0, :, o + V7X_LANES:o + V7X_MXU_DIM] = kr
    v_ref[0] = kv[:, MLA_HEADS * MLA_NOPE:].astype(BF16)
    for c in range((_C_KS - _C_QS) // V7X_LANES):
        o = c * V7X_LANES
        qs_ref[0, :, o:o + V7X_LANES] = (rope(u[:, _C_QS + o:_C_QS + o + V7X_LANES]) * scale_b).astype(BF16)
    for c in range(_SWA_W // V7X_LANES):
        o = c * V7X_LANES
        ks_ref[0, :, o:o + V7X_LANES] = rope(u[:, _C_KS + o:_C_KS + o + V7X_LANES]).astype(BF16)
    vs_ref[0] = u[:, _C_VS:_C_KR].astype(BF16)


def _attn_proj(h, mods, g, win, qn, kvn, wuq, wukv, cos, sin, nct):
    b, l, d = h.shape
    tl = TOKEN_TILE
    tok = lambda w: pl.BlockSpec((1, tl, w), lambda i, j: (i, j, 0))
    full = lambda a: pl.BlockSpec(a.shape, lambda i, j: (0,) * a.ndim)
    outs = [(_MLA_QK_W, BF16), (_MLA_QK_W, BF16), (MLA_HEADS * MLA_V, BF16),
            (SWA_HEADS * SWA_HEAD_DIM, BF16), (_SWA_W, BF16), (_SWA_W, BF16)]
    return pl.pallas_call(
        _attn_proj_kernel,
        out_shape=[jax.ShapeDtypeStruct((b, l, w), t) for w, t in outs],
        grid=(b, l // tl),
        in_specs=[tok(d),
                  pl.BlockSpec((1, 1, N_MODS, d), lambda i, j: (i, jnp.where(j < nct, 0, 1), 0, 0)),
                  full(g), full(win), full(qn), full(kvn), full(wuq), full(wukv),
                  pl.BlockSpec((tl, V7X_LANES), lambda i, j: (j, 0)),
                  pl.BlockSpec((tl, V7X_LANES), lambda i, j: (j, 0))],
        out_specs=[tok(w) for w, _ in outs],
        compiler_params=pltpu.CompilerParams(dimension_semantics=("parallel", "parallel"),
                                             vmem_limit_bytes=_vmem_limit(48)),
        name="attn_proj",
    )(h, mods, g, win, qn, kvn, wuq, wukv, cos, sin)


def _mla_kernel(q_ref, k_ref, v_ref, o_ref, *, nct_q, lc):
    j = pl.program_id(2)
    s = _dot_nt(q_ref[0], k_ref[0])
    col = lax.broadcasted_iota(jnp.int32, s.shape, 1)
    limit = jnp.where(j < nct_q, lc, s.shape[1])
    s = jnp.where(col < limit, s, NEG_INF)
    mx = jnp.max(s, axis=-1, keepdims=True)
    p = jnp.exp(s - mx)
    den = jnp.sum(p, axis=-1, keepdims=True)
    o_ref[0] = (_dot(p, v_ref[0]) / den).astype(o_ref.dtype)


def _mla_attention(q, k, v, lc, q_tile0):
    b, l, _ = q.shape
    tq = MLA_Q_TILE
    return pl.pallas_call(
        functools.partial(_mla_kernel, nct_q=lc // tq - q_tile0, lc=lc),
        out_shape=jax.ShapeDtypeStruct((b, l, MLA_HEADS * MLA_V), BF16),
        grid=(b, MLA_HEADS, l // tq - q_tile0),
        in_specs=[pl.BlockSpec((1, tq, V7X_MXU_DIM), lambda i, h, j: (i, j + q_tile0, h)),
                  pl.BlockSpec((1, l, V7X_MXU_DIM), lambda i, h, j: (i, 0, h)),
                  pl.BlockSpec((1, l, MLA_V), lambda i, h, j: (i, 0, h))],
        out_specs=pl.BlockSpec((1, tq, MLA_V), lambda i, h, j: (i, j + q_tile0, h)),
        compiler_params=pltpu.CompilerParams(dimension_semantics=("parallel", "parallel", "parallel"),
                                             vmem_limit_bytes=_vmem_limit(48)),
        name="mla_attention",
    )(q, k, v)


def _swa_kernel(sink_ref, q_ref, k_ref, v_ref, o_ref, *, lc, q_tile0):
    tq = SWA_Q_TILE
    l = k_ref.shape[1]
    r0 = (pl.program_id(1) + q_tile0) * tq
    start = pl.multiple_of(jnp.clip(r0 - WINDOW, lc, l - SWA_BAND), tq)
    rows = SWA_GROUP * tq
    row = lax.broadcasted_iota(jnp.int32, (rows, 1), 0)
    qpos = jnp.where(r0 >= lc, r0, -l) + row % tq
    kpos = start + lax.broadcasted_iota(jnp.int32, (1, SWA_BAND), 1)
    valid = jnp.abs(qpos - kpos) <= WINDOW
    lane = lax.broadcasted_iota(jnp.int32, (1, V7X_MXU_DIM), 1)
    for g in range(SWA_KV_HEADS):
        sl = slice(g * V7X_MXU_DIM, (g + 1) * V7X_MXU_DIM)
        qg = q_ref[0, :, sl]
        zero = jnp.zeros_like(qg)
        head = [(lane // SWA_HEAD_DIM) == hh for hh in range(SWA_GROUP)]
        qstack = jnp.concatenate([jnp.where(head[hh], qg, zero) for hh in range(SWA_GROUP)], axis=0)
        sc = _dot_nt(qstack, k_ref[0, 0:lc, sl])
        sb = jnp.where(valid, _dot_nt(qstack, k_ref[0, pl.ds(start, SWA_BAND), sl]), NEG_INF)
        sk = jnp.zeros((rows, 1), F32)
        for hh in range(SWA_GROUP):
            sk = jnp.where(row // tq == hh, sink_ref[g * SWA_GROUP + hh], sk)
        mx = jnp.maximum(jnp.maximum(jnp.max(sc, axis=-1, keepdims=True), jnp.max(sb, axis=-1, keepdims=True)), sk)
        pc = jnp.exp(sc - mx)
        pb = jnp.exp(sb - mx)
        den = jnp.sum(pc, axis=-1, keepdims=True) + jnp.sum(pb, axis=-1, keepdims=True) + jnp.exp(sk - mx)
        ostack = (_dot(pc, v_ref[0, 0:lc, sl]) + _dot(pb, v_ref[0, pl.ds(start, SWA_BAND), sl])) / den
        o = jnp.zeros((tq, V7X_MXU_DIM), F32)
        for hh in range(SWA_GROUP):
            o = o + jnp.where(head[hh], ostack[hh * tq:(hh + 1) * tq], 0.0)
        o_ref[0, :, sl] = o.astype(o_ref.dtype)


def _swa_attention(sinks, q, k, v, lc, q_tile0):
    b, l, _ = q.shape
    tq = SWA_Q_TILE
    return pl.pallas_call(
        functools.partial(_swa_kernel, lc=lc, q_tile0=q_tile0),
        out_shape=jax.ShapeDtypeStruct((b, l, SWA_HEADS * SWA_HEAD_DIM), BF16),
        grid=(b, l // tq - q_tile0),
        in_specs=[pl.BlockSpec(memory_space=pltpu.SMEM),
                  pl.BlockSpec((1, tq, SWA_HEADS * SWA_HEAD_DIM), lambda i, j: (i, j + q_tile0, 0)),
                  pl.BlockSpec((1, l, _SWA_W), lambda i, j: (i, 0, 0)),
                  pl.BlockSpec((1, l, _SWA_W), lambda i, j: (i, 0, 0))],
        out_specs=pl.BlockSpec((1, tq, SWA_HEADS * SWA_HEAD_DIM), lambda i, j: (i, j + q_tile0, 0)),
        compiler_params=pltpu.CompilerParams(dimension_semantics=("parallel", "parallel"),
                                             vmem_limit_bytes=_vmem_limit(48)),
        name="swa_attention",
    )(sinks, q, k, v)


def _pack_bf16_pair(x):
    w = x.shape[1] // 2
    lo = pltpu.bitcast(x[:, :w].astype(BF16).astype(F32), jnp.int32)
    hi = pltpu.bitcast(x[:, w:].astype(BF16).astype(F32), jnp.int32)
    return lax.shift_right_logical(lo, jnp.int32(16)) | (hi & jnp.int32(-65536))


def _unpack_bf16_pair(p):
    return pltpu.bitcast(p << 16, F32), pltpu.bitcast(p & jnp.int32(-65536), F32)


def _route(n2, wrt, bias, run_ref):
    logits = lax.dot_general(wrt, n2, (((1,), (1,)), ((), ())), precision=HIGHEST, preferred_element_type=F32)
    rows = logits.shape[1]
    shape3 = (N_GROUPS, GROUP_SIZE, rows)
    scores3 = _sigmoid(logits[0:N_EXPERTS]).reshape(shape3)
    choice = scores3 + bias
    ji = lax.broadcasted_iota(jnp.int32, shape3, 1).astype(F32)
    m1 = jnp.max(choice, axis=1, keepdims=True)
    first = jnp.min(jnp.where(choice == m1, ji, float(GROUP_SIZE)), axis=1, keepdims=True)
    m2 = jnp.max(jnp.where(ji == first, -jnp.inf, choice), axis=1, keepdims=True)
    gs = m1 + m2
    gidx = lax.broadcasted_iota(jnp.int32, gs.shape, 0).astype(F32)
    gsel = jnp.zeros_like(gs)
    for _ in range(TOPK_GROUPS):
        mx = jnp.max(gs, axis=0, keepdims=True)
        pick = gidx == jnp.min(jnp.where(gs == mx, gidx, float(N_GROUPS)), axis=0, keepdims=True)
        gsel = jnp.where(pick, 1.0, gsel)
        gs = jnp.where(pick, -jnp.inf, gs)
    cand = jnp.where(gsel > 0.0, choice, -jnp.inf).reshape(N_EXPERTS, rows)
    scores = scores3.reshape(N_EXPERTS, rows)
    ei = lax.broadcasted_iota(jnp.int32, (N_EXPERTS, rows), 0).astype(F32)
    picks = []
    for _ in range(TOP_K):
        mx = jnp.max(cand, axis=0, keepdims=True)
        pick = ei == jnp.min(jnp.where(cand == mx, ei, float(N_EXPERTS)), axis=0, keepdims=True)
        picks.append(pick)
        cand = jnp.where(pick, -jnp.inf, cand)
    esel = jnp.zeros((N_EXPERTS, rows), F32)
    for pick in picks:
        esel = jnp.where(pick, 1.0, esel)
    before = jnp.where(lax.broadcasted_iota(jnp.int32, (rows, rows), 0) < lax.broadcasted_iota(jnp.int32, (rows, rows), 1),
                       1.0, 0.0).astype(BF16)
    slot = jnp.dot(esel.astype(BF16), before, preferred_element_type=F32) + run_ref[...]
    run_ref[...] += jnp.sum(esel, axis=1, keepdims=True)
    sc = [jnp.sum(jnp.where(pick, scores, 0.0), axis=0, keepdims=True) for pick in picks]
    tot = sc[0]
    for x in sc[1:]:
        tot = tot + x
    k8 = lax.broadcasted_iota(jnp.int32, (8, rows), 0)
    kw = lax.broadcasted_iota(jnp.int32, (GATE_W, rows), 0)
    eid = jnp.zeros((8, rows), jnp.int32)
    rank = jnp.zeros((8, rows), jnp.int32)
    wk = jnp.zeros((GATE_W, rows), F32)
    for k, pick in enumerate(picks):
        e_k = jnp.sum(jnp.where(pick, ei, 0.0), axis=0, keepdims=True).astype(jnp.int32)
        r_k = jnp.sum(jnp.where(pick, slot, 0.0), axis=0, keepdims=True).astype(jnp.int32)
        eid = jnp.where(k8 == k, e_k, eid)
        rank = jnp.where(k8 == k, r_k, rank)
        wk = jnp.where(kw == k, sc[k] * (ROUTED_SCALE / tot), wk)
    return eid, rank, wk.T


def _mixer_tail(o, h_ref, m, gffn_ref, wrt_ref, bias_ref, hn_ref, n2_ref, eid_ref, rank_ref, w_ref, cnt_ref, run_ref):
    @pl.when((pl.program_id(0) == 0) & (pl.program_id(1) == 0))
    def _():
        run_ref[...] = jnp.zeros_like(run_ref)

    hn = h_ref[0] + m[2:3] * o
    hn_ref[0] = hn
    n2 = _norm_mod(hn, gffn_ref[...], m[3:4], m[4:5])
    n2_ref[0] = _pack_bf16_pair(n2)
    eid, rank, wcols = _route(n2, wrt_ref[...], bias_ref[...], run_ref)
    eid_ref[0] = eid
    rank_ref[0] = rank
    w_ref[0] = wcols
    cnt_ref[...] = run_ref[...]


def _attn_out_kernel(a_ref, b_ref, h_ref, mods_ref, wo_ref, gffn_ref, wrt_ref, bias_ref,
                     hn_ref, n2_ref, eid_ref, rank_ref, w_ref, cnt_ref, run_ref):
    wa = MLA_HEADS * MLA_V
    o = _dot(a_ref[0], wo_ref[0:wa, :]) + _dot(b_ref[0], wo_ref[wa:, :])
    _mixer_tail(o, h_ref, mods_ref[0, 0], gffn_ref, wrt_ref, bias_ref, hn_ref, n2_ref, eid_ref, rank_ref, w_ref,
                cnt_ref, run_ref)


def _tail_outs(b, l, d):
    tl = TOKEN_TILE
    nt = l // tl
    sds = jax.ShapeDtypeStruct
    tok = lambda w: pl.BlockSpec((1, tl, w), lambda i, j: (i, j, 0))
    blk = pl.BlockSpec((1, 8, tl), lambda i, j: (i * nt + j, 0, 0))
    shapes = [sds((b, l, d), F32), sds((b, l, d // 2), jnp.int32), sds((b * nt, 8, tl), jnp.int32),
              sds((b * nt, 8, tl), jnp.int32), sds((b, l, GATE_W), F32), sds((N_EXPERTS, 1), F32)]
    specs = [tok(d), tok(d // 2), blk, blk, tok(GATE_W), pl.BlockSpec((N_EXPERTS, 1), lambda i, j: (0, 0))]
    return shapes, specs


def _attn_out(a, bm, h, mods, wo, gffn, wrt, bias, nct):
    b, l, d = h.shape
    tl = TOKEN_TILE
    tok = lambda w: pl.BlockSpec((1, tl, w), lambda i, j: (i, j, 0))
    full = lambda x: pl.BlockSpec(x.shape, lambda i, j: (0,) * x.ndim)
    shapes, specs = _tail_outs(b, l, d)
    return pl.pallas_call(
        _attn_out_kernel,
        out_shape=shapes,
        grid=(b, l // tl),
        in_specs=[tok(a.shape[2]), tok(bm.shape[2]), tok(d),
                  pl.BlockSpec((1, 1, N_MODS, d), lambda i, j: (i, jnp.where(j < nct, 0, 1), 0, 0)),
                  full(wo), full(gffn), full(wrt), full(bias)],
        out_specs=specs,
        scratch_shapes=[pltpu.VMEM((N_EXPERTS, 1), F32)],
        compiler_params=pltpu.CompilerParams(dimension_semantics=("arbitrary", "arbitrary"),
                                             vmem_limit_bytes=_vmem_limit(40)),
        name="attn_out",
    )(a, bm, h, mods, wo, gffn, wrt, bias)


def _moe_dest_kernel(off_ref, eid_ref, rank_ref, dest_ref):
    eid = eid_ref[...]
    dest = rank_ref[...]
    for e in range(N_EXPERTS):
        dest = dest + jnp.where(eid == e, off_ref[e], 0)
    dest_ref[...] = dest


def _moe_dest(off, eid, rank):
    return pl.pallas_call(
        _moe_dest_kernel,
        out_shape=jax.ShapeDtypeStruct(eid.shape, jnp.int32),
        in_specs=[pl.BlockSpec(memory_space=pltpu.SMEM),
                  pl.BlockSpec(eid.shape, lambda: (0, 0, 0)), pl.BlockSpec(eid.shape, lambda: (0, 0, 0))],
        out_specs=pl.BlockSpec(eid.shape, lambda: (0, 0, 0)),
        name="moe_dest",
    )(off, eid, rank)


def _sc_mesh():
    return plsc.VectorSubcoreMesh(core_axis_name="c", subcore_axis_name="s",
                                  num_cores=V7X_SC_CORES, num_subcores=V7X_SC_SUBCORES)


def _sc_chunk(rows_per_worker):
    return max(c for c in range(8, SC_MAX_CHUNK + 1, 8) if rows_per_worker % c == 0)


def _sc_dispatch(xp, dest, p_rows):
    t, w = xp.shape
    tpw = t // V7X_SC_WORKERS
    ch = _sc_chunk(tpw)

    @functools.partial(
        pl.kernel, mesh=_sc_mesh(), out_type=jax.ShapeDtypeStruct((p_rows, w), xp.dtype),
        scratch_types=[pltpu.VMEM((ch, w), xp.dtype)] + [pltpu.VMEM((ch,), jnp.int32)] * TOP_K
        + [pltpu.SemaphoreType.DMA, pltpu.SemaphoreType.DMA],
        name="moe_dispatch")
    def run(x_hbm, dest_hbm, out_hbm, rows_v, *rest):
        idx, (sem_i, sem_o) = rest[:TOP_K], rest[TOP_K:]
        base = (lax.axis_index("s") * V7X_SC_CORES + lax.axis_index("c")) * tpw

        @pl.loop(0, tpw // ch)
        def _(i):
            t0 = base + i * ch
            loads = [pltpu.async_copy(dest_hbm.at[k, pl.ds(t0, ch)], idx[k], sem_i) for k in range(TOP_K)]
            pltpu.sync_copy(x_hbm.at[pl.ds(t0, ch)], rows_v)
            for c in loads:
                c.wait()
            puts = [pltpu.async_copy(rows_v, out_hbm.at[idx[k]], sem_o) for k in range(TOP_K)]
            for c in puts:
                c.wait()

    return run(xp, dest)


def _sc_gather(ys, dest, t):
    w = ys.shape[1]
    tpw = t // V7X_SC_WORKERS
    ch = _sc_chunk(tpw)

    @functools.partial(
        pl.kernel, mesh=_sc_mesh(), out_type=jax.ShapeDtypeStruct((TOP_K, t, w), ys.dtype),
        scratch_types=[pltpu.VMEM((ch, w), ys.dtype), pltpu.VMEM((ch,), jnp.int32), pltpu.SemaphoreType.DMA],
        name="moe_gather")
    def run(y_hbm, dest_hbm, out_hbm, rows_v, idx_v, sem):
        base = (lax.axis_index("s") * V7X_SC_CORES + lax.axis_index("c")) * tpw

        @pl.loop(0, tpw // ch)
        def _(i):
            t0 = base + i * ch
            for k in range(TOP_K):
                pltpu.sync_copy(dest_hbm.at[k, pl.ds(t0, ch)], idx_v)
                pltpu.async_copy(y_hbm.at[idx_v], rows_v, sem).wait()
                pltpu.sync_copy(rows_v, out_hbm.at[k, pl.ds(t0, ch)])

    return run(ys, dest)


def _moe_expert_kernel(te_ref, nv_ref, x_ref, wg_ref, wu_ref, wd_ref, y_ref):
    @pl.when(pl.program_id(0) < nv_ref[0])
    def _():
        lo, hi = _unpack_bf16_pair(x_ref[...])
        half = lo.shape[1]
        hg = _dot(lo, wg_ref[0, 0:half, :]) + _dot(hi, wg_ref[0, half:, :])
        hu = _dot(lo, wu_ref[0, 0:half, :]) + _dot(hi, wu_ref[0, half:, :])
        y_ref[...] = _pack_bf16_pair(_dot(_silu(hg) * hu, wd_ref[0]))


def _moe_experts(tile_expert, n_valid, xs, wg, wu, wd):
    p_rows, w = xs.shape
    tm = MOE_ROW_TILE
    _, d, f = wg.shape
    wspec = lambda shp: pl.BlockSpec((1,) + shp, lambda i, te, nv: (te[i], 0, 0))
    return pl.pallas_call(
        _moe_expert_kernel,
        out_shape=jax.ShapeDtypeStruct((p_rows, w), xs.dtype),
        grid_spec=pltpu.PrefetchScalarGridSpec(
            num_scalar_prefetch=2, grid=(p_rows // tm,),
            in_specs=[pl.BlockSpec((tm, w), lambda i, te, nv: (jnp.minimum(i, nv[0] - 1), 0)),
                      wspec((d, f)), wspec((d, f)), wspec((f, d))],
            out_specs=pl.BlockSpec((tm, w), lambda i, te, nv: (i, 0))),
        compiler_params=pltpu.CompilerParams(dimension_semantics=("arbitrary",),
                                             vmem_limit_bytes=_vmem_limit(32)),
        name="moe_experts",
    )(tile_expert, n_valid, xs, wg, wu, wd)


def _moe_combine_kernel(yg_ref, w_ref, xp_ref, sg_ref, su_ref, sd_ref, h_ref, mods_ref, gfin_ref, o_ref, *, final_norm):
    xlo, xhi = _unpack_bf16_pair(xp_ref[0])
    half = xlo.shape[1]
    hs = (_silu(_dot(xlo, sg_ref[0:half, :]) + _dot(xhi, sg_ref[half:, :]))
          * (_dot(xlo, su_ref[0:half, :]) + _dot(xhi, su_ref[half:, :])))
    acc = _dot(hs, sd_ref[...])
    lo = acc[:, :half]
    hi = acc[:, half:]
    w = w_ref[0]
    for k in range(TOP_K):
        ylo, yhi = _unpack_bf16_pair(yg_ref[k, 0])
        wk = w[:, k:k + 1]
        lo = lo + wk * ylo
        hi = hi + wk * yhi
    y = h_ref[0] + mods_ref[0, 0, N_MODS - 1:N_MODS, :] * jnp.concatenate([lo, hi], axis=1)
    if final_norm:
        y = _rms(y, gfin_ref[...])
    o_ref[0] = y


def _moe_combine(yg, wcols, xp, sg, su, sd, h, mods, gfin, nct, final_norm):
    b, l, d = h.shape
    tl = TOKEN_TILE
    tok = lambda w: pl.BlockSpec((1, tl, w), lambda i, j: (i, j, 0))
    full = lambda x: pl.BlockSpec(x.shape, lambda i, j: (0,) * x.ndim)
    return pl.pallas_call(
        functools.partial(_moe_combine_kernel, final_norm=final_norm),
        out_shape=jax.ShapeDtypeStruct((b, l, d), F32),
        grid=(b, l // tl),
        in_specs=[pl.BlockSpec((TOP_K, 1, tl, d // 2), lambda i, j: (0, i, j, 0)), tok(GATE_W), tok(d // 2),
                  full(sg), full(su), full(sd), tok(d),
                  pl.BlockSpec((1, 1, N_MODS, d), lambda i, j: (i, jnp.where(j < nct, 0, 1), 0, 0)),
                  full(gfin)],
        out_specs=tok(d),
        compiler_params=pltpu.CompilerParams(dimension_semantics=("parallel", "parallel"),
                                             vmem_limit_bytes=_vmem_limit(40)),
        name="moe_combine",
    )(yg, wcols, xp, sg, su, sd, h, mods, gfin)


def _moe_sparse(n2p, eid, rank, wcols, counts, h, mods, wg, wu, wd, sg, su, sd, gfin, nct, final_norm):
    b, l, d = h.shape
    t = b * l
    tm = MOE_ROW_TILE
    n_tiles = -(-(TOP_K * t + N_EXPERTS * (tm - 1)) // tm)
    tiles_e = (counts.reshape(N_EXPERTS).astype(jnp.int32) + (tm - 1)) // tm
    tile_end = jnp.cumsum(tiles_e)
    off = (tile_end - tiles_e) * tm
    n_valid = tile_end[-1:]
    tile_expert = jnp.searchsorted(tile_end, jnp.minimum(jnp.arange(n_tiles, dtype=jnp.int32), n_valid - 1),
                                   side="right").astype(jnp.int32)
    dest = _moe_dest(off, eid, rank).transpose(1, 0, 2).reshape(8, t)
    xs = _sc_dispatch(n2p.reshape(t, d // 2), dest, n_tiles * tm)
    ys = _moe_experts(tile_expert, n_valid, xs, wg, wu, wd)
    yg = _sc_gather(ys, dest, t).reshape(TOP_K, b, l, d // 2)
    return _moe_combine(yg, wcols, n2p, sg, su, sd, h, mods, gfin, nct, final_norm)


def _rwkv_proj_kernel(h_ref, hp_ref, hx_ref, mods_ref, g_ref, mu_ref, wr_ref, wk_ref, wv_ref, g1_ref, g2_ref,
                      w1_ref, w2_ref, a1_ref, a2_ref, w0_ref, a0_ref, kk_ref, ka_ref, rk_ref, bd_ref,
                      r_out, v_out, kk_out, g_out, km_out, b_out, lw_out, bonus_out, *, nct):
    j = pl.program_id(1)
    nt = pl.num_programs(1)
    m = mods_ref[0, 0]
    g = g_ref[...]
    n = _norm_mod(h_ref[0], g, m[0:1], m[1:2])
    tl, d = n.shape
    seg_first = (j == 0) | (j == nct)
    seg_last = (j == nct - 1) | (j == nt - 1)
    n_prev = _norm_mod(hp_ref[0], g, m[0:1], m[1:2])[7:8] * jnp.where(seg_first, 0.0, 1.0)
    n_next = _norm_mod(hx_ref[0], g, m[0:1], m[1:2])[0:1] * jnp.where(seg_last, 0.0, 1.0)
    row = lax.broadcasted_iota(jnp.int32, (tl, 1), 0)
    prev = jnp.where(row == 0, n_prev, pltpu.roll(n, 1, axis=0))
    nxt = jnp.where(row == tl - 1, n_next, pltpu.roll(n, tl - 1, axis=0))
    lane = lax.broadcasted_iota(jnp.int32, (1, d), 1)
    xx = jnp.where(lane < d // 2, prev, nxt) - n
    mu = mu_ref[...]
    xr, xw, xk, xv, xa, xg = [n + xx * mu[i:i + 1] for i in range(6)]
    r = _dot(xr, wr_ref[...])
    k = _dot(xk, wk_ref[...])
    v = _dot(xv, wv_ref[...])
    g_out[0] = _dot(_sigmoid(_dot(xg, g1_ref[...])), g2_ref[...]).astype(g_out.dtype)
    tw = jnp.tanh(_dot(xw, w1_ref[...]))
    ta = _dot(xa, a1_ref[...])
    bd = bd_ref[...]
    kk = k * kk_ref[...]
    kk = kk / jnp.maximum(jnp.sqrt(_head_sum(kk * kk, bd)), 1e-12)
    r_out[0] = r.astype(r_out.dtype)
    v_out[0] = v.astype(v_out.dtype)
    kk_out[0] = kk.astype(kk_out.dtype)
    bonus = jnp.zeros_like(v)
    for dr in range(2):
        zw = w0_ref[dr:dr + 1, :] + _dot(tw, w2_ref[dr])
        lw_out[dr, 0] = -jnp.exp(-0.5) * _sigmoid(zw)
        a = _sigmoid(a0_ref[dr:dr + 1, :] + _dot(ta, a2_ref[dr]))
        km = k * (1.0 + (a - 1.0) * ka_ref[...])
        km_out[dr, 0] = km.astype(km_out.dtype)
        b_out[dr, 0] = (kk * a).astype(b_out.dtype)
        bonus = bonus + _head_sum(r * km * rk_ref[...], bd) * v
    bonus_out[0] = bonus


def _rwkv_proj(h, mods, g, mu, wr, wk, wv, g1, g2, w1, w2, a1, a2, w0, a0, kk, ka, rk, bd, nct):
    b, l, d = h.shape
    tl = TOKEN_TILE
    nb8 = l // 8
    tok = pl.BlockSpec((1, tl, d), lambda i, j: (i, j, 0))
    tok2 = pl.BlockSpec((2, 1, tl, d), lambda i, j: (0, i, j, 0))
    full = lambda x: pl.BlockSpec(x.shape, lambda i, j: (0,) * x.ndim)
    sds = jax.ShapeDtypeStruct
    return pl.pallas_call(
        functools.partial(_rwkv_proj_kernel, nct=nct),
        out_shape=[sds((b, l, d), BF16), sds((b, l, d), BF16), sds((b, l, d), BF16), sds((b, l, d), BF16),
                   sds((2, b, l, d), BF16), sds((2, b, l, d), BF16), sds((2, b, l, d), F32), sds((b, l, d), F32)],
        grid=(b, l // tl),
        in_specs=[tok,
                  pl.BlockSpec((1, 8, d), lambda i, j: (i, jnp.maximum(j * (tl // 8) - 1, 0), 0)),
                  pl.BlockSpec((1, 8, d), lambda i, j: (i, jnp.minimum((j + 1) * (tl // 8), nb8 - 1), 0)),
                  pl.BlockSpec((1, 1, N_MODS, d), lambda i, j: (i, jnp.where(j < nct, 0, 1), 0, 0)),
                  full(g), full(mu), full(wr), full(wk), full(wv), full(g1), full(g2), full(w1), full(w2),
                  full(a1), full(a2), full(w0), full(a0), full(kk), full(ka), full(rk), full(bd)],
        out_specs=[tok, tok, tok, tok, tok2, tok2, tok2, tok],
        compiler_params=pltpu.CompilerParams(dimension_semantics=("parallel", "parallel"),
                                             vmem_limit_bytes=_vmem_limit(56)),
        name="rwkv_proj",
    )(h, h, h, mods, g, mu, wr, wk, wv, g1, g2, w1, w2, a1, a2, w0, a0, kk, ka, rk, bd)


def _wkv_kernel(r_ref, v_ref, kk_ref, km_ref, b_ref, lw_ref, y_ref, st_ref):
    c = WKV_CHUNK
    w = WKV_PAIR
    rev = pl.program_id(0)
    sign = 1 - 2 * rev

    @pl.when(pl.program_id(2) == 0)
    def _():
        st_ref[...] = jnp.zeros_like(st_ref)

    ti = lax.broadcasted_iota(jnp.int32, (c, c), 0)
    si = lax.broadcasted_iota(jnp.int32, (c, c), 1)
    tri = jnp.where((si - ti) * sign <= 0, 1.0, 0.0).astype(F32)
    lw = lw_ref[0, 0]
    l_incl = jnp.dot(tri, lw, precision=HIGHEST, preferred_element_type=F32)
    l_tot = jnp.sum(lw, axis=0, keepdims=True)
    mid = 0.5 * l_tot
    e_pos = jnp.exp(l_incl - mid)
    e_neg = jnp.exp(mid - l_incl)
    e_mid = jnp.exp(mid)
    rt = r_ref[0].astype(F32) * e_pos
    kt = kk_ref[0].astype(F32) * jnp.exp(l_incl - lw - mid)
    kh = km_ref[0, 0].astype(F32) * e_neg
    bh = b_ref[0, 0].astype(F32) * e_neg
    v = v_ref[0]

    ri = lax.broadcasted_iota(jnp.int32, (w, w), 0)
    ci = lax.broadcasted_iota(jnp.int32, (w, w), 1)
    same = (ri // c) == (ci // c)
    dlt = (ci % c - ri % c) * sign
    strict = same & (dlt < 0)
    incl = same & (dlt <= 0)
    eye = jnp.where(ri == ci, 1.0, 0.0).astype(F32)
    lane = lax.broadcasted_iota(jnp.int32, (1, w), 1)
    h0 = lane < RWKV_HEAD

    def rows2(x):
        return jnp.concatenate([jnp.where(h0, x, 0.0), jnp.where(h0, 0.0, x)], axis=0)

    def fold(x):
        return x[:c] + x[c:]

    pairs = range(st_ref.shape[0])
    sls = [slice(p * w, (p + 1) * w) for p in pairs]
    em = [e_mid[:, s] for s in sls]
    g = [_dot_nt(jnp.concatenate([rows2(kt[:, s]), rows2(rt[:, s])], axis=0),
                 jnp.concatenate([kh[:, s], kh[:, s], bh[:, s], bh[:, s]], axis=0)) for s in sls]
    a_kk = [jnp.where(strict, x[:w, :w], 0.0) for x in g]
    nmat = [jnp.where(strict, x[:w, w:], 0.0) for x in g]
    a_rk = [jnp.where(incl, x[w:, :w], 0.0) for x in g]
    a_rb = [jnp.where(incl, x[w:, w:], 0.0) for x in g]
    v32 = [v[:, s].astype(F32) for s in sls]
    v_rows = [rows2(x) for x in v32]
    r_pre = [_dot(a_kk[p], v_rows[p]) for p in pairs]
    tinv = [eye - x for x in nmat]
    npow = nmat
    for _ in range(c.bit_length() - 2):
        npow = [_dot(x, x) for x in npow]
        tinv = [tinv[p] + _dot(tinv[p], npow[p]) for p in pairs]
    sol = [_dot(tinv[p], jnp.concatenate([r_pre[p], rows2(kt[:, sls[p]] * em[p])], axis=1)) for p in pairs]
    u_rows = [x[:, :w] for x in sol]
    kq_rows = [x[:, w:] for x in sol]
    y_pre = [fold(_dot(jnp.concatenate([a_rk[p], -a_rb[p]], axis=1),
                       jnp.concatenate([v_rows[p], u_rows[p]], axis=0))) for p in pairs]
    r_eff = [rt[:, sls[p]] * em[p] - fold(_dot(a_rb[p], kq_rows[p])) for p in pairs]
    bbar = [bh[:, sls[p]] * em[p] for p in pairs]
    kbar = [kh[:, sls[p]] * em[p] for p in pairs]
    eye_dec = [eye * (em[p] * em[p]) for p in pairs]
    mmat = [eye_dec[p] - jnp.where(same, _dot_tn(fold(kq_rows[p]), bbar[p]), 0.0) for p in pairs]
    s_pre = [jnp.where(same, _dot_tn(jnp.concatenate([v32[p], -fold(u_rows[p])], axis=0),
                                     jnp.concatenate([kbar[p], bbar[p]], axis=0)), 0.0) for p in pairs]
    st = [st_ref[p] for p in pairs]
    for p in pairs:
        y_ref[0, 0, :, sls[p]] = _dot_nt(r_eff[p], st[p]) + y_pre[p]
    for p in pairs:
        hi = st[p].astype(BF16)
        lo = (st[p] - hi.astype(F32)).astype(BF16)
        mb = mmat[p].astype(BF16)
        st_ref[p] = (jnp.dot(hi, mb, preferred_element_type=F32) + jnp.dot(lo, mb, preferred_element_type=F32)
                     + s_pre[p])


def _wkv(r, v, kk, km, bv, lw, lc):
    b, l, d = r.shape
    c = WKV_CHUNK
    ncc = lc // c
    nlc = (l - lc) // c

    def chunk(dr, i):
        return jnp.where(dr == 0, i, jnp.where(i < ncc, ncc - 1 - i, nlc + 2 * ncc - 1 - i))

    shared = pl.BlockSpec((1, c, d), lambda dr, bi, i: (bi, chunk(dr, i), 0))
    per_dir = pl.BlockSpec((1, 1, c, d), lambda dr, bi, i: (dr, bi, chunk(dr, i), 0))
    return pl.pallas_call(
        _wkv_kernel,
        out_shape=jax.ShapeDtypeStruct((2, b, l, d), F32),
        grid=(2, b, l // c),
        in_specs=[shared, shared, shared, per_dir, per_dir, per_dir],
        out_specs=per_dir,
        scratch_shapes=[pltpu.VMEM((d // WKV_PAIR, WKV_PAIR, WKV_PAIR), F32)],
        compiler_params=pltpu.CompilerParams(dimension_semantics=("parallel", "parallel", "arbitrary"),
                                             vmem_limit_bytes=_vmem_limit(32)),
        name="wkv7_chunked",
    )(r, v, kk, km, bv, lw)


def _rwkv_out_kernel(y_ref, bonus_ref, g_ref, lnw_ref, lnb_ref, wo_ref, bd_ref, h_ref, mods_ref, gffn_ref,
                     wrt_ref, bias_ref, hn_ref, n2_ref, eid_ref, rank_ref, w_ref, cnt_ref, run_ref):
    y = y_ref[0, 0] + y_ref[1, 0]
    bd = bd_ref[...]
    mean = _head_sum(y, bd) * (1.0 / RWKV_HEAD)
    yc = y - mean
    var = _head_sum(yc * yc, bd) * (1.0 / RWKV_HEAD)
    yn = yc * lax.rsqrt(var + GN_EPS) * lnw_ref[...] + lnb_ref[...]
    out = (yn + bonus_ref[0]) * g_ref[0].astype(F32)
    _mixer_tail(_dot(out, wo_ref[...]), h_ref, mods_ref[0, 0], gffn_ref, wrt_ref, bias_ref, hn_ref, n2_ref,
                eid_ref, rank_ref, w_ref, cnt_ref, run_ref)


def _rwkv_out(y, bonus, g, lnw, lnb, wo, bd, h, mods, gffn, wrt, bias, nct):
    b, l, d = h.shape
    tl = TOKEN_TILE
    tok = lambda w: pl.BlockSpec((1, tl, w), lambda i, j: (i, j, 0))
    full = lambda x: pl.BlockSpec(x.shape, lambda i, j: (0,) * x.ndim)
    shapes, specs = _tail_outs(b, l, d)
    return pl.pallas_call(
        _rwkv_out_kernel,
        out_shape=shapes,
        grid=(b, l // tl),
        in_specs=[pl.BlockSpec((2, 1, tl, d), lambda i, j: (0, i, j, 0)), tok(d), tok(d),
                  full(lnw), full(lnb), full(wo), full(bd), tok(d),
                  pl.BlockSpec((1, 1, N_MODS, d), lambda i, j: (i, jnp.where(j < nct, 0, 1), 0, 0)),
                  full(gffn), full(wrt), full(bias)],
        out_specs=specs,
        scratch_shapes=[pltpu.VMEM((N_EXPERTS, 1), F32)],
        compiler_params=pltpu.CompilerParams(dimension_semantics=("arbitrary", "arbitrary"),
                                             vmem_limit_bytes=_vmem_limit(40)),
        name="rwkv_out",
    )(y, bonus, g, lnw, lnb, wo, bd, h, mods, gffn, wrt, bias)


def _rope_table(n_lat, n_ctx):
    dim = SWA_HEAD_DIM
    nf = dim // 4
    inv = ROPE_THETA ** (-jnp.arange(nf, dtype=F32) / nf)
    row = jnp.repeat(jnp.arange(n_lat // GRID_W, dtype=F32), GRID_W)
    col = jnp.tile(jnp.arange(GRID_W, dtype=F32), n_lat // GRID_W)
    ar = row[:, None] * inv
    ac = col[:, None] * inv
    ang = jnp.concatenate([ar, ar, ac, ac], axis=-1)
    cos = jnp.concatenate([jnp.ones((n_ctx, dim), F32), jnp.cos(ang)], axis=0)
    sin = jnp.concatenate([jnp.zeros((n_ctx, dim), F32), jnp.sin(ang)], axis=0)
    return jnp.tile(cos, (1, 2)), jnp.tile(sin, (1, 2))


def _layout_attn_weights(w_in, w_uq, w_ukv):
    d = w_in.shape[0]
    s0 = MLA_Q_RANK
    s1 = s0 + MLA_KV_RANK
    s2 = s1 + MLA_ROPE
    s3 = s2 + SWA_HEADS * SWA_HEAD_DIM
    s4 = s3 + SWA_KV_HEADS * SWA_HEAD_DIM
    rep = lambda w: jnp.concatenate(
        [jnp.tile(w[:, g * SWA_HEAD_DIM:(g + 1) * SWA_HEAD_DIM], (1, SWA_GROUP)) for g in range(SWA_KV_HEADS)], axis=1)
    win = jnp.concatenate([w_in[:, :s1], w_in[:, s2:s3], rep(w_in[:, s3:s4]), rep(w_in[:, s4:]),
                           w_in[:, s1:s2], jnp.zeros((d, V7X_LANES - MLA_ROPE), w_in.dtype)], axis=1)
    qh = MLA_NOPE + MLA_ROPE
    pad = jnp.zeros((w_uq.shape[0], V7X_MXU_DIM - qh), w_uq.dtype)
    wuq = jnp.concatenate([jnp.concatenate([w_uq[:, h * qh:(h + 1) * qh], pad], axis=1) for h in range(MLA_HEADS)], axis=1)
    kvh = MLA_NOPE + MLA_V
    wukv = jnp.concatenate([w_ukv[:, h * kvh:h * kvh + MLA_NOPE] for h in range(MLA_HEADS)]
                           + [w_ukv[:, h * kvh + MLA_NOPE:(h + 1) * kvh] for h in range(MLA_HEADS)], axis=1)
    return win.astype(BF16), wuq.astype(BF16), wukv.astype(BF16)


def _lora_pair(w_down, w_up):
    rank = w_down.shape[2]
    down = jnp.concatenate([w_down[0], w_down[1]], axis=1)
    z = jnp.zeros((rank, w_up.shape[2]), w_up.dtype)
    up = jnp.stack([jnp.concatenate([w_up[0], z], axis=0), jnp.concatenate([z, w_up[1]], axis=0)], axis=0)
    return down.astype(BF16), up.astype(BF16)


def _head_block_diag():
    i = jnp.arange(V7X_MXU_DIM) // RWKV_HEAD
    return (i[:, None] == i[None, :]).astype(BF16)


def kernel(x, c, ctx, c_ctx, ada_w, ada_b, norm_mix, norm_ffn, norm_final, attn_w_in, attn_q_norm, attn_kv_norm, attn_w_uq, attn_w_ukv, attn_sinks, attn_w_o, rwkv_mu, rwkv_w_r, rwkv_w_k, rwkv_w_v, rwkv_w_o, rwkv_g1, rwkv_g2, rwkv_w0, rwkv_w1, rwkv_w2, rwkv_a0, rwkv_a1, rwkv_a2, rwkv_k_k, rwkv_k_a, rwkv_r_k, rwkv_ln_w, rwkv_ln_b, moe_router, moe_bias, moe_w_gate, moe_w_up, moe_w_down, moe_ws_gate, moe_ws_up, moe_ws_down):
    bsz, s, d = x.shape
    lc = ctx.shape[1]
    l = lc + s
    depth = ada_w.shape[0]
    nct = lc // TOKEN_TILE
    assert lc % TOKEN_TILE == 0 and s % TOKEN_TILE == 0 and s >= SWA_BAND and lc % WKV_CHUNK == 0
    assert d % V7X_MXU_DIM == 0 and WKV_CHUNK * 2 == V7X_LANES
    assert (bsz * l) % (8 * V7X_SC_WORKERS) == 0

    h = jnp.concatenate([ctx, x], axis=1)
    cos, sin = _rope_table(s, lc)
    bd = _head_block_diag()
    rows = -(-(bsz + 1) // 8) * 8
    cc = jnp.concatenate([c, c_ctx[None, :], jnp.zeros((rows - bsz - 1, d), F32)], axis=0)
    row2 = lambda a: a.reshape(1, -1)

    for li in range(depth):
        with_ctx = li < depth - 1
        i = li // 2
        ada = _ada_mods(cc, ada_w[li], ada_b[li])
        mods = jnp.stack([jnp.broadcast_to(ada[bsz].reshape(1, N_MODS, d), (bsz, N_MODS, d)),
                          ada[:bsz].reshape(bsz, N_MODS, d)], axis=1)
        wrt = jnp.concatenate([moe_router[li].T, jnp.zeros((GATE_W - N_EXPERTS, d), F32)], axis=0)
        bias = moe_bias[li].reshape(N_GROUPS, GROUP_SIZE, 1)
        if li % 2 == 0:
            win, wuq, wukv = _layout_attn_weights(attn_w_in[i], attn_w_uq[i], attn_w_ukv[i])
            q, k, v, qs, ks, vs = _attn_proj(h, mods, row2(norm_mix[li]), win, row2(attn_q_norm[i]),
                                             row2(attn_kv_norm[i]), wuq, wukv, cos, sin, nct)
            a = _mla_attention(q, k, v, lc, 0 if with_ctx else lc // MLA_Q_TILE)
            bm = _swa_attention(attn_sinks[i], qs, ks, vs, lc, 0 if with_ctx else lc // SWA_Q_TILE)
            tail = _attn_out(a, bm, h, mods, attn_w_o[i].astype(BF16), row2(norm_ffn[li]), wrt, bias, nct)
        else:
            w1, w2 = _lora_pair(rwkv_w1[i], rwkv_w2[i])
            a1, a2 = _lora_pair(rwkv_a1[i], rwkv_a2[i])
            r, v, kk, g, km, bv, lw, bonus = _rwkv_proj(
                h, mods, row2(norm_mix[li]), rwkv_mu[i], rwkv_w_r[i].astype(BF16), rwkv_w_k[i].astype(BF16),
                rwkv_w_v[i].astype(BF16), rwkv_g1[i].astype(BF16), rwkv_g2[i].astype(BF16), w1, w2, a1, a2,
                rwkv_w0[i], rwkv_a0[i], row2(rwkv_k_k[i]), row2(rwkv_k_a[i]), row2(rwkv_r_k[i]), bd, nct)
            y = _wkv(r, v, kk, km, bv, lw, lc)
            tail = _rwkv_out(y, bonus, g, row2(rwkv_ln_w[i]), row2(rwkv_ln_b[i]), rwkv_w_o[i].astype(BF16),
                             bd, h, mods, row2(norm_ffn[li]), wrt, bias, nct)
        h, n2p, eid, rank, wcols, counts = tail
        h = _moe_sparse(n2p, eid, rank, wcols, counts, h, mods, moe_w_gate[li].astype(BF16),
                        moe_w_up[li].astype(BF16), moe_w_down[li].astype(BF16), moe_ws_gate[li].astype(BF16),
                        moe_ws_up[li].astype(BF16), moe_ws_down[li].astype(BF16), row2(norm_final), nct,
                        li == depth - 1)
    return h[:, lc:]
```

```python
import functools

import jax
import jax.numpy as jnp
from jax import lax
from jax.experimental import pallas as pl
from jax.experimental.pallas import tpu as pltpu
from jax.experimental.pallas import tpu_sc as plsc

F32 = jnp.float32
BF16 = jnp.bfloat16
HIGHEST = lax.Precision.HIGHEST

GRID_W = 64
NORM_EPS = 1e-6
ROPE_THETA = 10000.0
NEG_INF = -1e30
N_MODS = 6

MLA_HEADS = 4
MLA_Q_RANK = 384
MLA_KV_RANK = 256
MLA_NOPE = 128
MLA_ROPE = 64
MLA_V = 128

SWA_HEADS = 8
SWA_KV_HEADS = 2
SWA_GROUP = SWA_HEADS // SWA_KV_HEADS
SWA_HEAD_DIM = 64
WINDOW = 128

RWKV_HEAD = 64
DECAY_LORA = 64
ICLR_LORA = 64
GATE_LORA = 128
GN_EPS = 64e-5

N_EXPERTS = 64
TOP_K = 6
N_GROUPS = 8
TOPK_GROUPS = 4
GROUP_SIZE = N_EXPERTS // N_GROUPS
ROUTED_SCALE = 2.5
GATE_W = 128

V7X_LANES = 128
V7X_MXU_DIM = 256
V7X_VMEM_BYTES = 64 * 1024 * 1024
V7X_SC_CORES = 2
V7X_SC_SUBCORES = 16
V7X_SC_WORKERS = V7X_SC_CORES * V7X_SC_SUBCORES

TOKEN_TILE = 256
MLA_Q_TILE = 256
MLA_HEADS_PER_STEP = 2
LOG2E = 1.4426950408889634
SWA_Q_TILE = 128
SWA_BAND = SWA_Q_TILE + 2 * WINDOW
WKV_CHUNK = 64
WKV_PAIR = 2 * RWKV_HEAD
MOE_ROW_TILE = 512
SC_MAX_CHUNK = 64


def _vmem_limit(mib):
    return min(mib * 1024 * 1024, V7X_VMEM_BYTES - 4 * 1024 * 1024)


def _dot(a, b):
    return jnp.dot(a.astype(BF16), b.astype(BF16), preferred_element_type=F32)


def _dot_nt(a, b):
    return lax.dot_general(a.astype(BF16), b.astype(BF16), (((1,), (1,)), ((), ())),
                           preferred_element_type=F32)


def _dot_tn(a, b):
    return lax.dot_general(a.astype(BF16), b.astype(BF16), (((0,), (0,)), ((), ())),
                           preferred_element_type=F32)


def _sigmoid(x):
    return 1.0 / (1.0 + jnp.exp(-x))


def _silu(x):
    return x * _sigmoid(x)


def _rms(x, g):
    return x * lax.rsqrt(jnp.mean(x * x, axis=-1, keepdims=True) + NORM_EPS) * g


def _norm_mod(x, g, shift, scale):
    return _rms(x, g) * (1.0 + scale) + shift


def _split_dot(x, w):
    hi = x.astype(BF16)
    lo = (x - hi.astype(F32)).astype(BF16)
    return (jnp.dot(hi, w, preferred_element_type=F32) + jnp.dot(lo, w, preferred_element_type=F32))


def _head_sum(x, bd):
    w = bd.shape[0]
    parts = [_split_dot(x[:, c * w:(c + 1) * w], bd) for c in range(x.shape[1] // w)]
    return jnp.concatenate(parts, axis=1)


def _ada_kernel(c_ref, w_ref, b_ref, o_ref):
    s = _silu(c_ref[...])
    o_ref[...] = jnp.dot(s, w_ref[...], precision=HIGHEST, preferred_element_type=F32) + b_ref[...]


def _ada_mods(cc, w, b):
    rows, d = cc.shape
    n = w.shape[1]
    return pl.pallas_call(
        _ada_kernel,
        out_shape=jax.ShapeDtypeStruct((rows, n), F32),
        grid=(n // d,),
        in_specs=[pl.BlockSpec((rows, d), lambda i: (0, 0)),
                  pl.BlockSpec((d, d), lambda i: (0, i)),
                  pl.BlockSpec((1, d), lambda i: (0, i))],
        out_specs=pl.BlockSpec((rows, d), lambda i: (0, i)),
        compiler_params=pltpu.CompilerParams(dimension_semantics=("parallel",),
                                             vmem_limit_bytes=_vmem_limit(32)),
        name="ada_mods",
    )(cc, w, b.reshape(1, n))


def _rope128(x, cos, sin, first_half):
    rot = jnp.where(first_half, -pltpu.roll(x, V7X_LANES - 16, axis=1), pltpu.roll(x, 16, axis=1))
    return x * cos + rot * sin


_C_CQ = 0
_C_CKV = _C_CQ + MLA_Q_RANK
_C_QS = _C_CKV + MLA_KV_RANK
_C_KS = _C_QS + SWA_HEADS * SWA_HEAD_DIM
_C_VS = _C_KS + SWA_KV_HEADS * V7X_MXU_DIM
_C_KR = _C_VS + SWA_KV_HEADS * V7X_MXU_DIM
_C_END = _C_KR + V7X_LANES
_SWA_W = SWA_KV_HEADS * V7X_MXU_DIM
_MLA_QK_W = MLA_HEADS * V7X_MXU_DIM


def _attn_proj_kernel(h_ref, mods_ref, g_ref, win_ref, qn_ref, kvn_ref, wuq_ref, wuk_ref, wuvt_ref, cos_ref, sin_ref,
                      q_ref, k_ref, vt_ref, qs_ref, ks_ref, vs_ref):
    m = mods_ref[0, 0]
    n = _norm_mod(h_ref[0], g_ref[...], m[0:1], m[1:2])
    u = _dot(n, win_ref[...])
    cos = cos_ref[...]
    sin = sin_ref[...]
    lane = lax.broadcasted_iota(jnp.int32, (1, V7X_LANES), 1)
    first_half = (lane % 32) < 16

    def rope(x):
        return _rope128(x, cos, sin, first_half)

    scale_a = (MLA_NOPE + MLA_ROPE) ** -0.5 * LOG2E
    scale_b = SWA_HEAD_DIM ** -0.5
    q = _dot(_rms(u[:, _C_CQ:_C_CKV], qn_ref[...]), wuq_ref[...])
    ckv = _rms(u[:, _C_CKV:_C_QS], kvn_ref[...])
    kn = _dot(ckv, wuk_ref[...])
    vt_ref[0] = _dot_nt(wuvt_ref[...], ckv).astype(BF16)
    kr = rope(u[:, _C_KR:_C_END]).astype(BF16)
    for h in range(MLA_HEADS):
        o = h * V7X_MXU_DIM
        q_ref[0, :, o:o + V7X_LANES] = (q[:, o:o + V7X_LANES] * scale_a).astype(BF16)
        q_ref[0, :, o + V7X_LANES:o + V7X_MXU_DIM] = (rope(q[:, o + V7X_LANES:o + V7X_MXU_DIM]) * scale_a).astype(BF16)
        k_ref[0, :, o:o + V7X_LANES] = kn[:, h * MLA_NOPE:(h + 1) * MLA_NOPE].astype(BF16)
        k_ref[0, :, o + V7X_LANES:o + V7X_MXU_DIM] = kr
    for c in range((_C_KS - _C_QS) // V7X_LANES):
        o = c * V7X_LANES
        qs_ref[0, :, o:o + V7X_LANES] = (rope(u[:, _C_QS + o:_C_QS + o + V7X_LANES]) * scale_b).astype(BF16)
    for c in range(_SWA_W // V7X_LANES):
        o = c * V7X_LANES
        ks_ref[0, :, o:o + V7X_LANES] = rope(u[:, _C_KS + o:_C_KS + o + V7X_LANES]).astype(BF16)
    vs_ref[0] = u[:, _C_VS:_C_KR].astype(BF16)


def _attn_proj(h, mods, g, win, qn, kvn, wuq, wuk, wuvt, cos, sin, nct):
    b, l, d = h.shape
    tl = TOKEN_TILE
    tok = lambda w: pl.BlockSpec((1, tl, w), lambda i, j: (i, j, 0))
    full = lambda a: pl.BlockSpec(a.shape, lambda i, j: (0,) * a.ndim)
    sds = jax.ShapeDtypeStruct
    dv = MLA_HEADS * MLA_V
    return pl.pallas_call(
        _attn_proj_kernel,
        out_shape=[sds((b, l, _MLA_QK_W), BF16), sds((b, l, _MLA_QK_W), BF16), sds((b, dv, l), BF16),
                   sds((b, l, SWA_HEADS * SWA_HEAD_DIM), BF16), sds((b, l, _SWA_W), BF16), sds((b, l, _SWA_W), BF16)],
        grid=(b, l // tl),
        in_specs=[tok(d),
                  pl.BlockSpec((1, 1, N_MODS, d), lambda i, j: (i, jnp.where(j < nct, 0, 1), 0, 0)),
                  full(g), full(win), full(qn), full(kvn), full(wuq), full(wuk), full(wuvt),
                  pl.BlockSpec((tl, V7X_LANES), lambda i, j: (j, 0)),
                  pl.BlockSpec((tl, V7X_LANES), lambda i, j: (j, 0))],
        out_specs=[tok(_MLA_QK_W), tok(_MLA_QK_W), pl.BlockSpec((1, dv, tl), lambda i, j: (i, 0, j)),
                   tok(SWA_HEADS * SWA_HEAD_DIM), tok(_SWA_W), tok(_SWA_W)],
        compiler_params=pltpu.CompilerParams(dimension_semantics=("parallel", "parallel"),
                                             vmem_limit_bytes=_vmem_limit(48)),
        name="attn_proj",
    )(h, mods, g, win, qn, kvn, wuq, wuk, wuvt, cos, sin)


def _mla_kernel(q_ref, k_ref, vt_ref, o_ref, *, nct_q, lc):
    hw = V7X_MXU_DIM

    def attend(nk):
        st = [_dot_nt(k_ref[0, 0:nk, hh * hw:(hh + 1) * hw], q_ref[0, :, hh * hw:(hh + 1) * hw])
              for hh in range(MLA_HEADS_PER_STEP)]
        for hh, s in enumerate(st):
            p = jnp.exp2(s - jnp.max(s, axis=0, keepdims=True))
            den = jnp.sum(p, axis=0, keepdims=True)
            ot = _dot(vt_ref[0, hh * MLA_V:(hh + 1) * MLA_V, 0:nk], p) / den
            o_ref[0, :, hh * MLA_V:(hh + 1) * MLA_V] = ot.T.astype(o_ref.dtype)

    @pl.when(pl.program_id(2) < nct_q)
    def _():
        attend(lc)

    @pl.when(pl.program_id(2) >= nct_q)
    def _():
        attend(k_ref.shape[1])


def _mla_attention(q, k, vt, lc, q_tile0):
    b, l, _ = q.shape
    tq = MLA_Q_TILE
    hps = MLA_HEADS_PER_STEP
    return pl.pallas_call(
        functools.partial(_mla_kernel, nct_q=lc // tq - q_tile0, lc=lc),
        out_shape=jax.ShapeDtypeStruct((b, l, MLA_HEADS * MLA_V), BF16),
        grid=(b, MLA_HEADS // hps, l // tq - q_tile0),
        in_specs=[pl.BlockSpec((1, tq, hps * V7X_MXU_DIM), lambda i, h, j: (i, j + q_tile0, h)),
                  pl.BlockSpec((1, l, hps * V7X_MXU_DIM), lambda i, h, j: (i, 0, h)),
                  pl.BlockSpec((1, hps * MLA_V, l), lambda i, h, j: (i, h, 0))],
        out_specs=pl.BlockSpec((1, tq, hps * MLA_V), lambda i, h, j: (i, j + q_tile0, h)),
        compiler_params=pltpu.CompilerParams(dimension_semantics=("parallel", "parallel", "parallel"),
                                             vmem_limit_bytes=_vmem_limit(48)),
        name="mla_attention",
    )(q, k, vt)


def _swa_kernel(sink_ref, q_ref, k_ref, v_ref, o_ref, *, lc, q_tile0):
    tq = SWA_Q_TILE
    l = k_ref.shape[1]
    r0 = (pl.program_id(1) + q_tile0) * tq
    start = pl.multiple_of(jnp.clip(r0 - WINDOW, lc, l - SWA_BAND), tq)
    rows = SWA_GROUP * tq
    row = lax.broadcasted_iota(jnp.int32, (rows, 1), 0)
    qpos = jnp.where(r0 >= lc, r0, -l) + row % tq
    kpos = start + lax.broadcasted_iota(jnp.int32, (1, SWA_BAND), 1)
    valid = jnp.abs(qpos - kpos) <= WINDOW
    lane = lax.broadcasted_iota(jnp.int32, (1, V7X_MXU_DIM), 1)
    for g in range(SWA_KV_HEADS):
        sl = slice(g * V7X_MXU_DIM, (g + 1) * V7X_MXU_DIM)
        qg = q_ref[0, :, sl]
        zero = jnp.zeros_like(qg)
        head = [(lane // SWA_HEAD_DIM) == hh for hh in range(SWA_GROUP)]
        qstack = jnp.concatenate([jnp.where(head[hh], qg, zero) for hh in range(SWA_GROUP)], axis=0)
        sc = _dot_nt(qstack, k_ref[0, 0:lc, sl])
        sb = jnp.where(valid, _dot_nt(qstack, k_ref[0, pl.ds(start, SWA_BAND), sl]), NEG_INF)
        sk = jnp.zeros((rows, 1), F32)
        for hh in range(SWA_GROUP):
            sk = jnp.where(row // tq == hh, sink_ref[g * SWA_GROUP + hh], sk)
        mx = jnp.maximum(jnp.maximum(jnp.max(sc, axis=-1, keepdims=True), jnp.max(sb, axis=-1, keepdims=True)), sk)
        pc = jnp.exp(sc - mx)
        pb = jnp.exp(sb - mx)
        den = jnp.sum(pc, axis=-1, keepdims=True) + jnp.sum(pb, axis=-1, keepdims=True) + jnp.exp(sk - mx)
        ostack = (_dot(pc, v_ref[0, 0:lc, sl]) + _dot(pb, v_ref[0, pl.ds(start, SWA_BAND), sl])) / den
        o = jnp.zeros((tq, V7X_MXU_DIM), F32)
        for hh in range(SWA_GROUP):
            o = o + jnp.where(head[hh], ostack[hh * tq:(hh + 1) * tq], 0.0)
        o_ref[0, :, sl] = o.astype(o_ref.dtype)


def _swa_attention(sinks, q, k, v, lc, q_tile0):
    b, l, _ = q.shape
    tq = SWA_Q_TILE
    return pl.pallas_call(
        functools.partial(_swa_kernel, lc=lc, q_tile0=q_tile0),
        out_shape=jax.ShapeDtypeStruct((b, l, SWA_HEADS * SWA_HEAD_DIM), BF16),
        grid=(b, l // tq - q_tile0),
        in_specs=[pl.BlockSpec(memory_space=pltpu.SMEM),
                  pl.BlockSpec((1, tq, SWA_HEADS * SWA_HEAD_DIM), lambda i, j: (i, j + q_tile0, 0)),
                  pl.BlockSpec((1, l, _SWA_W), lambda i, j: (i, 0, 0)),
                  pl.BlockSpec((1, l, _SWA_W), lambda i, j: (i, 0, 0))],
        out_specs=pl.BlockSpec((1, tq, SWA_HEADS * SWA_HEAD_DIM), lambda i, j: (i, j + q_tile0, 0)),
        compiler_params=pltpu.CompilerParams(dimension_semantics=("parallel", "parallel"),
                                             vmem_limit_bytes=_vmem_limit(48)),
        name="swa_attention",
    )(sinks, q, k, v)


def _pack_bf16_pair(x):
    w = x.shape[1] // 2
    lo = pltpu.bitcast(x[:, :w].astype(BF16).astype(F32), jnp.int32)
    hi = pltpu.bitcast(x[:, w:].astype(BF16).astype(F32), jnp.int32)
    return lax.shift_right_logical(lo, jnp.int32(16)) | (hi & jnp.int32(-65536))


def _unpack_bf16_pair(p):
    return pltpu.bitcast(p << 16, F32), pltpu.bitcast(p & jnp.int32(-65536), F32)


def _route(n2, wrt, bias, run_ref):
    logits = lax.dot_general(wrt, n2, (((1,), (1,)), ((), ())), precision=HIGHEST, preferred_element_type=F32)
    rows = logits.shape[1]
    shape3 = (N_GROUPS, GROUP_SIZE, rows)
    scores3 = _sigmoid(logits[0:N_EXPERTS]).reshape(shape3)
    choice = scores3 + bias
    ji = lax.broadcasted_iota(jnp.int32, shape3, 1).astype(F32)
    m1 = jnp.max(choice, axis=1, keepdims=True)
    first = jnp.min(jnp.where(choice == m1, ji, float(GROUP_SIZE)), axis=1, keepdims=True)
    m2 = jnp.max(jnp.where(ji == first, -jnp.inf, choice), axis=1, keepdims=True)
    gs = m1 + m2
    gidx = lax.broadcasted_iota(jnp.int32, gs.shape, 0).astype(F32)
    gsel = jnp.zeros_like(gs)
    for _ in range(TOPK_GROUPS):
        mx = jnp.max(gs, axis=0, keepdims=True)
        pick = gidx == jnp.min(jnp.where(gs == mx, gidx, float(N_GROUPS)), axis=0, keepdims=True)
        gsel = jnp.where(pick, 1.0, gsel)
        gs = jnp.where(pick, -jnp.inf, gs)
    cand = jnp.where(gsel > 0.0, choice, -jnp.inf).reshape(N_EXPERTS, rows)
    scores = scores3.reshape(N_EXPERTS, rows)
    ei = lax.broadcasted_iota(jnp.int32, (N_EXPERTS, rows), 0).astype(F32)
    picks = []
    for _ in range(TOP_K):
        mx = jnp.max(cand, axis=0, keepdims=True)
        pick = ei == jnp.min(jnp.where(cand == mx, ei, float(N_EXPERTS)), axis=0, keepdims=True)
        picks.append(pick)
        cand = jnp.where(pick, -jnp.inf, cand)
    esel = jnp.zeros((N_EXPERTS, rows), F32)
    for pick in picks:
        esel = jnp.where(pick, 1.0, esel)
    before = jnp.where(lax.broadcasted_iota(jnp.int32, (rows, rows), 0) < lax.broadcasted_iota(jnp.int32, (rows, rows), 1),
                       1.0, 0.0).astype(BF16)
    slot = jnp.dot(esel.astype(BF16), before, preferred_element_type=F32) + run_ref[...]
    run_ref[...] += jnp.sum(esel, axis=1, keepdims=True)
    sc = [jnp.sum(jnp.where(pick, scores, 0.0), axis=0, keepdims=True) for pick in picks]
    tot = sc[0]
    for x in sc[1:]:
        tot = tot + x
    k8 = lax.broadcasted_iota(jnp.int32, (8, rows), 0)
    kw = lax.broadcasted_iota(jnp.int32, (GATE_W, rows), 0)
    eid = jnp.zeros((8, rows), jnp.int32)
    rank = jnp.zeros((8, rows), jnp.int32)
    wk = jnp.zeros((GATE_W, rows), F32)
    for k, pick in enumerate(picks):
        e_k = jnp.sum(jnp.where(pick, ei, 0.0), axis=0, keepdims=True).astype(jnp.int32)
        r_k = jnp.sum(jnp.where(pick, slot, 0.0), axis=0, keepdims=True).astype(jnp.int32)
        eid = jnp.where(k8 == k, e_k, eid)
        rank = jnp.where(k8 == k, r_k, rank)
        wk = jnp.where(kw == k, sc[k] * (ROUTED_SCALE / tot), wk)
    return eid, rank, wk.T


def _mixer_tail(o, h_ref, m, gffn_ref, wrt_ref, bias_ref, hn_ref, n2_ref, eid_ref, rank_ref, w_ref, cnt_ref, run_ref):
    @pl.when((pl.program_id(0) == 0) & (pl.program_id(1) == 0))
    def _():
        run_ref[...] = jnp.zeros_like(run_ref)

    hn = h_ref[0] + m[2:3] * o
    hn_ref[0] = hn
    n2 = _norm_mod(hn, gffn_ref[...], m[3:4], m[4:5])
    n2_ref[0] = _pack_bf16_pair(n2)
    eid, rank, wcols = _route(n2, wrt_ref[...], bias_ref[...], run_ref)
    eid_ref[0] = eid
    rank_ref[0] = rank
    w_ref[0] = wcols
    cnt_ref[...] = run_ref[...]


def _attn_out_kernel(a_ref, b_ref, h_ref, mods_ref, wo_ref, gffn_ref, wrt_ref, bias_ref,
                     hn_ref, n2_ref, eid_ref, rank_ref, w_ref, cnt_ref, run_ref):
    wa = MLA_HEADS * MLA_V
    o = _dot(a_ref[0], wo_ref[0:wa, :]) + _dot(b_ref[0], wo_ref[wa:, :])
    _mixer_tail(o, h_ref, mods_ref[0, 0], gffn_ref, wrt_ref, bias_ref, hn_ref, n2_ref, eid_ref, rank_ref, w_ref,
                cnt_ref, run_ref)


def _tail_outs(b, l, d):
    tl = TOKEN_TILE
    nt = l // tl
    sds = jax.ShapeDtypeStruct
    tok = lambda w: pl.BlockSpec((1, tl, w), lambda i, j: (i, j, 0))
    blk = pl.BlockSpec((1, 8, tl), lambda i, j: (i * nt + j, 0, 0))
    shapes = [sds((b, l, d), F32), sds((b, l, d // 2), jnp.int32), sds((b * nt, 8, tl), jnp.int32),
              sds((b * nt, 8, tl), jnp.int32), sds((b, l, GATE_W), F32), sds((N_EXPERTS, 1), F32)]
    specs = [tok(d), tok(d // 2), blk, blk, tok(GATE_W), pl.BlockSpec((N_EXPERTS, 1), lambda i, j: (0, 0))]
    return shapes, specs


def _attn_out(a, bm, h, mods, wo, gffn, wrt, bias, nct):
    b, l, d = h.shape
    tl = TOKEN_TILE
    tok = lambda w: pl.BlockSpec((1, tl, w), lambda i, j: (i, j, 0))
    full = lambda x: pl.BlockSpec(x.shape, lambda i, j: (0,) * x.ndim)
    shapes, specs = _tail_outs(b, l, d)
    return pl.pallas_call(
        _attn_out_kernel,
        out_shape=shapes,
        grid=(b, l // tl),
        in_specs=[tok(a.shape[2]), tok(bm.shape[2]), tok(d),
                  pl.BlockSpec((1, 1, N_MODS, d), lambda i, j: (i, jnp.where(j < nct, 0, 1), 0, 0)),
                  full(wo), full(gffn), full(wrt), full(bias)],
        out_specs=specs,
        scratch_shapes=[pltpu.VMEM((N_EXPERTS, 1), F32)],
        compiler_params=pltpu.CompilerParams(dimension_semantics=("arbitrary", "arbitrary"),
                                             vmem_limit_bytes=_vmem_limit(40)),
        name="attn_out",
    )(a, bm, h, mods, wo, gffn, wrt, bias)


def _moe_dest_kernel(off_ref, eid_ref, rank_ref, dest_ref):
    eid = eid_ref[...]
    dest = rank_ref[...]
    for e in range(N_EXPERTS):
        dest = dest + jnp.where(eid == e, off_ref[e], 0)
    dest_ref[...] = dest


def _moe_dest(off, eid, rank):
    return pl.pallas_call(
        _moe_dest_kernel,
        out_shape=jax.ShapeDtypeStruct(eid.shape, jnp.int32),
        in_specs=[pl.BlockSpec(memory_space=pltpu.SMEM),
                  pl.BlockSpec(eid.shape, lambda: (0, 0, 0)), pl.BlockSpec(eid.shape, lambda: (0, 0, 0))],
        out_specs=pl.BlockSpec(eid.shape, lambda: (0, 0, 0)),
        name="moe_dest",
    )(off, eid, rank)


def _sc_mesh():
    return plsc.VectorSubcoreMesh(core_axis_name="c", subcore_axis_name="s",
                                  num_cores=V7X_SC_CORES, num_subcores=V7X_SC_SUBCORES)


def _sc_chunk(rows_per_worker):
    return max(c for c in range(8, SC_MAX_CHUNK + 1, 8) if rows_per_worker % c == 0)


def _sc_dispatch(xp, dest, p_rows):
    t, w = xp.shape
    tpw = t // V7X_SC_WORKERS
    ch = _sc_chunk(tpw)

    @functools.partial(
        pl.kernel, mesh=_sc_mesh(), out_type=jax.ShapeDtypeStruct((p_rows, w), xp.dtype),
        scratch_types=[pltpu.VMEM((ch, w), xp.dtype)] + [pltpu.VMEM((ch,), jnp.int32)] * TOP_K
        + [pltpu.SemaphoreType.DMA, pltpu.SemaphoreType.DMA],
        name="moe_dispatch")
    def run(x_hbm, dest_hbm, out_hbm, rows_v, *rest):
        idx, (sem_i, sem_o) = rest[:TOP_K], rest[TOP_K:]
        base = (lax.axis_index("s") * V7X_SC_CORES + lax.axis_index("c")) * tpw

        @pl.loop(0, tpw // ch)
        def _(i):
            t0 = base + i * ch
            loads = [pltpu.async_copy(dest_hbm.at[k, pl.ds(t0, ch)], idx[k], sem_i) for k in range(TOP_K)]
            pltpu.sync_copy(x_hbm.at[pl.ds(t0, ch)], rows_v)
            for c in loads:
                c.wait()
            puts = [pltpu.async_copy(rows_v, out_hbm.at[idx[k]], sem_o) for k in range(TOP_K)]
            for c in puts:
                c.wait()

    return run(xp, dest)


def _sc_gather(ys, dest, t):
    w = ys.shape[1]
    tpw = t // V7X_SC_WORKERS
    ch = _sc_chunk(tpw)

    @functools.partial(
        pl.kernel, mesh=_sc_mesh(), out_type=jax.ShapeDtypeStruct((TOP_K, t, w), ys.dtype),
        scratch_types=[pltpu.VMEM((ch, w), ys.dtype)] * 2 + [pltpu.VMEM((ch,), jnp.int32)] * TOP_K
        + [pltpu.SemaphoreType.DMA] * 5,
        name="moe_gather")
    def run(y_hbm, dest_hbm, out_hbm, rows_a, rows_b, *rest):
        idx, (sem_i, sem_ga, sem_gb, sem_wa, sem_wb) = rest[:TOP_K], rest[TOP_K:]
        rows, sem_g, sem_w = (rows_a, rows_b), (sem_ga, sem_gb), (sem_wa, sem_wb)
        base = (lax.axis_index("s") * V7X_SC_CORES + lax.axis_index("c")) * tpw

        @pl.loop(0, tpw // ch)
        def _(i):
            t0 = base + i * ch
            loads = [pltpu.async_copy(dest_hbm.at[k, pl.ds(t0, ch)], idx[k], sem_i) for k in range(TOP_K)]
            for c in loads:
                c.wait()
            gets, puts = [None] * TOP_K, [None] * TOP_K
            gets[0] = pltpu.async_copy(y_hbm.at[idx[0]], rows[0], sem_g[0])
            for k in range(TOP_K):
                if k + 1 < TOP_K:
                    if k >= 1:
                        puts[k - 1].wait()
                    gets[k + 1] = pltpu.async_copy(y_hbm.at[idx[k + 1]], rows[(k + 1) % 2], sem_g[(k + 1) % 2])
                gets[k].wait()
                puts[k] = pltpu.async_copy(rows[k % 2], out_hbm.at[k, pl.ds(t0, ch)], sem_w[k % 2])
            puts[TOP_K - 2].wait()
            puts[TOP_K - 1].wait()

    return run(ys, dest)


def _moe_expert_kernel(te_ref, nv_ref, x_ref, wg_ref, wu_ref, wd_ref, y_ref):
    @pl.when(pl.program_id(0) < nv_ref[0])
    def _():
        lo, hi = _unpack_bf16_pair(x_ref[...])
        half = lo.shape[1]
        hg = _dot(lo, wg_ref[0, 0:half, :]) + _dot(hi, wg_ref[0, half:, :])
        hu = _dot(lo, wu_ref[0, 0:half, :]) + _dot(hi, wu_ref[0, half:, :])
        y_ref[...] = _pack_bf16_pair(_dot(_silu(hg) * hu, wd_ref[0]))


def _moe_experts(tile_expert, n_valid, xs, wg, wu, wd):
    p_rows, w = xs.shape
    tm = MOE_ROW_TILE
    _, d, f = wg.shape
    wspec = lambda shp: pl.BlockSpec((1,) + shp, lambda i, te, nv: (te[i], 0, 0))
    return pl.pallas_call(
        _moe_expert_kernel,
        out_shape=jax.ShapeDtypeStruct((p_rows, w), xs.dtype),
        grid_spec=pltpu.PrefetchScalarGridSpec(
            num_scalar_prefetch=2, grid=(p_rows // tm,),
            in_specs=[pl.BlockSpec((tm, w), lambda i, te, nv: (jnp.minimum(i, nv[0] - 1), 0)),
                      wspec((d, f)), wspec((d, f)), wspec((f, d))],
            out_specs=pl.BlockSpec((tm, w), lambda i, te, nv: (i, 0))),
        compiler_params=pltpu.CompilerParams(dimension_semantics=("arbitrary",),
                                             vmem_limit_bytes=_vmem_limit(32)),
        name="moe_experts",
    )(tile_expert, n_valid, xs, wg, wu, wd)


def _moe_combine_kernel(yg_ref, w_ref, xp_ref, sg_ref, su_ref, sd_ref, h_ref, mods_ref, gfin_ref, o_ref, *, final_norm):
    xlo, xhi = _unpack_bf16_pair(xp_ref[0])
    half = xlo.shape[1]
    hs = (_silu(_dot(xlo, sg_ref[0:half, :]) + _dot(xhi, sg_ref[half:, :]))
          * (_dot(xlo, su_ref[0:half, :]) + _dot(xhi, su_ref[half:, :])))
    acc = _dot(hs, sd_ref[...])
    lo = acc[:, :half]
    hi = acc[:, half:]
    w = w_ref[0]
    for k in range(TOP_K):
        ylo, yhi = _unpack_bf16_pair(yg_ref[k, 0])
        wk = w[:, k:k + 1]
        lo = lo + wk * ylo
        hi = hi + wk * yhi
    y = h_ref[0] + mods_ref[0, 0, N_MODS - 1:N_MODS, :] * jnp.concatenate([lo, hi], axis=1)
    if final_norm:
        y = _rms(y, gfin_ref[...])
    o_ref[0] = y


def _moe_combine(yg, wcols, xp, sg, su, sd, h, mods, gfin, nct, final_norm):
    b, l, d = h.shape
    tl = TOKEN_TILE
    tok = lambda w: pl.BlockSpec((1, tl, w), lambda i, j: (i, j, 0))
    full = lambda x: pl.BlockSpec(x.shape, lambda i, j: (0,) * x.ndim)
    return pl.pallas_call(
        functools.partial(_moe_combine_kernel, final_norm=final_norm),
        out_shape=jax.ShapeDtypeStruct((b, l, d), F32),
        grid=(b, l // tl),
        in_specs=[pl.BlockSpec((TOP_K, 1, tl, d // 2), lambda i, j: (0, i, j, 0)), tok(GATE_W), tok(d // 2),
                  full(sg), full(su), full(sd), tok(d),
                  pl.BlockSpec((1, 1, N_MODS, d), lambda i, j: (i, jnp.where(j < nct, 0, 1), 0, 0)),
                  full(gfin)],
        out_specs=tok(d),
        compiler_params=pltpu.CompilerParams(dimension_semantics=("parallel", "parallel"),
                                             vmem_limit_bytes=_vmem_limit(40)),
        name="moe_combine",
    )(yg, wcols, xp, sg, su, sd, h, mods, gfin)


def _moe_sparse(n2p, eid, rank, wcols, counts, h, mods, wg, wu, wd, sg, su, sd, gfin, nct, final_norm):
    b, l, d = h.shape
    t = b * l
    tm = MOE_ROW_TILE
    n_tiles = -(-(TOP_K * t + N_EXPERTS * (tm - 1)) // tm)
    tiles_e = (counts.reshape(N_EXPERTS).astype(jnp.int32) + (tm - 1)) // tm
    tile_end = jnp.cumsum(tiles_e)
    off = (tile_end - tiles_e) * tm
    n_valid = tile_end[-1:]
    tile_id = jnp.minimum(jnp.arange(n_tiles, dtype=jnp.int32), n_valid - 1)
    tile_expert = jnp.sum((tile_end[None, :] <= tile_id[:, None]).astype(jnp.int32), axis=1)
    dest = _moe_dest(off, eid, rank).transpose(1, 0, 2).reshape(8, t)
    xs = _sc_dispatch(n2p.reshape(t, d // 2), dest, n_tiles * tm)
    ys = _moe_experts(tile_expert, n_valid, xs, wg, wu, wd)
    yg = _sc_gather(ys, dest, t).reshape(TOP_K, b, l, d // 2)
    return _moe_combine(yg, wcols, n2p, sg, su, sd, h, mods, gfin, nct, final_norm)


def _rwkv_proj_kernel(h_ref, hp_ref, hx_ref, mods_ref, g_ref, mu_ref, wr_ref, wk_ref, wv_ref, g1_ref, g2_ref,
                      w1_ref, w2_ref, a1_ref, a2_ref, w0_ref, a0_ref, kk_ref, ka_ref, rk_ref, bd_ref,
                      r_out, v_out, kk_out, g_out, km_out, b_out, lw_out, bonus_out, *, nct):
    j = pl.program_id(1)
    nt = pl.num_programs(1)
    m = mods_ref[0, 0]
    g = g_ref[...]
    n = _norm_mod(h_ref[0], g, m[0:1], m[1:2])
    tl, d = n.shape
    seg_first = (j == 0) | (j == nct)
    seg_last = (j == nct - 1) | (j == nt - 1)
    n_prev = _norm_mod(hp_ref[0], g, m[0:1], m[1:2])[7:8] * jnp.where(seg_first, 0.0, 1.0)
    n_next = _norm_mod(hx_ref[0], g, m[0:1], m[1:2])[0:1] * jnp.where(seg_last, 0.0, 1.0)
    row = lax.broadcasted_iota(jnp.int32, (tl, 1), 0)
    prev = jnp.where(row == 0, n_prev, pltpu.roll(n, 1, axis=0))
    nxt = jnp.where(row == tl - 1, n_next, pltpu.roll(n, tl - 1, axis=0))
    lane = lax.broadcasted_iota(jnp.int32, (1, d), 1)
    xx = jnp.where(lane < d // 2, prev, nxt) - n
    mu = mu_ref[...]
    xr, xw, xk, xv, xa, xg = [n + xx * mu[i:i + 1] for i in range(6)]
    r = _dot(xr, wr_ref[...])
    k = _dot(xk, wk_ref[...])
    v = _dot(xv, wv_ref[...])
    g_out[0] = _dot(_sigmoid(_dot(xg, g1_ref[...])), g2_ref[...]).astype(g_out.dtype)
    tw = jnp.tanh(_dot(xw, w1_ref[...]))
    ta = _dot(xa, a1_ref[...])
    bd = bd_ref[...]
    kk = k * kk_ref[...]
    kk = kk / jnp.maximum(jnp.sqrt(_head_sum(kk * kk, bd)), 1e-12)
    r_out[0] = r.astype(r_out.dtype)
    v_out[0] = v.astype(v_out.dtype)
    kk_out[0] = kk.astype(kk_out.dtype)
    bonus = jnp.zeros_like(v)
    for dr in range(2):
        zw = w0_ref[dr:dr + 1, :] + _dot(tw, w2_ref[dr])
        lw_out[dr, 0] = -jnp.exp(-0.5) * _sigmoid(zw)
        a = _sigmoid(a0_ref[dr:dr + 1, :] + _dot(ta, a2_ref[dr]))
        km = k * (1.0 + (a - 1.0) * ka_ref[...])
        km_out[dr, 0] = km.astype(km_out.dtype)
        b_out[dr, 0] = (kk * a).astype(b_out.dtype)
        bonus = bonus + _head_sum(r * km * rk_ref[...], bd) * v
    bonus_out[0] = bonus


def _rwkv_proj(h, mods, g, mu, wr, wk, wv, g1, g2, w1, w2, a1, a2, w0, a0, kk, ka, rk, bd, nct):
    b, l, d = h.shape
    tl = TOKEN_TILE
    nb8 = l // 8
    tok = pl.BlockSpec((1, tl, d), lambda i, j: (i, j, 0))
    tok2 = pl.BlockSpec((2, 1, tl, d), lambda i, j: (0, i, j, 0))
    full = lambda x: pl.BlockSpec(x.shape, lambda i, j: (0,) * x.ndim)
    sds = jax.ShapeDtypeStruct
    return pl.pallas_call(
        functools.partial(_rwkv_proj_kernel, nct=nct),
        out_shape=[sds((b, l, d), BF16), sds((b, l, d), BF16), sds((b, l, d), BF16), sds((b, l, d), BF16),
                   sds((2, b, l, d), BF16), sds((2, b, l, d), BF16), sds((2, b, l, d), F32), sds((b, l, d), F32)],
        grid=(b, l // tl),
        in_specs=[tok,
                  pl.BlockSpec((1, 8, d), lambda i, j: (i, jnp.maximum(j * (tl // 8) - 1, 0), 0)),
                  pl.BlockSpec((1, 8, d), lambda i, j: (i, jnp.minimum((j + 1) * (tl // 8), nb8 - 1), 0)),
                  pl.BlockSpec((1, 1, N_MODS, d), lambda i, j: (i, jnp.where(j < nct, 0, 1), 0, 0)),
                  full(g), full(mu), full(wr), full(wk), full(wv), full(g1), full(g2), full(w1), full(w2),
                  full(a1), full(a2), full(w0), full(a0), full(kk), full(ka), full(rk), full(bd)],
        out_specs=[tok, tok, tok, tok, tok2, tok2, tok2, tok],
        compiler_params=pltpu.CompilerParams(dimension_semantics=("parallel", "parallel"),
                                             vmem_limit_bytes=_vmem_limit(56)),
        name="rwkv_proj",
    )(h, h, h, mods, g, mu, wr, wk, wv, g1, g2, w1, w2, a1, a2, w0, a0, kk, ka, rk, bd)


def _wkv_kernel(r_ref, v_ref, kk_ref, km_ref, b_ref, lw_ref, y_ref, st_ref):
    c = WKV_CHUNK
    w = WKV_PAIR
    rev = pl.program_id(0)
    sign = 1 - 2 * rev

    @pl.when(pl.program_id(2) == 0)
    def _():
        st_ref[...] = jnp.zeros_like(st_ref)

    ti = lax.broadcasted_iota(jnp.int32, (c, c), 0)
    si = lax.broadcasted_iota(jnp.int32, (c, c), 1)
    tri = jnp.where((si - ti) * sign <= 0, 1.0, 0.0).astype(F32)
    lw = lw_ref[0, 0]
    l_incl = jnp.dot(tri, lw, precision=HIGHEST, preferred_element_type=F32)
    l_tot = jnp.sum(lw, axis=0, keepdims=True)
    mid = 0.5 * l_tot
    e_pos = jnp.exp(l_incl - mid)
    e_neg = jnp.exp(mid - l_incl)
    e_mid = jnp.exp(mid)
    rt = r_ref[0].astype(F32) * e_pos
    kt = kk_ref[0].astype(F32) * jnp.exp(l_incl - lw - mid)
    kh = km_ref[0, 0].astype(F32) * e_neg
    bh = b_ref[0, 0].astype(F32) * e_neg
    v = v_ref[0]

    ri = lax.broadcasted_iota(jnp.int32, (w, w), 0)
    ci = lax.broadcasted_iota(jnp.int32, (w, w), 1)
    same = (ri // c) == (ci // c)
    dlt = (ci % c - ri % c) * sign
    strict = same & (dlt < 0)
    incl = same & (dlt <= 0)
    eye = jnp.where(ri == ci, 1.0, 0.0).astype(F32)
    lane = lax.broadcasted_iota(jnp.int32, (1, w), 1)
    h0 = lane < RWKV_HEAD

    def rows2(x):
        return jnp.concatenate([jnp.where(h0, x, 0.0), jnp.where(h0, 0.0, x)], axis=0)

    def fold(x):
        return x[:c] + x[c:]

    pairs = range(st_ref.shape[0])
    sls = [slice(p * w, (p + 1) * w) for p in pairs]
    em = [e_mid[:, s] for s in sls]
    g = [_dot_nt(jnp.concatenate([rows2(kt[:, s]), rows2(rt[:, s])], axis=0),
                 jnp.concatenate([kh[:, s], kh[:, s], bh[:, s], bh[:, s]], axis=0)) for s in sls]
    a_kk = [jnp.where(strict, x[:w, :w], 0.0) for x in g]
    nmat = [jnp.where(strict, x[:w, w:], 0.0) for x in g]
    a_rk = [jnp.where(incl, x[w:, :w], 0.0) for x in g]
    a_rb = [jnp.where(incl, x[w:, w:], 0.0) for x in g]
    v32 = [v[:, s].astype(F32) for s in sls]
    v_rows = [rows2(x) for x in v32]
    r_pre = [_dot(a_kk[p], v_rows[p]) for p in pairs]
    tinv = [eye - x for x in nmat]
    npow = nmat
    for _ in range(c.bit_length() - 2):
        npow = [_dot(x, x) for x in npow]
        tinv = [tinv[p] + _dot(tinv[p], npow[p]) for p in pairs]
    sol = [_dot(tinv[p], jnp.concatenate([r_pre[p], rows2(kt[:, sls[p]] * em[p])], axis=1)) for p in pairs]
    u_rows = [x[:, :w] for x in sol]
    kq_rows = [x[:, w:] for x in sol]
    y_pre = [fold(_dot(jnp.concatenate([a_rk[p], -a_rb[p]], axis=1),
                       jnp.concatenate([v_rows[p], u_rows[p]], axis=0))) for p in pairs]
    r_eff = [rt[:, sls[p]] * em[p] - fold(_dot(a_rb[p], kq_rows[p])) for p in pairs]
    bbar = [bh[:, sls[p]] * em[p] for p in pairs]
    kbar = [kh[:, sls[p]] * em[p] for p in pairs]
    eye_dec = [eye * (em[p] * em[p]) for p in pairs]
    mmat = [eye_dec[p] - jnp.where(same, _dot_tn(fold(kq_rows[p]), bbar[p]), 0.0) for p in pairs]
    s_pre = [jnp.where(same, _dot_tn(jnp.concatenate([v32[p], -fold(u_rows[p])], axis=0),
                                     jnp.concatenate([kbar[p], bbar[p]], axis=0)), 0.0) for p in pairs]
    st = [st_ref[p] for p in pairs]
    for p in pairs:
        y_ref[0, 0, :, sls[p]] = _dot_nt(r_eff[p], st[p]) + y_pre[p]
    for p in pairs:
        hi = st[p].astype(BF16)
        lo = (st[p] - hi.astype(F32)).astype(BF16)
        mb = mmat[p].astype(BF16)
        st_ref[p] = (jnp.dot(hi, mb, preferred_element_type=F32) + jnp.dot(lo, mb, preferred_element_type=F32)
                     + s_pre[p])


def _wkv(r, v, kk, km, bv, lw, lc):
    b, l, d = r.shape
    c = WKV_CHUNK
    ncc = lc // c
    nlc = (l - lc) // c

    def chunk(dr, i):
        return jnp.where(dr == 0, i, jnp.where(i < ncc, ncc - 1 - i, nlc + 2 * ncc - 1 - i))

    shared = pl.BlockSpec((1, c, d), lambda dr, bi, i: (bi, chunk(dr, i), 0))
    per_dir = pl.BlockSpec((1, 1, c, d), lambda dr, bi, i: (dr, bi, chunk(dr, i), 0))
    return pl.pallas_call(
        _wkv_kernel,
        out_shape=jax.ShapeDtypeStruct((2, b, l, d), F32),
        grid=(2, b, l // c),
        in_specs=[shared, shared, shared, per_dir, per_dir, per_dir],
        out_specs=per_dir,
        scratch_shapes=[pltpu.VMEM((d // WKV_PAIR, WKV_PAIR, WKV_PAIR), F32)],
        compiler_params=pltpu.CompilerParams(dimension_semantics=("parallel", "parallel", "arbitrary"),
                                             vmem_limit_bytes=_vmem_limit(32)),
        name="wkv7_chunked",
    )(r, v, kk, km, bv, lw)


def _rwkv_out_kernel(y_ref, bonus_ref, g_ref, lnw_ref, lnb_ref, wo_ref, bd_ref, h_ref, mods_ref, gffn_ref,
                     wrt_ref, bias_ref, hn_ref, n2_ref, eid_ref, rank_ref, w_ref, cnt_ref, run_ref):
    y = y_ref[0, 0] + y_ref[1, 0]
    bd = bd_ref[...]
    mean = _head_sum(y, bd) * (1.0 / RWKV_HEAD)
    yc = y - mean
    var = _head_sum(yc * yc, bd) * (1.0 / RWKV_HEAD)
    yn = yc * lax.rsqrt(var + GN_EPS) * lnw_ref[...] + lnb_ref[...]
    out = (yn + bonus_ref[0]) * g_ref[0].astype(F32)
    _mixer_tail(_dot(out, wo_ref[...]), h_ref, mods_ref[0, 0], gffn_ref, wrt_ref, bias_ref, hn_ref, n2_ref,
                eid_ref, rank_ref, w_ref, cnt_ref, run_ref)


def _rwkv_out(y, bonus, g, lnw, lnb, wo, bd, h, mods, gffn, wrt, bias, nct):
    b, l, d = h.shape
    tl = TOKEN_TILE
    tok = lambda w: pl.BlockSpec((1, tl, w), lambda i, j: (i, j, 0))
    full = lambda x: pl.BlockSpec(x.shape, lambda i, j: (0,) * x.ndim)
    shapes, specs = _tail_outs(b, l, d)
    return pl.pallas_call(
        _rwkv_out_kernel,
        out_shape=shapes,
        grid=(b, l // tl),
        in_specs=[pl.BlockSpec((2, 1, tl, d), lambda i, j: (0, i, j, 0)), tok(d), tok(d),
                  full(lnw), full(lnb), full(wo), full(bd), tok(d),
                  pl.BlockSpec((1, 1, N_MODS, d), lambda i, j: (i, jnp.where(j < nct, 0, 1), 0, 0)),
                  full(gffn), full(wrt), full(bias)],
        out_specs=specs,
        scratch_shapes=[pltpu.VMEM((N_EXPERTS, 1), F32)],
        compiler_params=pltpu.CompilerParams(dimension_semantics=("arbitrary", "arbitrary"),
                                             vmem_limit_bytes=_vmem_limit(40)),
        name="rwkv_out",
    )(y, bonus, g, lnw, lnb, wo, bd, h, mods, gffn, wrt, bias)


def _rope_table(n_lat, n_ctx):
    dim = SWA_HEAD_DIM
    nf = dim // 4
    inv = ROPE_THETA ** (-jnp.arange(nf, dtype=F32) / nf)
    row = jnp.repeat(jnp.arange(n_lat // GRID_W, dtype=F32), GRID_W)
    col = jnp.tile(jnp.arange(GRID_W, dtype=F32), n_lat // GRID_W)
    ar = row[:, None] * inv
    ac = col[:, None] * inv
    ang = jnp.concatenate([ar, ar, ac, ac], axis=-1)
    cos = jnp.concatenate([jnp.ones((n_ctx, dim), F32), jnp.cos(ang)], axis=0)
    sin = jnp.concatenate([jnp.zeros((n_ctx, dim), F32), jnp.sin(ang)], axis=0)
    return jnp.tile(cos, (1, 2)), jnp.tile(sin, (1, 2))


def _layout_attn_weights(w_in, w_uq, w_ukv):
    d = w_in.shape[0]
    s0 = MLA_Q_RANK
    s1 = s0 + MLA_KV_RANK
    s2 = s1 + MLA_ROPE
    s3 = s2 + SWA_HEADS * SWA_HEAD_DIM
    s4 = s3 + SWA_KV_HEADS * SWA_HEAD_DIM
    rep = lambda w: jnp.concatenate(
        [jnp.tile(w[:, g * SWA_HEAD_DIM:(g + 1) * SWA_HEAD_DIM], (1, SWA_GROUP)) for g in range(SWA_KV_HEADS)], axis=1)
    win = jnp.concatenate([w_in[:, :s1], w_in[:, s2:s3], rep(w_in[:, s3:s4]), rep(w_in[:, s4:]),
                           w_in[:, s1:s2], jnp.zeros((d, V7X_LANES - MLA_ROPE), w_in.dtype)], axis=1)
    qh = MLA_NOPE + MLA_ROPE
    pad = jnp.zeros((w_uq.shape[0], V7X_MXU_DIM - qh), w_uq.dtype)
    wuq = jnp.concatenate([jnp.concatenate([w_uq[:, h * qh:(h + 1) * qh], pad], axis=1) for h in range(MLA_HEADS)], axis=1)
    kvh = MLA_NOPE + MLA_V
    wuk = jnp.concatenate([w_ukv[:, h * kvh:h * kvh + MLA_NOPE] for h in range(MLA_HEADS)], axis=1)
    wuvt = jnp.concatenate([w_ukv[:, h * kvh + MLA_NOPE:(h + 1) * kvh] for h in range(MLA_HEADS)], axis=1).T
    return win.astype(BF16), wuq.astype(BF16), wuk.astype(BF16), wuvt.astype(BF16)


def _lora_pair(w_down, w_up):
    rank = w_down.shape[2]
    down = jnp.concatenate([w_down[0], w_down[1]], axis=1)
    z = jnp.zeros((rank, w_up.shape[2]), w_up.dtype)
    up = jnp.stack([jnp.concatenate([w_up[0], z], axis=0), jnp.concatenate([z, w_up[1]], axis=0)], axis=0)
    return down.astype(BF16), up.astype(BF16)


def _head_block_diag():
    i = jnp.arange(V7X_MXU_DIM) // RWKV_HEAD
    return (i[:, None] == i[None, :]).astype(BF16)


def kernel(x, c, ctx, c_ctx, ada_w, ada_b, norm_mix, norm_ffn, norm_final, attn_w_in, attn_q_norm, attn_kv_norm, attn_w_uq, attn_w_ukv, attn_sinks, attn_w_o, rwkv_mu, rwkv_w_r, rwkv_w_k, rwkv_w_v, rwkv_w_o, rwkv_g1, rwkv_g2, rwkv_w0, rwkv_w1, rwkv_w2, rwkv_a0, rwkv_a1, rwkv_a2, rwkv_k_k, rwkv_k_a, rwkv_r_k, rwkv_ln_w, rwkv_ln_b, moe_router, moe_bias, moe_w_gate, moe_w_up, moe_w_down, moe_ws_gate, moe_ws_up, moe_ws_down):
    bsz, s, d = x.shape
    lc = ctx.shape[1]
    l = lc + s
    depth = ada_w.shape[0]
    nct = lc // TOKEN_TILE
    assert lc % TOKEN_TILE == 0 and s % TOKEN_TILE == 0 and s >= SWA_BAND and lc % WKV_CHUNK == 0
    assert d % V7X_MXU_DIM == 0 and WKV_CHUNK * 2 == V7X_LANES
    assert (bsz * l) % (8 * V7X_SC_WORKERS) == 0

    h = jnp.concatenate([ctx, x], axis=1)
    cos, sin = _rope_table(s, lc)
    bd = _head_block_diag()
    rows = -(-(bsz + 1) // 8) * 8
    cc = jnp.concatenate([c, c_ctx[None, :], jnp.zeros((rows - bsz - 1, d), F32)], axis=0)
    row2 = lambda a: a.reshape(1, -1)

    for li in range(depth):
        with_ctx = li < depth - 1
        i = li // 2
        ada = _ada_mods(cc, ada_w[li], ada_b[li])
        mods = jnp.stack([jnp.broadcast_to(ada[bsz].reshape(1, N_MODS, d), (bsz, N_MODS, d)),
                          ada[:bsz].reshape(bsz, N_MODS, d)], axis=1)
        wrt = jnp.concatenate([moe_router[li].T, jnp.zeros((GATE_W - N_EXPERTS, d), F32)], axis=0)
        bias = moe_bias[li].reshape(N_GROUPS, GROUP_SIZE, 1)
        if li % 2 == 0:
            win, wuq, wuk, wuvt = _layout_attn_weights(attn_w_in[i], attn_w_uq[i], attn_w_ukv[i])
            q, k, vt, qs, ks, vs = _attn_proj(h, mods, row2(norm_mix[li]), win, row2(attn_q_norm[i]),
                                              row2(attn_kv_norm[i]), wuq, wuk, wuvt, cos, sin, nct)
            a = _mla_attention(q, k, vt, lc, 0 if with_ctx else lc // MLA_Q_TILE)
            bm = _swa_attention(attn_sinks[i], qs, ks, vs, lc, 0 if with_ctx else lc // SWA_Q_TILE)
            tail = _attn_out(a, bm, h, mods, attn_w_o[i].astype(BF16), row2(norm_ffn[li]), wrt, bias, nct)
        else:
            w1, w2 = _lora_pair(rwkv_w1[i], rwkv_w2[i])
            a1, a2 = _lora_pair(rwkv_a1[i], rwkv_a2[i])
            r, v, kk, g, km, bv, lw, bonus = _rwkv_proj(
                h, mods, row2(norm_mix[li]), rwkv_mu[i], rwkv_w_r[i].astype(BF16), rwkv_w_k[i].astype(BF16),
                rwkv_w_v[i].astype(BF16), rwkv_g1[i].astype(BF16), rwkv_g2[i].astype(BF16), w1, w2, a1, a2,
                rwkv_w0[i], rwkv_a0[i], row2(rwkv_k_k[i]), row2(rwkv_k_a[i]), row2(rwkv_r_k[i]), bd, nct)
            y = _wkv(r, v, kk, km, bv, lw, lc)
            tail = _rwkv_out(y, bonus, g, row2(rwkv_ln_w[i]), row2(rwkv_ln_b[i]), rwkv_w_o[i].astype(BF16),
                             bd, h, mods, row2(norm_ffn[li]), wrt, bias, nct)
        h, n2p, eid, rank, wcols, counts = tail
        h = _moe_sparse(n2p, eid, rank, wcols, counts, h, mods, moe_w_gate[li], moe_w_up[li], moe_w_down[li],
                        moe_ws_gate[li], moe_ws_up[li], moe_ws_down[li], row2(norm_final), nct, li == depth - 1)
    return h[:, lc:]
```

```python
import functools

import jax
import jax.numpy as jnp
from jax import lax
from jax.experimental import pallas as pl
from jax.experimental.pallas import tpu as pltpu
from jax.experimental.pallas import tpu_sc as plsc

F32 = jnp.float32
BF16 = jnp.bfloat16
HIGHEST = lax.Precision.HIGHEST

GRID_W = 64
NORM_EPS = 1e-6
ROPE_THETA = 10000.0
NEG_INF = -1e30
N_MODS = 6

MLA_HEADS = 4
MLA_Q_RANK = 384
MLA_KV_RANK = 256
MLA_NOPE = 128
MLA_ROPE = 64
MLA_V = 128

SWA_HEADS = 8
SWA_KV_HEADS = 2
SWA_GROUP = SWA_HEADS // SWA_KV_HEADS
SWA_HEAD_DIM = 64
WINDOW = 128

RWKV_HEAD = 64
DECAY_LORA = 64
ICLR_LORA = 64
GATE_LORA = 128
GN_EPS = 64e-5

N_EXPERTS = 64
TOP_K = 6
N_GROUPS = 8
TOPK_GROUPS = 4
GROUP_SIZE = N_EXPERTS // N_GROUPS
ROUTED_SCALE = 2.5
GATE_W = 128

V7X_LANES = 128
V7X_MXU_DIM = 256
V7X_VMEM_BYTES = 64 * 1024 * 1024
V7X_SC_CORES = 2
V7X_SC_SUBCORES = 16
V7X_SC_WORKERS = V7X_SC_CORES * V7X_SC_SUBCORES

TOKEN_TILE = 256
MLA_Q_TILE = 256
MLA_HEADS_PER_STEP = 2
LOG2E = 1.4426950408889634
SWA_Q_TILE = 128
SWA_BAND = SWA_Q_TILE + 2 * WINDOW
WKV_CHUNK = 64
WKV_PAIR = 2 * RWKV_HEAD
WKV_CHUNKS_PER_STEP = 2
MOE_ROW_TILE = 512
SC_MAX_CHUNK = 64


def _vmem_limit(mib):
    return min(mib * 1024 * 1024, V7X_VMEM_BYTES - 4 * 1024 * 1024)


def _dot(a, b):
    return jnp.dot(a.astype(BF16), b.astype(BF16), preferred_element_type=F32)


def _dot_nt(a, b):
    return lax.dot_general(a.astype(BF16), b.astype(BF16), (((1,), (1,)), ((), ())),
                           preferred_element_type=F32)


def _dot_tn(a, b):
    return lax.dot_general(a.astype(BF16), b.astype(BF16), (((0,), (0,)), ((), ())),
                           preferred_element_type=F32)


def _sigmoid(x):
    return 1.0 / (1.0 + jnp.exp(-x))


def _silu(x):
    return x * _sigmoid(x)


def _rms(x, g):
    return x * lax.rsqrt(jnp.mean(x * x, axis=-1, keepdims=True) + NORM_EPS) * g


def _norm_mod(x, g, shift, scale):
    return _rms(x, g) * (1.0 + scale) + shift


def _split_dot(x, w):
    hi = x.astype(BF16)
    lo = (x - hi.astype(F32)).astype(BF16)
    return (jnp.dot(hi, w, preferred_element_type=F32) + jnp.dot(lo, w, preferred_element_type=F32))


def _head_sum(x, bd):
    w = bd.shape[0]
    parts = [_split_dot(x[:, c * w:(c + 1) * w], bd) for c in range(x.shape[1] // w)]
    return jnp.concatenate(parts, axis=1)


def _ada_kernel(c_ref, w_ref, b_ref, o_ref):
    s = _silu(c_ref[...])
    o_ref[...] = jnp.dot(s, w_ref[0], precision=HIGHEST, preferred_element_type=F32) + b_ref[0]


def _ada_mods(cc, w, b, layer):
    rows, d = cc.shape
    depth, _, n = w.shape
    return pl.pallas_call(
        _ada_kernel,
        out_shape=jax.ShapeDtypeStruct((rows, n), F32),
        grid=(n // d,),
        in_specs=[pl.BlockSpec((rows, d), lambda i: (0, 0)),
                  pl.BlockSpec((1, d, d), lambda i: (layer, 0, i)),
                  pl.BlockSpec((1, 1, d), lambda i: (layer, 0, i))],
        out_specs=pl.BlockSpec((rows, d), lambda i: (0, i)),
        compiler_params=pltpu.CompilerParams(dimension_semantics=("parallel",),
                                             vmem_limit_bytes=_vmem_limit(32)),
        name="ada_mods",
    )(cc, w, b.reshape(depth, 1, n))


def _rope128(x, cos, sin, first_half):
    rot = jnp.where(first_half, -pltpu.roll(x, V7X_LANES - 16, axis=1), pltpu.roll(x, 16, axis=1))
    return x * cos + rot * sin


_C_CQ = 0
_C_CKV = _C_CQ + MLA_Q_RANK
_C_QS = _C_CKV + MLA_KV_RANK
_C_KS = _C_QS + SWA_HEADS * SWA_HEAD_DIM
_C_VS = _C_KS + SWA_KV_HEADS * V7X_MXU_DIM
_C_KR = _C_VS + SWA_KV_HEADS * V7X_MXU_DIM
_C_END = _C_KR + V7X_LANES
_SWA_W = SWA_KV_HEADS * V7X_MXU_DIM
_MLA_QK_W = MLA_HEADS * V7X_MXU_DIM


def _attn_proj_kernel(h_ref, mods_ref, g_ref, win_ref, qn_ref, kvn_ref, wuq_ref, wuk_ref, wuvt_ref, cos_ref, sin_ref,
                      q_ref, k_ref, vt_ref, qs_ref, ks_ref, vs_ref):
    m = mods_ref[0, 0]
    n = _norm_mod(h_ref[0], g_ref[...], m[0:1], m[1:2])
    u = _dot(n, win_ref[...])
    cos = cos_ref[...]
    sin = sin_ref[...]
    lane = lax.broadcasted_iota(jnp.int32, (1, V7X_LANES), 1)
    first_half = (lane % 32) < 16

    def rope(x):
        return _rope128(x, cos, sin, first_half)

    scale_a = (MLA_NOPE + MLA_ROPE) ** -0.5 * LOG2E
    scale_b = SWA_HEAD_DIM ** -0.5
    q = _dot(_rms(u[:, _C_CQ:_C_CKV], qn_ref[...]), wuq_ref[...])
    ckv = _rms(u[:, _C_CKV:_C_QS], kvn_ref[...])
    kn = _dot(ckv, wuk_ref[...])
    vt_ref[0] = _dot_nt(wuvt_ref[...], ckv).astype(BF16)
    kr = rope(u[:, _C_KR:_C_END]).astype(BF16)
    for h in range(MLA_HEADS):
        o = h * V7X_MXU_DIM
        q_ref[0, :, o:o + V7X_LANES] = (q[:, o:o + V7X_LANES] * scale_a).astype(BF16)
        q_ref[0, :, o + V7X_LANES:o + V7X_MXU_DIM] = (rope(q[:, o + V7X_LANES:o + V7X_MXU_DIM]) * scale_a).astype(BF16)
        k_ref[0, :, o:o + V7X_LANES] = kn[:, h * MLA_NOPE:(h + 1) * MLA_NOPE].astype(BF16)
        k_ref[0, :, o + V7X_LANES:o + V7X_MXU_DIM] = kr
    for c in range((_C_KS - _C_QS) // V7X_LANES):
        o = c * V7X_LANES
        qs_ref[0, :, o:o + V7X_LANES] = (rope(u[:, _C_QS + o:_C_QS + o + V7X_LANES]) * scale_b).astype(BF16)
    for c in range(_SWA_W // V7X_LANES):
        o = c * V7X_LANES
        ks_ref[0, :, o:o + V7X_LANES] = rope(u[:, _C_KS + o:_C_KS + o + V7X_LANES]).astype(BF16)
    vs_ref[0] = u[:, _C_VS:_C_KR].astype(BF16)


def _attn_proj(h, mods, g, win, qn, kvn, wuq, wuk, wuvt, cos, sin, nct):
    b, l, d = h.shape
    tl = TOKEN_TILE
    tok = lambda w: pl.BlockSpec((1, tl, w), lambda i, j: (i, j, 0))
    full = lambda a: pl.BlockSpec(a.shape, lambda i, j: (0,) * a.ndim)
    sds = jax.ShapeDtypeStruct
    dv = MLA_HEADS * MLA_V
    return pl.pallas_call(
        _attn_proj_kernel,
        out_shape=[sds((b, l, _MLA_QK_W), BF16), sds((b, l, _MLA_QK_W), BF16), sds((b, dv, l), BF16),
                   sds((b, l, SWA_HEADS * SWA_HEAD_DIM), BF16), sds((b, l, _SWA_W), BF16), sds((b, l, _SWA_W), BF16)],
        grid=(b, l // tl),
        in_specs=[tok(d),
                  pl.BlockSpec((1, 1, N_MODS, d), lambda i, j: (i, jnp.where(j < nct, 0, 1), 0, 0)),
                  full(g), full(win), full(qn), full(kvn), full(wuq), full(wuk), full(wuvt),
                  pl.BlockSpec((tl, V7X_LANES), lambda i, j: (j, 0)),
                  pl.BlockSpec((tl, V7X_LANES), lambda i, j: (j, 0))],
        out_specs=[tok(_MLA_QK_W), tok(_MLA_QK_W), pl.BlockSpec((1, dv, tl), lambda i, j: (i, 0, j)),
                   tok(SWA_HEADS * SWA_HEAD_DIM), tok(_SWA_W), tok(_SWA_W)],
        compiler_params=pltpu.CompilerParams(dimension_semantics=("parallel", "parallel"),
                                             vmem_limit_bytes=_vmem_limit(48)),
        name="attn_proj",
    )(h, mods, g, win, qn, kvn, wuq, wuk, wuvt, cos, sin)


def _mla_kernel(q_ref, k_ref, vt_ref, o_ref, *, nct_q, lc):
    hw = V7X_MXU_DIM

    def attend(nk):
        st = [_dot_nt(k_ref[0, 0:nk, hh * hw:(hh + 1) * hw], q_ref[0, :, hh * hw:(hh + 1) * hw])
              for hh in range(MLA_HEADS_PER_STEP)]
        for hh, s in enumerate(st):
            p = jnp.exp2(s - jnp.max(s, axis=0, keepdims=True))
            den = jnp.sum(p, axis=0, keepdims=True)
            ot = _dot(vt_ref[0, hh * MLA_V:(hh + 1) * MLA_V, 0:nk], p) / den
            o_ref[0, :, hh * MLA_V:(hh + 1) * MLA_V] = ot.T.astype(o_ref.dtype)

    @pl.when(pl.program_id(2) < nct_q)
    def _():
        attend(lc)

    @pl.when(pl.program_id(2) >= nct_q)
    def _():
        attend(k_ref.shape[1])


def _mla_attention(q, k, vt, lc, q_tile0):
    b, l, _ = q.shape
    tq = MLA_Q_TILE
    hps = MLA_HEADS_PER_STEP
    return pl.pallas_call(
        functools.partial(_mla_kernel, nct_q=lc // tq - q_tile0, lc=lc),
        out_shape=jax.ShapeDtypeStruct((b, l, MLA_HEADS * MLA_V), BF16),
        grid=(b, MLA_HEADS // hps, l // tq - q_tile0),
        in_specs=[pl.BlockSpec((1, tq, hps * V7X_MXU_DIM), lambda i, h, j: (i, j + q_tile0, h)),
                  pl.BlockSpec((1, l, hps * V7X_MXU_DIM), lambda i, h, j: (i, 0, h)),
                  pl.BlockSpec((1, hps * MLA_V, l), lambda i, h, j: (i, h, 0))],
        out_specs=pl.BlockSpec((1, tq, hps * MLA_V), lambda i, h, j: (i, j + q_tile0, h)),
        compiler_params=pltpu.CompilerParams(dimension_semantics=("parallel", "parallel", "parallel"),
                                             vmem_limit_bytes=_vmem_limit(48)),
        name="mla_attention",
    )(q, k, vt)


def _swa_kernel(sink_ref, q_ref, k_ref, v_ref, o_ref, *, lc, q_tile0):
    tq = SWA_Q_TILE
    l = k_ref.shape[1]
    r0 = (pl.program_id(1) + q_tile0) * tq
    start = pl.multiple_of(jnp.clip(r0 - WINDOW, lc, l - SWA_BAND), tq)
    rows = SWA_GROUP * tq
    row = lax.broadcasted_iota(jnp.int32, (rows, 1), 0)
    qpos = jnp.where(r0 >= lc, r0, -l) + row % tq
    kpos = start + lax.broadcasted_iota(jnp.int32, (1, SWA_BAND), 1)
    valid = jnp.abs(qpos - kpos) <= WINDOW
    lane = lax.broadcasted_iota(jnp.int32, (1, V7X_MXU_DIM), 1)
    for g in range(SWA_KV_HEADS):
        sl = slice(g * V7X_MXU_DIM, (g + 1) * V7X_MXU_DIM)
        qg = q_ref[0, :, sl]
        zero = jnp.zeros_like(qg)
        head = [(lane // SWA_HEAD_DIM) == hh for hh in range(SWA_GROUP)]
        qstack = jnp.concatenate([jnp.where(head[hh], qg, zero) for hh in range(SWA_GROUP)], axis=0)
        sc = _dot_nt(qstack, k_ref[0, 0:lc, sl])
        sb = jnp.where(valid, _dot_nt(qstack, k_ref[0, pl.ds(start, SWA_BAND), sl]), NEG_INF)
        sk = jnp.zeros((rows, 1), F32)
        for hh in range(SWA_GROUP):
            sk = jnp.where(row // tq == hh, sink_ref[g * SWA_GROUP + hh], sk)
        mx = jnp.maximum(jnp.maximum(jnp.max(sc, axis=-1, keepdims=True), jnp.max(sb, axis=-1, keepdims=True)), sk)
        pc = jnp.exp(sc - mx)
        pb = jnp.exp(sb - mx)
        den = jnp.sum(pc, axis=-1, keepdims=True) + jnp.sum(pb, axis=-1, keepdims=True) + jnp.exp(sk - mx)
        ostack = (_dot(pc, v_ref[0, 0:lc, sl]) + _dot(pb, v_ref[0, pl.ds(start, SWA_BAND), sl])) / den
        o = jnp.zeros((tq, V7X_MXU_DIM), F32)
        for hh in range(SWA_GROUP):
            o = o + jnp.where(head[hh], ostack[hh * tq:(hh + 1) * tq], 0.0)
        o_ref[0, :, sl] = o.astype(o_ref.dtype)


def _swa_attention(sinks, q, k, v, lc, q_tile0):
    b, l, _ = q.shape
    tq = SWA_Q_TILE
    return pl.pallas_call(
        functools.partial(_swa_kernel, lc=lc, q_tile0=q_tile0),
        out_shape=jax.ShapeDtypeStruct((b, l, SWA_HEADS * SWA_HEAD_DIM), BF16),
        grid=(b, l // tq - q_tile0),
        in_specs=[pl.BlockSpec(memory_space=pltpu.SMEM),
                  pl.BlockSpec((1, tq, SWA_HEADS * SWA_HEAD_DIM), lambda i, j: (i, j + q_tile0, 0)),
                  pl.BlockSpec((1, l, _SWA_W), lambda i, j: (i, 0, 0)),
                  pl.BlockSpec((1, l, _SWA_W), lambda i, j: (i, 0, 0))],
        out_specs=pl.BlockSpec((1, tq, SWA_HEADS * SWA_HEAD_DIM), lambda i, j: (i, j + q_tile0, 0)),
        compiler_params=pltpu.CompilerParams(dimension_semantics=("parallel", "parallel"),
                                             vmem_limit_bytes=_vmem_limit(48)),
        name="swa_attention",
    )(sinks, q, k, v)


def _pack_bf16_pair(x):
    w = x.shape[1] // 2
    lo = pltpu.bitcast(x[:, :w].astype(BF16).astype(F32), jnp.int32)
    hi = pltpu.bitcast(x[:, w:].astype(BF16).astype(F32), jnp.int32)
    return lax.shift_right_logical(lo, jnp.int32(16)) | (hi & jnp.int32(-65536))


def _unpack_bf16_pair(p):
    return pltpu.bitcast(p << 16, F32), pltpu.bitcast(p & jnp.int32(-65536), F32)


def _route(n2, wrt, bias, run_ref):
    logits = lax.dot_general(wrt, n2, (((1,), (1,)), ((), ())), precision=HIGHEST, preferred_element_type=F32)
    rows = logits.shape[1]
    shape3 = (N_GROUPS, GROUP_SIZE, rows)
    scores3 = _sigmoid(logits[0:N_EXPERTS]).reshape(shape3)
    choice = scores3 + bias
    ji = lax.broadcasted_iota(jnp.int32, shape3, 1).astype(F32)
    m1 = jnp.max(choice, axis=1, keepdims=True)
    first = jnp.min(jnp.where(choice == m1, ji, float(GROUP_SIZE)), axis=1, keepdims=True)
    m2 = jnp.max(jnp.where(ji == first, -jnp.inf, choice), axis=1, keepdims=True)
    gs = m1 + m2
    gidx = lax.broadcasted_iota(jnp.int32, gs.shape, 0).astype(F32)
    gsel = jnp.zeros_like(gs)
    for _ in range(TOPK_GROUPS):
        mx = jnp.max(gs, axis=0, keepdims=True)
        pick = gidx == jnp.min(jnp.where(gs == mx, gidx, float(N_GROUPS)), axis=0, keepdims=True)
        gsel = jnp.where(pick, 1.0, gsel)
        gs = jnp.where(pick, -jnp.inf, gs)
    cand = jnp.where(gsel > 0.0, choice, -jnp.inf).reshape(N_EXPERTS, rows)
    scores = scores3.reshape(N_EXPERTS, rows)
    ei = lax.broadcasted_iota(jnp.int32, (N_EXPERTS, rows), 0).astype(F32)
    picks = []
    for _ in range(TOP_K):
        mx = jnp.max(cand, axis=0, keepdims=True)
        pick = ei == jnp.min(jnp.where(cand == mx, ei, float(N_EXPERTS)), axis=0, keepdims=True)
        picks.append(pick)
        cand = jnp.where(pick, -jnp.inf, cand)
    esel = jnp.zeros((N_EXPERTS, rows), F32)
    for pick in picks:
        esel = jnp.where(pick, 1.0, esel)
    before = jnp.where(lax.broadcasted_iota(jnp.int32, (rows, rows), 0) < lax.broadcasted_iota(jnp.int32, (rows, rows), 1),
                       1.0, 0.0).astype(BF16)
    slot = jnp.dot(esel.astype(BF16), before, preferred_element_type=F32) + run_ref[...]
    run_ref[...] += jnp.sum(esel, axis=1, keepdims=True)
    sc = [jnp.sum(jnp.where(pick, scores, 0.0), axis=0, keepdims=True) for pick in picks]
    tot = sc[0]
    for x in sc[1:]:
        tot = tot + x
    k8 = lax.broadcasted_iota(jnp.int32, (8, rows), 0)
    kw = lax.broadcasted_iota(jnp.int32, (GATE_W, rows), 0)
    eid = jnp.zeros((8, rows), jnp.int32)
    rank = jnp.zeros((8, rows), jnp.int32)
    wk = jnp.zeros((GATE_W, rows), F32)
    for k, pick in enumerate(picks):
        e_k = jnp.sum(jnp.where(pick, ei, 0.0), axis=0, keepdims=True).astype(jnp.int32)
        r_k = jnp.sum(jnp.where(pick, slot, 0.0), axis=0, keepdims=True).astype(jnp.int32)
        eid = jnp.where(k8 == k, e_k, eid)
        rank = jnp.where(k8 == k, r_k, rank)
        wk = jnp.where(kw == k, sc[k] * (ROUTED_SCALE / tot), wk)
    return eid, rank, wk.T


def _mixer_tail(o, h_ref, m, gffn_ref, wrt_ref, bias_ref, hn_ref, n2_ref, eid_ref, rank_ref, w_ref, cnt_ref, run_ref):
    @pl.when((pl.program_id(0) == 0) & (pl.program_id(1) == 0))
    def _():
        run_ref[...] = jnp.zeros_like(run_ref)

    hn = h_ref[0] + m[2:3] * o
    hn_ref[0] = hn
    n2 = _norm_mod(hn, gffn_ref[...], m[3:4], m[4:5])
    n2_ref[0] = _pack_bf16_pair(n2)
    eid, rank, wcols = _route(n2, wrt_ref[...], bias_ref[...], run_ref)
    eid_ref[0] = eid
    rank_ref[0] = rank
    w_ref[0] = wcols
    cnt_ref[...] = run_ref[...]


def _attn_out_kernel(a_ref, b_ref, h_ref, mods_ref, wo_ref, gffn_ref, wrt_ref, bias_ref,
                     hn_ref, n2_ref, eid_ref, rank_ref, w_ref, cnt_ref, run_ref):
    wa = MLA_HEADS * MLA_V
    o = _dot(a_ref[0], wo_ref[0:wa, :]) + _dot(b_ref[0], wo_ref[wa:, :])
    _mixer_tail(o, h_ref, mods_ref[0, 0], gffn_ref, wrt_ref, bias_ref, hn_ref, n2_ref, eid_ref, rank_ref, w_ref,
                cnt_ref, run_ref)


def _tail_outs(b, l, d):
    tl = TOKEN_TILE
    nt = l // tl
    sds = jax.ShapeDtypeStruct
    tok = lambda w: pl.BlockSpec((1, tl, w), lambda i, j: (i, j, 0))
    blk = pl.BlockSpec((1, 8, tl), lambda i, j: (i * nt + j, 0, 0))
    shapes = [sds((b, l, d), F32), sds((b, l, d // 2), jnp.int32), sds((b * nt, 8, tl), jnp.int32),
              sds((b * nt, 8, tl), jnp.int32), sds((b, l, GATE_W), F32), sds((N_EXPERTS, 1), F32)]
    specs = [tok(d), tok(d // 2), blk, blk, tok(GATE_W), pl.BlockSpec((N_EXPERTS, 1), lambda i, j: (0, 0))]
    return shapes, specs


def _attn_out(a, bm, h, mods, wo, gffn, wrt, bias, nct):
    b, l, d = h.shape
    tl = TOKEN_TILE
    tok = lambda w: pl.BlockSpec((1, tl, w), lambda i, j: (i, j, 0))
    full = lambda x: pl.BlockSpec(x.shape, lambda i, j: (0,) * x.ndim)
    shapes, specs = _tail_outs(b, l, d)
    return pl.pallas_call(
        _attn_out_kernel,
        out_shape=shapes,
        grid=(b, l // tl),
        in_specs=[tok(a.shape[2]), tok(bm.shape[2]), tok(d),
                  pl.BlockSpec((1, 1, N_MODS, d), lambda i, j: (i, jnp.where(j < nct, 0, 1), 0, 0)),
                  full(wo), full(gffn), full(wrt), full(bias)],
        out_specs=specs,
        scratch_shapes=[pltpu.VMEM((N_EXPERTS, 1), F32)],
        compiler_params=pltpu.CompilerParams(dimension_semantics=("arbitrary", "arbitrary"),
                                             vmem_limit_bytes=_vmem_limit(40)),
        name="attn_out",
    )(a, bm, h, mods, wo, gffn, wrt, bias)


def _moe_dest_kernel(off_ref, eid_ref, rank_ref, dest_ref):
    eid = eid_ref[...]
    dest = rank_ref[...]
    for e in range(N_EXPERTS):
        dest = dest + jnp.where(eid == e, off_ref[e], 0)
    dest_ref[...] = dest


def _moe_dest(off, eid, rank):
    return pl.pallas_call(
        _moe_dest_kernel,
        out_shape=jax.ShapeDtypeStruct(eid.shape, jnp.int32),
        in_specs=[pl.BlockSpec(memory_space=pltpu.SMEM),
                  pl.BlockSpec(eid.shape, lambda: (0, 0, 0)), pl.BlockSpec(eid.shape, lambda: (0, 0, 0))],
        out_specs=pl.BlockSpec(eid.shape, lambda: (0, 0, 0)),
        name="moe_dest",
    )(off, eid, rank)


def _sc_mesh():
    return plsc.VectorSubcoreMesh(core_axis_name="c", subcore_axis_name="s",
                                  num_cores=V7X_SC_CORES, num_subcores=V7X_SC_SUBCORES)


def _sc_chunk(rows_per_worker):
    return max(c for c in range(8, SC_MAX_CHUNK + 1, 8) if rows_per_worker % c == 0)


def _sc_dispatch(xp, dest, p_rows):
    t, w = xp.shape
    tpw = t // V7X_SC_WORKERS
    ch = _sc_chunk(tpw)

    @functools.partial(
        pl.kernel, mesh=_sc_mesh(), out_type=jax.ShapeDtypeStruct((p_rows, w), xp.dtype),
        scratch_types=[pltpu.VMEM((ch, w), xp.dtype)] + [pltpu.VMEM((ch,), jnp.int32)] * TOP_K
        + [pltpu.SemaphoreType.DMA, pltpu.SemaphoreType.DMA],
        name="moe_dispatch")
    def run(x_hbm, dest_hbm, out_hbm, rows_v, *rest):
        idx, (sem_i, sem_o) = rest[:TOP_K], rest[TOP_K:]
        base = (lax.axis_index("s") * V7X_SC_CORES + lax.axis_index("c")) * tpw

        @pl.loop(0, tpw // ch)
        def _(i):
            t0 = base + i * ch
            loads = [pltpu.async_copy(dest_hbm.at[k, pl.ds(t0, ch)], idx[k], sem_i) for k in range(TOP_K)]
            pltpu.sync_copy(x_hbm.at[pl.ds(t0, ch)], rows_v)
            for c in loads:
                c.wait()
            puts = [pltpu.async_copy(rows_v, out_hbm.at[idx[k]], sem_o) for k in range(TOP_K)]
            for c in puts:
                c.wait()

    return run(xp, dest)


def _sc_gather(ys, dest, t):
    w = ys.shape[1]
    tpw = t // V7X_SC_WORKERS
    ch = _sc_chunk(tpw)

    @functools.partial(
        pl.kernel, mesh=_sc_mesh(), out_type=jax.ShapeDtypeStruct((TOP_K, t, w), ys.dtype),
        scratch_types=[pltpu.VMEM((ch, w), ys.dtype)] * 2 + [pltpu.VMEM((ch,), jnp.int32)] * TOP_K
        + [pltpu.SemaphoreType.DMA] * 5,
        name="moe_gather")
    def run(y_hbm, dest_hbm, out_hbm, rows_a, rows_b, *rest):
        idx, (sem_i, sem_ga, sem_gb, sem_wa, sem_wb) = rest[:TOP_K], rest[TOP_K:]
        rows, sem_g, sem_w = (rows_a, rows_b), (sem_ga, sem_gb), (sem_wa, sem_wb)
        base = (lax.axis_index("s") * V7X_SC_CORES + lax.axis_index("c")) * tpw

        @pl.loop(0, tpw // ch)
        def _(i):
            t0 = base + i * ch
            loads = [pltpu.async_copy(dest_hbm.at[k, pl.ds(t0, ch)], idx[k], sem_i) for k in range(TOP_K)]
            for c in loads:
                c.wait()
            gets, puts = [None] * TOP_K, [None] * TOP_K
            gets[0] = pltpu.async_copy(y_hbm.at[idx[0]], rows[0], sem_g[0])
            for k in range(TOP_K):
                if k + 1 < TOP_K:
                    if k >= 1:
                        puts[k - 1].wait()
                    gets[k + 1] = pltpu.async_copy(y_hbm.at[idx[k + 1]], rows[(k + 1) % 2], sem_g[(k + 1) % 2])
                gets[k].wait()
                puts[k] = pltpu.async_copy(rows[k % 2], out_hbm.at[k, pl.ds(t0, ch)], sem_w[k % 2])
            puts[TOP_K - 2].wait()
            puts[TOP_K - 1].wait()

    return run(ys, dest)


def _moe_expert_kernel(te_ref, nv_ref, x_ref, wg_ref, wu_ref, wd_ref, y_ref):
    @pl.when(pl.program_id(0) < nv_ref[0])
    def _():
        lo, hi = _unpack_bf16_pair(x_ref[...])
        half = lo.shape[1]
        hg = _dot(lo, wg_ref[0, 0, 0:half, :]) + _dot(hi, wg_ref[0, 0, half:, :])
        hu = _dot(lo, wu_ref[0, 0, 0:half, :]) + _dot(hi, wu_ref[0, 0, half:, :])
        y_ref[...] = _pack_bf16_pair(_dot(_silu(hg) * hu, wd_ref[0, 0]))


def _moe_experts(tile_expert, n_valid, xs, wg, wu, wd, layer):
    p_rows, w = xs.shape
    tm = MOE_ROW_TILE
    _, _, d, f = wg.shape
    wspec = lambda shp: pl.BlockSpec((1, 1) + shp, lambda i, te, nv: (layer, te[i], 0, 0))
    return pl.pallas_call(
        _moe_expert_kernel,
        out_shape=jax.ShapeDtypeStruct((p_rows, w), xs.dtype),
        grid_spec=pltpu.PrefetchScalarGridSpec(
            num_scalar_prefetch=2, grid=(p_rows // tm,),
            in_specs=[pl.BlockSpec((tm, w), lambda i, te, nv: (jnp.minimum(i, nv[0] - 1), 0)),
                      wspec((d, f)), wspec((d, f)), wspec((f, d))],
            out_specs=pl.BlockSpec((tm, w), lambda i, te, nv: (i, 0))),
        compiler_params=pltpu.CompilerParams(dimension_semantics=("arbitrary",),
                                             vmem_limit_bytes=_vmem_limit(32)),
        name="moe_experts",
    )(tile_expert, n_valid, xs, wg, wu, wd)


def _moe_combine_kernel(yg_ref, w_ref, xp_ref, sg_ref, su_ref, sd_ref, h_ref, mods_ref, gfin_ref, o_ref, *, final_norm):
    xlo, xhi = _unpack_bf16_pair(xp_ref[0])
    half = xlo.shape[1]
    hs = (_silu(_dot(xlo, sg_ref[0, 0:half, :]) + _dot(xhi, sg_ref[0, half:, :]))
          * (_dot(xlo, su_ref[0, 0:half, :]) + _dot(xhi, su_ref[0, half:, :])))
    acc = _dot(hs, sd_ref[0])
    lo = acc[:, :half]
    hi = acc[:, half:]
    w = w_ref[0]
    for k in range(TOP_K):
        ylo, yhi = _unpack_bf16_pair(yg_ref[k, 0])
        wk = w[:, k:k + 1]
        lo = lo + wk * ylo
        hi = hi + wk * yhi
    y = h_ref[0] + mods_ref[0, 0, N_MODS - 1:N_MODS, :] * jnp.concatenate([lo, hi], axis=1)
    if final_norm:
        y = _rms(y, gfin_ref[...])
    o_ref[0] = y


def _moe_combine(yg, wcols, xp, sg, su, sd, h, mods, gfin, nct, layer, tile0, final_norm):
    b, l, d = h.shape
    tl = TOKEN_TILE
    tok = lambda w: pl.BlockSpec((1, tl, w), lambda i, j: (i, j + tile0, 0))
    lay = lambda x: pl.BlockSpec((1,) + x.shape[1:], lambda i, j: (layer,) + (0,) * (x.ndim - 1))
    return pl.pallas_call(
        functools.partial(_moe_combine_kernel, final_norm=final_norm),
        out_shape=jax.ShapeDtypeStruct((b, l - tile0 * tl, d), F32),
        grid=(b, l // tl - tile0),
        in_specs=[pl.BlockSpec((TOP_K, 1, tl, d // 2), lambda i, j: (0, i, j + tile0, 0)), tok(GATE_W), tok(d // 2),
                  lay(sg), lay(su), lay(sd), tok(d),
                  pl.BlockSpec((1, 1, N_MODS, d), lambda i, j: (i, jnp.where(j + tile0 < nct, 0, 1), 0, 0)),
                  pl.BlockSpec(gfin.shape, lambda i, j: (0, 0))],
        out_specs=pl.BlockSpec((1, tl, d), lambda i, j: (i, j, 0)),
        compiler_params=pltpu.CompilerParams(dimension_semantics=("parallel", "parallel"),
                                             vmem_limit_bytes=_vmem_limit(40)),
        name="moe_combine",
    )(yg, wcols, xp, sg, su, sd, h, mods, gfin)


def _moe_sparse(n2p, eid, rank, wcols, counts, h, mods, wg, wu, wd, sg, su, sd, gfin, nct, layer, last):
    b, l, d = h.shape
    t = b * l
    tm = MOE_ROW_TILE
    n_tiles = -(-(TOP_K * t + N_EXPERTS * (tm - 1)) // tm)
    tiles_e = (counts.reshape(N_EXPERTS).astype(jnp.int32) + (tm - 1)) // tm
    tile_end = jnp.cumsum(tiles_e)
    off = (tile_end - tiles_e) * tm
    n_valid = tile_end[-1:]
    tile_id = jnp.minimum(jnp.arange(n_tiles, dtype=jnp.int32), n_valid - 1)
    tile_expert = jnp.sum((tile_end[None, :] <= tile_id[:, None]).astype(jnp.int32), axis=1)
    dest = _moe_dest(off, eid, rank).transpose(1, 0, 2).reshape(8, t)
    xs = _sc_dispatch(n2p.reshape(t, d // 2), dest, n_tiles * tm)
    ys = _moe_experts(tile_expert, n_valid, xs, wg, wu, wd, layer)
    yg = _sc_gather(ys, dest, t).reshape(TOP_K, b, l, d // 2)
    return _moe_combine(yg, wcols, n2p, sg, su, sd, h, mods, gfin, nct, layer, nct if last else 0, last)


def _rwkv_proj_kernel(h_ref, hp_ref, hx_ref, mods_ref, g_ref, mu_ref, wr_ref, wk_ref, wv_ref, g1_ref, g2_ref,
                      w1_ref, w2_ref, a1_ref, a2_ref, w0_ref, a0_ref, kk_ref, ka_ref, rk_ref, bd_ref,
                      r_out, v_out, kk_out, g_out, km_out, b_out, lw_out, bonus_out, *, nct):
    j = pl.program_id(1)
    nt = pl.num_programs(1)
    m = mods_ref[0, 0]
    g = g_ref[...]
    n = _norm_mod(h_ref[0], g, m[0:1], m[1:2])
    tl, d = n.shape
    seg_first = (j == 0) | (j == nct)
    seg_last = (j == nct - 1) | (j == nt - 1)
    n_prev = _norm_mod(hp_ref[0], g, m[0:1], m[1:2])[7:8] * jnp.where(seg_first, 0.0, 1.0)
    n_next = _norm_mod(hx_ref[0], g, m[0:1], m[1:2])[0:1] * jnp.where(seg_last, 0.0, 1.0)
    row = lax.broadcasted_iota(jnp.int32, (tl, 1), 0)
    prev = jnp.where(row == 0, n_prev, pltpu.roll(n, 1, axis=0))
    nxt = jnp.where(row == tl - 1, n_next, pltpu.roll(n, tl - 1, axis=0))
    lane = lax.broadcasted_iota(jnp.int32, (1, d), 1)
    xx = jnp.where(lane < d // 2, prev, nxt) - n
    mu = mu_ref[...]
    xr, xw, xk, xv, xa, xg = [n + xx * mu[i:i + 1] for i in range(6)]
    r = _dot(xr, wr_ref[...])
    k = _dot(xk, wk_ref[...])
    v = _dot(xv, wv_ref[...])
    g_out[0] = _dot(_sigmoid(_dot(xg, g1_ref[...])), g2_ref[...]).astype(g_out.dtype)
    tw = jnp.tanh(_dot(xw, w1_ref[...]))
    ta = _dot(xa, a1_ref[...])
    bd = bd_ref[...]
    kk = k * kk_ref[...]
    kk = kk / jnp.maximum(jnp.sqrt(_head_sum(kk * kk, bd)), 1e-12)
    r_out[0] = r.astype(r_out.dtype)
    v_out[0] = v.astype(v_out.dtype)
    kk_out[0] = kk.astype(kk_out.dtype)
    bonus = jnp.zeros_like(v)
    for dr in range(2):
        zw = w0_ref[dr:dr + 1, :] + _dot(tw, w2_ref[dr])
        lw_out[dr, 0] = -jnp.exp(-0.5) * _sigmoid(zw)
        a = _sigmoid(a0_ref[dr:dr + 1, :] + _dot(ta, a2_ref[dr]))
        km = k * (1.0 + (a - 1.0) * ka_ref[...])
        km_out[dr, 0] = km.astype(km_out.dtype)
        b_out[dr, 0] = (kk * a).astype(b_out.dtype)
        bonus = bonus + _head_sum(r * km * rk_ref[...], bd) * v
    bonus_out[0] = bonus


def _rwkv_proj(h, mods, g, mu, wr, wk, wv, g1, g2, w1, w2, a1, a2, w0, a0, kk, ka, rk, bd, nct):
    b, l, d = h.shape
    tl = TOKEN_TILE
    nb8 = l // 8
    tok = pl.BlockSpec((1, tl, d), lambda i, j: (i, j, 0))
    tok2 = pl.BlockSpec((2, 1, tl, d), lambda i, j: (0, i, j, 0))
    full = lambda x: pl.BlockSpec(x.shape, lambda i, j: (0,) * x.ndim)
    sds = jax.ShapeDtypeStruct
    return pl.pallas_call(
        functools.partial(_rwkv_proj_kernel, nct=nct),
        out_shape=[sds((b, l, d), BF16), sds((b, l, d), BF16), sds((b, l, d), BF16), sds((b, l, d), BF16),
                   sds((2, b, l, d), BF16), sds((2, b, l, d), BF16), sds((2, b, l, d), F32), sds((b, l, d), F32)],
        grid=(b, l // tl),
        in_specs=[tok,
                  pl.BlockSpec((1, 8, d), lambda i, j: (i, jnp.maximum(j * (tl // 8) - 1, 0), 0)),
                  pl.BlockSpec((1, 8, d), lambda i, j: (i, jnp.minimum((j + 1) * (tl // 8), nb8 - 1), 0)),
                  pl.BlockSpec((1, 1, N_MODS, d), lambda i, j: (i, jnp.where(j < nct, 0, 1), 0, 0)),
                  full(g), full(mu), full(wr), full(wk), full(wv), full(g1), full(g2), full(w1), full(w2),
                  full(a1), full(a2), full(w0), full(a0), full(kk), full(ka), full(rk), full(bd)],
        out_specs=[tok, tok, tok, tok, tok2, tok2, tok2, tok],
        compiler_params=pltpu.CompilerParams(dimension_semantics=("parallel", "parallel"),
                                             vmem_limit_bytes=_vmem_limit(56)),
        name="rwkv_proj",
    )(h, h, h, mods, g, mu, wr, wk, wv, g1, g2, w1, w2, a1, a2, w0, a0, kk, ka, rk, bd)


def _wkv_kernel(r_ref, v_ref, kk_ref, km_ref, b_ref, lw_ref, y_ref, st_ref):
    c = WKV_CHUNK
    w = WKV_PAIR
    rev = pl.program_id(0)
    sign = 1 - 2 * rev

    @pl.when(pl.program_id(2) == 0)
    def _():
        st_ref[...] = jnp.zeros_like(st_ref)

    ti = lax.broadcasted_iota(jnp.int32, (c, c), 0)
    si = lax.broadcasted_iota(jnp.int32, (c, c), 1)
    tri = jnp.where((si - ti) * sign <= 0, 1.0, 0.0).astype(F32)
    nsub = WKV_CHUNKS_PER_STEP
    subs = [pl.ds(pl.multiple_of(jnp.where(rev == 0, s, nsub - 1 - s) * c, c), c) for s in range(nsub)]
    rt, kt, kh, bh, v32, e_mid = [], [], [], [], [], []
    for rows in subs:
        lw = lw_ref[0, 0, rows, :]
        l_incl = jnp.dot(tri, lw, precision=HIGHEST, preferred_element_type=F32)
        mid = 0.5 * jnp.sum(lw, axis=0, keepdims=True)
        e_neg = jnp.exp(mid - l_incl)
        e_mid.append(jnp.exp(mid))
        rt.append(r_ref[0, rows, :].astype(F32) * jnp.exp(l_incl - mid))
        kt.append(kk_ref[0, rows, :].astype(F32) * jnp.exp(l_incl - lw - mid))
        kh.append(km_ref[0, 0, rows, :].astype(F32) * e_neg)
        bh.append(b_ref[0, 0, rows, :].astype(F32) * e_neg)
        v32.append(v_ref[0, rows, :].astype(F32))

    ri = lax.broadcasted_iota(jnp.int32, (w, w), 0)
    ci = lax.broadcasted_iota(jnp.int32, (w, w), 1)
    same = (ri // c) == (ci // c)
    dlt = (ci % c - ri % c) * sign
    strict = same & (dlt < 0)
    incl = same & (dlt <= 0)
    eye = jnp.where(ri == ci, 1.0, 0.0).astype(F32)
    lane = lax.broadcasted_iota(jnp.int32, (1, w), 1)
    h0 = lane < RWKV_HEAD

    def rows2(x):
        return jnp.concatenate([jnp.where(h0, x, 0.0), jnp.where(h0, 0.0, x)], axis=0)

    def fold(x):
        return x[:c] + x[c:]

    npair = st_ref.shape[0]
    items = [(s, slice(p * w, (p + 1) * w)) for s in range(nsub) for p in range(npair)]
    n = range(len(items))
    em = [e_mid[s][:, sl] for s, sl in items]
    g = [_dot_nt(jnp.concatenate([rows2(kt[s][:, sl]), rows2(rt[s][:, sl])], axis=0),
                 jnp.concatenate([kh[s][:, sl], kh[s][:, sl], bh[s][:, sl], bh[s][:, sl]], axis=0)) for s, sl in items]
    a_kk = [jnp.where(strict, x[:w, :w], 0.0) for x in g]
    a_rk = [jnp.where(incl, x[w:, :w], 0.0) for x in g]
    a_rb = [jnp.where(incl, x[w:, w:], 0.0) for x in g]
    vi = [v32[s][:, sl] for s, sl in items]
    v_rows = [rows2(x) for x in vi]
    r_pre = [_dot(a_kk[i], v_rows[i]) for i in n]
    m = [jnp.where(strict, -x[:w, w:], 0.0) for x in g]
    tinv = [eye + x for x in m]
    m = [_dot(x, x) for x in m]
    for _ in range(c.bit_length() - 3):
        both = [_dot(jnp.concatenate([tinv[i], m[i]], axis=0), m[i]) for i in n]
        tinv = [tinv[i] + both[i][:w] for i in n]
        m = [x[w:] for x in both]
    tinv = [tinv[i] + _dot(tinv[i], m[i]) for i in n]
    sol = [_dot(tinv[i], jnp.concatenate([r_pre[i], rows2(kt[s][:, sl] * em[i])], axis=1))
           for i, (s, sl) in enumerate(items)]
    u_rows = [x[:, :w] for x in sol]
    kq_rows = [x[:, w:] for x in sol]
    y_pre = [fold(_dot(jnp.concatenate([a_rk[i], -a_rb[i]], axis=1),
                       jnp.concatenate([v_rows[i], u_rows[i]], axis=0))) for i in n]
    r_eff = [rt[s][:, sl] * em[i] - fold(_dot(a_rb[i], kq_rows[i])) for i, (s, sl) in enumerate(items)]
    bbar = [bh[s][:, sl] * em[i] for i, (s, sl) in enumerate(items)]
    kbar = [kh[s][:, sl] * em[i] for i, (s, sl) in enumerate(items)]
    mmat = [eye * (em[i] * em[i]) - jnp.where(same, _dot_tn(fold(kq_rows[i]), bbar[i]), 0.0) for i in n]
    s_pre = [jnp.where(same, _dot_tn(jnp.concatenate([vi[i], -fold(u_rows[i])], axis=0),
                                     jnp.concatenate([kbar[i], bbar[i]], axis=0)), 0.0) for i in n]
    st = [st_ref[p] for p in range(npair)]
    for i, (s, sl) in enumerate(items):
        p = i % npair
        y_ref[0, 0, subs[s], sl] = _dot_nt(r_eff[i], st[p]) + y_pre[i]
        hi = st[p].astype(BF16)
        lo = (st[p] - hi.astype(F32)).astype(BF16)
        mb = mmat[i].astype(BF16)
        st[p] = (jnp.dot(hi, mb, preferred_element_type=F32) + jnp.dot(lo, mb, preferred_element_type=F32)
                 + s_pre[i])
    for p in range(npair):
        st_ref[p] = st[p]


def _wkv(r, v, kk, km, bv, lw, lc):
    b, l, d = r.shape
    c = WKV_CHUNK * WKV_CHUNKS_PER_STEP
    ncc = lc // c
    nlc = (l - lc) // c

    def chunk(dr, i):
        return jnp.where(dr == 0, i, jnp.where(i < ncc, ncc - 1 - i, nlc + 2 * ncc - 1 - i))

    shared = pl.BlockSpec((1, c, d), lambda dr, bi, i: (bi, chunk(dr, i), 0))
    per_dir = pl.BlockSpec((1, 1, c, d), lambda dr, bi, i: (dr, bi, chunk(dr, i), 0))
    return pl.pallas_call(
        _wkv_kernel,
        out_shape=jax.ShapeDtypeStruct((2, b, l, d), F32),
        grid=(2, b, l // c),
        in_specs=[shared, shared, shared, per_dir, per_dir, per_dir],
        out_specs=per_dir,
        scratch_shapes=[pltpu.VMEM((d // WKV_PAIR, WKV_PAIR, WKV_PAIR), F32)],
        compiler_params=pltpu.CompilerParams(dimension_semantics=("parallel", "parallel", "arbitrary"),
                                             vmem_limit_bytes=_vmem_limit(32)),
        name="wkv7_chunked",
    )(r, v, kk, km, bv, lw)


def _rwkv_out_kernel(y_ref, bonus_ref, g_ref, lnw_ref, lnb_ref, wo_ref, bd_ref, h_ref, mods_ref, gffn_ref,
                     wrt_ref, bias_ref, hn_ref, n2_ref, eid_ref, rank_ref, w_ref, cnt_ref, run_ref):
    y = y_ref[0, 0] + y_ref[1, 0]
    bd = bd_ref[...]
    mean = _head_sum(y, bd) * (1.0 / RWKV_HEAD)
    yc = y - mean
    var = _head_sum(yc * yc, bd) * (1.0 / RWKV_HEAD)
    yn = yc * lax.rsqrt(var + GN_EPS) * lnw_ref[...] + lnb_ref[...]
    out = (yn + bonus_ref[0]) * g_ref[0].astype(F32)
    _mixer_tail(_dot(out, wo_ref[...]), h_ref, mods_ref[0, 0], gffn_ref, wrt_ref, bias_ref, hn_ref, n2_ref,
                eid_ref, rank_ref, w_ref, cnt_ref, run_ref)


def _rwkv_out(y, bonus, g, lnw, lnb, wo, bd, h, mods, gffn, wrt, bias, nct):
    b, l, d = h.shape
    tl = TOKEN_TILE
    tok = lambda w: pl.BlockSpec((1, tl, w), lambda i, j: (i, j, 0))
    full = lambda x: pl.BlockSpec(x.shape, lambda i, j: (0,) * x.ndim)
    shapes, specs = _tail_outs(b, l, d)
    return pl.pallas_call(
        _rwkv_out_kernel,
        out_shape=shapes,
        grid=(b, l // tl),
        in_specs=[pl.BlockSpec((2, 1, tl, d), lambda i, j: (0, i, j, 0)), tok(d), tok(d),
                  full(lnw), full(lnb), full(wo), full(bd), tok(d),
                  pl.BlockSpec((1, 1, N_MODS, d), lambda i, j: (i, jnp.where(j < nct, 0, 1), 0, 0)),
                  full(gffn), full(wrt), full(bias)],
        out_specs=specs,
        scratch_shapes=[pltpu.VMEM((N_EXPERTS, 1), F32)],
        compiler_params=pltpu.CompilerParams(dimension_semantics=("arbitrary", "arbitrary"),
                                             vmem_limit_bytes=_vmem_limit(40)),
        name="rwkv_out",
    )(y, bonus, g, lnw, lnb, wo, bd, h, mods, gffn, wrt, bias)


def _rope_table(n_lat, n_ctx):
    dim = SWA_HEAD_DIM
    nf = dim // 4
    inv = ROPE_THETA ** (-jnp.arange(nf, dtype=F32) / nf)
    row = jnp.repeat(jnp.arange(n_lat // GRID_W, dtype=F32), GRID_W)
    col = jnp.tile(jnp.arange(GRID_W, dtype=F32), n_lat // GRID_W)
    ar = row[:, None] * inv
    ac = col[:, None] * inv
    ang = jnp.concatenate([ar, ar, ac, ac], axis=-1)
    cos = jnp.concatenate([jnp.ones((n_ctx, dim), F32), jnp.cos(ang)], axis=0)
    sin = jnp.concatenate([jnp.zeros((n_ctx, dim), F32), jnp.sin(ang)], axis=0)
    return jnp.tile(cos, (1, 2)), jnp.tile(sin, (1, 2))


def _layout_attn_weights(w_in, w_uq, w_ukv):
    d = w_in.shape[0]
    s0 = MLA_Q_RANK
    s1 = s0 + MLA_KV_RANK
    s2 = s1 + MLA_ROPE
    s3 = s2 + SWA_HEADS * SWA_HEAD_DIM
    s4 = s3 + SWA_KV_HEADS * SWA_HEAD_DIM
    rep = lambda w: jnp.concatenate(
        [jnp.tile(w[:, g * SWA_HEAD_DIM:(g + 1) * SWA_HEAD_DIM], (1, SWA_GROUP)) for g in range(SWA_KV_HEADS)], axis=1)
    win = jnp.concatenate([w_in[:, :s1], w_in[:, s2:s3], rep(w_in[:, s3:s4]), rep(w_in[:, s4:]),
                           w_in[:, s1:s2], jnp.zeros((d, V7X_LANES - MLA_ROPE), w_in.dtype)], axis=1)
    qh = MLA_NOPE + MLA_ROPE
    pad = jnp.zeros((w_uq.shape[0], V7X_MXU_DIM - qh), w_uq.dtype)
    wuq = jnp.concatenate([jnp.concatenate([w_uq[:, h * qh:(h + 1) * qh], pad], axis=1) for h in range(MLA_HEADS)], axis=1)
    kvh = MLA_NOPE + MLA_V
    wuk = jnp.concatenate([w_ukv[:, h * kvh:h * kvh + MLA_NOPE] for h in range(MLA_HEADS)], axis=1)
    wuvt = jnp.concatenate([w_ukv[:, h * kvh + MLA_NOPE:(h + 1) * kvh] for h in range(MLA_HEADS)], axis=1).T
    return win.astype(BF16), wuq.astype(BF16), wuk.astype(BF16), wuvt.astype(BF16)


def _lora_pair(w_down, w_up):
    rank = w_down.shape[2]
    down = jnp.concatenate([w_down[0], w_down[1]], axis=1)
    z = jnp.zeros((rank, w_up.shape[2]), w_up.dtype)
    up = jnp.stack([jnp.concatenate([w_up[0], z], axis=0), jnp.concatenate([z, w_up[1]], axis=0)], axis=0)
    return down.astype(BF16), up.astype(BF16)


def _head_block_diag():
    i = jnp.arange(V7X_MXU_DIM) // RWKV_HEAD
    return (i[:, None] == i[None, :]).astype(BF16)


def kernel(x, c, ctx, c_ctx, ada_w, ada_b, norm_mix, norm_ffn, norm_final, attn_w_in, attn_q_norm, attn_kv_norm, attn_w_uq, attn_w_ukv, attn_sinks, attn_w_o, rwkv_mu, rwkv_w_r, rwkv_w_k, rwkv_w_v, rwkv_w_o, rwkv_g1, rwkv_g2, rwkv_w0, rwkv_w1, rwkv_w2, rwkv_a0, rwkv_a1, rwkv_a2, rwkv_k_k, rwkv_k_a, rwkv_r_k, rwkv_ln_w, rwkv_ln_b, moe_router, moe_bias, moe_w_gate, moe_w_up, moe_w_down, moe_ws_gate, moe_ws_up, moe_ws_down):
    bsz, s, d = x.shape
    lc = ctx.shape[1]
    l = lc + s
    depth = ada_w.shape[0]
    nct = lc // TOKEN_TILE
    assert lc % TOKEN_TILE == 0 and s % TOKEN_TILE == 0 and s >= SWA_BAND
    assert lc % (WKV_CHUNK * WKV_CHUNKS_PER_STEP) == 0
    assert d % V7X_MXU_DIM == 0 and WKV_CHUNK * 2 == V7X_LANES
    assert (bsz * l) % (8 * V7X_SC_WORKERS) == 0

    h = jnp.concatenate([ctx, x], axis=1)
    cos, sin = _rope_table(s, lc)
    bd = _head_block_diag()
    rows = -(-(bsz + 1) // 8) * 8
    cc = jnp.concatenate([c, c_ctx[None, :], jnp.zeros((rows - bsz - 1, d), F32)], axis=0)
    row2 = lambda a: a.reshape(1, -1)

    for li in range(depth):
        with_ctx = li < depth - 1
        i = li // 2
        ada = _ada_mods(cc, ada_w, ada_b, li)
        mods = jnp.stack([jnp.broadcast_to(ada[bsz].reshape(1, N_MODS, d), (bsz, N_MODS, d)),
                          ada[:bsz].reshape(bsz, N_MODS, d)], axis=1)
        wrt = jnp.concatenate([moe_router[li].T, jnp.zeros((GATE_W - N_EXPERTS, d), F32)], axis=0)
        bias = moe_bias[li].reshape(N_GROUPS, GROUP_SIZE, 1)
        if li % 2 == 0:
            win, wuq, wuk, wuvt = _layout_attn_weights(attn_w_in[i], attn_w_uq[i], attn_w_ukv[i])
            q, k, vt, qs, ks, vs = _attn_proj(h, mods, row2(norm_mix[li]), win, row2(attn_q_norm[i]),
                                              row2(attn_kv_norm[i]), wuq, wuk, wuvt, cos, sin, nct)
            a = _mla_attention(q, k, vt, lc, 0 if with_ctx else lc // MLA_Q_TILE)
            bm = _swa_attention(attn_sinks[i], qs, ks, vs, lc, 0 if with_ctx else lc // SWA_Q_TILE)
            tail = _attn_out(a, bm, h, mods, attn_w_o[i].astype(BF16), row2(norm_ffn[li]), wrt, bias, nct)
        else:
            w1, w2 = _lora_pair(rwkv_w1[i], rwkv_w2[i])
            a1, a2 = _lora_pair(rwkv_a1[i], rwkv_a2[i])
            r, v, kk, g, km, bv, lw, bonus = _rwkv_proj(
                h, mods, row2(norm_mix[li]), rwkv_mu[i], rwkv_w_r[i].astype(BF16), rwkv_w_k[i].astype(BF16),
                rwkv_w_v[i].astype(BF16), rwkv_g1[i].astype(BF16), rwkv_g2[i].astype(BF16), w1, w2, a1, a2,
                rwkv_w0[i], rwkv_a0[i], row2(rwkv_k_k[i]), row2(rwkv_k_a[i]), row2(rwkv_r_k[i]), bd, nct)
            y = _wkv(r, v, kk, km, bv, lw, lc)
            tail = _rwkv_out(y, bonus, g, row2(rwkv_ln_w[i]), row2(rwkv_ln_b[i]), rwkv_w_o[i].astype(BF16),
                             bd, h, mods, row2(norm_ffn[li]), wrt, bias, nct)
        h, n2p, eid, rank, wcols, counts = tail
        h = _moe_sparse(n2p, eid, rank, wcols, counts, h, mods, moe_w_gate, moe_w_up, moe_w_down,
                        moe_ws_gate, moe_ws_up, moe_ws_down, row2(norm_final), nct, li, li == depth - 1)
    return h
```

```python
import functools

import jax
import jax.numpy as jnp
from jax import lax
from jax.experimental import pallas as pl
from jax.experimental.pallas import tpu as pltpu
from jax.experimental.pallas import tpu_sc as plsc

F32 = jnp.float32
BF16 = jnp.bfloat16
HIGHEST = lax.Precision.HIGHEST

GRID_W = 64
NORM_EPS = 1e-6
ROPE_THETA = 10000.0
NEG_INF = -1e30
N_MODS = 6

MLA_HEADS = 4
MLA_Q_RANK = 384
MLA_KV_RANK = 256
MLA_NOPE = 128
MLA_ROPE = 64
MLA_V = 128

SWA_HEADS = 8
SWA_KV_HEADS = 2
SWA_GROUP = SWA_HEADS // SWA_KV_HEADS
SWA_HEAD_DIM = 64
WINDOW = 128

RWKV_HEAD = 64
DECAY_LORA = 64
ICLR_LORA = 64
GATE_LORA = 128
GN_EPS = 64e-5

N_EXPERTS = 64
TOP_K = 6
N_GROUPS = 8
TOPK_GROUPS = 4
GROUP_SIZE = N_EXPERTS // N_GROUPS
ROUTED_SCALE = 2.5
GATE_W = 128

V7X_LANES = 128
V7X_MXU_DIM = 256
V7X_VMEM_BYTES = 64 * 1024 * 1024
V7X_SC_CORES = 2
V7X_SC_SUBCORES = 16
V7X_SC_WORKERS = V7X_SC_CORES * V7X_SC_SUBCORES

TOKEN_TILE = 256
MLA_Q_TILE = 256
MLA_HEADS_PER_STEP = 2
LOG2E = 1.4426950408889634
SWA_Q_TILE = 128
SWA_BAND = SWA_Q_TILE + 2 * WINDOW
WKV_CHUNK = 64
WKV_PAIR = 2 * RWKV_HEAD
WKV_CHUNKS_PER_STEP = 2
MOE_ROW_TILE = 512
MOE_TOKEN_SPLITS = 2
SC_MAX_CHUNK = 64


def _vmem_limit(mib):
    return min(mib * 1024 * 1024, V7X_VMEM_BYTES - 4 * 1024 * 1024)


def _dot(a, b):
    return jnp.dot(a.astype(BF16), b.astype(BF16), preferred_element_type=F32)


def _dot_nt(a, b):
    return lax.dot_general(a.astype(BF16), b.astype(BF16), (((1,), (1,)), ((), ())),
                           preferred_element_type=F32)


def _dot_tn(a, b):
    return lax.dot_general(a.astype(BF16), b.astype(BF16), (((0,), (0,)), ((), ())),
                           preferred_element_type=F32)


def _sigmoid(x):
    return 1.0 / (1.0 + jnp.exp(-x))


def _silu(x):
    return x * _sigmoid(x)


def _rms(x, g):
    return x * lax.rsqrt(jnp.mean(x * x, axis=-1, keepdims=True) + NORM_EPS) * g


def _norm_mod(x, g, shift, scale):
    return _rms(x, g) * (1.0 + scale) + shift


def _split_dot(x, w):
    hi = x.astype(BF16)
    lo = (x - hi.astype(F32)).astype(BF16)
    return (jnp.dot(hi, w, preferred_element_type=F32) + jnp.dot(lo, w, preferred_element_type=F32))


def _head_sum(x, bd):
    w = bd.shape[0]
    parts = [_split_dot(x[:, c * w:(c + 1) * w], bd) for c in range(x.shape[1] // w)]
    return jnp.concatenate(parts, axis=1)


def _ada_kernel(c_ref, w_ref, b_ref, o_ref):
    s = _silu(c_ref[...])
    o_ref[...] = jnp.dot(s, w_ref[0], precision=HIGHEST, preferred_element_type=F32) + b_ref[0]


def _ada_mods(cc, w, b, layer):
    rows, d = cc.shape
    depth, _, n = w.shape
    return pl.pallas_call(
        _ada_kernel,
        out_shape=jax.ShapeDtypeStruct((rows, n), F32),
        grid=(n // d,),
        in_specs=[pl.BlockSpec((rows, d), lambda i: (0, 0)),
                  pl.BlockSpec((1, d, d), lambda i: (layer, 0, i)),
                  pl.BlockSpec((1, 1, d), lambda i: (layer, 0, i))],
        out_specs=pl.BlockSpec((rows, d), lambda i: (0, i)),
        compiler_params=pltpu.CompilerParams(dimension_semantics=("parallel",),
                                             vmem_limit_bytes=_vmem_limit(32)),
        name="ada_mods",
    )(cc, w, b.reshape(depth, 1, n))


def _rope128(x, cos, sin, first_half):
    rot = jnp.where(first_half, -pltpu.roll(x, V7X_LANES - 16, axis=1), pltpu.roll(x, 16, axis=1))
    return x * cos + rot * sin


_C_CQ = 0
_C_CKV = _C_CQ + MLA_Q_RANK
_C_QS = _C_CKV + MLA_KV_RANK
_C_KS = _C_QS + SWA_HEADS * SWA_HEAD_DIM
_C_VS = _C_KS + SWA_KV_HEADS * V7X_MXU_DIM
_C_KR = _C_VS + SWA_KV_HEADS * V7X_MXU_DIM
_C_END = _C_KR + V7X_LANES
_SWA_W = SWA_KV_HEADS * V7X_MXU_DIM
_MLA_QK_W = MLA_HEADS * V7X_MXU_DIM


def _attn_proj_kernel(h_ref, mods_ref, g_ref, win_ref, qn_ref, kvn_ref, wuq_ref, wuk_ref, wuvt_ref, cos_ref, sin_ref,
                      q_ref, k_ref, vt_ref, qs_ref, ks_ref, vs_ref):
    m = mods_ref[0, 0]
    n = _norm_mod(h_ref[0], g_ref[...], m[0:1], m[1:2])
    u = _dot(n, win_ref[...])
    cos = cos_ref[...]
    sin = sin_ref[...]
    lane = lax.broadcasted_iota(jnp.int32, (1, V7X_LANES), 1)
    first_half = (lane % 32) < 16

    def rope(x):
        return _rope128(x, cos, sin, first_half)

    scale_a = (MLA_NOPE + MLA_ROPE) ** -0.5 * LOG2E
    scale_b = SWA_HEAD_DIM ** -0.5
    q = _dot(_rms(u[:, _C_CQ:_C_CKV], qn_ref[...]), wuq_ref[...])
    ckv = _rms(u[:, _C_CKV:_C_QS], kvn_ref[...])
    kn = _dot(ckv, wuk_ref[...])
    vt_ref[0] = _dot_nt(wuvt_ref[...], ckv).astype(BF16)
    kr = rope(u[:, _C_KR:_C_END]).astype(BF16)
    for h in range(MLA_HEADS):
        o = h * V7X_MXU_DIM
        q_ref[0, :, o:o + V7X_LANES] = (q[:, o:o + V7X_LANES] * scale_a).astype(BF16)
        q_ref[0, :, o + V7X_LANES:o + V7X_MXU_DIM] = (rope(q[:, o + V7X_LANES:o + V7X_MXU_DIM]) * scale_a).astype(BF16)
        k_ref[0, :, o:o + V7X_LANES] = kn[:, h * MLA_NOPE:(h + 1) * MLA_NOPE].astype(BF16)
        k_ref[0, :, o + V7X_LANES:o + V7X_MXU_DIM] = kr
    for c in range((_C_KS - _C_QS) // V7X_LANES):
        o = c * V7X_LANES
        qs_ref[0, :, o:o + V7X_LANES] = (rope(u[:, _C_QS + o:_C_QS + o + V7X_LANES]) * scale_b).astype(BF16)
    for c in range(_SWA_W // V7X_LANES):
        o = c * V7X_LANES
        ks_ref[0, :, o:o + V7X_LANES] = rope(u[:, _C_KS + o:_C_KS + o + V7X_LANES]).astype(BF16)
    vs_ref[0] = u[:, _C_VS:_C_KR].astype(BF16)


def _attn_proj(h, mods, g, win, qn, kvn, wuq, wuk, wuvt, cos, sin, nct):
    b, l, d = h.shape
    tl = TOKEN_TILE
    tok = lambda w: pl.BlockSpec((1, tl, w), lambda i, j: (i, j, 0))
    full = lambda a: pl.BlockSpec(a.shape, lambda i, j: (0,) * a.ndim)
    sds = jax.ShapeDtypeStruct
    dv = MLA_HEADS * MLA_V
    return pl.pallas_call(
        _attn_proj_kernel,
        out_shape=[sds((b, l, _MLA_QK_W), BF16), sds((b, l, _MLA_QK_W), BF16), sds((b, dv, l), BF16),
                   sds((b, l, SWA_HEADS * SWA_HEAD_DIM), BF16), sds((b, l, _SWA_W), BF16), sds((b, l, _SWA_W), BF16)],
        grid=(b, l // tl),
        in_specs=[tok(d),
                  pl.BlockSpec((1, 1, N_MODS, d), lambda i, j: (i, jnp.where(j < nct, 0, 1), 0, 0)),
                  full(g), full(win), full(qn), full(kvn), full(wuq), full(wuk), full(wuvt),
                  pl.BlockSpec((tl, V7X_LANES), lambda i, j: (j, 0)),
                  pl.BlockSpec((tl, V7X_LANES), lambda i, j: (j, 0))],
        out_specs=[tok(_MLA_QK_W), tok(_MLA_QK_W), pl.BlockSpec((1, dv, tl), lambda i, j: (i, 0, j)),
                   tok(SWA_HEADS * SWA_HEAD_DIM), tok(_SWA_W), tok(_SWA_W)],
        compiler_params=pltpu.CompilerParams(dimension_semantics=("parallel", "parallel"),
                                             vmem_limit_bytes=_vmem_limit(48)),
        name="attn_proj",
    )(h, mods, g, win, qn, kvn, wuq, wuk, wuvt, cos, sin)


def _mla_kernel(q_ref, k_ref, vt_ref, o_ref, *, nct_q, lc):
    hw = V7X_MXU_DIM

    def attend(nk):
        st = [_dot_nt(k_ref[0, 0:nk, hh * hw:(hh + 1) * hw], q_ref[0, :, hh * hw:(hh + 1) * hw])
              for hh in range(MLA_HEADS_PER_STEP)]
        for hh, s in enumerate(st):
            p = jnp.exp2(s - jnp.max(s, axis=0, keepdims=True))
            den = jnp.sum(p, axis=0, keepdims=True)
            ot = _dot(vt_ref[0, hh * MLA_V:(hh + 1) * MLA_V, 0:nk], p) / den
            o_ref[0, :, hh * MLA_V:(hh + 1) * MLA_V] = ot.T.astype(o_ref.dtype)

    @pl.when(pl.program_id(2) < nct_q)
    def _():
        attend(lc)

    @pl.when(pl.program_id(2) >= nct_q)
    def _():
        attend(k_ref.shape[1])


def _mla_attention(q, k, vt, lc, q_tile0):
    b, l, _ = q.shape
    tq = MLA_Q_TILE
    hps = MLA_HEADS_PER_STEP
    return pl.pallas_call(
        functools.partial(_mla_kernel, nct_q=lc // tq - q_tile0, lc=lc),
        out_shape=jax.ShapeDtypeStruct((b, l, MLA_HEADS * MLA_V), BF16),
        grid=(b, MLA_HEADS // hps, l // tq - q_tile0),
        in_specs=[pl.BlockSpec((1, tq, hps * V7X_MXU_DIM), lambda i, h, j: (i, j + q_tile0, h)),
                  pl.BlockSpec((1, l, hps * V7X_MXU_DIM), lambda i, h, j: (i, 0, h)),
                  pl.BlockSpec((1, hps * MLA_V, l), lambda i, h, j: (i, h, 0))],
        out_specs=pl.BlockSpec((1, tq, hps * MLA_V), lambda i, h, j: (i, j + q_tile0, h)),
        compiler_params=pltpu.CompilerParams(dimension_semantics=("parallel", "parallel", "parallel"),
                                             vmem_limit_bytes=_vmem_limit(48)),
        name="mla_attention",
    )(q, k, vt)


def _swa_kernel(sink_ref, q_ref, k_ref, v_ref, o_ref, *, lc, q_tile0):
    tq = SWA_Q_TILE
    l = k_ref.shape[1]
    r0 = (pl.program_id(1) + q_tile0) * tq
    start = pl.multiple_of(jnp.clip(r0 - WINDOW, lc, l - SWA_BAND), tq)
    rows = SWA_GROUP * tq
    row = lax.broadcasted_iota(jnp.int32, (rows, 1), 0)
    qpos = jnp.where(r0 >= lc, r0, -l) + row % tq
    kpos = start + lax.broadcasted_iota(jnp.int32, (1, SWA_BAND), 1)
    valid = jnp.abs(qpos - kpos) <= WINDOW
    lane = lax.broadcasted_iota(jnp.int32, (1, V7X_MXU_DIM), 1)
    for g in range(SWA_KV_HEADS):
        sl = slice(g * V7X_MXU_DIM, (g + 1) * V7X_MXU_DIM)
        qg = q_ref[0, :, sl]
        zero = jnp.zeros_like(qg)
        head = [(lane // SWA_HEAD_DIM) == hh for hh in range(SWA_GROUP)]
        qstack = jnp.concatenate([jnp.where(head[hh], qg, zero) for hh in range(SWA_GROUP)], axis=0)
        sc = _dot_nt(qstack, k_ref[0, 0:lc, sl])
        sb = jnp.where(valid, _dot_nt(qstack, k_ref[0, pl.ds(start, SWA_BAND), sl]), NEG_INF)
        sk = jnp.zeros((rows, 1), F32)
        for hh in range(SWA_GROUP):
            sk = jnp.where(row // tq == hh, sink_ref[g * SWA_GROUP + hh], sk)
        mx = jnp.maximum(jnp.maximum(jnp.max(sc, axis=-1, keepdims=True), jnp.max(sb, axis=-1, keepdims=True)), sk)
        pc = jnp.exp(sc - mx)
        pb = jnp.exp(sb - mx)
        den = jnp.sum(pc, axis=-1, keepdims=True) + jnp.sum(pb, axis=-1, keepdims=True) + jnp.exp(sk - mx)
        ostack = (_dot(pc, v_ref[0, 0:lc, sl]) + _dot(pb, v_ref[0, pl.ds(start, SWA_BAND), sl])) / den
        o = jnp.zeros((tq, V7X_MXU_DIM), F32)
        for hh in range(SWA_GROUP):
            o = o + jnp.where(head[hh], ostack[hh * tq:(hh + 1) * tq], 0.0)
        o_ref[0, :, sl] = o.astype(o_ref.dtype)


def _swa_attention(sinks, q, k, v, lc, q_tile0):
    b, l, _ = q.shape
    tq = SWA_Q_TILE
    return pl.pallas_call(
        functools.partial(_swa_kernel, lc=lc, q_tile0=q_tile0),
        out_shape=jax.ShapeDtypeStruct((b, l, SWA_HEADS * SWA_HEAD_DIM), BF16),
        grid=(b, l // tq - q_tile0),
        in_specs=[pl.BlockSpec(memory_space=pltpu.SMEM),
                  pl.BlockSpec((1, tq, SWA_HEADS * SWA_HEAD_DIM), lambda i, j: (i, j + q_tile0, 0)),
                  pl.BlockSpec((1, l, _SWA_W), lambda i, j: (i, 0, 0)),
                  pl.BlockSpec((1, l, _SWA_W), lambda i, j: (i, 0, 0))],
        out_specs=pl.BlockSpec((1, tq, SWA_HEADS * SWA_HEAD_DIM), lambda i, j: (i, j + q_tile0, 0)),
        compiler_params=pltpu.CompilerParams(dimension_semantics=("parallel", "parallel"),
                                             vmem_limit_bytes=_vmem_limit(48)),
        name="swa_attention",
    )(sinks, q, k, v)


def _pack_bf16_pair(x):
    w = x.shape[1] // 2
    lo = pltpu.bitcast(x[:, :w].astype(BF16).astype(F32), jnp.int32)
    hi = pltpu.bitcast(x[:, w:].astype(BF16).astype(F32), jnp.int32)
    return lax.shift_right_logical(lo, jnp.int32(16)) | (hi & jnp.int32(-65536))


def _unpack_bf16_pair(p):
    return pltpu.bitcast(p << 16, F32), pltpu.bitcast(p & jnp.int32(-65536), F32)


def _route(n2, wrt, bias, run_ref):
    logits = lax.dot_general(wrt, n2, (((1,), (1,)), ((), ())), precision=HIGHEST, preferred_element_type=F32)
    rows = logits.shape[1]
    shape3 = (N_GROUPS, GROUP_SIZE, rows)
    scores3 = _sigmoid(logits[0:N_EXPERTS]).reshape(shape3)
    choice = scores3 + bias
    ji = lax.broadcasted_iota(jnp.int32, shape3, 1).astype(F32)
    m1 = jnp.max(choice, axis=1, keepdims=True)
    first = jnp.min(jnp.where(choice == m1, ji, float(GROUP_SIZE)), axis=1, keepdims=True)
    m2 = jnp.max(jnp.where(ji == first, -jnp.inf, choice), axis=1, keepdims=True)
    gs = m1 + m2
    gidx = lax.broadcasted_iota(jnp.int32, gs.shape, 0).astype(F32)
    gsel = jnp.zeros_like(gs)
    for _ in range(TOPK_GROUPS):
        mx = jnp.max(gs, axis=0, keepdims=True)
        pick = gidx == jnp.min(jnp.where(gs == mx, gidx, float(N_GROUPS)), axis=0, keepdims=True)
        gsel = jnp.where(pick, 1.0, gsel)
        gs = jnp.where(pick, -jnp.inf, gs)
    cand = jnp.where(gsel > 0.0, choice, -jnp.inf).reshape(N_EXPERTS, rows)
    scores = scores3.reshape(N_EXPERTS, rows)
    ei = lax.broadcasted_iota(jnp.int32, (N_EXPERTS, rows), 0).astype(F32)
    picks = []
    for _ in range(TOP_K):
        mx = jnp.max(cand, axis=0, keepdims=True)
        pick = ei == jnp.min(jnp.where(cand == mx, ei, float(N_EXPERTS)), axis=0, keepdims=True)
        picks.append(pick)
        cand = jnp.where(pick, -jnp.inf, cand)
    esel = jnp.zeros((N_EXPERTS, rows), F32)
    for pick in picks:
        esel = jnp.where(pick, 1.0, esel)
    before = jnp.where(lax.broadcasted_iota(jnp.int32, (rows, rows), 0) < lax.broadcasted_iota(jnp.int32, (rows, rows), 1),
                       1.0, 0.0).astype(BF16)
    slot = jnp.dot(esel.astype(BF16), before, preferred_element_type=F32) + run_ref[...]
    run_ref[...] += jnp.sum(esel, axis=1, keepdims=True)
    sc = [jnp.sum(jnp.where(pick, scores, 0.0), axis=0, keepdims=True) for pick in picks]
    tot = sc[0]
    for x in sc[1:]:
        tot = tot + x
    k8 = lax.broadcasted_iota(jnp.int32, (8, rows), 0)
    kw = lax.broadcasted_iota(jnp.int32, (GATE_W, rows), 0)
    eid = jnp.zeros((8, rows), jnp.int32)
    rank = jnp.zeros((8, rows), jnp.int32)
    wk = jnp.zeros((GATE_W, rows), F32)
    for k, pick in enumerate(picks):
        e_k = jnp.sum(jnp.where(pick, ei, 0.0), axis=0, keepdims=True).astype(jnp.int32)
        r_k = jnp.sum(jnp.where(pick, slot, 0.0), axis=0, keepdims=True).astype(jnp.int32)
        eid = jnp.where(k8 == k, e_k, eid)
        rank = jnp.where(k8 == k, r_k, rank)
        wk = jnp.where(kw == k, sc[k] * (ROUTED_SCALE / tot), wk)
    return eid, rank, wk.T


def _mixer_tail(o, h_ref, m, gffn_ref, wrt_ref, bias_ref, hn_ref, n2_ref, eid_ref, rank_ref, w_ref, cnt_ref, run_ref):
    @pl.when((pl.program_id(0) % (pl.num_programs(0) // MOE_TOKEN_SPLITS) == 0) & (pl.program_id(1) == 0))
    def _():
        run_ref[...] = jnp.zeros_like(run_ref)

    hn = h_ref[0] + m[2:3] * o
    hn_ref[0] = hn
    n2 = _norm_mod(hn, gffn_ref[...], m[3:4], m[4:5])
    n2_ref[0] = _pack_bf16_pair(n2)
    eid, rank, wcols = _route(n2, wrt_ref[...], bias_ref[...], run_ref)
    eid_ref[0] = eid
    rank_ref[0] = rank
    w_ref[0] = wcols
    cnt_ref[0] = run_ref[...]


def _attn_out_kernel(a_ref, b_ref, h_ref, mods_ref, wo_ref, gffn_ref, wrt_ref, bias_ref,
                     hn_ref, n2_ref, eid_ref, rank_ref, w_ref, cnt_ref, run_ref):
    wa = MLA_HEADS * MLA_V
    o = _dot(a_ref[0], wo_ref[0:wa, :]) + _dot(b_ref[0], wo_ref[wa:, :])
    _mixer_tail(o, h_ref, mods_ref[0, 0], gffn_ref, wrt_ref, bias_ref, hn_ref, n2_ref, eid_ref, rank_ref, w_ref,
                cnt_ref, run_ref)


def _tail_outs(b, l, d):
    tl = TOKEN_TILE
    nt = l // tl
    sds = jax.ShapeDtypeStruct
    tok = lambda w: pl.BlockSpec((1, tl, w), lambda i, j: (i, j, 0))
    blk = pl.BlockSpec((1, 8, tl), lambda i, j: (i * nt + j, 0, 0))
    bps = b // MOE_TOKEN_SPLITS
    shapes = [sds((b, l, d), F32), sds((b, l, d // 2), jnp.int32), sds((b * nt, 8, tl), jnp.int32),
              sds((b * nt, 8, tl), jnp.int32), sds((b, l, GATE_W), F32),
              sds((MOE_TOKEN_SPLITS, N_EXPERTS, 1), F32)]
    specs = [tok(d), tok(d // 2), blk, blk, tok(GATE_W),
             pl.BlockSpec((1, N_EXPERTS, 1), lambda i, j: (i // bps, 0, 0))]
    return shapes, specs


def _attn_out(a, bm, h, mods, wo, gffn, wrt, bias, nct):
    b, l, d = h.shape
    tl = TOKEN_TILE
    tok = lambda w: pl.BlockSpec((1, tl, w), lambda i, j: (i, j, 0))
    full = lambda x: pl.BlockSpec(x.shape, lambda i, j: (0,) * x.ndim)
    shapes, specs = _tail_outs(b, l, d)
    return pl.pallas_call(
        _attn_out_kernel,
        out_shape=shapes,
        grid=(b, l // tl),
        in_specs=[tok(a.shape[2]), tok(bm.shape[2]), tok(d),
                  pl.BlockSpec((1, 1, N_MODS, d), lambda i, j: (i, jnp.where(j < nct, 0, 1), 0, 0)),
                  full(wo), full(gffn), full(wrt), full(bias)],
        out_specs=specs,
        scratch_shapes=[pltpu.VMEM((N_EXPERTS, 1), F32)],
        compiler_params=pltpu.CompilerParams(dimension_semantics=("arbitrary", "arbitrary"),
                                             vmem_limit_bytes=_vmem_limit(40)),
        name="attn_out",
    )(a, bm, h, mods, wo, gffn, wrt, bias)


def _moe_dest_kernel(off_ref, eid_ref, rank_ref, dest_ref):
    eid = eid_ref[...]
    dest = rank_ref[...]
    for e in range(N_EXPERTS):
        dest = dest + jnp.where(eid == e, off_ref[e], 0)
    dest_ref[...] = dest


def _moe_dest(off, eid, rank):
    return pl.pallas_call(
        _moe_dest_kernel,
        out_shape=jax.ShapeDtypeStruct(eid.shape, jnp.int32),
        in_specs=[pl.BlockSpec(memory_space=pltpu.SMEM),
                  pl.BlockSpec(eid.shape, lambda: (0, 0, 0)), pl.BlockSpec(eid.shape, lambda: (0, 0, 0))],
        out_specs=pl.BlockSpec(eid.shape, lambda: (0, 0, 0)),
        name="moe_dest",
    )(off, eid, rank)


def _sc_mesh():
    return plsc.VectorSubcoreMesh(core_axis_name="c", subcore_axis_name="s",
                                  num_cores=V7X_SC_CORES, num_subcores=V7X_SC_SUBCORES)


def _sc_chunk(rows_per_worker):
    return max(c for c in range(8, SC_MAX_CHUNK + 1, 8) if rows_per_worker % c == 0)


def _sc_dispatch(xp, dest, p_rows, t_base):
    w = xp.shape[1]
    t = dest.shape[1]
    tpw = t // V7X_SC_WORKERS
    ch = _sc_chunk(tpw)

    @functools.partial(
        pl.kernel, mesh=_sc_mesh(), out_type=jax.ShapeDtypeStruct((p_rows, w), xp.dtype),
        scratch_types=[pltpu.VMEM((ch, w), xp.dtype)] + [pltpu.VMEM((ch,), jnp.int32)] * TOP_K
        + [pltpu.SemaphoreType.DMA, pltpu.SemaphoreType.DMA],
        name="moe_dispatch")
    def run(x_hbm, dest_hbm, out_hbm, rows_v, *rest):
        idx, (sem_i, sem_o) = rest[:TOP_K], rest[TOP_K:]
        base = (lax.axis_index("s") * V7X_SC_CORES + lax.axis_index("c")) * tpw

        @pl.loop(0, tpw // ch)
        def _(i):
            t0 = base + i * ch
            loads = [pltpu.async_copy(dest_hbm.at[k, pl.ds(t0, ch)], idx[k], sem_i) for k in range(TOP_K)]
            pltpu.sync_copy(x_hbm.at[pl.ds(t_base + t0, ch)], rows_v)
            for c in loads:
                c.wait()
            puts = [pltpu.async_copy(rows_v, out_hbm.at[idx[k]], sem_o) for k in range(TOP_K)]
            for c in puts:
                c.wait()

    return run(xp, dest)


def _sc_gather(ys, dest, t):
    w = ys.shape[1]
    tpw = t // V7X_SC_WORKERS
    ch = _sc_chunk(tpw)

    @functools.partial(
        pl.kernel, mesh=_sc_mesh(), out_type=jax.ShapeDtypeStruct((TOP_K, t, w), ys.dtype),
        scratch_types=[pltpu.VMEM((ch, w), ys.dtype)] * 2 + [pltpu.VMEM((ch,), jnp.int32)] * TOP_K
        + [pltpu.SemaphoreType.DMA] * 5,
        name="moe_gather")
    def run(y_hbm, dest_hbm, out_hbm, rows_a, rows_b, *rest):
        idx, (sem_i, sem_ga, sem_gb, sem_wa, sem_wb) = rest[:TOP_K], rest[TOP_K:]
        rows, sem_g, sem_w = (rows_a, rows_b), (sem_ga, sem_gb), (sem_wa, sem_wb)
        base = (lax.axis_index("s") * V7X_SC_CORES + lax.axis_index("c")) * tpw

        @pl.loop(0, tpw // ch)
        def _(i):
            t0 = base + i * ch
            loads = [pltpu.async_copy(dest_hbm.at[k, pl.ds(t0, ch)], idx[k], sem_i) for k in range(TOP_K)]
            for c in loads:
                c.wait()
            gets, puts = [None] * TOP_K, [None] * TOP_K
            gets[0] = pltpu.async_copy(y_hbm.at[idx[0]], rows[0], sem_g[0])
            for k in range(TOP_K):
                if k + 1 < TOP_K:
                    if k >= 1:
                        puts[k - 1].wait()
                    gets[k + 1] = pltpu.async_copy(y_hbm.at[idx[k + 1]], rows[(k + 1) % 2], sem_g[(k + 1) % 2])
                gets[k].wait()
                puts[k] = pltpu.async_copy(rows[k % 2], out_hbm.at[k, pl.ds(t0, ch)], sem_w[k % 2])
            puts[TOP_K - 2].wait()
            puts[TOP_K - 1].wait()

    return run(ys, dest)


def _moe_expert_kernel(te_ref, nv_ref, x_ref, wg_ref, wu_ref, wd_ref, y_ref, wgb_ref, wub_ref, wdb_ref):
    i = pl.program_id(0)

    @pl.when((i == 0) | (te_ref[i] != te_ref[jnp.maximum(i - 1, 0)]))
    def _():
        wgb_ref[...] = wg_ref[0, 0].astype(BF16)
        wub_ref[...] = wu_ref[0, 0].astype(BF16)
        wdb_ref[...] = wd_ref[0, 0].astype(BF16)

    @pl.when(i < nv_ref[0])
    def _():
        lo, hi = _unpack_bf16_pair(x_ref[...])
        half = lo.shape[1]
        hg = _dot(lo, wgb_ref[0:half, :]) + _dot(hi, wgb_ref[half:, :])
        hu = _dot(lo, wub_ref[0:half, :]) + _dot(hi, wub_ref[half:, :])
        y_ref[...] = _pack_bf16_pair(_dot(_silu(hg) * hu, wdb_ref[...]))


def _moe_experts(tile_expert, n_valid, xs, wg, wu, wd, layer):
    p_rows, w = xs.shape
    tm = MOE_ROW_TILE
    _, _, d, f = wg.shape
    wspec = lambda shp: pl.BlockSpec((1, 1) + shp, lambda i, te, nv: (layer, te[i], 0, 0))
    return pl.pallas_call(
        _moe_expert_kernel,
        out_shape=jax.ShapeDtypeStruct((p_rows, w), xs.dtype),
        grid_spec=pltpu.PrefetchScalarGridSpec(
            num_scalar_prefetch=2, grid=(p_rows // tm,),
            in_specs=[pl.BlockSpec((tm, w), lambda i, te, nv: (jnp.minimum(i, nv[0] - 1), 0)),
                      wspec((d, f)), wspec((d, f)), wspec((f, d))],
            out_specs=pl.BlockSpec((tm, w), lambda i, te, nv: (i, 0)),
            scratch_shapes=[pltpu.VMEM((d, f), BF16), pltpu.VMEM((d, f), BF16), pltpu.VMEM((f, d), BF16)]),
        compiler_params=pltpu.CompilerParams(dimension_semantics=("arbitrary",),
                                             vmem_limit_bytes=_vmem_limit(32)),
        name="moe_experts",
    )(tile_expert, n_valid, xs, wg, wu, wd)


def _moe_combine_kernel(yg_ref, w_ref, xp_ref, sg_ref, su_ref, sd_ref, h_ref, mods_ref, gfin_ref, *rest, final_norm):
    o_ref = rest[-1]
    xlo, xhi = _unpack_bf16_pair(xp_ref[0])
    half = xlo.shape[1]
    hs = (_silu(_dot(xlo, sg_ref[0, 0:half, :]) + _dot(xhi, sg_ref[0, half:, :]))
          * (_dot(xlo, su_ref[0, 0:half, :]) + _dot(xhi, su_ref[0, half:, :])))
    acc = _dot(hs, sd_ref[0])
    lo = acc[:, :half]
    hi = acc[:, half:]
    w = w_ref[0]
    for k in range(TOP_K):
        ylo, yhi = _unpack_bf16_pair(yg_ref[k, 0])
        wk = w[:, k:k + 1]
        lo = lo + wk * ylo
        hi = hi + wk * yhi
    y = h_ref[0] + mods_ref[0, 0, N_MODS - 1:N_MODS, :] * jnp.concatenate([lo, hi], axis=1)
    if final_norm:
        y = _rms(y, gfin_ref[...])
    o_ref[0] = y


def _moe_combine(yg, wcols, xp, sg, su, sd, h, mods, gfin, prev, nct, layer, tile0, b0, final_norm):
    b, l, d = h.shape
    bs = yg.shape[1]
    tl = TOKEN_TILE
    tok = lambda w: pl.BlockSpec((1, tl, w), lambda i, j: (i + b0, j + tile0, 0))
    lay = lambda x: pl.BlockSpec((1,) + x.shape[1:], lambda i, j: (layer,) + (0,) * (x.ndim - 1))
    args = [yg, wcols, xp, sg, su, sd, h, mods, gfin]
    in_specs = [pl.BlockSpec((TOP_K, 1, tl, d // 2), lambda i, j: (0, i, j + tile0, 0)), tok(GATE_W), tok(d // 2),
                lay(sg), lay(su), lay(sd), tok(d),
                pl.BlockSpec((1, 1, N_MODS, d), lambda i, j: (i + b0, jnp.where(j + tile0 < nct, 0, 1), 0, 0)),
                pl.BlockSpec(gfin.shape, lambda i, j: (0, 0))]
    aliases = {}
    if prev is not None:
        args.append(prev)
        in_specs.append(pl.BlockSpec(memory_space=pl.ANY))
        aliases = {len(args) - 1: 0}
    return pl.pallas_call(
        functools.partial(_moe_combine_kernel, final_norm=final_norm),
        out_shape=jax.ShapeDtypeStruct((b, l - tile0 * tl, d), F32),
        grid=(bs, l // tl - tile0),
        in_specs=in_specs,
        out_specs=pl.BlockSpec((1, tl, d), lambda i, j: (i + b0, j, 0)),
        input_output_aliases=aliases,
        compiler_params=pltpu.CompilerParams(dimension_semantics=("parallel", "parallel"),
                                             vmem_limit_bytes=_vmem_limit(40)),
        name="moe_combine",
    )(*args)


def _moe_sparse(n2p, eid, rank, wcols, counts, h, mods, wg, wu, wd, sg, su, sd, gfin, nct, layer, last):
    b, l, d = h.shape
    bs = b // MOE_TOKEN_SPLITS
    t = bs * l
    nblk = eid.shape[0] // MOE_TOKEN_SPLITS
    tm = MOE_ROW_TILE
    n_tiles = -(-(TOP_K * t + N_EXPERTS * (tm - 1)) // tm)
    x_rows = n2p.reshape(b * l, d // 2)
    out = None
    for s in range(MOE_TOKEN_SPLITS):
        tiles_e = (counts[s].reshape(N_EXPERTS).astype(jnp.int32) + (tm - 1)) // tm
        tile_end = jnp.cumsum(tiles_e)
        off = (tile_end - tiles_e) * tm
        n_valid = tile_end[-1:]
        tile_id = jnp.minimum(jnp.arange(n_tiles, dtype=jnp.int32), n_valid - 1)
        tile_expert = jnp.sum((tile_end[None, :] <= tile_id[:, None]).astype(jnp.int32), axis=1)
        blk = slice(s * nblk, (s + 1) * nblk)
        dest = _moe_dest(off, eid[blk], rank[blk]).transpose(1, 0, 2).reshape(8, t)
        xs = _sc_dispatch(x_rows, dest, n_tiles * tm, s * t)
        ys = _moe_experts(tile_expert, n_valid, xs, wg, wu, wd, layer)
        yg = _sc_gather(ys, dest, t).reshape(TOP_K, bs, l, d // 2)
        out = _moe_combine(yg, wcols, n2p, sg, su, sd, h, mods, gfin, out, nct, layer, nct if last else 0,
                           s * bs, last)
    return out


def _rwkv_proj_kernel(h_ref, hp_ref, hx_ref, mods_ref, g_ref, mu_ref, wr_ref, wk_ref, wv_ref, g1_ref, g2_ref,
                      w1_ref, w2_ref, a1_ref, a2_ref, w0_ref, a0_ref, kk_ref, ka_ref, rk_ref, bd_ref,
                      r_out, v_out, kk_out, g_out, km_out, b_out, lw_out, bonus_out, *, nct):
    j = pl.program_id(1)
    nt = pl.num_programs(1)
    m = mods_ref[0, 0]
    g = g_ref[...]
    n = _norm_mod(h_ref[0], g, m[0:1], m[1:2])
    tl, d = n.shape
    seg_first = (j == 0) | (j == nct)
    seg_last = (j == nct - 1) | (j == nt - 1)
    n_prev = _norm_mod(hp_ref[0], g, m[0:1], m[1:2])[7:8] * jnp.where(seg_first, 0.0, 1.0)
    n_next = _norm_mod(hx_ref[0], g, m[0:1], m[1:2])[0:1] * jnp.where(seg_last, 0.0, 1.0)
    row = lax.broadcasted_iota(jnp.int32, (tl, 1), 0)
    prev = jnp.where(row == 0, n_prev, pltpu.roll(n, 1, axis=0))
    nxt = jnp.where(row == tl - 1, n_next, pltpu.roll(n, tl - 1, axis=0))
    lane = lax.broadcasted_iota(jnp.int32, (1, d), 1)
    xx = jnp.where(lane < d // 2, prev, nxt) - n
    mu = mu_ref[...]
    xr, xw, xk, xv, xa, xg = [n + xx * mu[i:i + 1] for i in range(6)]
    r = _dot(xr, wr_ref[...])
    k = _dot(xk, wk_ref[...])
    v = _dot(xv, wv_ref[...])
    g_out[0] = _dot(_sigmoid(_dot(xg, g1_ref[...])), g2_ref[...]).astype(g_out.dtype)
    tw = jnp.tanh(_dot(xw, w1_ref[...]))
    ta = _dot(xa, a1_ref[...])
    bd = bd_ref[...]
    kk = k * kk_ref[...]
    kk = kk / jnp.maximum(jnp.sqrt(_head_sum(kk * kk, bd)), 1e-12)
    r_out[0] = r.astype(r_out.dtype)
    v_out[0] = v.astype(v_out.dtype)
    kk_out[0] = kk.astype(kk_out.dtype)
    bonus = jnp.zeros_like(v)
    for dr in range(2):
        zw = w0_ref[dr:dr + 1, :] + _dot(tw, w2_ref[dr])
        lw_out[dr, 0] = -jnp.exp(-0.5) * _sigmoid(zw)
        a = _sigmoid(a0_ref[dr:dr + 1, :] + _dot(ta, a2_ref[dr]))
        km = k * (1.0 + (a - 1.0) * ka_ref[...])
        km_out[dr, 0] = km.astype(km_out.dtype)
        b_out[dr, 0] = (kk * a).astype(b_out.dtype)
        bonus = bonus + _head_sum(r * km * rk_ref[...], bd) * v
    bonus_out[0] = bonus


def _rwkv_proj(h, mods, g, mu, wr, wk, wv, g1, g2, w1, w2, a1, a2, w0, a0, kk, ka, rk, bd, nct):
    b, l, d = h.shape
    tl = TOKEN_TILE
    nb8 = l // 8
    tok = pl.BlockSpec((1, tl, d), lambda i, j: (i, j, 0))
    tok2 = pl.BlockSpec((2, 1, tl, d), lambda i, j: (0, i, j, 0))
    full = lambda x: pl.BlockSpec(x.shape, lambda i, j: (0,) * x.ndim)
    sds = jax.ShapeDtypeStruct
    return pl.pallas_call(
        functools.partial(_rwkv_proj_kernel, nct=nct),
        out_shape=[sds((b, l, d), BF16), sds((b, l, d), BF16), sds((b, l, d), BF16), sds((b, l, d), BF16),
                   sds((2, b, l, d), BF16), sds((2, b, l, d), BF16), sds((2, b, l, d), F32), sds((b, l, d), F32)],
        grid=(b, l // tl),
        in_specs=[tok,
                  pl.BlockSpec((1, 8, d), lambda i, j: (i, jnp.maximum(j * (tl // 8) - 1, 0), 0)),
                  pl.BlockSpec((1, 8, d), lambda i, j: (i, jnp.minimum((j + 1) * (tl // 8), nb8 - 1), 0)),
                  pl.BlockSpec((1, 1, N_MODS, d), lambda i, j: (i, jnp.where(j < nct, 0, 1), 0, 0)),
                  full(g), full(mu), full(wr), full(wk), full(wv), full(g1), full(g2), full(w1), full(w2),
                  full(a1), full(a2), full(w0), full(a0), full(kk), full(ka), full(rk), full(bd)],
        out_specs=[tok, tok, tok, tok, tok2, tok2, tok2, tok],
        compiler_params=pltpu.CompilerParams(dimension_semantics=("parallel", "parallel"),
                                             vmem_limit_bytes=_vmem_limit(56)),
        name="rwkv_proj",
    )(h, h, h, mods, g, mu, wr, wk, wv, g1, g2, w1, w2, a1, a2, w0, a0, kk, ka, rk, bd)


def _wkv_kernel(r_ref, v_ref, kk_ref, km_ref, b_ref, lw_ref, y_ref, st_ref):
    c = WKV_CHUNK
    w = WKV_PAIR
    rev = pl.program_id(0)
    sign = 1 - 2 * rev

    @pl.when(pl.program_id(2) == 0)
    def _():
        st_ref[...] = jnp.zeros_like(st_ref)

    ti = lax.broadcasted_iota(jnp.int32, (c, c), 0)
    si = lax.broadcasted_iota(jnp.int32, (c, c), 1)
    tri = jnp.where((si - ti) * sign <= 0, 1.0, 0.0).astype(F32)
    nsub = WKV_CHUNKS_PER_STEP
    subs = [pl.ds(pl.multiple_of(jnp.where(rev == 0, s, nsub - 1 - s) * c, c), c) for s in range(nsub)]
    rt, kt, kh, bh, v32, e_mid = [], [], [], [], [], []
    for rows in subs:
        lw = lw_ref[0, 0, rows, :]
        l_incl = jnp.dot(tri, lw, precision=HIGHEST, preferred_element_type=F32)
        mid = 0.5 * jnp.sum(lw, axis=0, keepdims=True)
        e_neg = jnp.exp(mid - l_incl)
        e_mid.append(jnp.exp(mid))
        rt.append(r_ref[0, rows, :].astype(F32) * jnp.exp(l_incl - mid))
        kt.append(kk_ref[0, rows, :].astype(F32) * jnp.exp(l_incl - lw - mid))
        kh.append(km_ref[0, 0, rows, :].astype(F32) * e_neg)
        bh.append(b_ref[0, 0, rows, :].astype(F32) * e_neg)
        v32.append(v_ref[0, rows, :].astype(F32))

    ri = lax.broadcasted_iota(jnp.int32, (w, w), 0)
    ci = lax.broadcasted_iota(jnp.int32, (w, w), 1)
    same = (ri // c) == (ci // c)
    dlt = (ci % c - ri % c) * sign
    strict = same & (dlt < 0)
    incl = same & (dlt <= 0)
    eye = jnp.where(ri == ci, 1.0, 0.0).astype(F32)
    lane = lax.broadcasted_iota(jnp.int32, (1, w), 1)
    h0 = lane < RWKV_HEAD

    def rows2(x):
        return jnp.concatenate([jnp.where(h0, x, 0.0), jnp.where(h0, 0.0, x)], axis=0)

    def fold(x):
        return x[:c] + x[c:]

    npair = st_ref.shape[0]
    items = [(s, slice(p * w, (p + 1) * w)) for s in range(nsub) for p in range(npair)]
    n = range(len(items))
    em = [e_mid[s][:, sl] for s, sl in items]
    g = [_dot_nt(jnp.concatenate([rows2(kt[s][:, sl]), rows2(rt[s][:, sl])], axis=0),
                 jnp.concatenate([kh[s][:, sl], kh[s][:, sl], bh[s][:, sl], bh[s][:, sl]], axis=0)) for s, sl in items]
    a_kk = [jnp.where(strict, x[:w, :w], 0.0) for x in g]
    a_rk = [jnp.where(incl, x[w:, :w], 0.0) for x in g]
    a_rb = [jnp.where(incl, x[w:, w:], 0.0) for x in g]
    vi = [v32[s][:, sl] for s, sl in items]
    v_rows = [rows2(x) for x in vi]
    r_pre = [_dot(a_kk[i], v_rows[i]) for i in n]
    m = [jnp.where(strict, -x[:w, w:], 0.0) for x in g]
    tinv = [eye + x for x in m]
    m = [_dot(x, x) for x in m]
    for _ in range(c.bit_length() - 3):
        both = [_dot(jnp.concatenate([tinv[i], m[i]], axis=0), m[i]) for i in n]
        tinv = [tinv[i] + both[i][:w] for i in n]
        m = [x[w:] for x in both]
    tinv = [tinv[i] + _dot(tinv[i], m[i]) for i in n]
    sol = [_dot(tinv[i], jnp.concatenate([r_pre[i], rows2(kt[s][:, sl] * em[i])], axis=1))
           for i, (s, sl) in enumerate(items)]
    u_rows = [x[:, :w] for x in sol]
    kq_rows = [x[:, w:] for x in sol]
    y_pre = [fold(_dot(jnp.concatenate([a_rk[i], -a_rb[i]], axis=1),
                       jnp.concatenate([v_rows[i], u_rows[i]], axis=0))) for i in n]
    r_eff = [rt[s][:, sl] * em[i] - fold(_dot(a_rb[i], kq_rows[i])) for i, (s, sl) in enumerate(items)]
    bbar = [bh[s][:, sl] * em[i] for i, (s, sl) in enumerate(items)]
    kbar = [kh[s][:, sl] * em[i] for i, (s, sl) in enumerate(items)]
    mmat = [eye * (em[i] * em[i]) - jnp.where(same, _dot_tn(fold(kq_rows[i]), bbar[i]), 0.0) for i in n]
    s_pre = [jnp.where(same, _dot_tn(jnp.concatenate([vi[i], -fold(u_rows[i])], axis=0),
                                     jnp.concatenate([kbar[i], bbar[i]], axis=0)), 0.0) for i in n]
    st = [st_ref[p] for p in range(npair)]
    for i, (s, sl) in enumerate(items):
        p = i % npair
        y_ref[0, 0, subs[s], sl] = _dot_nt(r_eff[i], st[p]) + y_pre[i]
        hi = st[p].astype(BF16)
        lo = (st[p] - hi.astype(F32)).astype(BF16)
        mb = mmat[i].astype(BF16)
        st[p] = (jnp.dot(hi, mb, preferred_element_type=F32) + jnp.dot(lo, mb, preferred_element_type=F32)
                 + s_pre[i])
    for p in range(npair):
        st_ref[p] = st[p]


def _wkv(r, v, kk, km, bv, lw, lc):
    b, l, d = r.shape
    c = WKV_CHUNK * WKV_CHUNKS_PER_STEP
    ncc = lc // c
    nlc = (l - lc) // c

    def chunk(dr, i):
        return jnp.where(dr == 0, i, jnp.where(i < ncc, ncc - 1 - i, nlc + 2 * ncc - 1 - i))

    shared = pl.BlockSpec((1, c, d), lambda dr, bi, i: (bi, chunk(dr, i), 0))
    per_dir = pl.BlockSpec((1, 1, c, d), lambda dr, bi, i: (dr, bi, chunk(dr, i), 0))
    return pl.pallas_call(
        _wkv_kernel,
        out_shape=jax.ShapeDtypeStruct((2, b, l, d), F32),
        grid=(2, b, l // c),
        in_specs=[shared, shared, shared, per_dir, per_dir, per_dir],
        out_specs=per_dir,
        scratch_shapes=[pltpu.VMEM((d // WKV_PAIR, WKV_PAIR, WKV_PAIR), F32)],
        compiler_params=pltpu.CompilerParams(dimension_semantics=("parallel", "parallel", "arbitrary"),
                                             vmem_limit_bytes=_vmem_limit(32)),
        name="wkv7_chunked",
    )(r, v, kk, km, bv, lw)


def _rwkv_out_kernel(y_ref, bonus_ref, g_ref, lnw_ref, lnb_ref, wo_ref, bd_ref, h_ref, mods_ref, gffn_ref,
                     wrt_ref, bias_ref, hn_ref, n2_ref, eid_ref, rank_ref, w_ref, cnt_ref, run_ref):
    y = y_ref[0, 0] + y_ref[1, 0]
    bd = bd_ref[...]
    mean = _head_sum(y, bd) * (1.0 / RWKV_HEAD)
    yc = y - mean
    var = _head_sum(yc * yc, bd) * (1.0 / RWKV_HEAD)
    yn = yc * lax.rsqrt(var + GN_EPS) * lnw_ref[...] + lnb_ref[...]
    out = (yn + bonus_ref[0]) * g_ref[0].astype(F32)
    _mixer_tail(_dot(out, wo_ref[...]), h_ref, mods_ref[0, 0], gffn_ref, wrt_ref, bias_ref, hn_ref, n2_ref,
                eid_ref, rank_ref, w_ref, cnt_ref, run_ref)


def _rwkv_out(y, bonus, g, lnw, lnb, wo, bd, h, mods, gffn, wrt, bias, nct):
    b, l, d = h.shape
    tl = TOKEN_TILE
    tok = lambda w: pl.BlockSpec((1, tl, w), lambda i, j: (i, j, 0))
    full = lambda x: pl.BlockSpec(x.shape, lambda i, j: (0,) * x.ndim)
    shapes, specs = _tail_outs(b, l, d)
    return pl.pallas_call(
        _rwkv_out_kernel,
        out_shape=shapes,
        grid=(b, l // tl),
        in_specs=[pl.BlockSpec((2, 1, tl, d), lambda i, j: (0, i, j, 0)), tok(d), tok(d),
                  full(lnw), full(lnb), full(wo), full(bd), tok(d),
                  pl.BlockSpec((1, 1, N_MODS, d), lambda i, j: (i, jnp.where(j < nct, 0, 1), 0, 0)),
                  full(gffn), full(wrt), full(bias)],
        out_specs=specs,
        scratch_shapes=[pltpu.VMEM((N_EXPERTS, 1), F32)],
        compiler_params=pltpu.CompilerParams(dimension_semantics=("arbitrary", "arbitrary"),
                                             vmem_limit_bytes=_vmem_limit(40)),
        name="rwkv_out",
    )(y, bonus, g, lnw, lnb, wo, bd, h, mods, gffn, wrt, bias)


def _rope_table(n_lat, n_ctx):
    dim = SWA_HEAD_DIM
    nf = dim // 4
    inv = ROPE_THETA ** (-jnp.arange(nf, dtype=F32) / nf)
    row = jnp.repeat(jnp.arange(n_lat // GRID_W, dtype=F32), GRID_W)
    col = jnp.tile(jnp.arange(GRID_W, dtype=F32), n_lat // GRID_W)
    ar = row[:, None] * inv
    ac = col[:, None] * inv
    ang = jnp.concatenate([ar, ar, ac, ac], axis=-1)
    cos = jnp.concatenate([jnp.ones((n_ctx, dim), F32), jnp.cos(ang)], axis=0)
    sin = jnp.concatenate([jnp.zeros((n_ctx, dim), F32), jnp.sin(ang)], axis=0)
    return jnp.tile(cos, (1, 2)), jnp.tile(sin, (1, 2))


def _layout_attn_weights(w_in, w_uq, w_ukv):
    d = w_in.shape[0]
    s0 = MLA_Q_RANK
    s1 = s0 + MLA_KV_RANK
    s2 = s1 + MLA_ROPE
    s3 = s2 + SWA_HEADS * SWA_HEAD_DIM
    s4 = s3 + SWA_KV_HEADS * SWA_HEAD_DIM
    rep = lambda w: jnp.concatenate(
        [jnp.tile(w[:, g * SWA_HEAD_DIM:(g + 1) * SWA_HEAD_DIM], (1, SWA_GROUP)) for g in range(SWA_KV_HEADS)], axis=1)
    win = jnp.concatenate([w_in[:, :s1], w_in[:, s2:s3], rep(w_in[:, s3:s4]), rep(w_in[:, s4:]),
                           w_in[:, s1:s2], jnp.zeros((d, V7X_LANES - MLA_ROPE), w_in.dtype)], axis=1)
    qh = MLA_NOPE + MLA_ROPE
    pad = jnp.zeros((w_uq.shape[0], V7X_MXU_DIM - qh), w_uq.dtype)
    wuq = jnp.concatenate([jnp.concatenate([w_uq[:, h * qh:(h + 1) * qh], pad], axis=1) for h in range(MLA_HEADS)], axis=1)
    kvh = MLA_NOPE + MLA_V
    wuk = jnp.concatenate([w_ukv[:, h * kvh:h * kvh + MLA_NOPE] for h in range(MLA_HEADS)], axis=1)
    wuvt = jnp.concatenate([w_ukv[:, h * kvh + MLA_NOPE:(h + 1) * kvh] for h in range(MLA_HEADS)], axis=1).T
    return win.astype(BF16), wuq.astype(BF16), wuk.astype(BF16), wuvt.astype(BF16)


def _lora_pair(w_down, w_up):
    rank = w_down.shape[2]
    down = jnp.concatenate([w_down[0], w_down[1]], axis=1)
    z = jnp.zeros((rank, w_up.shape[2]), w_up.dtype)
    up = jnp.stack([jnp.concatenate([w_up[0], z], axis=0), jnp.concatenate([z, w_up[1]], axis=0)], axis=0)
    return down.astype(BF16), up.astype(BF16)


def _head_block_diag():
    i = jnp.arange(V7X_MXU_DIM) // RWKV_HEAD
    return (i[:, None] == i[None, :]).astype(BF16)


def kernel(x, c, ctx, c_ctx, ada_w, ada_b, norm_mix, norm_ffn, norm_final, attn_w_in, attn_q_norm, attn_kv_norm, attn_w_uq, attn_w_ukv, attn_sinks, attn_w_o, rwkv_mu, rwkv_w_r, rwkv_w_k, rwkv_w_v, rwkv_w_o, rwkv_g1, rwkv_g2, rwkv_w0, rwkv_w1, rwkv_w2, rwkv_a0, rwkv_a1, rwkv_a2, rwkv_k_k, rwkv_k_a, rwkv_r_k, rwkv_ln_w, rwkv_ln_b, moe_router, moe_bias, moe_w_gate, moe_w_up, moe_w_down, moe_ws_gate, moe_ws_up, moe_ws_down):
    bsz, s, d = x.shape
    lc = ctx.shape[1]
    l = lc + s
    depth = ada_w.shape[0]
    nct = lc // TOKEN_TILE
    assert lc % TOKEN_TILE == 0 and s % TOKEN_TILE == 0 and s >= SWA_BAND
    assert lc % (WKV_CHUNK * WKV_CHUNKS_PER_STEP) == 0
    assert d % V7X_MXU_DIM == 0 and WKV_CHUNK * 2 == V7X_LANES
    assert bsz % MOE_TOKEN_SPLITS == 0 and (bsz // MOE_TOKEN_SPLITS * l) % (8 * V7X_SC_WORKERS) == 0

    h = jnp.concatenate([ctx, x], axis=1)
    cos, sin = _rope_table(s, lc)
    bd = _head_block_diag()
    rows = -(-(bsz + 1) // 8) * 8
    cc = jnp.concatenate([c, c_ctx[None, :], jnp.zeros((rows - bsz - 1, d), F32)], axis=0)
    row2 = lambda a: a.reshape(1, -1)

    for li in range(depth):
        with_ctx = li < depth - 1
        i = li // 2
        ada = _ada_mods(cc, ada_w, ada_b, li)
        mods = jnp.stack([jnp.broadcast_to(ada[bsz].reshape(1, N_MODS, d), (bsz, N_MODS, d)),
                          ada[:bsz].reshape(bsz, N_MODS, d)], axis=1)
        wrt = jnp.concatenate([moe_router[li].T, jnp.zeros((GATE_W - N_EXPERTS, d), F32)], axis=0)
        bias = moe_bias[li].reshape(N_GROUPS, GROUP_SIZE, 1)
        if li % 2 == 0:
            win, wuq, wuk, wuvt = _layout_attn_weights(attn_w_in[i], attn_w_uq[i], attn_w_ukv[i])
            q, k, vt, qs, ks, vs = _attn_proj(h, mods, row2(norm_mix[li]), win, row2(attn_q_norm[i]),
                                              row2(attn_kv_norm[i]), wuq, wuk, wuvt, cos, sin, nct)
            a = _mla_attention(q, k, vt, lc, 0 if with_ctx else lc // MLA_Q_TILE)
            bm = _swa_attention(attn_sinks[i], qs, ks, vs, lc, 0 if with_ctx else lc // SWA_Q_TILE)
            tail = _attn_out(a, bm, h, mods, attn_w_o[i].astype(BF16), row2(norm_ffn[li]), wrt, bias, nct)
        else:
            w1, w2 = _lora_pair(rwkv_w1[i], rwkv_w2[i])
            a1, a2 = _lora_pair(rwkv_a1[i], rwkv_a2[i])
            r, v, kk, g, km, bv, lw, bonus = _rwkv_proj(
                h, mods, row2(norm_mix[li]), rwkv_mu[i], rwkv_w_r[i].astype(BF16), rwkv_w_k[i].astype(BF16),
                rwkv_w_v[i].astype(BF16), rwkv_g1[i].astype(BF16), rwkv_g2[i].astype(BF16), w1, w2, a1, a2,
                rwkv_w0[i], rwkv_a0[i], row2(rwkv_k_k[i]), row2(rwkv_k_a[i]), row2(rwkv_r_k[i]), bd, nct)
            y = _wkv(r, v, kk, km, bv, lw, lc)
            tail = _rwkv_out(y, bonus, g, row2(rwkv_ln_w[i]), row2(rwkv_ln_b[i]), rwkv_w_o[i].astype(BF16),
                             bd, h, mods, row2(norm_ffn[li]), wrt, bias, nct)
        h, n2p, eid, rank, wcols, counts = tail
        h = _moe_sparse(n2p, eid, rank, wcols, counts, h, mods, moe_w_gate, moe_w_up, moe_w_down,
                        moe_ws_gate, moe_ws_up, moe_ws_down, row2(norm_final), nct, li, li == depth - 1)
    return h
```

```python
import functools

import jax
import jax.numpy as jnp
from jax import lax
from jax.experimental import pallas as pl
from jax.experimental.pallas import tpu as pltpu
from jax.experimental.pallas import tpu_sc as plsc

F32 = jnp.float32
BF16 = jnp.bfloat16
HIGHEST = lax.Precision.HIGHEST

GRID_W = 64
NORM_EPS = 1e-6
ROPE_THETA = 10000.0
NEG_INF = -1e30
N_MODS = 6

MLA_HEADS = 4
MLA_Q_RANK = 384
MLA_KV_RANK = 256
MLA_NOPE = 128
MLA_ROPE = 64
MLA_V = 128

SWA_HEADS = 8
SWA_KV_HEADS = 2
SWA_GROUP = SWA_HEADS // SWA_KV_HEADS
SWA_HEAD_DIM = 64
WINDOW = 128

RWKV_HEAD = 64
DECAY_LORA = 64
ICLR_LORA = 64
GATE_LORA = 128
GN_EPS = 64e-5

N_EXPERTS = 64
TOP_K = 6
N_GROUPS = 8
TOPK_GROUPS = 4
GROUP_SIZE = N_EXPERTS // N_GROUPS
ROUTED_SCALE = 2.5
GATE_W = 128

V7X_LANES = 128
V7X_MXU_DIM = 256
V7X_VMEM_BYTES = 64 * 1024 * 1024
V7X_SC_CORES = 2
V7X_SC_SUBCORES = 16
V7X_SC_WORKERS = V7X_SC_CORES * V7X_SC_SUBCORES

TOKEN_TILE = 256
MLA_Q_TILE = 256
MLA_HEADS_PER_STEP = 2
LOG2E = 1.4426950408889634
SWA_Q_TILE = 128
SWA_BAND = SWA_Q_TILE + 2 * WINDOW
WKV_CHUNK = 64
WKV_PAIR = 2 * RWKV_HEAD
WKV_CHUNKS_PER_STEP = 2
MOE_ROW_TILE = 512
MOE_TOKEN_SPLITS = 1
SAMPLE_GROUPS = 2
SC_MAX_CHUNK = 64


def _vmem_limit(mib):
    return min(mib * 1024 * 1024, V7X_VMEM_BYTES - 4 * 1024 * 1024)


def _dot(a, b):
    return jnp.dot(a.astype(BF16), b.astype(BF16), preferred_element_type=F32)


def _dot_nt(a, b):
    return lax.dot_general(a.astype(BF16), b.astype(BF16), (((1,), (1,)), ((), ())),
                           preferred_element_type=F32)


def _dot_tn(a, b):
    return lax.dot_general(a.astype(BF16), b.astype(BF16), (((0,), (0,)), ((), ())),
                           preferred_element_type=F32)


def _sigmoid(x):
    return 1.0 / (1.0 + jnp.exp(-x))


def _silu(x):
    return x * _sigmoid(x)


def _rms(x, g):
    return x * lax.rsqrt(jnp.mean(x * x, axis=-1, keepdims=True) + NORM_EPS) * g


def _norm_mod(x, g, shift, scale):
    return _rms(x, g) * (1.0 + scale) + shift


def _split_dot(x, w):
    hi = x.astype(BF16)
    lo = (x - hi.astype(F32)).astype(BF16)
    return (jnp.dot(hi, w, preferred_element_type=F32) + jnp.dot(lo, w, preferred_element_type=F32))


def _head_sum(x, bd):
    w = bd.shape[0]
    parts = [_split_dot(x[:, c * w:(c + 1) * w], bd) for c in range(x.shape[1] // w)]
    return jnp.concatenate(parts, axis=1)


def _ada_kernel(c_ref, w_ref, b_ref, o_ref):
    s = _silu(c_ref[...])
    o_ref[...] = jnp.dot(s, w_ref[0], precision=HIGHEST, preferred_element_type=F32) + b_ref[0]


def _ada_mods(cc, w, b, layer):
    rows, d = cc.shape
    depth, _, n = w.shape
    return pl.pallas_call(
        _ada_kernel,
        out_shape=jax.ShapeDtypeStruct((rows, n), F32),
        grid=(n // d,),
        in_specs=[pl.BlockSpec((rows, d), lambda i: (0, 0)),
                  pl.BlockSpec((1, d, d), lambda i: (layer, 0, i)),
                  pl.BlockSpec((1, 1, d), lambda i: (layer, 0, i))],
        out_specs=pl.BlockSpec((rows, d), lambda i: (0, i)),
        compiler_params=pltpu.CompilerParams(dimension_semantics=("parallel",),
                                             vmem_limit_bytes=_vmem_limit(32)),
        name="ada_mods",
    )(cc, w, b.reshape(depth, 1, n))


def _rope128(x, cos, sin, first_half):
    rot = jnp.where(first_half, -pltpu.roll(x, V7X_LANES - 16, axis=1), pltpu.roll(x, 16, axis=1))
    return x * cos + rot * sin


_C_CQ = 0
_C_CKV = _C_CQ + MLA_Q_RANK
_C_QS = _C_CKV + MLA_KV_RANK
_C_KS = _C_QS + SWA_HEADS * SWA_HEAD_DIM
_C_VS = _C_KS + SWA_KV_HEADS * V7X_MXU_DIM
_C_KR = _C_VS + SWA_KV_HEADS * V7X_MXU_DIM
_C_END = _C_KR + V7X_LANES
_SWA_W = SWA_KV_HEADS * V7X_MXU_DIM
_MLA_QK_W = MLA_HEADS * V7X_MXU_DIM


def _attn_proj_kernel(h_ref, mods_ref, g_ref, win_ref, qn_ref, kvn_ref, wuq_ref, wuk_ref, wuvt_ref, cos_ref, sin_ref,
                      q_ref, k_ref, vt_ref, qs_ref, ks_ref, vs_ref):
    m = mods_ref[0, 0]
    n = _norm_mod(h_ref[0], g_ref[...], m[0:1], m[1:2])
    u = _dot(n, win_ref[...])
    cos = cos_ref[...]
    sin = sin_ref[...]
    lane = lax.broadcasted_iota(jnp.int32, (1, V7X_LANES), 1)
    first_half = (lane % 32) < 16

    def rope(x):
        return _rope128(x, cos, sin, first_half)

    scale_a = (MLA_NOPE + MLA_ROPE) ** -0.5 * LOG2E
    scale_b = SWA_HEAD_DIM ** -0.5
    q = _dot(_rms(u[:, _C_CQ:_C_CKV], qn_ref[...]), wuq_ref[...])
    ckv = _rms(u[:, _C_CKV:_C_QS], kvn_ref[...])
    kn = _dot(ckv, wuk_ref[...])
    vt_ref[0] = _dot_nt(wuvt_ref[...], ckv).astype(BF16)
    kr = rope(u[:, _C_KR:_C_END]).astype(BF16)
    for h in range(MLA_HEADS):
        o = h * V7X_MXU_DIM
        q_ref[0, :, o:o + V7X_LANES] = (q[:, o:o + V7X_LANES] * scale_a).astype(BF16)
        q_ref[0, :, o + V7X_LANES:o + V7X_MXU_DIM] = (rope(q[:, o + V7X_LANES:o + V7X_MXU_DIM]) * scale_a).astype(BF16)
        k_ref[0, :, o:o + V7X_LANES] = kn[:, h * MLA_NOPE:(h + 1) * MLA_NOPE].astype(BF16)
        k_ref[0, :, o + V7X_LANES:o + V7X_MXU_DIM] = kr
    for c in range((_C_KS - _C_QS) // V7X_LANES):
        o = c * V7X_LANES
        qs_ref[0, :, o:o + V7X_LANES] = (rope(u[:, _C_QS + o:_C_QS + o + V7X_LANES]) * scale_b).astype(BF16)
    for c in range(_SWA_W // V7X_LANES):
        o = c * V7X_LANES
        ks_ref[0, :, o:o + V7X_LANES] = rope(u[:, _C_KS + o:_C_KS + o + V7X_LANES]).astype(BF16)
    vs_ref[0] = u[:, _C_VS:_C_KR].astype(BF16)


def _attn_proj(h, mods, g, win, qn, kvn, wuq, wuk, wuvt, cos, sin, nct):
    b, l, d = h.shape
    tl = TOKEN_TILE
    tok = lambda w: pl.BlockSpec((1, tl, w), lambda i, j: (i, j, 0))
    full = lambda a: pl.BlockSpec(a.shape, lambda i, j: (0,) * a.ndim)
    sds = jax.ShapeDtypeStruct
    dv = MLA_HEADS * MLA_V
    return pl.pallas_call(
        _attn_proj_kernel,
        out_shape=[sds((b, l, _MLA_QK_W), BF16), sds((b, l, _MLA_QK_W), BF16), sds((b, dv, l), BF16),
                   sds((b, l, SWA_HEADS * SWA_HEAD_DIM), BF16), sds((b, l, _SWA_W), BF16), sds((b, l, _SWA_W), BF16)],
        grid=(b, l // tl),
        in_specs=[tok(d),
                  pl.BlockSpec((1, 1, N_MODS, d), lambda i, j: (i, jnp.where(j < nct, 0, 1), 0, 0)),
                  full(g), full(win), full(qn), full(kvn), full(wuq), full(wuk), full(wuvt),
                  pl.BlockSpec((tl, V7X_LANES), lambda i, j: (j, 0)),
                  pl.BlockSpec((tl, V7X_LANES), lambda i, j: (j, 0))],
        out_specs=[tok(_MLA_QK_W), tok(_MLA_QK_W), pl.BlockSpec((1, dv, tl), lambda i, j: (i, 0, j)),
                   tok(SWA_HEADS * SWA_HEAD_DIM), tok(_SWA_W), tok(_SWA_W)],
        compiler_params=pltpu.CompilerParams(dimension_semantics=("parallel", "parallel"),
                                             vmem_limit_bytes=_vmem_limit(48)),
        name="attn_proj",
    )(h, mods, g, win, qn, kvn, wuq, wuk, wuvt, cos, sin)


def _mla_kernel(q_ref, k_ref, vt_ref, o_ref, *, nct_q, lc):
    hw = V7X_MXU_DIM

    def attend(nk):
        st = [_dot_nt(k_ref[0, 0:nk, hh * hw:(hh + 1) * hw], q_ref[0, :, hh * hw:(hh + 1) * hw])
              for hh in range(MLA_HEADS_PER_STEP)]
        for hh, s in enumerate(st):
            p = jnp.exp2(s - jnp.max(s, axis=0, keepdims=True))
            den = jnp.sum(p, axis=0, keepdims=True)
            ot = _dot(vt_ref[0, hh * MLA_V:(hh + 1) * MLA_V, 0:nk], p) / den
            o_ref[0, :, hh * MLA_V:(hh + 1) * MLA_V] = ot.T.astype(o_ref.dtype)

    @pl.when(pl.program_id(2) < nct_q)
    def _():
        attend(lc)

    @pl.when(pl.program_id(2) >= nct_q)
    def _():
        attend(k_ref.shape[1])


def _mla_attention(q, k, vt, lc, q_tile0):
    b, l, _ = q.shape
    tq = MLA_Q_TILE
    hps = MLA_HEADS_PER_STEP
    return pl.pallas_call(
        functools.partial(_mla_kernel, nct_q=lc // tq - q_tile0, lc=lc),
        out_shape=jax.ShapeDtypeStruct((b, l, MLA_HEADS * MLA_V), BF16),
        grid=(b, MLA_HEADS // hps, l // tq - q_tile0),
        in_specs=[pl.BlockSpec((1, tq, hps * V7X_MXU_DIM), lambda i, h, j: (i, j + q_tile0, h)),
                  pl.BlockSpec((1, l, hps * V7X_MXU_DIM), lambda i, h, j: (i, 0, h)),
                  pl.BlockSpec((1, hps * MLA_V, l), lambda i, h, j: (i, h, 0))],
        out_specs=pl.BlockSpec((1, tq, hps * MLA_V), lambda i, h, j: (i, j + q_tile0, h)),
        compiler_params=pltpu.CompilerParams(dimension_semantics=("parallel", "parallel", "parallel"),
                                             vmem_limit_bytes=_vmem_limit(48)),
        name="mla_attention",
    )(q, k, vt)


def _swa_kernel(sink_ref, q_ref, k_ref, v_ref, o_ref, *, lc, q_tile0):
    tq = SWA_Q_TILE
    l = k_ref.shape[1]
    r0 = (pl.program_id(1) + q_tile0) * tq
    start = pl.multiple_of(jnp.clip(r0 - WINDOW, lc, l - SWA_BAND), tq)
    rows = SWA_GROUP * tq
    row = lax.broadcasted_iota(jnp.int32, (rows, 1), 0)
    qpos = jnp.where(r0 >= lc, r0, -l) + row % tq
    kpos = start + lax.broadcasted_iota(jnp.int32, (1, SWA_BAND), 1)
    valid = jnp.abs(qpos - kpos) <= WINDOW
    lane = lax.broadcasted_iota(jnp.int32, (1, V7X_MXU_DIM), 1)
    for g in range(SWA_KV_HEADS):
        sl = slice(g * V7X_MXU_DIM, (g + 1) * V7X_MXU_DIM)
        qg = q_ref[0, :, sl]
        zero = jnp.zeros_like(qg)
        head = [(lane // SWA_HEAD_DIM) == hh for hh in range(SWA_GROUP)]
        qstack = jnp.concatenate([jnp.where(head[hh], qg, zero) for hh in range(SWA_GROUP)], axis=0)
        sc = _dot_nt(qstack, k_ref[0, 0:lc, sl])
        sb = jnp.where(valid, _dot_nt(qstack, k_ref[0, pl.ds(start, SWA_BAND), sl]), NEG_INF)
        sk = jnp.zeros((rows, 1), F32)
        for hh in range(SWA_GROUP):
            sk = jnp.where(row // tq == hh, sink_ref[g * SWA_GROUP + hh], sk)
        mx = jnp.maximum(jnp.maximum(jnp.max(sc, axis=-1, keepdims=True), jnp.max(sb, axis=-1, keepdims=True)), sk)
        pc = jnp.exp(sc - mx)
        pb = jnp.exp(sb - mx)
        den = jnp.sum(pc, axis=-1, keepdims=True) + jnp.sum(pb, axis=-1, keepdims=True) + jnp.exp(sk - mx)
        ostack = (_dot(pc, v_ref[0, 0:lc, sl]) + _dot(pb, v_ref[0, pl.ds(start, SWA_BAND), sl])) / den
        o = jnp.zeros((tq, V7X_MXU_DIM), F32)
        for hh in range(SWA_GROUP):
            o = o + jnp.where(head[hh], ostack[hh * tq:(hh + 1) * tq], 0.0)
        o_ref[0, :, sl] = o.astype(o_ref.dtype)


def _swa_attention(sinks, q, k, v, lc, q_tile0):
    b, l, _ = q.shape
    tq = SWA_Q_TILE
    return pl.pallas_call(
        functools.partial(_swa_kernel, lc=lc, q_tile0=q_tile0),
        out_shape=jax.ShapeDtypeStruct((b, l, SWA_HEADS * SWA_HEAD_DIM), BF16),
        grid=(b, l // tq - q_tile0),
        in_specs=[pl.BlockSpec(memory_space=pltpu.SMEM),
                  pl.BlockSpec((1, tq, SWA_HEADS * SWA_HEAD_DIM), lambda i, j: (i, j + q_tile0, 0)),
                  pl.BlockSpec((1, l, _SWA_W), lambda i, j: (i, 0, 0)),
                  pl.BlockSpec((1, l, _SWA_W), lambda i, j: (i, 0, 0))],
        out_specs=pl.BlockSpec((1, tq, SWA_HEADS * SWA_HEAD_DIM), lambda i, j: (i, j + q_tile0, 0)),
        compiler_params=pltpu.CompilerParams(dimension_semantics=("parallel", "parallel"),
                                             vmem_limit_bytes=_vmem_limit(48)),
        name="swa_attention",
    )(sinks, q, k, v)


def _pack_bf16_pair(x):
    w = x.shape[1] // 2
    lo = pltpu.bitcast(x[:, :w].astype(BF16).astype(F32), jnp.int32)
    hi = pltpu.bitcast(x[:, w:].astype(BF16).astype(F32), jnp.int32)
    return lax.shift_right_logical(lo, jnp.int32(16)) | (hi & jnp.int32(-65536))


def _unpack_bf16_pair(p):
    return pltpu.bitcast(p << 16, F32), pltpu.bitcast(p & jnp.int32(-65536), F32)


def _route(n2, wrt, bias, run_ref):
    n_hi = n2.astype(BF16)
    n_lo = (n2 - n_hi.astype(F32)).astype(BF16)
    w_hi = wrt.astype(BF16)
    w_lo = (wrt - w_hi.astype(F32)).astype(BF16)
    logits = _dot_nt(w_hi, n_hi) + (_dot_nt(w_hi, n_lo) + _dot_nt(w_lo, n_hi))
    rows = logits.shape[1]
    shape3 = (N_GROUPS, GROUP_SIZE, rows)
    scores3 = _sigmoid(logits[0:N_EXPERTS]).reshape(shape3)
    choice = scores3 + bias
    ji = lax.broadcasted_iota(jnp.int32, shape3, 1).astype(F32)
    m1 = jnp.max(choice, axis=1, keepdims=True)
    first = jnp.min(jnp.where(choice == m1, ji, float(GROUP_SIZE)), axis=1, keepdims=True)
    m2 = jnp.max(jnp.where(ji == first, -jnp.inf, choice), axis=1, keepdims=True)
    gs = m1 + m2
    gidx = lax.broadcasted_iota(jnp.int32, gs.shape, 0).astype(F32)
    gsel = jnp.zeros_like(gs)
    for _ in range(TOPK_GROUPS):
        mx = jnp.max(gs, axis=0, keepdims=True)
        pick = gidx == jnp.min(jnp.where(gs == mx, gidx, float(N_GROUPS)), axis=0, keepdims=True)
        gsel = jnp.where(pick, 1.0, gsel)
        gs = jnp.where(pick, -jnp.inf, gs)
    cand = jnp.where(gsel > 0.0, choice, -jnp.inf).reshape(N_EXPERTS, rows)
    scores = scores3.reshape(N_EXPERTS, rows)
    ei = lax.broadcasted_iota(jnp.int32, (N_EXPERTS, rows), 0).astype(F32)
    picks = []
    for _ in range(TOP_K):
        mx = jnp.max(cand, axis=0, keepdims=True)
        pick = ei == jnp.min(jnp.where(cand == mx, ei, float(N_EXPERTS)), axis=0, keepdims=True)
        picks.append(pick)
        cand = jnp.where(pick, -jnp.inf, cand)
    esel = jnp.zeros((N_EXPERTS, rows), F32)
    for pick in picks:
        esel = jnp.where(pick, 1.0, esel)
    before = jnp.where(lax.broadcasted_iota(jnp.int32, (rows, rows), 0) < lax.broadcasted_iota(jnp.int32, (rows, rows), 1),
                       1.0, 0.0).astype(BF16)
    slot = jnp.dot(esel.astype(BF16), before, preferred_element_type=F32) + run_ref[...]
    run_ref[...] += jnp.sum(esel, axis=1, keepdims=True)
    sc = [jnp.sum(jnp.where(pick, scores, 0.0), axis=0, keepdims=True) for pick in picks]
    tot = sc[0]
    for x in sc[1:]:
        tot = tot + x
    k8 = lax.broadcasted_iota(jnp.int32, (8, rows), 0)
    kw = lax.broadcasted_iota(jnp.int32, (GATE_W, rows), 0)
    eid = jnp.zeros((8, rows), jnp.int32)
    rank = jnp.zeros((8, rows), jnp.int32)
    wk = jnp.zeros((GATE_W, rows), F32)
    for k, pick in enumerate(picks):
        e_k = jnp.sum(jnp.where(pick, ei, 0.0), axis=0, keepdims=True).astype(jnp.int32)
        r_k = jnp.sum(jnp.where(pick, slot, 0.0), axis=0, keepdims=True).astype(jnp.int32)
        eid = jnp.where(k8 == k, e_k, eid)
        rank = jnp.where(k8 == k, r_k, rank)
        wk = jnp.where(kw == k, sc[k] * (ROUTED_SCALE / tot), wk)
    return eid, rank, wk.T


def _mixer_tail(o, h_ref, m, gffn_ref, wrt_ref, bias_ref, hn_ref, n2_ref, eid_ref, rank_ref, w_ref, cnt_ref, run_ref):
    @pl.when((pl.program_id(0) % (pl.num_programs(0) // MOE_TOKEN_SPLITS) == 0) & (pl.program_id(1) == 0))
    def _():
        run_ref[...] = jnp.zeros_like(run_ref)

    hn = h_ref[0] + m[2:3] * o
    hn_ref[0] = hn
    n2 = _norm_mod(hn, gffn_ref[...], m[3:4], m[4:5])
    n2_ref[0] = _pack_bf16_pair(n2)
    eid, rank, wcols = _route(n2, wrt_ref[...], bias_ref[...], run_ref)
    eid_ref[0] = eid
    rank_ref[0] = rank
    w_ref[0] = wcols
    cnt_ref[0] = run_ref[...]


def _attn_out_kernel(a_ref, b_ref, h_ref, mods_ref, wo_ref, gffn_ref, wrt_ref, bias_ref,
                     hn_ref, n2_ref, eid_ref, rank_ref, w_ref, cnt_ref, run_ref):
    wa = MLA_HEADS * MLA_V
    o = _dot(a_ref[0], wo_ref[0:wa, :]) + _dot(b_ref[0], wo_ref[wa:, :])
    _mixer_tail(o, h_ref, mods_ref[0, 0], gffn_ref, wrt_ref, bias_ref, hn_ref, n2_ref, eid_ref, rank_ref, w_ref,
                cnt_ref, run_ref)


def _tail_outs(b, l, d):
    tl = TOKEN_TILE
    nt = l // tl
    sds = jax.ShapeDtypeStruct
    tok = lambda w: pl.BlockSpec((1, tl, w), lambda i, j: (i, j, 0))
    blk = pl.BlockSpec((1, 8, tl), lambda i, j: (i * nt + j, 0, 0))
    bps = b // MOE_TOKEN_SPLITS
    shapes = [sds((b, l, d), F32), sds((b, l, d // 2), jnp.int32), sds((b * nt, 8, tl), jnp.int32),
              sds((b * nt, 8, tl), jnp.int32), sds((b, l, GATE_W), F32),
              sds((MOE_TOKEN_SPLITS, N_EXPERTS, 1), F32)]
    specs = [tok(d), tok(d // 2), blk, blk, tok(GATE_W),
             pl.BlockSpec((1, N_EXPERTS, 1), lambda i, j: (i // bps, 0, 0))]
    return shapes, specs


def _attn_out(a, bm, h, mods, wo, gffn, wrt, bias, nct):
    b, l, d = h.shape
    tl = TOKEN_TILE
    tok = lambda w: pl.BlockSpec((1, tl, w), lambda i, j: (i, j, 0))
    full = lambda x: pl.BlockSpec(x.shape, lambda i, j: (0,) * x.ndim)
    shapes, specs = _tail_outs(b, l, d)
    return pl.pallas_call(
        _attn_out_kernel,
        out_shape=shapes,
        grid=(b, l // tl),
        in_specs=[tok(a.shape[2]), tok(bm.shape[2]), tok(d),
                  pl.BlockSpec((1, 1, N_MODS, d), lambda i, j: (i, jnp.where(j < nct, 0, 1), 0, 0)),
                  full(wo), full(gffn), full(wrt), full(bias)],
        out_specs=specs,
        scratch_shapes=[pltpu.VMEM((N_EXPERTS, 1), F32)],
        compiler_params=pltpu.CompilerParams(dimension_semantics=("arbitrary", "arbitrary"),
                                             vmem_limit_bytes=_vmem_limit(40)),
        name="attn_out",
    )(a, bm, h, mods, wo, gffn, wrt, bias)


def _moe_dest_kernel(off_ref, eid_ref, rank_ref, dest_ref):
    eid = eid_ref[...]
    dest = rank_ref[...]
    for e in range(N_EXPERTS):
        dest = dest + jnp.where(eid == e, off_ref[e], 0)
    dest_ref[...] = dest


def _moe_dest(off, eid, rank):
    return pl.pallas_call(
        _moe_dest_kernel,
        out_shape=jax.ShapeDtypeStruct(eid.shape, jnp.int32),
        in_specs=[pl.BlockSpec(memory_space=pltpu.SMEM),
                  pl.BlockSpec(eid.shape, lambda: (0, 0, 0)), pl.BlockSpec(eid.shape, lambda: (0, 0, 0))],
        out_specs=pl.BlockSpec(eid.shape, lambda: (0, 0, 0)),
        name="moe_dest",
    )(off, eid, rank)


def _sc_mesh():
    return plsc.VectorSubcoreMesh(core_axis_name="c", subcore_axis_name="s",
                                  num_cores=V7X_SC_CORES, num_subcores=V7X_SC_SUBCORES)


def _sc_chunk(rows_per_worker):
    return max(c for c in range(8, SC_MAX_CHUNK + 1, 8) if rows_per_worker % c == 0)


def _sc_dispatch(xp, dest, p_rows, t_base):
    w = xp.shape[1]
    t = dest.shape[1]
    tpw = t // V7X_SC_WORKERS
    ch = _sc_chunk(tpw)

    @functools.partial(
        pl.kernel, mesh=_sc_mesh(), out_type=jax.ShapeDtypeStruct((p_rows, w), xp.dtype),
        scratch_types=[pltpu.VMEM((ch, w), xp.dtype)] + [pltpu.VMEM((ch,), jnp.int32)] * TOP_K
        + [pltpu.SemaphoreType.DMA, pltpu.SemaphoreType.DMA],
        name="moe_dispatch")
    def run(x_hbm, dest_hbm, out_hbm, rows_v, *rest):
        idx, (sem_i, sem_o) = rest[:TOP_K], rest[TOP_K:]
        base = (lax.axis_index("s") * V7X_SC_CORES + lax.axis_index("c")) * tpw

        @pl.loop(0, tpw // ch)
        def _(i):
            t0 = base + i * ch
            loads = [pltpu.async_copy(dest_hbm.at[k, pl.ds(t0, ch)], idx[k], sem_i) for k in range(TOP_K)]
            pltpu.sync_copy(x_hbm.at[pl.ds(t_base + t0, ch)], rows_v)
            for c in loads:
                c.wait()
            puts = [pltpu.async_copy(rows_v, out_hbm.at[idx[k]], sem_o) for k in range(TOP_K)]
            for c in puts:
                c.wait()

    return run(xp, dest)


def _sc_gather(ys, dest, t):
    w = ys.shape[1]
    tpw = t // V7X_SC_WORKERS
    ch = _sc_chunk(tpw)

    @functools.partial(
        pl.kernel, mesh=_sc_mesh(), out_type=jax.ShapeDtypeStruct((TOP_K, t, w), ys.dtype),
        scratch_types=[pltpu.VMEM((ch, w), ys.dtype)] * 2 + [pltpu.VMEM((ch,), jnp.int32)] * TOP_K
        + [pltpu.SemaphoreType.DMA] * 5,
        name="moe_gather")
    def run(y_hbm, dest_hbm, out_hbm, rows_a, rows_b, *rest):
        idx, (sem_i, sem_ga, sem_gb, sem_wa, sem_wb) = rest[:TOP_K], rest[TOP_K:]
        rows, sem_g, sem_w = (rows_a, rows_b), (sem_ga, sem_gb), (sem_wa, sem_wb)
        base = (lax.axis_index("s") * V7X_SC_CORES + lax.axis_index("c")) * tpw

        @pl.loop(0, tpw // ch)
        def _(i):
            t0 = base + i * ch
            loads = [pltpu.async_copy(dest_hbm.at[k, pl.ds(t0, ch)], idx[k], sem_i) for k in range(TOP_K)]
            for c in loads:
                c.wait()
            gets, puts = [None] * TOP_K, [None] * TOP_K
            gets[0] = pltpu.async_copy(y_hbm.at[idx[0]], rows[0], sem_g[0])
            for k in range(TOP_K):
                if k + 1 < TOP_K:
                    if k >= 1:
                        puts[k - 1].wait()
                    gets[k + 1] = pltpu.async_copy(y_hbm.at[idx[k + 1]], rows[(k + 1) % 2], sem_g[(k + 1) % 2])
                gets[k].wait()
                puts[k] = pltpu.async_copy(rows[k % 2], out_hbm.at[k, pl.ds(t0, ch)], sem_w[k % 2])
            puts[TOP_K - 2].wait()
            puts[TOP_K - 1].wait()

    return run(ys, dest)


def _moe_expert_kernel(te_ref, nv_ref, x_ref, wg_ref, wu_ref, wd_ref, y_ref, wgb_ref, wub_ref, wdb_ref):
    i = pl.program_id(0)

    @pl.when((i == 0) | (te_ref[i] != te_ref[jnp.maximum(i - 1, 0)]))
    def _():
        wgb_ref[...] = wg_ref[0, 0].astype(BF16)
        wub_ref[...] = wu_ref[0, 0].astype(BF16)
        wdb_ref[...] = wd_ref[0, 0].astype(BF16)

    @pl.when(i < nv_ref[0])
    def _():
        lo, hi = _unpack_bf16_pair(x_ref[...])
        half = lo.shape[1]
        hg = _dot(lo, wgb_ref[0:half, :]) + _dot(hi, wgb_ref[half:, :])
        hu = _dot(lo, wub_ref[0:half, :]) + _dot(hi, wub_ref[half:, :])
        y_ref[...] = _pack_bf16_pair(_dot(_silu(hg) * hu, wdb_ref[...]))


def _moe_experts(tile_expert, n_valid, xs, wg, wu, wd, layer):
    p_rows, w = xs.shape
    tm = MOE_ROW_TILE
    _, _, d, f = wg.shape
    wspec = lambda shp: pl.BlockSpec((1, 1) + shp, lambda i, te, nv: (layer, te[i], 0, 0))
    return pl.pallas_call(
        _moe_expert_kernel,
        out_shape=jax.ShapeDtypeStruct((p_rows, w), xs.dtype),
        grid_spec=pltpu.PrefetchScalarGridSpec(
            num_scalar_prefetch=2, grid=(p_rows // tm,),
            in_specs=[pl.BlockSpec((tm, w), lambda i, te, nv: (jnp.minimum(i, nv[0] - 1), 0)),
                      wspec((d, f)), wspec((d, f)), wspec((f, d))],
            out_specs=pl.BlockSpec((tm, w), lambda i, te, nv: (i, 0)),
            scratch_shapes=[pltpu.VMEM((d, f), BF16), pltpu.VMEM((d, f), BF16), pltpu.VMEM((f, d), BF16)]),
        compiler_params=pltpu.CompilerParams(dimension_semantics=("arbitrary",),
                                             vmem_limit_bytes=_vmem_limit(32)),
        name="moe_experts",
    )(tile_expert, n_valid, xs, wg, wu, wd)


def _moe_combine_kernel(yg_ref, w_ref, xp_ref, sg_ref, su_ref, sd_ref, h_ref, mods_ref, gfin_ref, *rest, final_norm):
    o_ref = rest[-1]
    xlo, xhi = _unpack_bf16_pair(xp_ref[0])
    half = xlo.shape[1]
    hs = (_silu(_dot(xlo, sg_ref[0, 0:half, :]) + _dot(xhi, sg_ref[0, half:, :]))
          * (_dot(xlo, su_ref[0, 0:half, :]) + _dot(xhi, su_ref[0, half:, :])))
    acc = _dot(hs, sd_ref[0])
    lo = acc[:, :half]
    hi = acc[:, half:]
    w = w_ref[0]
    for k in range(TOP_K):
        ylo, yhi = _unpack_bf16_pair(yg_ref[k, 0])
        wk = w[:, k:k + 1]
        lo = lo + wk * ylo
        hi = hi + wk * yhi
    y = h_ref[0] + mods_ref[0, 0, N_MODS - 1:N_MODS, :] * jnp.concatenate([lo, hi], axis=1)
    if final_norm:
        y = _rms(y, gfin_ref[...])
    o_ref[0] = y


def _moe_combine(yg, wcols, xp, sg, su, sd, h, mods, gfin, prev, nct, layer, tile0, b0, final_norm):
    b, l, d = h.shape
    bs = yg.shape[1]
    tl = TOKEN_TILE
    tok = lambda w: pl.BlockSpec((1, tl, w), lambda i, j: (i + b0, j + tile0, 0))
    lay = lambda x: pl.BlockSpec((1,) + x.shape[1:], lambda i, j: (layer,) + (0,) * (x.ndim - 1))
    args = [yg, wcols, xp, sg, su, sd, h, mods, gfin]
    in_specs = [pl.BlockSpec((TOP_K, 1, tl, d // 2), lambda i, j: (0, i, j + tile0, 0)), tok(GATE_W), tok(d // 2),
                lay(sg), lay(su), lay(sd), tok(d),
                pl.BlockSpec((1, 1, N_MODS, d), lambda i, j: (i + b0, jnp.where(j + tile0 < nct, 0, 1), 0, 0)),
                pl.BlockSpec(gfin.shape, lambda i, j: (0, 0))]
    aliases = {}
    if prev is not None:
        args.append(prev)
        in_specs.append(pl.BlockSpec(memory_space=pl.ANY))
        aliases = {len(args) - 1: 0}
    return pl.pallas_call(
        functools.partial(_moe_combine_kernel, final_norm=final_norm),
        out_shape=jax.ShapeDtypeStruct((b, l - tile0 * tl, d), F32),
        grid=(bs, l // tl - tile0),
        in_specs=in_specs,
        out_specs=pl.BlockSpec((1, tl, d), lambda i, j: (i + b0, j, 0)),
        input_output_aliases=aliases,
        compiler_params=pltpu.CompilerParams(dimension_semantics=("parallel", "parallel"),
                                             vmem_limit_bytes=_vmem_limit(40)),
        name="moe_combine",
    )(*args)


def _moe_sparse(n2p, eid, rank, wcols, counts, h, mods, wg, wu, wd, sg, su, sd, gfin, nct, layer, last):
    b, l, d = h.shape
    bs = b // MOE_TOKEN_SPLITS
    t = bs * l
    nblk = eid.shape[0] // MOE_TOKEN_SPLITS
    tm = MOE_ROW_TILE
    n_tiles = -(-(TOP_K * t + N_EXPERTS * (tm - 1)) // tm)
    x_rows = n2p.reshape(b * l, d // 2)
    out = None
    for s in range(MOE_TOKEN_SPLITS):
        tiles_e = (counts[s].reshape(N_EXPERTS).astype(jnp.int32) + (tm - 1)) // tm
        tile_end = jnp.cumsum(tiles_e)
        off = (tile_end - tiles_e) * tm
        n_valid = tile_end[-1:]
        tile_id = jnp.minimum(jnp.arange(n_tiles, dtype=jnp.int32), n_valid - 1)
        tile_expert = jnp.sum((tile_end[None, :] <= tile_id[:, None]).astype(jnp.int32), axis=1)
        blk = slice(s * nblk, (s + 1) * nblk)
        dest = _moe_dest(off, eid[blk], rank[blk]).transpose(1, 0, 2).reshape(8, t)
        xs = _sc_dispatch(x_rows, dest, n_tiles * tm, s * t)
        ys = _moe_experts(tile_expert, n_valid, xs, wg, wu, wd, layer)
        yg = _sc_gather(ys, dest, t).reshape(TOP_K, bs, l, d // 2)
        out = _moe_combine(yg, wcols, n2p, sg, su, sd, h, mods, gfin, out, nct, layer, nct if last else 0,
                           s * bs, last)
    return out


def _rwkv_proj_kernel(h_ref, hp_ref, hx_ref, mods_ref, g_ref, mu_ref, wr_ref, wk_ref, wv_ref, g1_ref, g2_ref,
                      w1_ref, w2_ref, a1_ref, a2_ref, w0_ref, a0_ref, kk_ref, ka_ref, rk_ref, bd_ref,
                      r_out, v_out, kk_out, g_out, km_out, b_out, lw_out, bonus_out, *, nct):
    j = pl.program_id(1)
    nt = pl.num_programs(1)
    m = mods_ref[0, 0]
    g = g_ref[...]
    n = _norm_mod(h_ref[0], g, m[0:1], m[1:2])
    tl, d = n.shape
    seg_first = (j == 0) | (j == nct)
    seg_last = (j == nct - 1) | (j == nt - 1)
    n_prev = _norm_mod(hp_ref[0], g, m[0:1], m[1:2])[7:8] * jnp.where(seg_first, 0.0, 1.0)
    n_next = _norm_mod(hx_ref[0], g, m[0:1], m[1:2])[0:1] * jnp.where(seg_last, 0.0, 1.0)
    row = lax.broadcasted_iota(jnp.int32, (tl, 1), 0)
    prev = jnp.where(row == 0, n_prev, pltpu.roll(n, 1, axis=0))
    nxt = jnp.where(row == tl - 1, n_next, pltpu.roll(n, tl - 1, axis=0))
    lane = lax.broadcasted_iota(jnp.int32, (1, d), 1)
    xx = jnp.where(lane < d // 2, prev, nxt) - n
    mu = mu_ref[...]
    xr, xw, xk, xv, xa, xg = [n + xx * mu[i:i + 1] for i in range(6)]
    r = _dot(xr, wr_ref[...])
    k = _dot(xk, wk_ref[...])
    v = _dot(xv, wv_ref[...])
    g_out[0] = _dot(_sigmoid(_dot(xg, g1_ref[...])), g2_ref[...]).astype(g_out.dtype)
    tw = jnp.tanh(_dot(xw, w1_ref[...]))
    ta = _dot(xa, a1_ref[...])
    bd = bd_ref[...]
    kk = k * kk_ref[...]
    kk = kk / jnp.maximum(jnp.sqrt(_head_sum(kk * kk, bd)), 1e-12)
    r_out[0] = r.astype(r_out.dtype)
    v_out[0] = v.astype(v_out.dtype)
    kk_out[0] = kk.astype(kk_out.dtype)
    bonus = jnp.zeros_like(v)
    for dr in range(2):
        zw = w0_ref[dr:dr + 1, :] + _dot(tw, w2_ref[dr])
        lw_out[dr, 0] = -jnp.exp(-0.5) * _sigmoid(zw)
        a = _sigmoid(a0_ref[dr:dr + 1, :] + _dot(ta, a2_ref[dr]))
        km = k * (1.0 + (a - 1.0) * ka_ref[...])
        km_out[dr, 0] = km.astype(km_out.dtype)
        b_out[dr, 0] = (kk * a).astype(b_out.dtype)
        bonus = bonus + _head_sum(r * km * rk_ref[...], bd) * v
    bonus_out[0] = bonus


def _rwkv_proj(h, mods, g, mu, wr, wk, wv, g1, g2, w1, w2, a1, a2, w0, a0, kk, ka, rk, bd, nct):
    b, l, d = h.shape
    tl = TOKEN_TILE
    nb8 = l // 8
    tok = pl.BlockSpec((1, tl, d), lambda i, j: (i, j, 0))
    tok2 = pl.BlockSpec((2, 1, tl, d), lambda i, j: (0, i, j, 0))
    full = lambda x: pl.BlockSpec(x.shape, lambda i, j: (0,) * x.ndim)
    sds = jax.ShapeDtypeStruct
    return pl.pallas_call(
        functools.partial(_rwkv_proj_kernel, nct=nct),
        out_shape=[sds((b, l, d), BF16), sds((b, l, d), BF16), sds((b, l, d), BF16), sds((b, l, d), BF16),
                   sds((2, b, l, d), BF16), sds((2, b, l, d), BF16), sds((2, b, l, d), F32), sds((b, l, d), F32)],
        grid=(b, l // tl),
        in_specs=[tok,
                  pl.BlockSpec((1, 8, d), lambda i, j: (i, jnp.maximum(j * (tl // 8) - 1, 0), 0)),
                  pl.BlockSpec((1, 8, d), lambda i, j: (i, jnp.minimum((j + 1) * (tl // 8), nb8 - 1), 0)),
                  pl.BlockSpec((1, 1, N_MODS, d), lambda i, j: (i, jnp.where(j < nct, 0, 1), 0, 0)),
                  full(g), full(mu), full(wr), full(wk), full(wv), full(g1), full(g2), full(w1), full(w2),
                  full(a1), full(a2), full(w0), full(a0), full(kk), full(ka), full(rk), full(bd)],
        out_specs=[tok, tok, tok, tok, tok2, tok2, tok2, tok],
        compiler_params=pltpu.CompilerParams(dimension_semantics=("parallel", "parallel"),
                                             vmem_limit_bytes=_vmem_limit(56)),
        name="rwkv_proj",
    )(h, h, h, mods, g, mu, wr, wk, wv, g1, g2, w1, w2, a1, a2, w0, a0, kk, ka, rk, bd)


def _wkv_kernel(r_ref, v_ref, kk_ref, km_ref, b_ref, lw_ref, y_ref, st_ref):
    c = WKV_CHUNK
    w = WKV_PAIR
    rev = pl.program_id(0)
    sign = 1 - 2 * rev

    @pl.when(pl.program_id(2) == 0)
    def _():
        st_ref[...] = jnp.zeros_like(st_ref)

    ti = lax.broadcasted_iota(jnp.int32, (c, c), 0)
    si = lax.broadcasted_iota(jnp.int32, (c, c), 1)
    tri = jnp.where((si - ti) * sign <= 0, 1.0, 0.0).astype(F32)
    nsub = WKV_CHUNKS_PER_STEP
    subs = [pl.ds(pl.multiple_of(jnp.where(rev == 0, s, nsub - 1 - s) * c, c), c) for s in range(nsub)]
    rt, kt, kh, bh, v32, e_mid = [], [], [], [], [], []
    for rows in subs:
        lw = lw_ref[0, 0, rows, :]
        l_incl = jnp.dot(tri, lw, precision=HIGHEST, preferred_element_type=F32)
        mid = 0.5 * jnp.sum(lw, axis=0, keepdims=True)
        e_neg = jnp.exp(mid - l_incl)
        e_mid.append(jnp.exp(mid))
        rt.append(r_ref[0, rows, :].astype(F32) * jnp.exp(l_incl - mid))
        kt.append(kk_ref[0, rows, :].astype(F32) * jnp.exp(l_incl - lw - mid))
        kh.append(km_ref[0, 0, rows, :].astype(F32) * e_neg)
        bh.append(b_ref[0, 0, rows, :].astype(F32) * e_neg)
        v32.append(v_ref[0, rows, :].astype(F32))

    ri = lax.broadcasted_iota(jnp.int32, (w, w), 0)
    ci = lax.broadcasted_iota(jnp.int32, (w, w), 1)
    same = (ri // c) == (ci // c)
    dlt = (ci % c - ri % c) * sign
    strict = same & (dlt < 0)
    incl = same & (dlt <= 0)
    eye = jnp.where(ri == ci, 1.0, 0.0).astype(F32)
    lane = lax.broadcasted_iota(jnp.int32, (1, w), 1)
    h0 = lane < RWKV_HEAD

    def rows2(x):
        return jnp.concatenate([jnp.where(h0, x, 0.0), jnp.where(h0, 0.0, x)], axis=0)

    def fold(x):
        return x[:c] + x[c:]

    npair = st_ref.shape[0]
    items = [(s, slice(p * w, (p + 1) * w)) for s in range(nsub) for p in range(npair)]
    n = range(len(items))
    em = [e_mid[s][:, sl] for s, sl in items]
    g = [_dot_nt(jnp.concatenate([rows2(kt[s][:, sl]), rows2(rt[s][:, sl])], axis=0),
                 jnp.concatenate([kh[s][:, sl], kh[s][:, sl], bh[s][:, sl], bh[s][:, sl]], axis=0)) for s, sl in items]
    a_kk = [jnp.where(strict, x[:w, :w], 0.0) for x in g]
    a_rk = [jnp.where(incl, x[w:, :w], 0.0) for x in g]
    a_rb = [jnp.where(incl, x[w:, w:], 0.0) for x in g]
    vi = [v32[s][:, sl] for s, sl in items]
    v_rows = [rows2(x) for x in vi]
    r_pre = [_dot(a_kk[i], v_rows[i]) for i in n]
    m = [jnp.where(strict, -x[:w, w:], 0.0) for x in g]
    tinv = [eye + x for x in m]
    m = [_dot(x, x) for x in m]
    for _ in range(c.bit_length() - 3):
        both = [_dot(jnp.concatenate([tinv[i], m[i]], axis=0), m[i]) for i in n]
        tinv = [tinv[i] + both[i][:w] for i in n]
        m = [x[w:] for x in both]
    tinv = [tinv[i] + _dot(tinv[i], m[i]) for i in n]
    sol = [_dot(tinv[i], jnp.concatenate([r_pre[i], rows2(kt[s][:, sl] * em[i])], axis=1))
           for i, (s, sl) in enumerate(items)]
    u_rows = [x[:, :w] for x in sol]
    kq_rows = [x[:, w:] for x in sol]
    y_pre = [fold(_dot(jnp.concatenate([a_rk[i], -a_rb[i]], axis=1),
                       jnp.concatenate([v_rows[i], u_rows[i]], axis=0))) for i in n]
    r_eff = [rt[s][:, sl] * em[i] - fold(_dot(a_rb[i], kq_rows[i])) for i, (s, sl) in enumerate(items)]
    bbar = [bh[s][:, sl] * em[i] for i, (s, sl) in enumerate(items)]
    kbar = [kh[s][:, sl] * em[i] for i, (s, sl) in enumerate(items)]
    mmat = [eye * (em[i] * em[i]) - jnp.where(same, _dot_tn(fold(kq_rows[i]), bbar[i]), 0.0) for i in n]
    s_pre = [jnp.where(same, _dot_tn(jnp.concatenate([vi[i], -fold(u_rows[i])], axis=0),
                                     jnp.concatenate([kbar[i], bbar[i]], axis=0)), 0.0) for i in n]
    st = [st_ref[p] for p in range(npair)]
    for i, (s, sl) in enumerate(items):
        p = i % npair
        y_ref[0, 0, subs[s], sl] = _dot_nt(r_eff[i], st[p]) + y_pre[i]
        hi = st[p].astype(BF16)
        lo = (st[p] - hi.astype(F32)).astype(BF16)
        mb = mmat[i].astype(BF16)
        st[p] = (jnp.dot(hi, mb, preferred_element_type=F32) + jnp.dot(lo, mb, preferred_element_type=F32)
                 + s_pre[i])
    for p in range(npair):
        st_ref[p] = st[p]


def _wkv(r, v, kk, km, bv, lw, lc):
    b, l, d = r.shape
    c = WKV_CHUNK * WKV_CHUNKS_PER_STEP
    ncc = lc // c
    nlc = (l - lc) // c

    def chunk(dr, i):
        return jnp.where(dr == 0, i, jnp.where(i < ncc, ncc - 1 - i, nlc + 2 * ncc - 1 - i))

    shared = pl.BlockSpec((1, c, d), lambda dr, bi, i: (bi, chunk(dr, i), 0))
    per_dir = pl.BlockSpec((1, 1, c, d), lambda dr, bi, i: (dr, bi, chunk(dr, i), 0))
    return pl.pallas_call(
        _wkv_kernel,
        out_shape=jax.ShapeDtypeStruct((2, b, l, d), F32),
        grid=(2, b, l // c),
        in_specs=[shared, shared, shared, per_dir, per_dir, per_dir],
        out_specs=per_dir,
        scratch_shapes=[pltpu.VMEM((d // WKV_PAIR, WKV_PAIR, WKV_PAIR), F32)],
        compiler_params=pltpu.CompilerParams(dimension_semantics=("parallel", "parallel", "arbitrary"),
                                             vmem_limit_bytes=_vmem_limit(32)),
        name="wkv7_chunked",
    )(r, v, kk, km, bv, lw)


def _rwkv_out_kernel(y_ref, bonus_ref, g_ref, lnw_ref, lnb_ref, wo_ref, bd_ref, h_ref, mods_ref, gffn_ref,
                     wrt_ref, bias_ref, hn_ref, n2_ref, eid_ref, rank_ref, w_ref, cnt_ref, run_ref):
    y = y_ref[0, 0] + y_ref[1, 0]
    bd = bd_ref[...]
    mean = _head_sum(y, bd) * (1.0 / RWKV_HEAD)
    yc = y - mean
    var = _head_sum(yc * yc, bd) * (1.0 / RWKV_HEAD)
    yn = yc * lax.rsqrt(var + GN_EPS) * lnw_ref[...] + lnb_ref[...]
    out = (yn + bonus_ref[0]) * g_ref[0].astype(F32)
    _mixer_tail(_dot(out, wo_ref[...]), h_ref, mods_ref[0, 0], gffn_ref, wrt_ref, bias_ref, hn_ref, n2_ref,
                eid_ref, rank_ref, w_ref, cnt_ref, run_ref)


def _rwkv_out(y, bonus, g, lnw, lnb, wo, bd, h, mods, gffn, wrt, bias, nct):
    b, l, d = h.shape
    tl = TOKEN_TILE
    tok = lambda w: pl.BlockSpec((1, tl, w), lambda i, j: (i, j, 0))
    full = lambda x: pl.BlockSpec(x.shape, lambda i, j: (0,) * x.ndim)
    shapes, specs = _tail_outs(b, l, d)
    return pl.pallas_call(
        _rwkv_out_kernel,
        out_shape=shapes,
        grid=(b, l // tl),
        in_specs=[pl.BlockSpec((2, 1, tl, d), lambda i, j: (0, i, j, 0)), tok(d), tok(d),
                  full(lnw), full(lnb), full(wo), full(bd), tok(d),
                  pl.BlockSpec((1, 1, N_MODS, d), lambda i, j: (i, jnp.where(j < nct, 0, 1), 0, 0)),
                  full(gffn), full(wrt), full(bias)],
        out_specs=specs,
        scratch_shapes=[pltpu.VMEM((N_EXPERTS, 1), F32)],
        compiler_params=pltpu.CompilerParams(dimension_semantics=("arbitrary", "arbitrary"),
                                             vmem_limit_bytes=_vmem_limit(40)),
        name="rwkv_out",
    )(y, bonus, g, lnw, lnb, wo, bd, h, mods, gffn, wrt, bias)


def _rope_table(n_lat, n_ctx):
    dim = SWA_HEAD_DIM
    nf = dim // 4
    inv = ROPE_THETA ** (-jnp.arange(nf, dtype=F32) / nf)
    row = jnp.repeat(jnp.arange(n_lat // GRID_W, dtype=F32), GRID_W)
    col = jnp.tile(jnp.arange(GRID_W, dtype=F32), n_lat // GRID_W)
    ar = row[:, None] * inv
    ac = col[:, None] * inv
    ang = jnp.concatenate([ar, ar, ac, ac], axis=-1)
    cos = jnp.concatenate([jnp.ones((n_ctx, dim), F32), jnp.cos(ang)], axis=0)
    sin = jnp.concatenate([jnp.zeros((n_ctx, dim), F32), jnp.sin(ang)], axis=0)
    return jnp.tile(cos, (1, 2)), jnp.tile(sin, (1, 2))


def _layout_attn_weights(w_in, w_uq, w_ukv):
    d = w_in.shape[0]
    s0 = MLA_Q_RANK
    s1 = s0 + MLA_KV_RANK
    s2 = s1 + MLA_ROPE
    s3 = s2 + SWA_HEADS * SWA_HEAD_DIM
    s4 = s3 + SWA_KV_HEADS * SWA_HEAD_DIM
    rep = lambda w: jnp.concatenate(
        [jnp.tile(w[:, g * SWA_HEAD_DIM:(g + 1) * SWA_HEAD_DIM], (1, SWA_GROUP)) for g in range(SWA_KV_HEADS)], axis=1)
    win = jnp.concatenate([w_in[:, :s1], w_in[:, s2:s3], rep(w_in[:, s3:s4]), rep(w_in[:, s4:]),
                           w_in[:, s1:s2], jnp.zeros((d, V7X_LANES - MLA_ROPE), w_in.dtype)], axis=1)
    qh = MLA_NOPE + MLA_ROPE
    pad = jnp.zeros((w_uq.shape[0], V7X_MXU_DIM - qh), w_uq.dtype)
    wuq = jnp.concatenate([jnp.concatenate([w_uq[:, h * qh:(h + 1) * qh], pad], axis=1) for h in range(MLA_HEADS)], axis=1)
    kvh = MLA_NOPE + MLA_V
    wuk = jnp.concatenate([w_ukv[:, h * kvh:h * kvh + MLA_NOPE] for h in range(MLA_HEADS)], axis=1)
    wuvt = jnp.concatenate([w_ukv[:, h * kvh + MLA_NOPE:(h + 1) * kvh] for h in range(MLA_HEADS)], axis=1).T
    return win.astype(BF16), wuq.astype(BF16), wuk.astype(BF16), wuvt.astype(BF16)


def _lora_pair(w_down, w_up):
    rank = w_down.shape[2]
    down = jnp.concatenate([w_down[0], w_down[1]], axis=1)
    z = jnp.zeros((rank, w_up.shape[2]), w_up.dtype)
    up = jnp.stack([jnp.concatenate([w_up[0], z], axis=0), jnp.concatenate([z, w_up[1]], axis=0)], axis=0)
    return down.astype(BF16), up.astype(BF16)


def _head_block_diag():
    i = jnp.arange(V7X_MXU_DIM) // RWKV_HEAD
    return (i[:, None] == i[None, :]).astype(BF16)


def kernel(x, c, ctx, c_ctx, ada_w, ada_b, norm_mix, norm_ffn, norm_final, attn_w_in, attn_q_norm, attn_kv_norm, attn_w_uq, attn_w_ukv, attn_sinks, attn_w_o, rwkv_mu, rwkv_w_r, rwkv_w_k, rwkv_w_v, rwkv_w_o, rwkv_g1, rwkv_g2, rwkv_w0, rwkv_w1, rwkv_w2, rwkv_a0, rwkv_a1, rwkv_a2, rwkv_k_k, rwkv_k_a, rwkv_r_k, rwkv_ln_w, rwkv_ln_b, moe_router, moe_bias, moe_w_gate, moe_w_up, moe_w_down, moe_ws_gate, moe_ws_up, moe_ws_down):
    bsz, s, d = x.shape
    lc = ctx.shape[1]
    l = lc + s
    depth = ada_w.shape[0]
    nct = lc // TOKEN_TILE
    assert lc % TOKEN_TILE == 0 and s % TOKEN_TILE == 0 and s >= SWA_BAND
    assert lc % (WKV_CHUNK * WKV_CHUNKS_PER_STEP) == 0
    assert d % V7X_MXU_DIM == 0 and WKV_CHUNK * 2 == V7X_LANES
    ngrp = SAMPLE_GROUPS
    bg = bsz // ngrp
    assert bsz % (ngrp * MOE_TOKEN_SPLITS) == 0 and (bg // MOE_TOKEN_SPLITS * l) % (8 * V7X_SC_WORKERS) == 0

    hs = [jnp.concatenate([ctx[g * bg:(g + 1) * bg], x[g * bg:(g + 1) * bg]], axis=1) for g in range(ngrp)]
    cos, sin = _rope_table(s, lc)
    bd = _head_block_diag()
    rows = -(-(bsz + 1) // 8) * 8
    cc = jnp.concatenate([c, c_ctx[None, :], jnp.zeros((rows - bsz - 1, d), F32)], axis=0)
    row2 = lambda a: a.reshape(1, -1)

    for li in range(depth):
        with_ctx = li < depth - 1
        i = li // 2
        ada = _ada_mods(cc, ada_w, ada_b, li)
        mods_all = jnp.stack([jnp.broadcast_to(ada[bsz].reshape(1, N_MODS, d), (bsz, N_MODS, d)),
                              ada[:bsz].reshape(bsz, N_MODS, d)], axis=1)
        wrt = jnp.concatenate([moe_router[li].T, jnp.zeros((GATE_W - N_EXPERTS, d), F32)], axis=0)
        bias = moe_bias[li].reshape(N_GROUPS, GROUP_SIZE, 1)
        if li % 2 == 0:
            win, wuq, wuk, wuvt = _layout_attn_weights(attn_w_in[i], attn_w_uq[i], attn_w_ukv[i])
            wo = attn_w_o[i].astype(BF16)
        else:
            w1, w2 = _lora_pair(rwkv_w1[i], rwkv_w2[i])
            a1, a2 = _lora_pair(rwkv_a1[i], rwkv_a2[i])
            wr, wk, wv, wo = [w[i].astype(BF16) for w in (rwkv_w_r, rwkv_w_k, rwkv_w_v, rwkv_w_o)]
            g1, g2 = rwkv_g1[i].astype(BF16), rwkv_g2[i].astype(BF16)
        for g in range(ngrp):
            h = hs[g]
            mods = mods_all[g * bg:(g + 1) * bg]
            if li % 2 == 0:
                q, k, vt, qs, ks, vs = _attn_proj(h, mods, row2(norm_mix[li]), win, row2(attn_q_norm[i]),
                                                  row2(attn_kv_norm[i]), wuq, wuk, wuvt, cos, sin, nct)
                a = _mla_attention(q, k, vt, lc, 0 if with_ctx else lc // MLA_Q_TILE)
                bm = _swa_attention(attn_sinks[i], qs, ks, vs, lc, 0 if with_ctx else lc // SWA_Q_TILE)
                tail = _attn_out(a, bm, h, mods, wo, row2(norm_ffn[li]), wrt, bias, nct)
            else:
                r, v, kk, gt, km, bv, lw, bonus = _rwkv_proj(
                    h, mods, row2(norm_mix[li]), rwkv_mu[i], wr, wk, wv, g1, g2, w1, w2, a1, a2,
                    rwkv_w0[i], rwkv_a0[i], row2(rwkv_k_k[i]), row2(rwkv_k_a[i]), row2(rwkv_r_k[i]), bd, nct)
                y = _wkv(r, v, kk, km, bv, lw, lc)
                tail = _rwkv_out(y, bonus, gt, row2(rwkv_ln_w[i]), row2(rwkv_ln_b[i]), wo,
                                 bd, h, mods, row2(norm_ffn[li]), wrt, bias, nct)
            h, n2p, eid, rank, wcols, counts = tail
            hs[g] = _moe_sparse(n2p, eid, rank, wcols, counts, h, mods, moe_w_gate, moe_w_up, moe_w_down,
                                moe_ws_gate, moe_ws_up, moe_ws_down, row2(norm_final), nct, li, li == depth - 1)
    return jnp.concatenate(hs, axis=0)
```

```python
import functools

import jax
import jax.numpy as jnp
from jax import lax
from jax.experimental import pallas as pl
from jax.experimental.pallas import tpu as pltpu
from jax.experimental.pallas import tpu_sc as plsc

F32 = jnp.float32
BF16 = jnp.bfloat16
HIGHEST = lax.Precision.HIGHEST

GRID_W = 64
NORM_EPS = 1e-6
ROPE_THETA = 10000.0
NEG_INF = -1e30
N_MODS = 6

MLA_HEADS = 4
MLA_Q_RANK = 384
MLA_KV_RANK = 256
MLA_NOPE = 128
MLA_ROPE = 64
MLA_V = 128

SWA_HEADS = 8
SWA_KV_HEADS = 2
SWA_GROUP = SWA_HEADS // SWA_KV_HEADS
SWA_HEAD_DIM = 64
WINDOW = 128

RWKV_HEAD = 64
DECAY_LORA = 64
ICLR_LORA = 64
GATE_LORA = 128
GN_EPS = 64e-5

N_EXPERTS = 64
TOP_K = 6
N_GROUPS = 8
TOPK_GROUPS = 4
GROUP_SIZE = N_EXPERTS // N_GROUPS
ROUTED_SCALE = 2.5
GATE_W = 128

V7X_LANES = 128
V7X_MXU_DIM = 256
V7X_VMEM_BYTES = 64 * 1024 * 1024
V7X_SC_CORES = 2
V7X_SC_SUBCORES = 16
V7X_SC_WORKERS = V7X_SC_CORES * V7X_SC_SUBCORES

TOKEN_TILE = 256
MLA_Q_TILE = 256
MLA_HEADS_PER_STEP = 2
LOG2E = 1.4426950408889634
SWA_Q_TILE = 128
SWA_BAND = SWA_Q_TILE + 2 * WINDOW
WKV_CHUNK = 64
WKV_PAIR = 2 * RWKV_HEAD
WKV_CHUNKS_PER_STEP = 2
MOE_ROW_TILE = 512
SAMPLE_GROUPS = 2
SC_MAX_CHUNK = 64


def _vmem_limit(mib):
    return min(mib * 1024 * 1024, V7X_VMEM_BYTES - 4 * 1024 * 1024)


def _dot(a, b):
    return jnp.dot(a.astype(BF16), b.astype(BF16), preferred_element_type=F32)


def _dot_nt(a, b):
    return lax.dot_general(a.astype(BF16), b.astype(BF16), (((1,), (1,)), ((), ())),
                           preferred_element_type=F32)


def _dot_tn(a, b):
    return lax.dot_general(a.astype(BF16), b.astype(BF16), (((0,), (0,)), ((), ())),
                           preferred_element_type=F32)


def _sigmoid(x):
    return 1.0 / (1.0 + jnp.exp(-x))


def _silu(x):
    return x * _sigmoid(x)


def _rms(x, g):
    return x * lax.rsqrt(jnp.mean(x * x, axis=-1, keepdims=True) + NORM_EPS) * g


def _norm_mod(x, g, shift, scale):
    return _rms(x, g) * (1.0 + scale) + shift


def _split_dot(x, w):
    hi = x.astype(BF16)
    lo = (x - hi.astype(F32)).astype(BF16)
    return (jnp.dot(hi, w, preferred_element_type=F32) + jnp.dot(lo, w, preferred_element_type=F32))


def _head_sum(x, bd):
    w = bd.shape[0]
    parts = [_split_dot(x[:, c * w:(c + 1) * w], bd) for c in range(x.shape[1] // w)]
    return jnp.concatenate(parts, axis=1)


def _ada_kernel(c_ref, w_ref, b_ref, o_ref):
    s = _silu(c_ref[...])
    o_ref[...] = jnp.dot(s, w_ref[0], precision=HIGHEST, preferred_element_type=F32) + b_ref[0]


def _ada_mods(cc, w, b, layer):
    rows, d = cc.shape
    depth, _, n = w.shape
    return pl.pallas_call(
        _ada_kernel,
        out_shape=jax.ShapeDtypeStruct((rows, n), F32),
        grid=(n // d,),
        in_specs=[pl.BlockSpec((rows, d), lambda i: (0, 0)),
                  pl.BlockSpec((1, d, d), lambda i: (layer, 0, i)),
                  pl.BlockSpec((1, 1, d), lambda i: (layer, 0, i))],
        out_specs=pl.BlockSpec((rows, d), lambda i: (0, i)),
        compiler_params=pltpu.CompilerParams(dimension_semantics=("parallel",),
                                             vmem_limit_bytes=_vmem_limit(32)),
        name="ada_mods",
    )(cc, w, b.reshape(depth, 1, n))


def _rope128(x, cos, sin, first_half):
    rot = jnp.where(first_half, -pltpu.roll(x, V7X_LANES - 16, axis=1), pltpu.roll(x, 16, axis=1))
    return x * cos + rot * sin


_C_CQ = 0
_C_CKV = _C_CQ + MLA_Q_RANK
_C_QS = _C_CKV + MLA_KV_RANK
_C_KS = _C_QS + SWA_HEADS * SWA_HEAD_DIM
_C_VS = _C_KS + SWA_KV_HEADS * V7X_MXU_DIM
_C_KR = _C_VS + SWA_KV_HEADS * V7X_MXU_DIM
_C_END = _C_KR + V7X_LANES
_SWA_W = SWA_KV_HEADS * V7X_MXU_DIM
_MLA_QK_W = MLA_HEADS * V7X_MXU_DIM


def _stream_specs(stream, nct, tl):
    ctx_arr, lat_arr, b0, lat_off = stream
    d = ctx_arr.shape[2]
    return [pl.BlockSpec((1, tl, d), lambda i, j: (i + b0, jnp.minimum(j, nct - 1), 0)),
            pl.BlockSpec((1, tl, d), lambda i, j: (i + b0, jnp.maximum(j - nct, 0) + lat_off, 0))]


def _stream_tile(c_ref, x_ref, nct):
    rows = c_ref.shape[1]
    take_ctx = lax.broadcasted_iota(jnp.int32, (rows, 1), 0) < jnp.where(pl.program_id(1) < nct, rows, 0)
    return jnp.where(take_ctx, c_ref[0], x_ref[0])


def _attn_proj_kernel(c_ref, x_ref, mods_ref, g_ref, win_ref, qn_ref, kvn_ref, wuq_ref, wuk_ref, wuvt_ref, cos_ref,
                      sin_ref, q_ref, k_ref, vt_ref, qs_ref, ks_ref, vs_ref, *, nct):
    m = mods_ref[0, 0]
    n = _norm_mod(_stream_tile(c_ref, x_ref, nct), g_ref[...], m[0:1], m[1:2])
    u = _dot(n, win_ref[...])
    cos = cos_ref[...]
    sin = sin_ref[...]
    lane = lax.broadcasted_iota(jnp.int32, (1, V7X_LANES), 1)
    first_half = (lane % 32) < 16

    def rope(x):
        return _rope128(x, cos, sin, first_half)

    scale_a = (MLA_NOPE + MLA_ROPE) ** -0.5 * LOG2E
    scale_b = SWA_HEAD_DIM ** -0.5
    q = _dot(_rms(u[:, _C_CQ:_C_CKV], qn_ref[...]), wuq_ref[...])
    ckv = _rms(u[:, _C_CKV:_C_QS], kvn_ref[...])
    kn = _dot(ckv, wuk_ref[...])
    vt_ref[0] = _dot_nt(wuvt_ref[...], ckv).astype(BF16)
    kr = rope(u[:, _C_KR:_C_END]).astype(BF16)
    for h in range(MLA_HEADS):
        o = h * V7X_MXU_DIM
        q_ref[0, :, o:o + V7X_LANES] = (q[:, o:o + V7X_LANES] * scale_a).astype(BF16)
        q_ref[0, :, o + V7X_LANES:o + V7X_MXU_DIM] = (rope(q[:, o + V7X_LANES:o + V7X_MXU_DIM]) * scale_a).astype(BF16)
        k_ref[0, :, o:o + V7X_LANES] = kn[:, h * MLA_NOPE:(h + 1) * MLA_NOPE].astype(BF16)
        k_ref[0, :, o + V7X_LANES:o + V7X_MXU_DIM] = kr
    for c in range((_C_KS - _C_QS) // V7X_LANES):
        o = c * V7X_LANES
        qs_ref[0, :, o:o + V7X_LANES] = (rope(u[:, _C_QS + o:_C_QS + o + V7X_LANES]) * scale_b).astype(BF16)
    for c in range(_SWA_W // V7X_LANES):
        o = c * V7X_LANES
        ks_ref[0, :, o:o + V7X_LANES] = rope(u[:, _C_KS + o:_C_KS + o + V7X_LANES]).astype(BF16)
    vs_ref[0] = u[:, _C_VS:_C_KR].astype(BF16)


def _attn_proj(stream, b, l, mods, g, win, qn, kvn, wuq, wuk, wuvt, cos, sin, nct):
    d = stream[0].shape[2]
    tl = TOKEN_TILE
    tok = lambda w: pl.BlockSpec((1, tl, w), lambda i, j: (i, j, 0))
    full = lambda a: pl.BlockSpec(a.shape, lambda i, j: (0,) * a.ndim)
    sds = jax.ShapeDtypeStruct
    dv = MLA_HEADS * MLA_V
    return pl.pallas_call(
        functools.partial(_attn_proj_kernel, nct=nct),
        out_shape=[sds((b, l, _MLA_QK_W), BF16), sds((b, l, _MLA_QK_W), BF16), sds((b, dv, l), BF16),
                   sds((b, l, SWA_HEADS * SWA_HEAD_DIM), BF16), sds((b, l, _SWA_W), BF16), sds((b, l, _SWA_W), BF16)],
        grid=(b, l // tl),
        in_specs=_stream_specs(stream, nct, tl) + [
                  pl.BlockSpec((1, 1, N_MODS, d), lambda i, j: (i, jnp.where(j < nct, 0, 1), 0, 0)),
                  full(g), full(win), full(qn), full(kvn), full(wuq), full(wuk), full(wuvt),
                  pl.BlockSpec((tl, V7X_LANES), lambda i, j: (j, 0)),
                  pl.BlockSpec((tl, V7X_LANES), lambda i, j: (j, 0))],
        out_specs=[tok(_MLA_QK_W), tok(_MLA_QK_W), pl.BlockSpec((1, dv, tl), lambda i, j: (i, 0, j)),
                   tok(SWA_HEADS * SWA_HEAD_DIM), tok(_SWA_W), tok(_SWA_W)],
        compiler_params=pltpu.CompilerParams(dimension_semantics=("parallel", "parallel"),
                                             vmem_limit_bytes=_vmem_limit(48)),
        name="attn_proj",
    )(stream[0], stream[1], mods, g, win, qn, kvn, wuq, wuk, wuvt, cos, sin)


def _mla_kernel(q_ref, k_ref, vt_ref, o_ref, *, nct_q, lc):
    hw = V7X_MXU_DIM

    def attend(nk):
        st = [_dot_nt(k_ref[0, 0:nk, hh * hw:(hh + 1) * hw], q_ref[0, :, hh * hw:(hh + 1) * hw])
              for hh in range(MLA_HEADS_PER_STEP)]
        for hh, s in enumerate(st):
            p = jnp.exp2(s - jnp.max(s, axis=0, keepdims=True))
            den = jnp.sum(p, axis=0, keepdims=True)
            ot = _dot(vt_ref[0, hh * MLA_V:(hh + 1) * MLA_V, 0:nk], p) / den
            o_ref[0, :, hh * MLA_V:(hh + 1) * MLA_V] = ot.T.astype(o_ref.dtype)

    @pl.when(pl.program_id(2) < nct_q)
    def _():
        attend(lc)

    @pl.when(pl.program_id(2) >= nct_q)
    def _():
        attend(k_ref.shape[1])


def _mla_attention(q, k, vt, lc, q_tile0):
    b, l, _ = q.shape
    tq = MLA_Q_TILE
    hps = MLA_HEADS_PER_STEP
    return pl.pallas_call(
        functools.partial(_mla_kernel, nct_q=lc // tq - q_tile0, lc=lc),
        out_shape=jax.ShapeDtypeStruct((b, l, MLA_HEADS * MLA_V), BF16),
        grid=(b, MLA_HEADS // hps, l // tq - q_tile0),
        in_specs=[pl.BlockSpec((1, tq, hps * V7X_MXU_DIM), lambda i, h, j: (i, j + q_tile0, h)),
                  pl.BlockSpec((1, l, hps * V7X_MXU_DIM), lambda i, h, j: (i, 0, h)),
                  pl.BlockSpec((1, hps * MLA_V, l), lambda i, h, j: (i, h, 0))],
        out_specs=pl.BlockSpec((1, tq, hps * MLA_V), lambda i, h, j: (i, j + q_tile0, h)),
        compiler_params=pltpu.CompilerParams(dimension_semantics=("parallel", "parallel", "parallel"),
                                             vmem_limit_bytes=_vmem_limit(48)),
        name="mla_attention",
    )(q, k, vt)


def _swa_kernel(sink_ref, q_ref, k_ref, v_ref, o_ref, *, lc, q_tile0):
    tq = SWA_Q_TILE
    l = k_ref.shape[1]
    r0 = (pl.program_id(1) + q_tile0) * tq
    start = pl.multiple_of(jnp.clip(r0 - WINDOW, lc, l - SWA_BAND), tq)
    rows = SWA_GROUP * tq
    row = lax.broadcasted_iota(jnp.int32, (rows, 1), 0)
    qpos = jnp.where(r0 >= lc, r0, -l) + row % tq
    kpos = start + lax.broadcasted_iota(jnp.int32, (1, SWA_BAND), 1)
    valid = jnp.abs(qpos - kpos) <= WINDOW
    lane = lax.broadcasted_iota(jnp.int32, (1, V7X_MXU_DIM), 1)
    for g in range(SWA_KV_HEADS):
        sl = slice(g * V7X_MXU_DIM, (g + 1) * V7X_MXU_DIM)
        qg = q_ref[0, :, sl]
        zero = jnp.zeros_like(qg)
        head = [(lane // SWA_HEAD_DIM) == hh for hh in range(SWA_GROUP)]
        qstack = jnp.concatenate([jnp.where(head[hh], qg, zero) for hh in range(SWA_GROUP)], axis=0)
        sc = _dot_nt(qstack, k_ref[0, 0:lc, sl])
        sb = jnp.where(valid, _dot_nt(qstack, k_ref[0, pl.ds(start, SWA_BAND), sl]), NEG_INF)
        sk = jnp.zeros((rows, 1), F32)
        for hh in range(SWA_GROUP):
            sk = jnp.where(row // tq == hh, sink_ref[g * SWA_GROUP + hh], sk)
        mx = jnp.maximum(jnp.maximum(jnp.max(sc, axis=-1, keepdims=True), jnp.max(sb, axis=-1, keepdims=True)), sk)
        pc = jnp.exp(sc - mx)
        pb = jnp.exp(sb - mx)
        den = jnp.sum(pc, axis=-1, keepdims=True) + jnp.sum(pb, axis=-1, keepdims=True) + jnp.exp(sk - mx)
        ostack = (_dot(pc, v_ref[0, 0:lc, sl]) + _dot(pb, v_ref[0, pl.ds(start, SWA_BAND), sl])) / den
        o = jnp.zeros((tq, V7X_MXU_DIM), F32)
        for hh in range(SWA_GROUP):
            o = o + jnp.where(head[hh], ostack[hh * tq:(hh + 1) * tq], 0.0)
        o_ref[0, :, sl] = o.astype(o_ref.dtype)


def _swa_attention(sinks, q, k, v, lc, q_tile0):
    b, l, _ = q.shape
    tq = SWA_Q_TILE
    return pl.pallas_call(
        functools.partial(_swa_kernel, lc=lc, q_tile0=q_tile0),
        out_shape=jax.ShapeDtypeStruct((b, l, SWA_HEADS * SWA_HEAD_DIM), BF16),
        grid=(b, l // tq - q_tile0),
        in_specs=[pl.BlockSpec(memory_space=pltpu.SMEM),
                  pl.BlockSpec((1, tq, SWA_HEADS * SWA_HEAD_DIM), lambda i, j: (i, j + q_tile0, 0)),
                  pl.BlockSpec((1, l, _SWA_W), lambda i, j: (i, 0, 0)),
                  pl.BlockSpec((1, l, _SWA_W), lambda i, j: (i, 0, 0))],
        out_specs=pl.BlockSpec((1, tq, SWA_HEADS * SWA_HEAD_DIM), lambda i, j: (i, j + q_tile0, 0)),
        compiler_params=pltpu.CompilerParams(dimension_semantics=("parallel", "parallel"),
                                             vmem_limit_bytes=_vmem_limit(48)),
        name="swa_attention",
    )(sinks, q, k, v)


def _pack_bf16_pair(x):
    w = x.shape[1] // 2
    lo = pltpu.bitcast(x[:, :w].astype(BF16).astype(F32), jnp.int32)
    hi = pltpu.bitcast(x[:, w:].astype(BF16).astype(F32), jnp.int32)
    return lax.shift_right_logical(lo, jnp.int32(16)) | (hi & jnp.int32(-65536))


def _unpack_bf16_pair(p):
    return pltpu.bitcast(p << 16, F32), pltpu.bitcast(p & jnp.int32(-65536), F32)


def _route(n2, wrt, bias, run_ref):
    n_hi = n2.astype(BF16)
    n_lo = (n2 - n_hi.astype(F32)).astype(BF16)
    w_hi = wrt.astype(BF16)
    w_lo = (wrt - w_hi.astype(F32)).astype(BF16)
    logits = _dot_nt(w_hi, n_hi) + (_dot_nt(w_hi, n_lo) + _dot_nt(w_lo, n_hi))
    rows = logits.shape[1]
    shape3 = (N_GROUPS, GROUP_SIZE, rows)
    scores3 = _sigmoid(logits[0:N_EXPERTS]).reshape(shape3)
    choice = scores3 + bias
    ji = lax.broadcasted_iota(jnp.int32, shape3, 1).astype(F32)
    m1 = jnp.max(choice, axis=1, keepdims=True)
    first = jnp.min(jnp.where(choice == m1, ji, float(GROUP_SIZE)), axis=1, keepdims=True)
    m2 = jnp.max(jnp.where(ji == first, -jnp.inf, choice), axis=1, keepdims=True)
    gs = m1 + m2
    gidx = lax.broadcasted_iota(jnp.int32, gs.shape, 0).astype(F32)
    gsel = jnp.zeros_like(gs)
    for _ in range(TOPK_GROUPS):
        mx = jnp.max(gs, axis=0, keepdims=True)
        pick = gidx == jnp.min(jnp.where(gs == mx, gidx, float(N_GROUPS)), axis=0, keepdims=True)
        gsel = jnp.where(pick, 1.0, gsel)
        gs = jnp.where(pick, -jnp.inf, gs)
    cand = jnp.where(gsel > 0.0, choice, -jnp.inf).reshape(N_EXPERTS, rows)
    scores = scores3.reshape(N_EXPERTS, rows)
    ei = lax.broadcasted_iota(jnp.int32, (N_EXPERTS, rows), 0).astype(F32)
    picks = []
    for _ in range(TOP_K):
        mx = jnp.max(cand, axis=0, keepdims=True)
        pick = ei == jnp.min(jnp.where(cand == mx, ei, float(N_EXPERTS)), axis=0, keepdims=True)
        picks.append(pick)
        cand = jnp.where(pick, -jnp.inf, cand)
    esel = jnp.zeros((N_EXPERTS, rows), F32)
    for pick in picks:
        esel = jnp.where(pick, 1.0, esel)
    before = jnp.where(lax.broadcasted_iota(jnp.int32, (rows, rows), 0) < lax.broadcasted_iota(jnp.int32, (rows, rows), 1),
                       1.0, 0.0).astype(BF16)
    slot = jnp.dot(esel.astype(BF16), before, preferred_element_type=F32) + run_ref[...]
    run_ref[...] += jnp.sum(esel, axis=1, keepdims=True)
    sc = [jnp.sum(jnp.where(pick, scores, 0.0), axis=0, keepdims=True) for pick in picks]
    tot = sc[0]
    for x in sc[1:]:
        tot = tot + x
    k8 = lax.broadcasted_iota(jnp.int32, (8, rows), 0)
    kw = lax.broadcasted_iota(jnp.int32, (GATE_W, rows), 0)
    eid = jnp.zeros((8, rows), jnp.int32)
    rank = jnp.zeros((8, rows), jnp.int32)
    wk = jnp.zeros((GATE_W, rows), F32)
    for k, pick in enumerate(picks):
        e_k = jnp.sum(jnp.where(pick, ei, 0.0), axis=0, keepdims=True).astype(jnp.int32)
        r_k = jnp.sum(jnp.where(pick, slot, 0.0), axis=0, keepdims=True).astype(jnp.int32)
        eid = jnp.where(k8 == k, e_k, eid)
        rank = jnp.where(k8 == k, r_k, rank)
        wk = jnp.where(kw == k, sc[k] * (ROUTED_SCALE / tot), wk)
    return eid, rank, wk.T


def _mixer_tail(o, h, m, gffn_ref, wrt_ref, bias_ref, hn_ref, n2_ref, eid_ref, rank_ref, w_ref, cnt_ref, run_ref):
    @pl.when((pl.program_id(0) == 0) & (pl.program_id(1) == 0))
    def _():
        run_ref[...] = jnp.zeros_like(run_ref)

    hn = h + m[2:3] * o
    hn_ref[0] = hn
    n2 = _norm_mod(hn, gffn_ref[...], m[3:4], m[4:5])
    n2_ref[0] = _pack_bf16_pair(n2)
    eid, rank, wcols = _route(n2, wrt_ref[...], bias_ref[...], run_ref)
    eid_ref[0] = eid
    rank_ref[0] = rank
    w_ref[0] = wcols
    cnt_ref[...] = run_ref[...]


def _attn_out_kernel(a_ref, b_ref, c_ref, x_ref, mods_ref, wo_ref, gffn_ref, wrt_ref, bias_ref,
                     hn_ref, n2_ref, eid_ref, rank_ref, w_ref, cnt_ref, run_ref, *, nct):
    wa = MLA_HEADS * MLA_V
    o = _dot(a_ref[0], wo_ref[0:wa, :]) + _dot(b_ref[0], wo_ref[wa:, :])
    _mixer_tail(o, _stream_tile(c_ref, x_ref, nct), mods_ref[0, 0], gffn_ref, wrt_ref, bias_ref, hn_ref, n2_ref,
                eid_ref, rank_ref, w_ref, cnt_ref, run_ref)


def _tail_outs(b, l, d):
    tl = TOKEN_TILE
    nt = l // tl
    sds = jax.ShapeDtypeStruct
    tok = lambda w: pl.BlockSpec((1, tl, w), lambda i, j: (i, j, 0))
    blk = pl.BlockSpec((1, 8, tl), lambda i, j: (i * nt + j, 0, 0))
    shapes = [sds((b, l, d), F32), sds((b, l, d // 2), jnp.int32), sds((b * nt, 8, tl), jnp.int32),
              sds((b * nt, 8, tl), jnp.int32), sds((b, l, GATE_W), F32), sds((N_EXPERTS, 1), F32)]
    specs = [tok(d), tok(d // 2), blk, blk, tok(GATE_W), pl.BlockSpec((N_EXPERTS, 1), lambda i, j: (0, 0))]
    return shapes, specs


def _attn_out(a, bm, stream, mods, wo, gffn, wrt, bias, nct):
    b, l, _ = a.shape
    d = stream[0].shape[2]
    tl = TOKEN_TILE
    tok = lambda w: pl.BlockSpec((1, tl, w), lambda i, j: (i, j, 0))
    full = lambda x: pl.BlockSpec(x.shape, lambda i, j: (0,) * x.ndim)
    shapes, specs = _tail_outs(b, l, d)
    return pl.pallas_call(
        functools.partial(_attn_out_kernel, nct=nct),
        out_shape=shapes,
        grid=(b, l // tl),
        in_specs=[tok(a.shape[2]), tok(bm.shape[2])] + _stream_specs(stream, nct, tl) + [
                  pl.BlockSpec((1, 1, N_MODS, d), lambda i, j: (i, jnp.where(j < nct, 0, 1), 0, 0)),
                  full(wo), full(gffn), full(wrt), full(bias)],
        out_specs=specs,
        scratch_shapes=[pltpu.VMEM((N_EXPERTS, 1), F32)],
        compiler_params=pltpu.CompilerParams(dimension_semantics=("arbitrary", "arbitrary"),
                                             vmem_limit_bytes=_vmem_limit(40)),
        name="attn_out",
    )(a, bm, stream[0], stream[1], mods, wo, gffn, wrt, bias)


def _moe_dest_kernel(off_ref, eid_ref, rank_ref, dest_ref):
    eid = eid_ref[...]
    dest = rank_ref[...]
    for e in range(N_EXPERTS):
        dest = dest + jnp.where(eid == e, off_ref[e], 0)
    dest_ref[...] = dest


def _moe_dest(off, eid, rank):
    return pl.pallas_call(
        _moe_dest_kernel,
        out_shape=jax.ShapeDtypeStruct(eid.shape, jnp.int32),
        in_specs=[pl.BlockSpec(memory_space=pltpu.SMEM),
                  pl.BlockSpec(eid.shape, lambda: (0, 0, 0)), pl.BlockSpec(eid.shape, lambda: (0, 0, 0))],
        out_specs=pl.BlockSpec(eid.shape, lambda: (0, 0, 0)),
        name="moe_dest",
    )(off, eid, rank)


def _sc_mesh():
    return plsc.VectorSubcoreMesh(core_axis_name="c", subcore_axis_name="s",
                                  num_cores=V7X_SC_CORES, num_subcores=V7X_SC_SUBCORES)


def _sc_chunk(rows_per_worker):
    return max(c for c in range(8, SC_MAX_CHUNK + 1, 8) if rows_per_worker % c == 0)


def _sc_dispatch(xp, dest, p_rows):
    t, w = xp.shape
    tpw = t // V7X_SC_WORKERS
    ch = _sc_chunk(tpw)

    @functools.partial(
        pl.kernel, mesh=_sc_mesh(), out_type=jax.ShapeDtypeStruct((p_rows, w), xp.dtype),
        scratch_types=[pltpu.VMEM((ch, w), xp.dtype)] + [pltpu.VMEM((ch,), jnp.int32)] * TOP_K
        + [pltpu.SemaphoreType.DMA, pltpu.SemaphoreType.DMA],
        name="moe_dispatch")
    def run(x_hbm, dest_hbm, out_hbm, rows_v, *rest):
        idx, (sem_i, sem_o) = rest[:TOP_K], rest[TOP_K:]
        base = (lax.axis_index("s") * V7X_SC_CORES + lax.axis_index("c")) * tpw

        @pl.loop(0, tpw // ch)
        def _(i):
            t0 = base + i * ch
            loads = [pltpu.async_copy(dest_hbm.at[k, pl.ds(t0, ch)], idx[k], sem_i) for k in range(TOP_K)]
            pltpu.sync_copy(x_hbm.at[pl.ds(t0, ch)], rows_v)
            for c in loads:
                c.wait()
            puts = [pltpu.async_copy(rows_v, out_hbm.at[idx[k]], sem_o) for k in range(TOP_K)]
            for c in puts:
                c.wait()

    return run(xp, dest)


def _sc_gather(ys, dest, t):
    w = ys.shape[1]
    tpw = t // V7X_SC_WORKERS
    ch = _sc_chunk(tpw)

    @functools.partial(
        pl.kernel, mesh=_sc_mesh(), out_type=jax.ShapeDtypeStruct((TOP_K, t, w), ys.dtype),
        scratch_types=[pltpu.VMEM((ch, w), ys.dtype)] * 2 + [pltpu.VMEM((ch,), jnp.int32)] * TOP_K
        + [pltpu.SemaphoreType.DMA] * 5,
        name="moe_gather")
    def run(y_hbm, dest_hbm, out_hbm, rows_a, rows_b, *rest):
        idx, (sem_i, sem_ga, sem_gb, sem_wa, sem_wb) = rest[:TOP_K], rest[TOP_K:]
        rows, sem_g, sem_w = (rows_a, rows_b), (sem_ga, sem_gb), (sem_wa, sem_wb)
        base = (lax.axis_index("s") * V7X_SC_CORES + lax.axis_index("c")) * tpw

        @pl.loop(0, tpw // ch)
        def _(i):
            t0 = base + i * ch
            loads = [pltpu.async_copy(dest_hbm.at[k, pl.ds(t0, ch)], idx[k], sem_i) for k in range(TOP_K)]
            for c in loads:
                c.wait()
            gets, puts = [None] * TOP_K, [None] * TOP_K
            gets[0] = pltpu.async_copy(y_hbm.at[idx[0]], rows[0], sem_g[0])
            for k in range(TOP_K):
                if k + 1 < TOP_K:
                    if k >= 1:
                        puts[k - 1].wait()
                    gets[k + 1] = pltpu.async_copy(y_hbm.at[idx[k + 1]], rows[(k + 1) % 2], sem_g[(k + 1) % 2])
                gets[k].wait()
                puts[k] = pltpu.async_copy(rows[k % 2], out_hbm.at[k, pl.ds(t0, ch)], sem_w[k % 2])
            puts[TOP_K - 2].wait()
            puts[TOP_K - 1].wait()

    return run(ys, dest)


def _moe_expert_kernel(te_ref, nv_ref, x_ref, wg_ref, wu_ref, wd_ref, y_ref, wgb_ref, wub_ref, wdb_ref):
    i = pl.program_id(0)

    @pl.when((i == 0) | (te_ref[i] != te_ref[jnp.maximum(i - 1, 0)]))
    def _():
        wgb_ref[...] = wg_ref[0, 0].astype(BF16)
        wub_ref[...] = wu_ref[0, 0].astype(BF16)
        wdb_ref[...] = wd_ref[0, 0].astype(BF16)

    @pl.when(i < nv_ref[0])
    def _():
        lo, hi = _unpack_bf16_pair(x_ref[...])
        half = lo.shape[1]
        hg = _dot(lo, wgb_ref[0:half, :]) + _dot(hi, wgb_ref[half:, :])
        hu = _dot(lo, wub_ref[0:half, :]) + _dot(hi, wub_ref[half:, :])
        y_ref[...] = _pack_bf16_pair(_dot(_silu(hg) * hu, wdb_ref[...]))


def _moe_experts(tile_expert, n_valid, xs, wg, wu, wd, layer):
    p_rows, w = xs.shape
    tm = MOE_ROW_TILE
    _, _, d, f = wg.shape
    wspec = lambda shp: pl.BlockSpec((1, 1) + shp, lambda i, te, nv: (layer, te[i], 0, 0))
    return pl.pallas_call(
        _moe_expert_kernel,
        out_shape=jax.ShapeDtypeStruct((p_rows, w), xs.dtype),
        grid_spec=pltpu.PrefetchScalarGridSpec(
            num_scalar_prefetch=2, grid=(p_rows // tm,),
            in_specs=[pl.BlockSpec((tm, w), lambda i, te, nv: (jnp.minimum(i, nv[0] - 1), 0)),
                      wspec((d, f)), wspec((d, f)), wspec((f, d))],
            out_specs=pl.BlockSpec((tm, w), lambda i, te, nv: (jnp.minimum(i, nv[0] - 1), 0)),
            scratch_shapes=[pltpu.VMEM((d, f), BF16), pltpu.VMEM((d, f), BF16), pltpu.VMEM((f, d), BF16)]),
        compiler_params=pltpu.CompilerParams(dimension_semantics=("arbitrary",),
                                             vmem_limit_bytes=_vmem_limit(32)),
        name="moe_experts",
    )(tile_expert, n_valid, xs, wg, wu, wd)


def _moe_combine_kernel(yg_ref, w_ref, xp_ref, sg_ref, su_ref, sd_ref, h_ref, mods_ref, gfin_ref, *rest, final_norm):
    o_ref = rest[-1]
    xlo, xhi = _unpack_bf16_pair(xp_ref[0])
    half = xlo.shape[1]
    hs = (_silu(_dot(xlo, sg_ref[0, 0:half, :]) + _dot(xhi, sg_ref[0, half:, :]))
          * (_dot(xlo, su_ref[0, 0:half, :]) + _dot(xhi, su_ref[0, half:, :])))
    acc = _dot(hs, sd_ref[0])
    lo = acc[:, :half]
    hi = acc[:, half:]
    w = w_ref[0]
    for k in range(TOP_K):
        ylo, yhi = _unpack_bf16_pair(yg_ref[k, 0])
        wk = w[:, k:k + 1]
        lo = lo + wk * ylo
        hi = hi + wk * yhi
    y = h_ref[0] + mods_ref[0, 0, N_MODS - 1:N_MODS, :] * jnp.concatenate([lo, hi], axis=1)
    if final_norm:
        y = _rms(y, gfin_ref[...])
    o_ref[0] = y


def _moe_combine(yg, wcols, xp, sg, su, sd, h, mods, gfin, nct, layer, out_buf, out_b0, out_batch, latent_only,
                 final_norm):
    b, l, d = h.shape
    tl = TOKEN_TILE
    tile0 = nct if latent_only else 0
    tok = lambda w: pl.BlockSpec((1, tl, w), lambda i, j: (i, j + tile0, 0))
    lay = lambda x: pl.BlockSpec((1,) + x.shape[1:], lambda i, j: (layer,) + (0,) * (x.ndim - 1))
    args = [yg, wcols, xp, sg, su, sd, h, mods, gfin]
    in_specs = [pl.BlockSpec((TOP_K, 1, tl, d // 2), lambda i, j: (0, i, j + tile0, 0)), tok(GATE_W), tok(d // 2),
                lay(sg), lay(su), lay(sd), tok(d),
                pl.BlockSpec((1, 1, N_MODS, d), lambda i, j: (i, jnp.where(j + tile0 < nct, 0, 1), 0, 0)),
                pl.BlockSpec(gfin.shape, lambda i, j: (0, 0))]
    aliases = {}
    if out_buf is not None:
        args.append(out_buf)
        in_specs.append(pl.BlockSpec(memory_space=pl.ANY))
        aliases = {len(args) - 1: 0}
    return pl.pallas_call(
        functools.partial(_moe_combine_kernel, final_norm=final_norm),
        out_shape=jax.ShapeDtypeStruct((out_batch, l - tile0 * tl, d), F32),
        grid=(b, l // tl - tile0),
        in_specs=in_specs,
        out_specs=pl.BlockSpec((1, tl, d), lambda i, j: (i + out_b0, j, 0)),
        input_output_aliases=aliases,
        compiler_params=pltpu.CompilerParams(dimension_semantics=("parallel", "parallel"),
                                             vmem_limit_bytes=_vmem_limit(40)),
        name="moe_combine",
    )(*args)


def _moe_sparse(n2p, eid, rank, wcols, counts, h, mods, wg, wu, wd, sg, su, sd, gfin, nct, layer,
                out_buf=None, out_b0=0, out_batch=None, last=False):
    b, l, d = h.shape
    t = b * l
    tm = MOE_ROW_TILE
    n_tiles = -(-(TOP_K * t + N_EXPERTS * (tm - 1)) // tm)
    tiles_e = (counts.reshape(N_EXPERTS).astype(jnp.int32) + (tm - 1)) // tm
    tile_end = jnp.cumsum(tiles_e)
    off = (tile_end - tiles_e) * tm
    n_valid = tile_end[-1:]
    tile_id = jnp.minimum(jnp.arange(n_tiles, dtype=jnp.int32), n_valid - 1)
    tile_expert = jnp.sum((tile_end[None, :] <= tile_id[:, None]).astype(jnp.int32), axis=1)
    dest = _moe_dest(off, eid, rank).transpose(1, 0, 2).reshape(8, t)
    xs = _sc_dispatch(n2p.reshape(t, d // 2), dest, n_tiles * tm)
    ys = _moe_experts(tile_expert, n_valid, xs, wg, wu, wd, layer)
    yg = _sc_gather(ys, dest, t).reshape(TOP_K, b, l, d // 2)
    return _moe_combine(yg, wcols, n2p, sg, su, sd, h, mods, gfin, nct, layer, out_buf, out_b0,
                        b if out_batch is None else out_batch, last, last)


def _rwkv_proj_kernel(h_ref, hp_ref, hx_ref, mods_ref, g_ref, mu_ref, wr_ref, wk_ref, wv_ref, g1_ref, g2_ref,
                      w1_ref, w2_ref, a1_ref, a2_ref, w0_ref, a0_ref, kk_ref, ka_ref, rk_ref, bd_ref,
                      r_out, v_out, kk_out, g_out, km_out, b_out, lw_out, bonus_out, *, nct):
    j = pl.program_id(1)
    nt = pl.num_programs(1)
    m = mods_ref[0, 0]
    g = g_ref[...]
    n = _norm_mod(h_ref[0], g, m[0:1], m[1:2])
    tl, d = n.shape
    seg_first = (j == 0) | (j == nct)
    seg_last = (j == nct - 1) | (j == nt - 1)
    n_prev = _norm_mod(hp_ref[0], g, m[0:1], m[1:2])[7:8] * jnp.where(seg_first, 0.0, 1.0)
    n_next = _norm_mod(hx_ref[0], g, m[0:1], m[1:2])[0:1] * jnp.where(seg_last, 0.0, 1.0)
    row = lax.broadcasted_iota(jnp.int32, (tl, 1), 0)
    prev = jnp.where(row == 0, n_prev, pltpu.roll(n, 1, axis=0))
    nxt = jnp.where(row == tl - 1, n_next, pltpu.roll(n, tl - 1, axis=0))
    lane = lax.broadcasted_iota(jnp.int32, (1, d), 1)
    xx = jnp.where(lane < d // 2, prev, nxt) - n
    mu = mu_ref[...]
    xr, xw, xk, xv, xa, xg = [n + xx * mu[i:i + 1] for i in range(6)]
    r = _dot(xr, wr_ref[...])
    k = _dot(xk, wk_ref[...])
    v = _dot(xv, wv_ref[...])
    g_out[0] = _dot(_sigmoid(_dot(xg, g1_ref[...])), g2_ref[...]).astype(g_out.dtype)
    tw = jnp.tanh(_dot(xw, w1_ref[...]))
    ta = _dot(xa, a1_ref[...])
    bd = bd_ref[...]
    kk = k * kk_ref[...]
    kk = kk / jnp.maximum(jnp.sqrt(_head_sum(kk * kk, bd)), 1e-12)
    r_out[0] = r.astype(r_out.dtype)
    v_out[0] = v.astype(v_out.dtype)
    kk_out[0] = kk.astype(kk_out.dtype)
    bonus = jnp.zeros_like(v)
    for dr in range(2):
        zw = w0_ref[dr:dr + 1, :] + _dot(tw, w2_ref[dr])
        lw_out[dr, 0] = -jnp.exp(-0.5) * _sigmoid(zw)
        a = _sigmoid(a0_ref[dr:dr + 1, :] + _dot(ta, a2_ref[dr]))
        km = k * (1.0 + (a - 1.0) * ka_ref[...])
        km_out[dr, 0] = km.astype(km_out.dtype)
        b_out[dr, 0] = (kk * a).astype(b_out.dtype)
        bonus = bonus + _head_sum(r * km * rk_ref[...], bd) * v
    bonus_out[0] = bonus


def _rwkv_proj(h, mods, g, mu, wr, wk, wv, g1, g2, w1, w2, a1, a2, w0, a0, kk, ka, rk, bd, nct):
    b, l, d = h.shape
    tl = TOKEN_TILE
    nb8 = l // 8
    tok = pl.BlockSpec((1, tl, d), lambda i, j: (i, j, 0))
    tok2 = pl.BlockSpec((2, 1, tl, d), lambda i, j: (0, i, j, 0))
    full = lambda x: pl.BlockSpec(x.shape, lambda i, j: (0,) * x.ndim)
    sds = jax.ShapeDtypeStruct
    return pl.pallas_call(
        functools.partial(_rwkv_proj_kernel, nct=nct),
        out_shape=[sds((b, l, d), BF16), sds((b, l, d), BF16), sds((b, l, d), BF16), sds((b, l, d), BF16),
                   sds((2, b, l, d), BF16), sds((2, b, l, d), BF16), sds((2, b, l, d), F32), sds((b, l, d), F32)],
        grid=(b, l // tl),
        in_specs=[tok,
                  pl.BlockSpec((1, 8, d), lambda i, j: (i, jnp.maximum(j * (tl // 8) - 1, 0), 0)),
                  pl.BlockSpec((1, 8, d), lambda i, j: (i, jnp.minimum((j + 1) * (tl // 8), nb8 - 1), 0)),
                  pl.BlockSpec((1, 1, N_MODS, d), lambda i, j: (i, jnp.where(j < nct, 0, 1), 0, 0)),
                  full(g), full(mu), full(wr), full(wk), full(wv), full(g1), full(g2), full(w1), full(w2),
                  full(a1), full(a2), full(w0), full(a0), full(kk), full(ka), full(rk), full(bd)],
        out_specs=[tok, tok, tok, tok, tok2, tok2, tok2, tok],
        compiler_params=pltpu.CompilerParams(dimension_semantics=("parallel", "parallel"),
                                             vmem_limit_bytes=_vmem_limit(56)),
        name="rwkv_proj",
    )(h, h, h, mods, g, mu, wr, wk, wv, g1, g2, w1, w2, a1, a2, w0, a0, kk, ka, rk, bd)


def _wkv_kernel(r_ref, v_ref, kk_ref, km_ref, b_ref, lw_ref, y_ref, st_ref):
    c = WKV_CHUNK
    w = WKV_PAIR
    rev = pl.program_id(0)
    sign = 1 - 2 * rev

    @pl.when(pl.program_id(2) == 0)
    def _():
        st_ref[...] = jnp.zeros_like(st_ref)

    ti = lax.broadcasted_iota(jnp.int32, (c, c), 0)
    si = lax.broadcasted_iota(jnp.int32, (c, c), 1)
    tri = jnp.where((si - ti) * sign <= 0, 1.0, 0.0).astype(F32)
    nsub = WKV_CHUNKS_PER_STEP
    subs = [pl.ds(pl.multiple_of(jnp.where(rev == 0, s, nsub - 1 - s) * c, c), c) for s in range(nsub)]
    rt, kt, kh, bh, v32, e_mid = [], [], [], [], [], []
    for rows in subs:
        lw = lw_ref[0, 0, rows, :]
        l_incl = jnp.dot(tri, lw, precision=HIGHEST, preferred_element_type=F32)
        mid = 0.5 * jnp.sum(lw, axis=0, keepdims=True)
        e_neg = jnp.exp(mid - l_incl)
        e_mid.append(jnp.exp(mid))
        rt.append(r_ref[0, rows, :].astype(F32) * jnp.exp(l_incl - mid))
        kt.append(kk_ref[0, rows, :].astype(F32) * jnp.exp(l_incl - lw - mid))
        kh.append(km_ref[0, 0, rows, :].astype(F32) * e_neg)
        bh.append(b_ref[0, 0, rows, :].astype(F32) * e_neg)
        v32.append(v_ref[0, rows, :].astype(F32))

    ri = lax.broadcasted_iota(jnp.int32, (w, w), 0)
    ci = lax.broadcasted_iota(jnp.int32, (w, w), 1)
    same = (ri // c) == (ci // c)
    dlt = (ci % c - ri % c) * sign
    strict = same & (dlt < 0)
    incl = same & (dlt <= 0)
    eye = jnp.where(ri == ci, 1.0, 0.0).astype(F32)
    lane = lax.broadcasted_iota(jnp.int32, (1, w), 1)
    h0 = lane < RWKV_HEAD

    def rows2(x):
        return jnp.concatenate([jnp.where(h0, x, 0.0), jnp.where(h0, 0.0, x)], axis=0)

    def fold(x):
        return x[:c] + x[c:]

    npair = st_ref.shape[0]
    items = [(s, slice(p * w, (p + 1) * w)) for s in range(nsub) for p in range(npair)]
    n = range(len(items))
    em = [e_mid[s][:, sl] for s, sl in items]
    g = [_dot_nt(jnp.concatenate([rows2(kt[s][:, sl]), rows2(rt[s][:, sl])], axis=0),
                 jnp.concatenate([kh[s][:, sl], kh[s][:, sl], bh[s][:, sl], bh[s][:, sl]], axis=0)) for s, sl in items]
    a_kk = [jnp.where(strict, x[:w, :w], 0.0) for x in g]
    a_rk = [jnp.where(incl, x[w:, :w], 0.0) for x in g]
    a_rb = [jnp.where(incl, x[w:, w:], 0.0) for x in g]
    vi = [v32[s][:, sl] for s, sl in items]
    v_rows = [rows2(x) for x in vi]
    r_pre = [_dot(a_kk[i], v_rows[i]) for i in n]
    m = [jnp.where(strict, -x[:w, w:], 0.0) for x in g]
    tinv = [eye + x for x in m]
    m = [_dot(x, x) for x in m]
    for _ in range(c.bit_length() - 3):
        both = [_dot(jnp.concatenate([tinv[i], m[i]], axis=0), m[i]) for i in n]
        tinv = [tinv[i] + both[i][:w] for i in n]
        m = [x[w:] for x in both]
    tinv = [tinv[i] + _dot(tinv[i], m[i]) for i in n]
    sol = [_dot(tinv[i], jnp.concatenate([r_pre[i], rows2(kt[s][:, sl] * em[i])], axis=1))
           for i, (s, sl) in enumerate(items)]
    u_rows = [x[:, :w] for x in sol]
    kq_rows = [x[:, w:] for x in sol]
    y_pre = [fold(_dot(jnp.concatenate([a_rk[i], -a_rb[i]], axis=1),
                       jnp.concatenate([v_rows[i], u_rows[i]], axis=0))) for i in n]
    r_eff = [rt[s][:, sl] * em[i] - fold(_dot(a_rb[i], kq_rows[i])) for i, (s, sl) in enumerate(items)]
    bbar = [bh[s][:, sl] * em[i] for i, (s, sl) in enumerate(items)]
    kbar = [kh[s][:, sl] * em[i] for i, (s, sl) in enumerate(items)]
    mmat = [eye * (em[i] * em[i]) - jnp.where(same, _dot_tn(fold(kq_rows[i]), bbar[i]), 0.0) for i in n]
    s_pre = [jnp.where(same, _dot_tn(jnp.concatenate([vi[i], -fold(u_rows[i])], axis=0),
                                     jnp.concatenate([kbar[i], bbar[i]], axis=0)), 0.0) for i in n]
    st = [st_ref[p] for p in range(npair)]
    for i, (s, sl) in enumerate(items):
        p = i % npair
        y_ref[0, 0, subs[s], sl] = _dot_nt(r_eff[i], st[p]) + y_pre[i]
        hi = st[p].astype(BF16)
        lo = (st[p] - hi.astype(F32)).astype(BF16)
        mb = mmat[i].astype(BF16)
        st[p] = (jnp.dot(hi, mb, preferred_element_type=F32) + jnp.dot(lo, mb, preferred_element_type=F32)
                 + s_pre[i])
    for p in range(npair):
        st_ref[p] = st[p]


def _wkv(r, v, kk, km, bv, lw, lc):
    b, l, d = r.shape
    c = WKV_CHUNK * WKV_CHUNKS_PER_STEP
    ncc = lc // c
    nlc = (l - lc) // c

    def chunk(dr, i):
        return jnp.where(dr == 0, i, jnp.where(i < ncc, ncc - 1 - i, nlc + 2 * ncc - 1 - i))

    shared = pl.BlockSpec((1, c, d), lambda dr, bi, i: (bi, chunk(dr, i), 0))
    per_dir = pl.BlockSpec((1, 1, c, d), lambda dr, bi, i: (dr, bi, chunk(dr, i), 0))
    return pl.pallas_call(
        _wkv_kernel,
        out_shape=jax.ShapeDtypeStruct((2, b, l, d), F32),
        grid=(2, b, l // c),
        in_specs=[shared, shared, shared, per_dir, per_dir, per_dir],
        out_specs=per_dir,
        scratch_shapes=[pltpu.VMEM((d // WKV_PAIR, WKV_PAIR, WKV_PAIR), F32)],
        compiler_params=pltpu.CompilerParams(dimension_semantics=("parallel", "parallel", "arbitrary"),
                                             vmem_limit_bytes=_vmem_limit(32)),
        name="wkv7_chunked",
    )(r, v, kk, km, bv, lw)


def _rwkv_out_kernel(y_ref, bonus_ref, g_ref, lnw_ref, lnb_ref, wo_ref, bd_ref, h_ref, mods_ref, gffn_ref,
                     wrt_ref, bias_ref, hn_ref, n2_ref, eid_ref, rank_ref, w_ref, cnt_ref, run_ref):
    y = y_ref[0, 0] + y_ref[1, 0]
    bd = bd_ref[...]
    mean = _head_sum(y, bd) * (1.0 / RWKV_HEAD)
    yc = y - mean
    var = _head_sum(yc * yc, bd) * (1.0 / RWKV_HEAD)
    yn = yc * lax.rsqrt(var + GN_EPS) * lnw_ref[...] + lnb_ref[...]
    out = (yn + bonus_ref[0]) * g_ref[0].astype(F32)
    _mixer_tail(_dot(out, wo_ref[...]), h_ref[0], mods_ref[0, 0], gffn_ref, wrt_ref, bias_ref, hn_ref, n2_ref,
                eid_ref, rank_ref, w_ref, cnt_ref, run_ref)


def _rwkv_out(y, bonus, g, lnw, lnb, wo, bd, h, mods, gffn, wrt, bias, nct):
    b, l, d = h.shape
    tl = TOKEN_TILE
    tok = lambda w: pl.BlockSpec((1, tl, w), lambda i, j: (i, j, 0))
    full = lambda x: pl.BlockSpec(x.shape, lambda i, j: (0,) * x.ndim)
    shapes, specs = _tail_outs(b, l, d)
    return pl.pallas_call(
        _rwkv_out_kernel,
        out_shape=shapes,
        grid=(b, l // tl),
        in_specs=[pl.BlockSpec((2, 1, tl, d), lambda i, j: (0, i, j, 0)), tok(d), tok(d),
                  full(lnw), full(lnb), full(wo), full(bd), tok(d),
                  pl.BlockSpec((1, 1, N_MODS, d), lambda i, j: (i, jnp.where(j < nct, 0, 1), 0, 0)),
                  full(gffn), full(wrt), full(bias)],
        out_specs=specs,
        scratch_shapes=[pltpu.VMEM((N_EXPERTS, 1), F32)],
        compiler_params=pltpu.CompilerParams(dimension_semantics=("arbitrary", "arbitrary"),
                                             vmem_limit_bytes=_vmem_limit(40)),
        name="rwkv_out",
    )(y, bonus, g, lnw, lnb, wo, bd, h, mods, gffn, wrt, bias)


def _rope_table(n_lat, n_ctx):
    dim = SWA_HEAD_DIM
    nf = dim // 4
    inv = ROPE_THETA ** (-jnp.arange(nf, dtype=F32) / nf)
    row = jnp.repeat(jnp.arange(n_lat // GRID_W, dtype=F32), GRID_W)
    col = jnp.tile(jnp.arange(GRID_W, dtype=F32), n_lat // GRID_W)
    ar = row[:, None] * inv
    ac = col[:, None] * inv
    ang = jnp.concatenate([ar, ar, ac, ac], axis=-1)
    cos = jnp.concatenate([jnp.ones((n_ctx, dim), F32), jnp.cos(ang)], axis=0)
    sin = jnp.concatenate([jnp.zeros((n_ctx, dim), F32), jnp.sin(ang)], axis=0)
    return jnp.tile(cos, (1, 2)), jnp.tile(sin, (1, 2))


def _layout_attn_weights(w_in, w_uq, w_ukv):
    d = w_in.shape[0]
    s0 = MLA_Q_RANK
    s1 = s0 + MLA_KV_RANK
    s2 = s1 + MLA_ROPE
    s3 = s2 + SWA_HEADS * SWA_HEAD_DIM
    s4 = s3 + SWA_KV_HEADS * SWA_HEAD_DIM
    rep = lambda w: jnp.concatenate(
        [jnp.tile(w[:, g * SWA_HEAD_DIM:(g + 1) * SWA_HEAD_DIM], (1, SWA_GROUP)) for g in range(SWA_KV_HEADS)], axis=1)
    win = jnp.concatenate([w_in[:, :s1], w_in[:, s2:s3], rep(w_in[:, s3:s4]), rep(w_in[:, s4:]),
                           w_in[:, s1:s2], jnp.zeros((d, V7X_LANES - MLA_ROPE), w_in.dtype)], axis=1)
    qh = MLA_NOPE + MLA_ROPE
    pad = jnp.zeros((w_uq.shape[0], V7X_MXU_DIM - qh), w_uq.dtype)
    wuq = jnp.concatenate([jnp.concatenate([w_uq[:, h * qh:(h + 1) * qh], pad], axis=1) for h in range(MLA_HEADS)], axis=1)
    kvh = MLA_NOPE + MLA_V
    wuk = jnp.concatenate([w_ukv[:, h * kvh:h * kvh + MLA_NOPE] for h in range(MLA_HEADS)], axis=1)
    wuvt = jnp.concatenate([w_ukv[:, h * kvh + MLA_NOPE:(h + 1) * kvh] for h in range(MLA_HEADS)], axis=1).T
    return win.astype(BF16), wuq.astype(BF16), wuk.astype(BF16), wuvt.astype(BF16)


def _lora_pair(w_down, w_up):
    rank = w_down.shape[2]
    down = jnp.concatenate([w_down[0], w_down[1]], axis=1)
    z = jnp.zeros((rank, w_up.shape[2]), w_up.dtype)
    up = jnp.stack([jnp.concatenate([w_up[0], z], axis=0), jnp.concatenate([z, w_up[1]], axis=0)], axis=0)
    return down.astype(BF16), up.astype(BF16)


def _head_block_diag():
    i = jnp.arange(V7X_MXU_DIM) // RWKV_HEAD
    return (i[:, None] == i[None, :]).astype(BF16)


def kernel(x, c, ctx, c_ctx, ada_w, ada_b, norm_mix, norm_ffn, norm_final, attn_w_in, attn_q_norm, attn_kv_norm, attn_w_uq, attn_w_ukv, attn_sinks, attn_w_o, rwkv_mu, rwkv_w_r, rwkv_w_k, rwkv_w_v, rwkv_w_o, rwkv_g1, rwkv_g2, rwkv_w0, rwkv_w1, rwkv_w2, rwkv_a0, rwkv_a1, rwkv_a2, rwkv_k_k, rwkv_k_a, rwkv_r_k, rwkv_ln_w, rwkv_ln_b, moe_router, moe_bias, moe_w_gate, moe_w_up, moe_w_down, moe_ws_gate, moe_ws_up, moe_ws_down):
    bsz, s, d = x.shape
    lc = ctx.shape[1]
    l = lc + s
    depth = ada_w.shape[0]
    nct = lc // TOKEN_TILE
    assert lc % TOKEN_TILE == 0 and s % TOKEN_TILE == 0 and s >= SWA_BAND
    assert lc % (WKV_CHUNK * WKV_CHUNKS_PER_STEP) == 0
    assert d % V7X_MXU_DIM == 0 and WKV_CHUNK * 2 == V7X_LANES
    ngrp = SAMPLE_GROUPS
    bg = bsz // ngrp
    assert bsz % ngrp == 0 and (bg * l) % (8 * V7X_SC_WORKERS) == 0

    streams = [(ctx, x, g * bg, 0) for g in range(ngrp)]
    out = None
    cos, sin = _rope_table(s, lc)
    bd = _head_block_diag()
    rows = -(-(bsz + 1) // 8) * 8
    cc = jnp.concatenate([c, c_ctx[None, :], jnp.zeros((rows - bsz - 1, d), F32)], axis=0)
    row2 = lambda a: a.reshape(1, -1)

    for li in range(depth):
        with_ctx = li < depth - 1
        i = li // 2
        ada = _ada_mods(cc, ada_w, ada_b, li)
        mods_all = jnp.stack([jnp.broadcast_to(ada[bsz].reshape(1, N_MODS, d), (bsz, N_MODS, d)),
                              ada[:bsz].reshape(bsz, N_MODS, d)], axis=1)
        wrt = jnp.concatenate([moe_router[li].T, jnp.zeros((GATE_W - N_EXPERTS, d), F32)], axis=0)
        bias = moe_bias[li].reshape(N_GROUPS, GROUP_SIZE, 1)
        if li % 2 == 0:
            win, wuq, wuk, wuvt = _layout_attn_weights(attn_w_in[i], attn_w_uq[i], attn_w_ukv[i])
            wo = attn_w_o[i].astype(BF16)
        else:
            w1, w2 = _lora_pair(rwkv_w1[i], rwkv_w2[i])
            a1, a2 = _lora_pair(rwkv_a1[i], rwkv_a2[i])
            wr, wk, wv, wo = [w[i].astype(BF16) for w in (rwkv_w_r, rwkv_w_k, rwkv_w_v, rwkv_w_o)]
            g1, g2 = rwkv_g1[i].astype(BF16), rwkv_g2[i].astype(BF16)
        for g in range(ngrp):
            mods = mods_all[g * bg:(g + 1) * bg]
            if li % 2 == 0:
                q, k, vt, qs, ks, vs = _attn_proj(streams[g], bg, l, mods, row2(norm_mix[li]), win,
                                                  row2(attn_q_norm[i]), row2(attn_kv_norm[i]), wuq, wuk, wuvt,
                                                  cos, sin, nct)
                a = _mla_attention(q, k, vt, lc, 0 if with_ctx else lc // MLA_Q_TILE)
                bm = _swa_attention(attn_sinks[i], qs, ks, vs, lc, 0 if with_ctx else lc // SWA_Q_TILE)
                tail = _attn_out(a, bm, streams[g], mods, wo, row2(norm_ffn[li]), wrt, bias, nct)
            else:
                h = streams[g][0]
                assert streams[g][0] is streams[g][1]
                r, v, kk, gt, km, bv, lw, bonus = _rwkv_proj(
                    h, mods, row2(norm_mix[li]), rwkv_mu[i], wr, wk, wv, g1, g2, w1, w2, a1, a2,
                    rwkv_w0[i], rwkv_a0[i], row2(rwkv_k_k[i]), row2(rwkv_k_a[i]), row2(rwkv_r_k[i]), bd, nct)
                y = _wkv(r, v, kk, km, bv, lw, lc)
                tail = _rwkv_out(y, bonus, gt, row2(rwkv_ln_w[i]), row2(rwkv_ln_b[i]), wo,
                                 bd, h, mods, row2(norm_ffn[li]), wrt, bias, nct)
            h, n2p, eid, rank, wcols, counts = tail
            moe_w = (moe_w_gate, moe_w_up, moe_w_down, moe_ws_gate, moe_ws_up, moe_ws_down)
            if li < depth - 1:
                h = _moe_sparse(n2p, eid, rank, wcols, counts, h, mods, *moe_w, row2(norm_final), nct, li)
                streams[g] = (h, h, 0, nct)
            else:
                out = _moe_sparse(n2p, eid, rank, wcols, counts, h, mods, *moe_w, row2(norm_final), nct, li,
                                  out_buf=out, out_b0=g * bg, out_batch=bsz, last=True)
    return out
```

```python
import functools

import jax
import jax.numpy as jnp
from jax import lax
from jax.experimental import pallas as pl
from jax.experimental.pallas import tpu as pltpu
from jax.experimental.pallas import tpu_sc as plsc

F32 = jnp.float32
BF16 = jnp.bfloat16
HIGHEST = lax.Precision.HIGHEST

GRID_W = 64
NORM_EPS = 1e-6
ROPE_THETA = 10000.0
NEG_INF = -1e30
N_MODS = 6

MLA_HEADS = 4
MLA_Q_RANK = 384
MLA_KV_RANK = 256
MLA_NOPE = 128
MLA_ROPE = 64
MLA_V = 128

SWA_HEADS = 8
SWA_KV_HEADS = 2
SWA_GROUP = SWA_HEADS // SWA_KV_HEADS
SWA_HEAD_DIM = 64
WINDOW = 128

RWKV_HEAD = 64
DECAY_LORA = 64
ICLR_LORA = 64
GATE_LORA = 128
GN_EPS = 64e-5

N_EXPERTS = 64
TOP_K = 6
N_GROUPS = 8
TOPK_GROUPS = 4
GROUP_SIZE = N_EXPERTS // N_GROUPS
ROUTED_SCALE = 2.5
GATE_W = 128

V7X_LANES = 128
V7X_MXU_DIM = 256
V7X_VMEM_BYTES = 64 * 1024 * 1024
V7X_SC_CORES = 2
V7X_SC_SUBCORES = 16
V7X_SC_WORKERS = V7X_SC_CORES * V7X_SC_SUBCORES

TOKEN_TILE = 256
MLA_Q_TILE = 256
MLA_HEADS_PER_STEP = 2
LOG2E = 1.4426950408889634
SWA_Q_TILE = 128
SWA_BAND = SWA_Q_TILE + 2 * WINDOW
WKV_CHUNK = 64
WKV_PAIR = 2 * RWKV_HEAD
WKV_CHUNKS_PER_STEP = 2
MOE_ROW_TILE = 512
SAMPLE_GROUPS = 2
SC_MAX_CHUNK = 64


def _vmem_limit(mib):
    return min(mib * 1024 * 1024, V7X_VMEM_BYTES - 4 * 1024 * 1024)


def _dot(a, b):
    return jnp.dot(a.astype(BF16), b.astype(BF16), preferred_element_type=F32)


def _dot_nt(a, b):
    return lax.dot_general(a.astype(BF16), b.astype(BF16), (((1,), (1,)), ((), ())),
                           preferred_element_type=F32)


def _dot_tn(a, b):
    return lax.dot_general(a.astype(BF16), b.astype(BF16), (((0,), (0,)), ((), ())),
                           preferred_element_type=F32)


def _sigmoid(x):
    return 1.0 / (1.0 + jnp.exp(-x))


def _silu(x):
    return x * _sigmoid(x)


def _rms(x, g):
    return x * lax.rsqrt(jnp.mean(x * x, axis=-1, keepdims=True) + NORM_EPS) * g


def _norm_mod(x, g, shift, scale):
    return _rms(x, g) * (1.0 + scale) + shift


def _split_dot(x, w):
    hi = x.astype(BF16)
    lo = (x - hi.astype(F32)).astype(BF16)
    return (jnp.dot(hi, w, preferred_element_type=F32) + jnp.dot(lo, w, preferred_element_type=F32))


def _head_sum(x, bd):
    w = bd.shape[0]
    parts = [_split_dot(x[:, c * w:(c + 1) * w], bd) for c in range(x.shape[1] // w)]
    return jnp.concatenate(parts, axis=1)


def _ada_kernel(c_ref, w_ref, b_ref, o_ref):
    s = _silu(c_ref[...])
    o_ref[...] = jnp.dot(s, w_ref[0], precision=HIGHEST, preferred_element_type=F32) + b_ref[0]


def _ada_mods(cc, w, b, layer):
    rows, d = cc.shape
    depth, _, n = w.shape
    return pl.pallas_call(
        _ada_kernel,
        out_shape=jax.ShapeDtypeStruct((rows, n), F32),
        grid=(n // d,),
        in_specs=[pl.BlockSpec((rows, d), lambda i: (0, 0)),
                  pl.BlockSpec((1, d, d), lambda i: (layer, 0, i)),
                  pl.BlockSpec((1, 1, d), lambda i: (layer, 0, i))],
        out_specs=pl.BlockSpec((rows, d), lambda i: (0, i)),
        compiler_params=pltpu.CompilerParams(dimension_semantics=("parallel",),
                                             vmem_limit_bytes=_vmem_limit(32)),
        name="ada_mods",
    )(cc, w, b.reshape(depth, 1, n))


def _rope128(x, cos, sin, first_half):
    rot = jnp.where(first_half, -pltpu.roll(x, V7X_LANES - 16, axis=1), pltpu.roll(x, 16, axis=1))
    return x * cos + rot * sin


_C_CQ = 0
_C_CKV = _C_CQ + MLA_Q_RANK
_C_QS = _C_CKV + MLA_KV_RANK
_C_KS = _C_QS + SWA_HEADS * SWA_HEAD_DIM
_C_VS = _C_KS + SWA_KV_HEADS * V7X_MXU_DIM
_C_KR = _C_VS + SWA_KV_HEADS * V7X_MXU_DIM
_C_END = _C_KR + V7X_LANES
_SWA_W = SWA_KV_HEADS * V7X_MXU_DIM
_MLA_QK_W = MLA_HEADS * V7X_MXU_DIM


def _stream_specs(stream, nct, tl):
    ctx_arr, lat_arr, b0, lat_off = stream
    d = ctx_arr.shape[2]
    return [pl.BlockSpec((1, tl, d), lambda i, j: (i + b0, jnp.minimum(j, nct - 1), 0)),
            pl.BlockSpec((1, tl, d), lambda i, j: (i + b0, jnp.maximum(j - nct, 0) + lat_off, 0))]


def _stream_tile(c_ref, x_ref, nct):
    rows = c_ref.shape[1]
    take_ctx = lax.broadcasted_iota(jnp.int32, (rows, 1), 0) < jnp.where(pl.program_id(1) < nct, rows, 0)
    return jnp.where(take_ctx, c_ref[0], x_ref[0])


def _attn_proj_kernel(c_ref, x_ref, mods_ref, g_ref, win_ref, qn_ref, kvn_ref, wuq_ref, wuk_ref, wuvt_ref, cos_ref,
                      sin_ref, q_ref, k_ref, vt_ref, qs_ref, ks_ref, vs_ref, *, nct):
    m = mods_ref[0, 0]
    n = _norm_mod(_stream_tile(c_ref, x_ref, nct), g_ref[...], m[0:1], m[1:2])
    u = _dot(n, win_ref[...])
    cos = cos_ref[...]
    sin = sin_ref[...]
    lane = lax.broadcasted_iota(jnp.int32, (1, V7X_LANES), 1)
    first_half = (lane % 32) < 16

    def rope(x):
        return _rope128(x, cos, sin, first_half)

    scale_a = (MLA_NOPE + MLA_ROPE) ** -0.5 * LOG2E
    scale_b = SWA_HEAD_DIM ** -0.5
    q = _dot(_rms(u[:, _C_CQ:_C_CKV], qn_ref[...]), wuq_ref[...])
    ckv = _rms(u[:, _C_CKV:_C_QS], kvn_ref[...])
    kn = _dot(ckv, wuk_ref[...])
    vt_ref[0] = _dot_nt(wuvt_ref[...], ckv).astype(BF16)
    kr = rope(u[:, _C_KR:_C_END]).astype(BF16)
    for h in range(MLA_HEADS):
        o = h * V7X_MXU_DIM
        q_ref[0, :, o:o + V7X_LANES] = (q[:, o:o + V7X_LANES] * scale_a).astype(BF16)
        q_ref[0, :, o + V7X_LANES:o + V7X_MXU_DIM] = (rope(q[:, o + V7X_LANES:o + V7X_MXU_DIM]) * scale_a).astype(BF16)
        k_ref[0, :, o:o + V7X_LANES] = kn[:, h * MLA_NOPE:(h + 1) * MLA_NOPE].astype(BF16)
        k_ref[0, :, o + V7X_LANES:o + V7X_MXU_DIM] = kr
    for c in range((_C_KS - _C_QS) // V7X_LANES):
        o = c * V7X_LANES
        qs_ref[0, :, o:o + V7X_LANES] = (rope(u[:, _C_QS + o:_C_QS + o + V7X_LANES]) * scale_b).astype(BF16)
    for c in range(_SWA_W // V7X_LANES):
        o = c * V7X_LANES
        ks_ref[0, :, o:o + V7X_LANES] = rope(u[:, _C_KS + o:_C_KS + o + V7X_LANES]).astype(BF16)
    vs_ref[0] = u[:, _C_VS:_C_KR].astype(BF16)


def _attn_proj(stream, b, l, mods, g, win, qn, kvn, wuq, wuk, wuvt, cos, sin, nct):
    d = stream[0].shape[2]
    tl = TOKEN_TILE
    tok = lambda w: pl.BlockSpec((1, tl, w), lambda i, j: (i, j, 0))
    full = lambda a: pl.BlockSpec(a.shape, lambda i, j: (0,) * a.ndim)
    sds = jax.ShapeDtypeStruct
    dv = MLA_HEADS * MLA_V
    return pl.pallas_call(
        functools.partial(_attn_proj_kernel, nct=nct),
        out_shape=[sds((b, l, _MLA_QK_W), BF16), sds((b, l, _MLA_QK_W), BF16), sds((b, dv, l), BF16),
                   sds((b, l, SWA_HEADS * SWA_HEAD_DIM), BF16), sds((b, l, _SWA_W), BF16), sds((b, l, _SWA_W), BF16)],
        grid=(b, l // tl),
        in_specs=_stream_specs(stream, nct, tl) + [
                  pl.BlockSpec((1, 1, N_MODS, d), lambda i, j: (i, jnp.where(j < nct, 0, 1), 0, 0)),
                  full(g), full(win), full(qn), full(kvn), full(wuq), full(wuk), full(wuvt),
                  pl.BlockSpec((tl, V7X_LANES), lambda i, j: (j, 0)),
                  pl.BlockSpec((tl, V7X_LANES), lambda i, j: (j, 0))],
        out_specs=[tok(_MLA_QK_W), tok(_MLA_QK_W), pl.BlockSpec((1, dv, tl), lambda i, j: (i, 0, j)),
                   tok(SWA_HEADS * SWA_HEAD_DIM), tok(_SWA_W), tok(_SWA_W)],
        compiler_params=pltpu.CompilerParams(dimension_semantics=("parallel", "parallel"),
                                             vmem_limit_bytes=_vmem_limit(48)),
        name="attn_proj",
    )(stream[0], stream[1], mods, g, win, qn, kvn, wuq, wuk, wuvt, cos, sin)


def _mla_kernel(q_ref, k_ref, vt_ref, o_ref, *, nct_q, lc):
    hw = V7X_MXU_DIM

    def attend(nk):
        st = [_dot_nt(k_ref[0, 0:nk, hh * hw:(hh + 1) * hw], q_ref[0, :, hh * hw:(hh + 1) * hw])
              for hh in range(MLA_HEADS_PER_STEP)]
        for hh, s in enumerate(st):
            p = jnp.exp2(s - jnp.max(s, axis=0, keepdims=True))
            den = jnp.sum(p, axis=0, keepdims=True)
            ot = _dot(vt_ref[0, hh * MLA_V:(hh + 1) * MLA_V, 0:nk], p) / den
            o_ref[0, :, hh * MLA_V:(hh + 1) * MLA_V] = ot.T.astype(o_ref.dtype)

    @pl.when(pl.program_id(2) < nct_q)
    def _():
        attend(lc)

    @pl.when(pl.program_id(2) >= nct_q)
    def _():
        attend(k_ref.shape[1])


def _mla_attention(q, k, vt, lc, q_tile0):
    b, l, _ = q.shape
    tq = MLA_Q_TILE
    hps = MLA_HEADS_PER_STEP
    return pl.pallas_call(
        functools.partial(_mla_kernel, nct_q=lc // tq - q_tile0, lc=lc),
        out_shape=jax.ShapeDtypeStruct((b, l, MLA_HEADS * MLA_V), BF16),
        grid=(b, MLA_HEADS // hps, l // tq - q_tile0),
        in_specs=[pl.BlockSpec((1, tq, hps * V7X_MXU_DIM), lambda i, h, j: (i, j + q_tile0, h)),
                  pl.BlockSpec((1, l, hps * V7X_MXU_DIM), lambda i, h, j: (i, 0, h)),
                  pl.BlockSpec((1, hps * MLA_V, l), lambda i, h, j: (i, h, 0))],
        out_specs=pl.BlockSpec((1, tq, hps * MLA_V), lambda i, h, j: (i, j + q_tile0, h)),
        compiler_params=pltpu.CompilerParams(dimension_semantics=("parallel", "parallel", "parallel"),
                                             vmem_limit_bytes=_vmem_limit(48)),
        name="mla_attention",
    )(q, k, vt)


def _swa_kernel(sink_ref, q_ref, k_ref, v_ref, o_ref, *, lc, q_tile0):
    tq = SWA_Q_TILE
    l = k_ref.shape[1]
    r0 = (pl.program_id(1) + q_tile0) * tq
    start = pl.multiple_of(jnp.clip(r0 - WINDOW, lc, l - SWA_BAND), tq)
    rows = SWA_GROUP * tq
    row = lax.broadcasted_iota(jnp.int32, (rows, 1), 0)
    qpos = jnp.where(r0 >= lc, r0, -l) + row % tq
    kpos = start + lax.broadcasted_iota(jnp.int32, (1, SWA_BAND), 1)
    valid = jnp.abs(qpos - kpos) <= WINDOW
    lane = lax.broadcasted_iota(jnp.int32, (1, V7X_MXU_DIM), 1)
    for g in range(SWA_KV_HEADS):
        sl = slice(g * V7X_MXU_DIM, (g + 1) * V7X_MXU_DIM)
        qg = q_ref[0, :, sl]
        zero = jnp.zeros_like(qg)
        head = [(lane // SWA_HEAD_DIM) == hh for hh in range(SWA_GROUP)]
        qstack = jnp.concatenate([jnp.where(head[hh], qg, zero) for hh in range(SWA_GROUP)], axis=0)
        sc = _dot_nt(qstack, k_ref[0, 0:lc, sl])
        sb = jnp.where(valid, _dot_nt(qstack, k_ref[0, pl.ds(start, SWA_BAND), sl]), NEG_INF)
        sk = jnp.zeros((rows, 1), F32)
        for hh in range(SWA_GROUP):
            sk = jnp.where(row // tq == hh, sink_ref[g * SWA_GROUP + hh], sk)
        mx = jnp.maximum(jnp.maximum(jnp.max(sc, axis=-1, keepdims=True), jnp.max(sb, axis=-1, keepdims=True)), sk)
        pc = jnp.exp(sc - mx)
        pb = jnp.exp(sb - mx)
        den = jnp.sum(pc, axis=-1, keepdims=True) + jnp.sum(pb, axis=-1, keepdims=True) + jnp.exp(sk - mx)
        ostack = (_dot(pc, v_ref[0, 0:lc, sl]) + _dot(pb, v_ref[0, pl.ds(start, SWA_BAND), sl])) / den
        o = jnp.zeros((tq, V7X_MXU_DIM), F32)
        for hh in range(SWA_GROUP):
            o = o + jnp.where(head[hh], ostack[hh * tq:(hh + 1) * tq], 0.0)
        o_ref[0, :, sl] = o.astype(o_ref.dtype)


def _swa_attention(sinks, q, k, v, lc, q_tile0):
    b, l, _ = q.shape
    tq = SWA_Q_TILE
    return pl.pallas_call(
        functools.partial(_swa_kernel, lc=lc, q_tile0=q_tile0),
        out_shape=jax.ShapeDtypeStruct((b, l, SWA_HEADS * SWA_HEAD_DIM), BF16),
        grid=(b, l // tq - q_tile0),
        in_specs=[pl.BlockSpec(memory_space=pltpu.SMEM),
                  pl.BlockSpec((1, tq, SWA_HEADS * SWA_HEAD_DIM), lambda i, j: (i, j + q_tile0, 0)),
                  pl.BlockSpec((1, l, _SWA_W), lambda i, j: (i, 0, 0)),
                  pl.BlockSpec((1, l, _SWA_W), lambda i, j: (i, 0, 0))],
        out_specs=pl.BlockSpec((1, tq, SWA_HEADS * SWA_HEAD_DIM), lambda i, j: (i, j + q_tile0, 0)),
        compiler_params=pltpu.CompilerParams(dimension_semantics=("parallel", "parallel"),
                                             vmem_limit_bytes=_vmem_limit(48)),
        name="swa_attention",
    )(sinks, q, k, v)


def _pack_bf16_pair(x):
    w = x.shape[1] // 2
    lo = pltpu.bitcast(x[:, :w].astype(BF16).astype(F32), jnp.int32)
    hi = pltpu.bitcast(x[:, w:].astype(BF16).astype(F32), jnp.int32)
    return lax.shift_right_logical(lo, jnp.int32(16)) | (hi & jnp.int32(-65536))


def _unpack_bf16_pair(p):
    return pltpu.bitcast(p << 16, F32), pltpu.bitcast(p & jnp.int32(-65536), F32)


def _route(n2, wrt, bias, run_ref):
    n_hi = n2.astype(BF16)
    n_lo = (n2 - n_hi.astype(F32)).astype(BF16)
    w_hi = wrt.astype(BF16)
    w_lo = (wrt - w_hi.astype(F32)).astype(BF16)
    logits = _dot_nt(w_hi, n_hi) + (_dot_nt(w_hi, n_lo) + _dot_nt(w_lo, n_hi))
    rows = logits.shape[1]
    shape3 = (N_GROUPS, GROUP_SIZE, rows)
    scores3 = _sigmoid(logits[0:N_EXPERTS]).reshape(shape3)
    choice = scores3 + bias
    ji = lax.broadcasted_iota(jnp.int32, shape3, 1).astype(F32)
    m1 = jnp.max(choice, axis=1, keepdims=True)
    first = jnp.min(jnp.where(choice == m1, ji, float(GROUP_SIZE)), axis=1, keepdims=True)
    m2 = jnp.max(jnp.where(ji == first, -jnp.inf, choice), axis=1, keepdims=True)
    gs = m1 + m2
    gidx = lax.broadcasted_iota(jnp.int32, gs.shape, 0).astype(F32)
    gsel = jnp.zeros_like(gs)
    for _ in range(TOPK_GROUPS):
        mx = jnp.max(gs, axis=0, keepdims=True)
        pick = gidx == jnp.min(jnp.where(gs == mx, gidx, float(N_GROUPS)), axis=0, keepdims=True)
        gsel = jnp.where(pick, 1.0, gsel)
        gs = jnp.where(pick, -jnp.inf, gs)
    cand = jnp.where(gsel > 0.0, choice, -jnp.inf).reshape(N_EXPERTS, rows)
    scores = scores3.reshape(N_EXPERTS, rows)
    ei = lax.broadcasted_iota(jnp.int32, (N_EXPERTS, rows), 0).astype(F32)
    picks = []
    for _ in range(TOP_K):
        mx = jnp.max(cand, axis=0, keepdims=True)
        pick = ei == jnp.min(jnp.where(cand == mx, ei, float(N_EXPERTS)), axis=0, keepdims=True)
        picks.append(pick)
        cand = jnp.where(pick, -jnp.inf, cand)
    esel = jnp.zeros((N_EXPERTS, rows), F32)
    for pick in picks:
        esel = jnp.where(pick, 1.0, esel)
    before = jnp.where(lax.broadcasted_iota(jnp.int32, (rows, rows), 0) < lax.broadcasted_iota(jnp.int32, (rows, rows), 1),
                       1.0, 0.0).astype(BF16)
    slot = jnp.dot(esel.astype(BF16), before, preferred_element_type=F32) + run_ref[...]
    run_ref[...] += jnp.sum(esel, axis=1, keepdims=True)
    sc = [jnp.sum(jnp.where(pick, scores, 0.0), axis=0, keepdims=True) for pick in picks]
    tot = sc[0]
    for x in sc[1:]:
        tot = tot + x
    k8 = lax.broadcasted_iota(jnp.int32, (8, rows), 0)
    kw = lax.broadcasted_iota(jnp.int32, (GATE_W, rows), 0)
    eid = jnp.zeros((8, rows), jnp.int32)
    rank = jnp.zeros((8, rows), jnp.int32)
    wk = jnp.zeros((GATE_W, rows), F32)
    for k, pick in enumerate(picks):
        e_k = jnp.sum(jnp.where(pick, ei, 0.0), axis=0, keepdims=True).astype(jnp.int32)
        r_k = jnp.sum(jnp.where(pick, slot, 0.0), axis=0, keepdims=True).astype(jnp.int32)
        eid = jnp.where(k8 == k, e_k, eid)
        rank = jnp.where(k8 == k, r_k, rank)
        wk = jnp.where(kw == k, sc[k] * (ROUTED_SCALE / tot), wk)
    return eid, rank, wk.T


def _mixer_tail(o, h, m, gffn_ref, wrt_ref, bias_ref, hn_ref, n2_ref, eid_ref, rank_ref, w_ref, cnt_ref, run_ref):
    @pl.when((pl.program_id(0) == 0) & (pl.program_id(1) == 0))
    def _():
        run_ref[...] = jnp.zeros_like(run_ref)

    hn = h + m[2:3] * o
    hn_ref[0] = hn
    n2 = _norm_mod(hn, gffn_ref[...], m[3:4], m[4:5])
    n2_ref[0] = _pack_bf16_pair(n2)
    eid, rank, wcols = _route(n2, wrt_ref[...], bias_ref[...], run_ref)
    eid_ref[0] = eid
    rank_ref[0] = rank
    w_ref[0] = wcols
    cnt_ref[...] = run_ref[...]


def _attn_out_kernel(a_ref, b_ref, c_ref, x_ref, mods_ref, wo_ref, gffn_ref, wrt_ref, bias_ref,
                     hn_ref, n2_ref, eid_ref, rank_ref, w_ref, cnt_ref, run_ref, *, nct):
    wa = MLA_HEADS * MLA_V
    o = _dot(a_ref[0], wo_ref[0:wa, :]) + _dot(b_ref[0], wo_ref[wa:, :])
    _mixer_tail(o, _stream_tile(c_ref, x_ref, nct), mods_ref[0, 0], gffn_ref, wrt_ref, bias_ref, hn_ref, n2_ref,
                eid_ref, rank_ref, w_ref, cnt_ref, run_ref)


def _tail_outs(b, l, d):
    tl = TOKEN_TILE
    nt = l // tl
    sds = jax.ShapeDtypeStruct
    tok = lambda w: pl.BlockSpec((1, tl, w), lambda i, j: (i, j, 0))
    blk = pl.BlockSpec((1, 8, tl), lambda i, j: (i * nt + j, 0, 0))
    shapes = [sds((b, l, d), F32), sds((b, l, d // 2), jnp.int32), sds((b * nt, 8, tl), jnp.int32),
              sds((b * nt, 8, tl), jnp.int32), sds((b, l, GATE_W), F32), sds((N_EXPERTS, 1), F32)]
    specs = [tok(d), tok(d // 2), blk, blk, tok(GATE_W), pl.BlockSpec((N_EXPERTS, 1), lambda i, j: (0, 0))]
    return shapes, specs


def _attn_out(a, bm, stream, mods, wo, gffn, wrt, bias, nct):
    b, l, _ = a.shape
    d = stream[0].shape[2]
    tl = TOKEN_TILE
    tok = lambda w: pl.BlockSpec((1, tl, w), lambda i, j: (i, j, 0))
    full = lambda x: pl.BlockSpec(x.shape, lambda i, j: (0,) * x.ndim)
    shapes, specs = _tail_outs(b, l, d)
    return pl.pallas_call(
        functools.partial(_attn_out_kernel, nct=nct),
        out_shape=shapes,
        grid=(b, l // tl),
        in_specs=[tok(a.shape[2]), tok(bm.shape[2])] + _stream_specs(stream, nct, tl) + [
                  pl.BlockSpec((1, 1, N_MODS, d), lambda i, j: (i, jnp.where(j < nct, 0, 1), 0, 0)),
                  full(wo), full(gffn), full(wrt), full(bias)],
        out_specs=specs,
        scratch_shapes=[pltpu.VMEM((N_EXPERTS, 1), F32)],
        compiler_params=pltpu.CompilerParams(dimension_semantics=("arbitrary", "arbitrary"),
                                             vmem_limit_bytes=_vmem_limit(40)),
        name="attn_out",
    )(a, bm, stream[0], stream[1], mods, wo, gffn, wrt, bias)


def _moe_dest_kernel(off_ref, eid_ref, rank_ref, dest_ref):
    eid = eid_ref[...]
    dest = rank_ref[...]
    for e in range(N_EXPERTS):
        dest = dest + jnp.where(eid == e, off_ref[e], 0)
    dest_ref[...] = dest


def _moe_dest(off, eid, rank):
    return pl.pallas_call(
        _moe_dest_kernel,
        out_shape=jax.ShapeDtypeStruct(eid.shape, jnp.int32),
        in_specs=[pl.BlockSpec(memory_space=pltpu.SMEM),
                  pl.BlockSpec(eid.shape, lambda: (0, 0, 0)), pl.BlockSpec(eid.shape, lambda: (0, 0, 0))],
        out_specs=pl.BlockSpec(eid.shape, lambda: (0, 0, 0)),
        name="moe_dest",
    )(off, eid, rank)


def _sc_mesh():
    return plsc.VectorSubcoreMesh(core_axis_name="c", subcore_axis_name="s",
                                  num_cores=V7X_SC_CORES, num_subcores=V7X_SC_SUBCORES)


def _sc_chunk(rows_per_worker):
    return max(c for c in range(8, SC_MAX_CHUNK + 1, 8) if rows_per_worker % c == 0)


def _sc_dispatch(xp, dest, p_rows):
    t, w = xp.shape
    tpw = t // V7X_SC_WORKERS
    ch = _sc_chunk(tpw)

    @functools.partial(
        pl.kernel, mesh=_sc_mesh(), out_type=jax.ShapeDtypeStruct((p_rows, w), xp.dtype),
        scratch_types=[pltpu.VMEM((ch, w), xp.dtype)] + [pltpu.VMEM((ch,), jnp.int32)] * TOP_K
        + [pltpu.SemaphoreType.DMA, pltpu.SemaphoreType.DMA],
        name="moe_dispatch")
    def run(x_hbm, dest_hbm, out_hbm, rows_v, *rest):
        idx, (sem_i, sem_o) = rest[:TOP_K], rest[TOP_K:]
        base = (lax.axis_index("s") * V7X_SC_CORES + lax.axis_index("c")) * tpw

        @pl.loop(0, tpw // ch)
        def _(i):
            t0 = base + i * ch
            loads = [pltpu.async_copy(dest_hbm.at[k, pl.ds(t0, ch)], idx[k], sem_i) for k in range(TOP_K)]
            pltpu.sync_copy(x_hbm.at[pl.ds(t0, ch)], rows_v)
            for c in loads:
                c.wait()
            puts = [pltpu.async_copy(rows_v, out_hbm.at[idx[k]], sem_o) for k in range(TOP_K)]
            for c in puts:
                c.wait()

    return run(xp, dest)


def _sc_gather(ys, dest, t):
    w = ys.shape[1]
    tpw = t // V7X_SC_WORKERS
    ch = _sc_chunk(tpw)

    @functools.partial(
        pl.kernel, mesh=_sc_mesh(), out_type=jax.ShapeDtypeStruct((TOP_K, t, w), ys.dtype),
        scratch_types=[pltpu.VMEM((ch, w), ys.dtype)] * 2 + [pltpu.VMEM((ch,), jnp.int32)] * TOP_K
        + [pltpu.SemaphoreType.DMA] * 5,
        name="moe_gather")
    def run(y_hbm, dest_hbm, out_hbm, rows_a, rows_b, *rest):
        idx, (sem_i, sem_ga, sem_gb, sem_wa, sem_wb) = rest[:TOP_K], rest[TOP_K:]
        rows, sem_g, sem_w = (rows_a, rows_b), (sem_ga, sem_gb), (sem_wa, sem_wb)
        base = (lax.axis_index("s") * V7X_SC_CORES + lax.axis_index("c")) * tpw

        @pl.loop(0, tpw // ch)
        def _(i):
            t0 = base + i * ch
            loads = [pltpu.async_copy(dest_hbm.at[k, pl.ds(t0, ch)], idx[k], sem_i) for k in range(TOP_K)]
            for c in loads:
                c.wait()
            gets, puts = [None] * TOP_K, [None] * TOP_K
            gets[0] = pltpu.async_copy(y_hbm.at[idx[0]], rows[0], sem_g[0])
            for k in range(TOP_K):
                if k + 1 < TOP_K:
                    if k >= 1:
                        puts[k - 1].wait()
                    gets[k + 1] = pltpu.async_copy(y_hbm.at[idx[k + 1]], rows[(k + 1) % 2], sem_g[(k + 1) % 2])
                gets[k].wait()
                puts[k] = pltpu.async_copy(rows[k % 2], out_hbm.at[k, pl.ds(t0, ch)], sem_w[k % 2])
            puts[TOP_K - 2].wait()
            puts[TOP_K - 1].wait()

    return run(ys, dest)


def _moe_expert_kernel(te_ref, nv_ref, x_ref, wg_ref, wu_ref, wd_ref, y_ref, wgb_ref, wub_ref, wdb_ref):
    i = pl.program_id(0)

    @pl.when((i == 0) | (te_ref[i] != te_ref[jnp.maximum(i - 1, 0)]))
    def _():
        wgb_ref[...] = wg_ref[0, 0].astype(BF16)
        wub_ref[...] = wu_ref[0, 0].astype(BF16)
        wdb_ref[...] = wd_ref[0, 0].astype(BF16)

    @pl.when(i < nv_ref[0])
    def _():
        lo, hi = _unpack_bf16_pair(x_ref[...])
        half = lo.shape[1]
        hg = _dot(lo, wgb_ref[0:half, :]) + _dot(hi, wgb_ref[half:, :])
        hu = _dot(lo, wub_ref[0:half, :]) + _dot(hi, wub_ref[half:, :])
        y_ref[...] = _pack_bf16_pair(_dot(_silu(hg) * hu, wdb_ref[...]))


def _moe_experts(tile_expert, n_valid, xs, wg, wu, wd, layer):
    p_rows, w = xs.shape
    tm = MOE_ROW_TILE
    _, _, d, f = wg.shape
    wspec = lambda shp: pl.BlockSpec((1, 1) + shp, lambda i, te, nv: (layer, te[i], 0, 0))
    return pl.pallas_call(
        _moe_expert_kernel,
        out_shape=jax.ShapeDtypeStruct((p_rows, w), xs.dtype),
        grid_spec=pltpu.PrefetchScalarGridSpec(
            num_scalar_prefetch=2, grid=(p_rows // tm,),
            in_specs=[pl.BlockSpec((tm, w), lambda i, te, nv: (jnp.minimum(i, nv[0] - 1), 0)),
                      wspec((d, f)), wspec((d, f)), wspec((f, d))],
            out_specs=pl.BlockSpec((tm, w), lambda i, te, nv: (jnp.minimum(i, nv[0] - 1), 0)),
            scratch_shapes=[pltpu.VMEM((d, f), BF16), pltpu.VMEM((d, f), BF16), pltpu.VMEM((f, d), BF16)]),
        compiler_params=pltpu.CompilerParams(dimension_semantics=("arbitrary",),
                                             vmem_limit_bytes=_vmem_limit(32)),
        name="moe_experts",
    )(tile_expert, n_valid, xs, wg, wu, wd)


def _moe_combine_kernel(yg_ref, w_ref, xp_ref, sg_ref, su_ref, sd_ref, h_ref, mods_ref, gfin_ref, *rest, final_norm):
    o_ref = rest[-1]
    xlo, xhi = _unpack_bf16_pair(xp_ref[0])
    half = xlo.shape[1]
    hs = (_silu(_dot(xlo, sg_ref[0, 0:half, :]) + _dot(xhi, sg_ref[0, half:, :]))
          * (_dot(xlo, su_ref[0, 0:half, :]) + _dot(xhi, su_ref[0, half:, :])))
    acc = _dot(hs, sd_ref[0])
    lo = acc[:, :half]
    hi = acc[:, half:]
    w = w_ref[0]
    for k in range(TOP_K):
        ylo, yhi = _unpack_bf16_pair(yg_ref[k, 0])
        wk = w[:, k:k + 1]
        lo = lo + wk * ylo
        hi = hi + wk * yhi
    y = h_ref[0] + mods_ref[0, 0, N_MODS - 1:N_MODS, :] * jnp.concatenate([lo, hi], axis=1)
    if final_norm:
        y = _rms(y, gfin_ref[...])
    o_ref[0] = y


def _moe_combine(yg, wcols, xp, sg, su, sd, h, mods, gfin, nct, layer, out_buf, out_b0, out_batch, latent_only,
                 final_norm):
    b, l, d = h.shape
    tl = TOKEN_TILE
    tile0 = nct if latent_only else 0
    tok = lambda w: pl.BlockSpec((1, tl, w), lambda i, j: (i, j + tile0, 0))
    lay = lambda x: pl.BlockSpec((1,) + x.shape[1:], lambda i, j: (layer,) + (0,) * (x.ndim - 1))
    args = [yg, wcols, xp, sg, su, sd, h, mods, gfin]
    in_specs = [pl.BlockSpec((TOP_K, 1, tl, d // 2), lambda i, j: (0, i, j + tile0, 0)), tok(GATE_W), tok(d // 2),
                lay(sg), lay(su), lay(sd), tok(d),
                pl.BlockSpec((1, 1, N_MODS, d), lambda i, j: (i, jnp.where(j + tile0 < nct, 0, 1), 0, 0)),
                pl.BlockSpec(gfin.shape, lambda i, j: (0, 0))]
    aliases = {}
    if out_buf is not None:
        args.append(out_buf)
        in_specs.append(pl.BlockSpec(memory_space=pl.ANY))
        aliases = {len(args) - 1: 0}
    return pl.pallas_call(
        functools.partial(_moe_combine_kernel, final_norm=final_norm),
        out_shape=jax.ShapeDtypeStruct((out_batch, l - tile0 * tl, d), F32),
        grid=(b, l // tl - tile0),
        in_specs=in_specs,
        out_specs=pl.BlockSpec((1, tl, d), lambda i, j: (i + out_b0, j, 0)),
        input_output_aliases=aliases,
        compiler_params=pltpu.CompilerParams(dimension_semantics=("parallel", "parallel"),
                                             vmem_limit_bytes=_vmem_limit(40)),
        name="moe_combine",
    )(*args)


def _moe_sparse(n2p, eid, rank, wcols, counts, h, mods, wg, wu, wd, sg, su, sd, gfin, nct, layer,
                out_buf=None, out_b0=0, out_batch=None, last=False):
    b, l, d = h.shape
    t = b * l
    tm = MOE_ROW_TILE
    n_tiles = -(-(TOP_K * t + N_EXPERTS * (tm - 1)) // tm)
    tiles_e = (counts.reshape(N_EXPERTS).astype(jnp.int32) + (tm - 1)) // tm
    tile_end = jnp.cumsum(tiles_e)
    off = (tile_end - tiles_e) * tm
    n_valid = tile_end[-1:]
    tile_id = jnp.minimum(jnp.arange(n_tiles, dtype=jnp.int32), n_valid - 1)
    tile_expert = jnp.sum((tile_end[None, :] <= tile_id[:, None]).astype(jnp.int32), axis=1)
    dest = _moe_dest(off, eid, rank).transpose(1, 0, 2).reshape(8, t)
    xs = _sc_dispatch(n2p.reshape(t, d // 2), dest, n_tiles * tm)
    ys = _moe_experts(tile_expert, n_valid, xs, wg, wu, wd, layer)
    yg = _sc_gather(ys, dest, t).reshape(TOP_K, b, l, d // 2)
    return _moe_combine(yg, wcols, n2p, sg, su, sd, h, mods, gfin, nct, layer, out_buf, out_b0,
                        b if out_batch is None else out_batch, last, last)


def _rwkv_proj_kernel(h_ref, hp_ref, hx_ref, mods_ref, g_ref, mu_ref, wr_ref, wk_ref, wv_ref, g1_ref, g2_ref,
                      w1_ref, w2_ref, a1_ref, a2_ref, w0_ref, a0_ref, kk_ref, ka_ref, rk_ref, bd_ref,
                      r_out, v_out, kk_out, g_out, km_out, b_out, lw_out, bonus_out, *, nct):
    j = pl.program_id(1)
    nt = pl.num_programs(1)
    m = mods_ref[0, 0]
    g = g_ref[...]
    n = _norm_mod(h_ref[0], g, m[0:1], m[1:2])
    tl, d = n.shape
    seg_first = (j == 0) | (j == nct)
    seg_last = (j == nct - 1) | (j == nt - 1)
    n_prev = _norm_mod(hp_ref[0], g, m[0:1], m[1:2])[7:8] * jnp.where(seg_first, 0.0, 1.0)
    n_next = _norm_mod(hx_ref[0], g, m[0:1], m[1:2])[0:1] * jnp.where(seg_last, 0.0, 1.0)
    row = lax.broadcasted_iota(jnp.int32, (tl, 1), 0)
    prev = jnp.where(row == 0, n_prev, pltpu.roll(n, 1, axis=0))
    nxt = jnp.where(row == tl - 1, n_next, pltpu.roll(n, tl - 1, axis=0))
    lane = lax.broadcasted_iota(jnp.int32, (1, d), 1)
    xx = jnp.where(lane < d // 2, prev, nxt) - n
    mu = mu_ref[...]
    xr, xw, xk, xv, xa, xg = [n + xx * mu[i:i + 1] for i in range(6)]
    r = _dot(xr, wr_ref[...])
    k = _dot(xk, wk_ref[...])
    v = _dot(xv, wv_ref[...])
    g_out[0] = _dot(_sigmoid(_dot(xg, g1_ref[...])), g2_ref[...]).astype(g_out.dtype)
    tw = jnp.tanh(_dot(xw, w1_ref[...]))
    ta = _dot(xa, a1_ref[...])
    bd = bd_ref[...]
    kk = k * kk_ref[...]
    kk = kk / jnp.maximum(jnp.sqrt(_head_sum(kk * kk, bd)), 1e-12)
    r_out[0] = r.astype(r_out.dtype)
    v_out[0] = v.astype(v_out.dtype)
    kk_out[0] = kk.astype(kk_out.dtype)
    bonus = jnp.zeros_like(v)
    for dr in range(2):
        zw = w0_ref[dr:dr + 1, :] + _dot(tw, w2_ref[dr])
        lw_out[dr, 0] = -jnp.exp(-0.5) * _sigmoid(zw)
        a = _sigmoid(a0_ref[dr:dr + 1, :] + _dot(ta, a2_ref[dr]))
        km = k * (1.0 + (a - 1.0) * ka_ref[...])
        km_out[dr, 0] = km.astype(km_out.dtype)
        b_out[dr, 0] = (kk * a).astype(b_out.dtype)
        bonus = bonus + _head_sum(r * km * rk_ref[...], bd) * v
    bonus_out[0] = bonus


def _rwkv_proj(h, mods, g, mu, wr, wk, wv, g1, g2, w1, w2, a1, a2, w0, a0, kk, ka, rk, bd, nct):
    b, l, d = h.shape
    tl = TOKEN_TILE
    nb8 = l // 8
    tok = pl.BlockSpec((1, tl, d), lambda i, j: (i, j, 0))
    tok2 = pl.BlockSpec((2, 1, tl, d), lambda i, j: (0, i, j, 0))
    full = lambda x: pl.BlockSpec(x.shape, lambda i, j: (0,) * x.ndim)
    sds = jax.ShapeDtypeStruct
    return pl.pallas_call(
        functools.partial(_rwkv_proj_kernel, nct=nct),
        out_shape=[sds((b, l, d), BF16), sds((b, l, d), BF16), sds((b, l, d), BF16), sds((b, l, d), BF16),
                   sds((2, b, l, d), BF16), sds((2, b, l, d), BF16), sds((2, b, l, d), F32), sds((b, l, d), F32)],
        grid=(b, l // tl),
        in_specs=[tok,
                  pl.BlockSpec((1, 8, d), lambda i, j: (i, jnp.maximum(j * (tl // 8) - 1, 0), 0)),
                  pl.BlockSpec((1, 8, d), lambda i, j: (i, jnp.minimum((j + 1) * (tl // 8), nb8 - 1), 0)),
                  pl.BlockSpec((1, 1, N_MODS, d), lambda i, j: (i, jnp.where(j < nct, 0, 1), 0, 0)),
                  full(g), full(mu), full(wr), full(wk), full(wv), full(g1), full(g2), full(w1), full(w2),
                  full(a1), full(a2), full(w0), full(a0), full(kk), full(ka), full(rk), full(bd)],
        out_specs=[tok, tok, tok, tok, tok2, tok2, tok2, tok],
        compiler_params=pltpu.CompilerParams(dimension_semantics=("parallel", "parallel"),
                                             vmem_limit_bytes=_vmem_limit(56)),
        name="rwkv_proj",
    )(h, h, h, mods, g, mu, wr, wk, wv, g1, g2, w1, w2, a1, a2, w0, a0, kk, ka, rk, bd)


def _wkv_kernel(r_ref, v_ref, kk_ref, km_ref, b_ref, lw_ref, y_ref, st_ref):
    c = WKV_CHUNK
    w = WKV_PAIR
    rev = pl.program_id(0)
    sign = 1 - 2 * rev

    @pl.when(pl.program_id(2) == 0)
    def _():
        st_ref[...] = jnp.zeros_like(st_ref)

    ti = lax.broadcasted_iota(jnp.int32, (c, c), 0)
    si = lax.broadcasted_iota(jnp.int32, (c, c), 1)
    tri = jnp.where((si - ti) * sign <= 0, 1.0, 0.0).astype(F32)
    nsub = WKV_CHUNKS_PER_STEP
    subs = [pl.ds(pl.multiple_of(jnp.where(rev == 0, s, nsub - 1 - s) * c, c), c) for s in range(nsub)]
    rt, kt, kh, bh, v32, e_mid = [], [], [], [], [], []
    for rows in subs:
        lw = lw_ref[0, 0, rows, :]
        l_incl = jnp.dot(tri, lw, precision=HIGHEST, preferred_element_type=F32)
        mid = 0.5 * jnp.sum(lw, axis=0, keepdims=True)
        e_neg = jnp.exp(mid - l_incl)
        e_mid.append(jnp.exp(mid))
        rt.append(r_ref[0, rows, :].astype(F32) * jnp.exp(l_incl - mid))
        kt.append(kk_ref[0, rows, :].astype(F32) * jnp.exp(l_incl - lw - mid))
        kh.append(km_ref[0, 0, rows, :].astype(F32) * e_neg)
        bh.append(b_ref[0, 0, rows, :].astype(F32) * e_neg)
        v32.append(v_ref[0, rows, :].astype(F32))

    ri = lax.broadcasted_iota(jnp.int32, (w, w), 0)
    ci = lax.broadcasted_iota(jnp.int32, (w, w), 1)
    same = (ri // c) == (ci // c)
    eye = jnp.where(ri == ci, 1.0, 0.0).astype(F32)
    tl_ = lax.broadcasted_iota(jnp.int32, (c, w), 0)
    jl_ = lax.broadcasted_iota(jnp.int32, (c, w), 1) % c
    strict = (jl_ - tl_) * sign < 0
    incl = (jl_ - tl_) * sign <= 0
    eye2 = jnp.where(jl_ == tl_, 1.0, 0.0).astype(F32)
    lane = lax.broadcasted_iota(jnp.int32, (1, w), 1)
    h0 = lane < RWKV_HEAD

    def rows2(x):
        return jnp.concatenate([jnp.where(h0, x, 0.0), jnp.where(h0, 0.0, x)], axis=0)

    npair = st_ref.shape[0]
    items = [(s, slice(p * w, (p + 1) * w)) for s in range(nsub) for p in range(npair)]
    n = range(len(items))
    em = [e_mid[s][:, sl] for s, sl in items]
    g = [_dot_nt(jnp.concatenate([kt[s][:, sl], rt[s][:, sl]], axis=0),
                 jnp.concatenate([rows2(kh[s][:, sl]), rows2(bh[s][:, sl])], axis=0)) for s, sl in items]
    a_kk = [jnp.where(strict, x[:c, :w], 0.0) for x in g]
    a_rk = [jnp.where(incl, x[c:, :w], 0.0) for x in g]
    a_rb = [jnp.where(incl, x[c:, w:], 0.0) for x in g]
    vi = [v32[s][:, sl] for s, sl in items]
    v_rows = [rows2(x) for x in vi]
    r_pre = [_dot(a_kk[i], v_rows[i]) for i in n]
    m = [jnp.where(strict, -x[:c, w:], 0.0) for x in g]
    tinv = [eye2 + x for x in m]
    m = [_dot(x, rows2(x)) for x in m]
    for _ in range(c.bit_length() - 3):
        both = [_dot(jnp.concatenate([tinv[i], m[i]], axis=0), rows2(m[i])) for i in n]
        tinv = [tinv[i] + both[i][:c] for i in n]
        m = [x[c:] for x in both]
    tinv = [tinv[i] + _dot(tinv[i], rows2(m[i])) for i in n]
    sol = [_dot(tinv[i], jnp.concatenate([rows2(r_pre[i]), rows2(kt[s][:, sl] * em[i])], axis=1))
           for i, (s, sl) in enumerate(items)]
    u_pre = [x[:, :w] for x in sol]
    kq = [x[:, w:] for x in sol]
    y_pre = [_dot(jnp.concatenate([a_rk[i], -a_rb[i]], axis=1),
                  jnp.concatenate([v_rows[i], rows2(u_pre[i])], axis=0)) for i in n]
    r_eff = [rt[s][:, sl] * em[i] - _dot(a_rb[i], rows2(kq[i])) for i, (s, sl) in enumerate(items)]
    bbar = [bh[s][:, sl] * em[i] for i, (s, sl) in enumerate(items)]
    kbar = [kh[s][:, sl] * em[i] for i, (s, sl) in enumerate(items)]
    mmat = [eye * (em[i] * em[i]) - jnp.where(same, _dot_tn(kq[i], bbar[i]), 0.0) for i in n]
    s_pre = [jnp.where(same, _dot_tn(jnp.concatenate([vi[i], -u_pre[i]], axis=0),
                                     jnp.concatenate([kbar[i], bbar[i]], axis=0)), 0.0) for i in n]
    st = [st_ref[p] for p in range(npair)]
    for i, (s, sl) in enumerate(items):
        p = i % npair
        y_ref[0, 0, subs[s], sl] = _dot_nt(r_eff[i], st[p]) + y_pre[i]
        hi = st[p].astype(BF16)
        lo = (st[p] - hi.astype(F32)).astype(BF16)
        mb = mmat[i].astype(BF16)
        st[p] = (jnp.dot(hi, mb, preferred_element_type=F32) + jnp.dot(lo, mb, preferred_element_type=F32)
                 + s_pre[i])
    for p in range(npair):
        st_ref[p] = st[p]


def _wkv(r, v, kk, km, bv, lw, lc):
    b, l, d = r.shape
    c = WKV_CHUNK * WKV_CHUNKS_PER_STEP
    ncc = lc // c
    nlc = (l - lc) // c

    def chunk(dr, i):
        return jnp.where(dr == 0, i, jnp.where(i < ncc, ncc - 1 - i, nlc + 2 * ncc - 1 - i))

    shared = pl.BlockSpec((1, c, d), lambda dr, bi, i: (bi, chunk(dr, i), 0))
    per_dir = pl.BlockSpec((1, 1, c, d), lambda dr, bi, i: (dr, bi, chunk(dr, i), 0))
    return pl.pallas_call(
        _wkv_kernel,
        out_shape=jax.ShapeDtypeStruct((2, b, l, d), F32),
        grid=(2, b, l // c),
        in_specs=[shared, shared, shared, per_dir, per_dir, per_dir],
        out_specs=per_dir,
        scratch_shapes=[pltpu.VMEM((d // WKV_PAIR, WKV_PAIR, WKV_PAIR), F32)],
        compiler_params=pltpu.CompilerParams(dimension_semantics=("parallel", "parallel", "arbitrary"),
                                             vmem_limit_bytes=_vmem_limit(32)),
        name="wkv7_chunked",
    )(r, v, kk, km, bv, lw)


def _rwkv_out_kernel(y_ref, bonus_ref, g_ref, lnw_ref, lnb_ref, wo_ref, bd_ref, h_ref, mods_ref, gffn_ref,
                     wrt_ref, bias_ref, hn_ref, n2_ref, eid_ref, rank_ref, w_ref, cnt_ref, run_ref):
    y = y_ref[0, 0] + y_ref[1, 0]
    bd = bd_ref[...]
    mean = _head_sum(y, bd) * (1.0 / RWKV_HEAD)
    yc = y - mean
    var = _head_sum(yc * yc, bd) * (1.0 / RWKV_HEAD)
    yn = yc * lax.rsqrt(var + GN_EPS) * lnw_ref[...] + lnb_ref[...]
    out = (yn + bonus_ref[0]) * g_ref[0].astype(F32)
    _mixer_tail(_dot(out, wo_ref[...]), h_ref[0], mods_ref[0, 0], gffn_ref, wrt_ref, bias_ref, hn_ref, n2_ref,
                eid_ref, rank_ref, w_ref, cnt_ref, run_ref)


def _rwkv_out(y, bonus, g, lnw, lnb, wo, bd, h, mods, gffn, wrt, bias, nct):
    b, l, d = h.shape
    tl = TOKEN_TILE
    tok = lambda w: pl.BlockSpec((1, tl, w), lambda i, j: (i, j, 0))
    full = lambda x: pl.BlockSpec(x.shape, lambda i, j: (0,) * x.ndim)
    shapes, specs = _tail_outs(b, l, d)
    return pl.pallas_call(
        _rwkv_out_kernel,
        out_shape=shapes,
        grid=(b, l // tl),
        in_specs=[pl.BlockSpec((2, 1, tl, d), lambda i, j: (0, i, j, 0)), tok(d), tok(d),
                  full(lnw), full(lnb), full(wo), full(bd), tok(d),
                  pl.BlockSpec((1, 1, N_MODS, d), lambda i, j: (i, jnp.where(j < nct, 0, 1), 0, 0)),
                  full(gffn), full(wrt), full(bias)],
        out_specs=specs,
        scratch_shapes=[pltpu.VMEM((N_EXPERTS, 1), F32)],
        compiler_params=pltpu.CompilerParams(dimension_semantics=("arbitrary", "arbitrary"),
                                             vmem_limit_bytes=_vmem_limit(40)),
        name="rwkv_out",
    )(y, bonus, g, lnw, lnb, wo, bd, h, mods, gffn, wrt, bias)


def _rope_table(n_lat, n_ctx):
    dim = SWA_HEAD_DIM
    nf = dim // 4
    inv = ROPE_THETA ** (-jnp.arange(nf, dtype=F32) / nf)
    row = jnp.repeat(jnp.arange(n_lat // GRID_W, dtype=F32), GRID_W)
    col = jnp.tile(jnp.arange(GRID_W, dtype=F32), n_lat // GRID_W)
    ar = row[:, None] * inv
    ac = col[:, None] * inv
    ang = jnp.concatenate([ar, ar, ac, ac], axis=-1)
    cos = jnp.concatenate([jnp.ones((n_ctx, dim), F32), jnp.cos(ang)], axis=0)
    sin = jnp.concatenate([jnp.zeros((n_ctx, dim), F32), jnp.sin(ang)], axis=0)
    return jnp.tile(cos, (1, 2)), jnp.tile(sin, (1, 2))


def _layout_attn_weights(w_in, w_uq, w_ukv):
    d = w_in.shape[0]
    s0 = MLA_Q_RANK
    s1 = s0 + MLA_KV_RANK
    s2 = s1 + MLA_ROPE
    s3 = s2 + SWA_HEADS * SWA_HEAD_DIM
    s4 = s3 + SWA_KV_HEADS * SWA_HEAD_DIM
    rep = lambda w: jnp.concatenate(
        [jnp.tile(w[:, g * SWA_HEAD_DIM:(g + 1) * SWA_HEAD_DIM], (1, SWA_GROUP)) for g in range(SWA_KV_HEADS)], axis=1)
    win = jnp.concatenate([w_in[:, :s1], w_in[:, s2:s3], rep(w_in[:, s3:s4]), rep(w_in[:, s4:]),
                           w_in[:, s1:s2], jnp.zeros((d, V7X_LANES - MLA_ROPE), w_in.dtype)], axis=1)
    qh = MLA_NOPE + MLA_ROPE
    pad = jnp.zeros((w_uq.shape[0], V7X_MXU_DIM - qh), w_uq.dtype)
    wuq = jnp.concatenate([jnp.concatenate([w_uq[:, h * qh:(h + 1) * qh], pad], axis=1) for h in range(MLA_HEADS)], axis=1)
    kvh = MLA_NOPE + MLA_V
    wuk = jnp.concatenate([w_ukv[:, h * kvh:h * kvh + MLA_NOPE] for h in range(MLA_HEADS)], axis=1)
    wuvt = jnp.concatenate([w_ukv[:, h * kvh + MLA_NOPE:(h + 1) * kvh] for h in range(MLA_HEADS)], axis=1).T
    return win.astype(BF16), wuq.astype(BF16), wuk.astype(BF16), wuvt.astype(BF16)


def _lora_pair(w_down, w_up):
    rank = w_down.shape[2]
    down = jnp.concatenate([w_down[0], w_down[1]], axis=1)
    z = jnp.zeros((rank, w_up.shape[2]), w_up.dtype)
    up = jnp.stack([jnp.concatenate([w_up[0], z], axis=0), jnp.concatenate([z, w_up[1]], axis=0)], axis=0)
    return down.astype(BF16), up.astype(BF16)


def _head_block_diag():
    i = jnp.arange(V7X_MXU_DIM) // RWKV_HEAD
    return (i[:, None] == i[None, :]).astype(BF16)


def kernel(x, c, ctx, c_ctx, ada_w, ada_b, norm_mix, norm_ffn, norm_final, attn_w_in, attn_q_norm, attn_kv_norm, attn_w_uq, attn_w_ukv, attn_sinks, attn_w_o, rwkv_mu, rwkv_w_r, rwkv_w_k, rwkv_w_v, rwkv_w_o, rwkv_g1, rwkv_g2, rwkv_w0, rwkv_w1, rwkv_w2, rwkv_a0, rwkv_a1, rwkv_a2, rwkv_k_k, rwkv_k_a, rwkv_r_k, rwkv_ln_w, rwkv_ln_b, moe_router, moe_bias, moe_w_gate, moe_w_up, moe_w_down, moe_ws_gate, moe_ws_up, moe_ws_down):
    bsz, s, d = x.shape
    lc = ctx.shape[1]
    l = lc + s
    depth = ada_w.shape[0]
    nct = lc // TOKEN_TILE
    assert lc % TOKEN_TILE == 0 and s % TOKEN_TILE == 0 and s >= SWA_BAND
    assert lc % (WKV_CHUNK * WKV_CHUNKS_PER_STEP) == 0
    assert d % V7X_MXU_DIM == 0 and WKV_CHUNK * 2 == V7X_LANES
    ngrp = SAMPLE_GROUPS
    bg = bsz // ngrp
    assert bsz % ngrp == 0 and (bg * l) % (8 * V7X_SC_WORKERS) == 0

    streams = [(ctx, x, g * bg, 0) for g in range(ngrp)]
    out = None
    cos, sin = _rope_table(s, lc)
    bd = _head_block_diag()
    rows = -(-(bsz + 1) // 8) * 8
    cc = jnp.concatenate([c, c_ctx[None, :], jnp.zeros((rows - bsz - 1, d), F32)], axis=0)
    row2 = lambda a: a.reshape(1, -1)

    for li in range(depth):
        with_ctx = li < depth - 1
        i = li // 2
        ada = _ada_mods(cc, ada_w, ada_b, li)
        mods_all = jnp.stack([jnp.broadcast_to(ada[bsz].reshape(1, N_MODS, d), (bsz, N_MODS, d)),
                              ada[:bsz].reshape(bsz, N_MODS, d)], axis=1)
        wrt = jnp.concatenate([moe_router[li].T, jnp.zeros((GATE_W - N_EXPERTS, d), F32)], axis=0)
        bias = moe_bias[li].reshape(N_GROUPS, GROUP_SIZE, 1)
        if li % 2 == 0:
            win, wuq, wuk, wuvt = _layout_attn_weights(attn_w_in[i], attn_w_uq[i], attn_w_ukv[i])
            wo = attn_w_o[i].astype(BF16)
        else:
            w1, w2 = _lora_pair(rwkv_w1[i], rwkv_w2[i])
            a1, a2 = _lora_pair(rwkv_a1[i], rwkv_a2[i])
            wr, wk, wv, wo = [w[i].astype(BF16) for w in (rwkv_w_r, rwkv_w_k, rwkv_w_v, rwkv_w_o)]
            g1, g2 = rwkv_g1[i].astype(BF16), rwkv_g2[i].astype(BF16)
        for g in range(ngrp):
            mods = mods_all[g * bg:(g + 1) * bg]
            if li % 2 == 0:
                q, k, vt, qs, ks, vs = _attn_proj(streams[g], bg, l, mods, row2(norm_mix[li]), win,
                                                  row2(attn_q_norm[i]), row2(attn_kv_norm[i]), wuq, wuk, wuvt,
                                                  cos, sin, nct)
                a = _mla_attention(q, k, vt, lc, 0 if with_ctx else lc // MLA_Q_TILE)
                bm = _swa_attention(attn_sinks[i], qs, ks, vs, lc, 0 if with_ctx else lc // SWA_Q_TILE)
                tail = _attn_out(a, bm, streams[g], mods, wo, row2(norm_ffn[li]), wrt, bias, nct)
            else:
                h = streams[g][0]
                assert streams[g][0] is streams[g][1]
                r, v, kk, gt, km, bv, lw, bonus = _rwkv_proj(
                    h, mods, row2(norm_mix[li]), rwkv_mu[i], wr, wk, wv, g1, g2, w1, w2, a1, a2,
                    rwkv_w0[i], rwkv_a0[i], row2(rwkv_k_k[i]), row2(rwkv_k_a[i]), row2(rwkv_r_k[i]), bd, nct)
                y = _wkv(r, v, kk, km, bv, lw, lc)
                tail = _rwkv_out(y, bonus, gt, row2(rwkv_ln_w[i]), row2(rwkv_ln_b[i]), wo,
                                 bd, h, mods, row2(norm_ffn[li]), wrt, bias, nct)
            h, n2p, eid, rank, wcols, counts = tail
            moe_w = (moe_w_gate, moe_w_up, moe_w_down, moe_ws_gate, moe_ws_up, moe_ws_down)
            if li < depth - 1:
                h = _moe_sparse(n2p, eid, rank, wcols, counts, h, mods, *moe_w, row2(norm_final), nct, li)
                streams[g] = (h, h, 0, nct)
            else:
                out = _moe_sparse(n2p, eid, rank, wcols, counts, h, mods, *moe_w, row2(norm_final), nct, li,
                                  out_buf=out, out_b0=g * bg, out_batch=bsz, last=True)
    return out
```

```python
import functools

import jax
import jax.numpy as jnp
from jax import lax
from jax.experimental import pallas as pl
from jax.experimental.pallas import tpu as pltpu
from jax.experimental.pallas import tpu_sc as plsc

F32 = jnp.float32
BF16 = jnp.bfloat16
HIGHEST = lax.Precision.HIGHEST

GRID_W = 64
NORM_EPS = 1e-6
ROPE_THETA = 10000.0
NEG_INF = -1e30
N_MODS = 6

MLA_HEADS = 4
MLA_Q_RANK = 384
MLA_KV_RANK = 256
MLA_NOPE = 128
MLA_ROPE = 64
MLA_V = 128

SWA_HEADS = 8
SWA_KV_HEADS = 2
SWA_GROUP = SWA_HEADS // SWA_KV_HEADS
SWA_HEAD_DIM = 64
WINDOW = 128

RWKV_HEAD = 64
DECAY_LORA = 64
ICLR_LORA = 64
GATE_LORA = 128
GN_EPS = 64e-5

N_EXPERTS = 64
TOP_K = 6
N_GROUPS = 8
TOPK_GROUPS = 4
GROUP_SIZE = N_EXPERTS // N_GROUPS
ROUTED_SCALE = 2.5
GATE_W = 128

V7X_LANES = 128
V7X_MXU_DIM = 256
V7X_VMEM_BYTES = 64 * 1024 * 1024
V7X_SC_CORES = 2
V7X_SC_SUBCORES = 16
V7X_SC_WORKERS = V7X_SC_CORES * V7X_SC_SUBCORES

TOKEN_TILE = 256
MLA_Q_TILE = 256
MLA_HEADS_PER_STEP = 2
LOG2E = 1.4426950408889634
SWA_Q_TILE = 128
SWA_BAND = SWA_Q_TILE + 2 * WINDOW
WKV_CHUNK = 64
WKV_PAIR = 2 * RWKV_HEAD
WKV_CHUNKS_PER_STEP = 4
MOE_ROW_TILE = 512
SAMPLE_GROUPS = 2
SC_MAX_CHUNK = 64


def _vmem_limit(mib):
    return min(mib * 1024 * 1024, V7X_VMEM_BYTES - 4 * 1024 * 1024)


def _dot(a, b):
    return jnp.dot(a.astype(BF16), b.astype(BF16), preferred_element_type=F32)


def _dot_nt(a, b):
    return lax.dot_general(a.astype(BF16), b.astype(BF16), (((1,), (1,)), ((), ())),
                           preferred_element_type=F32)


def _dot_tn(a, b):
    return lax.dot_general(a.astype(BF16), b.astype(BF16), (((0,), (0,)), ((), ())),
                           preferred_element_type=F32)


def _sigmoid(x):
    return 1.0 / (1.0 + jnp.exp(-x))


def _silu(x):
    return x * _sigmoid(x)


def _rms(x, g):
    return x * lax.rsqrt(jnp.mean(x * x, axis=-1, keepdims=True) + NORM_EPS) * g


def _norm_mod(x, g, shift, scale):
    return _rms(x, g) * (1.0 + scale) + shift


def _split_dot(x, w):
    hi = x.astype(BF16)
    lo = (x - hi.astype(F32)).astype(BF16)
    return (jnp.dot(hi, w, preferred_element_type=F32) + jnp.dot(lo, w, preferred_element_type=F32))


def _head_sum(x, bd):
    w = bd.shape[0]
    parts = [_split_dot(x[:, c * w:(c + 1) * w], bd) for c in range(x.shape[1] // w)]
    return jnp.concatenate(parts, axis=1)


def _ada_kernel(c_ref, w_ref, b_ref, o_ref):
    s = _silu(c_ref[...])
    o_ref[...] = jnp.dot(s, w_ref[0], precision=HIGHEST, preferred_element_type=F32) + b_ref[0]


def _ada_mods(cc, w, b, layer):
    rows, d = cc.shape
    depth, _, n = w.shape
    return pl.pallas_call(
        _ada_kernel,
        out_shape=jax.ShapeDtypeStruct((rows, n), F32),
        grid=(n // d,),
        in_specs=[pl.BlockSpec((rows, d), lambda i: (0, 0)),
                  pl.BlockSpec((1, d, d), lambda i: (layer, 0, i)),
                  pl.BlockSpec((1, 1, d), lambda i: (layer, 0, i))],
        out_specs=pl.BlockSpec((rows, d), lambda i: (0, i)),
        compiler_params=pltpu.CompilerParams(dimension_semantics=("parallel",),
                                             vmem_limit_bytes=_vmem_limit(32)),
        name="ada_mods",
    )(cc, w, b.reshape(depth, 1, n))


def _rope128(x, cos, sin, first_half):
    rot = jnp.where(first_half, -pltpu.roll(x, V7X_LANES - 16, axis=1), pltpu.roll(x, 16, axis=1))
    return x * cos + rot * sin


_C_CQ = 0
_C_CKV = _C_CQ + MLA_Q_RANK
_C_QS = _C_CKV + MLA_KV_RANK
_C_KS = _C_QS + SWA_HEADS * SWA_HEAD_DIM
_C_VS = _C_KS + SWA_KV_HEADS * V7X_MXU_DIM
_C_KR = _C_VS + SWA_KV_HEADS * V7X_MXU_DIM
_C_END = _C_KR + V7X_LANES
_SWA_W = SWA_KV_HEADS * V7X_MXU_DIM
_MLA_QK_W = MLA_HEADS * V7X_MXU_DIM


def _stream_specs(stream, nct, tl):
    ctx_arr, lat_arr, b0, lat_off = stream
    d = ctx_arr.shape[2]
    return [pl.BlockSpec((1, tl, d), lambda i, j: (i + b0, jnp.minimum(j, nct - 1), 0)),
            pl.BlockSpec((1, tl, d), lambda i, j: (i + b0, jnp.maximum(j - nct, 0) + lat_off, 0))]


def _stream_tile(c_ref, x_ref, nct):
    rows = c_ref.shape[1]
    take_ctx = lax.broadcasted_iota(jnp.int32, (rows, 1), 0) < jnp.where(pl.program_id(1) < nct, rows, 0)
    return jnp.where(take_ctx, c_ref[0], x_ref[0])


def _attn_proj_kernel(c_ref, x_ref, mods_ref, g_ref, win_ref, qn_ref, kvn_ref, wuq_ref, wuk_ref, wuvt_ref, cos_ref,
                      sin_ref, q_ref, k_ref, vt_ref, qs_ref, ks_ref, vs_ref, *, nct):
    m = mods_ref[0, 0]
    n = _norm_mod(_stream_tile(c_ref, x_ref, nct), g_ref[...], m[0:1], m[1:2])
    u = _dot(n, win_ref[...])
    cos = cos_ref[...]
    sin = sin_ref[...]
    lane = lax.broadcasted_iota(jnp.int32, (1, V7X_LANES), 1)
    first_half = (lane % 32) < 16

    def rope(x):
        return _rope128(x, cos, sin, first_half)

    scale_a = (MLA_NOPE + MLA_ROPE) ** -0.5 * LOG2E
    scale_b = SWA_HEAD_DIM ** -0.5 * LOG2E
    q = _dot(_rms(u[:, _C_CQ:_C_CKV], qn_ref[...]), wuq_ref[...])
    ckv = _rms(u[:, _C_CKV:_C_QS], kvn_ref[...])
    kn = _dot(ckv, wuk_ref[...])
    vt_ref[0] = _dot_nt(wuvt_ref[...], ckv).astype(BF16)
    kr = rope(u[:, _C_KR:_C_END]).astype(BF16)
    for h in range(MLA_HEADS):
        o = h * V7X_MXU_DIM
        q_ref[0, :, o:o + V7X_LANES] = (q[:, o:o + V7X_LANES] * scale_a).astype(BF16)
        q_ref[0, :, o + V7X_LANES:o + V7X_MXU_DIM] = (rope(q[:, o + V7X_LANES:o + V7X_MXU_DIM]) * scale_a).astype(BF16)
        k_ref[0, :, o:o + V7X_LANES] = kn[:, h * MLA_NOPE:(h + 1) * MLA_NOPE].astype(BF16)
        k_ref[0, :, o + V7X_LANES:o + V7X_MXU_DIM] = kr
    for c in range((_C_KS - _C_QS) // V7X_LANES):
        o = c * V7X_LANES
        qs_ref[0, :, o:o + V7X_LANES] = (rope(u[:, _C_QS + o:_C_QS + o + V7X_LANES]) * scale_b).astype(BF16)
    for c in range(_SWA_W // V7X_LANES):
        o = c * V7X_LANES
        ks_ref[0, :, o:o + V7X_LANES] = rope(u[:, _C_KS + o:_C_KS + o + V7X_LANES]).astype(BF16)
    vs_ref[0] = u[:, _C_VS:_C_KR].astype(BF16)


def _attn_proj(stream, b, l, mods, g, win, qn, kvn, wuq, wuk, wuvt, cos, sin, nct):
    d = stream[0].shape[2]
    tl = TOKEN_TILE
    tok = lambda w: pl.BlockSpec((1, tl, w), lambda i, j: (i, j, 0))
    full = lambda a: pl.BlockSpec(a.shape, lambda i, j: (0,) * a.ndim)
    sds = jax.ShapeDtypeStruct
    dv = MLA_HEADS * MLA_V
    return pl.pallas_call(
        functools.partial(_attn_proj_kernel, nct=nct),
        out_shape=[sds((b, l, _MLA_QK_W), BF16), sds((b, l, _MLA_QK_W), BF16), sds((b, dv, l), BF16),
                   sds((b, l, SWA_HEADS * SWA_HEAD_DIM), BF16), sds((b, l, _SWA_W), BF16), sds((b, l, _SWA_W), BF16)],
        grid=(b, l // tl),
        in_specs=_stream_specs(stream, nct, tl) + [
                  pl.BlockSpec((1, 1, N_MODS, d), lambda i, j: (i, jnp.where(j < nct, 0, 1), 0, 0)),
                  full(g), full(win), full(qn), full(kvn), full(wuq), full(wuk), full(wuvt),
                  pl.BlockSpec((tl, V7X_LANES), lambda i, j: (j, 0)),
                  pl.BlockSpec((tl, V7X_LANES), lambda i, j: (j, 0))],
        out_specs=[tok(_MLA_QK_W), tok(_MLA_QK_W), pl.BlockSpec((1, dv, tl), lambda i, j: (i, 0, j)),
                   tok(SWA_HEADS * SWA_HEAD_DIM), tok(_SWA_W), tok(_SWA_W)],
        compiler_params=pltpu.CompilerParams(dimension_semantics=("parallel", "parallel"),
                                             vmem_limit_bytes=_vmem_limit(48)),
        name="attn_proj",
    )(stream[0], stream[1], mods, g, win, qn, kvn, wuq, wuk, wuvt, cos, sin)


def _mla_kernel(q_ref, k_ref, vt_ref, o_ref, *, nct_q, lc):
    hw = V7X_MXU_DIM

    def attend(nk):
        st = [_dot_nt(k_ref[0, 0:nk, hh * hw:(hh + 1) * hw], q_ref[0, :, hh * hw:(hh + 1) * hw])
              for hh in range(MLA_HEADS_PER_STEP)]
        for hh, s in enumerate(st):
            p = jnp.exp2(s - jnp.max(s, axis=0, keepdims=True))
            den = jnp.sum(p, axis=0, keepdims=True)
            ot = _dot(vt_ref[0, hh * MLA_V:(hh + 1) * MLA_V, 0:nk], p) / den
            o_ref[0, :, hh * MLA_V:(hh + 1) * MLA_V] = ot.T.astype(o_ref.dtype)

    @pl.when(pl.program_id(2) < nct_q)
    def _():
        attend(lc)

    @pl.when(pl.program_id(2) >= nct_q)
    def _():
        attend(k_ref.shape[1])


def _mla_attention(q, k, vt, lc, q_tile0):
    b, l, _ = q.shape
    tq = MLA_Q_TILE
    hps = MLA_HEADS_PER_STEP
    return pl.pallas_call(
        functools.partial(_mla_kernel, nct_q=lc // tq - q_tile0, lc=lc),
        out_shape=jax.ShapeDtypeStruct((b, l, MLA_HEADS * MLA_V), BF16),
        grid=(b, MLA_HEADS // hps, l // tq - q_tile0),
        in_specs=[pl.BlockSpec((1, tq, hps * V7X_MXU_DIM), lambda i, h, j: (i, j + q_tile0, h)),
                  pl.BlockSpec((1, l, hps * V7X_MXU_DIM), lambda i, h, j: (i, 0, h)),
                  pl.BlockSpec((1, hps * MLA_V, l), lambda i, h, j: (i, h, 0))],
        out_specs=pl.BlockSpec((1, tq, hps * MLA_V), lambda i, h, j: (i, j + q_tile0, h)),
        compiler_params=pltpu.CompilerParams(dimension_semantics=("parallel", "parallel", "parallel"),
                                             vmem_limit_bytes=_vmem_limit(48)),
        name="mla_attention",
    )(q, k, vt)


def _swa_kernel(sink_ref, q_ref, k_ref, v_ref, o_ref, *, lc, q_tile0):
    tq = SWA_Q_TILE
    l = k_ref.shape[1]
    r0 = (pl.program_id(1) + q_tile0) * tq
    start = pl.multiple_of(jnp.clip(r0 - WINDOW, lc, l - SWA_BAND), tq)
    rows = SWA_GROUP * tq
    row = lax.broadcasted_iota(jnp.int32, (rows, 1), 0)
    qpos = jnp.where(r0 >= lc, r0, -l) + row % tq
    kpos = start + lax.broadcasted_iota(jnp.int32, (1, SWA_BAND), 1)
    valid = jnp.abs(qpos - kpos) <= WINDOW
    lane = lax.broadcasted_iota(jnp.int32, (1, V7X_MXU_DIM), 1)
    head = [(lane // SWA_HEAD_DIM) == hh for hh in range(SWA_GROUP)]
    groups = range(SWA_KV_HEADS)
    sls = [slice(g * V7X_MXU_DIM, (g + 1) * V7X_MXU_DIM) for g in groups]
    qstack = []
    for sl in sls:
        qg = q_ref[0, :, sl]
        zero = jnp.zeros_like(qg)
        qstack.append(jnp.concatenate([jnp.where(head[hh], qg, zero) for hh in range(SWA_GROUP)], axis=0))
    sc = [_dot_nt(qstack[g], k_ref[0, 0:lc, sls[g]]) for g in groups]
    sb = [_dot_nt(qstack[g], k_ref[0, pl.ds(start, SWA_BAND), sls[g]]) for g in groups]
    for g in groups:
        sl = sls[g]
        sbm = jnp.where(valid, sb[g], NEG_INF)
        sk = jnp.zeros((rows, 1), F32)
        for hh in range(SWA_GROUP):
            sk = jnp.where(row // tq == hh, sink_ref[g * SWA_GROUP + hh] * LOG2E, sk)
        mx = jnp.maximum(jnp.maximum(jnp.max(sc[g], axis=-1, keepdims=True), jnp.max(sbm, axis=-1, keepdims=True)), sk)
        pc = jnp.exp2(sc[g] - mx)
        pb = jnp.exp2(sbm - mx)
        den = jnp.sum(pc, axis=-1, keepdims=True) + jnp.sum(pb, axis=-1, keepdims=True) + jnp.exp2(sk - mx)
        ostack = (_dot(pc, v_ref[0, 0:lc, sl]) + _dot(pb, v_ref[0, pl.ds(start, SWA_BAND), sl])) / den
        o = jnp.zeros((tq, V7X_MXU_DIM), F32)
        for hh in range(SWA_GROUP):
            o = o + jnp.where(head[hh], ostack[hh * tq:(hh + 1) * tq], 0.0)
        o_ref[0, :, sl] = o.astype(o_ref.dtype)


def _swa_attention(sinks, q, k, v, lc, q_tile0):
    b, l, _ = q.shape
    tq = SWA_Q_TILE
    return pl.pallas_call(
        functools.partial(_swa_kernel, lc=lc, q_tile0=q_tile0),
        out_shape=jax.ShapeDtypeStruct((b, l, SWA_HEADS * SWA_HEAD_DIM), BF16),
        grid=(b, l // tq - q_tile0),
        in_specs=[pl.BlockSpec(memory_space=pltpu.SMEM),
                  pl.BlockSpec((1, tq, SWA_HEADS * SWA_HEAD_DIM), lambda i, j: (i, j + q_tile0, 0)),
                  pl.BlockSpec((1, l, _SWA_W), lambda i, j: (i, 0, 0)),
                  pl.BlockSpec((1, l, _SWA_W), lambda i, j: (i, 0, 0))],
        out_specs=pl.BlockSpec((1, tq, SWA_HEADS * SWA_HEAD_DIM), lambda i, j: (i, j + q_tile0, 0)),
        compiler_params=pltpu.CompilerParams(dimension_semantics=("parallel", "parallel"),
                                             vmem_limit_bytes=_vmem_limit(48)),
        name="swa_attention",
    )(sinks, q, k, v)


def _pack_bf16_pair(x):
    w = x.shape[1] // 2
    lo = pltpu.bitcast(x[:, :w].astype(BF16).astype(F32), jnp.int32)
    hi = pltpu.bitcast(x[:, w:].astype(BF16).astype(F32), jnp.int32)
    return lax.shift_right_logical(lo, jnp.int32(16)) | (hi & jnp.int32(-65536))


def _unpack_bf16_pair(p):
    return pltpu.bitcast(p << 16, F32), pltpu.bitcast(p & jnp.int32(-65536), F32)


def _route(n2, wrt, bias, run_ref):
    n_hi = n2.astype(BF16)
    n_lo = (n2 - n_hi.astype(F32)).astype(BF16)
    w_hi = wrt.astype(BF16)
    w_lo = (wrt - w_hi.astype(F32)).astype(BF16)
    logits = _dot_nt(w_hi, n_hi) + (_dot_nt(w_hi, n_lo) + _dot_nt(w_lo, n_hi))
    rows = logits.shape[1]
    scores = _sigmoid(logits[0:N_EXPERTS])

    def select(sc2):
        cols = sc2.shape[1]
        shape3 = (N_GROUPS, GROUP_SIZE, cols)
        choice = sc2.reshape(shape3) + bias
        ji = lax.broadcasted_iota(jnp.int32, shape3, 1).astype(F32)
        m1 = jnp.max(choice, axis=1, keepdims=True)
        first = jnp.min(jnp.where(choice == m1, ji, float(GROUP_SIZE)), axis=1, keepdims=True)
        m2 = jnp.max(jnp.where(ji == first, -jnp.inf, choice), axis=1, keepdims=True)
        gs = m1 + m2
        gidx = lax.broadcasted_iota(jnp.int32, gs.shape, 0).astype(F32)
        gsel = jnp.zeros_like(gs)
        for _ in range(TOPK_GROUPS):
            mx = jnp.max(gs, axis=0, keepdims=True)
            pick = gidx == jnp.min(jnp.where(gs == mx, gidx, float(N_GROUPS)), axis=0, keepdims=True)
            gsel = jnp.where(pick, 1.0, gsel)
            gs = jnp.where(pick, -jnp.inf, gs)
        cand = jnp.where(gsel > 0.0, choice, -jnp.inf).reshape(N_EXPERTS, cols)
        eidx = lax.broadcasted_iota(jnp.int32, (N_EXPERTS, cols), 0).astype(F32)
        out = []
        for _ in range(TOP_K):
            mx = jnp.max(cand, axis=0, keepdims=True)
            pick = eidx == jnp.min(jnp.where(cand == mx, eidx, float(N_EXPERTS)), axis=0, keepdims=True)
            out.append(jnp.where(pick, 1.0, 0.0))
            cand = jnp.where(pick, -jnp.inf, cand)
        return out

    blocks = [select(scores[:, o:o + V7X_LANES]) for o in range(0, rows, V7X_LANES)]
    picks = [jnp.concatenate([blk[k] for blk in blocks], axis=1) > 0.0 for k in range(TOP_K)]
    ei = lax.broadcasted_iota(jnp.int32, (N_EXPERTS, rows), 0).astype(F32)
    esel = jnp.zeros((N_EXPERTS, rows), F32)
    for pick in picks:
        esel = jnp.where(pick, 1.0, esel)
    before = jnp.where(lax.broadcasted_iota(jnp.int32, (rows, rows), 0) < lax.broadcasted_iota(jnp.int32, (rows, rows), 1),
                       1.0, 0.0).astype(BF16)
    slot = jnp.dot(esel.astype(BF16), before, preferred_element_type=F32) + run_ref[...]
    run_ref[...] += jnp.sum(esel, axis=1, keepdims=True)
    sc = [jnp.sum(jnp.where(pick, scores, 0.0), axis=0, keepdims=True) for pick in picks]
    tot = sc[0]
    for x in sc[1:]:
        tot = tot + x
    k8 = lax.broadcasted_iota(jnp.int32, (8, rows), 0)
    kw = lax.broadcasted_iota(jnp.int32, (GATE_W, rows), 0)
    eid = jnp.zeros((8, rows), jnp.int32)
    rank = jnp.zeros((8, rows), jnp.int32)
    wk = jnp.zeros((GATE_W, rows), F32)
    for k, pick in enumerate(picks):
        e_k = jnp.sum(jnp.where(pick, ei, 0.0), axis=0, keepdims=True).astype(jnp.int32)
        r_k = jnp.sum(jnp.where(pick, slot, 0.0), axis=0, keepdims=True).astype(jnp.int32)
        eid = jnp.where(k8 == k, e_k, eid)
        rank = jnp.where(k8 == k, r_k, rank)
        wk = jnp.where(kw == k, sc[k] * (ROUTED_SCALE / tot), wk)
    return eid, rank, wk.T


def _mixer_tail(o, h, m, gffn_ref, wrt_ref, bias_ref, hn_ref, n2_ref, eid_ref, rank_ref, w_ref, cnt_ref, run_ref):
    @pl.when((pl.program_id(0) == 0) & (pl.program_id(1) == 0))
    def _():
        run_ref[...] = jnp.zeros_like(run_ref)

    hn = h + m[2:3] * o
    hn_ref[0] = hn
    n2 = _norm_mod(hn, gffn_ref[...], m[3:4], m[4:5])
    n2_ref[0] = _pack_bf16_pair(n2)
    eid, rank, wcols = _route(n2, wrt_ref[...], bias_ref[...], run_ref)
    eid_ref[0] = eid
    rank_ref[0] = rank
    w_ref[0] = wcols
    cnt_ref[...] = run_ref[...]


def _attn_out_kernel(a_ref, b_ref, c_ref, x_ref, mods_ref, wo_ref, gffn_ref, wrt_ref, bias_ref,
                     hn_ref, n2_ref, eid_ref, rank_ref, w_ref, cnt_ref, run_ref, *, nct):
    wa = MLA_HEADS * MLA_V
    o = _dot(a_ref[0], wo_ref[0:wa, :]) + _dot(b_ref[0], wo_ref[wa:, :])
    _mixer_tail(o, _stream_tile(c_ref, x_ref, nct), mods_ref[0, 0], gffn_ref, wrt_ref, bias_ref, hn_ref, n2_ref,
                eid_ref, rank_ref, w_ref, cnt_ref, run_ref)


def _tail_outs(b, l, d):
    tl = TOKEN_TILE
    nt = l // tl
    sds = jax.ShapeDtypeStruct
    tok = lambda w: pl.BlockSpec((1, tl, w), lambda i, j: (i, j, 0))
    blk = pl.BlockSpec((1, 8, tl), lambda i, j: (i * nt + j, 0, 0))
    shapes = [sds((b, l, d), F32), sds((b, l, d // 2), jnp.int32), sds((b * nt, 8, tl), jnp.int32),
              sds((b * nt, 8, tl), jnp.int32), sds((b, l, GATE_W), F32), sds((N_EXPERTS, 1), F32)]
    specs = [tok(d), tok(d // 2), blk, blk, tok(GATE_W), pl.BlockSpec((N_EXPERTS, 1), lambda i, j: (0, 0))]
    return shapes, specs


def _attn_out(a, bm, stream, mods, wo, gffn, wrt, bias, nct):
    b, l, _ = a.shape
    d = stream[0].shape[2]
    tl = TOKEN_TILE
    tok = lambda w: pl.BlockSpec((1, tl, w), lambda i, j: (i, j, 0))
    full = lambda x: pl.BlockSpec(x.shape, lambda i, j: (0,) * x.ndim)
    shapes, specs = _tail_outs(b, l, d)
    return pl.pallas_call(
        functools.partial(_attn_out_kernel, nct=nct),
        out_shape=shapes,
        grid=(b, l // tl),
        in_specs=[tok(a.shape[2]), tok(bm.shape[2])] + _stream_specs(stream, nct, tl) + [
                  pl.BlockSpec((1, 1, N_MODS, d), lambda i, j: (i, jnp.where(j < nct, 0, 1), 0, 0)),
                  full(wo), full(gffn), full(wrt), full(bias)],
        out_specs=specs,
        scratch_shapes=[pltpu.VMEM((N_EXPERTS, 1), F32)],
        compiler_params=pltpu.CompilerParams(dimension_semantics=("arbitrary", "arbitrary"),
                                             vmem_limit_bytes=_vmem_limit(40)),
        name="attn_out",
    )(a, bm, stream[0], stream[1], mods, wo, gffn, wrt, bias)


def _moe_dest_kernel(off_ref, eid_ref, rank_ref, dest_ref):
    eid = eid_ref[...]
    dest = rank_ref[...]
    for e in range(N_EXPERTS):
        dest = dest + jnp.where(eid == e, off_ref[e], 0)
    dest_ref[...] = dest


def _moe_dest(off, eid, rank):
    return pl.pallas_call(
        _moe_dest_kernel,
        out_shape=jax.ShapeDtypeStruct(eid.shape, jnp.int32),
        in_specs=[pl.BlockSpec(memory_space=pltpu.SMEM),
                  pl.BlockSpec(eid.shape, lambda: (0, 0, 0)), pl.BlockSpec(eid.shape, lambda: (0, 0, 0))],
        out_specs=pl.BlockSpec(eid.shape, lambda: (0, 0, 0)),
        name="moe_dest",
    )(off, eid, rank)


def _sc_mesh():
    return plsc.VectorSubcoreMesh(core_axis_name="c", subcore_axis_name="s",
                                  num_cores=V7X_SC_CORES, num_subcores=V7X_SC_SUBCORES)


def _sc_chunk(rows_per_worker):
    return max(c for c in range(8, SC_MAX_CHUNK + 1, 8) if rows_per_worker % c == 0)


def _sc_dispatch(xp, dest, p_rows):
    t, w = xp.shape
    tpw = t // V7X_SC_WORKERS
    ch = _sc_chunk(tpw)

    @functools.partial(
        pl.kernel, mesh=_sc_mesh(), out_type=jax.ShapeDtypeStruct((p_rows, w), xp.dtype),
        scratch_types=[pltpu.VMEM((ch, w), xp.dtype)] + [pltpu.VMEM((ch,), jnp.int32)] * TOP_K
        + [pltpu.SemaphoreType.DMA, pltpu.SemaphoreType.DMA],
        name="moe_dispatch")
    def run(x_hbm, dest_hbm, out_hbm, rows_v, *rest):
        idx, (sem_i, sem_o) = rest[:TOP_K], rest[TOP_K:]
        base = (lax.axis_index("s") * V7X_SC_CORES + lax.axis_index("c")) * tpw

        @pl.loop(0, tpw // ch)
        def _(i):
            t0 = base + i * ch
            loads = [pltpu.async_copy(dest_hbm.at[k, pl.ds(t0, ch)], idx[k], sem_i) for k in range(TOP_K)]
            pltpu.sync_copy(x_hbm.at[pl.ds(t0, ch)], rows_v)
            for c in loads:
                c.wait()
            puts = [pltpu.async_copy(rows_v, out_hbm.at[idx[k]], sem_o) for k in range(TOP_K)]
            for c in puts:
                c.wait()

    return run(xp, dest)


def _sc_gather(ys, dest, t):
    w = ys.shape[1]
    tpw = t // V7X_SC_WORKERS
    ch = _sc_chunk(tpw)

    @functools.partial(
        pl.kernel, mesh=_sc_mesh(), out_type=jax.ShapeDtypeStruct((TOP_K, t, w), ys.dtype),
        scratch_types=[pltpu.VMEM((ch, w), ys.dtype)] * 2 + [pltpu.VMEM((ch,), jnp.int32)] * TOP_K
        + [pltpu.SemaphoreType.DMA] * 5,
        name="moe_gather")
    def run(y_hbm, dest_hbm, out_hbm, rows_a, rows_b, *rest):
        idx, (sem_i, sem_ga, sem_gb, sem_wa, sem_wb) = rest[:TOP_K], rest[TOP_K:]
        rows, sem_g, sem_w = (rows_a, rows_b), (sem_ga, sem_gb), (sem_wa, sem_wb)
        base = (lax.axis_index("s") * V7X_SC_CORES + lax.axis_index("c")) * tpw

        @pl.loop(0, tpw // ch)
        def _(i):
            t0 = base + i * ch
            loads = [pltpu.async_copy(dest_hbm.at[k, pl.ds(t0, ch)], idx[k], sem_i) for k in range(TOP_K)]
            for c in loads:
                c.wait()
            gets, puts = [None] * TOP_K, [None] * TOP_K
            gets[0] = pltpu.async_copy(y_hbm.at[idx[0]], rows[0], sem_g[0])
            for k in range(TOP_K):
                if k + 1 < TOP_K:
                    if k >= 1:
                        puts[k - 1].wait()
                    gets[k + 1] = pltpu.async_copy(y_hbm.at[idx[k + 1]], rows[(k + 1) % 2], sem_g[(k + 1) % 2])
                gets[k].wait()
                puts[k] = pltpu.async_copy(rows[k % 2], out_hbm.at[k, pl.ds(t0, ch)], sem_w[k % 2])
            puts[TOP_K - 2].wait()
            puts[TOP_K - 1].wait()

    return run(ys, dest)


def _moe_expert_kernel(te_ref, nv_ref, x_ref, wg_ref, wu_ref, wd_ref, y_ref, wgb_ref, wub_ref, wdb_ref):
    i = pl.program_id(0)

    @pl.when((i == 0) | (te_ref[i] != te_ref[jnp.maximum(i - 1, 0)]))
    def _():
        wgb_ref[...] = wg_ref[0, 0].astype(BF16)
        wub_ref[...] = wu_ref[0, 0].astype(BF16)
        wdb_ref[...] = wd_ref[0, 0].astype(BF16)

    @pl.when(i < nv_ref[0])
    def _():
        lo, hi = _unpack_bf16_pair(x_ref[...])
        half = lo.shape[1]
        hg = _dot(lo, wgb_ref[0:half, :]) + _dot(hi, wgb_ref[half:, :])
        hu = _dot(lo, wub_ref[0:half, :]) + _dot(hi, wub_ref[half:, :])
        y_ref[...] = _pack_bf16_pair(_dot(_silu(hg) * hu, wdb_ref[...]))


def _moe_experts(tile_expert, n_valid, xs, wg, wu, wd, layer):
    p_rows, w = xs.shape
    tm = MOE_ROW_TILE
    _, _, d, f = wg.shape
    wspec = lambda shp: pl.BlockSpec((1, 1) + shp, lambda i, te, nv: (layer, te[i], 0, 0))
    return pl.pallas_call(
        _moe_expert_kernel,
        out_shape=jax.ShapeDtypeStruct((p_rows, w), xs.dtype),
        grid_spec=pltpu.PrefetchScalarGridSpec(
            num_scalar_prefetch=2, grid=(p_rows // tm,),
            in_specs=[pl.BlockSpec((tm, w), lambda i, te, nv: (jnp.minimum(i, nv[0] - 1), 0)),
                      wspec((d, f)), wspec((d, f)), wspec((f, d))],
            out_specs=pl.BlockSpec((tm, w), lambda i, te, nv: (jnp.minimum(i, nv[0] - 1), 0)),
            scratch_shapes=[pltpu.VMEM((d, f), BF16), pltpu.VMEM((d, f), BF16), pltpu.VMEM((f, d), BF16)]),
        compiler_params=pltpu.CompilerParams(dimension_semantics=("arbitrary",),
                                             vmem_limit_bytes=_vmem_limit(32)),
        name="moe_experts",
    )(tile_expert, n_valid, xs, wg, wu, wd)


def _moe_combine_kernel(yg_ref, w_ref, xp_ref, sg_ref, su_ref, sd_ref, h_ref, mods_ref, gfin_ref, *rest, final_norm):
    o_ref = rest[-1]
    xlo, xhi = _unpack_bf16_pair(xp_ref[0])
    half = xlo.shape[1]
    hs = (_silu(_dot(xlo, sg_ref[0, 0:half, :]) + _dot(xhi, sg_ref[0, half:, :]))
          * (_dot(xlo, su_ref[0, 0:half, :]) + _dot(xhi, su_ref[0, half:, :])))
    acc = _dot(hs, sd_ref[0])
    lo = acc[:, :half]
    hi = acc[:, half:]
    w = w_ref[0]
    for k in range(TOP_K):
        ylo, yhi = _unpack_bf16_pair(yg_ref[k, 0])
        wk = w[:, k:k + 1]
        lo = lo + wk * ylo
        hi = hi + wk * yhi
    y = h_ref[0] + mods_ref[0, 0, N_MODS - 1:N_MODS, :] * jnp.concatenate([lo, hi], axis=1)
    if final_norm:
        y = _rms(y, gfin_ref[...])
    o_ref[0] = y


def _moe_combine(yg, wcols, xp, sg, su, sd, h, mods, gfin, nct, layer, out_buf, out_b0, out_batch, latent_only,
                 final_norm):
    b, l, d = h.shape
    tl = TOKEN_TILE
    tile0 = nct if latent_only else 0
    tok = lambda w: pl.BlockSpec((1, tl, w), lambda i, j: (i, j + tile0, 0))
    lay = lambda x: pl.BlockSpec((1,) + x.shape[1:], lambda i, j: (layer,) + (0,) * (x.ndim - 1))
    args = [yg, wcols, xp, sg, su, sd, h, mods, gfin]
    in_specs = [pl.BlockSpec((TOP_K, 1, tl, d // 2), lambda i, j: (0, i, j + tile0, 0)), tok(GATE_W), tok(d // 2),
                lay(sg), lay(su), lay(sd), tok(d),
                pl.BlockSpec((1, 1, N_MODS, d), lambda i, j: (i, jnp.where(j + tile0 < nct, 0, 1), 0, 0)),
                pl.BlockSpec(gfin.shape, lambda i, j: (0, 0))]
    aliases = {}
    if out_buf is not None:
        args.append(out_buf)
        in_specs.append(pl.BlockSpec(memory_space=pl.ANY))
        aliases = {len(args) - 1: 0}
    return pl.pallas_call(
        functools.partial(_moe_combine_kernel, final_norm=final_norm),
        out_shape=jax.ShapeDtypeStruct((out_batch, l - tile0 * tl, d), F32),
        grid=(b, l // tl - tile0),
        in_specs=in_specs,
        out_specs=pl.BlockSpec((1, tl, d), lambda i, j: (i + out_b0, j, 0)),
        input_output_aliases=aliases,
        compiler_params=pltpu.CompilerParams(dimension_semantics=("parallel", "parallel"),
                                             vmem_limit_bytes=_vmem_limit(40)),
        name="moe_combine",
    )(*args)


def _moe_sparse(n2p, eid, rank, wcols, counts, h, mods, wg, wu, wd, sg, su, sd, gfin, nct, layer,
                out_buf=None, out_b0=0, out_batch=None, last=False):
    b, l, d = h.shape
    t = b * l
    tm = MOE_ROW_TILE
    n_tiles = -(-(TOP_K * t + N_EXPERTS * (tm - 1)) // tm)
    tiles_e = (counts.reshape(N_EXPERTS).astype(jnp.int32) + (tm - 1)) // tm
    tile_end = jnp.cumsum(tiles_e)
    off = (tile_end - tiles_e) * tm
    n_valid = tile_end[-1:]
    tile_id = jnp.minimum(jnp.arange(n_tiles, dtype=jnp.int32), n_valid - 1)
    tile_expert = jnp.sum((tile_end[None, :] <= tile_id[:, None]).astype(jnp.int32), axis=1)
    dest = _moe_dest(off, eid, rank).transpose(1, 0, 2).reshape(8, t)
    xs = _sc_dispatch(n2p.reshape(t, d // 2), dest, n_tiles * tm)
    ys = _moe_experts(tile_expert, n_valid, xs, wg, wu, wd, layer)
    yg = _sc_gather(ys, dest, t).reshape(TOP_K, b, l, d // 2)
    return _moe_combine(yg, wcols, n2p, sg, su, sd, h, mods, gfin, nct, layer, out_buf, out_b0,
                        b if out_batch is None else out_batch, last, last)


def _rwkv_proj_kernel(h_ref, hp_ref, hx_ref, mods_ref, g_ref, mu_ref, wr_ref, wk_ref, wv_ref, g1_ref, g2_ref,
                      w1_ref, w2_ref, a1_ref, a2_ref, w0_ref, a0_ref, kk_ref, ka_ref, rk_ref, bd_ref,
                      r_out, v_out, kk_out, g_out, km_out, b_out, lw_out, bonus_out, *, nct):
    j = pl.program_id(1)
    nt = pl.num_programs(1)
    m = mods_ref[0, 0]
    g = g_ref[...]
    n = _norm_mod(h_ref[0], g, m[0:1], m[1:2])
    tl, d = n.shape
    seg_first = (j == 0) | (j == nct)
    seg_last = (j == nct - 1) | (j == nt - 1)
    n_prev = _norm_mod(hp_ref[0], g, m[0:1], m[1:2])[7:8] * jnp.where(seg_first, 0.0, 1.0)
    n_next = _norm_mod(hx_ref[0], g, m[0:1], m[1:2])[0:1] * jnp.where(seg_last, 0.0, 1.0)
    row = lax.broadcasted_iota(jnp.int32, (tl, 1), 0)
    prev = jnp.where(row == 0, n_prev, pltpu.roll(n, 1, axis=0))
    nxt = jnp.where(row == tl - 1, n_next, pltpu.roll(n, tl - 1, axis=0))
    lane = lax.broadcasted_iota(jnp.int32, (1, d), 1)
    xx = jnp.where(lane < d // 2, prev, nxt) - n
    mu = mu_ref[...]
    xr, xw, xk, xv, xa, xg = [n + xx * mu[i:i + 1] for i in range(6)]
    r = _dot(xr, wr_ref[...])
    k = _dot(xk, wk_ref[...])
    v = _dot(xv, wv_ref[...])
    g_out[0] = _dot(_sigmoid(_dot(xg, g1_ref[...])), g2_ref[...]).astype(g_out.dtype)
    tw = jnp.tanh(_dot(xw, w1_ref[...]))
    ta = _dot(xa, a1_ref[...])
    bd = bd_ref[...]
    kk = k * kk_ref[...]
    kk = kk / jnp.maximum(jnp.sqrt(_head_sum(kk * kk, bd)), 1e-12)
    r_out[0] = r.astype(r_out.dtype)
    v_out[0] = v.astype(v_out.dtype)
    kk_out[0] = kk.astype(kk_out.dtype)
    bonus = jnp.zeros_like(v)
    for dr in range(2):
        zw = w0_ref[dr:dr + 1, :] + _dot(tw, w2_ref[dr])
        lw_out[dr, 0] = -jnp.exp(-0.5) * _sigmoid(zw)
        a = _sigmoid(a0_ref[dr:dr + 1, :] + _dot(ta, a2_ref[dr]))
        km = k * (1.0 + (a - 1.0) * ka_ref[...])
        km_out[dr, 0] = km.astype(km_out.dtype)
        b_out[dr, 0] = (kk * a).astype(b_out.dtype)
        bonus = bonus + _head_sum(r * km * rk_ref[...], bd) * v
    bonus_out[0] = bonus


def _rwkv_proj(h, mods, g, mu, wr, wk, wv, g1, g2, w1, w2, a1, a2, w0, a0, kk, ka, rk, bd, nct):
    b, l, d = h.shape
    tl = TOKEN_TILE
    nb8 = l // 8
    tok = pl.BlockSpec((1, tl, d), lambda i, j: (i, j, 0))
    tok2 = pl.BlockSpec((2, 1, tl, d), lambda i, j: (0, i, j, 0))
    full = lambda x: pl.BlockSpec(x.shape, lambda i, j: (0,) * x.ndim)
    sds = jax.ShapeDtypeStruct
    return pl.pallas_call(
        functools.partial(_rwkv_proj_kernel, nct=nct),
        out_shape=[sds((b, l, d), BF16), sds((b, l, d), BF16), sds((b, l, d), BF16), sds((b, l, d), BF16),
                   sds((2, b, l, d), BF16), sds((2, b, l, d), BF16), sds((2, b, l, d), F32), sds((b, l, d), F32)],
        grid=(b, l // tl),
        in_specs=[tok,
                  pl.BlockSpec((1, 8, d), lambda i, j: (i, jnp.maximum(j * (tl // 8) - 1, 0), 0)),
                  pl.BlockSpec((1, 8, d), lambda i, j: (i, jnp.minimum((j + 1) * (tl // 8), nb8 - 1), 0)),
                  pl.BlockSpec((1, 1, N_MODS, d), lambda i, j: (i, jnp.where(j < nct, 0, 1), 0, 0)),
                  full(g), full(mu), full(wr), full(wk), full(wv), full(g1), full(g2), full(w1), full(w2),
                  full(a1), full(a2), full(w0), full(a0), full(kk), full(ka), full(rk), full(bd)],
        out_specs=[tok, tok, tok, tok, tok2, tok2, tok2, tok],
        compiler_params=pltpu.CompilerParams(dimension_semantics=("parallel", "parallel"),
                                             vmem_limit_bytes=_vmem_limit(56)),
        name="rwkv_proj",
    )(h, h, h, mods, g, mu, wr, wk, wv, g1, g2, w1, w2, a1, a2, w0, a0, kk, ka, rk, bd)


def _wkv_kernel(r_ref, v_ref, kk_ref, km_ref, b_ref, lw_ref, y_ref, st_ref):
    c = WKV_CHUNK
    w = WKV_PAIR
    rev = pl.program_id(0)
    sign = 1 - 2 * rev

    @pl.when(pl.program_id(2) == 0)
    def _():
        st_ref[...] = jnp.zeros_like(st_ref)

    ti = lax.broadcasted_iota(jnp.int32, (c, c), 0)
    si = lax.broadcasted_iota(jnp.int32, (c, c), 1)
    tri = jnp.where((si - ti) * sign <= 0, 1.0, 0.0).astype(F32)
    nsub = WKV_CHUNKS_PER_STEP
    subs = [pl.ds(pl.multiple_of(jnp.where(rev == 0, s, nsub - 1 - s) * c, c), c) for s in range(nsub)]
    rt, kt, kh, bh, v32, e_mid = [], [], [], [], [], []
    for rows in subs:
        lw = lw_ref[0, 0, rows, :]
        l_incl = jnp.dot(tri, lw, precision=HIGHEST, preferred_element_type=F32)
        mid = 0.5 * jnp.sum(lw, axis=0, keepdims=True)
        e_neg = jnp.exp(mid - l_incl)
        e_mid.append(jnp.exp(mid))
        rt.append(r_ref[0, rows, :].astype(F32) * jnp.exp(l_incl - mid))
        kt.append(kk_ref[0, rows, :].astype(F32) * jnp.exp(l_incl - lw - mid))
        kh.append(km_ref[0, 0, rows, :].astype(F32) * e_neg)
        bh.append(b_ref[0, 0, rows, :].astype(F32) * e_neg)
        v32.append(v_ref[0, rows, :].astype(F32))

    ri = lax.broadcasted_iota(jnp.int32, (w, w), 0)
    ci = lax.broadcasted_iota(jnp.int32, (w, w), 1)
    same = (ri // c) == (ci // c)
    eye = jnp.where(ri == ci, 1.0, 0.0).astype(F32)
    tl_ = lax.broadcasted_iota(jnp.int32, (c, w), 0)
    jl_ = lax.broadcasted_iota(jnp.int32, (c, w), 1) % c
    strict = (jl_ - tl_) * sign < 0
    incl = (jl_ - tl_) * sign <= 0
    eye2 = jnp.where(jl_ == tl_, 1.0, 0.0).astype(F32)
    lane = lax.broadcasted_iota(jnp.int32, (1, w), 1)
    h0 = lane < RWKV_HEAD

    def rows2(x):
        return jnp.concatenate([jnp.where(h0, x, 0.0), jnp.where(h0, 0.0, x)], axis=0)

    npair = st_ref.shape[0]
    items = [(s, slice(p * w, (p + 1) * w)) for s in range(nsub) for p in range(npair)]
    n = range(len(items))
    em = [e_mid[s][:, sl] for s, sl in items]
    g = [_dot_nt(jnp.concatenate([kt[s][:, sl], rt[s][:, sl]], axis=0),
                 jnp.concatenate([rows2(kh[s][:, sl]), rows2(bh[s][:, sl])], axis=0)) for s, sl in items]
    a_kk = [jnp.where(strict, x[:c, :w], 0.0) for x in g]
    a_rk = [jnp.where(incl, x[c:, :w], 0.0) for x in g]
    a_rb = [jnp.where(incl, x[c:, w:], 0.0) for x in g]
    vi = [v32[s][:, sl] for s, sl in items]
    v_rows = [rows2(x) for x in vi]
    r_pre = [_dot(a_kk[i], v_rows[i]) for i in n]
    m = [jnp.where(strict, -x[:c, w:], 0.0) for x in g]
    tinv = [eye2 + x for x in m]
    m = [_dot(x, rows2(x)) for x in m]
    for _ in range(c.bit_length() - 3):
        both = [_dot(jnp.concatenate([tinv[i], m[i]], axis=0), rows2(m[i])) for i in n]
        tinv = [tinv[i] + both[i][:c] for i in n]
        m = [x[c:] for x in both]
    tinv = [tinv[i] + _dot(tinv[i], rows2(m[i])) for i in n]
    sol = [_dot(tinv[i], jnp.concatenate([rows2(r_pre[i]), rows2(kt[s][:, sl] * em[i])], axis=1))
           for i, (s, sl) in enumerate(items)]
    u_pre = [x[:, :w] for x in sol]
    kq = [x[:, w:] for x in sol]
    y_pre = [_dot(jnp.concatenate([a_rk[i], -a_rb[i]], axis=1),
                  jnp.concatenate([v_rows[i], rows2(u_pre[i])], axis=0)) for i in n]
    r_eff = [rt[s][:, sl] * em[i] - _dot(a_rb[i], rows2(kq[i])) for i, (s, sl) in enumerate(items)]
    bbar = [bh[s][:, sl] * em[i] for i, (s, sl) in enumerate(items)]
    kbar = [kh[s][:, sl] * em[i] for i, (s, sl) in enumerate(items)]
    mmat = [eye * (em[i] * em[i]) - jnp.where(same, _dot_tn(kq[i], bbar[i]), 0.0) for i in n]
    s_pre = [jnp.where(same, _dot_tn(jnp.concatenate([vi[i], -u_pre[i]], axis=0),
                                     jnp.concatenate([kbar[i], bbar[i]], axis=0)), 0.0) for i in n]
    st = [st_ref[p] for p in range(npair)]
    for i, (s, sl) in enumerate(items):
        p = i % npair
        y_ref[0, 0, subs[s], sl] = _dot_nt(r_eff[i], st[p]) + y_pre[i]
        hi = st[p].astype(BF16)
        lo = (st[p] - hi.astype(F32)).astype(BF16)
        mb = mmat[i].astype(BF16)
        st[p] = (jnp.dot(hi, mb, preferred_element_type=F32) + jnp.dot(lo, mb, preferred_element_type=F32)
                 + s_pre[i])
    for p in range(npair):
        st_ref[p] = st[p]


def _wkv(r, v, kk, km, bv, lw, lc):
    b, l, d = r.shape
    c = WKV_CHUNK * WKV_CHUNKS_PER_STEP
    ncc = lc // c
    nlc = (l - lc) // c

    def chunk(dr, i):
        return jnp.where(dr == 0, i, jnp.where(i < ncc, ncc - 1 - i, nlc + 2 * ncc - 1 - i))

    shared = pl.BlockSpec((1, c, d), lambda dr, bi, i: (bi, chunk(dr, i), 0))
    per_dir = pl.BlockSpec((1, 1, c, d), lambda dr, bi, i: (dr, bi, chunk(dr, i), 0))
    return pl.pallas_call(
        _wkv_kernel,
        out_shape=jax.ShapeDtypeStruct((2, b, l, d), F32),
        grid=(2, b, l // c),
        in_specs=[shared, shared, shared, per_dir, per_dir, per_dir],
        out_specs=per_dir,
        scratch_shapes=[pltpu.VMEM((d // WKV_PAIR, WKV_PAIR, WKV_PAIR), F32)],
        compiler_params=pltpu.CompilerParams(dimension_semantics=("parallel", "parallel", "arbitrary"),
                                             vmem_limit_bytes=_vmem_limit(32)),
        name="wkv7_chunked",
    )(r, v, kk, km, bv, lw)


def _rwkv_out_kernel(y_ref, bonus_ref, g_ref, lnw_ref, lnb_ref, wo_ref, bd_ref, h_ref, mods_ref, gffn_ref,
                     wrt_ref, bias_ref, hn_ref, n2_ref, eid_ref, rank_ref, w_ref, cnt_ref, run_ref):
    y = y_ref[0, 0] + y_ref[1, 0]
    bd = bd_ref[...]
    mean = _head_sum(y, bd) * (1.0 / RWKV_HEAD)
    yc = y - mean
    var = _head_sum(yc * yc, bd) * (1.0 / RWKV_HEAD)
    yn = yc * lax.rsqrt(var + GN_EPS) * lnw_ref[...] + lnb_ref[...]
    out = (yn + bonus_ref[0]) * g_ref[0].astype(F32)
    _mixer_tail(_dot(out, wo_ref[...]), h_ref[0], mods_ref[0, 0], gffn_ref, wrt_ref, bias_ref, hn_ref, n2_ref,
                eid_ref, rank_ref, w_ref, cnt_ref, run_ref)


def _rwkv_out(y, bonus, g, lnw, lnb, wo, bd, h, mods, gffn, wrt, bias, nct):
    b, l, d = h.shape
    tl = TOKEN_TILE
    tok = lambda w: pl.BlockSpec((1, tl, w), lambda i, j: (i, j, 0))
    full = lambda x: pl.BlockSpec(x.shape, lambda i, j: (0,) * x.ndim)
    shapes, specs = _tail_outs(b, l, d)
    return pl.pallas_call(
        _rwkv_out_kernel,
        out_shape=shapes,
        grid=(b, l // tl),
        in_specs=[pl.BlockSpec((2, 1, tl, d), lambda i, j: (0, i, j, 0)), tok(d), tok(d),
                  full(lnw), full(lnb), full(wo), full(bd), tok(d),
                  pl.BlockSpec((1, 1, N_MODS, d), lambda i, j: (i, jnp.where(j < nct, 0, 1), 0, 0)),
                  full(gffn), full(wrt), full(bias)],
        out_specs=specs,
        scratch_shapes=[pltpu.VMEM((N_EXPERTS, 1), F32)],
        compiler_params=pltpu.CompilerParams(dimension_semantics=("arbitrary", "arbitrary"),
                                             vmem_limit_bytes=_vmem_limit(40)),
        name="rwkv_out",
    )(y, bonus, g, lnw, lnb, wo, bd, h, mods, gffn, wrt, bias)


def _rope_table(n_lat, n_ctx):
    dim = SWA_HEAD_DIM
    nf = dim // 4
    inv = ROPE_THETA ** (-jnp.arange(nf, dtype=F32) / nf)
    row = jnp.repeat(jnp.arange(n_lat // GRID_W, dtype=F32), GRID_W)
    col = jnp.tile(jnp.arange(GRID_W, dtype=F32), n_lat // GRID_W)
    ar = row[:, None] * inv
    ac = col[:, None] * inv
    ang = jnp.concatenate([ar, ar, ac, ac], axis=-1)
    cos = jnp.concatenate([jnp.ones((n_ctx, dim), F32), jnp.cos(ang)], axis=0)
    sin = jnp.concatenate([jnp.zeros((n_ctx, dim), F32), jnp.sin(ang)], axis=0)
    return jnp.tile(cos, (1, 2)), jnp.tile(sin, (1, 2))


def _layout_attn_weights(w_in, w_uq, w_ukv):
    d = w_in.shape[0]
    s0 = MLA_Q_RANK
    s1 = s0 + MLA_KV_RANK
    s2 = s1 + MLA_ROPE
    s3 = s2 + SWA_HEADS * SWA_HEAD_DIM
    s4 = s3 + SWA_KV_HEADS * SWA_HEAD_DIM
    rep = lambda w: jnp.concatenate(
        [jnp.tile(w[:, g * SWA_HEAD_DIM:(g + 1) * SWA_HEAD_DIM], (1, SWA_GROUP)) for g in range(SWA_KV_HEADS)], axis=1)
    win = jnp.concatenate([w_in[:, :s1], w_in[:, s2:s3], rep(w_in[:, s3:s4]), rep(w_in[:, s4:]),
                           w_in[:, s1:s2], jnp.zeros((d, V7X_LANES - MLA_ROPE), w_in.dtype)], axis=1)
    qh = MLA_NOPE + MLA_ROPE
    pad = jnp.zeros((w_uq.shape[0], V7X_MXU_DIM - qh), w_uq.dtype)
    wuq = jnp.concatenate([jnp.concatenate([w_uq[:, h * qh:(h + 1) * qh], pad], axis=1) for h in range(MLA_HEADS)], axis=1)
    kvh = MLA_NOPE + MLA_V
    wuk = jnp.concatenate([w_ukv[:, h * kvh:h * kvh + MLA_NOPE] for h in range(MLA_HEADS)], axis=1)
    wuvt = jnp.concatenate([w_ukv[:, h * kvh + MLA_NOPE:(h + 1) * kvh] for h in range(MLA_HEADS)], axis=1).T
    return win.astype(BF16), wuq.astype(BF16), wuk.astype(BF16), wuvt.astype(BF16)


def _lora_pair(w_down, w_up):
    rank = w_down.shape[2]
    down = jnp.concatenate([w_down[0], w_down[1]], axis=1)
    z = jnp.zeros((rank, w_up.shape[2]), w_up.dtype)
    up = jnp.stack([jnp.concatenate([w_up[0], z], axis=0), jnp.concatenate([z, w_up[1]], axis=0)], axis=0)
    return down.astype(BF16), up.astype(BF16)


def _head_block_diag():
    i = jnp.arange(V7X_MXU_DIM) // RWKV_HEAD
    return (i[:, None] == i[None, :]).astype(BF16)


def kernel(x, c, ctx, c_ctx, ada_w, ada_b, norm_mix, norm_ffn, norm_final, attn_w_in, attn_q_norm, attn_kv_norm, attn_w_uq, attn_w_ukv, attn_sinks, attn_w_o, rwkv_mu, rwkv_w_r, rwkv_w_k, rwkv_w_v, rwkv_w_o, rwkv_g1, rwkv_g2, rwkv_w0, rwkv_w1, rwkv_w2, rwkv_a0, rwkv_a1, rwkv_a2, rwkv_k_k, rwkv_k_a, rwkv_r_k, rwkv_ln_w, rwkv_ln_b, moe_router, moe_bias, moe_w_gate, moe_w_up, moe_w_down, moe_ws_gate, moe_ws_up, moe_ws_down):
    bsz, s, d = x.shape
    lc = ctx.shape[1]
    l = lc + s
    depth = ada_w.shape[0]
    nct = lc // TOKEN_TILE
    assert lc % TOKEN_TILE == 0 and s % TOKEN_TILE == 0 and s >= SWA_BAND
    assert lc % (WKV_CHUNK * WKV_CHUNKS_PER_STEP) == 0
    assert d % V7X_MXU_DIM == 0 and WKV_CHUNK * 2 == V7X_LANES
    ngrp = SAMPLE_GROUPS
    bg = bsz // ngrp
    assert bsz % ngrp == 0 and (bg * l) % (8 * V7X_SC_WORKERS) == 0

    streams = [(ctx, x, g * bg, 0) for g in range(ngrp)]
    out = None
    cos, sin = _rope_table(s, lc)
    bd = _head_block_diag()
    rows = -(-(bsz + 1) // 8) * 8
    cc = jnp.concatenate([c, c_ctx[None, :], jnp.zeros((rows - bsz - 1, d), F32)], axis=0)
    row2 = lambda a: a.reshape(1, -1)

    for li in range(depth):
        with_ctx = li < depth - 1
        i = li // 2
        ada = _ada_mods(cc, ada_w, ada_b, li)
        mods_all = jnp.stack([jnp.broadcast_to(ada[bsz].reshape(1, N_MODS, d), (bsz, N_MODS, d)),
                              ada[:bsz].reshape(bsz, N_MODS, d)], axis=1)
        wrt = jnp.concatenate([moe_router[li].T, jnp.zeros((GATE_W - N_EXPERTS, d), F32)], axis=0)
        bias = moe_bias[li].reshape(N_GROUPS, GROUP_SIZE, 1)
        if li % 2 == 0:
            win, wuq, wuk, wuvt = _layout_attn_weights(attn_w_in[i], attn_w_uq[i], attn_w_ukv[i])
            wo = attn_w_o[i].astype(BF16)
        else:
            w1, w2 = _lora_pair(rwkv_w1[i], rwkv_w2[i])
            a1, a2 = _lora_pair(rwkv_a1[i], rwkv_a2[i])
            wr, wk, wv, wo = [w[i].astype(BF16) for w in (rwkv_w_r, rwkv_w_k, rwkv_w_v, rwkv_w_o)]
            g1, g2 = rwkv_g1[i].astype(BF16), rwkv_g2[i].astype(BF16)
        for g in range(ngrp):
            mods = mods_all[g * bg:(g + 1) * bg]
            if li % 2 == 0:
                q, k, vt, qs, ks, vs = _attn_proj(streams[g], bg, l, mods, row2(norm_mix[li]), win,
                                                  row2(attn_q_norm[i]), row2(attn_kv_norm[i]), wuq, wuk, wuvt,
                                                  cos, sin, nct)
                a = _mla_attention(q, k, vt, lc, 0 if with_ctx else lc // MLA_Q_TILE)
                bm = _swa_attention(attn_sinks[i], qs, ks, vs, lc, 0 if with_ctx else lc // SWA_Q_TILE)
                tail = _attn_out(a, bm, streams[g], mods, wo, row2(norm_ffn[li]), wrt, bias, nct)
            else:
                h = streams[g][0]
                assert streams[g][0] is streams[g][1]
                r, v, kk, gt, km, bv, lw, bonus = _rwkv_proj(
                    h, mods, row2(norm_mix[li]), rwkv_mu[i], wr, wk, wv, g1, g2, w1, w2, a1, a2,
                    rwkv_w0[i], rwkv_a0[i], row2(rwkv_k_k[i]), row2(rwkv_k_a[i]), row2(rwkv_r_k[i]), bd, nct)
                y = _wkv(r, v, kk, km, bv, lw, lc)
                tail = _rwkv_out(y, bonus, gt, row2(rwkv_ln_w[i]), row2(rwkv_ln_b[i]), wo,
                                 bd, h, mods, row2(norm_ffn[li]), wrt, bias, nct)
            h, n2p, eid, rank, wcols, counts = tail
            moe_w = (moe_w_gate, moe_w_up, moe_w_down, moe_ws_gate, moe_ws_up, moe_ws_down)
            if li < depth - 1:
                h = _moe_sparse(n2p, eid, rank, wcols, counts, h, mods, *moe_w, row2(norm_final), nct, li)
                streams[g] = (h, h, 0, nct)
            else:
                out = _moe_sparse(n2p, eid, rank, wcols, counts, h, mods, *moe_w, row2(norm_final), nct, li,
                                  out_buf=out, out_b0=g * bg, out_batch=bsz, last=True)
    return out
```

```python
import functools

import jax
import jax.numpy as jnp
from jax import lax
from jax.experimental import pallas as pl
from jax.experimental.pallas import tpu as pltpu
from jax.experimental.pallas import tpu_sc as plsc

F32 = jnp.float32
BF16 = jnp.bfloat16
HIGHEST = lax.Precision.HIGHEST

GRID_W = 64
NORM_EPS = 1e-6
ROPE_THETA = 10000.0
NEG_INF = -1e30
N_MODS = 6

MLA_HEADS = 4
MLA_Q_RANK = 384
MLA_KV_RANK = 256
MLA_NOPE = 128
MLA_ROPE = 64
MLA_V = 128

SWA_HEADS = 8
SWA_KV_HEADS = 2
SWA_GROUP = SWA_HEADS // SWA_KV_HEADS
SWA_HEAD_DIM = 64
WINDOW = 128

RWKV_HEAD = 64
DECAY_LORA = 64
ICLR_LORA = 64
GATE_LORA = 128
GN_EPS = 64e-5

N_EXPERTS = 64
TOP_K = 6
N_GROUPS = 8
TOPK_GROUPS = 4
GROUP_SIZE = N_EXPERTS // N_GROUPS
ROUTED_SCALE = 2.5
GATE_W = 128

V7X_LANES = 128
V7X_MXU_DIM = 256
V7X_VMEM_BYTES = 64 * 1024 * 1024
V7X_SC_CORES = 2
V7X_SC_SUBCORES = 16
V7X_SC_WORKERS = V7X_SC_CORES * V7X_SC_SUBCORES

TOKEN_TILE = 256
MLA_Q_TILE = 256
MLA_HEADS_PER_STEP = 2
LOG2E = 1.4426950408889634
SWA_Q_TILE = 128
SWA_BAND = SWA_Q_TILE + 2 * WINDOW
WKV_CHUNK = 64
WKV_PAIR = 2 * RWKV_HEAD
WKV_CHUNKS_PER_STEP = 4
MOE_ROW_TILE = 512
SAMPLE_GROUPS = 2
SC_MAX_CHUNK = 64


def _vmem_limit(mib):
    return min(mib * 1024 * 1024, V7X_VMEM_BYTES - 4 * 1024 * 1024)


def _dot(a, b):
    return jnp.dot(a.astype(BF16), b.astype(BF16), preferred_element_type=F32)


def _dot_nt(a, b):
    return lax.dot_general(a.astype(BF16), b.astype(BF16), (((1,), (1,)), ((), ())),
                           preferred_element_type=F32)


def _dot_tn(a, b):
    return lax.dot_general(a.astype(BF16), b.astype(BF16), (((0,), (0,)), ((), ())),
                           preferred_element_type=F32)


def _sigmoid(x):
    return 1.0 / (1.0 + jnp.exp(-x))


def _silu(x):
    return x * _sigmoid(x)


def _rms(x, g):
    return x * lax.rsqrt(jnp.mean(x * x, axis=-1, keepdims=True) + NORM_EPS) * g


def _norm_mod(x, g, shift, scale):
    return _rms(x, g) * (1.0 + scale) + shift


def _split_dot(x, w):
    hi = x.astype(BF16)
    lo = (x - hi.astype(F32)).astype(BF16)
    return (jnp.dot(hi, w, preferred_element_type=F32) + jnp.dot(lo, w, preferred_element_type=F32))


def _head_sum(x, bd):
    w = bd.shape[0]
    parts = [_split_dot(x[:, c * w:(c + 1) * w], bd) for c in range(x.shape[1] // w)]
    return jnp.concatenate(parts, axis=1)


def _ada_kernel(c_ref, w_ref, b_ref, o_ref):
    s = _silu(c_ref[...])
    o_ref[...] = jnp.dot(s, w_ref[0], precision=HIGHEST, preferred_element_type=F32) + b_ref[0]


def _ada_mods(cc, w, b, layer):
    rows, d = cc.shape
    depth, _, n = w.shape
    return pl.pallas_call(
        _ada_kernel,
        out_shape=jax.ShapeDtypeStruct((rows, n), F32),
        grid=(n // d,),
        in_specs=[pl.BlockSpec((rows, d), lambda i: (0, 0)),
                  pl.BlockSpec((1, d, d), lambda i: (layer, 0, i)),
                  pl.BlockSpec((1, 1, d), lambda i: (layer, 0, i))],
        out_specs=pl.BlockSpec((rows, d), lambda i: (0, i)),
        compiler_params=pltpu.CompilerParams(dimension_semantics=("parallel",),
                                             vmem_limit_bytes=_vmem_limit(32)),
        name="ada_mods",
    )(cc, w, b.reshape(depth, 1, n))


def _rope128(x, cos, sin, first_half):
    rot = jnp.where(first_half, -pltpu.roll(x, V7X_LANES - 16, axis=1), pltpu.roll(x, 16, axis=1))
    return x * cos + rot * sin


_C_CQ = 0
_C_CKV = _C_CQ + MLA_Q_RANK
_C_QS = _C_CKV + MLA_KV_RANK
_C_KS = _C_QS + SWA_HEADS * SWA_HEAD_DIM
_C_VS = _C_KS + SWA_KV_HEADS * V7X_MXU_DIM
_C_KR = _C_VS + SWA_KV_HEADS * V7X_MXU_DIM
_C_END = _C_KR + V7X_LANES
_SWA_W = SWA_KV_HEADS * V7X_MXU_DIM
_MLA_QK_W = MLA_HEADS * V7X_MXU_DIM


def _stream_specs(stream, nct, tl):
    ctx_arr, lat_arr, b0, lat_off = stream
    d = ctx_arr.shape[2]
    return [pl.BlockSpec((1, tl, d), lambda i, j: (i + b0, jnp.minimum(j, nct - 1), 0)),
            pl.BlockSpec((1, tl, d), lambda i, j: (i + b0, jnp.maximum(j - nct, 0) + lat_off, 0))]


def _stream_tile(c_ref, x_ref, nct):
    rows = c_ref.shape[1]
    take_ctx = lax.broadcasted_iota(jnp.int32, (rows, 1), 0) < jnp.where(pl.program_id(1) < nct, rows, 0)
    return jnp.where(take_ctx, c_ref[0], x_ref[0])


def _attn_proj_kernel(c_ref, x_ref, mods_ref, g_ref, win_ref, qn_ref, kvn_ref, wuq_ref, wuk_ref, wuvt_ref, cos_ref,
                      sin_ref, q_ref, k_ref, vt_ref, qs_ref, ks_ref, vs_ref, *, nct):
    m = mods_ref[0, 0]
    n = _norm_mod(_stream_tile(c_ref, x_ref, nct), g_ref[...], m[0:1], m[1:2])
    u = _dot(n, win_ref[...])
    cos = cos_ref[...]
    sin = sin_ref[...]
    lane = lax.broadcasted_iota(jnp.int32, (1, V7X_LANES), 1)
    first_half = (lane % 32) < 16

    def rope(x):
        return _rope128(x, cos, sin, first_half)

    scale_a = (MLA_NOPE + MLA_ROPE) ** -0.5 * LOG2E
    scale_b = SWA_HEAD_DIM ** -0.5 * LOG2E
    q = _dot(_rms(u[:, _C_CQ:_C_CKV], qn_ref[...]), wuq_ref[...])
    ckv = _rms(u[:, _C_CKV:_C_QS], kvn_ref[...])
    kn = _dot(ckv, wuk_ref[...])
    vt_ref[0] = _dot_nt(wuvt_ref[...], ckv).astype(BF16)
    kr = rope(u[:, _C_KR:_C_END]).astype(BF16)
    for h in range(MLA_HEADS):
        o = h * V7X_MXU_DIM
        q_ref[0, :, o:o + V7X_LANES] = (q[:, o:o + V7X_LANES] * scale_a).astype(BF16)
        q_ref[0, :, o + V7X_LANES:o + V7X_MXU_DIM] = (rope(q[:, o + V7X_LANES:o + V7X_MXU_DIM]) * scale_a).astype(BF16)
        k_ref[0, :, o:o + V7X_LANES] = kn[:, h * MLA_NOPE:(h + 1) * MLA_NOPE].astype(BF16)
        k_ref[0, :, o + V7X_LANES:o + V7X_MXU_DIM] = kr
    for c in range((_C_KS - _C_QS) // V7X_LANES):
        o = c * V7X_LANES
        qs_ref[0, :, o:o + V7X_LANES] = (rope(u[:, _C_QS + o:_C_QS + o + V7X_LANES]) * scale_b).astype(BF16)
    for c in range(_SWA_W // V7X_LANES):
        o = c * V7X_LANES
        ks_ref[0, :, o:o + V7X_LANES] = rope(u[:, _C_KS + o:_C_KS + o + V7X_LANES]).astype(BF16)
    vs_ref[0] = u[:, _C_VS:_C_KR].astype(BF16)


def _attn_proj(stream, b, l, mods, g, win, qn, kvn, wuq, wuk, wuvt, cos, sin, nct):
    d = stream[0].shape[2]
    tl = TOKEN_TILE
    tok = lambda w: pl.BlockSpec((1, tl, w), lambda i, j: (i, j, 0))
    full = lambda a: pl.BlockSpec(a.shape, lambda i, j: (0,) * a.ndim)
    sds = jax.ShapeDtypeStruct
    dv = MLA_HEADS * MLA_V
    return pl.pallas_call(
        functools.partial(_attn_proj_kernel, nct=nct),
        out_shape=[sds((b, l, _MLA_QK_W), BF16), sds((b, l, _MLA_QK_W), BF16), sds((b, dv, l), BF16),
                   sds((b, l, SWA_HEADS * SWA_HEAD_DIM), BF16), sds((b, l, _SWA_W), BF16), sds((b, l, _SWA_W), BF16)],
        grid=(b, l // tl),
        in_specs=_stream_specs(stream, nct, tl) + [
                  pl.BlockSpec((1, 1, N_MODS, d), lambda i, j: (i, jnp.where(j < nct, 0, 1), 0, 0)),
                  full(g), full(win), full(qn), full(kvn), full(wuq), full(wuk), full(wuvt),
                  pl.BlockSpec((tl, V7X_LANES), lambda i, j: (j, 0)),
                  pl.BlockSpec((tl, V7X_LANES), lambda i, j: (j, 0))],
        out_specs=[tok(_MLA_QK_W), tok(_MLA_QK_W), pl.BlockSpec((1, dv, tl), lambda i, j: (i, 0, j)),
                   tok(SWA_HEADS * SWA_HEAD_DIM), tok(_SWA_W), tok(_SWA_W)],
        compiler_params=pltpu.CompilerParams(dimension_semantics=("parallel", "parallel"),
                                             vmem_limit_bytes=_vmem_limit(48)),
        name="attn_proj",
    )(stream[0], stream[1], mods, g, win, qn, kvn, wuq, wuk, wuvt, cos, sin)


def _mla_kernel(q_ref, k_ref, vt_ref, o_ref, *, nct_q, lc):
    hw = V7X_MXU_DIM

    def attend(nk):
        st = [_dot_nt(k_ref[0, 0:nk, hh * hw:(hh + 1) * hw], q_ref[0, :, hh * hw:(hh + 1) * hw])
              for hh in range(MLA_HEADS_PER_STEP)]
        for hh, s in enumerate(st):
            p = jnp.exp2(s - jnp.max(s, axis=0, keepdims=True))
            den = jnp.sum(p, axis=0, keepdims=True)
            ot = _dot(vt_ref[0, hh * MLA_V:(hh + 1) * MLA_V, 0:nk], p) / den
            o_ref[0, :, hh * MLA_V:(hh + 1) * MLA_V] = ot.T.astype(o_ref.dtype)

    @pl.when(pl.program_id(2) < nct_q)
    def _():
        attend(lc)

    @pl.when(pl.program_id(2) >= nct_q)
    def _():
        attend(k_ref.shape[1])


def _mla_attention(q, k, vt, lc, q_tile0):
    b, l, _ = q.shape
    tq = MLA_Q_TILE
    hps = MLA_HEADS_PER_STEP
    return pl.pallas_call(
        functools.partial(_mla_kernel, nct_q=lc // tq - q_tile0, lc=lc),
        out_shape=jax.ShapeDtypeStruct((b, l, MLA_HEADS * MLA_V), BF16),
        grid=(b, MLA_HEADS // hps, l // tq - q_tile0),
        in_specs=[pl.BlockSpec((1, tq, hps * V7X_MXU_DIM), lambda i, h, j: (i, j + q_tile0, h)),
                  pl.BlockSpec((1, l, hps * V7X_MXU_DIM), lambda i, h, j: (i, 0, h)),
                  pl.BlockSpec((1, hps * MLA_V, l), lambda i, h, j: (i, h, 0))],
        out_specs=pl.BlockSpec((1, tq, hps * MLA_V), lambda i, h, j: (i, j + q_tile0, h)),
        compiler_params=pltpu.CompilerParams(dimension_semantics=("parallel", "parallel", "parallel"),
                                             vmem_limit_bytes=_vmem_limit(48)),
        name="mla_attention",
    )(q, k, vt)


def _swa_kernel(sink_ref, q_ref, k_ref, v_ref, o_ref, *, lc, q_tile0):
    tq = SWA_Q_TILE
    l = k_ref.shape[1]
    r0 = (pl.program_id(1) + q_tile0) * tq
    start = pl.multiple_of(jnp.clip(r0 - WINDOW, lc, l - SWA_BAND), tq)
    rows = SWA_GROUP * tq
    row = lax.broadcasted_iota(jnp.int32, (rows, 1), 0)
    qpos = jnp.where(r0 >= lc, r0, -l) + row % tq
    kpos = start + lax.broadcasted_iota(jnp.int32, (1, SWA_BAND), 1)
    valid = jnp.abs(qpos - kpos) <= WINDOW
    lane = lax.broadcasted_iota(jnp.int32, (1, V7X_MXU_DIM), 1)
    head = [(lane // SWA_HEAD_DIM) == hh for hh in range(SWA_GROUP)]
    groups = range(SWA_KV_HEADS)
    sls = [slice(g * V7X_MXU_DIM, (g + 1) * V7X_MXU_DIM) for g in groups]
    qstack = []
    for sl in sls:
        qg = q_ref[0, :, sl]
        zero = jnp.zeros_like(qg)
        qstack.append(jnp.concatenate([jnp.where(head[hh], qg, zero) for hh in range(SWA_GROUP)], axis=0))
    sc = [_dot_nt(qstack[g], k_ref[0, 0:lc, sls[g]]) for g in groups]
    sb = [_dot_nt(qstack[g], k_ref[0, pl.ds(start, SWA_BAND), sls[g]]) for g in groups]
    for g in groups:
        sl = sls[g]
        sbm = jnp.where(valid, sb[g], NEG_INF)
        sk = jnp.zeros((rows, 1), F32)
        for hh in range(SWA_GROUP):
            sk = jnp.where(row // tq == hh, sink_ref[g * SWA_GROUP + hh] * LOG2E, sk)
        mx = jnp.maximum(jnp.maximum(jnp.max(sc[g], axis=-1, keepdims=True), jnp.max(sbm, axis=-1, keepdims=True)), sk)
        pc = jnp.exp2(sc[g] - mx)
        pb = jnp.exp2(sbm - mx)
        den = jnp.sum(pc, axis=-1, keepdims=True) + jnp.sum(pb, axis=-1, keepdims=True) + jnp.exp2(sk - mx)
        ostack = (_dot(pc, v_ref[0, 0:lc, sl]) + _dot(pb, v_ref[0, pl.ds(start, SWA_BAND), sl])) / den
        o = jnp.zeros((tq, V7X_MXU_DIM), F32)
        for hh in range(SWA_GROUP):
            o = o + jnp.where(head[hh], ostack[hh * tq:(hh + 1) * tq], 0.0)
        o_ref[0, :, sl] = o.astype(o_ref.dtype)


def _swa_attention(sinks, q, k, v, lc, q_tile0):
    b, l, _ = q.shape
    tq = SWA_Q_TILE
    return pl.pallas_call(
        functools.partial(_swa_kernel, lc=lc, q_tile0=q_tile0),
        out_shape=jax.ShapeDtypeStruct((b, l, SWA_HEADS * SWA_HEAD_DIM), BF16),
        grid=(b, l // tq - q_tile0),
        in_specs=[pl.BlockSpec(memory_space=pltpu.SMEM),
                  pl.BlockSpec((1, tq, SWA_HEADS * SWA_HEAD_DIM), lambda i, j: (i, j + q_tile0, 0)),
                  pl.BlockSpec((1, l, _SWA_W), lambda i, j: (i, 0, 0)),
                  pl.BlockSpec((1, l, _SWA_W), lambda i, j: (i, 0, 0))],
        out_specs=pl.BlockSpec((1, tq, SWA_HEADS * SWA_HEAD_DIM), lambda i, j: (i, j + q_tile0, 0)),
        compiler_params=pltpu.CompilerParams(dimension_semantics=("parallel", "parallel"),
                                             vmem_limit_bytes=_vmem_limit(48)),
        name="swa_attention",
    )(sinks, q, k, v)


def _pack_bf16_pair(x):
    w = x.shape[1] // 2
    lo = pltpu.bitcast(x[:, :w].astype(BF16).astype(F32), jnp.int32)
    hi = pltpu.bitcast(x[:, w:].astype(BF16).astype(F32), jnp.int32)
    return lax.shift_right_logical(lo, jnp.int32(16)) | (hi & jnp.int32(-65536))


def _unpack_bf16_pair(p):
    return pltpu.bitcast(p << 16, F32), pltpu.bitcast(p & jnp.int32(-65536), F32)


def _route(n2, wrt, bias, run_ref):
    n_hi = n2.astype(BF16)
    n_lo = (n2 - n_hi.astype(F32)).astype(BF16)
    w_hi = wrt.astype(BF16)
    w_lo = (wrt - w_hi.astype(F32)).astype(BF16)
    logits = _dot_nt(w_hi, n_hi) + (_dot_nt(w_hi, n_lo) + _dot_nt(w_lo, n_hi))
    rows = logits.shape[1]
    scores = _sigmoid(logits[0:N_EXPERTS])

    def select(sc2):
        cols = sc2.shape[1]
        shape3 = (N_GROUPS, GROUP_SIZE, cols)
        choice = sc2.reshape(shape3) + bias
        ji = lax.broadcasted_iota(jnp.int32, shape3, 1).astype(F32)
        m1 = jnp.max(choice, axis=1, keepdims=True)
        first = jnp.min(jnp.where(choice == m1, ji, float(GROUP_SIZE)), axis=1, keepdims=True)
        m2 = jnp.max(jnp.where(ji == first, -jnp.inf, choice), axis=1, keepdims=True)
        gs = m1 + m2
        gidx = lax.broadcasted_iota(jnp.int32, gs.shape, 0).astype(F32)
        gsel = jnp.zeros_like(gs)
        for _ in range(TOPK_GROUPS):
            mx = jnp.max(gs, axis=0, keepdims=True)
            pick = gidx == jnp.min(jnp.where(gs == mx, gidx, float(N_GROUPS)), axis=0, keepdims=True)
            gsel = jnp.where(pick, 1.0, gsel)
            gs = jnp.where(pick, -jnp.inf, gs)
        cand = jnp.where(gsel > 0.0, choice, -jnp.inf).reshape(N_EXPERTS, cols)
        eidx = lax.broadcasted_iota(jnp.int32, (N_EXPERTS, cols), 0).astype(F32)
        out = []
        for _ in range(TOP_K):
            mx = jnp.max(cand, axis=0, keepdims=True)
            pick = eidx == jnp.min(jnp.where(cand == mx, eidx, float(N_EXPERTS)), axis=0, keepdims=True)
            out.append(jnp.where(pick, 1.0, 0.0))
            cand = jnp.where(pick, -jnp.inf, cand)
        return out

    blocks = [select(scores[:, o:o + V7X_LANES]) for o in range(0, rows, V7X_LANES)]
    picks = [jnp.concatenate([blk[k] for blk in blocks], axis=1) > 0.0 for k in range(TOP_K)]
    ei = lax.broadcasted_iota(jnp.int32, (N_EXPERTS, rows), 0).astype(F32)
    esel = jnp.zeros((N_EXPERTS, rows), F32)
    for pick in picks:
        esel = jnp.where(pick, 1.0, esel)
    before = jnp.where(lax.broadcasted_iota(jnp.int32, (rows, rows), 0) < lax.broadcasted_iota(jnp.int32, (rows, rows), 1),
                       1.0, 0.0).astype(BF16)
    slot = jnp.dot(esel.astype(BF16), before, preferred_element_type=F32) + run_ref[...]
    run_ref[...] += jnp.sum(esel, axis=1, keepdims=True)
    sc = [jnp.sum(jnp.where(pick, scores, 0.0), axis=0, keepdims=True) for pick in picks]
    tot = sc[0]
    for x in sc[1:]:
        tot = tot + x
    k8 = lax.broadcasted_iota(jnp.int32, (8, rows), 0)
    kw = lax.broadcasted_iota(jnp.int32, (GATE_W, rows), 0)
    eid = jnp.zeros((8, rows), jnp.int32)
    rank = jnp.zeros((8, rows), jnp.int32)
    wk = jnp.zeros((GATE_W, rows), F32)
    for k, pick in enumerate(picks):
        e_k = jnp.sum(jnp.where(pick, ei, 0.0), axis=0, keepdims=True).astype(jnp.int32)
        r_k = jnp.sum(jnp.where(pick, slot, 0.0), axis=0, keepdims=True).astype(jnp.int32)
        eid = jnp.where(k8 == k, e_k, eid)
        rank = jnp.where(k8 == k, r_k, rank)
        wk = jnp.where(kw == k, sc[k] * (ROUTED_SCALE / tot), wk)
    return eid, rank, wk.T


def _mixer_tail(o, h, m, gffn_ref, wrt_ref, bias_ref, hn_ref, n2_ref, eid_ref, rank_ref, w_ref, cnt_ref, run_ref):
    @pl.when((pl.program_id(0) == 0) & (pl.program_id(1) == 0))
    def _():
        run_ref[...] = jnp.zeros_like(run_ref)

    hn = h + m[2:3] * o
    hn_ref[0] = hn
    n2 = _norm_mod(hn, gffn_ref[...], m[3:4], m[4:5])
    n2_ref[0] = _pack_bf16_pair(n2)
    eid, rank, wcols = _route(n2, wrt_ref[...], bias_ref[...], run_ref)
    eid_ref[0] = eid
    rank_ref[0] = rank
    w_ref[0] = wcols
    cnt_ref[...] = run_ref[...]


def _attn_out_kernel(a_ref, b_ref, c_ref, x_ref, mods_ref, wo_ref, gffn_ref, wrt_ref, bias_ref,
                     hn_ref, n2_ref, eid_ref, rank_ref, w_ref, cnt_ref, run_ref, *, nct):
    wa = MLA_HEADS * MLA_V
    o = _dot(a_ref[0], wo_ref[0:wa, :]) + _dot(b_ref[0], wo_ref[wa:, :])
    _mixer_tail(o, _stream_tile(c_ref, x_ref, nct), mods_ref[0, 0], gffn_ref, wrt_ref, bias_ref, hn_ref, n2_ref,
                eid_ref, rank_ref, w_ref, cnt_ref, run_ref)


def _tail_outs(b, l, d):
    tl = TOKEN_TILE
    nt = l // tl
    sds = jax.ShapeDtypeStruct
    tok = lambda w: pl.BlockSpec((1, tl, w), lambda i, j: (i, j, 0))
    blk = pl.BlockSpec((1, 8, tl), lambda i, j: (i * nt + j, 0, 0))
    shapes = [sds((b, l, d), F32), sds((b, l, d // 2), jnp.int32), sds((b * nt, 8, tl), jnp.int32),
              sds((b * nt, 8, tl), jnp.int32), sds((b, l, GATE_W), F32), sds((N_EXPERTS, 1), F32)]
    specs = [tok(d), tok(d // 2), blk, blk, tok(GATE_W), pl.BlockSpec((N_EXPERTS, 1), lambda i, j: (0, 0))]
    return shapes, specs


def _attn_out(a, bm, stream, mods, wo, gffn, wrt, bias, nct):
    b, l, _ = a.shape
    d = stream[0].shape[2]
    tl = TOKEN_TILE
    tok = lambda w: pl.BlockSpec((1, tl, w), lambda i, j: (i, j, 0))
    full = lambda x: pl.BlockSpec(x.shape, lambda i, j: (0,) * x.ndim)
    shapes, specs = _tail_outs(b, l, d)
    return pl.pallas_call(
        functools.partial(_attn_out_kernel, nct=nct),
        out_shape=shapes,
        grid=(b, l // tl),
        in_specs=[tok(a.shape[2]), tok(bm.shape[2])] + _stream_specs(stream, nct, tl) + [
                  pl.BlockSpec((1, 1, N_MODS, d), lambda i, j: (i, jnp.where(j < nct, 0, 1), 0, 0)),
                  full(wo), full(gffn), full(wrt), full(bias)],
        out_specs=specs,
        scratch_shapes=[pltpu.VMEM((N_EXPERTS, 1), F32)],
        compiler_params=pltpu.CompilerParams(dimension_semantics=("arbitrary", "arbitrary"),
                                             vmem_limit_bytes=_vmem_limit(40)),
        name="attn_out",
    )(a, bm, stream[0], stream[1], mods, wo, gffn, wrt, bias)


def _moe_dest_kernel(off_ref, eid_ref, rank_ref, dest_ref):
    eid = eid_ref[...]
    dest = rank_ref[...]
    for e in range(N_EXPERTS):
        dest = dest + jnp.where(eid == e, off_ref[e], 0)
    dest_ref[...] = dest


def _moe_dest(off, eid, rank):
    return pl.pallas_call(
        _moe_dest_kernel,
        out_shape=jax.ShapeDtypeStruct(eid.shape, jnp.int32),
        in_specs=[pl.BlockSpec(memory_space=pltpu.SMEM),
                  pl.BlockSpec(eid.shape, lambda: (0, 0, 0)), pl.BlockSpec(eid.shape, lambda: (0, 0, 0))],
        out_specs=pl.BlockSpec(eid.shape, lambda: (0, 0, 0)),
        name="moe_dest",
    )(off, eid, rank)


def _sc_mesh():
    return plsc.VectorSubcoreMesh(core_axis_name="c", subcore_axis_name="s",
                                  num_cores=V7X_SC_CORES, num_subcores=V7X_SC_SUBCORES)


def _sc_chunk(rows_per_worker):
    return max(c for c in range(8, SC_MAX_CHUNK + 1, 8) if rows_per_worker % c == 0)


def _sc_dispatch(xp, dest, p_rows):
    t, w = xp.shape
    tpw = t // V7X_SC_WORKERS
    ch = _sc_chunk(tpw)

    @functools.partial(
        pl.kernel, mesh=_sc_mesh(), out_type=jax.ShapeDtypeStruct((p_rows, w), xp.dtype),
        scratch_types=[pltpu.VMEM((ch, w), xp.dtype)] + [pltpu.VMEM((ch,), jnp.int32)] * TOP_K
        + [pltpu.SemaphoreType.DMA, pltpu.SemaphoreType.DMA],
        name="moe_dispatch")
    def run(x_hbm, dest_hbm, out_hbm, rows_v, *rest):
        idx, (sem_i, sem_o) = rest[:TOP_K], rest[TOP_K:]
        base = (lax.axis_index("s") * V7X_SC_CORES + lax.axis_index("c")) * tpw

        @pl.loop(0, tpw // ch)
        def _(i):
            t0 = base + i * ch
            loads = [pltpu.async_copy(dest_hbm.at[k, pl.ds(t0, ch)], idx[k], sem_i) for k in range(TOP_K)]
            pltpu.sync_copy(x_hbm.at[pl.ds(t0, ch)], rows_v)
            for c in loads:
                c.wait()
            puts = [pltpu.async_copy(rows_v, out_hbm.at[idx[k]], sem_o) for k in range(TOP_K)]
            for c in puts:
                c.wait()

    return run(xp, dest)


def _sc_gather(ys, dest, t):
    w = ys.shape[1]
    tpw = t // V7X_SC_WORKERS
    ch = _sc_chunk(tpw)

    @functools.partial(
        pl.kernel, mesh=_sc_mesh(), out_type=jax.ShapeDtypeStruct((TOP_K, t, w), ys.dtype),
        scratch_types=[pltpu.VMEM((ch, w), ys.dtype)] * 2 + [pltpu.VMEM((ch,), jnp.int32)] * TOP_K
        + [pltpu.SemaphoreType.DMA] * 5,
        name="moe_gather")
    def run(y_hbm, dest_hbm, out_hbm, rows_a, rows_b, *rest):
        idx, (sem_i, sem_ga, sem_gb, sem_wa, sem_wb) = rest[:TOP_K], rest[TOP_K:]
        rows, sem_g, sem_w = (rows_a, rows_b), (sem_ga, sem_gb), (sem_wa, sem_wb)
        base = (lax.axis_index("s") * V7X_SC_CORES + lax.axis_index("c")) * tpw

        @pl.loop(0, tpw // ch)
        def _(i):
            t0 = base + i * ch
            loads = [pltpu.async_copy(dest_hbm.at[k, pl.ds(t0, ch)], idx[k], sem_i) for k in range(TOP_K)]
            for c in loads:
                c.wait()
            gets, puts = [None] * TOP_K, [None] * TOP_K
            gets[0] = pltpu.async_copy(y_hbm.at[idx[0]], rows[0], sem_g[0])
            for k in range(TOP_K):
                if k + 1 < TOP_K:
                    if k >= 1:
                        puts[k - 1].wait()
                    gets[k + 1] = pltpu.async_copy(y_hbm.at[idx[k + 1]], rows[(k + 1) % 2], sem_g[(k + 1) % 2])
                gets[k].wait()
                puts[k] = pltpu.async_copy(rows[k % 2], out_hbm.at[k, pl.ds(t0, ch)], sem_w[k % 2])
            puts[TOP_K - 2].wait()
            puts[TOP_K - 1].wait()

    return run(ys, dest)


def _cache_mlp_weights(wg, wu, wd, wgu_ref, wdb_ref):
    f = wg.shape[1]
    wgu_ref[:, 0:f] = wg.astype(BF16)
    wgu_ref[:, f:] = wu.astype(BF16)
    wdb_ref[...] = wd.astype(BF16)


def _gated_mlp(xp, wgu_ref, wdb_ref):
    lo, hi = _unpack_bf16_pair(xp)
    x = jnp.concatenate([lo.astype(BF16), hi.astype(BF16)], axis=1)
    gu = jnp.dot(x, wgu_ref[...], preferred_element_type=F32)
    f = gu.shape[1] // 2
    return _dot(_silu(gu[:, :f]) * gu[:, f:], wdb_ref[...])


def _moe_expert_kernel(te_ref, nv_ref, x_ref, wg_ref, wu_ref, wd_ref, y_ref, wgu_ref, wdb_ref):
    i = pl.program_id(0)

    @pl.when((i == 0) | (te_ref[i] != te_ref[jnp.maximum(i - 1, 0)]))
    def _():
        _cache_mlp_weights(wg_ref[0, 0], wu_ref[0, 0], wd_ref[0, 0], wgu_ref, wdb_ref)

    @pl.when(i < nv_ref[0])
    def _():
        y_ref[...] = _pack_bf16_pair(_gated_mlp(x_ref[...], wgu_ref, wdb_ref))


def _moe_experts(tile_expert, n_valid, xs, wg, wu, wd, layer):
    p_rows, w = xs.shape
    tm = MOE_ROW_TILE
    _, _, d, f = wg.shape
    wspec = lambda shp: pl.BlockSpec((1, 1) + shp, lambda i, te, nv: (layer, te[i], 0, 0))
    return pl.pallas_call(
        _moe_expert_kernel,
        out_shape=jax.ShapeDtypeStruct((p_rows, w), xs.dtype),
        grid_spec=pltpu.PrefetchScalarGridSpec(
            num_scalar_prefetch=2, grid=(p_rows // tm,),
            in_specs=[pl.BlockSpec((tm, w), lambda i, te, nv: (jnp.minimum(i, nv[0] - 1), 0)),
                      wspec((d, f)), wspec((d, f)), wspec((f, d))],
            out_specs=pl.BlockSpec((tm, w), lambda i, te, nv: (jnp.minimum(i, nv[0] - 1), 0)),
            scratch_shapes=[pltpu.VMEM((d, 2 * f), BF16), pltpu.VMEM((f, d), BF16)]),
        compiler_params=pltpu.CompilerParams(dimension_semantics=("arbitrary",),
                                             vmem_limit_bytes=_vmem_limit(32)),
        name="moe_experts",
    )(tile_expert, n_valid, xs, wg, wu, wd)


def _moe_combine_kernel(yg_ref, w_ref, xp_ref, sg_ref, su_ref, sd_ref, h_ref, mods_ref, gfin_ref, *rest, final_norm):
    o_ref, wgu_ref, wdb_ref = rest[-3:]

    @pl.when((pl.program_id(0) == 0) & (pl.program_id(1) == 0))
    def _():
        _cache_mlp_weights(sg_ref[0], su_ref[0], sd_ref[0], wgu_ref, wdb_ref)

    acc = _gated_mlp(xp_ref[0], wgu_ref, wdb_ref)
    half = acc.shape[1] // 2
    lo = acc[:, :half]
    hi = acc[:, half:]
    w = w_ref[0]
    for k in range(TOP_K):
        ylo, yhi = _unpack_bf16_pair(yg_ref[k, 0])
        wk = w[:, k:k + 1]
        lo = lo + wk * ylo
        hi = hi + wk * yhi
    y = h_ref[0] + mods_ref[0, 0, N_MODS - 1:N_MODS, :] * jnp.concatenate([lo, hi], axis=1)
    if final_norm:
        y = _rms(y, gfin_ref[...])
    o_ref[0] = y


def _moe_combine(yg, wcols, xp, sg, su, sd, h, mods, gfin, nct, layer, out_buf, out_b0, out_batch, latent_only,
                 final_norm):
    b, l, d = h.shape
    tl = TOKEN_TILE
    tile0 = nct if latent_only else 0
    tok = lambda w: pl.BlockSpec((1, tl, w), lambda i, j: (i, j + tile0, 0))
    lay = lambda x: pl.BlockSpec((1,) + x.shape[1:], lambda i, j: (layer,) + (0,) * (x.ndim - 1))
    args = [yg, wcols, xp, sg, su, sd, h, mods, gfin]
    in_specs = [pl.BlockSpec((TOP_K, 1, tl, d // 2), lambda i, j: (0, i, j + tile0, 0)), tok(GATE_W), tok(d // 2),
                lay(sg), lay(su), lay(sd), tok(d),
                pl.BlockSpec((1, 1, N_MODS, d), lambda i, j: (i, jnp.where(j + tile0 < nct, 0, 1), 0, 0)),
                pl.BlockSpec(gfin.shape, lambda i, j: (0, 0))]
    aliases = {}
    if out_buf is not None:
        args.append(out_buf)
        in_specs.append(pl.BlockSpec(memory_space=pl.ANY))
        aliases = {len(args) - 1: 0}
    return pl.pallas_call(
        functools.partial(_moe_combine_kernel, final_norm=final_norm),
        out_shape=jax.ShapeDtypeStruct((out_batch, l - tile0 * tl, d), F32),
        grid=(b, l // tl - tile0),
        in_specs=in_specs,
        out_specs=pl.BlockSpec((1, tl, d), lambda i, j: (i + out_b0, j, 0)),
        scratch_shapes=[pltpu.VMEM((d, 2 * sg.shape[2]), BF16), pltpu.VMEM((sg.shape[2], d), BF16)],
        input_output_aliases=aliases,
        compiler_params=pltpu.CompilerParams(dimension_semantics=("arbitrary", "arbitrary"),
                                             vmem_limit_bytes=_vmem_limit(40)),
        name="moe_combine",
    )(*args)


def _moe_sparse(n2p, eid, rank, wcols, counts, h, mods, wg, wu, wd, sg, su, sd, gfin, nct, layer,
                out_buf=None, out_b0=0, out_batch=None, last=False):
    b, l, d = h.shape
    t = b * l
    tm = MOE_ROW_TILE
    n_tiles = -(-(TOP_K * t + N_EXPERTS * (tm - 1)) // tm)
    tiles_e = (counts.reshape(N_EXPERTS).astype(jnp.int32) + (tm - 1)) // tm
    tile_end = jnp.cumsum(tiles_e)
    off = (tile_end - tiles_e) * tm
    n_valid = tile_end[-1:]
    tile_id = jnp.minimum(jnp.arange(n_tiles, dtype=jnp.int32), n_valid - 1)
    tile_expert = jnp.sum((tile_end[None, :] <= tile_id[:, None]).astype(jnp.int32), axis=1)
    dest = _moe_dest(off, eid, rank).transpose(1, 0, 2).reshape(8, t)
    xs = _sc_dispatch(n2p.reshape(t, d // 2), dest, n_tiles * tm)
    ys = _moe_experts(tile_expert, n_valid, xs, wg, wu, wd, layer)
    yg = _sc_gather(ys, dest, t).reshape(TOP_K, b, l, d // 2)
    return _moe_combine(yg, wcols, n2p, sg, su, sd, h, mods, gfin, nct, layer, out_buf, out_b0,
                        b if out_batch is None else out_batch, last, last)


def _rwkv_proj_kernel(h_ref, hp_ref, hx_ref, mods_ref, g_ref, mu_ref, wr_ref, wk_ref, wv_ref, g1_ref, g2_ref,
                      w1_ref, w2_ref, a1_ref, a2_ref, w0_ref, a0_ref, kk_ref, ka_ref, rk_ref, bd_ref,
                      r_out, v_out, kk_out, g_out, km_out, b_out, lw_out, bonus_out, *, nct):
    j = pl.program_id(1)
    nt = pl.num_programs(1)
    m = mods_ref[0, 0]
    g = g_ref[...]
    n = _norm_mod(h_ref[0], g, m[0:1], m[1:2])
    tl, d = n.shape
    seg_first = (j == 0) | (j == nct)
    seg_last = (j == nct - 1) | (j == nt - 1)
    n_prev = _norm_mod(hp_ref[0], g, m[0:1], m[1:2])[7:8] * jnp.where(seg_first, 0.0, 1.0)
    n_next = _norm_mod(hx_ref[0], g, m[0:1], m[1:2])[0:1] * jnp.where(seg_last, 0.0, 1.0)
    row = lax.broadcasted_iota(jnp.int32, (tl, 1), 0)
    prev = jnp.where(row == 0, n_prev, pltpu.roll(n, 1, axis=0))
    nxt = jnp.where(row == tl - 1, n_next, pltpu.roll(n, tl - 1, axis=0))
    lane = lax.broadcasted_iota(jnp.int32, (1, d), 1)
    xx = jnp.where(lane < d // 2, prev, nxt) - n
    mu = mu_ref[...]
    xr, xw, xk, xv, xa, xg = [n + xx * mu[i:i + 1] for i in range(6)]
    r = _dot(xr, wr_ref[...])
    k = _dot(xk, wk_ref[...])
    v = _dot(xv, wv_ref[...])
    g_out[0] = _dot(_sigmoid(_dot(xg, g1_ref[...])), g2_ref[...]).astype(g_out.dtype)
    tw = jnp.tanh(_dot(xw, w1_ref[...]))
    ta = _dot(xa, a1_ref[...])
    bd = bd_ref[...]
    kk = k * kk_ref[...]
    kk = kk / jnp.maximum(jnp.sqrt(_head_sum(kk * kk, bd)), 1e-12)
    r_out[0] = r.astype(r_out.dtype)
    v_out[0] = v.astype(v_out.dtype)
    kk_out[0] = kk.astype(kk_out.dtype)
    bonus = jnp.zeros_like(v)
    for dr in range(2):
        zw = w0_ref[dr:dr + 1, :] + _dot(tw, w2_ref[dr])
        lw_out[dr, 0] = -jnp.exp(-0.5) * _sigmoid(zw)
        a = _sigmoid(a0_ref[dr:dr + 1, :] + _dot(ta, a2_ref[dr]))
        km = k * (1.0 + (a - 1.0) * ka_ref[...])
        km_out[dr, 0] = km.astype(km_out.dtype)
        b_out[dr, 0] = (kk * a).astype(b_out.dtype)
        bonus = bonus + _head_sum(r * km * rk_ref[...], bd) * v
    bonus_out[0] = bonus


def _rwkv_proj(h, mods, g, mu, wr, wk, wv, g1, g2, w1, w2, a1, a2, w0, a0, kk, ka, rk, bd, nct):
    b, l, d = h.shape
    tl = TOKEN_TILE
    nb8 = l // 8
    tok = pl.BlockSpec((1, tl, d), lambda i, j: (i, j, 0))
    tok2 = pl.BlockSpec((2, 1, tl, d), lambda i, j: (0, i, j, 0))
    full = lambda x: pl.BlockSpec(x.shape, lambda i, j: (0,) * x.ndim)
    sds = jax.ShapeDtypeStruct
    return pl.pallas_call(
        functools.partial(_rwkv_proj_kernel, nct=nct),
        out_shape=[sds((b, l, d), BF16), sds((b, l, d), BF16), sds((b, l, d), BF16), sds((b, l, d), BF16),
                   sds((2, b, l, d), BF16), sds((2, b, l, d), BF16), sds((2, b, l, d), F32), sds((b, l, d), F32)],
        grid=(b, l // tl),
        in_specs=[tok,
                  pl.BlockSpec((1, 8, d), lambda i, j: (i, jnp.maximum(j * (tl // 8) - 1, 0), 0)),
                  pl.BlockSpec((1, 8, d), lambda i, j: (i, jnp.minimum((j + 1) * (tl // 8), nb8 - 1), 0)),
                  pl.BlockSpec((1, 1, N_MODS, d), lambda i, j: (i, jnp.where(j < nct, 0, 1), 0, 0)),
                  full(g), full(mu), full(wr), full(wk), full(wv), full(g1), full(g2), full(w1), full(w2),
                  full(a1), full(a2), full(w0), full(a0), full(kk), full(ka), full(rk), full(bd)],
        out_specs=[tok, tok, tok, tok, tok2, tok2, tok2, tok],
        compiler_params=pltpu.CompilerParams(dimension_semantics=("parallel", "parallel"),
                                             vmem_limit_bytes=_vmem_limit(56)),
        name="rwkv_proj",
    )(h, h, h, mods, g, mu, wr, wk, wv, g1, g2, w1, w2, a1, a2, w0, a0, kk, ka, rk, bd)


def _wkv_kernel(r_ref, v_ref, kk_ref, km_ref, b_ref, lw_ref, y_ref, st_ref):
    c = WKV_CHUNK
    w = WKV_PAIR
    rev = pl.program_id(0)
    sign = 1 - 2 * rev

    @pl.when(pl.program_id(2) == 0)
    def _():
        st_ref[...] = jnp.zeros_like(st_ref)

    ti = lax.broadcasted_iota(jnp.int32, (c, c), 0)
    si = lax.broadcasted_iota(jnp.int32, (c, c), 1)
    tri = jnp.where((si - ti) * sign <= 0, 1.0, 0.0).astype(F32)
    nsub = WKV_CHUNKS_PER_STEP
    subs = [pl.ds(pl.multiple_of(jnp.where(rev == 0, s, nsub - 1 - s) * c, c), c) for s in range(nsub)]
    rt, kt, kh, bh, v32, e_mid = [], [], [], [], [], []
    for rows in subs:
        lw = lw_ref[0, 0, rows, :]
        l_incl = jnp.dot(tri, lw, precision=HIGHEST, preferred_element_type=F32)
        mid = 0.5 * jnp.sum(lw, axis=0, keepdims=True)
        e_neg = jnp.exp(mid - l_incl)
        e_mid.append(jnp.exp(mid))
        rt.append(r_ref[0, rows, :].astype(F32) * jnp.exp(l_incl - mid))
        kt.append(kk_ref[0, rows, :].astype(F32) * jnp.exp(l_incl - lw - mid))
        kh.append(km_ref[0, 0, rows, :].astype(F32) * e_neg)
        bh.append(b_ref[0, 0, rows, :].astype(F32) * e_neg)
        v32.append(v_ref[0, rows, :].astype(F32))

    ri = lax.broadcasted_iota(jnp.int32, (w, w), 0)
    ci = lax.broadcasted_iota(jnp.int32, (w, w), 1)
    same = (ri // c) == (ci // c)
    eye = jnp.where(ri == ci, 1.0, 0.0).astype(F32)
    tl_ = lax.broadcasted_iota(jnp.int32, (c, w), 0)
    jl_ = lax.broadcasted_iota(jnp.int32, (c, w), 1) % c
    strict = (jl_ - tl_) * sign < 0
    incl = (jl_ - tl_) * sign <= 0
    eye2 = jnp.where(jl_ == tl_, 1.0, 0.0).astype(F32)
    lane = lax.broadcasted_iota(jnp.int32, (1, w), 1)
    h0 = lane < RWKV_HEAD

    def rows2(x):
        return jnp.concatenate([jnp.where(h0, x, 0.0), jnp.where(h0, 0.0, x)], axis=0)

    npair = st_ref.shape[0]
    items = [(s, slice(p * w, (p + 1) * w)) for s in range(nsub) for p in range(npair)]
    n = range(len(items))
    em = [e_mid[s][:, sl] for s, sl in items]
    g = [_dot_nt(jnp.concatenate([kt[s][:, sl], rt[s][:, sl]], axis=0),
                 jnp.concatenate([rows2(kh[s][:, sl]), rows2(bh[s][:, sl])], axis=0)) for s, sl in items]
    a_kk = [jnp.where(strict, x[:c, :w], 0.0) for x in g]
    a_rk = [jnp.where(incl, x[c:, :w], 0.0) for x in g]
    a_rb = [jnp.where(incl, x[c:, w:], 0.0) for x in g]
    vi = [v32[s][:, sl] for s, sl in items]
    v_rows = [rows2(x) for x in vi]
    r_pre = [_dot(a_kk[i], v_rows[i]) for i in n]
    m = [jnp.where(strict, -x[:c, w:], 0.0) for x in g]
    tinv = [eye2 + x for x in m]
    m = [_dot(x, rows2(x)) for x in m]
    for _ in range(c.bit_length() - 3):
        both = [_dot(jnp.concatenate([tinv[i], m[i]], axis=0), rows2(m[i])) for i in n]
        tinv = [tinv[i] + both[i][:c] for i in n]
        m = [x[c:] for x in both]
    tinv = [tinv[i] + _dot(tinv[i], rows2(m[i])) for i in n]
    sol = [_dot(tinv[i], jnp.concatenate([rows2(r_pre[i]), rows2(kt[s][:, sl] * em[i])], axis=1))
           for i, (s, sl) in enumerate(items)]
    u_pre = [x[:, :w] for x in sol]
    kq = [x[:, w:] for x in sol]
    y_pre = [_dot(jnp.concatenate([a_rk[i], -a_rb[i]], axis=1),
                  jnp.concatenate([v_rows[i], rows2(u_pre[i])], axis=0)) for i in n]
    r_eff = [rt[s][:, sl] * em[i] - _dot(a_rb[i], rows2(kq[i])) for i, (s, sl) in enumerate(items)]
    bbar = [bh[s][:, sl] * em[i] for i, (s, sl) in enumerate(items)]
    kbar = [kh[s][:, sl] * em[i] for i, (s, sl) in enumerate(items)]
    mmat = [eye * (em[i] * em[i]) - jnp.where(same, _dot_tn(kq[i], bbar[i]), 0.0) for i in n]
    s_pre = [jnp.where(same, _dot_tn(jnp.concatenate([vi[i], -u_pre[i]], axis=0),
                                     jnp.concatenate([kbar[i], bbar[i]], axis=0)), 0.0) for i in n]
    st = [st_ref[p] for p in range(npair)]
    for i, (s, sl) in enumerate(items):
        p = i % npair
        y_ref[0, 0, subs[s], sl] = _dot_nt(r_eff[i], st[p]) + y_pre[i]
        hi = st[p].astype(BF16)
        lo = (st[p] - hi.astype(F32)).astype(BF16)
        mb = mmat[i].astype(BF16)
        st[p] = (jnp.dot(hi, mb, preferred_element_type=F32) + jnp.dot(lo, mb, preferred_element_type=F32)
                 + s_pre[i])
    for p in range(npair):
        st_ref[p] = st[p]


def _wkv(r, v, kk, km, bv, lw, lc):
    b, l, d = r.shape
    c = WKV_CHUNK * WKV_CHUNKS_PER_STEP
    ncc = lc // c
    nlc = (l - lc) // c

    def chunk(dr, i):
        return jnp.where(dr == 0, i, jnp.where(i < ncc, ncc - 1 - i, nlc + 2 * ncc - 1 - i))

    shared = pl.BlockSpec((1, c, d), lambda dr, bi, i: (bi, chunk(dr, i), 0))
    per_dir = pl.BlockSpec((1, 1, c, d), lambda dr, bi, i: (dr, bi, chunk(dr, i), 0))
    return pl.pallas_call(
        _wkv_kernel,
        out_shape=jax.ShapeDtypeStruct((2, b, l, d), F32),
        grid=(2, b, l // c),
        in_specs=[shared, shared, shared, per_dir, per_dir, per_dir],
        out_specs=per_dir,
        scratch_shapes=[pltpu.VMEM((d // WKV_PAIR, WKV_PAIR, WKV_PAIR), F32)],
        compiler_params=pltpu.CompilerParams(dimension_semantics=("parallel", "parallel", "arbitrary"),
                                             vmem_limit_bytes=_vmem_limit(32)),
        name="wkv7_chunked",
    )(r, v, kk, km, bv, lw)


def _rwkv_out_kernel(y_ref, bonus_ref, g_ref, lnw_ref, lnb_ref, wo_ref, bd_ref, h_ref, mods_ref, gffn_ref,
                     wrt_ref, bias_ref, hn_ref, n2_ref, eid_ref, rank_ref, w_ref, cnt_ref, run_ref):
    y = y_ref[0, 0] + y_ref[1, 0]
    bd = bd_ref[...]
    mean = _head_sum(y, bd) * (1.0 / RWKV_HEAD)
    yc = y - mean
    var = _head_sum(yc * yc, bd) * (1.0 / RWKV_HEAD)
    yn = yc * lax.rsqrt(var + GN_EPS) * lnw_ref[...] + lnb_ref[...]
    out = (yn + bonus_ref[0]) * g_ref[0].astype(F32)
    _mixer_tail(_dot(out, wo_ref[...]), h_ref[0], mods_ref[0, 0], gffn_ref, wrt_ref, bias_ref, hn_ref, n2_ref,
                eid_ref, rank_ref, w_ref, cnt_ref, run_ref)


def _rwkv_out(y, bonus, g, lnw, lnb, wo, bd, h, mods, gffn, wrt, bias, nct):
    b, l, d = h.shape
    tl = TOKEN_TILE
    tok = lambda w: pl.BlockSpec((1, tl, w), lambda i, j: (i, j, 0))
    full = lambda x: pl.BlockSpec(x.shape, lambda i, j: (0,) * x.ndim)
    shapes, specs = _tail_outs(b, l, d)
    return pl.pallas_call(
        _rwkv_out_kernel,
        out_shape=shapes,
        grid=(b, l // tl),
        in_specs=[pl.BlockSpec((2, 1, tl, d), lambda i, j: (0, i, j, 0)), tok(d), tok(d),
                  full(lnw), full(lnb), full(wo), full(bd), tok(d),
                  pl.BlockSpec((1, 1, N_MODS, d), lambda i, j: (i, jnp.where(j < nct, 0, 1), 0, 0)),
                  full(gffn), full(wrt), full(bias)],
        out_specs=specs,
        scratch_shapes=[pltpu.VMEM((N_EXPERTS, 1), F32)],
        compiler_params=pltpu.CompilerParams(dimension_semantics=("arbitrary", "arbitrary"),
                                             vmem_limit_bytes=_vmem_limit(40)),
        name="rwkv_out",
    )(y, bonus, g, lnw, lnb, wo, bd, h, mods, gffn, wrt, bias)


def _rope_table(n_lat, n_ctx):
    dim = SWA_HEAD_DIM
    nf = dim // 4
    inv = ROPE_THETA ** (-jnp.arange(nf, dtype=F32) / nf)
    row = jnp.repeat(jnp.arange(n_lat // GRID_W, dtype=F32), GRID_W)
    col = jnp.tile(jnp.arange(GRID_W, dtype=F32), n_lat // GRID_W)
    ar = row[:, None] * inv
    ac = col[:, None] * inv
    ang = jnp.concatenate([ar, ar, ac, ac], axis=-1)
    cos = jnp.concatenate([jnp.ones((n_ctx, dim), F32), jnp.cos(ang)], axis=0)
    sin = jnp.concatenate([jnp.zeros((n_ctx, dim), F32), jnp.sin(ang)], axis=0)
    return jnp.tile(cos, (1, 2)), jnp.tile(sin, (1, 2))


def _layout_attn_weights(w_in, w_uq, w_ukv):
    d = w_in.shape[0]
    s0 = MLA_Q_RANK
    s1 = s0 + MLA_KV_RANK
    s2 = s1 + MLA_ROPE
    s3 = s2 + SWA_HEADS * SWA_HEAD_DIM
    s4 = s3 + SWA_KV_HEADS * SWA_HEAD_DIM
    rep = lambda w: jnp.concatenate(
        [jnp.tile(w[:, g * SWA_HEAD_DIM:(g + 1) * SWA_HEAD_DIM], (1, SWA_GROUP)) for g in range(SWA_KV_HEADS)], axis=1)
    win = jnp.concatenate([w_in[:, :s1], w_in[:, s2:s3], rep(w_in[:, s3:s4]), rep(w_in[:, s4:]),
                           w_in[:, s1:s2], jnp.zeros((d, V7X_LANES - MLA_ROPE), w_in.dtype)], axis=1)
    qh = MLA_NOPE + MLA_ROPE
    pad = jnp.zeros((w_uq.shape[0], V7X_MXU_DIM - qh), w_uq.dtype)
    wuq = jnp.concatenate([jnp.concatenate([w_uq[:, h * qh:(h + 1) * qh], pad], axis=1) for h in range(MLA_HEADS)], axis=1)
    kvh = MLA_NOPE + MLA_V
    wuk = jnp.concatenate([w_ukv[:, h * kvh:h * kvh + MLA_NOPE] for h in range(MLA_HEADS)], axis=1)
    wuvt = jnp.concatenate([w_ukv[:, h * kvh + MLA_NOPE:(h + 1) * kvh] for h in range(MLA_HEADS)], axis=1).T
    return win.astype(BF16), wuq.astype(BF16), wuk.astype(BF16), wuvt.astype(BF16)


def _lora_pair(w_down, w_up):
    rank = w_down.shape[2]
    down = jnp.concatenate([w_down[0], w_down[1]], axis=1)
    z = jnp.zeros((rank, w_up.shape[2]), w_up.dtype)
    up = jnp.stack([jnp.concatenate([w_up[0], z], axis=0), jnp.concatenate([z, w_up[1]], axis=0)], axis=0)
    return down.astype(BF16), up.astype(BF16)


def _head_block_diag():
    i = jnp.arange(V7X_MXU_DIM) // RWKV_HEAD
    return (i[:, None] == i[None, :]).astype(BF16)


def kernel(x, c, ctx, c_ctx, ada_w, ada_b, norm_mix, norm_ffn, norm_final, attn_w_in, attn_q_norm, attn_kv_norm, attn_w_uq, attn_w_ukv, attn_sinks, attn_w_o, rwkv_mu, rwkv_w_r, rwkv_w_k, rwkv_w_v, rwkv_w_o, rwkv_g1, rwkv_g2, rwkv_w0, rwkv_w1, rwkv_w2, rwkv_a0, rwkv_a1, rwkv_a2, rwkv_k_k, rwkv_k_a, rwkv_r_k, rwkv_ln_w, rwkv_ln_b, moe_router, moe_bias, moe_w_gate, moe_w_up, moe_w_down, moe_ws_gate, moe_ws_up, moe_ws_down):
    bsz, s, d = x.shape
    lc = ctx.shape[1]
    l = lc + s
    depth = ada_w.shape[0]
    nct = lc // TOKEN_TILE
    assert lc % TOKEN_TILE == 0 and s % TOKEN_TILE == 0 and s >= SWA_BAND
    assert lc % (WKV_CHUNK * WKV_CHUNKS_PER_STEP) == 0
    assert d % V7X_MXU_DIM == 0 and WKV_CHUNK * 2 == V7X_LANES
    ngrp = SAMPLE_GROUPS
    bg = bsz // ngrp
    assert bsz % ngrp == 0 and (bg * l) % (8 * V7X_SC_WORKERS) == 0

    streams = [(ctx, x, g * bg, 0) for g in range(ngrp)]
    out = None
    cos, sin = _rope_table(s, lc)
    bd = _head_block_diag()
    rows = -(-(bsz + 1) // 8) * 8
    cc = jnp.concatenate([c, c_ctx[None, :], jnp.zeros((rows - bsz - 1, d), F32)], axis=0)
    row2 = lambda a: a.reshape(1, -1)

    for li in range(depth):
        with_ctx = li < depth - 1
        i = li // 2
        ada = _ada_mods(cc, ada_w, ada_b, li)
        mods_all = jnp.stack([jnp.broadcast_to(ada[bsz].reshape(1, N_MODS, d), (bsz, N_MODS, d)),
                              ada[:bsz].reshape(bsz, N_MODS, d)], axis=1)
        wrt = jnp.concatenate([moe_router[li].T, jnp.zeros((GATE_W - N_EXPERTS, d), F32)], axis=0)
        bias = moe_bias[li].reshape(N_GROUPS, GROUP_SIZE, 1)
        if li % 2 == 0:
            win, wuq, wuk, wuvt = _layout_attn_weights(attn_w_in[i], attn_w_uq[i], attn_w_ukv[i])
            wo = attn_w_o[i].astype(BF16)
        else:
            w1, w2 = _lora_pair(rwkv_w1[i], rwkv_w2[i])
            a1, a2 = _lora_pair(rwkv_a1[i], rwkv_a2[i])
            wr, wk, wv, wo = [w[i].astype(BF16) for w in (rwkv_w_r, rwkv_w_k, rwkv_w_v, rwkv_w_o)]
            g1, g2 = rwkv_g1[i].astype(BF16), rwkv_g2[i].astype(BF16)
        for g in range(ngrp):
            mods = mods_all[g * bg:(g + 1) * bg]
            if li % 2 == 0:
                q, k, vt, qs, ks, vs = _attn_proj(streams[g], bg, l, mods, row2(norm_mix[li]), win,
                                                  row2(attn_q_norm[i]), row2(attn_kv_norm[i]), wuq, wuk, wuvt,
                                                  cos, sin, nct)
                a = _mla_attention(q, k, vt, lc, 0 if with_ctx else lc // MLA_Q_TILE)
                bm = _swa_attention(attn_sinks[i], qs, ks, vs, lc, 0 if with_ctx else lc // SWA_Q_TILE)
                tail = _attn_out(a, bm, streams[g], mods, wo, row2(norm_ffn[li]), wrt, bias, nct)
            else:
                h = streams[g][0]
                assert streams[g][0] is streams[g][1]
                r, v, kk, gt, km, bv, lw, bonus = _rwkv_proj(
                    h, mods, row2(norm_mix[li]), rwkv_mu[i], wr, wk, wv, g1, g2, w1, w2, a1, a2,
                    rwkv_w0[i], rwkv_a0[i], row2(rwkv_k_k[i]), row2(rwkv_k_a[i]), row2(rwkv_r_k[i]), bd, nct)
                y = _wkv(r, v, kk, km, bv, lw, lc)
                tail = _rwkv_out(y, bonus, gt, row2(rwkv_ln_w[i]), row2(rwkv_ln_b[i]), wo,
                                 bd, h, mods, row2(norm_ffn[li]), wrt, bias, nct)
            h, n2p, eid, rank, wcols, counts = tail
            moe_w = (moe_w_gate, moe_w_up, moe_w_down, moe_ws_gate, moe_ws_up, moe_ws_down)
            if li < depth - 1:
                h = _moe_sparse(n2p, eid, rank, wcols, counts, h, mods, *moe_w, row2(norm_final), nct, li)
                streams[g] = (h, h, 0, nct)
            else:
                out = _moe_sparse(n2p, eid, rank, wcols, counts, h, mods, *moe_w, row2(norm_final), nct, li,
                                  out_buf=out, out_b0=g * bg, out_batch=bsz, last=True)
    return out
```

```python
import functools

import jax
import jax.numpy as jnp
from jax import lax
from jax.experimental import pallas as pl
from jax.experimental.pallas import tpu as pltpu
from jax.experimental.pallas import tpu_sc as plsc

F32 = jnp.float32
BF16 = jnp.bfloat16
HIGHEST = lax.Precision.HIGHEST

GRID_W = 64
NORM_EPS = 1e-6
ROPE_THETA = 10000.0
NEG_INF = -1e30
N_MODS = 6

MLA_HEADS = 4
MLA_Q_RANK = 384
MLA_KV_RANK = 256
MLA_NOPE = 128
MLA_ROPE = 64
MLA_V = 128

SWA_HEADS = 8
SWA_KV_HEADS = 2
SWA_GROUP = SWA_HEADS // SWA_KV_HEADS
SWA_HEAD_DIM = 64
WINDOW = 128

RWKV_HEAD = 64
DECAY_LORA = 64
ICLR_LORA = 64
GATE_LORA = 128
GN_EPS = 64e-5

N_EXPERTS = 64
TOP_K = 6
N_GROUPS = 8
TOPK_GROUPS = 4
GROUP_SIZE = N_EXPERTS // N_GROUPS
ROUTED_SCALE = 2.5
GATE_W = 128

V7X_LANES = 128
V7X_MXU_DIM = 256
V7X_VMEM_BYTES = 64 * 1024 * 1024
V7X_SC_CORES = 2
V7X_SC_SUBCORES = 16
V7X_SC_WORKERS = V7X_SC_CORES * V7X_SC_SUBCORES

TOKEN_TILE = 256
MLA_Q_TILE = 256
MLA_HEADS_PER_STEP = 2
LOG2E = 1.4426950408889634
SWA_Q_TILE = 256
SWA_BAND = SWA_Q_TILE + 2 * WINDOW
WKV_CHUNK = 64
WKV_PAIR = 2 * RWKV_HEAD
WKV_CHUNKS_PER_STEP = 4
MOE_ROW_TILE = 512
SAMPLE_GROUPS = 2
SC_MAX_CHUNK = 64


def _vmem_limit(mib):
    return min(mib * 1024 * 1024, V7X_VMEM_BYTES - 4 * 1024 * 1024)


def _dot(a, b):
    return jnp.dot(a.astype(BF16), b.astype(BF16), preferred_element_type=F32)


def _dot_nt(a, b):
    return lax.dot_general(a.astype(BF16), b.astype(BF16), (((1,), (1,)), ((), ())),
                           preferred_element_type=F32)


def _dot_tn(a, b):
    return lax.dot_general(a.astype(BF16), b.astype(BF16), (((0,), (0,)), ((), ())),
                           preferred_element_type=F32)


def _sigmoid(x):
    return 1.0 / (1.0 + jnp.exp(-x))


def _silu(x):
    return x * _sigmoid(x)


def _rms(x, g):
    return x * lax.rsqrt(jnp.mean(x * x, axis=-1, keepdims=True) + NORM_EPS) * g


def _norm_mod(x, g, shift, scale):
    return _rms(x, g) * (1.0 + scale) + shift


def _split_dot(x, w):
    hi = x.astype(BF16)
    lo = (x - hi.astype(F32)).astype(BF16)
    return (jnp.dot(hi, w, preferred_element_type=F32) + jnp.dot(lo, w, preferred_element_type=F32))


def _head_sum(x, bd):
    w = bd.shape[0]
    parts = [_split_dot(x[:, c * w:(c + 1) * w], bd) for c in range(x.shape[1] // w)]
    return jnp.concatenate(parts, axis=1)


def _ada_kernel(c_ref, w_ref, b_ref, o_ref):
    s = _silu(c_ref[...])
    o_ref[...] = jnp.dot(s, w_ref[0], precision=HIGHEST, preferred_element_type=F32) + b_ref[0]


def _ada_mods(cc, w, b, layer):
    rows, d = cc.shape
    depth, _, n = w.shape
    return pl.pallas_call(
        _ada_kernel,
        out_shape=jax.ShapeDtypeStruct((rows, n), F32),
        grid=(n // d,),
        in_specs=[pl.BlockSpec((rows, d), lambda i: (0, 0)),
                  pl.BlockSpec((1, d, d), lambda i: (layer, 0, i)),
                  pl.BlockSpec((1, 1, d), lambda i: (layer, 0, i))],
        out_specs=pl.BlockSpec((rows, d), lambda i: (0, i)),
        compiler_params=pltpu.CompilerParams(dimension_semantics=("parallel",),
                                             vmem_limit_bytes=_vmem_limit(32)),
        name="ada_mods",
    )(cc, w, b.reshape(depth, 1, n))


def _rope128(x, cos, sin, first_half):
    rot = jnp.where(first_half, -pltpu.roll(x, V7X_LANES - 16, axis=1), pltpu.roll(x, 16, axis=1))
    return x * cos + rot * sin


_C_CQ = 0
_C_CKV = _C_CQ + MLA_Q_RANK
_C_QS = _C_CKV + MLA_KV_RANK
_C_KS = _C_QS + SWA_HEADS * SWA_HEAD_DIM
_C_VS = _C_KS + SWA_KV_HEADS * V7X_MXU_DIM
_C_KR = _C_VS + SWA_KV_HEADS * V7X_MXU_DIM
_C_END = _C_KR + V7X_LANES
_SWA_W = SWA_KV_HEADS * V7X_MXU_DIM
_MLA_QK_W = MLA_HEADS * V7X_MXU_DIM


def _stream_specs(stream, nct, tl):
    ctx_arr, lat_arr, b0, lat_off = stream
    d = ctx_arr.shape[2]
    return [pl.BlockSpec((1, tl, d), lambda i, j: (i + b0, jnp.minimum(j, nct - 1), 0)),
            pl.BlockSpec((1, tl, d), lambda i, j: (i + b0, jnp.maximum(j - nct, 0) + lat_off, 0))]


def _stream_tile(c_ref, x_ref, nct):
    rows = c_ref.shape[1]
    take_ctx = lax.broadcasted_iota(jnp.int32, (rows, 1), 0) < jnp.where(pl.program_id(1) < nct, rows, 0)
    return jnp.where(take_ctx, c_ref[0], x_ref[0])


def _attn_proj_kernel(c_ref, x_ref, mods_ref, g_ref, win_ref, qn_ref, kvn_ref, wuq_ref, wuk_ref, wuvt_ref, cos_ref,
                      sin_ref, q_ref, k_ref, vt_ref, qs_ref, ks_ref, vs_ref, *, nct):
    m = mods_ref[0, 0]
    n = _norm_mod(_stream_tile(c_ref, x_ref, nct), g_ref[...], m[0:1], m[1:2])
    u = _dot(n, win_ref[...])
    cos = cos_ref[...]
    sin = sin_ref[...]
    lane = lax.broadcasted_iota(jnp.int32, (1, V7X_LANES), 1)
    first_half = (lane % 32) < 16

    def rope(x):
        return _rope128(x, cos, sin, first_half)

    scale_a = (MLA_NOPE + MLA_ROPE) ** -0.5 * LOG2E
    scale_b = SWA_HEAD_DIM ** -0.5 * LOG2E
    q = _dot(_rms(u[:, _C_CQ:_C_CKV], qn_ref[...]), wuq_ref[...])
    ckv = _rms(u[:, _C_CKV:_C_QS], kvn_ref[...])
    kn = _dot(ckv, wuk_ref[...])
    vt_ref[0] = _dot_nt(wuvt_ref[...], ckv).astype(BF16)
    kr = rope(u[:, _C_KR:_C_END]).astype(BF16)
    for h in range(MLA_HEADS):
        o = h * V7X_MXU_DIM
        q_ref[0, :, o:o + V7X_LANES] = (q[:, o:o + V7X_LANES] * scale_a).astype(BF16)
        q_ref[0, :, o + V7X_LANES:o + V7X_MXU_DIM] = (rope(q[:, o + V7X_LANES:o + V7X_MXU_DIM]) * scale_a).astype(BF16)
        k_ref[0, :, o:o + V7X_LANES] = kn[:, h * MLA_NOPE:(h + 1) * MLA_NOPE].astype(BF16)
        k_ref[0, :, o + V7X_LANES:o + V7X_MXU_DIM] = kr
    for c in range((_C_KS - _C_QS) // V7X_LANES):
        o = c * V7X_LANES
        qs_ref[0, :, o:o + V7X_LANES] = (rope(u[:, _C_QS + o:_C_QS + o + V7X_LANES]) * scale_b).astype(BF16)
    for c in range(_SWA_W // V7X_LANES):
        o = c * V7X_LANES
        ks_ref[0, :, o:o + V7X_LANES] = rope(u[:, _C_KS + o:_C_KS + o + V7X_LANES]).astype(BF16)
    vs_ref[0] = u[:, _C_VS:_C_KR].astype(BF16)


def _attn_proj(stream, b, l, mods, g, win, qn, kvn, wuq, wuk, wuvt, cos, sin, nct):
    d = stream[0].shape[2]
    tl = TOKEN_TILE
    tok = lambda w: pl.BlockSpec((1, tl, w), lambda i, j: (i, j, 0))
    full = lambda a: pl.BlockSpec(a.shape, lambda i, j: (0,) * a.ndim)
    sds = jax.ShapeDtypeStruct
    dv = MLA_HEADS * MLA_V
    return pl.pallas_call(
        functools.partial(_attn_proj_kernel, nct=nct),
        out_shape=[sds((b, l, _MLA_QK_W), BF16), sds((b, l, _MLA_QK_W), BF16), sds((b, dv, l), BF16),
                   sds((b, l, SWA_HEADS * SWA_HEAD_DIM), BF16), sds((b, l, _SWA_W), BF16), sds((b, l, _SWA_W), BF16)],
        grid=(b, l // tl),
        in_specs=_stream_specs(stream, nct, tl) + [
                  pl.BlockSpec((1, 1, N_MODS, d), lambda i, j: (i, jnp.where(j < nct, 0, 1), 0, 0)),
                  full(g), full(win), full(qn), full(kvn), full(wuq), full(wuk), full(wuvt),
                  pl.BlockSpec((tl, V7X_LANES), lambda i, j: (j, 0)),
                  pl.BlockSpec((tl, V7X_LANES), lambda i, j: (j, 0))],
        out_specs=[tok(_MLA_QK_W), tok(_MLA_QK_W), pl.BlockSpec((1, dv, tl), lambda i, j: (i, 0, j)),
                   tok(SWA_HEADS * SWA_HEAD_DIM), tok(_SWA_W), tok(_SWA_W)],
        compiler_params=pltpu.CompilerParams(dimension_semantics=("parallel", "parallel"),
                                             vmem_limit_bytes=_vmem_limit(48)),
        name="attn_proj",
    )(stream[0], stream[1], mods, g, win, qn, kvn, wuq, wuk, wuvt, cos, sin)


def _mla_kernel(q_ref, k_ref, vt_ref, o_ref, *, nct_q, lc):
    hw = V7X_MXU_DIM

    def attend(nk):
        st = [_dot_nt(k_ref[0, 0:nk, hh * hw:(hh + 1) * hw], q_ref[0, :, hh * hw:(hh + 1) * hw])
              for hh in range(MLA_HEADS_PER_STEP)]
        for hh, s in enumerate(st):
            p = jnp.exp2(s - jnp.max(s, axis=0, keepdims=True))
            den = jnp.sum(p, axis=0, keepdims=True)
            ot = _dot(vt_ref[0, hh * MLA_V:(hh + 1) * MLA_V, 0:nk], p) / den
            o_ref[0, :, hh * MLA_V:(hh + 1) * MLA_V] = ot.T.astype(o_ref.dtype)

    @pl.when(pl.program_id(2) < nct_q)
    def _():
        attend(lc)

    @pl.when(pl.program_id(2) >= nct_q)
    def _():
        attend(k_ref.shape[1])


def _mla_attention(q, k, vt, lc, q_tile0):
    b, l, _ = q.shape
    tq = MLA_Q_TILE
    hps = MLA_HEADS_PER_STEP
    return pl.pallas_call(
        functools.partial(_mla_kernel, nct_q=lc // tq - q_tile0, lc=lc),
        out_shape=jax.ShapeDtypeStruct((b, l, MLA_HEADS * MLA_V), BF16),
        grid=(b, MLA_HEADS // hps, l // tq - q_tile0),
        in_specs=[pl.BlockSpec((1, tq, hps * V7X_MXU_DIM), lambda i, h, j: (i, j + q_tile0, h)),
                  pl.BlockSpec((1, l, hps * V7X_MXU_DIM), lambda i, h, j: (i, 0, h)),
                  pl.BlockSpec((1, hps * MLA_V, l), lambda i, h, j: (i, h, 0))],
        out_specs=pl.BlockSpec((1, tq, hps * MLA_V), lambda i, h, j: (i, j + q_tile0, h)),
        compiler_params=pltpu.CompilerParams(dimension_semantics=("parallel", "parallel", "parallel"),
                                             vmem_limit_bytes=_vmem_limit(48)),
        name="mla_attention",
    )(q, k, vt)


def _swa_kernel(sink_ref, q_ref, k_ref, v_ref, o_ref, *, lc, q_tile0):
    tq = SWA_Q_TILE
    l = k_ref.shape[1]
    r0 = (pl.program_id(1) + q_tile0) * tq
    start = pl.multiple_of(jnp.clip(r0 - WINDOW, lc, l - SWA_BAND), WINDOW)
    rows = SWA_GROUP * tq
    row = lax.broadcasted_iota(jnp.int32, (rows, 1), 0)
    qpos = jnp.where(r0 >= lc, r0, -l) + row % tq
    kpos = start + lax.broadcasted_iota(jnp.int32, (1, SWA_BAND), 1)
    valid = jnp.abs(qpos - kpos) <= WINDOW
    lane = lax.broadcasted_iota(jnp.int32, (1, V7X_MXU_DIM), 1)
    head = [(lane // SWA_HEAD_DIM) == hh for hh in range(SWA_GROUP)]
    groups = range(SWA_KV_HEADS)
    sls = [slice(g * V7X_MXU_DIM, (g + 1) * V7X_MXU_DIM) for g in groups]
    qstack = []
    for sl in sls:
        qg = q_ref[0, :, sl]
        zero = jnp.zeros_like(qg)
        qstack.append(jnp.concatenate([jnp.where(head[hh], qg, zero) for hh in range(SWA_GROUP)], axis=0))
    sc = [_dot_nt(qstack[g], k_ref[0, 0:lc, sls[g]]) for g in groups]
    sb = [_dot_nt(qstack[g], k_ref[0, pl.ds(start, SWA_BAND), sls[g]]) for g in groups]
    for g in groups:
        sl = sls[g]
        sbm = jnp.where(valid, sb[g], NEG_INF)
        sk = jnp.zeros((rows, 1), F32)
        for hh in range(SWA_GROUP):
            sk = jnp.where(row // tq == hh, sink_ref[g * SWA_GROUP + hh] * LOG2E, sk)
        mx = jnp.maximum(jnp.maximum(jnp.max(sc[g], axis=-1, keepdims=True), jnp.max(sbm, axis=-1, keepdims=True)), sk)
        pc = jnp.exp2(sc[g] - mx)
        pb = jnp.exp2(sbm - mx)
        den = jnp.sum(pc, axis=-1, keepdims=True) + jnp.sum(pb, axis=-1, keepdims=True) + jnp.exp2(sk - mx)
        ostack = (_dot(pc, v_ref[0, 0:lc, sl]) + _dot(pb, v_ref[0, pl.ds(start, SWA_BAND), sl])) / den
        o = jnp.zeros((tq, V7X_MXU_DIM), F32)
        for hh in range(SWA_GROUP):
            o = o + jnp.where(head[hh], ostack[hh * tq:(hh + 1) * tq], 0.0)
        o_ref[0, :, sl] = o.astype(o_ref.dtype)


def _swa_attention(sinks, q, k, v, lc, q_tile0):
    b, l, _ = q.shape
    tq = SWA_Q_TILE
    return pl.pallas_call(
        functools.partial(_swa_kernel, lc=lc, q_tile0=q_tile0),
        out_shape=jax.ShapeDtypeStruct((b, l, SWA_HEADS * SWA_HEAD_DIM), BF16),
        grid=(b, l // tq - q_tile0),
        in_specs=[pl.BlockSpec(memory_space=pltpu.SMEM),
                  pl.BlockSpec((1, tq, SWA_HEADS * SWA_HEAD_DIM), lambda i, j: (i, j + q_tile0, 0)),
                  pl.BlockSpec((1, l, _SWA_W), lambda i, j: (i, 0, 0)),
                  pl.BlockSpec((1, l, _SWA_W), lambda i, j: (i, 0, 0))],
        out_specs=pl.BlockSpec((1, tq, SWA_HEADS * SWA_HEAD_DIM), lambda i, j: (i, j + q_tile0, 0)),
        compiler_params=pltpu.CompilerParams(dimension_semantics=("parallel", "parallel"),
                                             vmem_limit_bytes=_vmem_limit(48)),
        name="swa_attention",
    )(sinks, q, k, v)


def _pack_bf16_pair(x):
    w = x.shape[1] // 2
    lo = pltpu.bitcast(x[:, :w].astype(BF16).astype(F32), jnp.int32)
    hi = pltpu.bitcast(x[:, w:].astype(BF16).astype(F32), jnp.int32)
    return lax.shift_right_logical(lo, jnp.int32(16)) | (hi & jnp.int32(-65536))


def _unpack_bf16_pair(p):
    return pltpu.bitcast(p << 16, F32), pltpu.bitcast(p & jnp.int32(-65536), F32)


def _route(n2, wrt, bias, run_ref):
    n_hi = n2.astype(BF16)
    n_lo = (n2 - n_hi.astype(F32)).astype(BF16)
    w_hi = wrt.astype(BF16)
    w_lo = (wrt - w_hi.astype(F32)).astype(BF16)
    logits = _dot_nt(w_hi, n_hi) + (_dot_nt(w_hi, n_lo) + _dot_nt(w_lo, n_hi))
    rows = logits.shape[1]
    scores = _sigmoid(logits[0:N_EXPERTS])

    def select(sc2):
        cols = sc2.shape[1]
        shape3 = (N_GROUPS, GROUP_SIZE, cols)
        choice = sc2.reshape(shape3) + bias
        ji = lax.broadcasted_iota(jnp.int32, shape3, 1).astype(F32)
        m1 = jnp.max(choice, axis=1, keepdims=True)
        first = jnp.min(jnp.where(choice == m1, ji, float(GROUP_SIZE)), axis=1, keepdims=True)
        m2 = jnp.max(jnp.where(ji == first, -jnp.inf, choice), axis=1, keepdims=True)
        gs = m1 + m2
        gidx = lax.broadcasted_iota(jnp.int32, gs.shape, 0).astype(F32)
        gsel = jnp.zeros_like(gs)
        for _ in range(TOPK_GROUPS):
            mx = jnp.max(gs, axis=0, keepdims=True)
            pick = gidx == jnp.min(jnp.where(gs == mx, gidx, float(N_GROUPS)), axis=0, keepdims=True)
            gsel = jnp.where(pick, 1.0, gsel)
            gs = jnp.where(pick, -jnp.inf, gs)
        cand = jnp.where(gsel > 0.0, choice, -jnp.inf).reshape(N_EXPERTS, cols)
        eidx = lax.broadcasted_iota(jnp.int32, (N_EXPERTS, cols), 0).astype(F32)
        out = []
        for _ in range(TOP_K):
            mx = jnp.max(cand, axis=0, keepdims=True)
            pick = eidx == jnp.min(jnp.where(cand == mx, eidx, float(N_EXPERTS)), axis=0, keepdims=True)
            out.append(jnp.where(pick, 1.0, 0.0))
            cand = jnp.where(pick, -jnp.inf, cand)
        return out

    blocks = [select(scores[:, o:o + V7X_LANES]) for o in range(0, rows, V7X_LANES)]
    picks = [jnp.concatenate([blk[k] for blk in blocks], axis=1) > 0.0 for k in range(TOP_K)]
    ei = lax.broadcasted_iota(jnp.int32, (N_EXPERTS, rows), 0).astype(F32)
    esel = jnp.zeros((N_EXPERTS, rows), F32)
    for pick in picks:
        esel = jnp.where(pick, 1.0, esel)
    before = jnp.where(lax.broadcasted_iota(jnp.int32, (rows, rows), 0) < lax.broadcasted_iota(jnp.int32, (rows, rows), 1),
                       1.0, 0.0).astype(BF16)
    slot = jnp.dot(esel.astype(BF16), before, preferred_element_type=F32) + run_ref[...]
    run_ref[...] += jnp.sum(esel, axis=1, keepdims=True)
    sc = [jnp.sum(jnp.where(pick, scores, 0.0), axis=0, keepdims=True) for pick in picks]
    tot = sc[0]
    for x in sc[1:]:
        tot = tot + x
    k8 = lax.broadcasted_iota(jnp.int32, (8, rows), 0)
    kw = lax.broadcasted_iota(jnp.int32, (GATE_W, rows), 0)
    eid = jnp.zeros((8, rows), jnp.int32)
    rank = jnp.zeros((8, rows), jnp.int32)
    wk = jnp.zeros((GATE_W, rows), F32)
    for k, pick in enumerate(picks):
        e_k = jnp.sum(jnp.where(pick, ei, 0.0), axis=0, keepdims=True).astype(jnp.int32)
        r_k = jnp.sum(jnp.where(pick, slot, 0.0), axis=0, keepdims=True).astype(jnp.int32)
        eid = jnp.where(k8 == k, e_k, eid)
        rank = jnp.where(k8 == k, r_k, rank)
        wk = jnp.where(kw == k, sc[k] * (ROUTED_SCALE / tot), wk)
    return eid, rank, wk.T


def _mixer_tail(o, h, m, gffn_ref, wrt_ref, bias_ref, hn_ref, n2_ref, eid_ref, rank_ref, w_ref, cnt_ref, run_ref):
    @pl.when((pl.program_id(0) == 0) & (pl.program_id(1) == 0))
    def _():
        run_ref[...] = jnp.zeros_like(run_ref)

    hn = h + m[2:3] * o
    hn_ref[0] = hn
    n2 = _norm_mod(hn, gffn_ref[...], m[3:4], m[4:5])
    n2_ref[0] = _pack_bf16_pair(n2)
    eid, rank, wcols = _route(n2, wrt_ref[...], bias_ref[...], run_ref)
    eid_ref[0] = eid
    rank_ref[0] = rank
    w_ref[0] = wcols
    cnt_ref[...] = run_ref[...]


def _attn_out_kernel(a_ref, b_ref, c_ref, x_ref, mods_ref, wo_ref, gffn_ref, wrt_ref, bias_ref,
                     hn_ref, n2_ref, eid_ref, rank_ref, w_ref, cnt_ref, run_ref, *, nct):
    wa = MLA_HEADS * MLA_V
    o = _dot(a_ref[0], wo_ref[0:wa, :]) + _dot(b_ref[0], wo_ref[wa:, :])
    _mixer_tail(o, _stream_tile(c_ref, x_ref, nct), mods_ref[0, 0], gffn_ref, wrt_ref, bias_ref, hn_ref, n2_ref,
                eid_ref, rank_ref, w_ref, cnt_ref, run_ref)


def _tail_outs(b, l, d):
    tl = TOKEN_TILE
    nt = l // tl
    sds = jax.ShapeDtypeStruct
    tok = lambda w: pl.BlockSpec((1, tl, w), lambda i, j: (i, j, 0))
    blk = pl.BlockSpec((1, 8, tl), lambda i, j: (i * nt + j, 0, 0))
    shapes = [sds((b, l, d), F32), sds((b, l, d // 2), jnp.int32), sds((b * nt, 8, tl), jnp.int32),
              sds((b * nt, 8, tl), jnp.int32), sds((b, l, GATE_W), F32), sds((N_EXPERTS, 1), F32)]
    specs = [tok(d), tok(d // 2), blk, blk, tok(GATE_W), pl.BlockSpec((N_EXPERTS, 1), lambda i, j: (0, 0))]
    return shapes, specs


def _attn_out(a, bm, stream, mods, wo, gffn, wrt, bias, nct):
    b, l, _ = a.shape
    d = stream[0].shape[2]
    tl = TOKEN_TILE
    tok = lambda w: pl.BlockSpec((1, tl, w), lambda i, j: (i, j, 0))
    full = lambda x: pl.BlockSpec(x.shape, lambda i, j: (0,) * x.ndim)
    shapes, specs = _tail_outs(b, l, d)
    return pl.pallas_call(
        functools.partial(_attn_out_kernel, nct=nct),
        out_shape=shapes,
        grid=(b, l // tl),
        in_specs=[tok(a.shape[2]), tok(bm.shape[2])] + _stream_specs(stream, nct, tl) + [
                  pl.BlockSpec((1, 1, N_MODS, d), lambda i, j: (i, jnp.where(j < nct, 0, 1), 0, 0)),
                  full(wo), full(gffn), full(wrt), full(bias)],
        out_specs=specs,
        scratch_shapes=[pltpu.VMEM((N_EXPERTS, 1), F32)],
        compiler_params=pltpu.CompilerParams(dimension_semantics=("arbitrary", "arbitrary"),
                                             vmem_limit_bytes=_vmem_limit(40)),
        name="attn_out",
    )(a, bm, stream[0], stream[1], mods, wo, gffn, wrt, bias)


def _moe_dest_kernel(off_ref, eid_ref, rank_ref, dest_ref):
    eid = eid_ref[...]
    dest = rank_ref[...]
    for e in range(N_EXPERTS):
        dest = dest + jnp.where(eid == e, off_ref[e], 0)
    dest_ref[...] = dest


def _moe_dest(off, eid, rank):
    return pl.pallas_call(
        _moe_dest_kernel,
        out_shape=jax.ShapeDtypeStruct(eid.shape, jnp.int32),
        in_specs=[pl.BlockSpec(memory_space=pltpu.SMEM),
                  pl.BlockSpec(eid.shape, lambda: (0, 0, 0)), pl.BlockSpec(eid.shape, lambda: (0, 0, 0))],
        out_specs=pl.BlockSpec(eid.shape, lambda: (0, 0, 0)),
        name="moe_dest",
    )(off, eid, rank)


def _sc_mesh():
    return plsc.VectorSubcoreMesh(core_axis_name="c", subcore_axis_name="s",
                                  num_cores=V7X_SC_CORES, num_subcores=V7X_SC_SUBCORES)


def _sc_chunk(rows_per_worker):
    return max(c for c in range(8, SC_MAX_CHUNK + 1, 8) if rows_per_worker % c == 0)


def _sc_dispatch(xp, dest, p_rows):
    t, w = xp.shape
    tpw = t // V7X_SC_WORKERS
    ch = _sc_chunk(tpw)

    @functools.partial(
        pl.kernel, mesh=_sc_mesh(), out_type=jax.ShapeDtypeStruct((p_rows, w), xp.dtype),
        scratch_types=[pltpu.VMEM((ch, w), xp.dtype)] + [pltpu.VMEM((ch,), jnp.int32)] * TOP_K
        + [pltpu.SemaphoreType.DMA, pltpu.SemaphoreType.DMA],
        name="moe_dispatch")
    def run(x_hbm, dest_hbm, out_hbm, rows_v, *rest):
        idx, (sem_i, sem_o) = rest[:TOP_K], rest[TOP_K:]
        base = (lax.axis_index("s") * V7X_SC_CORES + lax.axis_index("c")) * tpw

        @pl.loop(0, tpw // ch)
        def _(i):
            t0 = base + i * ch
            loads = [pltpu.async_copy(dest_hbm.at[k, pl.ds(t0, ch)], idx[k], sem_i) for k in range(TOP_K)]
            pltpu.sync_copy(x_hbm.at[pl.ds(t0, ch)], rows_v)
            for c in loads:
                c.wait()
            puts = [pltpu.async_copy(rows_v, out_hbm.at[idx[k]], sem_o) for k in range(TOP_K)]
            for c in puts:
                c.wait()

    return run(xp, dest)


def _sc_gather(ys, dest, t):
    w = ys.shape[1]
    tpw = t // V7X_SC_WORKERS
    ch = _sc_chunk(tpw)

    @functools.partial(
        pl.kernel, mesh=_sc_mesh(), out_type=jax.ShapeDtypeStruct((TOP_K, t, w), ys.dtype),
        scratch_types=[pltpu.VMEM((ch, w), ys.dtype)] * 2 + [pltpu.VMEM((ch,), jnp.int32)] * TOP_K
        + [pltpu.SemaphoreType.DMA] * 5,
        name="moe_gather")
    def run(y_hbm, dest_hbm, out_hbm, rows_a, rows_b, *rest):
        idx, (sem_i, sem_ga, sem_gb, sem_wa, sem_wb) = rest[:TOP_K], rest[TOP_K:]
        rows, sem_g, sem_w = (rows_a, rows_b), (sem_ga, sem_gb), (sem_wa, sem_wb)
        base = (lax.axis_index("s") * V7X_SC_CORES + lax.axis_index("c")) * tpw

        @pl.loop(0, tpw // ch)
        def _(i):
            t0 = base + i * ch
            loads = [pltpu.async_copy(dest_hbm.at[k, pl.ds(t0, ch)], idx[k], sem_i) for k in range(TOP_K)]
            for c in loads:
                c.wait()
            gets, puts = [None] * TOP_K, [None] * TOP_K
            gets[0] = pltpu.async_copy(y_hbm.at[idx[0]], rows[0], sem_g[0])
            for k in range(TOP_K):
                if k + 1 < TOP_K:
                    if k >= 1:
                        puts[k - 1].wait()
                    gets[k + 1] = pltpu.async_copy(y_hbm.at[idx[k + 1]], rows[(k + 1) % 2], sem_g[(k + 1) % 2])
                gets[k].wait()
                puts[k] = pltpu.async_copy(rows[k % 2], out_hbm.at[k, pl.ds(t0, ch)], sem_w[k % 2])
            puts[TOP_K - 2].wait()
            puts[TOP_K - 1].wait()

    return run(ys, dest)


def _cache_mlp_weights(wg, wu, wd, wgu_ref, wdb_ref):
    f = wg.shape[1]
    wgu_ref[:, 0:f] = wg.astype(BF16)
    wgu_ref[:, f:] = wu.astype(BF16)
    wdb_ref[...] = wd.astype(BF16)


def _gated_mlp(xp, wgu_ref, wdb_ref):
    lo, hi = _unpack_bf16_pair(xp)
    x = jnp.concatenate([lo.astype(BF16), hi.astype(BF16)], axis=1)
    gu = jnp.dot(x, wgu_ref[...], preferred_element_type=F32)
    f = gu.shape[1] // 2
    return _dot(_silu(gu[:, :f]) * gu[:, f:], wdb_ref[...])


def _moe_expert_kernel(te_ref, tb_ref, nv_ref, x_ref, wga_ref, wua_ref, wda_ref, wgb_ref, wub_ref, wdb_ref, y_ref,
                       gu_a, dn_a, gu_b, dn_b, ids_ref):
    i = pl.program_id(0)
    tm = MOE_ROW_TILE
    nv = nv_ref[0]
    first = 2 * jnp.minimum(i, (nv - 1) // 2)
    ea = te_ref[first]
    eb = te_ref[first + 1]
    two = 2 * i + 1 < nv

    @pl.when(i == 0)
    def _():
        ids_ref[0] = -1
        ids_ref[1] = -1

    @pl.when(ids_ref[0] != ea)
    def _():
        _cache_mlp_weights(wga_ref[0, 0], wua_ref[0, 0], wda_ref[0, 0], gu_a, dn_a)
        ids_ref[0] = ea

    @pl.when(two & (eb != ea) & (ids_ref[1] != eb))
    def _():
        _cache_mlp_weights(wgb_ref[0, 0], wub_ref[0, 0], wdb_ref[0, 0], gu_b, dn_b)
        ids_ref[1] = eb

    @pl.when(two & (eb == ea))
    def _():
        y_ref[...] = _pack_bf16_pair(_gated_mlp(x_ref[...], gu_a, dn_a))

    @pl.when((2 * i < nv) & jnp.logical_not(two & (eb == ea)))
    def _():
        y_ref[0:tm, :] = _pack_bf16_pair(_gated_mlp(x_ref[0:tm, :], gu_a, dn_a))

    @pl.when(two & (eb != ea))
    def _():
        y_ref[tm:, :] = _pack_bf16_pair(_gated_mlp(x_ref[tm:, :], gu_b, dn_b))


def _moe_experts(tile_expert, n_valid, xs, wg, wu, wd, layer):
    p_rows, w = xs.shape
    tm = MOE_ROW_TILE
    _, _, d, f = wg.shape
    npair = p_rows // (2 * tm)
    pairs = tile_expert.reshape(npair, 2)
    tile_b = jnp.maximum(lax.cummax(jnp.where(pairs[:, 1] != pairs[:, 0], pairs[:, 1], -1)), 0)
    step = lambda i, nv: jnp.minimum(i, (nv[0] - 1) // 2)
    spec_a = lambda shp: pl.BlockSpec((1, 1) + shp, lambda i, te, tb, nv: (layer, te[2 * step(i, nv)], 0, 0))
    spec_b = lambda shp: pl.BlockSpec((1, 1) + shp, lambda i, te, tb, nv: (layer, tb[step(i, nv)], 0, 0))
    rows = pl.BlockSpec((2 * tm, w), lambda i, te, tb, nv: (step(i, nv), 0))
    return pl.pallas_call(
        _moe_expert_kernel,
        out_shape=jax.ShapeDtypeStruct((p_rows, w), xs.dtype),
        grid_spec=pltpu.PrefetchScalarGridSpec(
            num_scalar_prefetch=3, grid=(npair,),
            in_specs=[rows, spec_a((d, f)), spec_a((d, f)), spec_a((f, d)),
                      spec_b((d, f)), spec_b((d, f)), spec_b((f, d))],
            out_specs=rows,
            scratch_shapes=[pltpu.VMEM((d, 2 * f), BF16), pltpu.VMEM((f, d), BF16),
                            pltpu.VMEM((d, 2 * f), BF16), pltpu.VMEM((f, d), BF16), pltpu.SMEM((2,), jnp.int32)]),
        compiler_params=pltpu.CompilerParams(dimension_semantics=("arbitrary",),
                                             vmem_limit_bytes=_vmem_limit(48)),
        name="moe_experts",
    )(tile_expert, tile_b, n_valid, xs, wg, wu, wd, wg, wu, wd)


def _moe_combine_kernel(yg_ref, w_ref, xp_ref, sg_ref, su_ref, sd_ref, h_ref, mods_ref, gfin_ref, *rest, final_norm):
    o_ref, wgu_ref, wdb_ref = rest[-3:]

    @pl.when((pl.program_id(0) == 0) & (pl.program_id(1) == 0))
    def _():
        _cache_mlp_weights(sg_ref[0], su_ref[0], sd_ref[0], wgu_ref, wdb_ref)

    acc = _gated_mlp(xp_ref[0], wgu_ref, wdb_ref)
    half = acc.shape[1] // 2
    lo = acc[:, :half]
    hi = acc[:, half:]
    w = w_ref[0]
    for k in range(TOP_K):
        ylo, yhi = _unpack_bf16_pair(yg_ref[k, 0])
        wk = w[:, k:k + 1]
        lo = lo + wk * ylo
        hi = hi + wk * yhi
    y = h_ref[0] + mods_ref[0, 0, N_MODS - 1:N_MODS, :] * jnp.concatenate([lo, hi], axis=1)
    if final_norm:
        y = _rms(y, gfin_ref[...])
    o_ref[0] = y


def _moe_combine(yg, wcols, xp, sg, su, sd, h, mods, gfin, nct, layer, out_buf, out_b0, out_batch, latent_only,
                 final_norm):
    b, l, d = h.shape
    tl = TOKEN_TILE
    tile0 = nct if latent_only else 0
    tok = lambda w: pl.BlockSpec((1, tl, w), lambda i, j: (i, j + tile0, 0))
    lay = lambda x: pl.BlockSpec((1,) + x.shape[1:], lambda i, j: (layer,) + (0,) * (x.ndim - 1))
    args = [yg, wcols, xp, sg, su, sd, h, mods, gfin]
    in_specs = [pl.BlockSpec((TOP_K, 1, tl, d // 2), lambda i, j: (0, i, j + tile0, 0)), tok(GATE_W), tok(d // 2),
                lay(sg), lay(su), lay(sd), tok(d),
                pl.BlockSpec((1, 1, N_MODS, d), lambda i, j: (i, jnp.where(j + tile0 < nct, 0, 1), 0, 0)),
                pl.BlockSpec(gfin.shape, lambda i, j: (0, 0))]
    aliases = {}
    if out_buf is not None:
        args.append(out_buf)
        in_specs.append(pl.BlockSpec(memory_space=pl.ANY))
        aliases = {len(args) - 1: 0}
    return pl.pallas_call(
        functools.partial(_moe_combine_kernel, final_norm=final_norm),
        out_shape=jax.ShapeDtypeStruct((out_batch, l - tile0 * tl, d), F32),
        grid=(b, l // tl - tile0),
        in_specs=in_specs,
        out_specs=pl.BlockSpec((1, tl, d), lambda i, j: (i + out_b0, j, 0)),
        scratch_shapes=[pltpu.VMEM((d, 2 * sg.shape[2]), BF16), pltpu.VMEM((sg.shape[2], d), BF16)],
        input_output_aliases=aliases,
        compiler_params=pltpu.CompilerParams(dimension_semantics=("arbitrary", "arbitrary"),
                                             vmem_limit_bytes=_vmem_limit(40)),
        name="moe_combine",
    )(*args)


def _moe_sparse(n2p, eid, rank, wcols, counts, h, mods, wg, wu, wd, sg, su, sd, gfin, nct, layer,
                out_buf=None, out_b0=0, out_batch=None, last=False):
    b, l, d = h.shape
    t = b * l
    tm = MOE_ROW_TILE
    n_tiles = 2 * -(-(TOP_K * t + N_EXPERTS * (tm - 1)) // (2 * tm))
    tiles_e = (counts.reshape(N_EXPERTS).astype(jnp.int32) + (tm - 1)) // tm
    tile_end = jnp.cumsum(tiles_e)
    off = (tile_end - tiles_e) * tm
    n_valid = tile_end[-1:]
    tile_id = jnp.minimum(jnp.arange(n_tiles, dtype=jnp.int32), n_valid - 1)
    tile_expert = jnp.sum((tile_end[None, :] <= tile_id[:, None]).astype(jnp.int32), axis=1)
    dest = _moe_dest(off, eid, rank).transpose(1, 0, 2).reshape(8, t)
    xs = _sc_dispatch(n2p.reshape(t, d // 2), dest, n_tiles * tm)
    ys = _moe_experts(tile_expert, n_valid, xs, wg, wu, wd, layer)
    yg = _sc_gather(ys, dest, t).reshape(TOP_K, b, l, d // 2)
    return _moe_combine(yg, wcols, n2p, sg, su, sd, h, mods, gfin, nct, layer, out_buf, out_b0,
                        b if out_batch is None else out_batch, last, last)


def _rwkv_proj_kernel(h_ref, hp_ref, hx_ref, mods_ref, g_ref, mu_ref, wr_ref, wk_ref, wv_ref, g1_ref, g2_ref,
                      w1_ref, w2_ref, a1_ref, a2_ref, w0_ref, a0_ref, kk_ref, ka_ref, rk_ref, bd_ref,
                      r_out, v_out, kk_out, g_out, km_out, b_out, lw_out, bonus_out, *, nct):
    j = pl.program_id(1)
    nt = pl.num_programs(1)
    m = mods_ref[0, 0]
    g = g_ref[...]
    n = _norm_mod(h_ref[0], g, m[0:1], m[1:2])
    tl, d = n.shape
    seg_first = (j == 0) | (j == nct)
    seg_last = (j == nct - 1) | (j == nt - 1)
    n_prev = _norm_mod(hp_ref[0], g, m[0:1], m[1:2])[7:8] * jnp.where(seg_first, 0.0, 1.0)
    n_next = _norm_mod(hx_ref[0], g, m[0:1], m[1:2])[0:1] * jnp.where(seg_last, 0.0, 1.0)
    row = lax.broadcasted_iota(jnp.int32, (tl, 1), 0)
    prev = jnp.where(row == 0, n_prev, pltpu.roll(n, 1, axis=0))
    nxt = jnp.where(row == tl - 1, n_next, pltpu.roll(n, tl - 1, axis=0))
    lane = lax.broadcasted_iota(jnp.int32, (1, d), 1)
    xx = jnp.where(lane < d // 2, prev, nxt) - n
    mu = mu_ref[...]
    xr, xw, xk, xv, xa, xg = [n + xx * mu[i:i + 1] for i in range(6)]
    r = _dot(xr, wr_ref[...])
    k = _dot(xk, wk_ref[...])
    v = _dot(xv, wv_ref[...])
    g_out[0] = _dot(_sigmoid(_dot(xg, g1_ref[...])), g2_ref[...]).astype(g_out.dtype)
    tw = jnp.tanh(_dot(xw, w1_ref[...]))
    ta = _dot(xa, a1_ref[...])
    bd = bd_ref[...]
    kk = k * kk_ref[...]
    kk = kk / jnp.maximum(jnp.sqrt(_head_sum(kk * kk, bd)), 1e-12)
    r_out[0] = r.astype(r_out.dtype)
    v_out[0] = v.astype(v_out.dtype)
    kk_out[0] = kk.astype(kk_out.dtype)
    bonus = jnp.zeros_like(v)
    for dr in range(2):
        zw = w0_ref[dr:dr + 1, :] + _dot(tw, w2_ref[dr])
        lw_out[dr, 0] = -jnp.exp(-0.5) * _sigmoid(zw)
        a = _sigmoid(a0_ref[dr:dr + 1, :] + _dot(ta, a2_ref[dr]))
        km = k * (1.0 + (a - 1.0) * ka_ref[...])
        km_out[dr, 0] = km.astype(km_out.dtype)
        b_out[dr, 0] = (kk * a).astype(b_out.dtype)
        bonus = bonus + _head_sum(r * km * rk_ref[...], bd) * v
    bonus_out[0] = bonus


def _rwkv_proj(h, mods, g, mu, wr, wk, wv, g1, g2, w1, w2, a1, a2, w0, a0, kk, ka, rk, bd, nct):
    b, l, d = h.shape
    tl = TOKEN_TILE
    nb8 = l // 8
    tok = pl.BlockSpec((1, tl, d), lambda i, j: (i, j, 0))
    tok2 = pl.BlockSpec((2, 1, tl, d), lambda i, j: (0, i, j, 0))
    full = lambda x: pl.BlockSpec(x.shape, lambda i, j: (0,) * x.ndim)
    sds = jax.ShapeDtypeStruct
    return pl.pallas_call(
        functools.partial(_rwkv_proj_kernel, nct=nct),
        out_shape=[sds((b, l, d), BF16), sds((b, l, d), BF16), sds((b, l, d), BF16), sds((b, l, d), BF16),
                   sds((2, b, l, d), BF16), sds((2, b, l, d), BF16), sds((2, b, l, d), F32), sds((b, l, d), F32)],
        grid=(b, l // tl),
        in_specs=[tok,
                  pl.BlockSpec((1, 8, d), lambda i, j: (i, jnp.maximum(j * (tl // 8) - 1, 0), 0)),
                  pl.BlockSpec((1, 8, d), lambda i, j: (i, jnp.minimum((j + 1) * (tl // 8), nb8 - 1), 0)),
                  pl.BlockSpec((1, 1, N_MODS, d), lambda i, j: (i, jnp.where(j < nct, 0, 1), 0, 0)),
                  full(g), full(mu), full(wr), full(wk), full(wv), full(g1), full(g2), full(w1), full(w2),
                  full(a1), full(a2), full(w0), full(a0), full(kk), full(ka), full(rk), full(bd)],
        out_specs=[tok, tok, tok, tok, tok2, tok2, tok2, tok],
        compiler_params=pltpu.CompilerParams(dimension_semantics=("parallel", "parallel"),
                                             vmem_limit_bytes=_vmem_limit(56)),
        name="rwkv_proj",
    )(h, h, h, mods, g, mu, wr, wk, wv, g1, g2, w1, w2, a1, a2, w0, a0, kk, ka, rk, bd)


def _wkv_kernel(r_ref, v_ref, kk_ref, km_ref, b_ref, lw_ref, y_ref, st_ref):
    c = WKV_CHUNK
    w = WKV_PAIR
    rev = pl.program_id(0)
    sign = 1 - 2 * rev

    @pl.when(pl.program_id(2) == 0)
    def _():
        st_ref[...] = jnp.zeros_like(st_ref)

    ti = lax.broadcasted_iota(jnp.int32, (c, c), 0)
    si = lax.broadcasted_iota(jnp.int32, (c, c), 1)
    tri = jnp.where((si - ti) * sign <= 0, 1.0, 0.0).astype(F32)
    nsub = WKV_CHUNKS_PER_STEP
    subs = [pl.ds(pl.multiple_of(jnp.where(rev == 0, s, nsub - 1 - s) * c, c), c) for s in range(nsub)]
    rt, kt, kh, bh, v32, e_mid = [], [], [], [], [], []
    for rows in subs:
        lw = lw_ref[0, 0, rows, :]
        l_incl = jnp.dot(tri, lw, precision=HIGHEST, preferred_element_type=F32)
        mid = 0.5 * jnp.sum(lw, axis=0, keepdims=True)
        e_neg = jnp.exp(mid - l_incl)
        e_mid.append(jnp.exp(mid))
        rt.append(r_ref[0, rows, :].astype(F32) * jnp.exp(l_incl - mid))
        kt.append(kk_ref[0, rows, :].astype(F32) * jnp.exp(l_incl - lw - mid))
        kh.append(km_ref[0, 0, rows, :].astype(F32) * e_neg)
        bh.append(b_ref[0, 0, rows, :].astype(F32) * e_neg)
        v32.append(v_ref[0, rows, :].astype(F32))

    ri = lax.broadcasted_iota(jnp.int32, (w, w), 0)
    ci = lax.broadcasted_iota(jnp.int32, (w, w), 1)
    same = (ri // c) == (ci // c)
    eye = jnp.where(ri == ci, 1.0, 0.0).astype(F32)
    tl_ = lax.broadcasted_iota(jnp.int32, (c, w), 0)
    jl_ = lax.broadcasted_iota(jnp.int32, (c, w), 1) % c
    strict = (jl_ - tl_) * sign < 0
    incl = (jl_ - tl_) * sign <= 0
    eye2 = jnp.where(jl_ == tl_, 1.0, 0.0).astype(F32)
    lane = lax.broadcasted_iota(jnp.int32, (1, w), 1)
    h0 = lane < RWKV_HEAD

    def rows2(x):
        return jnp.concatenate([jnp.where(h0, x, 0.0), jnp.where(h0, 0.0, x)], axis=0)

    npair = st_ref.shape[0]
    items = [(s, slice(p * w, (p + 1) * w)) for s in range(nsub) for p in range(npair)]
    n = range(len(items))
    em = [e_mid[s][:, sl] for s, sl in items]
    g = [_dot_nt(jnp.concatenate([kt[s][:, sl], rt[s][:, sl]], axis=0),
                 jnp.concatenate([rows2(kh[s][:, sl]), rows2(bh[s][:, sl])], axis=0)) for s, sl in items]
    a_kk = [jnp.where(strict, x[:c, :w], 0.0) for x in g]
    a_rk = [jnp.where(incl, x[c:, :w], 0.0) for x in g]
    a_rb = [jnp.where(incl, x[c:, w:], 0.0) for x in g]
    vi = [v32[s][:, sl] for s, sl in items]
    v_rows = [rows2(x) for x in vi]
    r_pre = [_dot(a_kk[i], v_rows[i]) for i in n]
    m = [jnp.where(strict, -x[:c, w:], 0.0) for x in g]
    tinv = [eye2 + x for x in m]
    m = [_dot(x, rows2(x)) for x in m]
    for _ in range(c.bit_length() - 3):
        both = [_dot(jnp.concatenate([tinv[i], m[i]], axis=0), rows2(m[i])) for i in n]
        tinv = [tinv[i] + both[i][:c] for i in n]
        m = [x[c:] for x in both]
    tinv = [tinv[i] + _dot(tinv[i], rows2(m[i])) for i in n]
    sol = [_dot(tinv[i], jnp.concatenate([rows2(r_pre[i]), rows2(kt[s][:, sl] * em[i])], axis=1))
           for i, (s, sl) in enumerate(items)]
    u_pre = [x[:, :w] for x in sol]
    kq = [x[:, w:] for x in sol]
    y_pre = [_dot(jnp.concatenate([a_rk[i], -a_rb[i]], axis=1),
                  jnp.concatenate([v_rows[i], rows2(u_pre[i])], axis=0)) for i in n]
    r_eff = [rt[s][:, sl] * em[i] - _dot(a_rb[i], rows2(kq[i])) for i, (s, sl) in enumerate(items)]
    bbar = [bh[s][:, sl] * em[i] for i, (s, sl) in enumerate(items)]
    kbar = [kh[s][:, sl] * em[i] for i, (s, sl) in enumerate(items)]
    mmat = [eye * (em[i] * em[i]) - jnp.where(same, _dot_tn(kq[i], bbar[i]), 0.0) for i in n]
    s_pre = [jnp.where(same, _dot_tn(jnp.concatenate([vi[i], -u_pre[i]], axis=0),
                                     jnp.concatenate([kbar[i], bbar[i]], axis=0)), 0.0) for i in n]
    st = [st_ref[p] for p in range(npair)]
    for i, (s, sl) in enumerate(items):
        p = i % npair
        y_ref[0, 0, subs[s], sl] = _dot_nt(r_eff[i], st[p]) + y_pre[i]
        hi = st[p].astype(BF16)
        lo = (st[p] - hi.astype(F32)).astype(BF16)
        mb = mmat[i].astype(BF16)
        st[p] = (jnp.dot(hi, mb, preferred_element_type=F32) + jnp.dot(lo, mb, preferred_element_type=F32)
                 + s_pre[i])
    for p in range(npair):
        st_ref[p] = st[p]


def _wkv(r, v, kk, km, bv, lw, lc):
    b, l, d = r.shape
    c = WKV_CHUNK * WKV_CHUNKS_PER_STEP
    ncc = lc // c
    nlc = (l - lc) // c

    def chunk(dr, i):
        return jnp.where(dr == 0, i, jnp.where(i < ncc, ncc - 1 - i, nlc + 2 * ncc - 1 - i))

    shared = pl.BlockSpec((1, c, d), lambda dr, bi, i: (bi, chunk(dr, i), 0))
    per_dir = pl.BlockSpec((1, 1, c, d), lambda dr, bi, i: (dr, bi, chunk(dr, i), 0))
    return pl.pallas_call(
        _wkv_kernel,
        out_shape=jax.ShapeDtypeStruct((2, b, l, d), F32),
        grid=(2, b, l // c),
        in_specs=[shared, shared, shared, per_dir, per_dir, per_dir],
        out_specs=per_dir,
        scratch_shapes=[pltpu.VMEM((d // WKV_PAIR, WKV_PAIR, WKV_PAIR), F32)],
        compiler_params=pltpu.CompilerParams(dimension_semantics=("parallel", "parallel", "arbitrary"),
                                             vmem_limit_bytes=_vmem_limit(32)),
        name="wkv7_chunked",
    )(r, v, kk, km, bv, lw)


def _rwkv_out_kernel(y_ref, bonus_ref, g_ref, lnw_ref, lnb_ref, wo_ref, bd_ref, h_ref, mods_ref, gffn_ref,
                     wrt_ref, bias_ref, hn_ref, n2_ref, eid_ref, rank_ref, w_ref, cnt_ref, run_ref):
    y = y_ref[0, 0] + y_ref[1, 0]
    bd = bd_ref[...]
    mean = _head_sum(y, bd) * (1.0 / RWKV_HEAD)
    yc = y - mean
    var = _head_sum(yc * yc, bd) * (1.0 / RWKV_HEAD)
    yn = yc * lax.rsqrt(var + GN_EPS) * lnw_ref[...] + lnb_ref[...]
    out = (yn + bonus_ref[0]) * g_ref[0].astype(F32)
    _mixer_tail(_dot(out, wo_ref[...]), h_ref[0], mods_ref[0, 0], gffn_ref, wrt_ref, bias_ref, hn_ref, n2_ref,
                eid_ref, rank_ref, w_ref, cnt_ref, run_ref)


def _rwkv_out(y, bonus, g, lnw, lnb, wo, bd, h, mods, gffn, wrt, bias, nct):
    b, l, d = h.shape
    tl = TOKEN_TILE
    tok = lambda w: pl.BlockSpec((1, tl, w), lambda i, j: (i, j, 0))
    full = lambda x: pl.BlockSpec(x.shape, lambda i, j: (0,) * x.ndim)
    shapes, specs = _tail_outs(b, l, d)
    return pl.pallas_call(
        _rwkv_out_kernel,
        out_shape=shapes,
        grid=(b, l // tl),
        in_specs=[pl.BlockSpec((2, 1, tl, d), lambda i, j: (0, i, j, 0)), tok(d), tok(d),
                  full(lnw), full(lnb), full(wo), full(bd), tok(d),
                  pl.BlockSpec((1, 1, N_MODS, d), lambda i, j: (i, jnp.where(j < nct, 0, 1), 0, 0)),
                  full(gffn), full(wrt), full(bias)],
        out_specs=specs,
        scratch_shapes=[pltpu.VMEM((N_EXPERTS, 1), F32)],
        compiler_params=pltpu.CompilerParams(dimension_semantics=("arbitrary", "arbitrary"),
                                             vmem_limit_bytes=_vmem_limit(40)),
        name="rwkv_out",
    )(y, bonus, g, lnw, lnb, wo, bd, h, mods, gffn, wrt, bias)


def _rope_table(n_lat, n_ctx):
    dim = SWA_HEAD_DIM
    nf = dim // 4
    inv = ROPE_THETA ** (-jnp.arange(nf, dtype=F32) / nf)
    row = jnp.repeat(jnp.arange(n_lat // GRID_W, dtype=F32), GRID_W)
    col = jnp.tile(jnp.arange(GRID_W, dtype=F32), n_lat // GRID_W)
    ar = row[:, None] * inv
    ac = col[:, None] * inv
    ang = jnp.concatenate([ar, ar, ac, ac], axis=-1)
    cos = jnp.concatenate([jnp.ones((n_ctx, dim), F32), jnp.cos(ang)], axis=0)
    sin = jnp.concatenate([jnp.zeros((n_ctx, dim), F32), jnp.sin(ang)], axis=0)
    return jnp.tile(cos, (1, 2)), jnp.tile(sin, (1, 2))


def _layout_attn_weights(w_in, w_uq, w_ukv):
    d = w_in.shape[0]
    s0 = MLA_Q_RANK
    s1 = s0 + MLA_KV_RANK
    s2 = s1 + MLA_ROPE
    s3 = s2 + SWA_HEADS * SWA_HEAD_DIM
    s4 = s3 + SWA_KV_HEADS * SWA_HEAD_DIM
    rep = lambda w: jnp.concatenate(
        [jnp.tile(w[:, g * SWA_HEAD_DIM:(g + 1) * SWA_HEAD_DIM], (1, SWA_GROUP)) for g in range(SWA_KV_HEADS)], axis=1)
    win = jnp.concatenate([w_in[:, :s1], w_in[:, s2:s3], rep(w_in[:, s3:s4]), rep(w_in[:, s4:]),
                           w_in[:, s1:s2], jnp.zeros((d, V7X_LANES - MLA_ROPE), w_in.dtype)], axis=1)
    qh = MLA_NOPE + MLA_ROPE
    pad = jnp.zeros((w_uq.shape[0], V7X_MXU_DIM - qh), w_uq.dtype)
    wuq = jnp.concatenate([jnp.concatenate([w_uq[:, h * qh:(h + 1) * qh], pad], axis=1) for h in range(MLA_HEADS)], axis=1)
    kvh = MLA_NOPE + MLA_V
    wuk = jnp.concatenate([w_ukv[:, h * kvh:h * kvh + MLA_NOPE] for h in range(MLA_HEADS)], axis=1)
    wuvt = jnp.concatenate([w_ukv[:, h * kvh + MLA_NOPE:(h + 1) * kvh] for h in range(MLA_HEADS)], axis=1).T
    return win.astype(BF16), wuq.astype(BF16), wuk.astype(BF16), wuvt.astype(BF16)


def _lora_pair(w_down, w_up):
    rank = w_down.shape[2]
    down = jnp.concatenate([w_down[0], w_down[1]], axis=1)
    z = jnp.zeros((rank, w_up.shape[2]), w_up.dtype)
    up = jnp.stack([jnp.concatenate([w_up[0], z], axis=0), jnp.concatenate([z, w_up[1]], axis=0)], axis=0)
    return down.astype(BF16), up.astype(BF16)


def _head_block_diag():
    i = jnp.arange(V7X_MXU_DIM) // RWKV_HEAD
    return (i[:, None] == i[None, :]).astype(BF16)


def kernel(x, c, ctx, c_ctx, ada_w, ada_b, norm_mix, norm_ffn, norm_final, attn_w_in, attn_q_norm, attn_kv_norm, attn_w_uq, attn_w_ukv, attn_sinks, attn_w_o, rwkv_mu, rwkv_w_r, rwkv_w_k, rwkv_w_v, rwkv_w_o, rwkv_g1, rwkv_g2, rwkv_w0, rwkv_w1, rwkv_w2, rwkv_a0, rwkv_a1, rwkv_a2, rwkv_k_k, rwkv_k_a, rwkv_r_k, rwkv_ln_w, rwkv_ln_b, moe_router, moe_bias, moe_w_gate, moe_w_up, moe_w_down, moe_ws_gate, moe_ws_up, moe_ws_down):
    bsz, s, d = x.shape
    lc = ctx.shape[1]
    l = lc + s
    depth = ada_w.shape[0]
    nct = lc // TOKEN_TILE
    assert lc % TOKEN_TILE == 0 and s % TOKEN_TILE == 0 and s >= SWA_BAND and lc % SWA_Q_TILE == 0
    assert lc % (WKV_CHUNK * WKV_CHUNKS_PER_STEP) == 0
    assert d % V7X_MXU_DIM == 0 and WKV_CHUNK * 2 == V7X_LANES
    ngrp = SAMPLE_GROUPS
    bg = bsz // ngrp
    assert bsz % ngrp == 0 and (bg * l) % (8 * V7X_SC_WORKERS) == 0

    streams = [(ctx, x, g * bg, 0) for g in range(ngrp)]
    out = None
    cos, sin = _rope_table(s, lc)
    bd = _head_block_diag()
    rows = -(-(bsz + 1) // 8) * 8
    cc = jnp.concatenate([c, c_ctx[None, :], jnp.zeros((rows - bsz - 1, d), F32)], axis=0)
    row2 = lambda a: a.reshape(1, -1)

    for li in range(depth):
        with_ctx = li < depth - 1
        i = li // 2
        ada = _ada_mods(cc, ada_w, ada_b, li)
        mods_all = jnp.stack([jnp.broadcast_to(ada[bsz].reshape(1, N_MODS, d), (bsz, N_MODS, d)),
                              ada[:bsz].reshape(bsz, N_MODS, d)], axis=1)
        wrt = jnp.concatenate([moe_router[li].T, jnp.zeros((GATE_W - N_EXPERTS, d), F32)], axis=0)
        bias = moe_bias[li].reshape(N_GROUPS, GROUP_SIZE, 1)
        if li % 2 == 0:
            win, wuq, wuk, wuvt = _layout_attn_weights(attn_w_in[i], attn_w_uq[i], attn_w_ukv[i])
            wo = attn_w_o[i].astype(BF16)
        else:
            w1, w2 = _lora_pair(rwkv_w1[i], rwkv_w2[i])
            a1, a2 = _lora_pair(rwkv_a1[i], rwkv_a2[i])
            wr, wk, wv, wo = [w[i].astype(BF16) for w in (rwkv_w_r, rwkv_w_k, rwkv_w_v, rwkv_w_o)]
            g1, g2 = rwkv_g1[i].astype(BF16), rwkv_g2[i].astype(BF16)
        for g in range(ngrp):
            mods = mods_all[g * bg:(g + 1) * bg]
            if li % 2 == 0:
                q, k, vt, qs, ks, vs = _attn_proj(streams[g], bg, l, mods, row2(norm_mix[li]), win,
                                                  row2(attn_q_norm[i]), row2(attn_kv_norm[i]), wuq, wuk, wuvt,
                                                  cos, sin, nct)
                a = _mla_attention(q, k, vt, lc, 0 if with_ctx else lc // MLA_Q_TILE)
                bm = _swa_attention(attn_sinks[i], qs, ks, vs, lc, 0 if with_ctx else lc // SWA_Q_TILE)
                tail = _attn_out(a, bm, streams[g], mods, wo, row2(norm_ffn[li]), wrt, bias, nct)
            else:
                h = streams[g][0]
                assert streams[g][0] is streams[g][1]
                r, v, kk, gt, km, bv, lw, bonus = _rwkv_proj(
                    h, mods, row2(norm_mix[li]), rwkv_mu[i], wr, wk, wv, g1, g2, w1, w2, a1, a2,
                    rwkv_w0[i], rwkv_a0[i], row2(rwkv_k_k[i]), row2(rwkv_k_a[i]), row2(rwkv_r_k[i]), bd, nct)
                y = _wkv(r, v, kk, km, bv, lw, lc)
                tail = _rwkv_out(y, bonus, gt, row2(rwkv_ln_w[i]), row2(rwkv_ln_b[i]), wo,
                                 bd, h, mods, row2(norm_ffn[li]), wrt, bias, nct)
            h, n2p, eid, rank, wcols, counts = tail
            moe_w = (moe_w_gate, moe_w_up, moe_w_down, moe_ws_gate, moe_ws_up, moe_ws_down)
            if li < depth - 1:
                h = _moe_sparse(n2p, eid, rank, wcols, counts, h, mods, *moe_w, row2(norm_final), nct, li)
                streams[g] = (h, h, 0, nct)
            else:
                out = _moe_sparse(n2p, eid, rank, wcols, counts, h, mods, *moe_w, row2(norm_final), nct, li,
                                  out_buf=out, out_b0=g * bg, out_batch=bsz, last=True)
    return out
```

```python
import functools

import jax
import jax.numpy as jnp
from jax import lax
from jax.experimental import pallas as pl
from jax.experimental.pallas import tpu as pltpu
from jax.experimental.pallas import tpu_sc as plsc

F32 = jnp.float32
BF16 = jnp.bfloat16
HIGHEST = lax.Precision.HIGHEST

GRID_W = 64
NORM_EPS = 1e-6
ROPE_THETA = 10000.0
NEG_INF = -1e30
N_MODS = 6

MLA_HEADS = 4
MLA_Q_RANK = 384
MLA_KV_RANK = 256
MLA_NOPE = 128
MLA_ROPE = 64
MLA_V = 128

SWA_HEADS = 8
SWA_KV_HEADS = 2
SWA_GROUP = SWA_HEADS // SWA_KV_HEADS
SWA_HEAD_DIM = 64
WINDOW = 128

RWKV_HEAD = 64
DECAY_LORA = 64
ICLR_LORA = 64
GATE_LORA = 128
GN_EPS = 64e-5

N_EXPERTS = 64
TOP_K = 6
N_GROUPS = 8
TOPK_GROUPS = 4
GROUP_SIZE = N_EXPERTS // N_GROUPS
ROUTED_SCALE = 2.5
GATE_W = 128

V7X_LANES = 128
V7X_MXU_DIM = 256
V7X_VMEM_BYTES = 64 * 1024 * 1024
V7X_SC_CORES = 2
V7X_SC_SUBCORES = 16
V7X_SC_WORKERS = V7X_SC_CORES * V7X_SC_SUBCORES

TOKEN_TILE = 256
MLA_Q_TILE = 256
MLA_HEADS_PER_STEP = 2
SWA_Q_TILE = 256
SWA_BAND = SWA_Q_TILE + 2 * WINDOW
WKV_CHUNK = 64
WKV_PAIR = 2 * RWKV_HEAD
WKV_CHUNKS_PER_STEP = 4
MOE_ROW_TILE = 512
SAMPLE_GROUPS = 2
SC_MAX_CHUNK = 64

LOG2E = 1.4426950408889634
MIB = 1024 * 1024
VMEM_RESERVE_BYTES = 4 * MIB


def _vmem_limit(mib):
    return min(mib * MIB, V7X_VMEM_BYTES - VMEM_RESERVE_BYTES)


def _dot(a, b):
    return jnp.dot(a.astype(BF16), b.astype(BF16), preferred_element_type=F32)


def _dot_nt(a, b):
    return lax.dot_general(a.astype(BF16), b.astype(BF16), (((1,), (1,)), ((), ())),
                           preferred_element_type=F32)


def _dot_tn(a, b):
    return lax.dot_general(a.astype(BF16), b.astype(BF16), (((0,), (0,)), ((), ())),
                           preferred_element_type=F32)


def _sigmoid(x):
    return 1.0 / (1.0 + jnp.exp(-x))


def _silu(x):
    return x * _sigmoid(x)


def _rms(x, g):
    return x * lax.rsqrt(jnp.mean(x * x, axis=-1, keepdims=True) + NORM_EPS) * g


def _norm_mod(x, g, shift, scale):
    return _rms(x, g) * (1.0 + scale) + shift


def _split_dot(x, w):
    hi = x.astype(BF16)
    lo = (x - hi.astype(F32)).astype(BF16)
    return (jnp.dot(hi, w, preferred_element_type=F32) + jnp.dot(lo, w, preferred_element_type=F32))


def _head_sum(x, bd):
    w = bd.shape[0]
    parts = [_split_dot(x[:, c * w:(c + 1) * w], bd) for c in range(x.shape[1] // w)]
    return jnp.concatenate(parts, axis=1)


def _after(dep, kernel, in_specs, args, n_lead=0):
    if dep is None:
        return kernel, list(in_specs), list(args)
    n_in = n_lead + len(in_specs)

    def ordered(*refs):
        return kernel(*refs[:n_in], *refs[n_in + 1:])

    return ordered, list(in_specs) + [pl.BlockSpec(memory_space=pl.ANY)], list(args) + [dep]


def _ada_kernel(c_ref, w_ref, b_ref, o_ref):
    s = _silu(c_ref[...])
    o_ref[...] = jnp.dot(s, w_ref[0], precision=HIGHEST, preferred_element_type=F32) + b_ref[0]


def _ada_mods(cc, w, b, layer):
    rows, d = cc.shape
    depth, _, n = w.shape
    return pl.pallas_call(
        _ada_kernel,
        out_shape=jax.ShapeDtypeStruct((rows, n), F32),
        grid=(n // d,),
        in_specs=[pl.BlockSpec((rows, d), lambda i: (0, 0)),
                  pl.BlockSpec((1, d, d), lambda i: (layer, 0, i)),
                  pl.BlockSpec((1, 1, d), lambda i: (layer, 0, i))],
        out_specs=pl.BlockSpec((rows, d), lambda i: (0, i)),
        compiler_params=pltpu.CompilerParams(dimension_semantics=("parallel",),
                                             vmem_limit_bytes=_vmem_limit(32)),
        name="ada_mods",
    )(cc, w, b.reshape(depth, 1, n))


def _rope128(x, cos, sin, first_half):
    rot = jnp.where(first_half, -pltpu.roll(x, V7X_LANES - 16, axis=1), pltpu.roll(x, 16, axis=1))
    return x * cos + rot * sin


_C_CQ = 0
_C_CKV = _C_CQ + MLA_Q_RANK
_C_QS = _C_CKV + MLA_KV_RANK
_C_KS = _C_QS + SWA_HEADS * SWA_HEAD_DIM
_C_VS = _C_KS + SWA_KV_HEADS * V7X_MXU_DIM
_C_KR = _C_VS + SWA_KV_HEADS * V7X_MXU_DIM
_C_END = _C_KR + V7X_LANES
_SWA_W = SWA_KV_HEADS * V7X_MXU_DIM
_MLA_QK_W = MLA_HEADS * V7X_MXU_DIM


def _stream_specs(stream, nct, tl):
    ctx_arr, lat_arr, b0, lat_off = stream
    d = ctx_arr.shape[2]
    return [pl.BlockSpec((1, tl, d), lambda i, j: (i + b0, jnp.minimum(j, nct - 1), 0)),
            pl.BlockSpec((1, tl, d), lambda i, j: (i + b0, jnp.maximum(j - nct, 0) + lat_off, 0))]


def _stream_tile(c_ref, x_ref, nct):
    rows = c_ref.shape[1]
    take_ctx = lax.broadcasted_iota(jnp.int32, (rows, 1), 0) < jnp.where(pl.program_id(1) < nct, rows, 0)
    return jnp.where(take_ctx, c_ref[0], x_ref[0])


def _attn_proj_kernel(c_ref, x_ref, mods_ref, g_ref, win_ref, qn_ref, kvn_ref, wuq_ref, wuk_ref, wuvt_ref, cos_ref,
                      sin_ref, q_ref, k_ref, vt_ref, qs_ref, ks_ref, vs_ref, *, nct):
    m = mods_ref[0, 0]
    n = _norm_mod(_stream_tile(c_ref, x_ref, nct), g_ref[...], m[0:1], m[1:2])
    u = _dot(n, win_ref[...])
    cos = cos_ref[...]
    sin = sin_ref[...]
    lane = lax.broadcasted_iota(jnp.int32, (1, V7X_LANES), 1)
    first_half = (lane % 32) < 16

    def rope(x):
        return _rope128(x, cos, sin, first_half)

    scale_a = (MLA_NOPE + MLA_ROPE) ** -0.5 * LOG2E
    scale_b = SWA_HEAD_DIM ** -0.5 * LOG2E
    q = _dot(_rms(u[:, _C_CQ:_C_CKV], qn_ref[...]), wuq_ref[...])
    ckv = _rms(u[:, _C_CKV:_C_QS], kvn_ref[...])
    kn = _dot(ckv, wuk_ref[...])
    vt_ref[0] = _dot_nt(wuvt_ref[...], ckv).astype(BF16)
    kr = rope(u[:, _C_KR:_C_END]).astype(BF16)
    for h in range(MLA_HEADS):
        o = h * V7X_MXU_DIM
        q_ref[0, :, o:o + V7X_LANES] = (q[:, o:o + V7X_LANES] * scale_a).astype(BF16)
        q_ref[0, :, o + V7X_LANES:o + V7X_MXU_DIM] = (rope(q[:, o + V7X_LANES:o + V7X_MXU_DIM]) * scale_a).astype(BF16)
        k_ref[0, :, o:o + V7X_LANES] = kn[:, h * MLA_NOPE:(h + 1) * MLA_NOPE].astype(BF16)
        k_ref[0, :, o + V7X_LANES:o + V7X_MXU_DIM] = kr
    for c in range((_C_KS - _C_QS) // V7X_LANES):
        o = c * V7X_LANES
        qs_ref[0, :, o:o + V7X_LANES] = (rope(u[:, _C_QS + o:_C_QS + o + V7X_LANES]) * scale_b).astype(BF16)
    for c in range(_SWA_W // V7X_LANES):
        o = c * V7X_LANES
        ks_ref[0, :, o:o + V7X_LANES] = rope(u[:, _C_KS + o:_C_KS + o + V7X_LANES]).astype(BF16)
    vs_ref[0] = u[:, _C_VS:_C_KR].astype(BF16)


def _attn_proj(stream, b, l, mods, g, win, qn, kvn, wuq, wuk, wuvt, cos, sin, nct, dep=None):
    d = stream[0].shape[2]
    tl = TOKEN_TILE
    tok = lambda w: pl.BlockSpec((1, tl, w), lambda i, j: (i, j, 0))
    full = lambda a: pl.BlockSpec(a.shape, lambda i, j: (0,) * a.ndim)
    sds = jax.ShapeDtypeStruct
    dv = MLA_HEADS * MLA_V
    kern, in_specs, args = _after(
        dep, functools.partial(_attn_proj_kernel, nct=nct),
        _stream_specs(stream, nct, tl) + [
            pl.BlockSpec((1, 1, N_MODS, d), lambda i, j: (i, jnp.where(j < nct, 0, 1), 0, 0)),
            full(g), full(win), full(qn), full(kvn), full(wuq), full(wuk), full(wuvt),
            pl.BlockSpec((tl, V7X_LANES), lambda i, j: (j, 0)),
            pl.BlockSpec((tl, V7X_LANES), lambda i, j: (j, 0))],
        [stream[0], stream[1], mods, g, win, qn, kvn, wuq, wuk, wuvt, cos, sin])
    return pl.pallas_call(
        kern,
        out_shape=[sds((b, l, _MLA_QK_W), BF16), sds((b, l, _MLA_QK_W), BF16), sds((b, dv, l), BF16),
                   sds((b, l, SWA_HEADS * SWA_HEAD_DIM), BF16), sds((b, l, _SWA_W), BF16), sds((b, l, _SWA_W), BF16)],
        grid=(b, l // tl),
        in_specs=in_specs,
        out_specs=[tok(_MLA_QK_W), tok(_MLA_QK_W), pl.BlockSpec((1, dv, tl), lambda i, j: (i, 0, j)),
                   tok(SWA_HEADS * SWA_HEAD_DIM), tok(_SWA_W), tok(_SWA_W)],
        compiler_params=pltpu.CompilerParams(dimension_semantics=("parallel", "parallel"),
                                             vmem_limit_bytes=_vmem_limit(48)),
        name="attn_proj",
    )(*args)


def _mla_kernel(q_ref, k_ref, vt_ref, o_ref, *, nct_q, lc):
    hw = V7X_MXU_DIM

    def attend(nk):
        st = [_dot_nt(k_ref[0, 0:nk, hh * hw:(hh + 1) * hw], q_ref[0, :, hh * hw:(hh + 1) * hw])
              for hh in range(MLA_HEADS_PER_STEP)]
        for hh, s in enumerate(st):
            p = jnp.exp2(s - jnp.max(s, axis=0, keepdims=True))
            den = jnp.sum(p, axis=0, keepdims=True)
            ot = _dot(vt_ref[0, hh * MLA_V:(hh + 1) * MLA_V, 0:nk], p) / den
            o_ref[0, :, hh * MLA_V:(hh + 1) * MLA_V] = ot.T.astype(o_ref.dtype)

    @pl.when(pl.program_id(2) < nct_q)
    def _():
        attend(lc)

    @pl.when(pl.program_id(2) >= nct_q)
    def _():
        attend(k_ref.shape[1])


def _mla_attention(q, k, vt, lc, q_tile0, dep=None):
    b, l, _ = q.shape
    tq = MLA_Q_TILE
    hps = MLA_HEADS_PER_STEP
    kern, in_specs, args = _after(
        dep, functools.partial(_mla_kernel, nct_q=lc // tq - q_tile0, lc=lc),
        [pl.BlockSpec((1, tq, hps * V7X_MXU_DIM), lambda i, h, j: (i, j + q_tile0, h)),
         pl.BlockSpec((1, l, hps * V7X_MXU_DIM), lambda i, h, j: (i, 0, h)),
         pl.BlockSpec((1, hps * MLA_V, l), lambda i, h, j: (i, h, 0))],
        [q, k, vt])
    return pl.pallas_call(
        kern,
        out_shape=jax.ShapeDtypeStruct((b, l, MLA_HEADS * MLA_V), BF16),
        grid=(b, MLA_HEADS // hps, l // tq - q_tile0),
        in_specs=in_specs,
        out_specs=pl.BlockSpec((1, tq, hps * MLA_V), lambda i, h, j: (i, j + q_tile0, h)),
        compiler_params=pltpu.CompilerParams(dimension_semantics=("parallel", "parallel", "parallel"),
                                             vmem_limit_bytes=_vmem_limit(48)),
        name="mla_attention",
    )(*args)


def _swa_kernel(sink_ref, q_ref, k_ref, v_ref, o_ref, *, lc, q_tile0):
    tq = SWA_Q_TILE
    l = k_ref.shape[1]
    r0 = (pl.program_id(1) + q_tile0) * tq
    start = pl.multiple_of(jnp.clip(r0 - WINDOW, lc, l - SWA_BAND), WINDOW)
    rows = SWA_GROUP * tq
    row = lax.broadcasted_iota(jnp.int32, (rows, 1), 0)
    qpos = jnp.where(r0 >= lc, r0, -l) + row % tq
    kpos = start + lax.broadcasted_iota(jnp.int32, (1, SWA_BAND), 1)
    valid = jnp.abs(qpos - kpos) <= WINDOW
    lane = lax.broadcasted_iota(jnp.int32, (1, V7X_MXU_DIM), 1)
    head = [(lane // SWA_HEAD_DIM) == hh for hh in range(SWA_GROUP)]
    groups = range(SWA_KV_HEADS)
    sls = [slice(g * V7X_MXU_DIM, (g + 1) * V7X_MXU_DIM) for g in groups]
    qstack = []
    for sl in sls:
        qg = q_ref[0, :, sl]
        zero = jnp.zeros_like(qg)
        qstack.append(jnp.concatenate([jnp.where(head[hh], qg, zero) for hh in range(SWA_GROUP)], axis=0))
    sc = [_dot_nt(qstack[g], k_ref[0, 0:lc, sls[g]]) for g in groups]
    sb = [_dot_nt(qstack[g], k_ref[0, pl.ds(start, SWA_BAND), sls[g]]) for g in groups]
    for g in groups:
        sl = sls[g]
        sbm = jnp.where(valid, sb[g], NEG_INF)
        sk = jnp.zeros((rows, 1), F32)
        for hh in range(SWA_GROUP):
            sk = jnp.where(row // tq == hh, sink_ref[g * SWA_GROUP + hh] * LOG2E, sk)
        mx = jnp.maximum(jnp.maximum(jnp.max(sc[g], axis=-1, keepdims=True), jnp.max(sbm, axis=-1, keepdims=True)), sk)
        pc = jnp.exp2(sc[g] - mx)
        pb = jnp.exp2(sbm - mx)
        den = jnp.sum(pc, axis=-1, keepdims=True) + jnp.sum(pb, axis=-1, keepdims=True) + jnp.exp2(sk - mx)
        ostack = (_dot(pc, v_ref[0, 0:lc, sl]) + _dot(pb, v_ref[0, pl.ds(start, SWA_BAND), sl])) / den
        o = jnp.zeros((tq, V7X_MXU_DIM), F32)
        for hh in range(SWA_GROUP):
            o = o + jnp.where(head[hh], ostack[hh * tq:(hh + 1) * tq], 0.0)
        o_ref[0, :, sl] = o.astype(o_ref.dtype)


def _swa_attention(sinks, q, k, v, lc, q_tile0, dep=None):
    b, l, _ = q.shape
    tq = SWA_Q_TILE
    kern, in_specs, args = _after(
        dep, functools.partial(_swa_kernel, lc=lc, q_tile0=q_tile0),
        [pl.BlockSpec(memory_space=pltpu.SMEM),
         pl.BlockSpec((1, tq, SWA_HEADS * SWA_HEAD_DIM), lambda i, j: (i, j + q_tile0, 0)),
         pl.BlockSpec((1, l, _SWA_W), lambda i, j: (i, 0, 0)),
         pl.BlockSpec((1, l, _SWA_W), lambda i, j: (i, 0, 0))],
        [sinks, q, k, v])
    return pl.pallas_call(
        kern,
        out_shape=jax.ShapeDtypeStruct((b, l, SWA_HEADS * SWA_HEAD_DIM), BF16),
        grid=(b, l // tq - q_tile0),
        in_specs=in_specs,
        out_specs=pl.BlockSpec((1, tq, SWA_HEADS * SWA_HEAD_DIM), lambda i, j: (i, j + q_tile0, 0)),
        compiler_params=pltpu.CompilerParams(dimension_semantics=("parallel", "parallel"),
                                             vmem_limit_bytes=_vmem_limit(48)),
        name="swa_attention",
    )(*args)


def _pack_bf16_pair(x):
    w = x.shape[1] // 2
    lo = pltpu.bitcast(x[:, :w].astype(BF16).astype(F32), jnp.int32)
    hi = pltpu.bitcast(x[:, w:].astype(BF16).astype(F32), jnp.int32)
    return lax.shift_right_logical(lo, jnp.int32(16)) | (hi & jnp.int32(-65536))


def _unpack_bf16_pair(p):
    return pltpu.bitcast(p << 16, F32), pltpu.bitcast(p & jnp.int32(-65536), F32)


def _route(n2, wrt, bias, run_ref):
    n_hi = n2.astype(BF16)
    n_lo = (n2 - n_hi.astype(F32)).astype(BF16)
    w_hi = wrt.astype(BF16)
    w_lo = (wrt - w_hi.astype(F32)).astype(BF16)
    logits = _dot_nt(w_hi, n_hi) + (_dot_nt(w_hi, n_lo) + _dot_nt(w_lo, n_hi))
    rows = logits.shape[1]
    scores = _sigmoid(logits[0:N_EXPERTS])

    def select(sc2):
        cols = sc2.shape[1]
        shape3 = (N_GROUPS, GROUP_SIZE, cols)
        choice = sc2.reshape(shape3) + bias
        ji = lax.broadcasted_iota(jnp.int32, shape3, 1).astype(F32)
        m1 = jnp.max(choice, axis=1, keepdims=True)
        first = jnp.min(jnp.where(choice == m1, ji, float(GROUP_SIZE)), axis=1, keepdims=True)
        m2 = jnp.max(jnp.where(ji == first, -jnp.inf, choice), axis=1, keepdims=True)
        gs = m1 + m2
        gidx = lax.broadcasted_iota(jnp.int32, gs.shape, 0).astype(F32)
        gsel = jnp.zeros_like(gs)
        for _ in range(TOPK_GROUPS):
            mx = jnp.max(gs, axis=0, keepdims=True)
            pick = gidx == jnp.min(jnp.where(gs == mx, gidx, float(N_GROUPS)), axis=0, keepdims=True)
            gsel = jnp.where(pick, 1.0, gsel)
            gs = jnp.where(pick, -jnp.inf, gs)
        cand = jnp.where(gsel > 0.0, choice, -jnp.inf).reshape(N_EXPERTS, cols)
        eidx = lax.broadcasted_iota(jnp.int32, (N_EXPERTS, cols), 0).astype(F32)
        out = []
        for _ in range(TOP_K):
            mx = jnp.max(cand, axis=0, keepdims=True)
            pick = eidx == jnp.min(jnp.where(cand == mx, eidx, float(N_EXPERTS)), axis=0, keepdims=True)
            out.append(jnp.where(pick, 1.0, 0.0))
            cand = jnp.where(pick, -jnp.inf, cand)
        return out

    blocks = [select(scores[:, o:o + V7X_LANES]) for o in range(0, rows, V7X_LANES)]
    picks = [jnp.concatenate([blk[k] for blk in blocks], axis=1) > 0.0 for k in range(TOP_K)]
    ei = lax.broadcasted_iota(jnp.int32, (N_EXPERTS, rows), 0).astype(F32)
    esel = jnp.zeros((N_EXPERTS, rows), F32)
    for pick in picks:
        esel = jnp.where(pick, 1.0, esel)
    before = jnp.where(lax.broadcasted_iota(jnp.int32, (rows, rows), 0) < lax.broadcasted_iota(jnp.int32, (rows, rows), 1),
                       1.0, 0.0).astype(BF16)
    slot = jnp.dot(esel.astype(BF16), before, preferred_element_type=F32) + run_ref[...]
    run_ref[...] += jnp.sum(esel, axis=1, keepdims=True)
    sc = [jnp.sum(jnp.where(pick, scores, 0.0), axis=0, keepdims=True) for pick in picks]
    tot = sc[0]
    for x in sc[1:]:
        tot = tot + x
    k8 = lax.broadcasted_iota(jnp.int32, (8, rows), 0)
    kw = lax.broadcasted_iota(jnp.int32, (GATE_W, rows), 0)
    eid = jnp.zeros((8, rows), jnp.int32)
    rank = jnp.zeros((8, rows), jnp.int32)
    wk = jnp.zeros((GATE_W, rows), F32)
    for k, pick in enumerate(picks):
        e_k = jnp.sum(jnp.where(pick, ei, 0.0), axis=0, keepdims=True).astype(jnp.int32)
        r_k = jnp.sum(jnp.where(pick, slot, 0.0), axis=0, keepdims=True).astype(jnp.int32)
        eid = jnp.where(k8 == k, e_k, eid)
        rank = jnp.where(k8 == k, r_k, rank)
        wk = jnp.where(kw == k, sc[k] * (ROUTED_SCALE / tot), wk)
    return eid, rank, wk.T


def _mixer_tail(o, h, m, gffn_ref, wrt_ref, bias_ref, hn_ref, n2_ref, eid_ref, rank_ref, w_ref, cnt_ref, run_ref):
    @pl.when((pl.program_id(0) == 0) & (pl.program_id(1) == 0))
    def _():
        run_ref[...] = jnp.zeros_like(run_ref)

    hn = h + m[2:3] * o
    hn_ref[0] = hn
    n2 = _norm_mod(hn, gffn_ref[...], m[3:4], m[4:5])
    n2_ref[0] = _pack_bf16_pair(n2)
    eid, rank, wcols = _route(n2, wrt_ref[...], bias_ref[...], run_ref)
    eid_ref[0] = eid
    rank_ref[0] = rank
    w_ref[0] = wcols
    cnt_ref[...] = run_ref[...]


def _attn_out_kernel(a_ref, b_ref, c_ref, x_ref, mods_ref, wo_ref, gffn_ref, wrt_ref, bias_ref,
                     hn_ref, n2_ref, eid_ref, rank_ref, w_ref, cnt_ref, run_ref, *, nct):
    wa = MLA_HEADS * MLA_V
    o = _dot(a_ref[0], wo_ref[0:wa, :]) + _dot(b_ref[0], wo_ref[wa:, :])
    _mixer_tail(o, _stream_tile(c_ref, x_ref, nct), mods_ref[0, 0], gffn_ref, wrt_ref, bias_ref, hn_ref, n2_ref,
                eid_ref, rank_ref, w_ref, cnt_ref, run_ref)


def _tail_outs(b, l, d):
    tl = TOKEN_TILE
    nt = l // tl
    sds = jax.ShapeDtypeStruct
    tok = lambda w: pl.BlockSpec((1, tl, w), lambda i, j: (i, j, 0))
    blk = pl.BlockSpec((1, 8, tl), lambda i, j: (i * nt + j, 0, 0))
    shapes = [sds((b, l, d), F32), sds((b, l, d // 2), jnp.int32), sds((b * nt, 8, tl), jnp.int32),
              sds((b * nt, 8, tl), jnp.int32), sds((b, l, GATE_W), F32), sds((N_EXPERTS, 1), F32)]
    specs = [tok(d), tok(d // 2), blk, blk, tok(GATE_W), pl.BlockSpec((N_EXPERTS, 1), lambda i, j: (0, 0))]
    return shapes, specs


def _attn_out(a, bm, stream, mods, wo, gffn, wrt, bias, nct, dep=None):
    b, l, _ = a.shape
    d = stream[0].shape[2]
    tl = TOKEN_TILE
    tok = lambda w: pl.BlockSpec((1, tl, w), lambda i, j: (i, j, 0))
    full = lambda x: pl.BlockSpec(x.shape, lambda i, j: (0,) * x.ndim)
    shapes, specs = _tail_outs(b, l, d)
    kern, in_specs, args = _after(
        dep, functools.partial(_attn_out_kernel, nct=nct),
        [tok(a.shape[2]), tok(bm.shape[2])] + _stream_specs(stream, nct, tl) + [
            pl.BlockSpec((1, 1, N_MODS, d), lambda i, j: (i, jnp.where(j < nct, 0, 1), 0, 0)),
            full(wo), full(gffn), full(wrt), full(bias)],
        [a, bm, stream[0], stream[1], mods, wo, gffn, wrt, bias])
    return pl.pallas_call(
        kern,
        out_shape=shapes,
        grid=(b, l // tl),
        in_specs=in_specs,
        out_specs=specs,
        scratch_shapes=[pltpu.VMEM((N_EXPERTS, 1), F32)],
        compiler_params=pltpu.CompilerParams(dimension_semantics=("arbitrary", "arbitrary"),
                                             vmem_limit_bytes=_vmem_limit(40)),
        name="attn_out",
    )(*args)


def _moe_dest_kernel(off_ref, eid_ref, rank_ref, dest_ref):
    eid = eid_ref[...]
    dest = rank_ref[...]
    for e in range(N_EXPERTS):
        dest = dest + jnp.where(eid == e, off_ref[e], 0)
    dest_ref[...] = dest


def _moe_dest(off, eid, rank):
    return pl.pallas_call(
        _moe_dest_kernel,
        out_shape=jax.ShapeDtypeStruct(eid.shape, jnp.int32),
        in_specs=[pl.BlockSpec(memory_space=pltpu.SMEM),
                  pl.BlockSpec(eid.shape, lambda: (0, 0, 0)), pl.BlockSpec(eid.shape, lambda: (0, 0, 0))],
        out_specs=pl.BlockSpec(eid.shape, lambda: (0, 0, 0)),
        name="moe_dest",
    )(off, eid, rank)


def _sc_mesh():
    return plsc.VectorSubcoreMesh(core_axis_name="c", subcore_axis_name="s",
                                  num_cores=V7X_SC_CORES, num_subcores=V7X_SC_SUBCORES)


def _sc_chunk(rows_per_worker):
    return max(c for c in range(8, SC_MAX_CHUNK + 1, 8) if rows_per_worker % c == 0)


def _sc_dispatch(xp, dest, p_rows):
    t, w = xp.shape
    tpw = t // V7X_SC_WORKERS
    ch = _sc_chunk(tpw)

    @functools.partial(
        pl.kernel, mesh=_sc_mesh(), out_type=jax.ShapeDtypeStruct((p_rows, w), xp.dtype),
        scratch_types=[pltpu.VMEM((ch, w), xp.dtype)] + [pltpu.VMEM((ch,), jnp.int32)] * TOP_K
        + [pltpu.SemaphoreType.DMA, pltpu.SemaphoreType.DMA],
        name="moe_dispatch")
    def run(x_hbm, dest_hbm, out_hbm, rows_v, *rest):
        idx, (sem_i, sem_o) = rest[:TOP_K], rest[TOP_K:]
        base = (lax.axis_index("s") * V7X_SC_CORES + lax.axis_index("c")) * tpw

        @pl.loop(0, tpw // ch)
        def _(i):
            t0 = base + i * ch
            loads = [pltpu.async_copy(dest_hbm.at[k, pl.ds(t0, ch)], idx[k], sem_i) for k in range(TOP_K)]
            pltpu.sync_copy(x_hbm.at[pl.ds(t0, ch)], rows_v)
            for c in loads:
                c.wait()
            puts = [pltpu.async_copy(rows_v, out_hbm.at[idx[k]], sem_o) for k in range(TOP_K)]
            for c in puts:
                c.wait()

    return run(xp, dest)


def _sc_gather(ys, dest, t):
    w = ys.shape[1]
    tpw = t // V7X_SC_WORKERS
    ch = _sc_chunk(tpw)

    @functools.partial(
        pl.kernel, mesh=_sc_mesh(), out_type=jax.ShapeDtypeStruct((TOP_K, t, w), ys.dtype),
        scratch_types=[pltpu.VMEM((ch, w), ys.dtype)] * 2 + [pltpu.VMEM((ch,), jnp.int32)] * TOP_K
        + [pltpu.SemaphoreType.DMA] * 5,
        name="moe_gather")
    def run(y_hbm, dest_hbm, out_hbm, rows_a, rows_b, *rest):
        idx, (sem_i, sem_ga, sem_gb, sem_wa, sem_wb) = rest[:TOP_K], rest[TOP_K:]
        rows, sem_g, sem_w = (rows_a, rows_b), (sem_ga, sem_gb), (sem_wa, sem_wb)
        base = (lax.axis_index("s") * V7X_SC_CORES + lax.axis_index("c")) * tpw

        @pl.loop(0, tpw // ch)
        def _(i):
            t0 = base + i * ch
            loads = [pltpu.async_copy(dest_hbm.at[k, pl.ds(t0, ch)], idx[k], sem_i) for k in range(TOP_K)]
            for c in loads:
                c.wait()
            gets, puts = [None] * TOP_K, [None] * TOP_K
            gets[0] = pltpu.async_copy(y_hbm.at[idx[0]], rows[0], sem_g[0])
            for k in range(TOP_K):
                if k + 1 < TOP_K:
                    if k >= 1:
                        puts[k - 1].wait()
                    gets[k + 1] = pltpu.async_copy(y_hbm.at[idx[k + 1]], rows[(k + 1) % 2], sem_g[(k + 1) % 2])
                gets[k].wait()
                puts[k] = pltpu.async_copy(rows[k % 2], out_hbm.at[k, pl.ds(t0, ch)], sem_w[k % 2])
            puts[TOP_K - 2].wait()
            puts[TOP_K - 1].wait()

    return run(ys, dest)


def _cache_mlp_weights(wg, wu, wd, wgu_ref, wdb_ref):
    f = wg.shape[1]
    wgu_ref[:, 0:f] = wg.astype(BF16)
    wgu_ref[:, f:] = wu.astype(BF16)
    wdb_ref[...] = wd.astype(BF16)


def _gated_mlp(xp, wgu_ref, wdb_ref):
    lo, hi = _unpack_bf16_pair(xp)
    x = jnp.concatenate([lo.astype(BF16), hi.astype(BF16)], axis=1)
    gu = jnp.dot(x, wgu_ref[...], preferred_element_type=F32)
    f = gu.shape[1] // 2
    return _dot(_silu(gu[:, :f]) * gu[:, f:], wdb_ref[...])


def _moe_expert_kernel(te_ref, tb_ref, nv_ref, x_ref, wga_ref, wua_ref, wda_ref, wgb_ref, wub_ref, wdb_ref, y_ref,
                       gu_a, dn_a, gu_b, dn_b, ids_ref):
    i = pl.program_id(0)
    tm = MOE_ROW_TILE
    nv = nv_ref[0]
    first = 2 * jnp.minimum(i, (nv - 1) // 2)
    ea = te_ref[first]
    eb = te_ref[first + 1]
    two = 2 * i + 1 < nv

    @pl.when(i == 0)
    def _():
        ids_ref[0] = -1
        ids_ref[1] = -1

    @pl.when(ids_ref[0] != ea)
    def _():
        _cache_mlp_weights(wga_ref[0, 0], wua_ref[0, 0], wda_ref[0, 0], gu_a, dn_a)
        ids_ref[0] = ea

    @pl.when(two & (eb != ea) & (ids_ref[1] != eb))
    def _():
        _cache_mlp_weights(wgb_ref[0, 0], wub_ref[0, 0], wdb_ref[0, 0], gu_b, dn_b)
        ids_ref[1] = eb

    @pl.when(two & (eb == ea))
    def _():
        y_ref[...] = _pack_bf16_pair(_gated_mlp(x_ref[...], gu_a, dn_a))

    @pl.when((2 * i < nv) & jnp.logical_not(two & (eb == ea)))
    def _():
        y_ref[0:tm, :] = _pack_bf16_pair(_gated_mlp(x_ref[0:tm, :], gu_a, dn_a))

    @pl.when(two & (eb != ea))
    def _():
        y_ref[tm:, :] = _pack_bf16_pair(_gated_mlp(x_ref[tm:, :], gu_b, dn_b))


def _moe_experts(tile_expert, n_valid, xs, wg, wu, wd, layer, dep=None):
    p_rows, w = xs.shape
    tm = MOE_ROW_TILE
    _, _, d, f = wg.shape
    npair = p_rows // (2 * tm)
    pairs = tile_expert.reshape(npair, 2)
    tile_b = jnp.maximum(lax.cummax(jnp.where(pairs[:, 1] != pairs[:, 0], pairs[:, 1], -1)), 0)
    step = lambda i, nv: jnp.minimum(i, (nv[0] - 1) // 2)
    spec_a = lambda shp: pl.BlockSpec((1, 1) + shp, lambda i, te, tb, nv: (layer, te[2 * step(i, nv)], 0, 0))
    spec_b = lambda shp: pl.BlockSpec((1, 1) + shp, lambda i, te, tb, nv: (layer, tb[step(i, nv)], 0, 0))
    rows = pl.BlockSpec((2 * tm, w), lambda i, te, tb, nv: (step(i, nv), 0))
    kern, in_specs, args = _after(
        dep, _moe_expert_kernel,
        [rows, spec_a((d, f)), spec_a((d, f)), spec_a((f, d)), spec_b((d, f)), spec_b((d, f)), spec_b((f, d))],
        [tile_expert, tile_b, n_valid, xs, wg, wu, wd, wg, wu, wd], n_lead=3)
    return pl.pallas_call(
        kern,
        out_shape=jax.ShapeDtypeStruct((p_rows, w), xs.dtype),
        grid_spec=pltpu.PrefetchScalarGridSpec(
            num_scalar_prefetch=3, grid=(npair,),
            in_specs=in_specs,
            out_specs=rows,
            scratch_shapes=[pltpu.VMEM((d, 2 * f), BF16), pltpu.VMEM((f, d), BF16),
                            pltpu.VMEM((d, 2 * f), BF16), pltpu.VMEM((f, d), BF16), pltpu.SMEM((2,), jnp.int32)]),
        compiler_params=pltpu.CompilerParams(dimension_semantics=("arbitrary",),
                                             vmem_limit_bytes=_vmem_limit(48)),
        name="moe_experts",
    )(*args)


def _moe_combine_kernel(yg_ref, w_ref, xp_ref, sg_ref, su_ref, sd_ref, h_ref, mods_ref, gfin_ref, *rest, final_norm):
    o_ref, wgu_ref, wdb_ref = rest[-3:]

    @pl.when((pl.program_id(0) == 0) & (pl.program_id(1) == 0))
    def _():
        _cache_mlp_weights(sg_ref[0], su_ref[0], sd_ref[0], wgu_ref, wdb_ref)

    acc = _gated_mlp(xp_ref[0], wgu_ref, wdb_ref)
    half = acc.shape[1] // 2
    lo = acc[:, :half]
    hi = acc[:, half:]
    w = w_ref[0]
    for k in range(TOP_K):
        ylo, yhi = _unpack_bf16_pair(yg_ref[k, 0])
        wk = w[:, k:k + 1]
        lo = lo + wk * ylo
        hi = hi + wk * yhi
    y = h_ref[0] + mods_ref[0, 0, N_MODS - 1:N_MODS, :] * jnp.concatenate([lo, hi], axis=1)
    if final_norm:
        y = _rms(y, gfin_ref[...])
    o_ref[0] = y


def _moe_combine(yg, wcols, xp, sg, su, sd, h, mods, gfin, nct, layer, out_buf, out_b0, out_batch, latent_only,
                 final_norm, dep=None):
    b, l, d = h.shape
    tl = TOKEN_TILE
    tile0 = nct if latent_only else 0
    tok = lambda w: pl.BlockSpec((1, tl, w), lambda i, j: (i, j + tile0, 0))
    lay = lambda x: pl.BlockSpec((1,) + x.shape[1:], lambda i, j: (layer,) + (0,) * (x.ndim - 1))
    args = [yg, wcols, xp, sg, su, sd, h, mods, gfin]
    in_specs = [pl.BlockSpec((TOP_K, 1, tl, d // 2), lambda i, j: (0, i, j + tile0, 0)), tok(GATE_W), tok(d // 2),
                lay(sg), lay(su), lay(sd), tok(d),
                pl.BlockSpec((1, 1, N_MODS, d), lambda i, j: (i, jnp.where(j + tile0 < nct, 0, 1), 0, 0)),
                pl.BlockSpec(gfin.shape, lambda i, j: (0, 0))]
    _, in_specs, args = _after(dep, None, in_specs, args)
    aliases = {}
    if out_buf is not None:
        args.append(out_buf)
        in_specs.append(pl.BlockSpec(memory_space=pl.ANY))
        aliases = {len(args) - 1: 0}
    return pl.pallas_call(
        functools.partial(_moe_combine_kernel, final_norm=final_norm),
        out_shape=jax.ShapeDtypeStruct((out_batch, l - tile0 * tl, d), F32),
        grid=(b, l // tl - tile0),
        in_specs=in_specs,
        out_specs=pl.BlockSpec((1, tl, d), lambda i, j: (i + out_b0, j, 0)),
        scratch_shapes=[pltpu.VMEM((d, 2 * sg.shape[2]), BF16), pltpu.VMEM((sg.shape[2], d), BF16)],
        input_output_aliases=aliases,
        compiler_params=pltpu.CompilerParams(dimension_semantics=("arbitrary", "arbitrary"),
                                             vmem_limit_bytes=_vmem_limit(40)),
        name="moe_combine",
    )(*args)


def _moe_route_rows(n2p, eid, rank, counts, b, l):
    d2 = n2p.shape[2]
    t = b * l
    tm = MOE_ROW_TILE
    n_tiles = 2 * -(-(TOP_K * t + N_EXPERTS * (tm - 1)) // (2 * tm))
    tiles_e = (counts.reshape(N_EXPERTS).astype(jnp.int32) + (tm - 1)) // tm
    tile_end = jnp.cumsum(tiles_e)
    off = (tile_end - tiles_e) * tm
    n_valid = tile_end[-1:]
    tile_id = jnp.minimum(jnp.arange(n_tiles, dtype=jnp.int32), n_valid - 1)
    tile_expert = jnp.sum((tile_end[None, :] <= tile_id[:, None]).astype(jnp.int32), axis=1)
    dest = _moe_dest(off, eid, rank).transpose(1, 0, 2).reshape(8, t)
    xs = _sc_dispatch(n2p.reshape(t, d2), dest, n_tiles * tm)
    return xs, dest, tile_expert, n_valid


def _rwkv_proj_kernel(h_ref, hp_ref, hx_ref, mods_ref, g_ref, mu_ref, wr_ref, wk_ref, wv_ref, g1_ref, g2_ref,
                      w1_ref, w2_ref, a1_ref, a2_ref, w0_ref, a0_ref, kk_ref, ka_ref, rk_ref, bd_ref,
                      r_out, v_out, kk_out, g_out, km_out, b_out, lw_out, bonus_out, *, nct):
    j = pl.program_id(1)
    nt = pl.num_programs(1)
    m = mods_ref[0, 0]
    g = g_ref[...]
    n = _norm_mod(h_ref[0], g, m[0:1], m[1:2])
    tl, d = n.shape
    seg_first = (j == 0) | (j == nct)
    seg_last = (j == nct - 1) | (j == nt - 1)
    n_prev = _norm_mod(hp_ref[0], g, m[0:1], m[1:2])[7:8] * jnp.where(seg_first, 0.0, 1.0)
    n_next = _norm_mod(hx_ref[0], g, m[0:1], m[1:2])[0:1] * jnp.where(seg_last, 0.0, 1.0)
    row = lax.broadcasted_iota(jnp.int32, (tl, 1), 0)
    prev = jnp.where(row == 0, n_prev, pltpu.roll(n, 1, axis=0))
    nxt = jnp.where(row == tl - 1, n_next, pltpu.roll(n, tl - 1, axis=0))
    lane = lax.broadcasted_iota(jnp.int32, (1, d), 1)
    xx = jnp.where(lane < d // 2, prev, nxt) - n
    mu = mu_ref[...]
    xr, xw, xk, xv, xa, xg = [n + xx * mu[i:i + 1] for i in range(6)]
    r = _dot(xr, wr_ref[...])
    k = _dot(xk, wk_ref[...])
    v = _dot(xv, wv_ref[...])
    g_out[0] = _dot(_sigmoid(_dot(xg, g1_ref[...])), g2_ref[...]).astype(g_out.dtype)
    tw = jnp.tanh(_dot(xw, w1_ref[...]))
    ta = _dot(xa, a1_ref[...])
    bd = bd_ref[...]
    kk = k * kk_ref[...]
    kk = kk / jnp.maximum(jnp.sqrt(_head_sum(kk * kk, bd)), 1e-12)
    r_out[0] = r.astype(r_out.dtype)
    v_out[0] = v.astype(v_out.dtype)
    kk_out[0] = kk.astype(kk_out.dtype)
    bonus = jnp.zeros_like(v)
    for dr in range(2):
        zw = w0_ref[dr:dr + 1, :] + _dot(tw, w2_ref[dr])
        lw_out[dr, 0] = -jnp.exp(-0.5) * _sigmoid(zw)
        a = _sigmoid(a0_ref[dr:dr + 1, :] + _dot(ta, a2_ref[dr]))
        km = k * (1.0 + (a - 1.0) * ka_ref[...])
        km_out[dr, 0] = km.astype(km_out.dtype)
        b_out[dr, 0] = (kk * a).astype(b_out.dtype)
        bonus = bonus + _head_sum(r * km * rk_ref[...], bd) * v
    bonus_out[0] = bonus


def _rwkv_proj(h, mods, g, mu, wr, wk, wv, g1, g2, w1, w2, a1, a2, w0, a0, kk, ka, rk, bd, nct, dep=None):
    b, l, d = h.shape
    tl = TOKEN_TILE
    nb8 = l // 8
    tok = pl.BlockSpec((1, tl, d), lambda i, j: (i, j, 0))
    tok2 = pl.BlockSpec((2, 1, tl, d), lambda i, j: (0, i, j, 0))
    full = lambda x: pl.BlockSpec(x.shape, lambda i, j: (0,) * x.ndim)
    sds = jax.ShapeDtypeStruct
    kern, in_specs, args = _after(
        dep, functools.partial(_rwkv_proj_kernel, nct=nct),
        [tok,
         pl.BlockSpec((1, 8, d), lambda i, j: (i, jnp.maximum(j * (tl // 8) - 1, 0), 0)),
         pl.BlockSpec((1, 8, d), lambda i, j: (i, jnp.minimum((j + 1) * (tl // 8), nb8 - 1), 0)),
         pl.BlockSpec((1, 1, N_MODS, d), lambda i, j: (i, jnp.where(j < nct, 0, 1), 0, 0)),
         full(g), full(mu), full(wr), full(wk), full(wv), full(g1), full(g2), full(w1), full(w2),
         full(a1), full(a2), full(w0), full(a0), full(kk), full(ka), full(rk), full(bd)],
        [h, h, h, mods, g, mu, wr, wk, wv, g1, g2, w1, w2, a1, a2, w0, a0, kk, ka, rk, bd])
    return pl.pallas_call(
        kern,
        out_shape=[sds((b, l, d), BF16), sds((b, l, d), BF16), sds((b, l, d), BF16), sds((b, l, d), BF16),
                   sds((2, b, l, d), BF16), sds((2, b, l, d), BF16), sds((2, b, l, d), F32), sds((b, l, d), F32)],
        grid=(b, l // tl),
        in_specs=in_specs,
        out_specs=[tok, tok, tok, tok, tok2, tok2, tok2, tok],
        compiler_params=pltpu.CompilerParams(dimension_semantics=("parallel", "parallel"),
                                             vmem_limit_bytes=_vmem_limit(56)),
        name="rwkv_proj",
    )(*args)


def _wkv_kernel(r_ref, v_ref, kk_ref, km_ref, b_ref, lw_ref, y_ref, st_ref):
    c = WKV_CHUNK
    w = WKV_PAIR
    rev = pl.program_id(0)
    sign = 1 - 2 * rev

    @pl.when(pl.program_id(2) == 0)
    def _():
        st_ref[...] = jnp.zeros_like(st_ref)

    ti = lax.broadcasted_iota(jnp.int32, (c, c), 0)
    si = lax.broadcasted_iota(jnp.int32, (c, c), 1)
    tri = jnp.where((si - ti) * sign <= 0, 1.0, 0.0).astype(F32)
    nsub = WKV_CHUNKS_PER_STEP
    subs = [pl.ds(pl.multiple_of(jnp.where(rev == 0, s, nsub - 1 - s) * c, c), c) for s in range(nsub)]
    rt, kt, kh, bh, v32, e_mid = [], [], [], [], [], []
    for rows in subs:
        lw = lw_ref[0, 0, rows, :]
        l_incl = jnp.dot(tri, lw, precision=HIGHEST, preferred_element_type=F32)
        mid = 0.5 * jnp.sum(lw, axis=0, keepdims=True)
        e_neg = jnp.exp(mid - l_incl)
        e_mid.append(jnp.exp(mid))
        rt.append(r_ref[0, rows, :].astype(F32) * jnp.exp(l_incl - mid))
        kt.append(kk_ref[0, rows, :].astype(F32) * jnp.exp(l_incl - lw - mid))
        kh.append(km_ref[0, 0, rows, :].astype(F32) * e_neg)
        bh.append(b_ref[0, 0, rows, :].astype(F32) * e_neg)
        v32.append(v_ref[0, rows, :].astype(F32))

    ri = lax.broadcasted_iota(jnp.int32, (w, w), 0)
    ci = lax.broadcasted_iota(jnp.int32, (w, w), 1)
    same = (ri // c) == (ci // c)
    eye = jnp.where(ri == ci, 1.0, 0.0).astype(F32)
    tl_ = lax.broadcasted_iota(jnp.int32, (c, w), 0)
    jl_ = lax.broadcasted_iota(jnp.int32, (c, w), 1) % c
    strict = (jl_ - tl_) * sign < 0
    incl = (jl_ - tl_) * sign <= 0
    eye2 = jnp.where(jl_ == tl_, 1.0, 0.0).astype(F32)
    lane = lax.broadcasted_iota(jnp.int32, (1, w), 1)
    h0 = lane < RWKV_HEAD

    def rows2(x):
        return jnp.concatenate([jnp.where(h0, x, 0.0), jnp.where(h0, 0.0, x)], axis=0)

    npair = st_ref.shape[0]
    items = [(s, slice(p * w, (p + 1) * w)) for s in range(nsub) for p in range(npair)]
    n = range(len(items))
    em = [e_mid[s][:, sl] for s, sl in items]
    g = [_dot_nt(jnp.concatenate([kt[s][:, sl], rt[s][:, sl]], axis=0),
                 jnp.concatenate([rows2(kh[s][:, sl]), rows2(bh[s][:, sl])], axis=0)) for s, sl in items]
    a_kk = [jnp.where(strict, x[:c, :w], 0.0) for x in g]
    a_rk = [jnp.where(incl, x[c:, :w], 0.0) for x in g]
    a_rb = [jnp.where(incl, x[c:, w:], 0.0) for x in g]
    vi = [v32[s][:, sl] for s, sl in items]
    v_rows = [rows2(x) for x in vi]
    r_pre = [_dot(a_kk[i], v_rows[i]) for i in n]
    m = [jnp.where(strict, -x[:c, w:], 0.0) for x in g]
    tinv = [eye2 + x for x in m]
    m = [_dot(x, rows2(x)) for x in m]
    for _ in range(c.bit_length() - 3):
        both = [_dot(jnp.concatenate([tinv[i], m[i]], axis=0), rows2(m[i])) for i in n]
        tinv = [tinv[i] + both[i][:c] for i in n]
        m = [x[c:] for x in both]
    tinv = [tinv[i] + _dot(tinv[i], rows2(m[i])) for i in n]
    sol = [_dot(tinv[i], jnp.concatenate([rows2(r_pre[i]), rows2(kt[s][:, sl] * em[i])], axis=1))
           for i, (s, sl) in enumerate(items)]
    u_pre = [x[:, :w] for x in sol]
    kq = [x[:, w:] for x in sol]
    y_pre = [_dot(jnp.concatenate([a_rk[i], -a_rb[i]], axis=1),
                  jnp.concatenate([v_rows[i], rows2(u_pre[i])], axis=0)) for i in n]
    r_eff = [rt[s][:, sl] * em[i] - _dot(a_rb[i], rows2(kq[i])) for i, (s, sl) in enumerate(items)]
    bbar = [bh[s][:, sl] * em[i] for i, (s, sl) in enumerate(items)]
    kbar = [kh[s][:, sl] * em[i] for i, (s, sl) in enumerate(items)]
    mmat = [eye * (em[i] * em[i]) - jnp.where(same, _dot_tn(kq[i], bbar[i]), 0.0) for i in n]
    s_pre = [jnp.where(same, _dot_tn(jnp.concatenate([vi[i], -u_pre[i]], axis=0),
                                     jnp.concatenate([kbar[i], bbar[i]], axis=0)), 0.0) for i in n]
    st = [st_ref[p] for p in range(npair)]
    for i, (s, sl) in enumerate(items):
        p = i % npair
        y_ref[0, 0, subs[s], sl] = _dot_nt(r_eff[i], st[p]) + y_pre[i]
        hi = st[p].astype(BF16)
        lo = (st[p] - hi.astype(F32)).astype(BF16)
        mb = mmat[i].astype(BF16)
        st[p] = (jnp.dot(hi, mb, preferred_element_type=F32) + jnp.dot(lo, mb, preferred_element_type=F32)
                 + s_pre[i])
    for p in range(npair):
        st_ref[p] = st[p]


def _wkv(r, v, kk, km, bv, lw, lc, dep=None):
    b, l, d = r.shape
    c = WKV_CHUNK * WKV_CHUNKS_PER_STEP
    ncc = lc // c
    nlc = (l - lc) // c

    def chunk(dr, i):
        return jnp.where(dr == 0, i, jnp.where(i < ncc, ncc - 1 - i, nlc + 2 * ncc - 1 - i))

    shared = pl.BlockSpec((1, c, d), lambda dr, bi, i: (bi, chunk(dr, i), 0))
    per_dir = pl.BlockSpec((1, 1, c, d), lambda dr, bi, i: (dr, bi, chunk(dr, i), 0))
    kern, in_specs, args = _after(dep, _wkv_kernel, [shared, shared, shared, per_dir, per_dir, per_dir],
                                  [r, v, kk, km, bv, lw])
    return pl.pallas_call(
        kern,
        out_shape=jax.ShapeDtypeStruct((2, b, l, d), F32),
        grid=(2, b, l // c),
        in_specs=in_specs,
        out_specs=per_dir,
        scratch_shapes=[pltpu.VMEM((d // WKV_PAIR, WKV_PAIR, WKV_PAIR), F32)],
        compiler_params=pltpu.CompilerParams(dimension_semantics=("parallel", "parallel", "arbitrary"),
                                             vmem_limit_bytes=_vmem_limit(32)),
        name="wkv7_chunked",
    )(*args)


def _rwkv_out_kernel(y_ref, bonus_ref, g_ref, lnw_ref, lnb_ref, wo_ref, bd_ref, h_ref, mods_ref, gffn_ref,
                     wrt_ref, bias_ref, hn_ref, n2_ref, eid_ref, rank_ref, w_ref, cnt_ref, run_ref):
    y = y_ref[0, 0] + y_ref[1, 0]
    bd = bd_ref[...]
    mean = _head_sum(y, bd) * (1.0 / RWKV_HEAD)
    yc = y - mean
    var = _head_sum(yc * yc, bd) * (1.0 / RWKV_HEAD)
    yn = yc * lax.rsqrt(var + GN_EPS) * lnw_ref[...] + lnb_ref[...]
    out = (yn + bonus_ref[0]) * g_ref[0].astype(F32)
    _mixer_tail(_dot(out, wo_ref[...]), h_ref[0], mods_ref[0, 0], gffn_ref, wrt_ref, bias_ref, hn_ref, n2_ref,
                eid_ref, rank_ref, w_ref, cnt_ref, run_ref)


def _rwkv_out(y, bonus, g, lnw, lnb, wo, bd, h, mods, gffn, wrt, bias, nct, dep=None):
    b, l, d = h.shape
    tl = TOKEN_TILE
    tok = lambda w: pl.BlockSpec((1, tl, w), lambda i, j: (i, j, 0))
    full = lambda x: pl.BlockSpec(x.shape, lambda i, j: (0,) * x.ndim)
    shapes, specs = _tail_outs(b, l, d)
    kern, in_specs, args = _after(
        dep, _rwkv_out_kernel,
        [pl.BlockSpec((2, 1, tl, d), lambda i, j: (0, i, j, 0)), tok(d), tok(d),
         full(lnw), full(lnb), full(wo), full(bd), tok(d),
         pl.BlockSpec((1, 1, N_MODS, d), lambda i, j: (i, jnp.where(j < nct, 0, 1), 0, 0)),
         full(gffn), full(wrt), full(bias)],
        [y, bonus, g, lnw, lnb, wo, bd, h, mods, gffn, wrt, bias])
    return pl.pallas_call(
        kern,
        out_shape=shapes,
        grid=(b, l // tl),
        in_specs=in_specs,
        out_specs=specs,
        scratch_shapes=[pltpu.VMEM((N_EXPERTS, 1), F32)],
        compiler_params=pltpu.CompilerParams(dimension_semantics=("arbitrary", "arbitrary"),
                                             vmem_limit_bytes=_vmem_limit(40)),
        name="rwkv_out",
    )(*args)


def _rope_table(n_lat, n_ctx):
    dim = SWA_HEAD_DIM
    nf = dim // 4
    inv = ROPE_THETA ** (-jnp.arange(nf, dtype=F32) / nf)
    row = jnp.repeat(jnp.arange(n_lat // GRID_W, dtype=F32), GRID_W)
    col = jnp.tile(jnp.arange(GRID_W, dtype=F32), n_lat // GRID_W)
    ar = row[:, None] * inv
    ac = col[:, None] * inv
    ang = jnp.concatenate([ar, ar, ac, ac], axis=-1)
    cos = jnp.concatenate([jnp.ones((n_ctx, dim), F32), jnp.cos(ang)], axis=0)
    sin = jnp.concatenate([jnp.zeros((n_ctx, dim), F32), jnp.sin(ang)], axis=0)
    return jnp.tile(cos, (1, 2)), jnp.tile(sin, (1, 2))


def _layout_attn_weights(w_in, w_uq, w_ukv):
    d = w_in.shape[0]
    s0 = MLA_Q_RANK
    s1 = s0 + MLA_KV_RANK
    s2 = s1 + MLA_ROPE
    s3 = s2 + SWA_HEADS * SWA_HEAD_DIM
    s4 = s3 + SWA_KV_HEADS * SWA_HEAD_DIM
    rep = lambda w: jnp.concatenate(
        [jnp.tile(w[:, g * SWA_HEAD_DIM:(g + 1) * SWA_HEAD_DIM], (1, SWA_GROUP)) for g in range(SWA_KV_HEADS)], axis=1)
    win = jnp.concatenate([w_in[:, :s1], w_in[:, s2:s3], rep(w_in[:, s3:s4]), rep(w_in[:, s4:]),
                           w_in[:, s1:s2], jnp.zeros((d, V7X_LANES - MLA_ROPE), w_in.dtype)], axis=1)
    qh = MLA_NOPE + MLA_ROPE
    pad = jnp.zeros((w_uq.shape[0], V7X_MXU_DIM - qh), w_uq.dtype)
    wuq = jnp.concatenate([jnp.concatenate([w_uq[:, h * qh:(h + 1) * qh], pad], axis=1) for h in range(MLA_HEADS)], axis=1)
    kvh = MLA_NOPE + MLA_V
    wuk = jnp.concatenate([w_ukv[:, h * kvh:h * kvh + MLA_NOPE] for h in range(MLA_HEADS)], axis=1)
    wuvt = jnp.concatenate([w_ukv[:, h * kvh + MLA_NOPE:(h + 1) * kvh] for h in range(MLA_HEADS)], axis=1).T
    return win.astype(BF16), wuq.astype(BF16), wuk.astype(BF16), wuvt.astype(BF16)


def _lora_pair(w_down, w_up):
    rank = w_down.shape[2]
    down = jnp.concatenate([w_down[0], w_down[1]], axis=1)
    z = jnp.zeros((rank, w_up.shape[2]), w_up.dtype)
    up = jnp.stack([jnp.concatenate([w_up[0], z], axis=0), jnp.concatenate([z, w_up[1]], axis=0)], axis=0)
    return down.astype(BF16), up.astype(BF16)


def _head_block_diag():
    i = jnp.arange(V7X_MXU_DIM) // RWKV_HEAD
    return (i[:, None] == i[None, :]).astype(BF16)


def kernel(x, c, ctx, c_ctx, ada_w, ada_b, norm_mix, norm_ffn, norm_final, attn_w_in, attn_q_norm, attn_kv_norm, attn_w_uq, attn_w_ukv, attn_sinks, attn_w_o, rwkv_mu, rwkv_w_r, rwkv_w_k, rwkv_w_v, rwkv_w_o, rwkv_g1, rwkv_g2, rwkv_w0, rwkv_w1, rwkv_w2, rwkv_a0, rwkv_a1, rwkv_a2, rwkv_k_k, rwkv_k_a, rwkv_r_k, rwkv_ln_w, rwkv_ln_b, moe_router, moe_bias, moe_w_gate, moe_w_up, moe_w_down, moe_ws_gate, moe_ws_up, moe_ws_down):
    bsz, s, d = x.shape
    lc = ctx.shape[1]
    l = lc + s
    depth = ada_w.shape[0]
    nct = lc // TOKEN_TILE
    assert lc % TOKEN_TILE == 0 and s % TOKEN_TILE == 0 and s >= SWA_BAND and lc % SWA_Q_TILE == 0
    assert lc % (WKV_CHUNK * WKV_CHUNKS_PER_STEP) == 0
    assert d % V7X_MXU_DIM == 0 and WKV_CHUNK * 2 == V7X_LANES
    ngrp = SAMPLE_GROUPS
    bg = bsz // ngrp
    assert bsz % ngrp == 0 and (bg * l) % (8 * V7X_SC_WORKERS) == 0

    assert ngrp == 2
    cos, sin = _rope_table(s, lc)
    bd = _head_block_diag()
    rows = -(-(bsz + 1) // 8) * 8
    cc = jnp.concatenate([c, c_ctx[None, :], jnp.zeros((rows - bsz - 1, d), F32)], axis=0)
    row2 = lambda a: a.reshape(1, -1)
    moe_w = (moe_w_gate, moe_w_up, moe_w_down)
    moe_ws = (moe_ws_gate, moe_ws_up, moe_ws_down)

    shared = {}

    def layer_weights(li):
        if li not in shared:
            i = li // 2
            ada = _ada_mods(cc, ada_w, ada_b, li)
            w = dict(
                mods=jnp.stack([jnp.broadcast_to(ada[bsz].reshape(1, N_MODS, d), (bsz, N_MODS, d)),
                                ada[:bsz].reshape(bsz, N_MODS, d)], axis=1),
                wrt=jnp.concatenate([moe_router[li].T, jnp.zeros((GATE_W - N_EXPERTS, d), F32)], axis=0),
                bias=moe_bias[li].reshape(N_GROUPS, GROUP_SIZE, 1))
            if li % 2 == 0:
                w["win"], w["wuq"], w["wuk"], w["wuvt"] = _layout_attn_weights(attn_w_in[i], attn_w_uq[i], attn_w_ukv[i])
                w["wo"] = attn_w_o[i].astype(BF16)
            else:
                w["w1"], w["w2"] = _lora_pair(rwkv_w1[i], rwkv_w2[i])
                w["a1"], w["a2"] = _lora_pair(rwkv_a1[i], rwkv_a2[i])
                w["wr"], w["wk"], w["wv"], w["wo"] = [x[i].astype(BF16) for x in (rwkv_w_r, rwkv_w_k, rwkv_w_v, rwkv_w_o)]
                w["g1"], w["g2"] = rwkv_g1[i].astype(BF16), rwkv_g2[i].astype(BF16)
            shared[li] = w
        return shared[li]

    groups = [dict(stream=(ctx, x, g * bg, 0), b0=g * bg) for g in range(ngrp)]
    result = [None]

    def run_stage(st, li, name, dep):
        w = layer_weights(li)
        i = li // 2
        with_ctx = li < depth - 1
        mods = w["mods"][st["b0"]:st["b0"] + bg]
        if name == "proj" and li % 2 == 0:
            st["qkv"] = _attn_proj(st["stream"], bg, l, mods, row2(norm_mix[li]), w["win"], row2(attn_q_norm[i]),
                                   row2(attn_kv_norm[i]), w["wuq"], w["wuk"], w["wuvt"], cos, sin, nct, dep=dep)
            return st["qkv"][0]
        if name == "mid" and li % 2 == 0:
            q, k, vt, qs, ks, vs = st.pop("qkv")
            st["a"] = _mla_attention(q, k, vt, lc, 0 if with_ctx else lc // MLA_Q_TILE, dep=dep)
            st["bm"] = _swa_attention(attn_sinks[i], qs, ks, vs, lc, 0 if with_ctx else lc // SWA_Q_TILE, dep=st["a"])
            return st["bm"]
        if name == "proj":
            assert st["stream"][0] is st["stream"][1]
            st["feat"] = _rwkv_proj(st["stream"][0], mods, row2(norm_mix[li]), rwkv_mu[i], w["wr"], w["wk"], w["wv"],
                                    w["g1"], w["g2"], w["w1"], w["w2"], w["a1"], w["a2"], rwkv_w0[i], rwkv_a0[i],
                                    row2(rwkv_k_k[i]), row2(rwkv_k_a[i]), row2(rwkv_r_k[i]), bd, nct, dep=dep)
            return st["feat"][0]
        if name == "mid":
            r, v, kk, gt, km, bv, lw, bonus = st.pop("feat")
            st["y"] = _wkv(r, v, kk, km, bv, lw, lc, dep=dep)
            st["gate"], st["bonus"] = gt, bonus
            return st["y"]
        if name == "out":
            if li % 2 == 0:
                tail = _attn_out(st.pop("a"), st.pop("bm"), st["stream"], mods, w["wo"], row2(norm_ffn[li]),
                                 w["wrt"], w["bias"], nct, dep=dep)
            else:
                tail = _rwkv_out(st.pop("y"), st.pop("bonus"), st.pop("gate"), row2(rwkv_ln_w[i]), row2(rwkv_ln_b[i]),
                                 w["wo"], bd, st["stream"][0], mods, row2(norm_ffn[li]), w["wrt"], w["bias"], nct, dep=dep)
            st["h"], st["n2p"], eid, rank, st["wcols"], counts = tail
            st["xs"], st["dest"], st["tile_expert"], st["n_valid"] = _moe_route_rows(st["n2p"], eid, rank, counts, bg, l)
            return st["h"]
        if name == "experts":
            ys = _moe_experts(st.pop("tile_expert"), st.pop("n_valid"), st.pop("xs"), *moe_w, li, dep=dep)
            st["yg"] = _sc_gather(ys, st.pop("dest"), bg * l).reshape(TOP_K, bg, l, d // 2)
            return ys
        assert name == "combine"
        last = li == depth - 1
        h = _moe_combine(st.pop("yg"), st.pop("wcols"), st.pop("n2p"), *moe_ws, st.pop("h"), mods, row2(norm_final),
                         nct, li, result[0] if last else None, st["b0"] if last else 0, bsz if last else bg,
                         last, last, dep=dep)
        if last:
            result[0] = h
        else:
            st["stream"] = (h, h, 0, nct)
        return h

    order = [(0, 0, "proj"), (0, 0, "mid")]
    for li in range(depth):
        order += [(0, li, "out"), (1, li, "proj"), (0, li, "experts"), (1, li, "mid")]
        if li < depth - 1:
            order += [(0, li, "combine"), (1, li, "out"), (0, li + 1, "proj"), (1, li, "experts"),
                      (0, li + 1, "mid"), (1, li, "combine")]
        else:
            order += [(1, li, "out"), (0, li, "combine"), (1, li, "experts"), (1, li, "combine")]
    dep = None
    for g, li, name in order:
        dep = run_stage(groups[g], li, name, dep)
    return result[0]
```

```python
import functools

import jax
import jax.numpy as jnp
from jax import lax
from jax.experimental import pallas as pl
from jax.experimental.pallas import tpu as pltpu
from jax.experimental.pallas import tpu_sc as plsc

F32 = jnp.float32
BF16 = jnp.bfloat16
HIGHEST = lax.Precision.HIGHEST

GRID_W = 64
NORM_EPS = 1e-6
ROPE_THETA = 10000.0
NEG_INF = -1e30
N_MODS = 6

MLA_HEADS = 4
MLA_Q_RANK = 384
MLA_KV_RANK = 256
MLA_NOPE = 128
MLA_ROPE = 64
MLA_V = 128

SWA_HEADS = 8
SWA_KV_HEADS = 2
SWA_GROUP = SWA_HEADS // SWA_KV_HEADS
SWA_HEAD_DIM = 64
WINDOW = 128

RWKV_HEAD = 64
DECAY_LORA = 64
ICLR_LORA = 64
GATE_LORA = 128
GN_EPS = 64e-5

N_EXPERTS = 64
TOP_K = 6
N_GROUPS = 8
TOPK_GROUPS = 4
GROUP_SIZE = N_EXPERTS // N_GROUPS
ROUTED_SCALE = 2.5
GATE_W = 128

V7X_LANES = 128
V7X_MXU_DIM = 256
V7X_VMEM_BYTES = 64 * 1024 * 1024
V7X_SC_CORES = 2
V7X_SC_SUBCORES = 16
V7X_SC_WORKERS = V7X_SC_CORES * V7X_SC_SUBCORES

TOKEN_TILE = 256
MLA_Q_TILE = 256
MLA_HEADS_PER_STEP = 2
SWA_Q_TILE = 256
SWA_BAND = SWA_Q_TILE + 2 * WINDOW
WKV_CHUNK = 64
WKV_PAIR = 2 * RWKV_HEAD
WKV_CHUNKS_PER_STEP = 4
MOE_ROW_TILE = 512
SAMPLE_GROUPS = 2
SC_MAX_CHUNK = 64

LOG2E = 1.4426950408889634
MIB = 1024 * 1024
VMEM_RESERVE_BYTES = 4 * MIB


def _vmem_limit(mib):
    return min(mib * MIB, V7X_VMEM_BYTES - VMEM_RESERVE_BYTES)


def _dot(a, b):
    return jnp.dot(a.astype(BF16), b.astype(BF16), preferred_element_type=F32)


def _dot_nt(a, b):
    return lax.dot_general(a.astype(BF16), b.astype(BF16), (((1,), (1,)), ((), ())),
                           preferred_element_type=F32)


def _dot_tn(a, b):
    return lax.dot_general(a.astype(BF16), b.astype(BF16), (((0,), (0,)), ((), ())),
                           preferred_element_type=F32)


def _sigmoid(x):
    return 1.0 / (1.0 + jnp.exp(-x))


def _silu(x):
    return x * _sigmoid(x)


def _rms(x, g):
    return x * lax.rsqrt(jnp.mean(x * x, axis=-1, keepdims=True) + NORM_EPS) * g


def _norm_mod(x, g, shift, scale):
    return _rms(x, g) * (1.0 + scale) + shift


def _split_dot(x, w):
    hi = x.astype(BF16)
    lo = (x - hi.astype(F32)).astype(BF16)
    return (jnp.dot(hi, w, preferred_element_type=F32) + jnp.dot(lo, w, preferred_element_type=F32))


def _head_sum(x, bd):
    w = bd.shape[0]
    parts = [_split_dot(x[:, c * w:(c + 1) * w], bd) for c in range(x.shape[1] // w)]
    return jnp.concatenate(parts, axis=1)


def _after(dep, kernel, in_specs, args, n_lead=0):
    if dep is None:
        return kernel, list(in_specs), list(args)
    n_in = n_lead + len(in_specs)

    def ordered(*refs):
        return kernel(*refs[:n_in], *refs[n_in + 1:])

    return ordered, list(in_specs) + [pl.BlockSpec(memory_space=pl.ANY)], list(args) + [dep]


def _ada_kernel(c_ref, w_ref, b_ref, o_ref):
    s = _silu(c_ref[...])
    o_ref[...] = jnp.dot(s, w_ref[0], precision=HIGHEST, preferred_element_type=F32) + b_ref[0]


def _ada_mods(cc, w, b, layer):
    rows, d = cc.shape
    depth, _, n = w.shape
    return pl.pallas_call(
        _ada_kernel,
        out_shape=jax.ShapeDtypeStruct((rows, n), F32),
        grid=(n // d,),
        in_specs=[pl.BlockSpec((rows, d), lambda i: (0, 0)),
                  pl.BlockSpec((1, d, d), lambda i: (layer, 0, i)),
                  pl.BlockSpec((1, 1, d), lambda i: (layer, 0, i))],
        out_specs=pl.BlockSpec((rows, d), lambda i: (0, i)),
        compiler_params=pltpu.CompilerParams(dimension_semantics=("parallel",),
                                             vmem_limit_bytes=_vmem_limit(32)),
        name="ada_mods",
    )(cc, w, b.reshape(depth, 1, n))


def _rope128(x, cos, sin, first_half):
    rot = jnp.where(first_half, -pltpu.roll(x, V7X_LANES - 16, axis=1), pltpu.roll(x, 16, axis=1))
    return x * cos + rot * sin


_C_CQ = 0
_C_CKV = _C_CQ + MLA_Q_RANK
_C_QS = _C_CKV + MLA_KV_RANK
_C_KS = _C_QS + SWA_HEADS * SWA_HEAD_DIM
_C_VS = _C_KS + SWA_KV_HEADS * V7X_MXU_DIM
_C_KR = _C_VS + SWA_KV_HEADS * V7X_MXU_DIM
_C_END = _C_KR + V7X_LANES
_SWA_W = SWA_KV_HEADS * V7X_MXU_DIM
_MLA_QK_W = MLA_HEADS * V7X_MXU_DIM


def _stream_specs(stream, nct, tl):
    ctx_arr, lat_arr, b0, lat_off = stream
    d = ctx_arr.shape[2]
    return [pl.BlockSpec((1, tl, d), lambda i, j: (i + b0, jnp.minimum(j, nct - 1), 0)),
            pl.BlockSpec((1, tl, d), lambda i, j: (i + b0, jnp.maximum(j - nct, 0) + lat_off, 0))]


def _stream_tile(c_ref, x_ref, nct):
    rows = c_ref.shape[1]
    take_ctx = lax.broadcasted_iota(jnp.int32, (rows, 1), 0) < jnp.where(pl.program_id(1) < nct, rows, 0)
    return jnp.where(take_ctx, c_ref[0], x_ref[0])


def _attn_proj_kernel(c_ref, x_ref, mods_ref, g_ref, win_ref, qn_ref, kvn_ref, wuq_ref, wuk_ref, wuvt_ref, cos_ref,
                      sin_ref, q_ref, k_ref, vt_ref, qs_ref, ks_ref, vs_ref, *, nct):
    m = mods_ref[0, 0]
    n = _norm_mod(_stream_tile(c_ref, x_ref, nct), g_ref[...], m[0:1], m[1:2])
    u = _dot(n, win_ref[...])
    cos = cos_ref[...]
    sin = sin_ref[...]
    lane = lax.broadcasted_iota(jnp.int32, (1, V7X_LANES), 1)
    first_half = (lane % 32) < 16

    def rope(x):
        return _rope128(x, cos, sin, first_half)

    scale_a = (MLA_NOPE + MLA_ROPE) ** -0.5 * LOG2E
    scale_b = SWA_HEAD_DIM ** -0.5 * LOG2E
    q = _dot(_rms(u[:, _C_CQ:_C_CKV], qn_ref[...]), wuq_ref[...])
    ckv = _rms(u[:, _C_CKV:_C_QS], kvn_ref[...])
    kn = _dot(ckv, wuk_ref[...])
    vt_ref[0] = _dot_nt(wuvt_ref[...], ckv).astype(BF16)
    kr = rope(u[:, _C_KR:_C_END]).astype(BF16)
    for h in range(MLA_HEADS):
        o = h * V7X_MXU_DIM
        q_ref[0, :, o:o + V7X_LANES] = (q[:, o:o + V7X_LANES] * scale_a).astype(BF16)
        q_ref[0, :, o + V7X_LANES:o + V7X_MXU_DIM] = (rope(q[:, o + V7X_LANES:o + V7X_MXU_DIM]) * scale_a).astype(BF16)
        k_ref[0, :, o:o + V7X_LANES] = kn[:, h * MLA_NOPE:(h + 1) * MLA_NOPE].astype(BF16)
        k_ref[0, :, o + V7X_LANES:o + V7X_MXU_DIM] = kr
    for c in range((_C_KS - _C_QS) // V7X_LANES):
        o = c * V7X_LANES
        qs_ref[0, :, o:o + V7X_LANES] = (rope(u[:, _C_QS + o:_C_QS + o + V7X_LANES]) * scale_b).astype(BF16)
    for c in range(_SWA_W // V7X_LANES):
        o = c * V7X_LANES
        ks_ref[0, :, o:o + V7X_LANES] = rope(u[:, _C_KS + o:_C_KS + o + V7X_LANES]).astype(BF16)
    vs_ref[0] = u[:, _C_VS:_C_KR].astype(BF16)


def _attn_proj(stream, b, l, mods, g, win, qn, kvn, wuq, wuk, wuvt, cos, sin, nct, dep=None):
    d = stream[0].shape[2]
    tl = TOKEN_TILE
    tok = lambda w: pl.BlockSpec((1, tl, w), lambda i, j: (i, j, 0))
    full = lambda a: pl.BlockSpec(a.shape, lambda i, j: (0,) * a.ndim)
    sds = jax.ShapeDtypeStruct
    dv = MLA_HEADS * MLA_V
    kern, in_specs, args = _after(
        dep, functools.partial(_attn_proj_kernel, nct=nct),
        _stream_specs(stream, nct, tl) + [
            pl.BlockSpec((1, 1, N_MODS, d), lambda i, j: (i, jnp.where(j < nct, 0, 1), 0, 0)),
            full(g), full(win), full(qn), full(kvn), full(wuq), full(wuk), full(wuvt),
            pl.BlockSpec((tl, V7X_LANES), lambda i, j: (j, 0)),
            pl.BlockSpec((tl, V7X_LANES), lambda i, j: (j, 0))],
        [stream[0], stream[1], mods, g, win, qn, kvn, wuq, wuk, wuvt, cos, sin])
    return pl.pallas_call(
        kern,
        out_shape=[sds((b, l, _MLA_QK_W), BF16), sds((b, l, _MLA_QK_W), BF16), sds((b, dv, l), BF16),
                   sds((b, l, SWA_HEADS * SWA_HEAD_DIM), BF16), sds((b, l, _SWA_W), BF16), sds((b, l, _SWA_W), BF16)],
        grid=(b, l // tl),
        in_specs=in_specs,
        out_specs=[tok(_MLA_QK_W), tok(_MLA_QK_W), pl.BlockSpec((1, dv, tl), lambda i, j: (i, 0, j)),
                   tok(SWA_HEADS * SWA_HEAD_DIM), tok(_SWA_W), tok(_SWA_W)],
        compiler_params=pltpu.CompilerParams(dimension_semantics=("parallel", "parallel"),
                                             vmem_limit_bytes=_vmem_limit(48)),
        name="attn_proj",
    )(*args)


def _mla_kernel(q_ref, k_ref, vt_ref, o_ref, *, nct_q, lc):
    hw = V7X_MXU_DIM

    def attend(nk):
        st = [_dot_nt(k_ref[0, 0:nk, hh * hw:(hh + 1) * hw], q_ref[0, :, hh * hw:(hh + 1) * hw])
              for hh in range(MLA_HEADS_PER_STEP)]
        for hh, s in enumerate(st):
            p = jnp.exp2(s - jnp.max(s, axis=0, keepdims=True))
            den = jnp.sum(p, axis=0, keepdims=True)
            ot = _dot(vt_ref[0, hh * MLA_V:(hh + 1) * MLA_V, 0:nk], p) / den
            o_ref[0, :, hh * MLA_V:(hh + 1) * MLA_V] = ot.T.astype(o_ref.dtype)

    @pl.when(pl.program_id(2) < nct_q)
    def _():
        attend(lc)

    @pl.when(pl.program_id(2) >= nct_q)
    def _():
        attend(k_ref.shape[1])


def _mla_attention(q, k, vt, lc, q_tile0, dep=None):
    b, l, _ = q.shape
    tq = MLA_Q_TILE
    hps = MLA_HEADS_PER_STEP
    kern, in_specs, args = _after(
        dep, functools.partial(_mla_kernel, nct_q=lc // tq - q_tile0, lc=lc),
        [pl.BlockSpec((1, tq, hps * V7X_MXU_DIM), lambda i, h, j: (i, j + q_tile0, h)),
         pl.BlockSpec((1, l, hps * V7X_MXU_DIM), lambda i, h, j: (i, 0, h)),
         pl.BlockSpec((1, hps * MLA_V, l), lambda i, h, j: (i, h, 0))],
        [q, k, vt])
    return pl.pallas_call(
        kern,
        out_shape=jax.ShapeDtypeStruct((b, l, MLA_HEADS * MLA_V), BF16),
        grid=(b, MLA_HEADS // hps, l // tq - q_tile0),
        in_specs=in_specs,
        out_specs=pl.BlockSpec((1, tq, hps * MLA_V), lambda i, h, j: (i, j + q_tile0, h)),
        compiler_params=pltpu.CompilerParams(dimension_semantics=("parallel", "parallel", "parallel"),
                                             vmem_limit_bytes=_vmem_limit(48)),
        name="mla_attention",
    )(*args)


def _swa_kernel(sink_ref, q_ref, k_ref, v_ref, o_ref, *, lc, q_tile0):
    tq = SWA_Q_TILE
    l = k_ref.shape[1]
    r0 = (pl.program_id(1) + q_tile0) * tq
    start = pl.multiple_of(jnp.clip(r0 - WINDOW, lc, l - SWA_BAND), WINDOW)
    rows = SWA_GROUP * tq
    row = lax.broadcasted_iota(jnp.int32, (rows, 1), 0)
    qpos = jnp.where(r0 >= lc, r0, -l) + row % tq
    kpos = start + lax.broadcasted_iota(jnp.int32, (1, SWA_BAND), 1)
    valid = jnp.abs(qpos - kpos) <= WINDOW
    lane = lax.broadcasted_iota(jnp.int32, (1, V7X_MXU_DIM), 1)
    head = [(lane // SWA_HEAD_DIM) == hh for hh in range(SWA_GROUP)]
    groups = range(SWA_KV_HEADS)
    sls = [slice(g * V7X_MXU_DIM, (g + 1) * V7X_MXU_DIM) for g in groups]
    qstack = []
    for sl in sls:
        qg = q_ref[0, :, sl]
        zero = jnp.zeros_like(qg)
        qstack.append(jnp.concatenate([jnp.where(head[hh], qg, zero) for hh in range(SWA_GROUP)], axis=0))
    sc = [_dot_nt(qstack[g], k_ref[0, 0:lc, sls[g]]) for g in groups]
    sb = [_dot_nt(qstack[g], k_ref[0, pl.ds(start, SWA_BAND), sls[g]]) for g in groups]
    for g in groups:
        sl = sls[g]
        sbm = jnp.where(valid, sb[g], NEG_INF)
        sk = jnp.zeros((rows, 1), F32)
        for hh in range(SWA_GROUP):
            sk = jnp.where(row // tq == hh, sink_ref[g * SWA_GROUP + hh] * LOG2E, sk)
        mx = jnp.maximum(jnp.maximum(jnp.max(sc[g], axis=-1, keepdims=True), jnp.max(sbm, axis=-1, keepdims=True)), sk)
        pc = jnp.exp2(sc[g] - mx)
        pb = jnp.exp2(sbm - mx)
        den = jnp.sum(pc, axis=-1, keepdims=True) + jnp.sum(pb, axis=-1, keepdims=True) + jnp.exp2(sk - mx)
        ostack = (_dot(pc, v_ref[0, 0:lc, sl]) + _dot(pb, v_ref[0, pl.ds(start, SWA_BAND), sl])) / den
        o = jnp.zeros((tq, V7X_MXU_DIM), F32)
        for hh in range(SWA_GROUP):
            o = o + jnp.where(head[hh], ostack[hh * tq:(hh + 1) * tq], 0.0)
        o_ref[0, :, sl] = o.astype(o_ref.dtype)


def _swa_attention(sinks, q, k, v, lc, q_tile0, dep=None):
    b, l, _ = q.shape
    tq = SWA_Q_TILE
    kern, in_specs, args = _after(
        dep, functools.partial(_swa_kernel, lc=lc, q_tile0=q_tile0),
        [pl.BlockSpec(memory_space=pltpu.SMEM),
         pl.BlockSpec((1, tq, SWA_HEADS * SWA_HEAD_DIM), lambda i, j: (i, j + q_tile0, 0)),
         pl.BlockSpec((1, l, _SWA_W), lambda i, j: (i, 0, 0)),
         pl.BlockSpec((1, l, _SWA_W), lambda i, j: (i, 0, 0))],
        [sinks, q, k, v])
    return pl.pallas_call(
        kern,
        out_shape=jax.ShapeDtypeStruct((b, l, SWA_HEADS * SWA_HEAD_DIM), BF16),
        grid=(b, l // tq - q_tile0),
        in_specs=in_specs,
        out_specs=pl.BlockSpec((1, tq, SWA_HEADS * SWA_HEAD_DIM), lambda i, j: (i, j + q_tile0, 0)),
        compiler_params=pltpu.CompilerParams(dimension_semantics=("parallel", "parallel"),
                                             vmem_limit_bytes=_vmem_limit(48)),
        name="swa_attention",
    )(*args)


def _pack_bf16_pair(x):
    w = x.shape[1] // 2
    lo = pltpu.bitcast(x[:, :w].astype(BF16).astype(F32), jnp.int32)
    hi = pltpu.bitcast(x[:, w:].astype(BF16).astype(F32), jnp.int32)
    return lax.shift_right_logical(lo, jnp.int32(16)) | (hi & jnp.int32(-65536))


def _unpack_bf16_pair(p):
    return pltpu.bitcast(p << 16, F32), pltpu.bitcast(p & jnp.int32(-65536), F32)


def _route(n2, wrt, bias, run_ref):
    n_hi = n2.astype(BF16)
    n_lo = (n2 - n_hi.astype(F32)).astype(BF16)
    w_hi = wrt.astype(BF16)
    w_lo = (wrt - w_hi.astype(F32)).astype(BF16)
    logits = _dot_nt(w_hi, n_hi) + (_dot_nt(w_hi, n_lo) + _dot_nt(w_lo, n_hi))
    rows = logits.shape[1]
    scores = _sigmoid(logits[0:N_EXPERTS])

    def select(sc2):
        cols = sc2.shape[1]
        shape3 = (N_GROUPS, GROUP_SIZE, cols)
        choice = sc2.reshape(shape3) + bias
        ji = lax.broadcasted_iota(jnp.int32, shape3, 1).astype(F32)
        m1 = jnp.max(choice, axis=1, keepdims=True)
        first = jnp.min(jnp.where(choice == m1, ji, float(GROUP_SIZE)), axis=1, keepdims=True)
        m2 = jnp.max(jnp.where(ji == first, -jnp.inf, choice), axis=1, keepdims=True)
        gs = m1 + m2
        gidx = lax.broadcasted_iota(jnp.int32, gs.shape, 0).astype(F32)
        gsel = jnp.zeros_like(gs)
        for _ in range(TOPK_GROUPS):
            mx = jnp.max(gs, axis=0, keepdims=True)
            pick = gidx == jnp.min(jnp.where(gs == mx, gidx, float(N_GROUPS)), axis=0, keepdims=True)
            gsel = jnp.where(pick, 1.0, gsel)
            gs = jnp.where(pick, -jnp.inf, gs)
        cand = jnp.where(gsel > 0.0, choice, -jnp.inf).reshape(N_EXPERTS, cols)
        eidx = lax.broadcasted_iota(jnp.int32, (N_EXPERTS, cols), 0).astype(F32)
        out = []
        for _ in range(TOP_K):
            mx = jnp.max(cand, axis=0, keepdims=True)
            pick = eidx == jnp.min(jnp.where(cand == mx, eidx, float(N_EXPERTS)), axis=0, keepdims=True)
            out.append(jnp.where(pick, 1.0, 0.0))
            cand = jnp.where(pick, -jnp.inf, cand)
        return out

    blocks = [select(scores[:, o:o + V7X_LANES]) for o in range(0, rows, V7X_LANES)]
    picks = [jnp.concatenate([blk[k] for blk in blocks], axis=1) > 0.0 for k in range(TOP_K)]
    ei = lax.broadcasted_iota(jnp.int32, (N_EXPERTS, rows), 0).astype(F32)
    esel = jnp.zeros((N_EXPERTS, rows), F32)
    for pick in picks:
        esel = jnp.where(pick, 1.0, esel)
    before = jnp.where(lax.broadcasted_iota(jnp.int32, (rows, rows), 0) < lax.broadcasted_iota(jnp.int32, (rows, rows), 1),
                       1.0, 0.0).astype(BF16)
    slot = jnp.dot(esel.astype(BF16), before, preferred_element_type=F32) + run_ref[...]
    run_ref[...] += jnp.sum(esel, axis=1, keepdims=True)
    sc = [jnp.sum(jnp.where(pick, scores, 0.0), axis=0, keepdims=True) for pick in picks]
    tot = sc[0]
    for x in sc[1:]:
        tot = tot + x
    k8 = lax.broadcasted_iota(jnp.int32, (8, rows), 0)
    kw = lax.broadcasted_iota(jnp.int32, (GATE_W, rows), 0)
    eid = jnp.zeros((8, rows), jnp.int32)
    rank = jnp.zeros((8, rows), jnp.int32)
    wk = jnp.zeros((GATE_W, rows), F32)
    for k, pick in enumerate(picks):
        e_k = jnp.sum(jnp.where(pick, ei, 0.0), axis=0, keepdims=True).astype(jnp.int32)
        r_k = jnp.sum(jnp.where(pick, slot, 0.0), axis=0, keepdims=True).astype(jnp.int32)
        eid = jnp.where(k8 == k, e_k, eid)
        rank = jnp.where(k8 == k, r_k, rank)
        wk = jnp.where(kw == k, sc[k] * (ROUTED_SCALE / tot), wk)
    return eid, rank, wk.T


def _mixer_tail(o, h, m, gffn_ref, wrt_ref, bias_ref, hn_ref, n2_ref, eid_ref, rank_ref, w_ref, cnt_ref, run_ref):
    @pl.when((pl.program_id(0) == 0) & (pl.program_id(1) == 0))
    def _():
        run_ref[...] = jnp.zeros_like(run_ref)

    hn = h + m[2:3] * o
    hn_ref[0] = hn
    n2 = _norm_mod(hn, gffn_ref[...], m[3:4], m[4:5])
    n2_ref[0] = _pack_bf16_pair(n2)
    eid, rank, wcols = _route(n2, wrt_ref[...], bias_ref[...], run_ref)
    eid_ref[0] = eid
    rank_ref[0] = rank
    w_ref[0] = wcols
    cnt_ref[...] = run_ref[...]


def _attn_out_kernel(a_ref, b_ref, c_ref, x_ref, mods_ref, wo_ref, gffn_ref, wrt_ref, bias_ref,
                     hn_ref, n2_ref, eid_ref, rank_ref, w_ref, cnt_ref, run_ref, *, nct):
    wa = MLA_HEADS * MLA_V
    o = _dot(a_ref[0], wo_ref[0:wa, :]) + _dot(b_ref[0], wo_ref[wa:, :])
    _mixer_tail(o, _stream_tile(c_ref, x_ref, nct), mods_ref[0, 0], gffn_ref, wrt_ref, bias_ref, hn_ref, n2_ref,
                eid_ref, rank_ref, w_ref, cnt_ref, run_ref)


def _tail_outs(b, l, d):
    tl = TOKEN_TILE
    nt = l // tl
    sds = jax.ShapeDtypeStruct
    tok = lambda w: pl.BlockSpec((1, tl, w), lambda i, j: (i, j, 0))
    blk = pl.BlockSpec((1, 8, tl), lambda i, j: (i * nt + j, 0, 0))
    shapes = [sds((b, l, d), F32), sds((b, l, d // 2), jnp.int32), sds((b * nt, 8, tl), jnp.int32),
              sds((b * nt, 8, tl), jnp.int32), sds((b, l, GATE_W), F32), sds((N_EXPERTS, 1), F32)]
    specs = [tok(d), tok(d // 2), blk, blk, tok(GATE_W), pl.BlockSpec((N_EXPERTS, 1), lambda i, j: (0, 0))]
    return shapes, specs


def _attn_out(a, bm, stream, mods, wo, gffn, wrt, bias, nct, dep=None):
    b, l, _ = a.shape
    d = stream[0].shape[2]
    tl = TOKEN_TILE
    tok = lambda w: pl.BlockSpec((1, tl, w), lambda i, j: (i, j, 0))
    full = lambda x: pl.BlockSpec(x.shape, lambda i, j: (0,) * x.ndim)
    shapes, specs = _tail_outs(b, l, d)
    kern, in_specs, args = _after(
        dep, functools.partial(_attn_out_kernel, nct=nct),
        [tok(a.shape[2]), tok(bm.shape[2])] + _stream_specs(stream, nct, tl) + [
            pl.BlockSpec((1, 1, N_MODS, d), lambda i, j: (i, jnp.where(j < nct, 0, 1), 0, 0)),
            full(wo), full(gffn), full(wrt), full(bias)],
        [a, bm, stream[0], stream[1], mods, wo, gffn, wrt, bias])
    return pl.pallas_call(
        kern,
        out_shape=shapes,
        grid=(b, l // tl),
        in_specs=in_specs,
        out_specs=specs,
        scratch_shapes=[pltpu.VMEM((N_EXPERTS, 1), F32)],
        compiler_params=pltpu.CompilerParams(dimension_semantics=("arbitrary", "arbitrary"),
                                             vmem_limit_bytes=_vmem_limit(40)),
        name="attn_out",
    )(*args)


def _moe_dest_kernel(off_ref, eid_ref, rank_ref, dest_ref):
    eid = eid_ref[...]
    dest = rank_ref[...]
    for e in range(N_EXPERTS):
        dest = dest + jnp.where(eid == e, off_ref[e], 0)
    dest_ref[...] = dest


def _moe_dest(off, eid, rank):
    return pl.pallas_call(
        _moe_dest_kernel,
        out_shape=jax.ShapeDtypeStruct(eid.shape, jnp.int32),
        in_specs=[pl.BlockSpec(memory_space=pltpu.SMEM),
                  pl.BlockSpec(eid.shape, lambda: (0, 0, 0)), pl.BlockSpec(eid.shape, lambda: (0, 0, 0))],
        out_specs=pl.BlockSpec(eid.shape, lambda: (0, 0, 0)),
        name="moe_dest",
    )(off, eid, rank)


def _sc_mesh():
    return plsc.VectorSubcoreMesh(core_axis_name="c", subcore_axis_name="s",
                                  num_cores=V7X_SC_CORES, num_subcores=V7X_SC_SUBCORES)


def _sc_chunk(rows_per_worker):
    return max(c for c in range(8, SC_MAX_CHUNK + 1, 8) if rows_per_worker % c == 0)


def _sc_dispatch(xp, dest, p_rows):
    t, w = xp.shape
    tpw = t // V7X_SC_WORKERS
    ch = _sc_chunk(tpw)

    @functools.partial(
        pl.kernel, mesh=_sc_mesh(), out_type=jax.ShapeDtypeStruct((p_rows, w), xp.dtype),
        scratch_types=[pltpu.VMEM((ch, w), xp.dtype)] + [pltpu.VMEM((ch,), jnp.int32)] * TOP_K
        + [pltpu.SemaphoreType.DMA, pltpu.SemaphoreType.DMA],
        name="moe_dispatch")
    def run(x_hbm, dest_hbm, out_hbm, rows_v, *rest):
        idx, (sem_i, sem_o) = rest[:TOP_K], rest[TOP_K:]
        base = (lax.axis_index("s") * V7X_SC_CORES + lax.axis_index("c")) * tpw

        @pl.loop(0, tpw // ch)
        def _(i):
            t0 = base + i * ch
            loads = [pltpu.async_copy(dest_hbm.at[k, pl.ds(t0, ch)], idx[k], sem_i) for k in range(TOP_K)]
            pltpu.sync_copy(x_hbm.at[pl.ds(t0, ch)], rows_v)
            for c in loads:
                c.wait()
            puts = [pltpu.async_copy(rows_v, out_hbm.at[idx[k]], sem_o) for k in range(TOP_K)]
            for c in puts:
                c.wait()

    return run(xp, dest)


def _sc_gather(ys, dest, t):
    w = ys.shape[1]
    tpw = t // V7X_SC_WORKERS
    ch = _sc_chunk(tpw)

    @functools.partial(
        pl.kernel, mesh=_sc_mesh(), out_type=jax.ShapeDtypeStruct((TOP_K, t, w), ys.dtype),
        scratch_types=[pltpu.VMEM((ch, w), ys.dtype)] * 2 + [pltpu.VMEM((ch,), jnp.int32)] * TOP_K
        + [pltpu.SemaphoreType.DMA] * 5,
        name="moe_gather")
    def run(y_hbm, dest_hbm, out_hbm, rows_a, rows_b, *rest):
        idx, (sem_i, sem_ga, sem_gb, sem_wa, sem_wb) = rest[:TOP_K], rest[TOP_K:]
        rows, sem_g, sem_w = (rows_a, rows_b), (sem_ga, sem_gb), (sem_wa, sem_wb)
        base = (lax.axis_index("s") * V7X_SC_CORES + lax.axis_index("c")) * tpw

        @pl.loop(0, tpw // ch)
        def _(i):
            t0 = base + i * ch
            loads = [pltpu.async_copy(dest_hbm.at[k, pl.ds(t0, ch)], idx[k], sem_i) for k in range(TOP_K)]
            for c in loads:
                c.wait()
            gets, puts = [None] * TOP_K, [None] * TOP_K
            gets[0] = pltpu.async_copy(y_hbm.at[idx[0]], rows[0], sem_g[0])
            for k in range(TOP_K):
                if k + 1 < TOP_K:
                    if k >= 1:
                        puts[k - 1].wait()
                    gets[k + 1] = pltpu.async_copy(y_hbm.at[idx[k + 1]], rows[(k + 1) % 2], sem_g[(k + 1) % 2])
                gets[k].wait()
                puts[k] = pltpu.async_copy(rows[k % 2], out_hbm.at[k, pl.ds(t0, ch)], sem_w[k % 2])
            puts[TOP_K - 2].wait()
            puts[TOP_K - 1].wait()

    return run(ys, dest)


def _cache_mlp_weights(wg, wu, wd, wgu_ref, wdb_ref):
    f = wg.shape[1]
    wgu_ref[:, 0:f] = wg.astype(BF16)
    wgu_ref[:, f:] = wu.astype(BF16)
    wdb_ref[...] = wd.astype(BF16)


def _gated_mlp(xp, wgu_ref, wdb_ref):
    lo, hi = _unpack_bf16_pair(xp)
    x = jnp.concatenate([lo.astype(BF16), hi.astype(BF16)], axis=1)
    gu = jnp.dot(x, wgu_ref[...], preferred_element_type=F32)
    f = gu.shape[1] // 2
    return _dot(_silu(gu[:, :f]) * gu[:, f:], wdb_ref[...])


def _moe_expert_kernel(te_ref, tb_ref, nv_ref, x_ref, wga_ref, wua_ref, wda_ref, wgb_ref, wub_ref, wdb_ref, y_ref,
                       gu_a, dn_a, gu_b, dn_b, ids_ref):
    i = pl.program_id(0)
    tm = MOE_ROW_TILE
    nv = nv_ref[0]
    first = 2 * jnp.minimum(i, (nv - 1) // 2)
    ea = te_ref[first]
    eb = te_ref[first + 1]
    two = 2 * i + 1 < nv

    @pl.when(i == 0)
    def _():
        ids_ref[0] = -1
        ids_ref[1] = -1

    @pl.when(ids_ref[0] != ea)
    def _():
        _cache_mlp_weights(wga_ref[0, 0], wua_ref[0, 0], wda_ref[0, 0], gu_a, dn_a)
        ids_ref[0] = ea

    @pl.when(two & (eb != ea) & (ids_ref[1] != eb))
    def _():
        _cache_mlp_weights(wgb_ref[0, 0], wub_ref[0, 0], wdb_ref[0, 0], gu_b, dn_b)
        ids_ref[1] = eb

    @pl.when(two & (eb == ea))
    def _():
        y_ref[...] = _pack_bf16_pair(_gated_mlp(x_ref[...], gu_a, dn_a))

    @pl.when((2 * i < nv) & jnp.logical_not(two & (eb == ea)))
    def _():
        y_ref[0:tm, :] = _pack_bf16_pair(_gated_mlp(x_ref[0:tm, :], gu_a, dn_a))

    @pl.when(two & (eb != ea))
    def _():
        y_ref[tm:, :] = _pack_bf16_pair(_gated_mlp(x_ref[tm:, :], gu_b, dn_b))


def _moe_experts(tile_expert, n_valid, xs, wg, wu, wd, layer, dep=None):
    p_rows, w = xs.shape
    tm = MOE_ROW_TILE
    _, _, d, f = wg.shape
    npair = p_rows // (2 * tm)
    pairs = tile_expert.reshape(npair, 2)
    tile_b = jnp.maximum(lax.cummax(jnp.where(pairs[:, 1] != pairs[:, 0], pairs[:, 1], -1)), 0)
    step = lambda i, nv: jnp.minimum(i, (nv[0] - 1) // 2)
    spec_a = lambda shp: pl.BlockSpec((1, 1) + shp, lambda i, te, tb, nv: (layer, te[2 * step(i, nv)], 0, 0))
    spec_b = lambda shp: pl.BlockSpec((1, 1) + shp, lambda i, te, tb, nv: (layer, tb[step(i, nv)], 0, 0))
    rows = pl.BlockSpec((2 * tm, w), lambda i, te, tb, nv: (step(i, nv), 0))
    kern, in_specs, args = _after(
        dep, _moe_expert_kernel,
        [rows, spec_a((d, f)), spec_a((d, f)), spec_a((f, d)), spec_b((d, f)), spec_b((d, f)), spec_b((f, d))],
        [tile_expert, tile_b, n_valid, xs, wg, wu, wd, wg, wu, wd], n_lead=3)
    return pl.pallas_call(
        kern,
        out_shape=jax.ShapeDtypeStruct((p_rows, w), xs.dtype),
        grid_spec=pltpu.PrefetchScalarGridSpec(
            num_scalar_prefetch=3, grid=(npair,),
            in_specs=in_specs,
            out_specs=rows,
            scratch_shapes=[pltpu.VMEM((d, 2 * f), BF16), pltpu.VMEM((f, d), BF16),
                            pltpu.VMEM((d, 2 * f), BF16), pltpu.VMEM((f, d), BF16), pltpu.SMEM((2,), jnp.int32)]),
        compiler_params=pltpu.CompilerParams(dimension_semantics=("arbitrary",),
                                             vmem_limit_bytes=_vmem_limit(48)),
        name="moe_experts",
    )(*args)


def _moe_combine_kernel(yg_ref, w_ref, xp_ref, sg_ref, su_ref, sd_ref, h_ref, mods_ref, gfin_ref, *rest, final_norm):
    o_ref, wgu_ref, wdb_ref = rest[-3:]

    @pl.when((pl.program_id(0) == 0) & (pl.program_id(1) == 0))
    def _():
        _cache_mlp_weights(sg_ref[0], su_ref[0], sd_ref[0], wgu_ref, wdb_ref)

    acc = _gated_mlp(xp_ref[0], wgu_ref, wdb_ref)
    half = acc.shape[1] // 2
    lo = acc[:, :half]
    hi = acc[:, half:]
    w = w_ref[0]
    for k in range(TOP_K):
        ylo, yhi = _unpack_bf16_pair(yg_ref[k, 0])
        wk = w[:, k:k + 1]
        lo = lo + wk * ylo
        hi = hi + wk * yhi
    y = h_ref[0] + mods_ref[0, 0, N_MODS - 1:N_MODS, :] * jnp.concatenate([lo, hi], axis=1)
    if final_norm:
        y = _rms(y, gfin_ref[...])
    o_ref[0] = y


def _moe_combine(yg, wcols, xp, sg, su, sd, h, mods, gfin, nct, layer, out_buf, out_b0, out_batch, latent_only,
                 final_norm, dep=None):
    b, l, d = h.shape
    tl = TOKEN_TILE
    tile0 = nct if latent_only else 0
    tok = lambda w: pl.BlockSpec((1, tl, w), lambda i, j: (i, j + tile0, 0))
    lay = lambda x: pl.BlockSpec((1,) + x.shape[1:], lambda i, j: (layer,) + (0,) * (x.ndim - 1))
    args = [yg, wcols, xp, sg, su, sd, h, mods, gfin]
    in_specs = [pl.BlockSpec((TOP_K, 1, tl, d // 2), lambda i, j: (0, i, j + tile0, 0)), tok(GATE_W), tok(d // 2),
                lay(sg), lay(su), lay(sd), tok(d),
                pl.BlockSpec((1, 1, N_MODS, d), lambda i, j: (i, jnp.where(j + tile0 < nct, 0, 1), 0, 0)),
                pl.BlockSpec(gfin.shape, lambda i, j: (0, 0))]
    _, in_specs, args = _after(dep, None, in_specs, args)
    aliases = {}
    if out_buf is not None:
        args.append(out_buf)
        in_specs.append(pl.BlockSpec(memory_space=pl.ANY))
        aliases = {len(args) - 1: 0}
    return pl.pallas_call(
        functools.partial(_moe_combine_kernel, final_norm=final_norm),
        out_shape=jax.ShapeDtypeStruct((out_batch, l - tile0 * tl, d), F32),
        grid=(b, l // tl - tile0),
        in_specs=in_specs,
        out_specs=pl.BlockSpec((1, tl, d), lambda i, j: (i + out_b0, j, 0)),
        scratch_shapes=[pltpu.VMEM((d, 2 * sg.shape[2]), BF16), pltpu.VMEM((sg.shape[2], d), BF16)],
        input_output_aliases=aliases,
        compiler_params=pltpu.CompilerParams(dimension_semantics=("arbitrary", "arbitrary"),
                                             vmem_limit_bytes=_vmem_limit(40)),
        name="moe_combine",
    )(*args)


def _moe_route_rows(n2p, eid, rank, counts, b, l):
    d2 = n2p.shape[2]
    t = b * l
    tm = MOE_ROW_TILE
    n_tiles = 2 * -(-(TOP_K * t + N_EXPERTS * (tm - 1)) // (2 * tm))
    tiles_e = (counts.reshape(N_EXPERTS).astype(jnp.int32) + (tm - 1)) // tm
    tile_end = jnp.cumsum(tiles_e)
    off = (tile_end - tiles_e) * tm
    n_valid = tile_end[-1:]
    tile_id = jnp.minimum(jnp.arange(n_tiles, dtype=jnp.int32), n_valid - 1)
    tile_expert = jnp.sum((tile_end[None, :] <= tile_id[:, None]).astype(jnp.int32), axis=1)
    dest = _moe_dest(off, eid, rank).transpose(1, 0, 2).reshape(8, t)
    xs = _sc_dispatch(n2p.reshape(t, d2), dest, n_tiles * tm)
    return xs, dest, tile_expert, n_valid


def _rwkv_proj_kernel(h_ref, hp_ref, hx_ref, mods_ref, g_ref, mu_ref, wr_ref, wk_ref, wv_ref, g1_ref, g2_ref,
                      w1_ref, w2_ref, a1_ref, a2_ref, w0_ref, a0_ref, kk_ref, ka_ref, rk_ref, bd_ref,
                      r_out, v_out, kk_out, g_out, km_out, b_out, lw_out, bonus_out, *, nct):
    j = pl.program_id(1)
    nt = pl.num_programs(1)
    m = mods_ref[0, 0]
    g = g_ref[...]
    n = _norm_mod(h_ref[0], g, m[0:1], m[1:2])
    tl, d = n.shape
    seg_first = (j == 0) | (j == nct)
    seg_last = (j == nct - 1) | (j == nt - 1)
    n_prev = _norm_mod(hp_ref[0], g, m[0:1], m[1:2])[7:8] * jnp.where(seg_first, 0.0, 1.0)
    n_next = _norm_mod(hx_ref[0], g, m[0:1], m[1:2])[0:1] * jnp.where(seg_last, 0.0, 1.0)
    row = lax.broadcasted_iota(jnp.int32, (tl, 1), 0)
    prev = jnp.where(row == 0, n_prev, pltpu.roll(n, 1, axis=0))
    nxt = jnp.where(row == tl - 1, n_next, pltpu.roll(n, tl - 1, axis=0))
    lane = lax.broadcasted_iota(jnp.int32, (1, d), 1)
    xx = jnp.where(lane < d // 2, prev, nxt) - n
    mu = mu_ref[...]
    bd = bd_ref[...]
    halves = [slice(0, tl // 2), slice(tl // 2, tl)]
    first = []
    for rs in halves:
        nh, xh = n[rs], xx[rs]
        xr, xw, xk, xv, xa, xg = [nh + xh * mu[i:i + 1] for i in range(6)]
        first.append((_dot(xr, wr_ref[...]), _dot(xk, wk_ref[...]), _dot(xv, wv_ref[...]),
                      _dot(xg, g1_ref[...]), _dot(xw, w1_ref[...]), _dot(xa, a1_ref[...])))
    second = []
    for r, k, v, gq, tq, ta in first:
        tw = jnp.tanh(tq)
        kk = k * kk_ref[...]
        second.append((_dot(_sigmoid(gq), g2_ref[...]), [_dot(tw, w2_ref[dr]) for dr in range(2)],
                       [_dot(ta, a2_ref[dr]) for dr in range(2)], kk, _head_sum(kk * kk, bd)))
    for rs, (r, k, v, _, _, _), (gate, zw, za, kk, kk_sq) in zip(halves, first, second):
        kk = kk / jnp.maximum(jnp.sqrt(kk_sq), 1e-12)
        g_out[0, rs, :] = gate.astype(g_out.dtype)
        r_out[0, rs, :] = r.astype(r_out.dtype)
        v_out[0, rs, :] = v.astype(v_out.dtype)
        kk_out[0, rs, :] = kk.astype(kk_out.dtype)
        bonus = jnp.zeros_like(v)
        for dr in range(2):
            lw_out[dr, 0, rs, :] = -jnp.exp(-0.5) * _sigmoid(w0_ref[dr:dr + 1, :] + zw[dr])
            a = _sigmoid(a0_ref[dr:dr + 1, :] + za[dr])
            km = k * (1.0 + (a - 1.0) * ka_ref[...])
            km_out[dr, 0, rs, :] = km.astype(km_out.dtype)
            b_out[dr, 0, rs, :] = (kk * a).astype(b_out.dtype)
            bonus = bonus + _head_sum(r * km * rk_ref[...], bd) * v
        bonus_out[0, rs, :] = bonus.astype(bonus_out.dtype)


def _rwkv_proj(h, mods, g, mu, wr, wk, wv, g1, g2, w1, w2, a1, a2, w0, a0, kk, ka, rk, bd, nct, dep=None):
    b, l, d = h.shape
    tl = TOKEN_TILE
    nb8 = l // 8
    tok = pl.BlockSpec((1, tl, d), lambda i, j: (i, j, 0))
    tok2 = pl.BlockSpec((2, 1, tl, d), lambda i, j: (0, i, j, 0))
    full = lambda x: pl.BlockSpec(x.shape, lambda i, j: (0,) * x.ndim)
    sds = jax.ShapeDtypeStruct
    kern, in_specs, args = _after(
        dep, functools.partial(_rwkv_proj_kernel, nct=nct),
        [tok,
         pl.BlockSpec((1, 8, d), lambda i, j: (i, jnp.maximum(j * (tl // 8) - 1, 0), 0)),
         pl.BlockSpec((1, 8, d), lambda i, j: (i, jnp.minimum((j + 1) * (tl // 8), nb8 - 1), 0)),
         pl.BlockSpec((1, 1, N_MODS, d), lambda i, j: (i, jnp.where(j < nct, 0, 1), 0, 0)),
         full(g), full(mu), full(wr), full(wk), full(wv), full(g1), full(g2), full(w1), full(w2),
         full(a1), full(a2), full(w0), full(a0), full(kk), full(ka), full(rk), full(bd)],
        [h, h, h, mods, g, mu, wr, wk, wv, g1, g2, w1, w2, a1, a2, w0, a0, kk, ka, rk, bd])
    return pl.pallas_call(
        kern,
        out_shape=[sds((b, l, d), BF16), sds((b, l, d), BF16), sds((b, l, d), BF16), sds((b, l, d), BF16),
                   sds((2, b, l, d), BF16), sds((2, b, l, d), BF16), sds((2, b, l, d), F32), sds((b, l, d), BF16)],
        grid=(b, l // tl),
        in_specs=in_specs,
        out_specs=[tok, tok, tok, tok, tok2, tok2, tok2, tok],
        compiler_params=pltpu.CompilerParams(dimension_semantics=("parallel", "parallel"),
                                             vmem_limit_bytes=_vmem_limit(56)),
        name="rwkv_proj",
    )(*args)


def _wkv_kernel(r_ref, v_ref, kk_ref, km_ref, b_ref, lw_ref, y_ref, st_ref):
    c = WKV_CHUNK
    w = WKV_PAIR
    rev = pl.program_id(0)
    sign = 1 - 2 * rev

    @pl.when(pl.program_id(2) == 0)
    def _():
        st_ref[...] = jnp.zeros_like(st_ref)

    ti = lax.broadcasted_iota(jnp.int32, (c, c), 0)
    si = lax.broadcasted_iota(jnp.int32, (c, c), 1)
    tri = jnp.where((si - ti) * sign <= 0, 1.0, 0.0).astype(F32)
    nsub = WKV_CHUNKS_PER_STEP
    subs = [pl.ds(pl.multiple_of(jnp.where(rev == 0, s, nsub - 1 - s) * c, c), c) for s in range(nsub)]
    rt, kt, kh, bh, v32, e_mid = [], [], [], [], [], []
    for rows in subs:
        lw = lw_ref[0, 0, rows, :]
        l_incl = jnp.dot(tri, lw, precision=HIGHEST, preferred_element_type=F32)
        mid = 0.5 * jnp.sum(lw, axis=0, keepdims=True)
        e_neg = jnp.exp(mid - l_incl)
        e_mid.append(jnp.exp(mid))
        rt.append(r_ref[0, rows, :].astype(F32) * jnp.exp(l_incl - mid))
        kt.append(kk_ref[0, rows, :].astype(F32) * jnp.exp(l_incl - lw - mid))
        kh.append(km_ref[0, 0, rows, :].astype(F32) * e_neg)
        bh.append(b_ref[0, 0, rows, :].astype(F32) * e_neg)
        v32.append(v_ref[0, rows, :].astype(F32))

    ri = lax.broadcasted_iota(jnp.int32, (w, w), 0)
    ci = lax.broadcasted_iota(jnp.int32, (w, w), 1)
    same = (ri // c) == (ci // c)
    eye = jnp.where(ri == ci, 1.0, 0.0).astype(F32)
    tl_ = lax.broadcasted_iota(jnp.int32, (c, w), 0)
    jl_ = lax.broadcasted_iota(jnp.int32, (c, w), 1) % c
    strict = (jl_ - tl_) * sign < 0
    incl = (jl_ - tl_) * sign <= 0
    eye2 = jnp.where(jl_ == tl_, 1.0, 0.0).astype(F32)
    lane = lax.broadcasted_iota(jnp.int32, (1, w), 1)
    h0 = lane < RWKV_HEAD

    def rows2(x):
        return jnp.concatenate([jnp.where(h0, x, 0.0), jnp.where(h0, 0.0, x)], axis=0)

    npair = st_ref.shape[0]
    items = [(s, slice(p * w, (p + 1) * w)) for s in range(nsub) for p in range(npair)]
    n = range(len(items))
    em = [e_mid[s][:, sl] for s, sl in items]
    g = [_dot_nt(jnp.concatenate([kt[s][:, sl], rt[s][:, sl]], axis=0),
                 jnp.concatenate([rows2(kh[s][:, sl]), rows2(bh[s][:, sl])], axis=0)) for s, sl in items]
    a_kk = [jnp.where(strict, x[:c, :w], 0.0) for x in g]
    a_rk = [jnp.where(incl, x[c:, :w], 0.0) for x in g]
    a_rb = [jnp.where(incl, x[c:, w:], 0.0) for x in g]
    vi = [v32[s][:, sl] for s, sl in items]
    v_rows = [rows2(x) for x in vi]
    r_pre = [_dot(a_kk[i], v_rows[i]) for i in n]
    m = [jnp.where(strict, -x[:c, w:], 0.0) for x in g]
    tinv = [eye2 + x for x in m]
    m = [_dot(x, rows2(x)) for x in m]
    for _ in range(c.bit_length() - 3):
        both = [_dot(jnp.concatenate([tinv[i], m[i]], axis=0), rows2(m[i])) for i in n]
        tinv = [tinv[i] + both[i][:c] for i in n]
        m = [x[c:] for x in both]
    tinv = [tinv[i] + _dot(tinv[i], rows2(m[i])) for i in n]
    sol = [_dot(tinv[i], jnp.concatenate([rows2(r_pre[i]), rows2(kt[s][:, sl] * em[i])], axis=1))
           for i, (s, sl) in enumerate(items)]
    u_pre = [x[:, :w] for x in sol]
    kq = [x[:, w:] for x in sol]
    y_pre = [_dot(jnp.concatenate([a_rk[i], -a_rb[i]], axis=1),
                  jnp.concatenate([v_rows[i], rows2(u_pre[i])], axis=0)) for i in n]
    r_eff = [rt[s][:, sl] * em[i] - _dot(a_rb[i], rows2(kq[i])) for i, (s, sl) in enumerate(items)]
    bbar = [bh[s][:, sl] * em[i] for i, (s, sl) in enumerate(items)]
    kbar = [kh[s][:, sl] * em[i] for i, (s, sl) in enumerate(items)]
    mmat = [eye * (em[i] * em[i]) - jnp.where(same, _dot_tn(kq[i], bbar[i]), 0.0) for i in n]
    s_pre = [jnp.where(same, _dot_tn(jnp.concatenate([vi[i], -u_pre[i]], axis=0),
                                     jnp.concatenate([kbar[i], bbar[i]], axis=0)), 0.0) for i in n]
    st = [st_ref[p] for p in range(npair)]
    for i, (s, sl) in enumerate(items):
        p = i % npair
        y_ref[0, 0, subs[s], sl] = (_dot_nt(r_eff[i], st[p]) + y_pre[i]).astype(y_ref.dtype)
        hi = st[p].astype(BF16)
        lo = (st[p] - hi.astype(F32)).astype(BF16)
        mb = mmat[i].astype(BF16)
        st[p] = (jnp.dot(hi, mb, preferred_element_type=F32) + jnp.dot(lo, mb, preferred_element_type=F32)
                 + s_pre[i])
    for p in range(npair):
        st_ref[p] = st[p]


def _wkv(r, v, kk, km, bv, lw, lc, dep=None):
    b, l, d = r.shape
    c = WKV_CHUNK * WKV_CHUNKS_PER_STEP
    ncc = lc // c
    nlc = (l - lc) // c

    def chunk(dr, i):
        return jnp.where(dr == 0, i, jnp.where(i < ncc, ncc - 1 - i, nlc + 2 * ncc - 1 - i))

    shared = pl.BlockSpec((1, c, d), lambda dr, bi, i: (bi, chunk(dr, i), 0))
    per_dir = pl.BlockSpec((1, 1, c, d), lambda dr, bi, i: (dr, bi, chunk(dr, i), 0))
    kern, in_specs, args = _after(dep, _wkv_kernel, [shared, shared, shared, per_dir, per_dir, per_dir],
                                  [r, v, kk, km, bv, lw])
    return pl.pallas_call(
        kern,
        out_shape=jax.ShapeDtypeStruct((2, b, l, d), BF16),
        grid=(2, b, l // c),
        in_specs=in_specs,
        out_specs=per_dir,
        scratch_shapes=[pltpu.VMEM((d // WKV_PAIR, WKV_PAIR, WKV_PAIR), F32)],
        compiler_params=pltpu.CompilerParams(dimension_semantics=("parallel", "parallel", "arbitrary"),
                                             vmem_limit_bytes=_vmem_limit(32)),
        name="wkv7_chunked",
    )(*args)


def _rwkv_out_kernel(y_ref, bonus_ref, g_ref, lnw_ref, lnb_ref, wo_ref, bd_ref, h_ref, mods_ref, gffn_ref,
                     wrt_ref, bias_ref, hn_ref, n2_ref, eid_ref, rank_ref, w_ref, cnt_ref, run_ref):
    y = y_ref[0, 0].astype(F32) + y_ref[1, 0].astype(F32)
    bd = bd_ref[...]
    mean = _head_sum(y, bd) * (1.0 / RWKV_HEAD)
    yc = y - mean
    var = _head_sum(yc * yc, bd) * (1.0 / RWKV_HEAD)
    yn = yc * lax.rsqrt(var + GN_EPS) * lnw_ref[...] + lnb_ref[...]
    out = (yn + bonus_ref[0].astype(F32)) * g_ref[0].astype(F32)
    _mixer_tail(_dot(out, wo_ref[...]), h_ref[0], mods_ref[0, 0], gffn_ref, wrt_ref, bias_ref, hn_ref, n2_ref,
                eid_ref, rank_ref, w_ref, cnt_ref, run_ref)


def _rwkv_out(y, bonus, g, lnw, lnb, wo, bd, h, mods, gffn, wrt, bias, nct, dep=None):
    b, l, d = h.shape
    tl = TOKEN_TILE
    tok = lambda w: pl.BlockSpec((1, tl, w), lambda i, j: (i, j, 0))
    full = lambda x: pl.BlockSpec(x.shape, lambda i, j: (0,) * x.ndim)
    shapes, specs = _tail_outs(b, l, d)
    kern, in_specs, args = _after(
        dep, _rwkv_out_kernel,
        [pl.BlockSpec((2, 1, tl, d), lambda i, j: (0, i, j, 0)), tok(d), tok(d),
         full(lnw), full(lnb), full(wo), full(bd), tok(d),
         pl.BlockSpec((1, 1, N_MODS, d), lambda i, j: (i, jnp.where(j < nct, 0, 1), 0, 0)),
         full(gffn), full(wrt), full(bias)],
        [y, bonus, g, lnw, lnb, wo, bd, h, mods, gffn, wrt, bias])
    return pl.pallas_call(
        kern,
        out_shape=shapes,
        grid=(b, l // tl),
        in_specs=in_specs,
        out_specs=specs,
        scratch_shapes=[pltpu.VMEM((N_EXPERTS, 1), F32)],
        compiler_params=pltpu.CompilerParams(dimension_semantics=("arbitrary", "arbitrary"),
                                             vmem_limit_bytes=_vmem_limit(40)),
        name="rwkv_out",
    )(*args)


def _rope_table(n_lat, n_ctx):
    dim = SWA_HEAD_DIM
    nf = dim // 4
    inv = ROPE_THETA ** (-jnp.arange(nf, dtype=F32) / nf)
    row = jnp.repeat(jnp.arange(n_lat // GRID_W, dtype=F32), GRID_W)
    col = jnp.tile(jnp.arange(GRID_W, dtype=F32), n_lat // GRID_W)
    ar = row[:, None] * inv
    ac = col[:, None] * inv
    ang = jnp.concatenate([ar, ar, ac, ac], axis=-1)
    cos = jnp.concatenate([jnp.ones((n_ctx, dim), F32), jnp.cos(ang)], axis=0)
    sin = jnp.concatenate([jnp.zeros((n_ctx, dim), F32), jnp.sin(ang)], axis=0)
    return jnp.tile(cos, (1, 2)), jnp.tile(sin, (1, 2))


def _layout_attn_weights(w_in, w_uq, w_ukv):
    d = w_in.shape[0]
    s0 = MLA_Q_RANK
    s1 = s0 + MLA_KV_RANK
    s2 = s1 + MLA_ROPE
    s3 = s2 + SWA_HEADS * SWA_HEAD_DIM
    s4 = s3 + SWA_KV_HEADS * SWA_HEAD_DIM
    rep = lambda w: jnp.concatenate(
        [jnp.tile(w[:, g * SWA_HEAD_DIM:(g + 1) * SWA_HEAD_DIM], (1, SWA_GROUP)) for g in range(SWA_KV_HEADS)], axis=1)
    win = jnp.concatenate([w_in[:, :s1], w_in[:, s2:s3], rep(w_in[:, s3:s4]), rep(w_in[:, s4:]),
                           w_in[:, s1:s2], jnp.zeros((d, V7X_LANES - MLA_ROPE), w_in.dtype)], axis=1)
    qh = MLA_NOPE + MLA_ROPE
    pad = jnp.zeros((w_uq.shape[0], V7X_MXU_DIM - qh), w_uq.dtype)
    wuq = jnp.concatenate([jnp.concatenate([w_uq[:, h * qh:(h + 1) * qh], pad], axis=1) for h in range(MLA_HEADS)], axis=1)
    kvh = MLA_NOPE + MLA_V
    wuk = jnp.concatenate([w_ukv[:, h * kvh:h * kvh + MLA_NOPE] for h in range(MLA_HEADS)], axis=1)
    wuvt = jnp.concatenate([w_ukv[:, h * kvh + MLA_NOPE:(h + 1) * kvh] for h in range(MLA_HEADS)], axis=1).T
    return win.astype(BF16), wuq.astype(BF16), wuk.astype(BF16), wuvt.astype(BF16)


def _lora_pair(w_down, w_up):
    rank = w_down.shape[2]
    down = jnp.concatenate([w_down[0], w_down[1]], axis=1)
    z = jnp.zeros((rank, w_up.shape[2]), w_up.dtype)
    up = jnp.stack([jnp.concatenate([w_up[0], z], axis=0), jnp.concatenate([z, w_up[1]], axis=0)], axis=0)
    return down.astype(BF16), up.astype(BF16)


def _head_block_diag():
    i = jnp.arange(V7X_MXU_DIM) // RWKV_HEAD
    return (i[:, None] == i[None, :]).astype(BF16)


def kernel(x, c, ctx, c_ctx, ada_w, ada_b, norm_mix, norm_ffn, norm_final, attn_w_in, attn_q_norm, attn_kv_norm, attn_w_uq, attn_w_ukv, attn_sinks, attn_w_o, rwkv_mu, rwkv_w_r, rwkv_w_k, rwkv_w_v, rwkv_w_o, rwkv_g1, rwkv_g2, rwkv_w0, rwkv_w1, rwkv_w2, rwkv_a0, rwkv_a1, rwkv_a2, rwkv_k_k, rwkv_k_a, rwkv_r_k, rwkv_ln_w, rwkv_ln_b, moe_router, moe_bias, moe_w_gate, moe_w_up, moe_w_down, moe_ws_gate, moe_ws_up, moe_ws_down):
    bsz, s, d = x.shape
    lc = ctx.shape[1]
    l = lc + s
    depth = ada_w.shape[0]
    nct = lc // TOKEN_TILE
    assert lc % TOKEN_TILE == 0 and s % TOKEN_TILE == 0 and s >= SWA_BAND and lc % SWA_Q_TILE == 0
    assert lc % (WKV_CHUNK * WKV_CHUNKS_PER_STEP) == 0
    assert d % V7X_MXU_DIM == 0 and WKV_CHUNK * 2 == V7X_LANES
    ngrp = SAMPLE_GROUPS
    bg = bsz // ngrp
    assert bsz % ngrp == 0 and (bg * l) % (8 * V7X_SC_WORKERS) == 0

    assert ngrp == 2
    cos, sin = _rope_table(s, lc)
    bd = _head_block_diag()
    rows = -(-(bsz + 1) // 8) * 8
    cc = jnp.concatenate([c, c_ctx[None, :], jnp.zeros((rows - bsz - 1, d), F32)], axis=0)
    row2 = lambda a: a.reshape(1, -1)
    moe_w = (moe_w_gate, moe_w_up, moe_w_down)
    moe_ws = (moe_ws_gate, moe_ws_up, moe_ws_down)

    shared = {}

    def layer_weights(li):
        if li not in shared:
            i = li // 2
            ada = _ada_mods(cc, ada_w, ada_b, li)
            w = dict(
                mods=jnp.stack([jnp.broadcast_to(ada[bsz].reshape(1, N_MODS, d), (bsz, N_MODS, d)),
                                ada[:bsz].reshape(bsz, N_MODS, d)], axis=1),
                wrt=jnp.concatenate([moe_router[li].T, jnp.zeros((GATE_W - N_EXPERTS, d), F32)], axis=0),
                bias=moe_bias[li].reshape(N_GROUPS, GROUP_SIZE, 1))
            if li % 2 == 0:
                w["win"], w["wuq"], w["wuk"], w["wuvt"] = _layout_attn_weights(attn_w_in[i], attn_w_uq[i], attn_w_ukv[i])
                w["wo"] = attn_w_o[i].astype(BF16)
            else:
                w["w1"], w["w2"] = _lora_pair(rwkv_w1[i], rwkv_w2[i])
                w["a1"], w["a2"] = _lora_pair(rwkv_a1[i], rwkv_a2[i])
                w["wr"], w["wk"], w["wv"], w["wo"] = [x[i].astype(BF16) for x in (rwkv_w_r, rwkv_w_k, rwkv_w_v, rwkv_w_o)]
                w["g1"], w["g2"] = rwkv_g1[i].astype(BF16), rwkv_g2[i].astype(BF16)
            shared[li] = w
        return shared[li]

    groups = [dict(stream=(ctx, x, g * bg, 0), b0=g * bg) for g in range(ngrp)]
    result = [None]

    def run_stage(st, li, name, dep):
        w = layer_weights(li)
        i = li // 2
        with_ctx = li < depth - 1
        mods = w["mods"][st["b0"]:st["b0"] + bg]
        if name == "proj" and li % 2 == 0:
            st["qkv"] = _attn_proj(st["stream"], bg, l, mods, row2(norm_mix[li]), w["win"], row2(attn_q_norm[i]),
                                   row2(attn_kv_norm[i]), w["wuq"], w["wuk"], w["wuvt"], cos, sin, nct, dep=dep)
            return st["qkv"][0]
        if name == "mid" and li % 2 == 0:
            q, k, vt, qs, ks, vs = st.pop("qkv")
            st["a"] = _mla_attention(q, k, vt, lc, 0 if with_ctx else lc // MLA_Q_TILE, dep=dep)
            st["bm"] = _swa_attention(attn_sinks[i], qs, ks, vs, lc, 0 if with_ctx else lc // SWA_Q_TILE, dep=st["a"])
            return st["bm"]
        if name == "proj":
            assert st["stream"][0] is st["stream"][1]
            st["feat"] = _rwkv_proj(st["stream"][0], mods, row2(norm_mix[li]), rwkv_mu[i], w["wr"], w["wk"], w["wv"],
                                    w["g1"], w["g2"], w["w1"], w["w2"], w["a1"], w["a2"], rwkv_w0[i], rwkv_a0[i],
                                    row2(rwkv_k_k[i]), row2(rwkv_k_a[i]), row2(rwkv_r_k[i]), bd, nct, dep=dep)
            return st["feat"][0]
        if name == "mid":
            r, v, kk, gt, km, bv, lw, bonus = st.pop("feat")
            st["y"] = _wkv(r, v, kk, km, bv, lw, lc, dep=dep)
            st["gate"], st["bonus"] = gt, bonus
            return st["y"]
        if name == "out":
            if li % 2 == 0:
                tail = _attn_out(st.pop("a"), st.pop("bm"), st["stream"], mods, w["wo"], row2(norm_ffn[li]),
                                 w["wrt"], w["bias"], nct, dep=dep)
            else:
                tail = _rwkv_out(st.pop("y"), st.pop("bonus"), st.pop("gate"), row2(rwkv_ln_w[i]), row2(rwkv_ln_b[i]),
                                 w["wo"], bd, st["stream"][0], mods, row2(norm_ffn[li]), w["wrt"], w["bias"], nct, dep=dep)
            st["h"], st["n2p"], eid, rank, st["wcols"], counts = tail
            st["xs"], st["dest"], st["tile_expert"], st["n_valid"] = _moe_route_rows(st["n2p"], eid, rank, counts, bg, l)
            return st["h"]
        if name == "experts":
            ys = _moe_experts(st.pop("tile_expert"), st.pop("n_valid"), st.pop("xs"), *moe_w, li, dep=dep)
            st["yg"] = _sc_gather(ys, st.pop("dest"), bg * l).reshape(TOP_K, bg, l, d // 2)
            return ys
        assert name == "combine"
        last = li == depth - 1
        h = _moe_combine(st.pop("yg"), st.pop("wcols"), st.pop("n2p"), *moe_ws, st.pop("h"), mods, row2(norm_final),
                         nct, li, result[0] if last else None, st["b0"] if last else 0, bsz if last else bg,
                         last, last, dep=dep)
        if last:
            result[0] = h
        else:
            st["stream"] = (h, h, 0, nct)
        return h

    order = [(0, 0, "proj"), (0, 0, "mid")]
    for li in range(depth):
        order += [(0, li, "out"), (1, li, "proj"), (0, li, "experts"), (1, li, "mid")]
        if li < depth - 1:
            order += [(0, li, "combine"), (1, li, "out"), (0, li + 1, "proj"), (1, li, "experts"),
                      (0, li + 1, "mid"), (1, li, "combine")]
        else:
            order += [(1, li, "out"), (0, li, "combine"), (1, li, "experts"), (1, li, "combine")]
    dep = None
    for g, li, name in order:
        dep = run_stage(groups[g], li, name, dep)
    return result[0]
```

```python
import functools

import jax
import jax.numpy as jnp
from jax import lax
from jax.experimental import pallas as pl
from jax.experimental.pallas import tpu as pltpu
from jax.experimental.pallas import tpu_sc as plsc

F32 = jnp.float32
BF16 = jnp.bfloat16
HIGHEST = lax.Precision.HIGHEST

GRID_W = 64
NORM_EPS = 1e-6
ROPE_THETA = 10000.0
NEG_INF = -1e30
N_MODS = 6

MLA_HEADS = 4
MLA_Q_RANK = 384
MLA_KV_RANK = 256
MLA_NOPE = 128
MLA_ROPE = 64
MLA_V = 128

SWA_HEADS = 8
SWA_KV_HEADS = 2
SWA_GROUP = SWA_HEADS // SWA_KV_HEADS
SWA_HEAD_DIM = 64
WINDOW = 128

RWKV_HEAD = 64
DECAY_LORA = 64
ICLR_LORA = 64
GATE_LORA = 128
GN_EPS = 64e-5

N_EXPERTS = 64
TOP_K = 6
N_GROUPS = 8
TOPK_GROUPS = 4
GROUP_SIZE = N_EXPERTS // N_GROUPS
ROUTED_SCALE = 2.5
GATE_W = 128

V7X_LANES = 128
V7X_MXU_DIM = 256
V7X_VMEM_BYTES = 64 * 1024 * 1024
V7X_SC_CORES = 2
V7X_SC_SUBCORES = 16
V7X_SC_WORKERS = V7X_SC_CORES * V7X_SC_SUBCORES

TOKEN_TILE = 256
MLA_Q_TILE = 256
MLA_HEADS_PER_STEP = 2
SWA_Q_TILE = 256
SWA_BAND = SWA_Q_TILE + 2 * WINDOW
WKV_CHUNK = 64
WKV_PAIR = 2 * RWKV_HEAD
WKV_CHUNKS_PER_STEP = 4
MOE_ROW_TILE = 512
SAMPLE_GROUPS = 2
SC_MAX_CHUNK = 64

LOG2E = 1.4426950408889634
MIB = 1024 * 1024
VMEM_RESERVE_BYTES = 4 * MIB


def _vmem_limit(mib):
    return min(mib * MIB, V7X_VMEM_BYTES - VMEM_RESERVE_BYTES)


def _dot(a, b):
    return jnp.dot(a.astype(BF16), b.astype(BF16), preferred_element_type=F32)


def _dot_nt(a, b):
    return lax.dot_general(a.astype(BF16), b.astype(BF16), (((1,), (1,)), ((), ())),
                           preferred_element_type=F32)


def _dot_tn(a, b):
    return lax.dot_general(a.astype(BF16), b.astype(BF16), (((0,), (0,)), ((), ())),
                           preferred_element_type=F32)


def _sigmoid(x):
    return 1.0 / (1.0 + jnp.exp(-x))


def _silu(x):
    return x * _sigmoid(x)


def _rms(x, g):
    return x * lax.rsqrt(jnp.mean(x * x, axis=-1, keepdims=True) + NORM_EPS) * g


def _norm_mod(x, g, shift, scale):
    return _rms(x, g) * (1.0 + scale) + shift


def _split_dot(x, w):
    hi = x.astype(BF16)
    lo = (x - hi.astype(F32)).astype(BF16)
    return (jnp.dot(hi, w, preferred_element_type=F32) + jnp.dot(lo, w, preferred_element_type=F32))


def _head_sum(x, bd):
    w = bd.shape[0]
    parts = [_split_dot(x[:, c * w:(c + 1) * w], bd) for c in range(x.shape[1] // w)]
    return jnp.concatenate(parts, axis=1)


def _after(dep, kernel, in_specs, args, n_lead=0):
    if dep is None:
        return kernel, list(in_specs), list(args)
    n_in = n_lead + len(in_specs)

    def ordered(*refs):
        return kernel(*refs[:n_in], *refs[n_in + 1:])

    return ordered, list(in_specs) + [pl.BlockSpec(memory_space=pl.ANY)], list(args) + [dep]


def _ada_kernel(c_ref, w_ref, b_ref, o_ref):
    s = _silu(c_ref[...])
    o_ref[...] = jnp.dot(s, w_ref[0], precision=HIGHEST, preferred_element_type=F32) + b_ref[0]


def _ada_mods(cc, w, b, layer):
    rows, d = cc.shape
    depth, _, n = w.shape
    return pl.pallas_call(
        _ada_kernel,
        out_shape=jax.ShapeDtypeStruct((rows, n), F32),
        grid=(n // d,),
        in_specs=[pl.BlockSpec((rows, d), lambda i: (0, 0)),
                  pl.BlockSpec((1, d, d), lambda i: (layer, 0, i)),
                  pl.BlockSpec((1, 1, d), lambda i: (layer, 0, i))],
        out_specs=pl.BlockSpec((rows, d), lambda i: (0, i)),
        compiler_params=pltpu.CompilerParams(dimension_semantics=("parallel",),
                                             vmem_limit_bytes=_vmem_limit(32)),
        name="ada_mods",
    )(cc, w, b.reshape(depth, 1, n))


def _rope128(x, cos, sin, first_half):
    rot = jnp.where(first_half, -pltpu.roll(x, V7X_LANES - 16, axis=1), pltpu.roll(x, 16, axis=1))
    return x * cos + rot * sin


_C_CQ = 0
_C_CKV = _C_CQ + MLA_Q_RANK
_C_QS = _C_CKV + MLA_KV_RANK
_C_KS = _C_QS + SWA_HEADS * SWA_HEAD_DIM
_C_VS = _C_KS + SWA_KV_HEADS * V7X_MXU_DIM
_C_KR = _C_VS + SWA_KV_HEADS * V7X_MXU_DIM
_C_END = _C_KR + V7X_LANES
_SWA_W = SWA_KV_HEADS * V7X_MXU_DIM
_MLA_QK_W = MLA_HEADS * V7X_MXU_DIM


def _stream_specs(stream, nct, tl):
    ctx_arr, lat_arr, b0, lat_off = stream
    d = ctx_arr.shape[2]
    return [pl.BlockSpec((1, tl, d), lambda i, j: (i + b0, jnp.minimum(j, nct - 1), 0)),
            pl.BlockSpec((1, tl, d), lambda i, j: (i + b0, jnp.maximum(j - nct, 0) + lat_off, 0))]


def _stream_tile(c_ref, x_ref, nct):
    rows = c_ref.shape[1]
    take_ctx = lax.broadcasted_iota(jnp.int32, (rows, 1), 0) < jnp.where(pl.program_id(1) < nct, rows, 0)
    return jnp.where(take_ctx, c_ref[0], x_ref[0])


def _attn_proj_kernel(c_ref, x_ref, mods_ref, g_ref, win_ref, qn_ref, kvn_ref, wuq_ref, wuk_ref, wuvt_ref, cos_ref,
                      sin_ref, q_ref, k_ref, vt_ref, qs_ref, ks_ref, vs_ref, *, nct):
    m = mods_ref[0, 0]
    n = _norm_mod(_stream_tile(c_ref, x_ref, nct), g_ref[...], m[0:1], m[1:2])
    u = _dot(n, win_ref[...])
    cos = cos_ref[...]
    sin = sin_ref[...]
    lane = lax.broadcasted_iota(jnp.int32, (1, V7X_LANES), 1)
    first_half = (lane % 32) < 16

    def rope(x):
        return _rope128(x, cos, sin, first_half)

    scale_a = (MLA_NOPE + MLA_ROPE) ** -0.5 * LOG2E
    scale_b = SWA_HEAD_DIM ** -0.5 * LOG2E
    q = _dot(_rms(u[:, _C_CQ:_C_CKV], qn_ref[...]), wuq_ref[...])
    ckv = _rms(u[:, _C_CKV:_C_QS], kvn_ref[...])
    kn = _dot(ckv, wuk_ref[...])
    vt_ref[0] = _dot_nt(wuvt_ref[...], ckv).astype(BF16)
    kr = rope(u[:, _C_KR:_C_END]).astype(BF16)
    for h in range(MLA_HEADS):
        o = h * V7X_MXU_DIM
        q_ref[0, :, o:o + V7X_LANES] = (q[:, o:o + V7X_LANES] * scale_a).astype(BF16)
        q_ref[0, :, o + V7X_LANES:o + V7X_MXU_DIM] = (rope(q[:, o + V7X_LANES:o + V7X_MXU_DIM]) * scale_a).astype(BF16)
        k_ref[0, :, o:o + V7X_LANES] = kn[:, h * MLA_NOPE:(h + 1) * MLA_NOPE].astype(BF16)
        k_ref[0, :, o + V7X_LANES:o + V7X_MXU_DIM] = kr
    for c in range((_C_KS - _C_QS) // V7X_LANES):
        o = c * V7X_LANES
        qs_ref[0, :, o:o + V7X_LANES] = (rope(u[:, _C_QS + o:_C_QS + o + V7X_LANES]) * scale_b).astype(BF16)
    for c in range(_SWA_W // V7X_LANES):
        o = c * V7X_LANES
        ks_ref[0, :, o:o + V7X_LANES] = rope(u[:, _C_KS + o:_C_KS + o + V7X_LANES]).astype(BF16)
    vs_ref[0] = u[:, _C_VS:_C_KR].astype(BF16)


def _attn_proj(stream, b, l, mods, g, win, qn, kvn, wuq, wuk, wuvt, cos, sin, nct, dep=None):
    d = stream[0].shape[2]
    tl = TOKEN_TILE
    tok = lambda w: pl.BlockSpec((1, tl, w), lambda i, j: (i, j, 0))
    full = lambda a: pl.BlockSpec(a.shape, lambda i, j: (0,) * a.ndim)
    sds = jax.ShapeDtypeStruct
    dv = MLA_HEADS * MLA_V
    kern, in_specs, args = _after(
        dep, functools.partial(_attn_proj_kernel, nct=nct),
        _stream_specs(stream, nct, tl) + [
            pl.BlockSpec((1, 1, N_MODS, d), lambda i, j: (i, jnp.where(j < nct, 0, 1), 0, 0)),
            full(g), full(win), full(qn), full(kvn), full(wuq), full(wuk), full(wuvt),
            pl.BlockSpec((tl, V7X_LANES), lambda i, j: (j, 0)),
            pl.BlockSpec((tl, V7X_LANES), lambda i, j: (j, 0))],
        [stream[0], stream[1], mods, g, win, qn, kvn, wuq, wuk, wuvt, cos, sin])
    return pl.pallas_call(
        kern,
        out_shape=[sds((b, l, _MLA_QK_W), BF16), sds((b, l, _MLA_QK_W), BF16), sds((b, dv, l), BF16),
                   sds((b, l, SWA_HEADS * SWA_HEAD_DIM), BF16), sds((b, l, _SWA_W), BF16), sds((b, l, _SWA_W), BF16)],
        grid=(b, l // tl),
        in_specs=in_specs,
        out_specs=[tok(_MLA_QK_W), tok(_MLA_QK_W), pl.BlockSpec((1, dv, tl), lambda i, j: (i, 0, j)),
                   tok(SWA_HEADS * SWA_HEAD_DIM), tok(_SWA_W), tok(_SWA_W)],
        compiler_params=pltpu.CompilerParams(dimension_semantics=("parallel", "parallel"),
                                             vmem_limit_bytes=_vmem_limit(48)),
        name="attn_proj",
    )(*args)


def _mla_kernel(q_ref, k_ref, vt_ref, o_ref, *, nct_q, lc):
    hw = V7X_MXU_DIM

    def attend(nk):
        st = [_dot_nt(k_ref[0, 0:nk, hh * hw:(hh + 1) * hw], q_ref[0, :, hh * hw:(hh + 1) * hw])
              for hh in range(MLA_HEADS_PER_STEP)]
        for hh, s in enumerate(st):
            p = jnp.exp2(s - jnp.max(s, axis=0, keepdims=True))
            den = jnp.sum(p, axis=0, keepdims=True)
            ot = _dot(vt_ref[0, hh * MLA_V:(hh + 1) * MLA_V, 0:nk], p) / den
            o_ref[0, :, hh * MLA_V:(hh + 1) * MLA_V] = ot.T.astype(o_ref.dtype)

    @pl.when(pl.program_id(2) < nct_q)
    def _():
        attend(lc)

    @pl.when(pl.program_id(2) >= nct_q)
    def _():
        attend(k_ref.shape[1])


def _mla_attention(q, k, vt, lc, q_tile0, dep=None):
    b, l, _ = q.shape
    tq = MLA_Q_TILE
    hps = MLA_HEADS_PER_STEP
    kern, in_specs, args = _after(
        dep, functools.partial(_mla_kernel, nct_q=lc // tq - q_tile0, lc=lc),
        [pl.BlockSpec((1, tq, hps * V7X_MXU_DIM), lambda i, h, j: (i, j + q_tile0, h)),
         pl.BlockSpec((1, l, hps * V7X_MXU_DIM), lambda i, h, j: (i, 0, h)),
         pl.BlockSpec((1, hps * MLA_V, l), lambda i, h, j: (i, h, 0))],
        [q, k, vt])
    return pl.pallas_call(
        kern,
        out_shape=jax.ShapeDtypeStruct((b, l, MLA_HEADS * MLA_V), BF16),
        grid=(b, MLA_HEADS // hps, l // tq - q_tile0),
        in_specs=in_specs,
        out_specs=pl.BlockSpec((1, tq, hps * MLA_V), lambda i, h, j: (i, j + q_tile0, h)),
        compiler_params=pltpu.CompilerParams(dimension_semantics=("parallel", "parallel", "parallel"),
                                             vmem_limit_bytes=_vmem_limit(48)),
        name="mla_attention",
    )(*args)


def _swa_kernel(sink_ref, q_ref, k_ref, v_ref, o_ref, *, lc, q_tile0):
    tq = SWA_Q_TILE
    l = k_ref.shape[1]
    r0 = (pl.program_id(1) + q_tile0) * tq
    start = pl.multiple_of(jnp.clip(r0 - WINDOW, lc, l - SWA_BAND), WINDOW)
    rows = SWA_GROUP * tq
    row = lax.broadcasted_iota(jnp.int32, (rows, 1), 0)
    qpos = jnp.where(r0 >= lc, r0, -l) + row % tq
    kpos = start + lax.broadcasted_iota(jnp.int32, (1, SWA_BAND), 1)
    valid = jnp.abs(qpos - kpos) <= WINDOW
    lane = lax.broadcasted_iota(jnp.int32, (1, V7X_MXU_DIM), 1)
    head = [(lane // SWA_HEAD_DIM) == hh for hh in range(SWA_GROUP)]
    groups = range(SWA_KV_HEADS)
    sls = [slice(g * V7X_MXU_DIM, (g + 1) * V7X_MXU_DIM) for g in groups]
    qstack = []
    for sl in sls:
        qg = q_ref[0, :, sl]
        zero = jnp.zeros_like(qg)
        qstack.append(jnp.concatenate([jnp.where(head[hh], qg, zero) for hh in range(SWA_GROUP)], axis=0))
    sc = [_dot_nt(qstack[g], k_ref[0, 0:lc, sls[g]]) for g in groups]
    sb = [_dot_nt(qstack[g], k_ref[0, pl.ds(start, SWA_BAND), sls[g]]) for g in groups]
    for g in groups:
        sl = sls[g]
        sbm = jnp.where(valid, sb[g], NEG_INF)
        sk = jnp.zeros((rows, 1), F32)
        for hh in range(SWA_GROUP):
            sk = jnp.where(row // tq == hh, sink_ref[g * SWA_GROUP + hh] * LOG2E, sk)
        mx = jnp.maximum(jnp.maximum(jnp.max(sc[g], axis=-1, keepdims=True), jnp.max(sbm, axis=-1, keepdims=True)), sk)
        pc = jnp.exp2(sc[g] - mx)
        pb = jnp.exp2(sbm - mx)
        den = jnp.sum(pc, axis=-1, keepdims=True) + jnp.sum(pb, axis=-1, keepdims=True) + jnp.exp2(sk - mx)
        ostack = (_dot(pc, v_ref[0, 0:lc, sl]) + _dot(pb, v_ref[0, pl.ds(start, SWA_BAND), sl])) / den
        o = jnp.zeros((tq, V7X_MXU_DIM), F32)
        for hh in range(SWA_GROUP):
            o = o + jnp.where(head[hh], ostack[hh * tq:(hh + 1) * tq], 0.0)
        o_ref[0, :, sl] = o.astype(o_ref.dtype)


def _swa_attention(sinks, q, k, v, lc, q_tile0, dep=None):
    b, l, _ = q.shape
    tq = SWA_Q_TILE
    kern, in_specs, args = _after(
        dep, functools.partial(_swa_kernel, lc=lc, q_tile0=q_tile0),
        [pl.BlockSpec(memory_space=pltpu.SMEM),
         pl.BlockSpec((1, tq, SWA_HEADS * SWA_HEAD_DIM), lambda i, j: (i, j + q_tile0, 0)),
         pl.BlockSpec((1, l, _SWA_W), lambda i, j: (i, 0, 0)),
         pl.BlockSpec((1, l, _SWA_W), lambda i, j: (i, 0, 0))],
        [sinks, q, k, v])
    return pl.pallas_call(
        kern,
        out_shape=jax.ShapeDtypeStruct((b, l, SWA_HEADS * SWA_HEAD_DIM), BF16),
        grid=(b, l // tq - q_tile0),
        in_specs=in_specs,
        out_specs=pl.BlockSpec((1, tq, SWA_HEADS * SWA_HEAD_DIM), lambda i, j: (i, j + q_tile0, 0)),
        compiler_params=pltpu.CompilerParams(dimension_semantics=("parallel", "parallel"),
                                             vmem_limit_bytes=_vmem_limit(48)),
        name="swa_attention",
    )(*args)


def _pack_bf16_pair(x):
    w = x.shape[1] // 2
    lo = pltpu.bitcast(x[:, :w].astype(BF16).astype(F32), jnp.int32)
    hi = pltpu.bitcast(x[:, w:].astype(BF16).astype(F32), jnp.int32)
    return lax.shift_right_logical(lo, jnp.int32(16)) | (hi & jnp.int32(-65536))


def _unpack_bf16_pair(p):
    return pltpu.bitcast(p << 16, F32), pltpu.bitcast(p & jnp.int32(-65536), F32)


def _route(n2, wrt, bias, run_ref):
    n_hi = n2.astype(BF16)
    n_lo = (n2 - n_hi.astype(F32)).astype(BF16)
    w_hi = wrt.astype(BF16)
    w_lo = (wrt - w_hi.astype(F32)).astype(BF16)
    logits = _dot_nt(w_hi, n_hi) + (_dot_nt(w_hi, n_lo) + _dot_nt(w_lo, n_hi))
    rows = logits.shape[1]
    scores = _sigmoid(logits[0:N_EXPERTS])

    def select(sc2):
        cols = sc2.shape[1]
        shape3 = (N_GROUPS, GROUP_SIZE, cols)
        choice = sc2.reshape(shape3) + bias
        ji = lax.broadcasted_iota(jnp.int32, shape3, 1).astype(F32)
        m1 = jnp.max(choice, axis=1, keepdims=True)
        first = jnp.min(jnp.where(choice == m1, ji, float(GROUP_SIZE)), axis=1, keepdims=True)
        m2 = jnp.max(jnp.where(ji == first, -jnp.inf, choice), axis=1, keepdims=True)
        gs = m1 + m2
        gidx = lax.broadcasted_iota(jnp.int32, gs.shape, 0).astype(F32)
        gsel = jnp.zeros_like(gs)
        for _ in range(TOPK_GROUPS):
            mx = jnp.max(gs, axis=0, keepdims=True)
            pick = gidx == jnp.min(jnp.where(gs == mx, gidx, float(N_GROUPS)), axis=0, keepdims=True)
            gsel = jnp.where(pick, 1.0, gsel)
            gs = jnp.where(pick, -jnp.inf, gs)
        cand = jnp.where(gsel > 0.0, choice, -jnp.inf).reshape(N_EXPERTS, cols)
        eidx = lax.broadcasted_iota(jnp.int32, (N_EXPERTS, cols), 0).astype(F32)
        out = []
        for _ in range(TOP_K):
            mx = jnp.max(cand, axis=0, keepdims=True)
            pick = eidx == jnp.min(jnp.where(cand == mx, eidx, float(N_EXPERTS)), axis=0, keepdims=True)
            out.append(jnp.where(pick, 1.0, 0.0))
            cand = jnp.where(pick, -jnp.inf, cand)
        return out

    blocks = [select(scores[:, o:o + V7X_LANES]) for o in range(0, rows, V7X_LANES)]
    picks = [jnp.concatenate([blk[k] for blk in blocks], axis=1) > 0.0 for k in range(TOP_K)]
    ei = lax.broadcasted_iota(jnp.int32, (N_EXPERTS, rows), 0).astype(F32)
    esel = jnp.zeros((N_EXPERTS, rows), F32)
    for pick in picks:
        esel = jnp.where(pick, 1.0, esel)
    before = jnp.where(lax.broadcasted_iota(jnp.int32, (rows, rows), 0) < lax.broadcasted_iota(jnp.int32, (rows, rows), 1),
                       1.0, 0.0).astype(BF16)
    slot = jnp.dot(esel.astype(BF16), before, preferred_element_type=F32) + run_ref[...]
    run_ref[...] += jnp.sum(esel, axis=1, keepdims=True)
    sc = [jnp.sum(jnp.where(pick, scores, 0.0), axis=0, keepdims=True) for pick in picks]
    tot = sc[0]
    for x in sc[1:]:
        tot = tot + x
    k8 = lax.broadcasted_iota(jnp.int32, (8, rows), 0)
    kw = lax.broadcasted_iota(jnp.int32, (GATE_W, rows), 0)
    eid = jnp.zeros((8, rows), jnp.int32)
    rank = jnp.zeros((8, rows), jnp.int32)
    wk = jnp.zeros((GATE_W, rows), F32)
    for k, pick in enumerate(picks):
        e_k = jnp.sum(jnp.where(pick, ei, 0.0), axis=0, keepdims=True).astype(jnp.int32)
        r_k = jnp.sum(jnp.where(pick, slot, 0.0), axis=0, keepdims=True).astype(jnp.int32)
        eid = jnp.where(k8 == k, e_k, eid)
        rank = jnp.where(k8 == k, r_k, rank)
        wk = jnp.where(kw == k, sc[k] * (ROUTED_SCALE / tot), wk)
    return eid, rank, wk.T


def _mixer_tail(o, h, m, gffn_ref, wrt_ref, bias_ref, hn_ref, n2_ref, eid_ref, rank_ref, w_ref, cnt_ref, run_ref):
    @pl.when((pl.program_id(0) == 0) & (pl.program_id(1) == 0))
    def _():
        run_ref[...] = jnp.zeros_like(run_ref)

    hn = h + m[2:3] * o
    hn_ref[0] = hn
    n2 = _norm_mod(hn, gffn_ref[...], m[3:4], m[4:5])
    n2_ref[0] = _pack_bf16_pair(n2)
    eid, rank, wcols = _route(n2, wrt_ref[...], bias_ref[...], run_ref)
    eid_ref[0] = eid
    rank_ref[0] = rank
    w_ref[0] = wcols
    cnt_ref[...] = run_ref[...]


def _attn_out_kernel(a_ref, b_ref, c_ref, x_ref, mods_ref, wo_ref, gffn_ref, wrt_ref, bias_ref,
                     hn_ref, n2_ref, eid_ref, rank_ref, w_ref, cnt_ref, run_ref, *, nct):
    wa = MLA_HEADS * MLA_V
    o = _dot(a_ref[0], wo_ref[0:wa, :]) + _dot(b_ref[0], wo_ref[wa:, :])
    _mixer_tail(o, _stream_tile(c_ref, x_ref, nct), mods_ref[0, 0], gffn_ref, wrt_ref, bias_ref, hn_ref, n2_ref,
                eid_ref, rank_ref, w_ref, cnt_ref, run_ref)


def _tail_outs(b, l, d):
    tl = TOKEN_TILE
    nt = l // tl
    sds = jax.ShapeDtypeStruct
    tok = lambda w: pl.BlockSpec((1, tl, w), lambda i, j: (i, j, 0))
    blk = pl.BlockSpec((1, 8, tl), lambda i, j: (i * nt + j, 0, 0))
    shapes = [sds((b, l, d), F32), sds((b, l, d // 2), jnp.int32), sds((b * nt, 8, tl), jnp.int32),
              sds((b * nt, 8, tl), jnp.int32), sds((b, l, GATE_W), F32), sds((N_EXPERTS, 1), F32)]
    specs = [tok(d), tok(d // 2), blk, blk, tok(GATE_W), pl.BlockSpec((N_EXPERTS, 1), lambda i, j: (0, 0))]
    return shapes, specs


def _attn_out(a, bm, stream, mods, wo, gffn, wrt, bias, nct, dep=None):
    b, l, _ = a.shape
    d = stream[0].shape[2]
    tl = TOKEN_TILE
    tok = lambda w: pl.BlockSpec((1, tl, w), lambda i, j: (i, j, 0))
    full = lambda x: pl.BlockSpec(x.shape, lambda i, j: (0,) * x.ndim)
    shapes, specs = _tail_outs(b, l, d)
    kern, in_specs, args = _after(
        dep, functools.partial(_attn_out_kernel, nct=nct),
        [tok(a.shape[2]), tok(bm.shape[2])] + _stream_specs(stream, nct, tl) + [
            pl.BlockSpec((1, 1, N_MODS, d), lambda i, j: (i, jnp.where(j < nct, 0, 1), 0, 0)),
            full(wo), full(gffn), full(wrt), full(bias)],
        [a, bm, stream[0], stream[1], mods, wo, gffn, wrt, bias])
    return pl.pallas_call(
        kern,
        out_shape=shapes,
        grid=(b, l // tl),
        in_specs=in_specs,
        out_specs=specs,
        scratch_shapes=[pltpu.VMEM((N_EXPERTS, 1), F32)],
        compiler_params=pltpu.CompilerParams(dimension_semantics=("arbitrary", "arbitrary"),
                                             vmem_limit_bytes=_vmem_limit(40)),
        name="attn_out",
    )(*args)


def _moe_dest_kernel(off_ref, eid_ref, rank_ref, dest_ref):
    eid = eid_ref[...]
    dest = rank_ref[...]
    for e in range(N_EXPERTS):
        dest = dest + jnp.where(eid == e, off_ref[e], 0)
    dest_ref[...] = dest


def _moe_dest(off, eid, rank):
    return pl.pallas_call(
        _moe_dest_kernel,
        out_shape=jax.ShapeDtypeStruct(eid.shape, jnp.int32),
        in_specs=[pl.BlockSpec(memory_space=pltpu.SMEM),
                  pl.BlockSpec(eid.shape, lambda: (0, 0, 0)), pl.BlockSpec(eid.shape, lambda: (0, 0, 0))],
        out_specs=pl.BlockSpec(eid.shape, lambda: (0, 0, 0)),
        name="moe_dest",
    )(off, eid, rank)


def _sc_mesh():
    return plsc.VectorSubcoreMesh(core_axis_name="c", subcore_axis_name="s",
                                  num_cores=V7X_SC_CORES, num_subcores=V7X_SC_SUBCORES)


def _sc_chunk(rows_per_worker):
    return max(c for c in range(8, SC_MAX_CHUNK + 1, 8) if rows_per_worker % c == 0)


def _sc_dispatch(xp, dest, p_rows):
    t, w = xp.shape
    tpw = t // V7X_SC_WORKERS
    ch = _sc_chunk(tpw)

    @functools.partial(
        pl.kernel, mesh=_sc_mesh(), out_type=jax.ShapeDtypeStruct((p_rows, w), xp.dtype),
        scratch_types=[pltpu.VMEM((ch, w), xp.dtype)] + [pltpu.VMEM((ch,), jnp.int32)] * TOP_K
        + [pltpu.SemaphoreType.DMA, pltpu.SemaphoreType.DMA],
        name="moe_dispatch")
    def run(x_hbm, dest_hbm, out_hbm, rows_v, *rest):
        idx, (sem_i, sem_o) = rest[:TOP_K], rest[TOP_K:]
        base = (lax.axis_index("s") * V7X_SC_CORES + lax.axis_index("c")) * tpw

        @pl.loop(0, tpw // ch)
        def _(i):
            t0 = base + i * ch
            loads = [pltpu.async_copy(dest_hbm.at[k, pl.ds(t0, ch)], idx[k], sem_i) for k in range(TOP_K)]
            pltpu.sync_copy(x_hbm.at[pl.ds(t0, ch)], rows_v)
            for c in loads:
                c.wait()
            puts = [pltpu.async_copy(rows_v, out_hbm.at[idx[k]], sem_o) for k in range(TOP_K)]
            for c in puts:
                c.wait()

    return run(xp, dest)


def _sc_gather(ys, dest, t):
    w = ys.shape[1]
    tpw = t // V7X_SC_WORKERS
    ch = _sc_chunk(tpw)

    @functools.partial(
        pl.kernel, mesh=_sc_mesh(), out_type=jax.ShapeDtypeStruct((TOP_K, t, w), ys.dtype),
        scratch_types=[pltpu.VMEM((ch, w), ys.dtype)] * 2 + [pltpu.VMEM((ch,), jnp.int32)] * TOP_K
        + [pltpu.SemaphoreType.DMA] * 5,
        name="moe_gather")
    def run(y_hbm, dest_hbm, out_hbm, rows_a, rows_b, *rest):
        idx, (sem_i, sem_ga, sem_gb, sem_wa, sem_wb) = rest[:TOP_K], rest[TOP_K:]
        rows, sem_g, sem_w = (rows_a, rows_b), (sem_ga, sem_gb), (sem_wa, sem_wb)
        base = (lax.axis_index("s") * V7X_SC_CORES + lax.axis_index("c")) * tpw

        @pl.loop(0, tpw // ch)
        def _(i):
            t0 = base + i * ch
            loads = [pltpu.async_copy(dest_hbm.at[k, pl.ds(t0, ch)], idx[k], sem_i) for k in range(TOP_K)]
            for c in loads:
                c.wait()
            gets, puts = [None] * TOP_K, [None] * TOP_K
            gets[0] = pltpu.async_copy(y_hbm.at[idx[0]], rows[0], sem_g[0])
            for k in range(TOP_K):
                if k + 1 < TOP_K:
                    if k >= 1:
                        puts[k - 1].wait()
                    gets[k + 1] = pltpu.async_copy(y_hbm.at[idx[k + 1]], rows[(k + 1) % 2], sem_g[(k + 1) % 2])
                gets[k].wait()
                puts[k] = pltpu.async_copy(rows[k % 2], out_hbm.at[k, pl.ds(t0, ch)], sem_w[k % 2])
            puts[TOP_K - 2].wait()
            puts[TOP_K - 1].wait()

    return run(ys, dest)


def _cache_mlp_weights(wg, wu, wd, wgu_ref, wdb_ref):
    f = wg.shape[1]
    wgu_ref[:, 0:f] = wg.astype(BF16)
    wgu_ref[:, f:] = wu.astype(BF16)
    wdb_ref[...] = wd.astype(BF16)


def _gated_mlp(xp, wgu_ref, wdb_ref):
    lo, hi = _unpack_bf16_pair(xp)
    x = jnp.concatenate([lo.astype(BF16), hi.astype(BF16)], axis=1)
    gu = jnp.dot(x, wgu_ref[...], preferred_element_type=F32)
    f = gu.shape[1] // 2
    return _dot(_silu(gu[:, :f]) * gu[:, f:], wdb_ref[...])


def _moe_expert_kernel(te_ref, tb_ref, nv_ref, x_ref, wga_ref, wua_ref, wda_ref, wgb_ref, wub_ref, wdb_ref, y_ref,
                       gu_a, dn_a, gu_b, dn_b, ids_ref):
    i = pl.program_id(0)
    tm = MOE_ROW_TILE
    nv = nv_ref[0]
    first = 2 * jnp.minimum(i, (nv - 1) // 2)
    ea = te_ref[first]
    eb = te_ref[first + 1]
    two = 2 * i + 1 < nv

    @pl.when(i == 0)
    def _():
        ids_ref[0] = -1
        ids_ref[1] = -1

    @pl.when(ids_ref[0] != ea)
    def _():
        _cache_mlp_weights(wga_ref[0, 0], wua_ref[0, 0], wda_ref[0, 0], gu_a, dn_a)
        ids_ref[0] = ea

    @pl.when(two & (eb != ea) & (ids_ref[1] != eb))
    def _():
        _cache_mlp_weights(wgb_ref[0, 0], wub_ref[0, 0], wdb_ref[0, 0], gu_b, dn_b)
        ids_ref[1] = eb

    @pl.when(two & (eb == ea))
    def _():
        y_ref[...] = _pack_bf16_pair(_gated_mlp(x_ref[...], gu_a, dn_a))

    @pl.when((2 * i < nv) & jnp.logical_not(two & (eb == ea)))
    def _():
        y_ref[0:tm, :] = _pack_bf16_pair(_gated_mlp(x_ref[0:tm, :], gu_a, dn_a))

    @pl.when(two & (eb != ea))
    def _():
        y_ref[tm:, :] = _pack_bf16_pair(_gated_mlp(x_ref[tm:, :], gu_b, dn_b))


def _moe_experts(tile_expert, n_valid, xs, wg, wu, wd, layer, dep=None):
    p_rows, w = xs.shape
    tm = MOE_ROW_TILE
    _, _, d, f = wg.shape
    npair = p_rows // (2 * tm)
    pairs = tile_expert.reshape(npair, 2)
    tile_b = jnp.maximum(lax.cummax(jnp.where(pairs[:, 1] != pairs[:, 0], pairs[:, 1], -1)), 0)
    step = lambda i, nv: jnp.minimum(i, (nv[0] - 1) // 2)
    spec_a = lambda shp: pl.BlockSpec((1, 1) + shp, lambda i, te, tb, nv: (layer, te[2 * step(i, nv)], 0, 0))
    spec_b = lambda shp: pl.BlockSpec((1, 1) + shp, lambda i, te, tb, nv: (layer, tb[step(i, nv)], 0, 0))
    rows = pl.BlockSpec((2 * tm, w), lambda i, te, tb, nv: (step(i, nv), 0))
    kern, in_specs, args = _after(
        dep, _moe_expert_kernel,
        [rows, spec_a((d, f)), spec_a((d, f)), spec_a((f, d)), spec_b((d, f)), spec_b((d, f)), spec_b((f, d))],
        [tile_expert, tile_b, n_valid, xs, wg, wu, wd, wg, wu, wd], n_lead=3)
    return pl.pallas_call(
        kern,
        out_shape=jax.ShapeDtypeStruct((p_rows, w), xs.dtype),
        grid_spec=pltpu.PrefetchScalarGridSpec(
            num_scalar_prefetch=3, grid=(npair,),
            in_specs=in_specs,
            out_specs=rows,
            scratch_shapes=[pltpu.VMEM((d, 2 * f), BF16), pltpu.VMEM((f, d), BF16),
                            pltpu.VMEM((d, 2 * f), BF16), pltpu.VMEM((f, d), BF16), pltpu.SMEM((2,), jnp.int32)]),
        compiler_params=pltpu.CompilerParams(dimension_semantics=("arbitrary",),
                                             vmem_limit_bytes=_vmem_limit(48)),
        name="moe_experts",
    )(*args)


def _moe_combine_kernel(yg_ref, w_ref, xp_ref, sg_ref, su_ref, sd_ref, h_ref, mods_ref, gfin_ref, *rest, final_norm):
    o_ref, wgu_ref, wdb_ref = rest[-3:]

    @pl.when((pl.program_id(0) == 0) & (pl.program_id(1) == 0))
    def _():
        _cache_mlp_weights(sg_ref[0], su_ref[0], sd_ref[0], wgu_ref, wdb_ref)

    acc = _gated_mlp(xp_ref[0], wgu_ref, wdb_ref)
    half = acc.shape[1] // 2
    lo = acc[:, :half]
    hi = acc[:, half:]
    w = w_ref[0]
    for k in range(TOP_K):
        ylo, yhi = _unpack_bf16_pair(yg_ref[k, 0])
        wk = w[:, k:k + 1]
        lo = lo + wk * ylo
        hi = hi + wk * yhi
    y = h_ref[0] + mods_ref[0, 0, N_MODS - 1:N_MODS, :] * jnp.concatenate([lo, hi], axis=1)
    if final_norm:
        y = _rms(y, gfin_ref[...])
    o_ref[0] = y


def _moe_combine(yg, wcols, xp, sg, su, sd, h, mods, gfin, nct, layer, out_buf, out_b0, out_batch, latent_only,
                 final_norm, dep=None):
    b, l, d = h.shape
    tl = TOKEN_TILE
    tile0 = nct if latent_only else 0
    tok = lambda w: pl.BlockSpec((1, tl, w), lambda i, j: (i, j + tile0, 0))
    lay = lambda x: pl.BlockSpec((1,) + x.shape[1:], lambda i, j: (layer,) + (0,) * (x.ndim - 1))
    args = [yg, wcols, xp, sg, su, sd, h, mods, gfin]
    in_specs = [pl.BlockSpec((TOP_K, 1, tl, d // 2), lambda i, j: (0, i, j + tile0, 0)), tok(GATE_W), tok(d // 2),
                lay(sg), lay(su), lay(sd), tok(d),
                pl.BlockSpec((1, 1, N_MODS, d), lambda i, j: (i, jnp.where(j + tile0 < nct, 0, 1), 0, 0)),
                pl.BlockSpec(gfin.shape, lambda i, j: (0, 0))]
    _, in_specs, args = _after(dep, None, in_specs, args)
    aliases = {}
    if out_buf is not None:
        args.append(out_buf)
        in_specs.append(pl.BlockSpec(memory_space=pl.ANY))
        aliases = {len(args) - 1: 0}
    return pl.pallas_call(
        functools.partial(_moe_combine_kernel, final_norm=final_norm),
        out_shape=jax.ShapeDtypeStruct((out_batch, l - tile0 * tl, d), F32),
        grid=(b, l // tl - tile0),
        in_specs=in_specs,
        out_specs=pl.BlockSpec((1, tl, d), lambda i, j: (i + out_b0, j, 0)),
        scratch_shapes=[pltpu.VMEM((d, 2 * sg.shape[2]), BF16), pltpu.VMEM((sg.shape[2], d), BF16)],
        input_output_aliases=aliases,
        compiler_params=pltpu.CompilerParams(dimension_semantics=("arbitrary", "arbitrary"),
                                             vmem_limit_bytes=_vmem_limit(40)),
        name="moe_combine",
    )(*args)


def _moe_route_rows(n2p, eid, rank, counts, b, l):
    d2 = n2p.shape[2]
    t = b * l
    tm = MOE_ROW_TILE
    n_tiles = 2 * -(-(TOP_K * t + N_EXPERTS * (tm - 1)) // (2 * tm))
    tiles_e = (counts.reshape(N_EXPERTS).astype(jnp.int32) + (tm - 1)) // tm
    tile_end = jnp.cumsum(tiles_e)
    off = (tile_end - tiles_e) * tm
    n_valid = tile_end[-1:]
    tile_id = jnp.minimum(jnp.arange(n_tiles, dtype=jnp.int32), n_valid - 1)
    tile_expert = jnp.sum((tile_end[None, :] <= tile_id[:, None]).astype(jnp.int32), axis=1)
    dest = _moe_dest(off, eid, rank).transpose(1, 0, 2).reshape(8, t)
    xs = _sc_dispatch(n2p.reshape(t, d2), dest, n_tiles * tm)
    return xs, dest, tile_expert, n_valid


def _rwkv_proj_kernel(h_ref, hp_ref, hx_ref, mods_ref, g_ref, mu_ref, wr_ref, wk_ref, wv_ref, g1_ref, g2_ref,
                      w1_ref, w2_ref, a1_ref, a2_ref, w0_ref, a0_ref, kk_ref, ka_ref, rk_ref, bd_ref,
                      r_out, v_out, kk_out, g_out, km_out, b_out, lw_out, bonus_out, *, nct):
    j = pl.program_id(1)
    nt = pl.num_programs(1)
    m = mods_ref[0, 0]
    g = g_ref[...]
    n = _norm_mod(h_ref[0], g, m[0:1], m[1:2])
    tl, d = n.shape
    seg_first = (j == 0) | (j == nct)
    seg_last = (j == nct - 1) | (j == nt - 1)
    n_prev = _norm_mod(hp_ref[0], g, m[0:1], m[1:2])[7:8] * jnp.where(seg_first, 0.0, 1.0)
    n_next = _norm_mod(hx_ref[0], g, m[0:1], m[1:2])[0:1] * jnp.where(seg_last, 0.0, 1.0)
    row = lax.broadcasted_iota(jnp.int32, (tl, 1), 0)
    prev = jnp.where(row == 0, n_prev, pltpu.roll(n, 1, axis=0))
    nxt = jnp.where(row == tl - 1, n_next, pltpu.roll(n, tl - 1, axis=0))
    lane = lax.broadcasted_iota(jnp.int32, (1, d), 1)
    xx = jnp.where(lane < d // 2, prev, nxt) - n
    mu = mu_ref[...]
    bd = bd_ref[...]
    halves = [slice(0, tl // 2), slice(tl // 2, tl)]
    first = []
    for rs in halves:
        nh, xh = n[rs], xx[rs]
        xr, xw, xk, xv, xa, xg = [nh + xh * mu[i:i + 1] for i in range(6)]
        first.append((_dot(xr, wr_ref[...]), _dot(xk, wk_ref[...]), _dot(xv, wv_ref[...]),
                      _dot(xg, g1_ref[...]), _dot(xw, w1_ref[...]), _dot(xa, a1_ref[...])))
    second = []
    for r, k, v, gq, tq, ta in first:
        tw = jnp.tanh(tq)
        kk = k * kk_ref[...]
        second.append((_dot(_sigmoid(gq), g2_ref[...]), [_dot(tw, w2_ref[dr]) for dr in range(2)],
                       [_dot(ta, a2_ref[dr]) for dr in range(2)], kk, _head_sum(kk * kk, bd)))
    for rs, (r, k, v, _, _, _), (gate, zw, za, kk, kk_sq) in zip(halves, first, second):
        kk = kk / jnp.maximum(jnp.sqrt(kk_sq), 1e-12)
        g_out[0, rs, :] = gate.astype(g_out.dtype)
        r_out[0, rs, :] = r.astype(r_out.dtype)
        v_out[0, rs, :] = v.astype(v_out.dtype)
        kk_out[0, rs, :] = kk.astype(kk_out.dtype)
        bonus = jnp.zeros_like(v)
        for dr in range(2):
            lw_out[dr, 0, rs, :] = -jnp.exp(-0.5) * _sigmoid(w0_ref[dr:dr + 1, :] + zw[dr])
            a = _sigmoid(a0_ref[dr:dr + 1, :] + za[dr])
            km = k * (1.0 + (a - 1.0) * ka_ref[...])
            km_out[dr, 0, rs, :] = km.astype(km_out.dtype)
            b_out[dr, 0, rs, :] = (kk * a).astype(b_out.dtype)
            bonus = bonus + _head_sum(r * km * rk_ref[...], bd) * v
        bonus_out[0, rs, :] = bonus.astype(bonus_out.dtype)


def _rwkv_proj(h, mods, g, mu, wr, wk, wv, g1, g2, w1, w2, a1, a2, w0, a0, kk, ka, rk, bd, nct, dep=None):
    b, l, d = h.shape
    tl = TOKEN_TILE
    nb8 = l // 8
    tok = pl.BlockSpec((1, tl, d), lambda i, j: (i, j, 0))
    tok2 = pl.BlockSpec((2, 1, tl, d), lambda i, j: (0, i, j, 0))
    full = lambda x: pl.BlockSpec(x.shape, lambda i, j: (0,) * x.ndim)
    sds = jax.ShapeDtypeStruct
    kern, in_specs, args = _after(
        dep, functools.partial(_rwkv_proj_kernel, nct=nct),
        [tok,
         pl.BlockSpec((1, 8, d), lambda i, j: (i, jnp.maximum(j * (tl // 8) - 1, 0), 0)),
         pl.BlockSpec((1, 8, d), lambda i, j: (i, jnp.minimum((j + 1) * (tl // 8), nb8 - 1), 0)),
         pl.BlockSpec((1, 1, N_MODS, d), lambda i, j: (i, jnp.where(j < nct, 0, 1), 0, 0)),
         full(g), full(mu), full(wr), full(wk), full(wv), full(g1), full(g2), full(w1), full(w2),
         full(a1), full(a2), full(w0), full(a0), full(kk), full(ka), full(rk), full(bd)],
        [h, h, h, mods, g, mu, wr, wk, wv, g1, g2, w1, w2, a1, a2, w0, a0, kk, ka, rk, bd])
    return pl.pallas_call(
        kern,
        out_shape=[sds((b, l, d), BF16), sds((b, l, d), BF16), sds((b, l, d), BF16), sds((b, l, d), BF16),
                   sds((2, b, l, d), BF16), sds((2, b, l, d), BF16), sds((2, b, l, d), F32), sds((b, l, d), BF16)],
        grid=(b, l // tl),
        in_specs=in_specs,
        out_specs=[tok, tok, tok, tok, tok2, tok2, tok2, tok],
        compiler_params=pltpu.CompilerParams(dimension_semantics=("parallel", "parallel"),
                                             vmem_limit_bytes=_vmem_limit(56)),
        name="rwkv_proj",
    )(*args)


def _wkv_kernel(r_ref, v_ref, kk_ref, km_ref, b_ref, lw_ref, y_ref, st_ref):
    c = WKV_CHUNK
    w = WKV_PAIR
    rev = pl.program_id(0)
    sign = 1 - 2 * rev

    @pl.when(pl.program_id(2) == 0)
    def _():
        st_ref[...] = jnp.zeros_like(st_ref)

    ti = lax.broadcasted_iota(jnp.int32, (c, c), 0)
    si = lax.broadcasted_iota(jnp.int32, (c, c), 1)
    tri = jnp.where((si - ti) * sign <= 0, 1.0, 0.0).astype(F32)
    nsub = WKV_CHUNKS_PER_STEP
    subs = [pl.ds(pl.multiple_of(jnp.where(rev == 0, s, nsub - 1 - s) * c, c), c) for s in range(nsub)]
    rt, kt, kh, bh, v32, e_mid = [], [], [], [], [], []
    for rows in subs:
        lw = lw_ref[0, 0, rows, :]
        l_incl = jnp.dot(tri, lw, precision=HIGHEST, preferred_element_type=F32)
        mid = 0.5 * jnp.sum(lw, axis=0, keepdims=True)
        e_neg = jnp.exp(mid - l_incl)
        e_mid.append(jnp.exp(mid))
        rt.append(r_ref[0, rows, :].astype(F32) * jnp.exp(l_incl - mid))
        kt.append(kk_ref[0, rows, :].astype(F32) * jnp.exp(l_incl - lw - mid))
        kh.append(km_ref[0, 0, rows, :].astype(F32) * e_neg)
        bh.append(b_ref[0, 0, rows, :].astype(F32) * e_neg)
        v32.append(v_ref[0, rows, :].astype(F32))

    ri = lax.broadcasted_iota(jnp.int32, (w, w), 0)
    ci = lax.broadcasted_iota(jnp.int32, (w, w), 1)
    same = (ri // c) == (ci // c)
    eye = jnp.where(ri == ci, 1.0, 0.0).astype(F32)
    tl_ = lax.broadcasted_iota(jnp.int32, (c, w), 0)
    jl_ = lax.broadcasted_iota(jnp.int32, (c, w), 1) % c
    strict = (jl_ - tl_) * sign < 0
    incl = (jl_ - tl_) * sign <= 0
    eye2 = jnp.where(jl_ == tl_, 1.0, 0.0).astype(F32)
    lane = lax.broadcasted_iota(jnp.int32, (1, w), 1)
    h0 = lane < RWKV_HEAD

    def rows2(x):
        return jnp.concatenate([jnp.where(h0, x, 0.0), jnp.where(h0, 0.0, x)], axis=0)

    npair = st_ref.shape[0]
    items = [(s, slice(p * w, (p + 1) * w)) for s in range(nsub) for p in range(npair)]
    n = range(len(items))
    em = [e_mid[s][:, sl] for s, sl in items]
    g = [_dot_nt(jnp.concatenate([kt[s][:, sl], rt[s][:, sl]], axis=0),
                 jnp.concatenate([rows2(kh[s][:, sl]), rows2(bh[s][:, sl])], axis=0)) for s, sl in items]
    a_kk = [jnp.where(strict, x[:c, :w], 0.0) for x in g]
    a_rk = [jnp.where(incl, x[c:, :w], 0.0) for x in g]
    a_rb = [jnp.where(incl, x[c:, w:], 0.0) for x in g]
    vi = [v32[s][:, sl] for s, sl in items]
    v_rows = [rows2(x) for x in vi]
    av = [_dot(jnp.concatenate([a_kk[i], a_rk[i]], axis=0), v_rows[i]) for i in n]
    r_pre = [x[:c] for x in av]
    ark_v = [x[c:] for x in av]
    m = [jnp.where(strict, -x[:c, w:], 0.0) for x in g]
    tinv = [eye2 + x for x in m]
    m = [_dot(x, rows2(x)) for x in m]
    for _ in range(c.bit_length() - 3):
        both = [_dot(jnp.concatenate([tinv[i], m[i]], axis=0), rows2(m[i])) for i in n]
        tinv = [tinv[i] + both[i][:c] for i in n]
        m = [x[c:] for x in both]
    tinv = [tinv[i] + _dot(tinv[i], rows2(m[i])) for i in n]
    sol = [_dot(tinv[i], jnp.concatenate([rows2(r_pre[i]), rows2(kt[s][:, sl] * em[i])], axis=1))
           for i, (s, sl) in enumerate(items)]
    u_pre = [x[:, :w] for x in sol]
    kq = [x[:, w:] for x in sol]
    arb = [_dot(a_rb[i], jnp.concatenate([rows2(u_pre[i]), rows2(kq[i])], axis=1)) for i in n]
    y_pre = [ark_v[i] - arb[i][:, :w] for i in n]
    r_eff = [rt[s][:, sl] * em[i] - arb[i][:, w:] for i, (s, sl) in enumerate(items)]
    bbar = [bh[s][:, sl] * em[i] for i, (s, sl) in enumerate(items)]
    kbar = [kh[s][:, sl] * em[i] for i, (s, sl) in enumerate(items)]
    mmat = [eye * (em[i] * em[i]) - jnp.where(same, _dot_tn(kq[i], bbar[i]), 0.0) for i in n]
    s_pre = [jnp.where(same, _dot_tn(jnp.concatenate([vi[i], -u_pre[i]], axis=0),
                                     jnp.concatenate([kbar[i], bbar[i]], axis=0)), 0.0) for i in n]
    st = [st_ref[p] for p in range(npair)]
    for i, (s, sl) in enumerate(items):
        p = i % npair
        y_ref[0, 0, subs[s], sl] = (_dot_nt(r_eff[i], st[p]) + y_pre[i]).astype(y_ref.dtype)
        hi = st[p].astype(BF16)
        lo = (st[p] - hi.astype(F32)).astype(BF16)
        mb = mmat[i].astype(BF16)
        st[p] = (jnp.dot(hi, mb, preferred_element_type=F32) + jnp.dot(lo, mb, preferred_element_type=F32)
                 + s_pre[i])
    for p in range(npair):
        st_ref[p] = st[p]


def _wkv(r, v, kk, km, bv, lw, lc, dep=None):
    b, l, d = r.shape
    c = WKV_CHUNK * WKV_CHUNKS_PER_STEP
    ncc = lc // c
    nlc = (l - lc) // c

    def chunk(dr, i):
        return jnp.where(dr == 0, i, jnp.where(i < ncc, ncc - 1 - i, nlc + 2 * ncc - 1 - i))

    shared = pl.BlockSpec((1, c, d), lambda dr, bi, i: (bi, chunk(dr, i), 0))
    per_dir = pl.BlockSpec((1, 1, c, d), lambda dr, bi, i: (dr, bi, chunk(dr, i), 0))
    kern, in_specs, args = _after(dep, _wkv_kernel, [shared, shared, shared, per_dir, per_dir, per_dir],
                                  [r, v, kk, km, bv, lw])
    return pl.pallas_call(
        kern,
        out_shape=jax.ShapeDtypeStruct((2, b, l, d), BF16),
        grid=(2, b, l // c),
        in_specs=in_specs,
        out_specs=per_dir,
        scratch_shapes=[pltpu.VMEM((d // WKV_PAIR, WKV_PAIR, WKV_PAIR), F32)],
        compiler_params=pltpu.CompilerParams(dimension_semantics=("parallel", "parallel", "arbitrary"),
                                             vmem_limit_bytes=_vmem_limit(32)),
        name="wkv7_chunked",
    )(*args)


def _rwkv_out_kernel(y_ref, bonus_ref, g_ref, lnw_ref, lnb_ref, wo_ref, bd_ref, h_ref, mods_ref, gffn_ref,
                     wrt_ref, bias_ref, hn_ref, n2_ref, eid_ref, rank_ref, w_ref, cnt_ref, run_ref):
    y = y_ref[0, 0].astype(F32) + y_ref[1, 0].astype(F32)
    bd = bd_ref[...]
    mean = _head_sum(y, bd) * (1.0 / RWKV_HEAD)
    yc = y - mean
    var = _head_sum(yc * yc, bd) * (1.0 / RWKV_HEAD)
    yn = yc * lax.rsqrt(var + GN_EPS) * lnw_ref[...] + lnb_ref[...]
    out = (yn + bonus_ref[0].astype(F32)) * g_ref[0].astype(F32)
    _mixer_tail(_dot(out, wo_ref[...]), h_ref[0], mods_ref[0, 0], gffn_ref, wrt_ref, bias_ref, hn_ref, n2_ref,
                eid_ref, rank_ref, w_ref, cnt_ref, run_ref)


def _rwkv_out(y, bonus, g, lnw, lnb, wo, bd, h, mods, gffn, wrt, bias, nct, dep=None):
    b, l, d = h.shape
    tl = TOKEN_TILE
    tok = lambda w: pl.BlockSpec((1, tl, w), lambda i, j: (i, j, 0))
    full = lambda x: pl.BlockSpec(x.shape, lambda i, j: (0,) * x.ndim)
    shapes, specs = _tail_outs(b, l, d)
    kern, in_specs, args = _after(
        dep, _rwkv_out_kernel,
        [pl.BlockSpec((2, 1, tl, d), lambda i, j: (0, i, j, 0)), tok(d), tok(d),
         full(lnw), full(lnb), full(wo), full(bd), tok(d),
         pl.BlockSpec((1, 1, N_MODS, d), lambda i, j: (i, jnp.where(j < nct, 0, 1), 0, 0)),
         full(gffn), full(wrt), full(bias)],
        [y, bonus, g, lnw, lnb, wo, bd, h, mods, gffn, wrt, bias])
    return pl.pallas_call(
        kern,
        out_shape=shapes,
        grid=(b, l // tl),
        in_specs=in_specs,
        out_specs=specs,
        scratch_shapes=[pltpu.VMEM((N_EXPERTS, 1), F32)],
        compiler_params=pltpu.CompilerParams(dimension_semantics=("arbitrary", "arbitrary"),
                                             vmem_limit_bytes=_vmem_limit(40)),
        name="rwkv_out",
    )(*args)


def _rope_table(n_lat, n_ctx):
    dim = SWA_HEAD_DIM
    nf = dim // 4
    inv = ROPE_THETA ** (-jnp.arange(nf, dtype=F32) / nf)
    row = jnp.repeat(jnp.arange(n_lat // GRID_W, dtype=F32), GRID_W)
    col = jnp.tile(jnp.arange(GRID_W, dtype=F32), n_lat // GRID_W)
    ar = row[:, None] * inv
    ac = col[:, None] * inv
    ang = jnp.concatenate([ar, ar, ac, ac], axis=-1)
    cos = jnp.concatenate([jnp.ones((n_ctx, dim), F32), jnp.cos(ang)], axis=0)
    sin = jnp.concatenate([jnp.zeros((n_ctx, dim), F32), jnp.sin(ang)], axis=0)
    return jnp.tile(cos, (1, 2)), jnp.tile(sin, (1, 2))


def _layout_attn_weights(w_in, w_uq, w_ukv):
    d = w_in.shape[0]
    s0 = MLA_Q_RANK
    s1 = s0 + MLA_KV_RANK
    s2 = s1 + MLA_ROPE
    s3 = s2 + SWA_HEADS * SWA_HEAD_DIM
    s4 = s3 + SWA_KV_HEADS * SWA_HEAD_DIM
    rep = lambda w: jnp.concatenate(
        [jnp.tile(w[:, g * SWA_HEAD_DIM:(g + 1) * SWA_HEAD_DIM], (1, SWA_GROUP)) for g in range(SWA_KV_HEADS)], axis=1)
    win = jnp.concatenate([w_in[:, :s1], w_in[:, s2:s3], rep(w_in[:, s3:s4]), rep(w_in[:, s4:]),
                           w_in[:, s1:s2], jnp.zeros((d, V7X_LANES - MLA_ROPE), w_in.dtype)], axis=1)
    qh = MLA_NOPE + MLA_ROPE
    pad = jnp.zeros((w_uq.shape[0], V7X_MXU_DIM - qh), w_uq.dtype)
    wuq = jnp.concatenate([jnp.concatenate([w_uq[:, h * qh:(h + 1) * qh], pad], axis=1) for h in range(MLA_HEADS)], axis=1)
    kvh = MLA_NOPE + MLA_V
    wuk = jnp.concatenate([w_ukv[:, h * kvh:h * kvh + MLA_NOPE] for h in range(MLA_HEADS)], axis=1)
    wuvt = jnp.concatenate([w_ukv[:, h * kvh + MLA_NOPE:(h + 1) * kvh] for h in range(MLA_HEADS)], axis=1).T
    return win.astype(BF16), wuq.astype(BF16), wuk.astype(BF16), wuvt.astype(BF16)


def _lora_pair(w_down, w_up):
    rank = w_down.shape[2]
    down = jnp.concatenate([w_down[0], w_down[1]], axis=1)
    z = jnp.zeros((rank, w_up.shape[2]), w_up.dtype)
    up = jnp.stack([jnp.concatenate([w_up[0], z], axis=0), jnp.concatenate([z, w_up[1]], axis=0)], axis=0)
    return down.astype(BF16), up.astype(BF16)


def _head_block_diag():
    i = jnp.arange(V7X_MXU_DIM) // RWKV_HEAD
    return (i[:, None] == i[None, :]).astype(BF16)


def kernel(x, c, ctx, c_ctx, ada_w, ada_b, norm_mix, norm_ffn, norm_final, attn_w_in, attn_q_norm, attn_kv_norm, attn_w_uq, attn_w_ukv, attn_sinks, attn_w_o, rwkv_mu, rwkv_w_r, rwkv_w_k, rwkv_w_v, rwkv_w_o, rwkv_g1, rwkv_g2, rwkv_w0, rwkv_w1, rwkv_w2, rwkv_a0, rwkv_a1, rwkv_a2, rwkv_k_k, rwkv_k_a, rwkv_r_k, rwkv_ln_w, rwkv_ln_b, moe_router, moe_bias, moe_w_gate, moe_w_up, moe_w_down, moe_ws_gate, moe_ws_up, moe_ws_down):
    bsz, s, d = x.shape
    lc = ctx.shape[1]
    l = lc + s
    depth = ada_w.shape[0]
    nct = lc // TOKEN_TILE
    assert lc % TOKEN_TILE == 0 and s % TOKEN_TILE == 0 and s >= SWA_BAND and lc % SWA_Q_TILE == 0
    assert lc % (WKV_CHUNK * WKV_CHUNKS_PER_STEP) == 0
    assert d % V7X_MXU_DIM == 0 and WKV_CHUNK * 2 == V7X_LANES
    ngrp = SAMPLE_GROUPS
    bg = bsz // ngrp
    assert bsz % ngrp == 0 and (bg * l) % (8 * V7X_SC_WORKERS) == 0

    assert ngrp == 2
    cos, sin = _rope_table(s, lc)
    bd = _head_block_diag()
    rows = -(-(bsz + 1) // 8) * 8
    cc = jnp.concatenate([c, c_ctx[None, :], jnp.zeros((rows - bsz - 1, d), F32)], axis=0)
    row2 = lambda a: a.reshape(1, -1)
    moe_w = (moe_w_gate, moe_w_up, moe_w_down)
    moe_ws = (moe_ws_gate, moe_ws_up, moe_ws_down)

    shared = {}

    def layer_weights(li):
        if li not in shared:
            i = li // 2
            ada = _ada_mods(cc, ada_w, ada_b, li)
            w = dict(
                mods=jnp.stack([jnp.broadcast_to(ada[bsz].reshape(1, N_MODS, d), (bsz, N_MODS, d)),
                                ada[:bsz].reshape(bsz, N_MODS, d)], axis=1),
                wrt=jnp.concatenate([moe_router[li].T, jnp.zeros((GATE_W - N_EXPERTS, d), F32)], axis=0),
                bias=moe_bias[li].reshape(N_GROUPS, GROUP_SIZE, 1))
            if li % 2 == 0:
                w["win"], w["wuq"], w["wuk"], w["wuvt"] = _layout_attn_weights(attn_w_in[i], attn_w_uq[i], attn_w_ukv[i])
                w["wo"] = attn_w_o[i].astype(BF16)
            else:
                w["w1"], w["w2"] = _lora_pair(rwkv_w1[i], rwkv_w2[i])
                w["a1"], w["a2"] = _lora_pair(rwkv_a1[i], rwkv_a2[i])
                w["wr"], w["wk"], w["wv"], w["wo"] = [x[i].astype(BF16) for x in (rwkv_w_r, rwkv_w_k, rwkv_w_v, rwkv_w_o)]
                w["g1"], w["g2"] = rwkv_g1[i].astype(BF16), rwkv_g2[i].astype(BF16)
            shared[li] = w
        return shared[li]

    groups = [dict(stream=(ctx, x, g * bg, 0), b0=g * bg) for g in range(ngrp)]
    result = [None]

    def run_stage(st, li, name, dep):
        w = layer_weights(li)
        i = li // 2
        with_ctx = li < depth - 1
        mods = w["mods"][st["b0"]:st["b0"] + bg]
        if name == "proj" and li % 2 == 0:
            st["qkv"] = _attn_proj(st["stream"], bg, l, mods, row2(norm_mix[li]), w["win"], row2(attn_q_norm[i]),
                                   row2(attn_kv_norm[i]), w["wuq"], w["wuk"], w["wuvt"], cos, sin, nct, dep=dep)
            return st["qkv"][0]
        if name == "mid" and li % 2 == 0:
            q, k, vt, qs, ks, vs = st.pop("qkv")
            st["a"] = _mla_attention(q, k, vt, lc, 0 if with_ctx else lc // MLA_Q_TILE, dep=dep)
            st["bm"] = _swa_attention(attn_sinks[i], qs, ks, vs, lc, 0 if with_ctx else lc // SWA_Q_TILE, dep=st["a"])
            return st["bm"]
        if name == "proj":
            assert st["stream"][0] is st["stream"][1]
            st["feat"] = _rwkv_proj(st["stream"][0], mods, row2(norm_mix[li]), rwkv_mu[i], w["wr"], w["wk"], w["wv"],
                                    w["g1"], w["g2"], w["w1"], w["w2"], w["a1"], w["a2"], rwkv_w0[i], rwkv_a0[i],
                                    row2(rwkv_k_k[i]), row2(rwkv_k_a[i]), row2(rwkv_r_k[i]), bd, nct, dep=dep)
            return st["feat"][0]
        if name == "mid":
            r, v, kk, gt, km, bv, lw, bonus = st.pop("feat")
            st["y"] = _wkv(r, v, kk, km, bv, lw, lc, dep=dep)
            st["gate"], st["bonus"] = gt, bonus
            return st["y"]
        if name == "out":
            if li % 2 == 0:
                tail = _attn_out(st.pop("a"), st.pop("bm"), st["stream"], mods, w["wo"], row2(norm_ffn[li]),
                                 w["wrt"], w["bias"], nct, dep=dep)
            else:
                tail = _rwkv_out(st.pop("y"), st.pop("bonus"), st.pop("gate"), row2(rwkv_ln_w[i]), row2(rwkv_ln_b[i]),
                                 w["wo"], bd, st["stream"][0], mods, row2(norm_ffn[li]), w["wrt"], w["bias"], nct, dep=dep)
            st["h"], st["n2p"], eid, rank, st["wcols"], counts = tail
            st["xs"], st["dest"], st["tile_expert"], st["n_valid"] = _moe_route_rows(st["n2p"], eid, rank, counts, bg, l)
            return st["h"]
        if name == "experts":
            ys = _moe_experts(st.pop("tile_expert"), st.pop("n_valid"), st.pop("xs"), *moe_w, li, dep=dep)
            st["yg"] = _sc_gather(ys, st.pop("dest"), bg * l).reshape(TOP_K, bg, l, d // 2)
            return ys
        assert name == "combine"
        last = li == depth - 1
        h = _moe_combine(st.pop("yg"), st.pop("wcols"), st.pop("n2p"), *moe_ws, st.pop("h"), mods, row2(norm_final),
                         nct, li, result[0] if last else None, st["b0"] if last else 0, bsz if last else bg,
                         last, last, dep=dep)
        if last:
            result[0] = h
        else:
            st["stream"] = (h, h, 0, nct)
        return h

    order = [(0, 0, "proj"), (0, 0, "mid")]
    for li in range(depth):
        order += [(0, li, "out"), (1, li, "proj"), (0, li, "experts"), (1, li, "mid")]
        if li < depth - 1:
            order += [(0, li, "combine"), (1, li, "out"), (0, li + 1, "proj"), (1, li, "experts"),
                      (0, li + 1, "mid"), (1, li, "combine")]
        else:
            order += [(1, li, "out"), (0, li, "combine"), (1, li, "experts"), (1, li, "combine")]
    dep = None
    for g, li, name in order:
        dep = run_stage(groups[g], li, name, dep)
    return result[0]
```

```python
import functools

import jax
import jax.numpy as jnp
from jax import lax
from jax.experimental import pallas as pl
from jax.experimental.pallas import tpu as pltpu
from jax.experimental.pallas import tpu_sc as plsc

F32 = jnp.float32
BF16 = jnp.bfloat16
HIGHEST = lax.Precision.HIGHEST

GRID_W = 64
NORM_EPS = 1e-6
ROPE_THETA = 10000.0
NEG_INF = -1e30
N_MODS = 6

MLA_HEADS = 4
MLA_Q_RANK = 384
MLA_KV_RANK = 256
MLA_NOPE = 128
MLA_ROPE = 64
MLA_V = 128

SWA_HEADS = 8
SWA_KV_HEADS = 2
SWA_GROUP = SWA_HEADS // SWA_KV_HEADS
SWA_HEAD_DIM = 64
WINDOW = 128

RWKV_HEAD = 64
DECAY_LORA = 64
ICLR_LORA = 64
GATE_LORA = 128
GN_EPS = 64e-5

N_EXPERTS = 64
TOP_K = 6
N_GROUPS = 8
TOPK_GROUPS = 4
GROUP_SIZE = N_EXPERTS // N_GROUPS
ROUTED_SCALE = 2.5
GATE_W = 128

V7X_LANES = 128
V7X_MXU_DIM = 256
V7X_VMEM_BYTES = 64 * 1024 * 1024
V7X_SC_CORES = 2
V7X_SC_SUBCORES = 16
V7X_SC_WORKERS = V7X_SC_CORES * V7X_SC_SUBCORES

TOKEN_TILE = 256
MLA_Q_TILE = 256
MLA_HEADS_PER_STEP = 2
SWA_Q_TILE = 256
SWA_BAND = SWA_Q_TILE + 2 * WINDOW
WKV_CHUNK = 64
WKV_PAIR = 2 * RWKV_HEAD
WKV_CHUNKS_PER_STEP = 4
MOE_ROW_TILE = 512
MOE_ROW_SLOTS = 3
SAMPLE_GROUPS = 2
SC_MAX_CHUNK = 64

LOG2E = 1.4426950408889634
MIB = 1024 * 1024
VMEM_RESERVE_BYTES = 4 * MIB


def _vmem_limit(mib):
    return min(mib * MIB, V7X_VMEM_BYTES - VMEM_RESERVE_BYTES)


def _dot(a, b):
    return jnp.dot(a.astype(BF16), b.astype(BF16), preferred_element_type=F32)


def _dot_nt(a, b):
    return lax.dot_general(a.astype(BF16), b.astype(BF16), (((1,), (1,)), ((), ())),
                           preferred_element_type=F32)


def _dot_tn(a, b):
    return lax.dot_general(a.astype(BF16), b.astype(BF16), (((0,), (0,)), ((), ())),
                           preferred_element_type=F32)


def _sigmoid(x):
    return 1.0 / (1.0 + jnp.exp(-x))


def _silu(x):
    return x * _sigmoid(x)


def _rms(x, g):
    return x * lax.rsqrt(jnp.mean(x * x, axis=-1, keepdims=True) + NORM_EPS) * g


def _norm_mod(x, g, shift, scale):
    return _rms(x, g) * (1.0 + scale) + shift


def _split_dot(x, w):
    hi = x.astype(BF16)
    lo = (x - hi.astype(F32)).astype(BF16)
    return (jnp.dot(hi, w, preferred_element_type=F32) + jnp.dot(lo, w, preferred_element_type=F32))


def _head_sum(x, bd):
    w = bd.shape[0]
    parts = [_split_dot(x[:, c * w:(c + 1) * w], bd) for c in range(x.shape[1] // w)]
    return jnp.concatenate(parts, axis=1)


def _after(dep, kernel, in_specs, args, n_lead=0):
    if dep is None:
        return kernel, list(in_specs), list(args)
    n_in = n_lead + len(in_specs)

    def ordered(*refs):
        return kernel(*refs[:n_in], *refs[n_in + 1:])

    return ordered, list(in_specs) + [pl.BlockSpec(memory_space=pl.ANY)], list(args) + [dep]


def _ada_kernel(c_ref, w_ref, b_ref, o_ref):
    s = _silu(c_ref[...])
    o_ref[...] = jnp.dot(s, w_ref[0], precision=HIGHEST, preferred_element_type=F32) + b_ref[0]


def _ada_mods(cc, w, b, layer):
    rows, d = cc.shape
    depth, _, n = w.shape
    return pl.pallas_call(
        _ada_kernel,
        out_shape=jax.ShapeDtypeStruct((rows, n), F32),
        grid=(n // d,),
        in_specs=[pl.BlockSpec((rows, d), lambda i: (0, 0)),
                  pl.BlockSpec((1, d, d), lambda i: (layer, 0, i)),
                  pl.BlockSpec((1, 1, d), lambda i: (layer, 0, i))],
        out_specs=pl.BlockSpec((rows, d), lambda i: (0, i)),
        compiler_params=pltpu.CompilerParams(dimension_semantics=("parallel",),
                                             vmem_limit_bytes=_vmem_limit(32)),
        name="ada_mods",
    )(cc, w, b.reshape(depth, 1, n))


def _rope128(x, cos, sin, first_half):
    rot = jnp.where(first_half, -pltpu.roll(x, V7X_LANES - 16, axis=1), pltpu.roll(x, 16, axis=1))
    return x * cos + rot * sin


_C_CQ = 0
_C_CKV = _C_CQ + MLA_Q_RANK
_C_QS = _C_CKV + MLA_KV_RANK
_C_KS = _C_QS + SWA_HEADS * SWA_HEAD_DIM
_C_VS = _C_KS + SWA_KV_HEADS * V7X_MXU_DIM
_C_KR = _C_VS + SWA_KV_HEADS * V7X_MXU_DIM
_C_END = _C_KR + V7X_LANES
_SWA_W = SWA_KV_HEADS * V7X_MXU_DIM
_MLA_QK_W = MLA_HEADS * V7X_MXU_DIM


def _stream_specs(stream, nct, tl):
    ctx_arr, lat_arr, b0, lat_off = stream
    d = ctx_arr.shape[2]
    return [pl.BlockSpec((1, tl, d), lambda i, j: (i + b0, jnp.minimum(j, nct - 1), 0)),
            pl.BlockSpec((1, tl, d), lambda i, j: (i + b0, jnp.maximum(j - nct, 0) + lat_off, 0))]


def _stream_tile(c_ref, x_ref, nct):
    rows = c_ref.shape[1]
    take_ctx = lax.broadcasted_iota(jnp.int32, (rows, 1), 0) < jnp.where(pl.program_id(1) < nct, rows, 0)
    return jnp.where(take_ctx, c_ref[0], x_ref[0])


def _attn_proj_kernel(c_ref, x_ref, mods_ref, g_ref, win_ref, qn_ref, kvn_ref, wuq_ref, wuk_ref, wuvt_ref, cos_ref,
                      sin_ref, q_ref, k_ref, vt_ref, qs_ref, ks_ref, vs_ref, *, nct):
    m = mods_ref[0, 0]
    n = _norm_mod(_stream_tile(c_ref, x_ref, nct), g_ref[...], m[0:1], m[1:2])
    u = _dot(n, win_ref[...])
    cos = cos_ref[...]
    sin = sin_ref[...]
    lane = lax.broadcasted_iota(jnp.int32, (1, V7X_LANES), 1)
    first_half = (lane % 32) < 16

    def rope(x):
        return _rope128(x, cos, sin, first_half)

    scale_a = (MLA_NOPE + MLA_ROPE) ** -0.5 * LOG2E
    scale_b = SWA_HEAD_DIM ** -0.5 * LOG2E
    q = _dot(_rms(u[:, _C_CQ:_C_CKV], qn_ref[...]), wuq_ref[...])
    ckv = _rms(u[:, _C_CKV:_C_QS], kvn_ref[...])
    kn = _dot(ckv, wuk_ref[...])
    vt_ref[0] = _dot_nt(wuvt_ref[...], ckv).astype(BF16)
    kr = rope(u[:, _C_KR:_C_END]).astype(BF16)
    for h in range(MLA_HEADS):
        o = h * V7X_MXU_DIM
        q_ref[0, :, o:o + V7X_LANES] = (q[:, o:o + V7X_LANES] * scale_a).astype(BF16)
        q_ref[0, :, o + V7X_LANES:o + V7X_MXU_DIM] = (rope(q[:, o + V7X_LANES:o + V7X_MXU_DIM]) * scale_a).astype(BF16)
        k_ref[0, :, o:o + V7X_LANES] = kn[:, h * MLA_NOPE:(h + 1) * MLA_NOPE].astype(BF16)
        k_ref[0, :, o + V7X_LANES:o + V7X_MXU_DIM] = kr
    for c in range((_C_KS - _C_QS) // V7X_LANES):
        o = c * V7X_LANES
        qs_ref[0, :, o:o + V7X_LANES] = (rope(u[:, _C_QS + o:_C_QS + o + V7X_LANES]) * scale_b).astype(BF16)
    for c in range(_SWA_W // V7X_LANES):
        o = c * V7X_LANES
        ks_ref[0, :, o:o + V7X_LANES] = rope(u[:, _C_KS + o:_C_KS + o + V7X_LANES]).astype(BF16)
    vs_ref[0] = u[:, _C_VS:_C_KR].astype(BF16)


def _attn_proj(stream, b, l, mods, g, win, qn, kvn, wuq, wuk, wuvt, cos, sin, nct, dep=None):
    d = stream[0].shape[2]
    tl = TOKEN_TILE
    tok = lambda w: pl.BlockSpec((1, tl, w), lambda i, j: (i, j, 0))
    full = lambda a: pl.BlockSpec(a.shape, lambda i, j: (0,) * a.ndim)
    sds = jax.ShapeDtypeStruct
    dv = MLA_HEADS * MLA_V
    kern, in_specs, args = _after(
        dep, functools.partial(_attn_proj_kernel, nct=nct),
        _stream_specs(stream, nct, tl) + [
            pl.BlockSpec((1, 1, N_MODS, d), lambda i, j: (i, jnp.where(j < nct, 0, 1), 0, 0)),
            full(g), full(win), full(qn), full(kvn), full(wuq), full(wuk), full(wuvt),
            pl.BlockSpec((tl, V7X_LANES), lambda i, j: (j, 0)),
            pl.BlockSpec((tl, V7X_LANES), lambda i, j: (j, 0))],
        [stream[0], stream[1], mods, g, win, qn, kvn, wuq, wuk, wuvt, cos, sin])
    return pl.pallas_call(
        kern,
        out_shape=[sds((b, l, _MLA_QK_W), BF16), sds((b, l, _MLA_QK_W), BF16), sds((b, dv, l), BF16),
                   sds((b, l, SWA_HEADS * SWA_HEAD_DIM), BF16), sds((b, l, _SWA_W), BF16), sds((b, l, _SWA_W), BF16)],
        grid=(b, l // tl),
        in_specs=in_specs,
        out_specs=[tok(_MLA_QK_W), tok(_MLA_QK_W), pl.BlockSpec((1, dv, tl), lambda i, j: (i, 0, j)),
                   tok(SWA_HEADS * SWA_HEAD_DIM), tok(_SWA_W), tok(_SWA_W)],
        compiler_params=pltpu.CompilerParams(dimension_semantics=("parallel", "parallel"),
                                             vmem_limit_bytes=_vmem_limit(48)),
        name="attn_proj",
    )(*args)


def _mla_kernel(q_ref, k_ref, vt_ref, o_ref, *, nct_q, lc):
    hw = V7X_MXU_DIM

    def attend(nk):
        st = [_dot_nt(k_ref[0, 0:nk, hh * hw:(hh + 1) * hw], q_ref[0, :, hh * hw:(hh + 1) * hw])
              for hh in range(MLA_HEADS_PER_STEP)]
        for hh, s in enumerate(st):
            p = jnp.exp2(s - jnp.max(s, axis=0, keepdims=True))
            den = jnp.sum(p, axis=0, keepdims=True)
            ot = _dot(vt_ref[0, hh * MLA_V:(hh + 1) * MLA_V, 0:nk], p) / den
            o_ref[0, :, hh * MLA_V:(hh + 1) * MLA_V] = ot.T.astype(o_ref.dtype)

    @pl.when(pl.program_id(2) < nct_q)
    def _():
        attend(lc)

    @pl.when(pl.program_id(2) >= nct_q)
    def _():
        attend(k_ref.shape[1])


def _mla_attention(q, k, vt, lc, q_tile0, dep=None):
    b, l, _ = q.shape
    tq = MLA_Q_TILE
    hps = MLA_HEADS_PER_STEP
    kern, in_specs, args = _after(
        dep, functools.partial(_mla_kernel, nct_q=lc // tq - q_tile0, lc=lc),
        [pl.BlockSpec((1, tq, hps * V7X_MXU_DIM), lambda i, h, j: (i, j + q_tile0, h)),
         pl.BlockSpec((1, l, hps * V7X_MXU_DIM), lambda i, h, j: (i, 0, h)),
         pl.BlockSpec((1, hps * MLA_V, l), lambda i, h, j: (i, h, 0))],
        [q, k, vt])
    return pl.pallas_call(
        kern,
        out_shape=jax.ShapeDtypeStruct((b, l, MLA_HEADS * MLA_V), BF16),
        grid=(b, MLA_HEADS // hps, l // tq - q_tile0),
        in_specs=in_specs,
        out_specs=pl.BlockSpec((1, tq, hps * MLA_V), lambda i, h, j: (i, j + q_tile0, h)),
        compiler_params=pltpu.CompilerParams(dimension_semantics=("parallel", "parallel", "parallel"),
                                             vmem_limit_bytes=_vmem_limit(48)),
        name="mla_attention",
    )(*args)


def _swa_kernel(sink_ref, q_ref, k_ref, v_ref, o_ref, *, lc, q_tile0):
    tq = SWA_Q_TILE
    l = k_ref.shape[1]
    r0 = (pl.program_id(1) + q_tile0) * tq
    start = pl.multiple_of(jnp.clip(r0 - WINDOW, lc, l - SWA_BAND), WINDOW)
    rows = SWA_GROUP * tq
    row = lax.broadcasted_iota(jnp.int32, (rows, 1), 0)
    qpos = jnp.where(r0 >= lc, r0, -l) + row % tq
    kpos = start + lax.broadcasted_iota(jnp.int32, (1, SWA_BAND), 1)
    valid = jnp.abs(qpos - kpos) <= WINDOW
    lane = lax.broadcasted_iota(jnp.int32, (1, V7X_MXU_DIM), 1)
    head = [(lane // SWA_HEAD_DIM) == hh for hh in range(SWA_GROUP)]
    groups = range(SWA_KV_HEADS)
    sls = [slice(g * V7X_MXU_DIM, (g + 1) * V7X_MXU_DIM) for g in groups]
    qstack = []
    for sl in sls:
        qg = q_ref[0, :, sl]
        zero = jnp.zeros_like(qg)
        qstack.append(jnp.concatenate([jnp.where(head[hh], qg, zero) for hh in range(SWA_GROUP)], axis=0))
    sc = [_dot_nt(qstack[g], k_ref[0, 0:lc, sls[g]]) for g in groups]
    sb = [_dot_nt(qstack[g], k_ref[0, pl.ds(start, SWA_BAND), sls[g]]) for g in groups]
    for g in groups:
        sl = sls[g]
        sbm = jnp.where(valid, sb[g], NEG_INF)
        sk = jnp.zeros((rows, 1), F32)
        for hh in range(SWA_GROUP):
            sk = jnp.where(row // tq == hh, sink_ref[g * SWA_GROUP + hh] * LOG2E, sk)
        mx = jnp.maximum(jnp.maximum(jnp.max(sc[g], axis=-1, keepdims=True), jnp.max(sbm, axis=-1, keepdims=True)), sk)
        pc = jnp.exp2(sc[g] - mx)
        pb = jnp.exp2(sbm - mx)
        den = jnp.sum(pc, axis=-1, keepdims=True) + jnp.sum(pb, axis=-1, keepdims=True) + jnp.exp2(sk - mx)
        ostack = (_dot(pc, v_ref[0, 0:lc, sl]) + _dot(pb, v_ref[0, pl.ds(start, SWA_BAND), sl])) / den
        o = jnp.zeros((tq, V7X_MXU_DIM), F32)
        for hh in range(SWA_GROUP):
            o = o + jnp.where(head[hh], ostack[hh * tq:(hh + 1) * tq], 0.0)
        o_ref[0, :, sl] = o.astype(o_ref.dtype)


def _swa_attention(sinks, q, k, v, lc, q_tile0, dep=None):
    b, l, _ = q.shape
    tq = SWA_Q_TILE
    kern, in_specs, args = _after(
        dep, functools.partial(_swa_kernel, lc=lc, q_tile0=q_tile0),
        [pl.BlockSpec(memory_space=pltpu.SMEM),
         pl.BlockSpec((1, tq, SWA_HEADS * SWA_HEAD_DIM), lambda i, j: (i, j + q_tile0, 0)),
         pl.BlockSpec((1, l, _SWA_W), lambda i, j: (i, 0, 0)),
         pl.BlockSpec((1, l, _SWA_W), lambda i, j: (i, 0, 0))],
        [sinks, q, k, v])
    return pl.pallas_call(
        kern,
        out_shape=jax.ShapeDtypeStruct((b, l, SWA_HEADS * SWA_HEAD_DIM), BF16),
        grid=(b, l // tq - q_tile0),
        in_specs=in_specs,
        out_specs=pl.BlockSpec((1, tq, SWA_HEADS * SWA_HEAD_DIM), lambda i, j: (i, j + q_tile0, 0)),
        compiler_params=pltpu.CompilerParams(dimension_semantics=("parallel", "parallel"),
                                             vmem_limit_bytes=_vmem_limit(48)),
        name="swa_attention",
    )(*args)


def _pack_bf16_pair(x):
    w = x.shape[1] // 2
    lo = pltpu.bitcast(x[:, :w].astype(BF16).astype(F32), jnp.int32)
    hi = pltpu.bitcast(x[:, w:].astype(BF16).astype(F32), jnp.int32)
    return lax.shift_right_logical(lo, jnp.int32(16)) | (hi & jnp.int32(-65536))


def _unpack_bf16_pair(p):
    return pltpu.bitcast(p << 16, F32), pltpu.bitcast(p & jnp.int32(-65536), F32)


def _route(n2, wrt, bias, run_ref):
    n_hi = n2.astype(BF16)
    n_lo = (n2 - n_hi.astype(F32)).astype(BF16)
    w_hi = wrt.astype(BF16)
    w_lo = (wrt - w_hi.astype(F32)).astype(BF16)
    logits = _dot_nt(w_hi, n_hi) + (_dot_nt(w_hi, n_lo) + _dot_nt(w_lo, n_hi))
    rows = logits.shape[1]
    scores = _sigmoid(logits[0:N_EXPERTS])

    def select(sc2):
        cols = sc2.shape[1]
        shape3 = (N_GROUPS, GROUP_SIZE, cols)
        choice = sc2.reshape(shape3) + bias
        ji = lax.broadcasted_iota(jnp.int32, shape3, 1).astype(F32)
        m1 = jnp.max(choice, axis=1, keepdims=True)
        first = jnp.min(jnp.where(choice == m1, ji, float(GROUP_SIZE)), axis=1, keepdims=True)
        m2 = jnp.max(jnp.where(ji == first, -jnp.inf, choice), axis=1, keepdims=True)
        gs = m1 + m2
        gidx = lax.broadcasted_iota(jnp.int32, gs.shape, 0).astype(F32)
        gsel = jnp.zeros_like(gs)
        for _ in range(TOPK_GROUPS):
            mx = jnp.max(gs, axis=0, keepdims=True)
            pick = gidx == jnp.min(jnp.where(gs == mx, gidx, float(N_GROUPS)), axis=0, keepdims=True)
            gsel = jnp.where(pick, 1.0, gsel)
            gs = jnp.where(pick, -jnp.inf, gs)
        cand = jnp.where(gsel > 0.0, choice, -jnp.inf).reshape(N_EXPERTS, cols)
        eidx = lax.broadcasted_iota(jnp.int32, (N_EXPERTS, cols), 0).astype(F32)
        out = []
        for _ in range(TOP_K):
            mx = jnp.max(cand, axis=0, keepdims=True)
            pick = eidx == jnp.min(jnp.where(cand == mx, eidx, float(N_EXPERTS)), axis=0, keepdims=True)
            out.append(jnp.where(pick, 1.0, 0.0))
            cand = jnp.where(pick, -jnp.inf, cand)
        return out

    blocks = [select(scores[:, o:o + V7X_LANES]) for o in range(0, rows, V7X_LANES)]
    picks = [jnp.concatenate([blk[k] for blk in blocks], axis=1) > 0.0 for k in range(TOP_K)]
    ei = lax.broadcasted_iota(jnp.int32, (N_EXPERTS, rows), 0).astype(F32)
    esel = jnp.zeros((N_EXPERTS, rows), F32)
    for pick in picks:
        esel = jnp.where(pick, 1.0, esel)
    before = jnp.where(lax.broadcasted_iota(jnp.int32, (rows, rows), 0) < lax.broadcasted_iota(jnp.int32, (rows, rows), 1),
                       1.0, 0.0).astype(BF16)
    slot = jnp.dot(esel.astype(BF16), before, preferred_element_type=F32) + run_ref[...]
    run_ref[...] += jnp.sum(esel, axis=1, keepdims=True)
    sc = [jnp.sum(jnp.where(pick, scores, 0.0), axis=0, keepdims=True) for pick in picks]
    tot = sc[0]
    for x in sc[1:]:
        tot = tot + x
    k8 = lax.broadcasted_iota(jnp.int32, (8, rows), 0)
    kw = lax.broadcasted_iota(jnp.int32, (GATE_W, rows), 0)
    eid = jnp.zeros((8, rows), jnp.int32)
    rank = jnp.zeros((8, rows), jnp.int32)
    wk = jnp.zeros((GATE_W, rows), F32)
    for k, pick in enumerate(picks):
        e_k = jnp.sum(jnp.where(pick, ei, 0.0), axis=0, keepdims=True).astype(jnp.int32)
        r_k = jnp.sum(jnp.where(pick, slot, 0.0), axis=0, keepdims=True).astype(jnp.int32)
        eid = jnp.where(k8 == k, e_k, eid)
        rank = jnp.where(k8 == k, r_k, rank)
        wk = jnp.where(kw == k, sc[k] * (ROUTED_SCALE / tot), wk)
    return eid, rank, wk.T


def _mixer_tail(o, h, m, gffn_ref, wrt_ref, bias_ref, hn_ref, n2_ref, eid_ref, rank_ref, w_ref, cnt_ref, run_ref):
    @pl.when((pl.program_id(0) == 0) & (pl.program_id(1) == 0))
    def _():
        run_ref[...] = jnp.zeros_like(run_ref)

    hn = h + m[2:3] * o
    hn_ref[0] = hn
    n2 = _norm_mod(hn, gffn_ref[...], m[3:4], m[4:5])
    n2_ref[0] = _pack_bf16_pair(n2)
    eid, rank, wcols = _route(n2, wrt_ref[...], bias_ref[...], run_ref)
    eid_ref[0] = eid
    rank_ref[0] = rank
    w_ref[0] = wcols
    cnt_ref[...] = run_ref[...]


def _attn_out_kernel(a_ref, b_ref, c_ref, x_ref, mods_ref, wo_ref, gffn_ref, wrt_ref, bias_ref,
                     hn_ref, n2_ref, eid_ref, rank_ref, w_ref, cnt_ref, run_ref, *, nct):
    wa = MLA_HEADS * MLA_V
    o = _dot(a_ref[0], wo_ref[0:wa, :]) + _dot(b_ref[0], wo_ref[wa:, :])
    _mixer_tail(o, _stream_tile(c_ref, x_ref, nct), mods_ref[0, 0], gffn_ref, wrt_ref, bias_ref, hn_ref, n2_ref,
                eid_ref, rank_ref, w_ref, cnt_ref, run_ref)


def _tail_outs(b, l, d):
    tl = TOKEN_TILE
    nt = l // tl
    sds = jax.ShapeDtypeStruct
    tok = lambda w: pl.BlockSpec((1, tl, w), lambda i, j: (i, j, 0))
    blk = pl.BlockSpec((1, 8, tl), lambda i, j: (i * nt + j, 0, 0))
    shapes = [sds((b, l, d), F32), sds((b, l, d // 2), jnp.int32), sds((b * nt, 8, tl), jnp.int32),
              sds((b * nt, 8, tl), jnp.int32), sds((b, l, GATE_W), F32), sds((N_EXPERTS, 1), F32)]
    specs = [tok(d), tok(d // 2), blk, blk, tok(GATE_W), pl.BlockSpec((N_EXPERTS, 1), lambda i, j: (0, 0))]
    return shapes, specs


def _attn_out(a, bm, stream, mods, wo, gffn, wrt, bias, nct, dep=None):
    b, l, _ = a.shape
    d = stream[0].shape[2]
    tl = TOKEN_TILE
    tok = lambda w: pl.BlockSpec((1, tl, w), lambda i, j: (i, j, 0))
    full = lambda x: pl.BlockSpec(x.shape, lambda i, j: (0,) * x.ndim)
    shapes, specs = _tail_outs(b, l, d)
    kern, in_specs, args = _after(
        dep, functools.partial(_attn_out_kernel, nct=nct),
        [tok(a.shape[2]), tok(bm.shape[2])] + _stream_specs(stream, nct, tl) + [
            pl.BlockSpec((1, 1, N_MODS, d), lambda i, j: (i, jnp.where(j < nct, 0, 1), 0, 0)),
            full(wo), full(gffn), full(wrt), full(bias)],
        [a, bm, stream[0], stream[1], mods, wo, gffn, wrt, bias])
    return pl.pallas_call(
        kern,
        out_shape=shapes,
        grid=(b, l // tl),
        in_specs=in_specs,
        out_specs=specs,
        scratch_shapes=[pltpu.VMEM((N_EXPERTS, 1), F32)],
        compiler_params=pltpu.CompilerParams(dimension_semantics=("arbitrary", "arbitrary"),
                                             vmem_limit_bytes=_vmem_limit(40)),
        name="attn_out",
    )(*args)


def _moe_dest_kernel(off_ref, eid_ref, rank_ref, dest_ref):
    eid = eid_ref[...]
    dest = rank_ref[...]
    for e in range(N_EXPERTS):
        dest = dest + jnp.where(eid == e, off_ref[e], 0)
    dest_ref[...] = dest


def _moe_dest(off, eid, rank):
    return pl.pallas_call(
        _moe_dest_kernel,
        out_shape=jax.ShapeDtypeStruct(eid.shape, jnp.int32),
        in_specs=[pl.BlockSpec(memory_space=pltpu.SMEM),
                  pl.BlockSpec(eid.shape, lambda: (0, 0, 0)), pl.BlockSpec(eid.shape, lambda: (0, 0, 0))],
        out_specs=pl.BlockSpec(eid.shape, lambda: (0, 0, 0)),
        name="moe_dest",
    )(off, eid, rank)


def _sc_mesh():
    return plsc.VectorSubcoreMesh(core_axis_name="c", subcore_axis_name="s",
                                  num_cores=V7X_SC_CORES, num_subcores=V7X_SC_SUBCORES)


def _sc_chunk(rows_per_worker):
    return max(c for c in range(8, SC_MAX_CHUNK + 1, 8) if rows_per_worker % c == 0)


def _sc_dispatch(xp, dest, p_rows):
    t, w = xp.shape
    tpw = t // V7X_SC_WORKERS
    ch = _sc_chunk(tpw)

    @functools.partial(
        pl.kernel, mesh=_sc_mesh(), out_type=jax.ShapeDtypeStruct((p_rows, w), xp.dtype),
        scratch_types=[pltpu.VMEM((ch, w), xp.dtype)] + [pltpu.VMEM((ch,), jnp.int32)] * TOP_K
        + [pltpu.SemaphoreType.DMA, pltpu.SemaphoreType.DMA],
        name="moe_dispatch")
    def run(x_hbm, dest_hbm, out_hbm, rows_v, *rest):
        idx, (sem_i, sem_o) = rest[:TOP_K], rest[TOP_K:]
        base = (lax.axis_index("s") * V7X_SC_CORES + lax.axis_index("c")) * tpw

        @pl.loop(0, tpw // ch)
        def _(i):
            t0 = base + i * ch
            loads = [pltpu.async_copy(dest_hbm.at[k, pl.ds(t0, ch)], idx[k], sem_i) for k in range(TOP_K)]
            pltpu.sync_copy(x_hbm.at[pl.ds(t0, ch)], rows_v)
            for c in loads:
                c.wait()
            puts = [pltpu.async_copy(rows_v, out_hbm.at[idx[k]], sem_o) for k in range(TOP_K)]
            for c in puts:
                c.wait()

    return run(xp, dest)


def _sc_gather(ys, dest, t):
    w = ys.shape[1]
    tpw = t // V7X_SC_WORKERS
    ch = _sc_chunk(tpw)

    @functools.partial(
        pl.kernel, mesh=_sc_mesh(), out_type=jax.ShapeDtypeStruct((TOP_K, t, w), ys.dtype),
        scratch_types=[pltpu.VMEM((ch, w), ys.dtype)] * 2 + [pltpu.VMEM((ch,), jnp.int32)] * TOP_K
        + [pltpu.SemaphoreType.DMA] * 5,
        name="moe_gather")
    def run(y_hbm, dest_hbm, out_hbm, rows_a, rows_b, *rest):
        idx, (sem_i, sem_ga, sem_gb, sem_wa, sem_wb) = rest[:TOP_K], rest[TOP_K:]
        rows, sem_g, sem_w = (rows_a, rows_b), (sem_ga, sem_gb), (sem_wa, sem_wb)
        base = (lax.axis_index("s") * V7X_SC_CORES + lax.axis_index("c")) * tpw

        @pl.loop(0, tpw // ch)
        def _(i):
            t0 = base + i * ch
            loads = [pltpu.async_copy(dest_hbm.at[k, pl.ds(t0, ch)], idx[k], sem_i) for k in range(TOP_K)]
            for c in loads:
                c.wait()
            gets, puts = [None] * TOP_K, [None] * TOP_K
            gets[0] = pltpu.async_copy(y_hbm.at[idx[0]], rows[0], sem_g[0])
            for k in range(TOP_K):
                if k + 1 < TOP_K:
                    if k >= 1:
                        puts[k - 1].wait()
                    gets[k + 1] = pltpu.async_copy(y_hbm.at[idx[k + 1]], rows[(k + 1) % 2], sem_g[(k + 1) % 2])
                gets[k].wait()
                puts[k] = pltpu.async_copy(rows[k % 2], out_hbm.at[k, pl.ds(t0, ch)], sem_w[k % 2])
            puts[TOP_K - 2].wait()
            puts[TOP_K - 1].wait()

    return run(ys, dest)


def _cache_mlp_weights(wg, wu, wd, wgu_ref, wdb_ref):
    f = wg.shape[1]
    wgu_ref[:, 0:f] = wg.astype(BF16)
    wgu_ref[:, f:] = wu.astype(BF16)
    wdb_ref[...] = wd.astype(BF16)


def _gated_mlp(xp, wgu_ref, wdb_ref):
    lo, hi = _unpack_bf16_pair(xp)
    x = jnp.concatenate([lo.astype(BF16), hi.astype(BF16)], axis=1)
    gu = jnp.dot(x, wgu_ref[...], preferred_element_type=F32)
    f = gu.shape[1] // 2
    return _dot(_silu(gu[:, :f]) * gu[:, f:], wdb_ref[...])


def _moe_expert_kernel(te_ref, tb_ref, nv_ref, x_hbm, wga_ref, wua_ref, wda_ref, wgb_ref, wub_ref, wdb_ref, y_ref,
                       gu_a, dn_a, gu_b, dn_b, ids_ref, xbuf, sems):
    i = pl.program_id(0)
    tm = MOE_ROW_TILE
    nv = nv_ref[0]
    last = (nv - 1) // 2
    first = 2 * jnp.minimum(i, last)
    ea = te_ref[first]
    eb = te_ref[first + 1]
    two = 2 * i + 1 < nv

    def rows_copy(step):
        slot = step % MOE_ROW_SLOTS
        row0 = step * (2 * tm)
        rows = pl.ds(row0 if isinstance(step, int) else pl.multiple_of(row0, 2 * tm), 2 * tm)
        return pltpu.make_async_copy(x_hbm.at[rows], xbuf.at[slot], sems.at[slot])

    @pl.when(i == 0)
    def _():
        ids_ref[0] = -1
        ids_ref[1] = -1
        for ahead in range(MOE_ROW_SLOTS - 1):
            @pl.when(ahead <= last)
            def _():
                rows_copy(ahead).start()

    @pl.when(i + (MOE_ROW_SLOTS - 1) <= last)
    def _():
        rows_copy(i + (MOE_ROW_SLOTS - 1)).start()

    @pl.when(i <= last)
    def _():
        rows_copy(i).wait()

    x_ref = xbuf.at[i % MOE_ROW_SLOTS]

    @pl.when(ids_ref[0] != ea)
    def _():
        _cache_mlp_weights(wga_ref[0, 0], wua_ref[0, 0], wda_ref[0, 0], gu_a, dn_a)
        ids_ref[0] = ea

    @pl.when(two & (eb != ea) & (ids_ref[1] != eb))
    def _():
        _cache_mlp_weights(wgb_ref[0, 0], wub_ref[0, 0], wdb_ref[0, 0], gu_b, dn_b)
        ids_ref[1] = eb

    @pl.when(two & (eb == ea))
    def _():
        y_ref[...] = _pack_bf16_pair(_gated_mlp(x_ref[...], gu_a, dn_a))

    @pl.when((2 * i < nv) & jnp.logical_not(two & (eb == ea)))
    def _():
        y_ref[0:tm, :] = _pack_bf16_pair(_gated_mlp(x_ref[0:tm, :], gu_a, dn_a))

    @pl.when(two & (eb != ea))
    def _():
        y_ref[tm:, :] = _pack_bf16_pair(_gated_mlp(x_ref[tm:, :], gu_b, dn_b))


def _moe_experts(tile_expert, n_valid, xs, wg, wu, wd, layer, dep=None):
    p_rows, w = xs.shape
    tm = MOE_ROW_TILE
    _, _, d, f = wg.shape
    npair = p_rows // (2 * tm)
    pairs = tile_expert.reshape(npair, 2)
    tile_b = jnp.maximum(lax.cummax(jnp.where(pairs[:, 1] != pairs[:, 0], pairs[:, 1], -1)), 0)
    step = lambda i, nv: jnp.minimum(i, (nv[0] - 1) // 2)
    spec_a = lambda shp: pl.BlockSpec((1, 1) + shp, lambda i, te, tb, nv: (layer, te[2 * step(i, nv)], 0, 0))
    spec_b = lambda shp: pl.BlockSpec((1, 1) + shp, lambda i, te, tb, nv: (layer, tb[step(i, nv)], 0, 0))
    rows = pl.BlockSpec((2 * tm, w), lambda i, te, tb, nv: (step(i, nv), 0))
    kern, in_specs, args = _after(
        dep, _moe_expert_kernel,
        [pl.BlockSpec(memory_space=pl.ANY), spec_a((d, f)), spec_a((d, f)), spec_a((f, d)),
         spec_b((d, f)), spec_b((d, f)), spec_b((f, d))],
        [tile_expert, tile_b, n_valid, xs, wg, wu, wd, wg, wu, wd], n_lead=3)
    return pl.pallas_call(
        kern,
        out_shape=jax.ShapeDtypeStruct((p_rows, w), xs.dtype),
        grid_spec=pltpu.PrefetchScalarGridSpec(
            num_scalar_prefetch=3, grid=(npair,),
            in_specs=in_specs,
            out_specs=rows,
            scratch_shapes=[pltpu.VMEM((d, 2 * f), BF16), pltpu.VMEM((f, d), BF16),
                            pltpu.VMEM((d, 2 * f), BF16), pltpu.VMEM((f, d), BF16), pltpu.SMEM((2,), jnp.int32),
                            pltpu.VMEM((MOE_ROW_SLOTS, 2 * tm, w), xs.dtype),
                            pltpu.SemaphoreType.DMA((MOE_ROW_SLOTS,))]),
        compiler_params=pltpu.CompilerParams(dimension_semantics=("arbitrary",),
                                             vmem_limit_bytes=_vmem_limit(48)),
        name="moe_experts",
    )(*args)


def _moe_combine_kernel(yg_ref, w_ref, xp_ref, sg_ref, su_ref, sd_ref, h_ref, mods_ref, gfin_ref, *rest, final_norm):
    o_ref, wgu_ref, wdb_ref = rest[-3:]

    @pl.when((pl.program_id(0) == 0) & (pl.program_id(1) == 0))
    def _():
        _cache_mlp_weights(sg_ref[0], su_ref[0], sd_ref[0], wgu_ref, wdb_ref)

    acc = _gated_mlp(xp_ref[0], wgu_ref, wdb_ref)
    half = acc.shape[1] // 2
    lo = acc[:, :half]
    hi = acc[:, half:]
    w = w_ref[0]
    for k in range(TOP_K):
        ylo, yhi = _unpack_bf16_pair(yg_ref[k, 0])
        wk = w[:, k:k + 1]
        lo = lo + wk * ylo
        hi = hi + wk * yhi
    y = h_ref[0] + mods_ref[0, 0, N_MODS - 1:N_MODS, :] * jnp.concatenate([lo, hi], axis=1)
    if final_norm:
        y = _rms(y, gfin_ref[...])
    o_ref[0] = y


def _moe_combine(yg, wcols, xp, sg, su, sd, h, mods, gfin, nct, layer, out_buf, out_b0, out_batch, latent_only,
                 final_norm, dep=None):
    b, l, d = h.shape
    tl = TOKEN_TILE
    tile0 = nct if latent_only else 0
    tok = lambda w: pl.BlockSpec((1, tl, w), lambda i, j: (i, j + tile0, 0))
    lay = lambda x: pl.BlockSpec((1,) + x.shape[1:], lambda i, j: (layer,) + (0,) * (x.ndim - 1))
    args = [yg, wcols, xp, sg, su, sd, h, mods, gfin]
    in_specs = [pl.BlockSpec((TOP_K, 1, tl, d // 2), lambda i, j: (0, i, j + tile0, 0)), tok(GATE_W), tok(d // 2),
                lay(sg), lay(su), lay(sd), tok(d),
                pl.BlockSpec((1, 1, N_MODS, d), lambda i, j: (i, jnp.where(j + tile0 < nct, 0, 1), 0, 0)),
                pl.BlockSpec(gfin.shape, lambda i, j: (0, 0))]
    _, in_specs, args = _after(dep, None, in_specs, args)
    aliases = {}
    if out_buf is not None:
        args.append(out_buf)
        in_specs.append(pl.BlockSpec(memory_space=pl.ANY))
        aliases = {len(args) - 1: 0}
    return pl.pallas_call(
        functools.partial(_moe_combine_kernel, final_norm=final_norm),
        out_shape=jax.ShapeDtypeStruct((out_batch, l - tile0 * tl, d), F32),
        grid=(b, l // tl - tile0),
        in_specs=in_specs,
        out_specs=pl.BlockSpec((1, tl, d), lambda i, j: (i + out_b0, j, 0)),
        scratch_shapes=[pltpu.VMEM((d, 2 * sg.shape[2]), BF16), pltpu.VMEM((sg.shape[2], d), BF16)],
        input_output_aliases=aliases,
        compiler_params=pltpu.CompilerParams(dimension_semantics=("arbitrary", "arbitrary"),
                                             vmem_limit_bytes=_vmem_limit(40)),
        name="moe_combine",
    )(*args)


def _moe_route_rows(n2p, eid, rank, counts, b, l):
    d2 = n2p.shape[2]
    t = b * l
    tm = MOE_ROW_TILE
    n_tiles = 2 * -(-(TOP_K * t + N_EXPERTS * (tm - 1)) // (2 * tm))
    tiles_e = (counts.reshape(N_EXPERTS).astype(jnp.int32) + (tm - 1)) // tm
    tile_end = jnp.cumsum(tiles_e)
    off = (tile_end - tiles_e) * tm
    n_valid = tile_end[-1:]
    tile_id = jnp.minimum(jnp.arange(n_tiles, dtype=jnp.int32), n_valid - 1)
    tile_expert = jnp.sum((tile_end[None, :] <= tile_id[:, None]).astype(jnp.int32), axis=1)
    dest = _moe_dest(off, eid, rank).transpose(1, 0, 2).reshape(8, t)
    xs = _sc_dispatch(n2p.reshape(t, d2), dest, n_tiles * tm)
    return xs, dest, tile_expert, n_valid


def _rwkv_proj_kernel(h_ref, hp_ref, hx_ref, mods_ref, g_ref, mu_ref, wr_ref, wk_ref, wv_ref, g1_ref, g2_ref,
                      w1_ref, w2_ref, a1_ref, a2_ref, w0_ref, a0_ref, kk_ref, ka_ref, rk_ref, bd_ref,
                      r_out, v_out, kk_out, g_out, km_out, b_out, lw_out, bonus_out, *, nct):
    j = pl.program_id(1)
    nt = pl.num_programs(1)
    m = mods_ref[0, 0]
    g = g_ref[...]
    n = _norm_mod(h_ref[0], g, m[0:1], m[1:2])
    tl, d = n.shape
    seg_first = (j == 0) | (j == nct)
    seg_last = (j == nct - 1) | (j == nt - 1)
    n_prev = _norm_mod(hp_ref[0], g, m[0:1], m[1:2])[7:8] * jnp.where(seg_first, 0.0, 1.0)
    n_next = _norm_mod(hx_ref[0], g, m[0:1], m[1:2])[0:1] * jnp.where(seg_last, 0.0, 1.0)
    row = lax.broadcasted_iota(jnp.int32, (tl, 1), 0)
    prev = jnp.where(row == 0, n_prev, pltpu.roll(n, 1, axis=0))
    nxt = jnp.where(row == tl - 1, n_next, pltpu.roll(n, tl - 1, axis=0))
    lane = lax.broadcasted_iota(jnp.int32, (1, d), 1)
    xx = jnp.where(lane < d // 2, prev, nxt) - n
    mu = mu_ref[...]
    bd = bd_ref[...]
    halves = [slice(0, tl // 2), slice(tl // 2, tl)]
    first = []
    for rs in halves:
        nh, xh = n[rs], xx[rs]
        xr, xw, xk, xv, xa, xg = [nh + xh * mu[i:i + 1] for i in range(6)]
        first.append((_dot(xr, wr_ref[...]), _dot(xk, wk_ref[...]), _dot(xv, wv_ref[...]),
                      _dot(xg, g1_ref[...]), _dot(xw, w1_ref[...]), _dot(xa, a1_ref[...])))
    second = []
    for r, k, v, gq, tq, ta in first:
        tw = jnp.tanh(tq)
        kk = k * kk_ref[...]
        second.append((_dot(_sigmoid(gq), g2_ref[...]), [_dot(tw, w2_ref[dr]) for dr in range(2)],
                       [_dot(ta, a2_ref[dr]) for dr in range(2)], kk, _head_sum(kk * kk, bd)))
    for rs, (r, k, v, _, _, _), (gate, zw, za, kk, kk_sq) in zip(halves, first, second):
        kk = kk / jnp.maximum(jnp.sqrt(kk_sq), 1e-12)
        g_out[0, rs, :] = gate.astype(g_out.dtype)
        r_out[0, rs, :] = r.astype(r_out.dtype)
        v_out[0, rs, :] = v.astype(v_out.dtype)
        kk_out[0, rs, :] = kk.astype(kk_out.dtype)
        bonus = jnp.zeros_like(v)
        for dr in range(2):
            lw_out[dr, 0, rs, :] = -jnp.exp(-0.5) * _sigmoid(w0_ref[dr:dr + 1, :] + zw[dr])
            a = _sigmoid(a0_ref[dr:dr + 1, :] + za[dr])
            km = k * (1.0 + (a - 1.0) * ka_ref[...])
            km_out[dr, 0, rs, :] = km.astype(km_out.dtype)
            b_out[dr, 0, rs, :] = (kk * a).astype(b_out.dtype)
            bonus = bonus + _head_sum(r * km * rk_ref[...], bd) * v
        bonus_out[0, rs, :] = bonus.astype(bonus_out.dtype)


def _rwkv_proj(h, mods, g, mu, wr, wk, wv, g1, g2, w1, w2, a1, a2, w0, a0, kk, ka, rk, bd, nct, dep=None):
    b, l, d = h.shape
    tl = TOKEN_TILE
    nb8 = l // 8
    tok = pl.BlockSpec((1, tl, d), lambda i, j: (i, j, 0))
    tok2 = pl.BlockSpec((2, 1, tl, d), lambda i, j: (0, i, j, 0))
    full = lambda x: pl.BlockSpec(x.shape, lambda i, j: (0,) * x.ndim)
    sds = jax.ShapeDtypeStruct
    kern, in_specs, args = _after(
        dep, functools.partial(_rwkv_proj_kernel, nct=nct),
        [tok,
         pl.BlockSpec((1, 8, d), lambda i, j: (i, jnp.maximum(j * (tl // 8) - 1, 0), 0)),
         pl.BlockSpec((1, 8, d), lambda i, j: (i, jnp.minimum((j + 1) * (tl // 8), nb8 - 1), 0)),
         pl.BlockSpec((1, 1, N_MODS, d), lambda i, j: (i, jnp.where(j < nct, 0, 1), 0, 0)),
         full(g), full(mu), full(wr), full(wk), full(wv), full(g1), full(g2), full(w1), full(w2),
         full(a1), full(a2), full(w0), full(a0), full(kk), full(ka), full(rk), full(bd)],
        [h, h, h, mods, g, mu, wr, wk, wv, g1, g2, w1, w2, a1, a2, w0, a0, kk, ka, rk, bd])
    return pl.pallas_call(
        kern,
        out_shape=[sds((b, l, d), BF16), sds((b, l, d), BF16), sds((b, l, d), BF16), sds((b, l, d), BF16),
                   sds((2, b, l, d), BF16), sds((2, b, l, d), BF16), sds((2, b, l, d), F32), sds((b, l, d), BF16)],
        grid=(b, l // tl),
        in_specs=in_specs,
        out_specs=[tok, tok, tok, tok, tok2, tok2, tok2, tok],
        compiler_params=pltpu.CompilerParams(dimension_semantics=("parallel", "parallel"),
                                             vmem_limit_bytes=_vmem_limit(56)),
        name="rwkv_proj",
    )(*args)


def _wkv_kernel(r_ref, v_ref, kk_ref, km_ref, b_ref, lw_ref, y_ref, st_ref):
    c = WKV_CHUNK
    w = WKV_PAIR
    rev = pl.program_id(0)
    sign = 1 - 2 * rev

    @pl.when(pl.program_id(2) == 0)
    def _():
        st_ref[...] = jnp.zeros_like(st_ref)

    ti = lax.broadcasted_iota(jnp.int32, (c, c), 0)
    si = lax.broadcasted_iota(jnp.int32, (c, c), 1)
    tri = jnp.where((si - ti) * sign <= 0, 1.0, 0.0).astype(F32)
    nsub = WKV_CHUNKS_PER_STEP
    subs = [pl.ds(pl.multiple_of(jnp.where(rev == 0, s, nsub - 1 - s) * c, c), c) for s in range(nsub)]
    rt, kt, kh, bh, v32, e_mid = [], [], [], [], [], []
    for rows in subs:
        lw = lw_ref[0, 0, rows, :]
        l_incl = jnp.dot(tri, lw, precision=HIGHEST, preferred_element_type=F32)
        mid = 0.5 * jnp.sum(lw, axis=0, keepdims=True)
        e_neg = jnp.exp(mid - l_incl)
        e_mid.append(jnp.exp(mid))
        rt.append(r_ref[0, rows, :].astype(F32) * jnp.exp(l_incl - mid))
        kt.append(kk_ref[0, rows, :].astype(F32) * jnp.exp(l_incl - lw - mid))
        kh.append(km_ref[0, 0, rows, :].astype(F32) * e_neg)
        bh.append(b_ref[0, 0, rows, :].astype(F32) * e_neg)
        v32.append(v_ref[0, rows, :].astype(F32))

    ri = lax.broadcasted_iota(jnp.int32, (w, w), 0)
    ci = lax.broadcasted_iota(jnp.int32, (w, w), 1)
    same = (ri // c) == (ci // c)
    eye = jnp.where(ri == ci, 1.0, 0.0).astype(F32)
    tl_ = lax.broadcasted_iota(jnp.int32, (c, w), 0)
    jl_ = lax.broadcasted_iota(jnp.int32, (c, w), 1) % c
    strict = (jl_ - tl_) * sign < 0
    incl = (jl_ - tl_) * sign <= 0
    eye2 = jnp.where(jl_ == tl_, 1.0, 0.0).astype(F32)
    lane = lax.broadcasted_iota(jnp.int32, (1, w), 1)
    h0 = lane < RWKV_HEAD

    def rows2(x):
        return jnp.concatenate([jnp.where(h0, x, 0.0), jnp.where(h0, 0.0, x)], axis=0)

    npair = st_ref.shape[0]
    items = [(s, slice(p * w, (p + 1) * w)) for s in range(nsub) for p in range(npair)]
    n = range(len(items))
    em = [e_mid[s][:, sl] for s, sl in items]
    g = [_dot_nt(jnp.concatenate([kt[s][:, sl], rt[s][:, sl]], axis=0),
                 jnp.concatenate([rows2(kh[s][:, sl]), rows2(bh[s][:, sl])], axis=0)) for s, sl in items]
    a_kk = [jnp.where(strict, x[:c, :w], 0.0) for x in g]
    a_rk = [jnp.where(incl, x[c:, :w], 0.0) for x in g]
    a_rb = [jnp.where(incl, x[c:, w:], 0.0) for x in g]
    vi = [v32[s][:, sl] for s, sl in items]
    v_rows = [rows2(x) for x in vi]
    av = [_dot(jnp.concatenate([a_kk[i], a_rk[i]], axis=0), v_rows[i]) for i in n]
    r_pre = [x[:c] for x in av]
    ark_v = [x[c:] for x in av]
    m = [jnp.where(strict, -x[:c, w:], 0.0) for x in g]
    tinv = [eye2 + x for x in m]
    m = [_dot(x, rows2(x)) for x in m]
    for _ in range(c.bit_length() - 3):
        both = [_dot(jnp.concatenate([tinv[i], m[i]], axis=0), rows2(m[i])) for i in n]
        tinv = [tinv[i] + both[i][:c] for i in n]
        m = [x[c:] for x in both]
    tinv = [tinv[i] + _dot(tinv[i], rows2(m[i])) for i in n]
    sol = [_dot(tinv[i], jnp.concatenate([rows2(r_pre[i]), rows2(kt[s][:, sl] * em[i])], axis=1))
           for i, (s, sl) in enumerate(items)]
    u_pre = [x[:, :w] for x in sol]
    kq = [x[:, w:] for x in sol]
    arb = [_dot(a_rb[i], jnp.concatenate([rows2(u_pre[i]), rows2(kq[i])], axis=1)) for i in n]
    y_pre = [ark_v[i] - arb[i][:, :w] for i in n]
    r_eff = [rt[s][:, sl] * em[i] - arb[i][:, w:] for i, (s, sl) in enumerate(items)]
    bbar = [bh[s][:, sl] * em[i] for i, (s, sl) in enumerate(items)]
    kbar = [kh[s][:, sl] * em[i] for i, (s, sl) in enumerate(items)]
    mmat = [eye * (em[i] * em[i]) - jnp.where(same, _dot_tn(kq[i], bbar[i]), 0.0) for i in n]
    s_pre = [jnp.where(same, _dot_tn(jnp.concatenate([vi[i], -u_pre[i]], axis=0),
                                     jnp.concatenate([kbar[i], bbar[i]], axis=0)), 0.0) for i in n]
    st = [st_ref[p] for p in range(npair)]
    for i, (s, sl) in enumerate(items):
        p = i % npair
        y_ref[0, 0, subs[s], sl] = (_dot_nt(r_eff[i], st[p]) + y_pre[i]).astype(y_ref.dtype)
        hi = st[p].astype(BF16)
        lo = (st[p] - hi.astype(F32)).astype(BF16)
        mb = mmat[i].astype(BF16)
        st[p] = (jnp.dot(hi, mb, preferred_element_type=F32) + jnp.dot(lo, mb, preferred_element_type=F32)
                 + s_pre[i])
    for p in range(npair):
        st_ref[p] = st[p]


def _wkv(r, v, kk, km, bv, lw, lc, dep=None):
    b, l, d = r.shape
    c = WKV_CHUNK * WKV_CHUNKS_PER_STEP
    ncc = lc // c
    nlc = (l - lc) // c

    def chunk(dr, i):
        return jnp.where(dr == 0, i, jnp.where(i < ncc, ncc - 1 - i, nlc + 2 * ncc - 1 - i))

    shared = pl.BlockSpec((1, c, d), lambda dr, bi, i: (bi, chunk(dr, i), 0))
    per_dir = pl.BlockSpec((1, 1, c, d), lambda dr, bi, i: (dr, bi, chunk(dr, i), 0))
    kern, in_specs, args = _after(dep, _wkv_kernel, [shared, shared, shared, per_dir, per_dir, per_dir],
                                  [r, v, kk, km, bv, lw])
    return pl.pallas_call(
        kern,
        out_shape=jax.ShapeDtypeStruct((2, b, l, d), BF16),
        grid=(2, b, l // c),
        in_specs=in_specs,
        out_specs=per_dir,
        scratch_shapes=[pltpu.VMEM((d // WKV_PAIR, WKV_PAIR, WKV_PAIR), F32)],
        compiler_params=pltpu.CompilerParams(dimension_semantics=("parallel", "parallel", "arbitrary"),
                                             vmem_limit_bytes=_vmem_limit(32)),
        name="wkv7_chunked",
    )(*args)


def _rwkv_out_kernel(y_ref, bonus_ref, g_ref, lnw_ref, lnb_ref, wo_ref, bd_ref, h_ref, mods_ref, gffn_ref,
                     wrt_ref, bias_ref, hn_ref, n2_ref, eid_ref, rank_ref, w_ref, cnt_ref, run_ref):
    y = y_ref[0, 0].astype(F32) + y_ref[1, 0].astype(F32)
    bd = bd_ref[...]
    mean = _head_sum(y, bd) * (1.0 / RWKV_HEAD)
    yc = y - mean
    var = _head_sum(yc * yc, bd) * (1.0 / RWKV_HEAD)
    yn = yc * lax.rsqrt(var + GN_EPS) * lnw_ref[...] + lnb_ref[...]
    out = (yn + bonus_ref[0].astype(F32)) * g_ref[0].astype(F32)
    _mixer_tail(_dot(out, wo_ref[...]), h_ref[0], mods_ref[0, 0], gffn_ref, wrt_ref, bias_ref, hn_ref, n2_ref,
                eid_ref, rank_ref, w_ref, cnt_ref, run_ref)


def _rwkv_out(y, bonus, g, lnw, lnb, wo, bd, h, mods, gffn, wrt, bias, nct, dep=None):
    b, l, d = h.shape
    tl = TOKEN_TILE
    tok = lambda w: pl.BlockSpec((1, tl, w), lambda i, j: (i, j, 0))
    full = lambda x: pl.BlockSpec(x.shape, lambda i, j: (0,) * x.ndim)
    shapes, specs = _tail_outs(b, l, d)
    kern, in_specs, args = _after(
        dep, _rwkv_out_kernel,
        [pl.BlockSpec((2, 1, tl, d), lambda i, j: (0, i, j, 0)), tok(d), tok(d),
         full(lnw), full(lnb), full(wo), full(bd), tok(d),
         pl.BlockSpec((1, 1, N_MODS, d), lambda i, j: (i, jnp.where(j < nct, 0, 1), 0, 0)),
         full(gffn), full(wrt), full(bias)],
        [y, bonus, g, lnw, lnb, wo, bd, h, mods, gffn, wrt, bias])
    return pl.pallas_call(
        kern,
        out_shape=shapes,
        grid=(b, l // tl),
        in_specs=in_specs,
        out_specs=specs,
        scratch_shapes=[pltpu.VMEM((N_EXPERTS, 1), F32)],
        compiler_params=pltpu.CompilerParams(dimension_semantics=("arbitrary", "arbitrary"),
                                             vmem_limit_bytes=_vmem_limit(40)),
        name="rwkv_out",
    )(*args)


def _rope_table(n_lat, n_ctx):
    dim = SWA_HEAD_DIM
    nf = dim // 4
    inv = ROPE_THETA ** (-jnp.arange(nf, dtype=F32) / nf)
    row = jnp.repeat(jnp.arange(n_lat // GRID_W, dtype=F32), GRID_W)
    col = jnp.tile(jnp.arange(GRID_W, dtype=F32), n_lat // GRID_W)
    ar = row[:, None] * inv
    ac = col[:, None] * inv
    ang = jnp.concatenate([ar, ar, ac, ac], axis=-1)
    cos = jnp.concatenate([jnp.ones((n_ctx, dim), F32), jnp.cos(ang)], axis=0)
    sin = jnp.concatenate([jnp.zeros((n_ctx, dim), F32), jnp.sin(ang)], axis=0)
    return jnp.tile(cos, (1, 2)), jnp.tile(sin, (1, 2))


def _layout_attn_weights(w_in, w_uq, w_ukv):
    d = w_in.shape[0]
    s0 = MLA_Q_RANK
    s1 = s0 + MLA_KV_RANK
    s2 = s1 + MLA_ROPE
    s3 = s2 + SWA_HEADS * SWA_HEAD_DIM
    s4 = s3 + SWA_KV_HEADS * SWA_HEAD_DIM
    rep = lambda w: jnp.concatenate(
        [jnp.tile(w[:, g * SWA_HEAD_DIM:(g + 1) * SWA_HEAD_DIM], (1, SWA_GROUP)) for g in range(SWA_KV_HEADS)], axis=1)
    win = jnp.concatenate([w_in[:, :s1], w_in[:, s2:s3], rep(w_in[:, s3:s4]), rep(w_in[:, s4:]),
                           w_in[:, s1:s2], jnp.zeros((d, V7X_LANES - MLA_ROPE), w_in.dtype)], axis=1)
    qh = MLA_NOPE + MLA_ROPE
    pad = jnp.zeros((w_uq.shape[0], V7X_MXU_DIM - qh), w_uq.dtype)
    wuq = jnp.concatenate([jnp.concatenate([w_uq[:, h * qh:(h + 1) * qh], pad], axis=1) for h in range(MLA_HEADS)], axis=1)
    kvh = MLA_NOPE + MLA_V
    wuk = jnp.concatenate([w_ukv[:, h * kvh:h * kvh + MLA_NOPE] for h in range(MLA_HEADS)], axis=1)
    wuvt = jnp.concatenate([w_ukv[:, h * kvh + MLA_NOPE:(h + 1) * kvh] for h in range(MLA_HEADS)], axis=1).T
    return win.astype(BF16), wuq.astype(BF16), wuk.astype(BF16), wuvt.astype(BF16)


def _lora_pair(w_down, w_up):
    rank = w_down.shape[2]
    down = jnp.concatenate([w_down[0], w_down[1]], axis=1)
    z = jnp.zeros((rank, w_up.shape[2]), w_up.dtype)
    up = jnp.stack([jnp.concatenate([w_up[0], z], axis=0), jnp.concatenate([z, w_up[1]], axis=0)], axis=0)
    return down.astype(BF16), up.astype(BF16)


def _head_block_diag():
    i = jnp.arange(V7X_MXU_DIM) // RWKV_HEAD
    return (i[:, None] == i[None, :]).astype(BF16)


def kernel(x, c, ctx, c_ctx, ada_w, ada_b, norm_mix, norm_ffn, norm_final, attn_w_in, attn_q_norm, attn_kv_norm, attn_w_uq, attn_w_ukv, attn_sinks, attn_w_o, rwkv_mu, rwkv_w_r, rwkv_w_k, rwkv_w_v, rwkv_w_o, rwkv_g1, rwkv_g2, rwkv_w0, rwkv_w1, rwkv_w2, rwkv_a0, rwkv_a1, rwkv_a2, rwkv_k_k, rwkv_k_a, rwkv_r_k, rwkv_ln_w, rwkv_ln_b, moe_router, moe_bias, moe_w_gate, moe_w_up, moe_w_down, moe_ws_gate, moe_ws_up, moe_ws_down):
    bsz, s, d = x.shape
    lc = ctx.shape[1]
    l = lc + s
    depth = ada_w.shape[0]
    nct = lc // TOKEN_TILE
    assert lc % TOKEN_TILE == 0 and s % TOKEN_TILE == 0 and s >= SWA_BAND and lc % SWA_Q_TILE == 0
    assert lc % (WKV_CHUNK * WKV_CHUNKS_PER_STEP) == 0
    assert d % V7X_MXU_DIM == 0 and WKV_CHUNK * 2 == V7X_LANES
    ngrp = SAMPLE_GROUPS
    bg = bsz // ngrp
    assert bsz % ngrp == 0 and (bg * l) % (8 * V7X_SC_WORKERS) == 0

    assert ngrp == 2
    cos, sin = _rope_table(s, lc)
    bd = _head_block_diag()
    rows = -(-(bsz + 1) // 8) * 8
    cc = jnp.concatenate([c, c_ctx[None, :], jnp.zeros((rows - bsz - 1, d), F32)], axis=0)
    row2 = lambda a: a.reshape(1, -1)
    moe_w = (moe_w_gate, moe_w_up, moe_w_down)
    moe_ws = (moe_ws_gate, moe_ws_up, moe_ws_down)

    shared = {}

    def layer_weights(li):
        if li not in shared:
            i = li // 2
            ada = _ada_mods(cc, ada_w, ada_b, li)
            w = dict(
                mods=jnp.stack([jnp.broadcast_to(ada[bsz].reshape(1, N_MODS, d), (bsz, N_MODS, d)),
                                ada[:bsz].reshape(bsz, N_MODS, d)], axis=1),
                wrt=jnp.concatenate([moe_router[li].T, jnp.zeros((GATE_W - N_EXPERTS, d), F32)], axis=0),
                bias=moe_bias[li].reshape(N_GROUPS, GROUP_SIZE, 1))
            if li % 2 == 0:
                w["win"], w["wuq"], w["wuk"], w["wuvt"] = _layout_attn_weights(attn_w_in[i], attn_w_uq[i], attn_w_ukv[i])
                w["wo"] = attn_w_o[i].astype(BF16)
            else:
                w["w1"], w["w2"] = _lora_pair(rwkv_w1[i], rwkv_w2[i])
                w["a1"], w["a2"] = _lora_pair(rwkv_a1[i], rwkv_a2[i])
                w["wr"], w["wk"], w["wv"], w["wo"] = [x[i].astype(BF16) for x in (rwkv_w_r, rwkv_w_k, rwkv_w_v, rwkv_w_o)]
                w["g1"], w["g2"] = rwkv_g1[i].astype(BF16), rwkv_g2[i].astype(BF16)
            shared[li] = w
        return shared[li]

    groups = [dict(stream=(ctx, x, g * bg, 0), b0=g * bg) for g in range(ngrp)]
    result = [None]

    def run_stage(st, li, name, dep):
        w = layer_weights(li)
        i = li // 2
        with_ctx = li < depth - 1
        mods = w["mods"][st["b0"]:st["b0"] + bg]
        if name == "proj" and li % 2 == 0:
            st["qkv"] = _attn_proj(st["stream"], bg, l, mods, row2(norm_mix[li]), w["win"], row2(attn_q_norm[i]),
                                   row2(attn_kv_norm[i]), w["wuq"], w["wuk"], w["wuvt"], cos, sin, nct, dep=dep)
            return st["qkv"][0]
        if name == "mid" and li % 2 == 0:
            q, k, vt, qs, ks, vs = st.pop("qkv")
            st["a"] = _mla_attention(q, k, vt, lc, 0 if with_ctx else lc // MLA_Q_TILE, dep=dep)
            st["bm"] = _swa_attention(attn_sinks[i], qs, ks, vs, lc, 0 if with_ctx else lc // SWA_Q_TILE, dep=st["a"])
            return st["bm"]
        if name == "proj":
            assert st["stream"][0] is st["stream"][1]
            st["feat"] = _rwkv_proj(st["stream"][0], mods, row2(norm_mix[li]), rwkv_mu[i], w["wr"], w["wk"], w["wv"],
                                    w["g1"], w["g2"], w["w1"], w["w2"], w["a1"], w["a2"], rwkv_w0[i], rwkv_a0[i],
                                    row2(rwkv_k_k[i]), row2(rwkv_k_a[i]), row2(rwkv_r_k[i]), bd, nct, dep=dep)
            return st["feat"][0]
        if name == "mid":
            r, v, kk, gt, km, bv, lw, bonus = st.pop("feat")
            st["y"] = _wkv(r, v, kk, km, bv, lw, lc, dep=dep)
            st["gate"], st["bonus"] = gt, bonus
            return st["y"]
        if name == "out":
            if li % 2 == 0:
                tail = _attn_out(st.pop("a"), st.pop("bm"), st["stream"], mods, w["wo"], row2(norm_ffn[li]),
                                 w["wrt"], w["bias"], nct, dep=dep)
            else:
                tail = _rwkv_out(st.pop("y"), st.pop("bonus"), st.pop("gate"), row2(rwkv_ln_w[i]), row2(rwkv_ln_b[i]),
                                 w["wo"], bd, st["stream"][0], mods, row2(norm_ffn[li]), w["wrt"], w["bias"], nct, dep=dep)
            st["h"], st["n2p"], eid, rank, st["wcols"], counts = tail
            st["xs"], st["dest"], st["tile_expert"], st["n_valid"] = _moe_route_rows(st["n2p"], eid, rank, counts, bg, l)
            return st["h"]
        if name == "experts":
            ys = _moe_experts(st.pop("tile_expert"), st.pop("n_valid"), st.pop("xs"), *moe_w, li, dep=dep)
            st["yg"] = _sc_gather(ys, st.pop("dest"), bg * l).reshape(TOP_K, bg, l, d // 2)
            return ys
        assert name == "combine"
        last = li == depth - 1
        h = _moe_combine(st.pop("yg"), st.pop("wcols"), st.pop("n2p"), *moe_ws, st.pop("h"), mods, row2(norm_final),
                         nct, li, result[0] if last else None, st["b0"] if last else 0, bsz if last else bg,
                         last, last, dep=dep)
        if last:
            result[0] = h
        else:
            st["stream"] = (h, h, 0, nct)
        return h

    order = [(0, 0, "proj"), (0, 0, "mid")]
    for li in range(depth):
        order += [(0, li, "out"), (1, li, "proj"), (0, li, "experts"), (1, li, "mid")]
        if li < depth - 1:
            order += [(0, li, "combine"), (1, li, "out"), (0, li + 1, "proj"), (1, li, "experts"),
                      (0, li + 1, "mid"), (1, li, "combine")]
        else:
            order += [(1, li, "out"), (0, li, "combine"), (1, li, "experts"), (1, li, "combine")]
    dep = None
    for g, li, name in order:
        dep = run_stage(groups[g], li, name, dep)
    return result[0]
```

```python
import functools

import jax
import jax.numpy as jnp
from jax import lax
from jax.experimental import pallas as pl
from jax.experimental.pallas import tpu as pltpu
from jax.experimental.pallas import tpu_sc as plsc

F32 = jnp.float32
BF16 = jnp.bfloat16
HIGHEST = lax.Precision.HIGHEST

GRID_W = 64
NORM_EPS = 1e-6
ROPE_THETA = 10000.0
NEG_INF = -1e30
N_MODS = 6

MLA_HEADS = 4
MLA_Q_RANK = 384
MLA_KV_RANK = 256
MLA_NOPE = 128
MLA_ROPE = 64
MLA_V = 128

SWA_HEADS = 8
SWA_KV_HEADS = 2
SWA_GROUP = SWA_HEADS // SWA_KV_HEADS
SWA_HEAD_DIM = 64
WINDOW = 128

RWKV_HEAD = 64
DECAY_LORA = 64
ICLR_LORA = 64
GATE_LORA = 128
GN_EPS = 64e-5

N_EXPERTS = 64
TOP_K = 6
N_GROUPS = 8
TOPK_GROUPS = 4
GROUP_SIZE = N_EXPERTS // N_GROUPS
ROUTED_SCALE = 2.5
GATE_W = 128

V7X_LANES = 128
V7X_MXU_DIM = 256
V7X_VMEM_BYTES = 64 * 1024 * 1024
V7X_SC_CORES = 2
V7X_SC_SUBCORES = 16
V7X_SC_WORKERS = V7X_SC_CORES * V7X_SC_SUBCORES

TOKEN_TILE = 256
MLA_Q_TILE = 256
MLA_HEADS_PER_STEP = 2
SWA_Q_TILE = 256
SWA_BAND = SWA_Q_TILE + 2 * WINDOW
WKV_CHUNK = 64
WKV_PAIR = 2 * RWKV_HEAD
WKV_CHUNKS_PER_STEP = 4
MOE_ROW_TILE = 512
MOE_ROW_SLOTS = 3
SAMPLE_GROUPS = 2
SC_MAX_CHUNK = 64

LOG2E = 1.4426950408889634
MIB = 1024 * 1024
VMEM_RESERVE_BYTES = 4 * MIB


def _vmem_limit(mib):
    return min(mib * MIB, V7X_VMEM_BYTES - VMEM_RESERVE_BYTES)


def _dot(a, b):
    return jnp.dot(a.astype(BF16), b.astype(BF16), preferred_element_type=F32)


def _dot_nt(a, b):
    return lax.dot_general(a.astype(BF16), b.astype(BF16), (((1,), (1,)), ((), ())),
                           preferred_element_type=F32)


def _dot_tn(a, b):
    return lax.dot_general(a.astype(BF16), b.astype(BF16), (((0,), (0,)), ((), ())),
                           preferred_element_type=F32)


def _sigmoid(x):
    return 1.0 / (1.0 + jnp.exp(-x))


def _silu(x):
    return x * _sigmoid(x)


def _rms(x, g):
    return x * lax.rsqrt(jnp.mean(x * x, axis=-1, keepdims=True) + NORM_EPS) * g


def _norm_mod(x, g, shift, scale):
    return _rms(x, g) * (1.0 + scale) + shift


def _split_dot(x, w):
    hi = x.astype(BF16)
    lo = (x - hi.astype(F32)).astype(BF16)
    return (jnp.dot(hi, w, preferred_element_type=F32) + jnp.dot(lo, w, preferred_element_type=F32))


def _head_sum(x, bd):
    w = bd.shape[0]
    parts = [_split_dot(x[:, c * w:(c + 1) * w], bd) for c in range(x.shape[1] // w)]
    return jnp.concatenate(parts, axis=1)


def _after(dep, kernel, in_specs, args, n_lead=0):
    if dep is None:
        return kernel, list(in_specs), list(args)
    n_in = n_lead + len(in_specs)

    def ordered(*refs):
        return kernel(*refs[:n_in], *refs[n_in + 1:])

    return ordered, list(in_specs) + [pl.BlockSpec(memory_space=pl.ANY)], list(args) + [dep]


def _ada_kernel(c_ref, w_ref, b_ref, o_ref):
    s = _silu(c_ref[...])
    o_ref[...] = jnp.dot(s, w_ref[0], precision=HIGHEST, preferred_element_type=F32) + b_ref[0]


def _ada_mods(cc, w, b, layer):
    rows, d = cc.shape
    depth, _, n = w.shape
    return pl.pallas_call(
        _ada_kernel,
        out_shape=jax.ShapeDtypeStruct((rows, n), F32),
        grid=(n // d,),
        in_specs=[pl.BlockSpec((rows, d), lambda i: (0, 0)),
                  pl.BlockSpec((1, d, d), lambda i: (layer, 0, i)),
                  pl.BlockSpec((1, 1, d), lambda i: (layer, 0, i))],
        out_specs=pl.BlockSpec((rows, d), lambda i: (0, i)),
        compiler_params=pltpu.CompilerParams(dimension_semantics=("parallel",),
                                             vmem_limit_bytes=_vmem_limit(32)),
        name="ada_mods",
    )(cc, w, b.reshape(depth, 1, n))


def _rope128(x, cos, sin, first_half):
    rot = jnp.where(first_half, -pltpu.roll(x, V7X_LANES - 16, axis=1), pltpu.roll(x, 16, axis=1))
    return x * cos + rot * sin


_C_CQ = 0
_C_CKV = _C_CQ + MLA_Q_RANK
_C_QS = _C_CKV + MLA_KV_RANK
_C_KS = _C_QS + SWA_HEADS * SWA_HEAD_DIM
_C_VS = _C_KS + SWA_KV_HEADS * V7X_MXU_DIM
_C_KR = _C_VS + SWA_KV_HEADS * V7X_MXU_DIM
_C_END = _C_KR + V7X_LANES
_SWA_W = SWA_KV_HEADS * V7X_MXU_DIM
_MLA_QK_W = MLA_HEADS * V7X_MXU_DIM


def _stream_specs(stream, nct, tl):
    ctx_arr, lat_arr, b0, lat_off = stream
    d = ctx_arr.shape[2]
    return [pl.BlockSpec((1, tl, d), lambda i, j: (i + b0, jnp.minimum(j, nct - 1), 0)),
            pl.BlockSpec((1, tl, d), lambda i, j: (i + b0, jnp.maximum(j - nct, 0) + lat_off, 0))]


def _stream_tile(c_ref, x_ref, nct):
    rows = c_ref.shape[1]
    take_ctx = lax.broadcasted_iota(jnp.int32, (rows, 1), 0) < jnp.where(pl.program_id(1) < nct, rows, 0)
    return jnp.where(take_ctx, c_ref[0], x_ref[0])


def _attn_proj_kernel(c_ref, x_ref, mods_ref, g_ref, win_ref, qn_ref, kvn_ref, wuq_ref, wuk_ref, wuvt_ref, cos_ref,
                      sin_ref, q_ref, k_ref, vt_ref, qs_ref, ks_ref, vs_ref, *, nct):
    m = mods_ref[0, 0]
    n = _norm_mod(_stream_tile(c_ref, x_ref, nct), g_ref[...], m[0:1], m[1:2])
    u = _dot(n, win_ref[...])
    cos = cos_ref[...]
    sin = sin_ref[...]
    lane = lax.broadcasted_iota(jnp.int32, (1, V7X_LANES), 1)
    first_half = (lane % 32) < 16

    def rope(x):
        return _rope128(x, cos, sin, first_half)

    scale_a = (MLA_NOPE + MLA_ROPE) ** -0.5 * LOG2E
    scale_b = SWA_HEAD_DIM ** -0.5 * LOG2E
    q = _dot(_rms(u[:, _C_CQ:_C_CKV], qn_ref[...]), wuq_ref[...])
    ckv = _rms(u[:, _C_CKV:_C_QS], kvn_ref[...])
    kn = _dot(ckv, wuk_ref[...])
    vt_ref[0] = _dot_nt(wuvt_ref[...], ckv).astype(BF16)
    kr = rope(u[:, _C_KR:_C_END]).astype(BF16)
    for h in range(MLA_HEADS):
        o = h * V7X_MXU_DIM
        q_ref[0, :, o:o + V7X_LANES] = (q[:, o:o + V7X_LANES] * scale_a).astype(BF16)
        q_ref[0, :, o + V7X_LANES:o + V7X_MXU_DIM] = (rope(q[:, o + V7X_LANES:o + V7X_MXU_DIM]) * scale_a).astype(BF16)
        k_ref[0, :, o:o + V7X_LANES] = kn[:, h * MLA_NOPE:(h + 1) * MLA_NOPE].astype(BF16)
        k_ref[0, :, o + V7X_LANES:o + V7X_MXU_DIM] = kr
    for c in range((_C_KS - _C_QS) // V7X_LANES):
        o = c * V7X_LANES
        qs_ref[0, :, o:o + V7X_LANES] = (rope(u[:, _C_QS + o:_C_QS + o + V7X_LANES]) * scale_b).astype(BF16)
    for c in range(_SWA_W // V7X_LANES):
        o = c * V7X_LANES
        ks_ref[0, :, o:o + V7X_LANES] = rope(u[:, _C_KS + o:_C_KS + o + V7X_LANES]).astype(BF16)
    vs_ref[0] = u[:, _C_VS:_C_KR].astype(BF16)


def _attn_proj(stream, b, l, mods, g, win, qn, kvn, wuq, wuk, wuvt, cos, sin, nct, dep=None):
    d = stream[0].shape[2]
    tl = TOKEN_TILE
    tok = lambda w: pl.BlockSpec((1, tl, w), lambda i, j: (i, j, 0))
    full = lambda a: pl.BlockSpec(a.shape, lambda i, j: (0,) * a.ndim)
    sds = jax.ShapeDtypeStruct
    dv = MLA_HEADS * MLA_V
    kern, in_specs, args = _after(
        dep, functools.partial(_attn_proj_kernel, nct=nct),
        _stream_specs(stream, nct, tl) + [
            pl.BlockSpec((1, 1, N_MODS, d), lambda i, j: (i, jnp.where(j < nct, 0, 1), 0, 0)),
            full(g), full(win), full(qn), full(kvn), full(wuq), full(wuk), full(wuvt),
            pl.BlockSpec((tl, V7X_LANES), lambda i, j: (j, 0)),
            pl.BlockSpec((tl, V7X_LANES), lambda i, j: (j, 0))],
        [stream[0], stream[1], mods, g, win, qn, kvn, wuq, wuk, wuvt, cos, sin])
    return pl.pallas_call(
        kern,
        out_shape=[sds((b, l, _MLA_QK_W), BF16), sds((b, l, _MLA_QK_W), BF16), sds((b, dv, l), BF16),
                   sds((b, l, SWA_HEADS * SWA_HEAD_DIM), BF16), sds((b, l, _SWA_W), BF16), sds((b, l, _SWA_W), BF16)],
        grid=(b, l // tl),
        in_specs=in_specs,
        out_specs=[tok(_MLA_QK_W), tok(_MLA_QK_W), pl.BlockSpec((1, dv, tl), lambda i, j: (i, 0, j)),
                   tok(SWA_HEADS * SWA_HEAD_DIM), tok(_SWA_W), tok(_SWA_W)],
        compiler_params=pltpu.CompilerParams(dimension_semantics=("parallel", "parallel"),
                                             vmem_limit_bytes=_vmem_limit(48)),
        name="attn_proj",
    )(*args)


def _mla_kernel(q_ref, k_ref, vt_ref, o_ref, *, nct_q, lc):
    hw = V7X_MXU_DIM

    def attend(nk):
        st = [_dot_nt(k_ref[0, 0:nk, hh * hw:(hh + 1) * hw], q_ref[0, :, hh * hw:(hh + 1) * hw])
              for hh in range(MLA_HEADS_PER_STEP)]
        for hh, s in enumerate(st):
            p = jnp.exp2(s - jnp.max(s, axis=0, keepdims=True))
            den = jnp.sum(p, axis=0, keepdims=True)
            ot = _dot(vt_ref[0, hh * MLA_V:(hh + 1) * MLA_V, 0:nk], p) / den
            o_ref[0, :, hh * MLA_V:(hh + 1) * MLA_V] = ot.T.astype(o_ref.dtype)

    @pl.when(pl.program_id(2) < nct_q)
    def _():
        attend(lc)

    @pl.when(pl.program_id(2) >= nct_q)
    def _():
        attend(k_ref.shape[1])


def _mla_attention(q, k, vt, lc, q_tile0, dep=None):
    b, l, _ = q.shape
    tq = MLA_Q_TILE
    hps = MLA_HEADS_PER_STEP
    kern, in_specs, args = _after(
        dep, functools.partial(_mla_kernel, nct_q=lc // tq - q_tile0, lc=lc),
        [pl.BlockSpec((1, tq, hps * V7X_MXU_DIM), lambda i, h, j: (i, j + q_tile0, h)),
         pl.BlockSpec((1, l, hps * V7X_MXU_DIM), lambda i, h, j: (i, 0, h)),
         pl.BlockSpec((1, hps * MLA_V, l), lambda i, h, j: (i, h, 0))],
        [q, k, vt])
    return pl.pallas_call(
        kern,
        out_shape=jax.ShapeDtypeStruct((b, l, MLA_HEADS * MLA_V), BF16),
        grid=(b, MLA_HEADS // hps, l // tq - q_tile0),
        in_specs=in_specs,
        out_specs=pl.BlockSpec((1, tq, hps * MLA_V), lambda i, h, j: (i, j + q_tile0, h)),
        compiler_params=pltpu.CompilerParams(dimension_semantics=("parallel", "parallel", "parallel"),
                                             vmem_limit_bytes=_vmem_limit(48)),
        name="mla_attention",
    )(*args)


def _swa_kernel(sink_ref, q_ref, k_ref, v_ref, o_ref, *, lc, q_tile0):
    tq = SWA_Q_TILE
    l = k_ref.shape[1]
    r0 = (pl.program_id(1) + q_tile0) * tq
    start = pl.multiple_of(jnp.clip(r0 - WINDOW, lc, l - SWA_BAND), WINDOW)
    rows = SWA_GROUP * tq
    row = lax.broadcasted_iota(jnp.int32, (rows, 1), 0)
    qpos = jnp.where(r0 >= lc, r0, -l) + row % tq
    kpos = start + lax.broadcasted_iota(jnp.int32, (1, SWA_BAND), 1)
    valid = jnp.abs(qpos - kpos) <= WINDOW
    lane = lax.broadcasted_iota(jnp.int32, (1, V7X_MXU_DIM), 1)
    head = [(lane // SWA_HEAD_DIM) == hh for hh in range(SWA_GROUP)]
    groups = range(SWA_KV_HEADS)
    sls = [slice(g * V7X_MXU_DIM, (g + 1) * V7X_MXU_DIM) for g in groups]
    qstack = []
    for sl in sls:
        qg = q_ref[0, :, sl]
        zero = jnp.zeros_like(qg)
        qstack.append(jnp.concatenate([jnp.where(head[hh], qg, zero) for hh in range(SWA_GROUP)], axis=0))
    sc = [_dot_nt(qstack[g], k_ref[0, 0:lc, sls[g]]) for g in groups]
    sb = [_dot_nt(qstack[g], k_ref[0, pl.ds(start, SWA_BAND), sls[g]]) for g in groups]
    for g in groups:
        sl = sls[g]
        sbm = jnp.where(valid, sb[g], NEG_INF)
        sk = jnp.zeros((rows, 1), F32)
        for hh in range(SWA_GROUP):
            sk = jnp.where(row // tq == hh, sink_ref[g * SWA_GROUP + hh] * LOG2E, sk)
        mx = jnp.maximum(jnp.maximum(jnp.max(sc[g], axis=-1, keepdims=True), jnp.max(sbm, axis=-1, keepdims=True)), sk)
        pc = jnp.exp2(sc[g] - mx)
        pb = jnp.exp2(sbm - mx)
        den = jnp.sum(pc, axis=-1, keepdims=True) + jnp.sum(pb, axis=-1, keepdims=True) + jnp.exp2(sk - mx)
        ostack = (_dot(pc, v_ref[0, 0:lc, sl]) + _dot(pb, v_ref[0, pl.ds(start, SWA_BAND), sl])) / den
        o = jnp.zeros((tq, V7X_MXU_DIM), F32)
        for hh in range(SWA_GROUP):
            o = o + jnp.where(head[hh], ostack[hh * tq:(hh + 1) * tq], 0.0)
        o_ref[0, :, sl] = o.astype(o_ref.dtype)


def _swa_attention(sinks, q, k, v, lc, q_tile0, dep=None):
    b, l, _ = q.shape
    tq = SWA_Q_TILE
    kern, in_specs, args = _after(
        dep, functools.partial(_swa_kernel, lc=lc, q_tile0=q_tile0),
        [pl.BlockSpec(memory_space=pltpu.SMEM),
         pl.BlockSpec((1, tq, SWA_HEADS * SWA_HEAD_DIM), lambda i, j: (i, j + q_tile0, 0)),
         pl.BlockSpec((1, l, _SWA_W), lambda i, j: (i, 0, 0)),
         pl.BlockSpec((1, l, _SWA_W), lambda i, j: (i, 0, 0))],
        [sinks, q, k, v])
    return pl.pallas_call(
        kern,
        out_shape=jax.ShapeDtypeStruct((b, l, SWA_HEADS * SWA_HEAD_DIM), BF16),
        grid=(b, l // tq - q_tile0),
        in_specs=in_specs,
        out_specs=pl.BlockSpec((1, tq, SWA_HEADS * SWA_HEAD_DIM), lambda i, j: (i, j + q_tile0, 0)),
        compiler_params=pltpu.CompilerParams(dimension_semantics=("parallel", "parallel"),
                                             vmem_limit_bytes=_vmem_limit(48)),
        name="swa_attention",
    )(*args)


def _pack_bf16_pair(x):
    w = x.shape[1] // 2
    lo = pltpu.bitcast(x[:, :w].astype(BF16).astype(F32), jnp.int32)
    hi = pltpu.bitcast(x[:, w:].astype(BF16).astype(F32), jnp.int32)
    return lax.shift_right_logical(lo, jnp.int32(16)) | (hi & jnp.int32(-65536))


def _unpack_bf16_pair(p):
    return pltpu.bitcast(p << 16, F32), pltpu.bitcast(p & jnp.int32(-65536), F32)


def _route(n2, wrt, bias, run_ref):
    n_hi = n2.astype(BF16)
    n_lo = (n2 - n_hi.astype(F32)).astype(BF16)
    w_hi = wrt.astype(BF16)
    w_lo = (wrt - w_hi.astype(F32)).astype(BF16)
    logits = _dot_nt(w_hi, n_hi) + (_dot_nt(w_hi, n_lo) + _dot_nt(w_lo, n_hi))
    rows = logits.shape[1]
    scores = _sigmoid(logits[0:N_EXPERTS])

    def select(sc2):
        cols = sc2.shape[1]
        shape3 = (N_GROUPS, GROUP_SIZE, cols)
        choice = sc2.reshape(shape3) + bias
        ji = lax.broadcasted_iota(jnp.int32, shape3, 1).astype(F32)
        m1 = jnp.max(choice, axis=1, keepdims=True)
        first = jnp.min(jnp.where(choice == m1, ji, float(GROUP_SIZE)), axis=1, keepdims=True)
        m2 = jnp.max(jnp.where(ji == first, -jnp.inf, choice), axis=1, keepdims=True)
        gs = m1 + m2
        gidx = lax.broadcasted_iota(jnp.int32, gs.shape, 0).astype(F32)
        gsel = jnp.zeros_like(gs)
        for _ in range(TOPK_GROUPS):
            mx = jnp.max(gs, axis=0, keepdims=True)
            pick = gidx == jnp.min(jnp.where(gs == mx, gidx, float(N_GROUPS)), axis=0, keepdims=True)
            gsel = jnp.where(pick, 1.0, gsel)
            gs = jnp.where(pick, -jnp.inf, gs)
        cand = jnp.where(gsel > 0.0, choice, -jnp.inf).reshape(N_EXPERTS, cols)
        eidx = lax.broadcasted_iota(jnp.int32, (N_EXPERTS, cols), 0).astype(F32)
        out = []
        for _ in range(TOP_K):
            mx = jnp.max(cand, axis=0, keepdims=True)
            pick = eidx == jnp.min(jnp.where(cand == mx, eidx, float(N_EXPERTS)), axis=0, keepdims=True)
            out.append(jnp.where(pick, 1.0, 0.0))
            cand = jnp.where(pick, -jnp.inf, cand)
        return out

    blocks = [select(scores[:, o:o + V7X_LANES]) for o in range(0, rows, V7X_LANES)]
    picks = [jnp.concatenate([blk[k] for blk in blocks], axis=1) > 0.0 for k in range(TOP_K)]
    ei = lax.broadcasted_iota(jnp.int32, (N_EXPERTS, rows), 0).astype(F32)
    esel = jnp.zeros((N_EXPERTS, rows), F32)
    for pick in picks:
        esel = jnp.where(pick, 1.0, esel)
    before = jnp.where(lax.broadcasted_iota(jnp.int32, (rows, rows), 0) < lax.broadcasted_iota(jnp.int32, (rows, rows), 1),
                       1.0, 0.0).astype(BF16)
    slot = jnp.dot(esel.astype(BF16), before, preferred_element_type=F32) + run_ref[...]
    run_ref[...] += jnp.sum(esel, axis=1, keepdims=True)
    sc = [jnp.sum(jnp.where(pick, scores, 0.0), axis=0, keepdims=True) for pick in picks]
    tot = sc[0]
    for x in sc[1:]:
        tot = tot + x
    k8 = lax.broadcasted_iota(jnp.int32, (8, rows), 0)
    kw = lax.broadcasted_iota(jnp.int32, (GATE_W, rows), 0)
    eid = jnp.zeros((8, rows), jnp.int32)
    rank = jnp.zeros((8, rows), jnp.int32)
    wk = jnp.zeros((GATE_W, rows), F32)
    for k, pick in enumerate(picks):
        e_k = jnp.sum(jnp.where(pick, ei, 0.0), axis=0, keepdims=True).astype(jnp.int32)
        r_k = jnp.sum(jnp.where(pick, slot, 0.0), axis=0, keepdims=True).astype(jnp.int32)
        eid = jnp.where(k8 == k, e_k, eid)
        rank = jnp.where(k8 == k, r_k, rank)
        wk = jnp.where(kw == k, sc[k] * (ROUTED_SCALE / tot), wk)
    return eid, rank, wk.T


def _mixer_tail(o, h, m, gffn_ref, wrt_ref, bias_ref, hn_ref, n2_ref, eid_ref, rank_ref, w_ref, cnt_ref, run_ref):
    @pl.when((pl.program_id(0) == 0) & (pl.program_id(1) == 0))
    def _():
        run_ref[...] = jnp.zeros_like(run_ref)

    hn = h + m[2:3] * o
    hn_ref[0] = hn
    n2 = _norm_mod(hn, gffn_ref[...], m[3:4], m[4:5])
    n2_ref[0] = _pack_bf16_pair(n2)
    eid, rank, wcols = _route(n2, wrt_ref[...], bias_ref[...], run_ref)
    eid_ref[0] = eid
    rank_ref[0] = rank
    w_ref[0] = wcols
    cnt_ref[...] = run_ref[...]


def _attn_out_kernel(a_ref, b_ref, c_ref, x_ref, mods_ref, wo_ref, gffn_ref, wrt_ref, bias_ref,
                     hn_ref, n2_ref, eid_ref, rank_ref, w_ref, cnt_ref, run_ref, *, nct):
    wa = MLA_HEADS * MLA_V
    o = _dot(a_ref[0], wo_ref[0:wa, :]) + _dot(b_ref[0], wo_ref[wa:, :])
    _mixer_tail(o, _stream_tile(c_ref, x_ref, nct), mods_ref[0, 0], gffn_ref, wrt_ref, bias_ref, hn_ref, n2_ref,
                eid_ref, rank_ref, w_ref, cnt_ref, run_ref)


def _tail_outs(b, l, d):
    tl = TOKEN_TILE
    nt = l // tl
    sds = jax.ShapeDtypeStruct
    tok = lambda w: pl.BlockSpec((1, tl, w), lambda i, j: (i, j, 0))
    blk = pl.BlockSpec((1, 8, tl), lambda i, j: (i * nt + j, 0, 0))
    shapes = [sds((b, l, d), F32), sds((b, l, d // 2), jnp.int32), sds((b * nt, 8, tl), jnp.int32),
              sds((b * nt, 8, tl), jnp.int32), sds((b, l, GATE_W), F32), sds((N_EXPERTS, 1), F32)]
    specs = [tok(d), tok(d // 2), blk, blk, tok(GATE_W), pl.BlockSpec((N_EXPERTS, 1), lambda i, j: (0, 0))]
    return shapes, specs


def _attn_out(a, bm, stream, mods, wo, gffn, wrt, bias, nct, dep=None):
    b, l, _ = a.shape
    d = stream[0].shape[2]
    tl = TOKEN_TILE
    tok = lambda w: pl.BlockSpec((1, tl, w), lambda i, j: (i, j, 0))
    full = lambda x: pl.BlockSpec(x.shape, lambda i, j: (0,) * x.ndim)
    shapes, specs = _tail_outs(b, l, d)
    kern, in_specs, args = _after(
        dep, functools.partial(_attn_out_kernel, nct=nct),
        [tok(a.shape[2]), tok(bm.shape[2])] + _stream_specs(stream, nct, tl) + [
            pl.BlockSpec((1, 1, N_MODS, d), lambda i, j: (i, jnp.where(j < nct, 0, 1), 0, 0)),
            full(wo), full(gffn), full(wrt), full(bias)],
        [a, bm, stream[0], stream[1], mods, wo, gffn, wrt, bias])
    return pl.pallas_call(
        kern,
        out_shape=shapes,
        grid=(b, l // tl),
        in_specs=in_specs,
        out_specs=specs,
        scratch_shapes=[pltpu.VMEM((N_EXPERTS, 1), F32)],
        compiler_params=pltpu.CompilerParams(dimension_semantics=("arbitrary", "arbitrary"),
                                             vmem_limit_bytes=_vmem_limit(40)),
        name="attn_out",
    )(*args)


def _moe_dest_kernel(off_ref, eid_ref, rank_ref, dest_ref):
    eid = eid_ref[...]
    dest = rank_ref[...]
    for e in range(N_EXPERTS):
        dest = dest + jnp.where(eid == e, off_ref[e], 0)
    dest_ref[...] = dest


def _moe_dest(off, eid, rank):
    return pl.pallas_call(
        _moe_dest_kernel,
        out_shape=jax.ShapeDtypeStruct(eid.shape, jnp.int32),
        in_specs=[pl.BlockSpec(memory_space=pltpu.SMEM),
                  pl.BlockSpec(eid.shape, lambda: (0, 0, 0)), pl.BlockSpec(eid.shape, lambda: (0, 0, 0))],
        out_specs=pl.BlockSpec(eid.shape, lambda: (0, 0, 0)),
        name="moe_dest",
    )(off, eid, rank)


def _sc_mesh():
    return plsc.VectorSubcoreMesh(core_axis_name="c", subcore_axis_name="s",
                                  num_cores=V7X_SC_CORES, num_subcores=V7X_SC_SUBCORES)


def _sc_chunk(rows_per_worker):
    return max(c for c in range(8, SC_MAX_CHUNK + 1, 8) if rows_per_worker % c == 0)


def _sc_dispatch(xp, dest, p_rows):
    t, w = xp.shape
    tpw = t // V7X_SC_WORKERS
    ch = _sc_chunk(tpw)

    @functools.partial(
        pl.kernel, mesh=_sc_mesh(), out_type=jax.ShapeDtypeStruct((p_rows, w), xp.dtype),
        scratch_types=[pltpu.VMEM((ch, w), xp.dtype)] + [pltpu.VMEM((ch,), jnp.int32)] * TOP_K
        + [pltpu.SemaphoreType.DMA, pltpu.SemaphoreType.DMA],
        name="moe_dispatch")
    def run(x_hbm, dest_hbm, out_hbm, rows_v, *rest):
        idx, (sem_i, sem_o) = rest[:TOP_K], rest[TOP_K:]
        base = (lax.axis_index("s") * V7X_SC_CORES + lax.axis_index("c")) * tpw

        @pl.loop(0, tpw // ch)
        def _(i):
            t0 = base + i * ch
            loads = [pltpu.async_copy(dest_hbm.at[k, pl.ds(t0, ch)], idx[k], sem_i) for k in range(TOP_K)]
            pltpu.sync_copy(x_hbm.at[pl.ds(t0, ch)], rows_v)
            for c in loads:
                c.wait()
            puts = [pltpu.async_copy(rows_v, out_hbm.at[idx[k]], sem_o) for k in range(TOP_K)]
            for c in puts:
                c.wait()

    return run(xp, dest)


def _sc_gather(ys, dest, t):
    w = ys.shape[1]
    tpw = t // V7X_SC_WORKERS
    ch = _sc_chunk(tpw)

    @functools.partial(
        pl.kernel, mesh=_sc_mesh(), out_type=jax.ShapeDtypeStruct((TOP_K, t, w), ys.dtype),
        scratch_types=[pltpu.VMEM((ch, w), ys.dtype)] * 2 + [pltpu.VMEM((ch,), jnp.int32)] * TOP_K
        + [pltpu.SemaphoreType.DMA] * 5,
        name="moe_gather")
    def run(y_hbm, dest_hbm, out_hbm, rows_a, rows_b, *rest):
        idx, (sem_i, sem_ga, sem_gb, sem_wa, sem_wb) = rest[:TOP_K], rest[TOP_K:]
        rows, sem_g, sem_w = (rows_a, rows_b), (sem_ga, sem_gb), (sem_wa, sem_wb)
        base = (lax.axis_index("s") * V7X_SC_CORES + lax.axis_index("c")) * tpw

        @pl.loop(0, tpw // ch)
        def _(i):
            t0 = base + i * ch
            loads = [pltpu.async_copy(dest_hbm.at[k, pl.ds(t0, ch)], idx[k], sem_i) for k in range(TOP_K)]
            for c in loads:
                c.wait()
            gets, puts = [None] * TOP_K, [None] * TOP_K
            gets[0] = pltpu.async_copy(y_hbm.at[idx[0]], rows[0], sem_g[0])
            for k in range(TOP_K):
                if k + 1 < TOP_K:
                    if k >= 1:
                        puts[k - 1].wait()
                    gets[k + 1] = pltpu.async_copy(y_hbm.at[idx[k + 1]], rows[(k + 1) % 2], sem_g[(k + 1) % 2])
                gets[k].wait()
                puts[k] = pltpu.async_copy(rows[k % 2], out_hbm.at[k, pl.ds(t0, ch)], sem_w[k % 2])
            puts[TOP_K - 2].wait()
            puts[TOP_K - 1].wait()

    return run(ys, dest)


def _cache_mlp_weights(wg, wu, wd, wgu_ref, wdb_ref):
    f = wg.shape[1]
    wgu_ref[:, 0:f] = wg.astype(BF16)
    wgu_ref[:, f:] = wu.astype(BF16)
    wdb_ref[...] = wd.astype(BF16)


def _gated_mlp(xp, wgu_ref, wdb_ref):
    lo, hi = _unpack_bf16_pair(xp)
    x = jnp.concatenate([lo.astype(BF16), hi.astype(BF16)], axis=1)
    gu = jnp.dot(x, wgu_ref[...], preferred_element_type=F32)
    f = gu.shape[1] // 2
    return _dot(_silu(gu[:, :f]) * gu[:, f:], wdb_ref[...])


def _moe_expert_kernel(te_ref, tb_ref, nv_ref, x_hbm, wga_ref, wua_ref, wda_ref, wgb_ref, wub_ref, wdb_ref, y_ref,
                       gu_a, dn_a, gu_b, dn_b, ids_ref, xbuf, sems):
    i = pl.program_id(0)
    tm = MOE_ROW_TILE
    nv = nv_ref[0]
    last = (nv - 1) // 2
    first = 2 * jnp.minimum(i, last)
    ea = te_ref[first]
    eb = te_ref[first + 1]
    two = 2 * i + 1 < nv

    def rows_copy(step):
        slot = step % MOE_ROW_SLOTS
        row0 = step * (2 * tm)
        rows = pl.ds(row0 if isinstance(step, int) else pl.multiple_of(row0, 2 * tm), 2 * tm)
        return pltpu.make_async_copy(x_hbm.at[rows], xbuf.at[slot], sems.at[slot])

    @pl.when(i == 0)
    def _():
        ids_ref[0] = -1
        ids_ref[1] = -1
        for ahead in range(MOE_ROW_SLOTS - 1):
            @pl.when(ahead <= last)
            def _():
                rows_copy(ahead).start()

    @pl.when(i + (MOE_ROW_SLOTS - 1) <= last)
    def _():
        rows_copy(i + (MOE_ROW_SLOTS - 1)).start()

    @pl.when(i <= last)
    def _():
        rows_copy(i).wait()

    x_ref = xbuf.at[i % MOE_ROW_SLOTS]

    @pl.when(ids_ref[0] != ea)
    def _():
        _cache_mlp_weights(wga_ref[0, 0], wua_ref[0, 0], wda_ref[0, 0], gu_a, dn_a)
        ids_ref[0] = ea

    @pl.when(two & (eb != ea) & (ids_ref[1] != eb))
    def _():
        _cache_mlp_weights(wgb_ref[0, 0], wub_ref[0, 0], wdb_ref[0, 0], gu_b, dn_b)
        ids_ref[1] = eb

    @pl.when(two & (eb == ea))
    def _():
        y_ref[...] = _pack_bf16_pair(_gated_mlp(x_ref[...], gu_a, dn_a))

    @pl.when((2 * i < nv) & jnp.logical_not(two & (eb == ea)))
    def _():
        y_ref[0:tm, :] = _pack_bf16_pair(_gated_mlp(x_ref[0:tm, :], gu_a, dn_a))

    @pl.when(two & (eb != ea))
    def _():
        y_ref[tm:, :] = _pack_bf16_pair(_gated_mlp(x_ref[tm:, :], gu_b, dn_b))


def _moe_experts(tile_expert, n_valid, xs, wg, wu, wd, layer, dep=None):
    p_rows, w = xs.shape
    tm = MOE_ROW_TILE
    _, _, d, f = wg.shape
    npair = p_rows // (2 * tm)
    pairs = tile_expert.reshape(npair, 2)
    tile_b = jnp.maximum(lax.cummax(jnp.where(pairs[:, 1] != pairs[:, 0], pairs[:, 1], -1)), 0)
    step = lambda i, nv: jnp.minimum(i, (nv[0] - 1) // 2)
    spec_a = lambda shp: pl.BlockSpec((1, 1) + shp, lambda i, te, tb, nv: (layer, te[2 * step(i, nv)], 0, 0))
    spec_b = lambda shp: pl.BlockSpec((1, 1) + shp, lambda i, te, tb, nv: (layer, tb[step(i, nv)], 0, 0))
    rows = pl.BlockSpec((2 * tm, w), lambda i, te, tb, nv: (step(i, nv), 0))
    kern, in_specs, args = _after(
        dep, _moe_expert_kernel,
        [pl.BlockSpec(memory_space=pl.ANY), spec_a((d, f)), spec_a((d, f)), spec_a((f, d)),
         spec_b((d, f)), spec_b((d, f)), spec_b((f, d))],
        [tile_expert, tile_b, n_valid, xs, wg, wu, wd, wg, wu, wd], n_lead=3)
    return pl.pallas_call(
        kern,
        out_shape=jax.ShapeDtypeStruct((p_rows, w), xs.dtype),
        grid_spec=pltpu.PrefetchScalarGridSpec(
            num_scalar_prefetch=3, grid=(npair,),
            in_specs=in_specs,
            out_specs=rows,
            scratch_shapes=[pltpu.VMEM((d, 2 * f), BF16), pltpu.VMEM((f, d), BF16),
                            pltpu.VMEM((d, 2 * f), BF16), pltpu.VMEM((f, d), BF16), pltpu.SMEM((2,), jnp.int32),
                            pltpu.VMEM((MOE_ROW_SLOTS, 2 * tm, w), xs.dtype),
                            pltpu.SemaphoreType.DMA((MOE_ROW_SLOTS,))]),
        compiler_params=pltpu.CompilerParams(dimension_semantics=("arbitrary",),
                                             vmem_limit_bytes=_vmem_limit(48)),
        name="moe_experts",
    )(*args)


def _moe_combine_kernel(yg_hbm, w_ref, xp_ref, sg_ref, su_ref, sd_ref, h_ref, mods_ref, gfin_ref, *rest, final_norm,
                        tile0):
    o_ref, wgu_ref, wdb_ref, ybuf, sems = rest[-5:]
    tl = ybuf.shape[2]
    nt = pl.num_programs(1)
    step = pl.program_id(0) * nt + pl.program_id(1)
    n_steps = pl.num_programs(0) * nt

    def yg_copy(s):
        rows = pl.ds(pl.multiple_of((s % nt + tile0) * tl, tl), tl)
        return pltpu.make_async_copy(yg_hbm.at[:, s // nt, rows, :], ybuf.at[s % MOE_ROW_SLOTS],
                                     sems.at[s % MOE_ROW_SLOTS])

    @pl.when(step == 0)
    def _():
        _cache_mlp_weights(sg_ref[0], su_ref[0], sd_ref[0], wgu_ref, wdb_ref)
        for ahead in range(MOE_ROW_SLOTS - 1):
            @pl.when(ahead < n_steps)
            def _():
                yg_copy(ahead).start()

    @pl.when(step + (MOE_ROW_SLOTS - 1) < n_steps)
    def _():
        yg_copy(step + (MOE_ROW_SLOTS - 1)).start()

    yg_copy(step).wait()
    yg_ref = ybuf.at[step % MOE_ROW_SLOTS]

    acc = _gated_mlp(xp_ref[0], wgu_ref, wdb_ref)
    half = acc.shape[1] // 2
    lo = acc[:, :half]
    hi = acc[:, half:]
    w = w_ref[0]
    for k in range(TOP_K):
        ylo, yhi = _unpack_bf16_pair(yg_ref[k])
        wk = w[:, k:k + 1]
        lo = lo + wk * ylo
        hi = hi + wk * yhi
    y = h_ref[0] + mods_ref[0, 0, N_MODS - 1:N_MODS, :] * jnp.concatenate([lo, hi], axis=1)
    if final_norm:
        y = _rms(y, gfin_ref[...])
    o_ref[0] = y


def _moe_combine(yg, wcols, xp, sg, su, sd, h, mods, gfin, nct, layer, out_buf, out_b0, out_batch, latent_only,
                 final_norm, dep=None):
    b, l, d = h.shape
    tl = TOKEN_TILE
    tile0 = nct if latent_only else 0
    tok = lambda w: pl.BlockSpec((1, tl, w), lambda i, j: (i, j + tile0, 0))
    lay = lambda x: pl.BlockSpec((1,) + x.shape[1:], lambda i, j: (layer,) + (0,) * (x.ndim - 1))
    args = [yg, wcols, xp, sg, su, sd, h, mods, gfin]
    in_specs = [pl.BlockSpec(memory_space=pl.ANY), tok(GATE_W), tok(d // 2),
                lay(sg), lay(su), lay(sd), tok(d),
                pl.BlockSpec((1, 1, N_MODS, d), lambda i, j: (i, jnp.where(j + tile0 < nct, 0, 1), 0, 0)),
                pl.BlockSpec(gfin.shape, lambda i, j: (0, 0))]
    _, in_specs, args = _after(dep, None, in_specs, args)
    aliases = {}
    if out_buf is not None:
        args.append(out_buf)
        in_specs.append(pl.BlockSpec(memory_space=pl.ANY))
        aliases = {len(args) - 1: 0}
    return pl.pallas_call(
        functools.partial(_moe_combine_kernel, final_norm=final_norm, tile0=tile0),
        out_shape=jax.ShapeDtypeStruct((out_batch, l - tile0 * tl, d), F32),
        grid=(b, l // tl - tile0),
        in_specs=in_specs,
        out_specs=pl.BlockSpec((1, tl, d), lambda i, j: (i + out_b0, j, 0)),
        scratch_shapes=[pltpu.VMEM((d, 2 * sg.shape[2]), BF16), pltpu.VMEM((sg.shape[2], d), BF16),
                        pltpu.VMEM((MOE_ROW_SLOTS, TOP_K, tl, d // 2), yg.dtype),
                        pltpu.SemaphoreType.DMA((MOE_ROW_SLOTS,))],
        input_output_aliases=aliases,
        compiler_params=pltpu.CompilerParams(dimension_semantics=("arbitrary", "arbitrary"),
                                             vmem_limit_bytes=_vmem_limit(40)),
        name="moe_combine",
    )(*args)


def _moe_route_rows(n2p, eid, rank, counts, b, l):
    d2 = n2p.shape[2]
    t = b * l
    tm = MOE_ROW_TILE
    n_tiles = 2 * -(-(TOP_K * t + N_EXPERTS * (tm - 1)) // (2 * tm))
    tiles_e = (counts.reshape(N_EXPERTS).astype(jnp.int32) + (tm - 1)) // tm
    tile_end = jnp.cumsum(tiles_e)
    off = (tile_end - tiles_e) * tm
    n_valid = tile_end[-1:]
    tile_id = jnp.minimum(jnp.arange(n_tiles, dtype=jnp.int32), n_valid - 1)
    tile_expert = jnp.sum((tile_end[None, :] <= tile_id[:, None]).astype(jnp.int32), axis=1)
    dest = _moe_dest(off, eid, rank).transpose(1, 0, 2).reshape(8, t)
    xs = _sc_dispatch(n2p.reshape(t, d2), dest, n_tiles * tm)
    return xs, dest, tile_expert, n_valid


def _rwkv_proj_kernel(h_ref, hp_ref, hx_ref, mods_ref, g_ref, mu_ref, wr_ref, wk_ref, wv_ref, g1_ref, g2_ref,
                      w1_ref, w2_ref, a1_ref, a2_ref, w0_ref, a0_ref, kk_ref, ka_ref, rk_ref, bd_ref,
                      r_out, v_out, kk_out, g_out, km_out, b_out, lw_out, bonus_out, *, nct):
    j = pl.program_id(1)
    nt = pl.num_programs(1)
    m = mods_ref[0, 0]
    g = g_ref[...]
    n = _norm_mod(h_ref[0], g, m[0:1], m[1:2])
    tl, d = n.shape
    seg_first = (j == 0) | (j == nct)
    seg_last = (j == nct - 1) | (j == nt - 1)
    n_prev = _norm_mod(hp_ref[0], g, m[0:1], m[1:2])[7:8] * jnp.where(seg_first, 0.0, 1.0)
    n_next = _norm_mod(hx_ref[0], g, m[0:1], m[1:2])[0:1] * jnp.where(seg_last, 0.0, 1.0)
    row = lax.broadcasted_iota(jnp.int32, (tl, 1), 0)
    prev = jnp.where(row == 0, n_prev, pltpu.roll(n, 1, axis=0))
    nxt = jnp.where(row == tl - 1, n_next, pltpu.roll(n, tl - 1, axis=0))
    lane = lax.broadcasted_iota(jnp.int32, (1, d), 1)
    xx = jnp.where(lane < d // 2, prev, nxt) - n
    mu = mu_ref[...]
    bd = bd_ref[...]
    halves = [slice(0, tl // 2), slice(tl // 2, tl)]
    first = []
    for rs in halves:
        nh, xh = n[rs], xx[rs]
        xr, xw, xk, xv, xa, xg = [nh + xh * mu[i:i + 1] for i in range(6)]
        first.append((_dot(xr, wr_ref[...]), _dot(xk, wk_ref[...]), _dot(xv, wv_ref[...]),
                      _dot(xg, g1_ref[...]), _dot(xw, w1_ref[...]), _dot(xa, a1_ref[...])))
    second = []
    for r, k, v, gq, tq, ta in first:
        tw = jnp.tanh(tq)
        kk = k * kk_ref[...]
        second.append((_dot(_sigmoid(gq), g2_ref[...]), [_dot(tw, w2_ref[dr]) for dr in range(2)],
                       [_dot(ta, a2_ref[dr]) for dr in range(2)], kk, _head_sum(kk * kk, bd)))
    for rs, (r, k, v, _, _, _), (gate, zw, za, kk, kk_sq) in zip(halves, first, second):
        kk = kk / jnp.maximum(jnp.sqrt(kk_sq), 1e-12)
        g_out[0, rs, :] = gate.astype(g_out.dtype)
        r_out[0, rs, :] = r.astype(r_out.dtype)
        v_out[0, rs, :] = v.astype(v_out.dtype)
        kk_out[0, rs, :] = kk.astype(kk_out.dtype)
        bonus = jnp.zeros_like(v)
        for dr in range(2):
            lw_out[dr, 0, rs, :] = -jnp.exp(-0.5) * _sigmoid(w0_ref[dr:dr + 1, :] + zw[dr])
            a = _sigmoid(a0_ref[dr:dr + 1, :] + za[dr])
            km = k * (1.0 + (a - 1.0) * ka_ref[...])
            km_out[dr, 0, rs, :] = km.astype(km_out.dtype)
            b_out[dr, 0, rs, :] = (kk * a).astype(b_out.dtype)
            bonus = bonus + _head_sum(r * km * rk_ref[...], bd) * v
        bonus_out[0, rs, :] = bonus.astype(bonus_out.dtype)


def _rwkv_proj(h, mods, g, mu, wr, wk, wv, g1, g2, w1, w2, a1, a2, w0, a0, kk, ka, rk, bd, nct, dep=None):
    b, l, d = h.shape
    tl = TOKEN_TILE
    nb8 = l // 8
    tok = pl.BlockSpec((1, tl, d), lambda i, j: (i, j, 0))
    tok2 = pl.BlockSpec((2, 1, tl, d), lambda i, j: (0, i, j, 0))
    full = lambda x: pl.BlockSpec(x.shape, lambda i, j: (0,) * x.ndim)
    sds = jax.ShapeDtypeStruct
    kern, in_specs, args = _after(
        dep, functools.partial(_rwkv_proj_kernel, nct=nct),
        [tok,
         pl.BlockSpec((1, 8, d), lambda i, j: (i, jnp.maximum(j * (tl // 8) - 1, 0), 0)),
         pl.BlockSpec((1, 8, d), lambda i, j: (i, jnp.minimum((j + 1) * (tl // 8), nb8 - 1), 0)),
         pl.BlockSpec((1, 1, N_MODS, d), lambda i, j: (i, jnp.where(j < nct, 0, 1), 0, 0)),
         full(g), full(mu), full(wr), full(wk), full(wv), full(g1), full(g2), full(w1), full(w2),
         full(a1), full(a2), full(w0), full(a0), full(kk), full(ka), full(rk), full(bd)],
        [h, h, h, mods, g, mu, wr, wk, wv, g1, g2, w1, w2, a1, a2, w0, a0, kk, ka, rk, bd])
    return pl.pallas_call(
        kern,
        out_shape=[sds((b, l, d), BF16), sds((b, l, d), BF16), sds((b, l, d), BF16), sds((b, l, d), BF16),
                   sds((2, b, l, d), BF16), sds((2, b, l, d), BF16), sds((2, b, l, d), F32), sds((b, l, d), BF16)],
        grid=(b, l // tl),
        in_specs=in_specs,
        out_specs=[tok, tok, tok, tok, tok2, tok2, tok2, tok],
        compiler_params=pltpu.CompilerParams(dimension_semantics=("parallel", "parallel"),
                                             vmem_limit_bytes=_vmem_limit(56)),
        name="rwkv_proj",
    )(*args)


def _wkv_kernel(r_ref, v_ref, kk_ref, km_ref, b_ref, lw_ref, y_ref, st_ref):
    c = WKV_CHUNK
    w = WKV_PAIR
    rev = pl.program_id(0)
    sign = 1 - 2 * rev

    @pl.when(pl.program_id(2) == 0)
    def _():
        st_ref[...] = jnp.zeros_like(st_ref)

    ti = lax.broadcasted_iota(jnp.int32, (c, c), 0)
    si = lax.broadcasted_iota(jnp.int32, (c, c), 1)
    tri = jnp.where((si - ti) * sign <= 0, 1.0, 0.0).astype(F32)
    nsub = WKV_CHUNKS_PER_STEP
    subs = [pl.ds(pl.multiple_of(jnp.where(rev == 0, s, nsub - 1 - s) * c, c), c) for s in range(nsub)]
    rt, kt, kh, bh, v32, e_mid = [], [], [], [], [], []
    for rows in subs:
        lw = lw_ref[0, 0, rows, :]
        l_incl = jnp.dot(tri, lw, precision=HIGHEST, preferred_element_type=F32)
        mid = 0.5 * jnp.sum(lw, axis=0, keepdims=True)
        e_neg = jnp.exp(mid - l_incl)
        e_mid.append(jnp.exp(mid))
        rt.append(r_ref[0, rows, :].astype(F32) * jnp.exp(l_incl - mid))
        kt.append(kk_ref[0, rows, :].astype(F32) * jnp.exp(l_incl - lw - mid))
        kh.append(km_ref[0, 0, rows, :].astype(F32) * e_neg)
        bh.append(b_ref[0, 0, rows, :].astype(F32) * e_neg)
        v32.append(v_ref[0, rows, :].astype(F32))

    ri = lax.broadcasted_iota(jnp.int32, (w, w), 0)
    ci = lax.broadcasted_iota(jnp.int32, (w, w), 1)
    same = (ri // c) == (ci // c)
    eye = jnp.where(ri == ci, 1.0, 0.0).astype(F32)
    tl_ = lax.broadcasted_iota(jnp.int32, (c, w), 0)
    jl_ = lax.broadcasted_iota(jnp.int32, (c, w), 1) % c
    strict = (jl_ - tl_) * sign < 0
    incl = (jl_ - tl_) * sign <= 0
    eye2 = jnp.where(jl_ == tl_, 1.0, 0.0).astype(F32)
    lane = lax.broadcasted_iota(jnp.int32, (1, w), 1)
    h0 = lane < RWKV_HEAD

    def rows2(x):
        return jnp.concatenate([jnp.where(h0, x, 0.0), jnp.where(h0, 0.0, x)], axis=0)

    npair = st_ref.shape[0]
    items = [(s, slice(p * w, (p + 1) * w)) for s in range(nsub) for p in range(npair)]
    n = range(len(items))
    em = [e_mid[s][:, sl] for s, sl in items]
    g = [_dot_nt(jnp.concatenate([kt[s][:, sl], rt[s][:, sl]], axis=0),
                 jnp.concatenate([rows2(kh[s][:, sl]), rows2(bh[s][:, sl])], axis=0)) for s, sl in items]
    a_kk = [jnp.where(strict, x[:c, :w], 0.0) for x in g]
    a_rk = [jnp.where(incl, x[c:, :w], 0.0) for x in g]
    a_rb = [jnp.where(incl, x[c:, w:], 0.0) for x in g]
    vi = [v32[s][:, sl] for s, sl in items]
    v_rows = [rows2(x) for x in vi]
    av = [_dot(jnp.concatenate([a_kk[i], a_rk[i]], axis=0), v_rows[i]) for i in n]
    r_pre = [x[:c] for x in av]
    ark_v = [x[c:] for x in av]
    m = [jnp.where(strict, -x[:c, w:], 0.0) for x in g]
    tinv = [eye2 + x for x in m]
    m = [_dot(x, rows2(x)) for x in m]
    for _ in range(c.bit_length() - 3):
        both = [_dot(jnp.concatenate([tinv[i], m[i]], axis=0), rows2(m[i])) for i in n]
        tinv = [tinv[i] + both[i][:c] for i in n]
        m = [x[c:] for x in both]
    tinv = [tinv[i] + _dot(tinv[i], rows2(m[i])) for i in n]
    sol = [_dot(tinv[i], jnp.concatenate([rows2(r_pre[i]), rows2(kt[s][:, sl] * em[i])], axis=1))
           for i, (s, sl) in enumerate(items)]
    u_pre = [x[:, :w] for x in sol]
    kq = [x[:, w:] for x in sol]
    arb = [_dot(a_rb[i], jnp.concatenate([rows2(u_pre[i]), rows2(kq[i])], axis=1)) for i in n]
    y_pre = [ark_v[i] - arb[i][:, :w] for i in n]
    r_eff = [rt[s][:, sl] * em[i] - arb[i][:, w:] for i, (s, sl) in enumerate(items)]
    bbar = [bh[s][:, sl] * em[i] for i, (s, sl) in enumerate(items)]
    kbar = [kh[s][:, sl] * em[i] for i, (s, sl) in enumerate(items)]
    mmat = [eye * (em[i] * em[i]) - jnp.where(same, _dot_tn(kq[i], bbar[i]), 0.0) for i in n]
    s_pre = [jnp.where(same, _dot_tn(jnp.concatenate([vi[i], -u_pre[i]], axis=0),
                                     jnp.concatenate([kbar[i], bbar[i]], axis=0)), 0.0) for i in n]
    st = [st_ref[p] for p in range(npair)]
    for i, (s, sl) in enumerate(items):
        p = i % npair
        y_ref[0, 0, subs[s], sl] = (_dot_nt(r_eff[i], st[p]) + y_pre[i]).astype(y_ref.dtype)
        hi = st[p].astype(BF16)
        lo = (st[p] - hi.astype(F32)).astype(BF16)
        mb = mmat[i].astype(BF16)
        st[p] = (jnp.dot(hi, mb, preferred_element_type=F32) + jnp.dot(lo, mb, preferred_element_type=F32)
                 + s_pre[i])
    for p in range(npair):
        st_ref[p] = st[p]


def _wkv(r, v, kk, km, bv, lw, lc, dep=None):
    b, l, d = r.shape
    c = WKV_CHUNK * WKV_CHUNKS_PER_STEP
    ncc = lc // c
    nlc = (l - lc) // c

    def chunk(dr, i):
        return jnp.where(dr == 0, i, jnp.where(i < ncc, ncc - 1 - i, nlc + 2 * ncc - 1 - i))

    shared = pl.BlockSpec((1, c, d), lambda dr, bi, i: (bi, chunk(dr, i), 0))
    per_dir = pl.BlockSpec((1, 1, c, d), lambda dr, bi, i: (dr, bi, chunk(dr, i), 0))
    kern, in_specs, args = _after(dep, _wkv_kernel, [shared, shared, shared, per_dir, per_dir, per_dir],
                                  [r, v, kk, km, bv, lw])
    return pl.pallas_call(
        kern,
        out_shape=jax.ShapeDtypeStruct((2, b, l, d), BF16),
        grid=(2, b, l // c),
        in_specs=in_specs,
        out_specs=per_dir,
        scratch_shapes=[pltpu.VMEM((d // WKV_PAIR, WKV_PAIR, WKV_PAIR), F32)],
        compiler_params=pltpu.CompilerParams(dimension_semantics=("parallel", "parallel", "arbitrary"),
                                             vmem_limit_bytes=_vmem_limit(32)),
        name="wkv7_chunked",
    )(*args)


def _rwkv_out_kernel(y_ref, bonus_ref, g_ref, lnw_ref, lnb_ref, wo_ref, bd_ref, h_ref, mods_ref, gffn_ref,
                     wrt_ref, bias_ref, hn_ref, n2_ref, eid_ref, rank_ref, w_ref, cnt_ref, run_ref):
    y = y_ref[0, 0].astype(F32) + y_ref[1, 0].astype(F32)
    bd = bd_ref[...]
    mean = _head_sum(y, bd) * (1.0 / RWKV_HEAD)
    yc = y - mean
    var = _head_sum(yc * yc, bd) * (1.0 / RWKV_HEAD)
    yn = yc * lax.rsqrt(var + GN_EPS) * lnw_ref[...] + lnb_ref[...]
    out = (yn + bonus_ref[0].astype(F32)) * g_ref[0].astype(F32)
    _mixer_tail(_dot(out, wo_ref[...]), h_ref[0], mods_ref[0, 0], gffn_ref, wrt_ref, bias_ref, hn_ref, n2_ref,
                eid_ref, rank_ref, w_ref, cnt_ref, run_ref)


def _rwkv_out(y, bonus, g, lnw, lnb, wo, bd, h, mods, gffn, wrt, bias, nct, dep=None):
    b, l, d = h.shape
    tl = TOKEN_TILE
    tok = lambda w: pl.BlockSpec((1, tl, w), lambda i, j: (i, j, 0))
    full = lambda x: pl.BlockSpec(x.shape, lambda i, j: (0,) * x.ndim)
    shapes, specs = _tail_outs(b, l, d)
    kern, in_specs, args = _after(
        dep, _rwkv_out_kernel,
        [pl.BlockSpec((2, 1, tl, d), lambda i, j: (0, i, j, 0)), tok(d), tok(d),
         full(lnw), full(lnb), full(wo), full(bd), tok(d),
         pl.BlockSpec((1, 1, N_MODS, d), lambda i, j: (i, jnp.where(j < nct, 0, 1), 0, 0)),
         full(gffn), full(wrt), full(bias)],
        [y, bonus, g, lnw, lnb, wo, bd, h, mods, gffn, wrt, bias])
    return pl.pallas_call(
        kern,
        out_shape=shapes,
        grid=(b, l // tl),
        in_specs=in_specs,
        out_specs=specs,
        scratch_shapes=[pltpu.VMEM((N_EXPERTS, 1), F32)],
        compiler_params=pltpu.CompilerParams(dimension_semantics=("arbitrary", "arbitrary"),
                                             vmem_limit_bytes=_vmem_limit(40)),
        name="rwkv_out",
    )(*args)


def _rope_table(n_lat, n_ctx):
    dim = SWA_HEAD_DIM
    nf = dim // 4
    inv = ROPE_THETA ** (-jnp.arange(nf, dtype=F32) / nf)
    row = jnp.repeat(jnp.arange(n_lat // GRID_W, dtype=F32), GRID_W)
    col = jnp.tile(jnp.arange(GRID_W, dtype=F32), n_lat // GRID_W)
    ar = row[:, None] * inv
    ac = col[:, None] * inv
    ang = jnp.concatenate([ar, ar, ac, ac], axis=-1)
    cos = jnp.concatenate([jnp.ones((n_ctx, dim), F32), jnp.cos(ang)], axis=0)
    sin = jnp.concatenate([jnp.zeros((n_ctx, dim), F32), jnp.sin(ang)], axis=0)
    return jnp.tile(cos, (1, 2)), jnp.tile(sin, (1, 2))


def _layout_attn_weights(w_in, w_uq, w_ukv):
    d = w_in.shape[0]
    s0 = MLA_Q_RANK
    s1 = s0 + MLA_KV_RANK
    s2 = s1 + MLA_ROPE
    s3 = s2 + SWA_HEADS * SWA_HEAD_DIM
    s4 = s3 + SWA_KV_HEADS * SWA_HEAD_DIM
    rep = lambda w: jnp.concatenate(
        [jnp.tile(w[:, g * SWA_HEAD_DIM:(g + 1) * SWA_HEAD_DIM], (1, SWA_GROUP)) for g in range(SWA_KV_HEADS)], axis=1)
    win = jnp.concatenate([w_in[:, :s1], w_in[:, s2:s3], rep(w_in[:, s3:s4]), rep(w_in[:, s4:]),
                           w_in[:, s1:s2], jnp.zeros((d, V7X_LANES - MLA_ROPE), w_in.dtype)], axis=1)
    qh = MLA_NOPE + MLA_ROPE
    pad = jnp.zeros((w_uq.shape[0], V7X_MXU_DIM - qh), w_uq.dtype)
    wuq = jnp.concatenate([jnp.concatenate([w_uq[:, h * qh:(h + 1) * qh], pad], axis=1) for h in range(MLA_HEADS)], axis=1)
    kvh = MLA_NOPE + MLA_V
    wuk = jnp.concatenate([w_ukv[:, h * kvh:h * kvh + MLA_NOPE] for h in range(MLA_HEADS)], axis=1)
    wuvt = jnp.concatenate([w_ukv[:, h * kvh + MLA_NOPE:(h + 1) * kvh] for h in range(MLA_HEADS)], axis=1).T
    return win.astype(BF16), wuq.astype(BF16), wuk.astype(BF16), wuvt.astype(BF16)


def _lora_pair(w_down, w_up):
    rank = w_down.shape[2]
    down = jnp.concatenate([w_down[0], w_down[1]], axis=1)
    z = jnp.zeros((rank, w_up.shape[2]), w_up.dtype)
    up = jnp.stack([jnp.concatenate([w_up[0], z], axis=0), jnp.concatenate([z, w_up[1]], axis=0)], axis=0)
    return down.astype(BF16), up.astype(BF16)


def _head_block_diag():
    i = jnp.arange(V7X_MXU_DIM) // RWKV_HEAD
    return (i[:, None] == i[None, :]).astype(BF16)


def kernel(x, c, ctx, c_ctx, ada_w, ada_b, norm_mix, norm_ffn, norm_final, attn_w_in, attn_q_norm, attn_kv_norm, attn_w_uq, attn_w_ukv, attn_sinks, attn_w_o, rwkv_mu, rwkv_w_r, rwkv_w_k, rwkv_w_v, rwkv_w_o, rwkv_g1, rwkv_g2, rwkv_w0, rwkv_w1, rwkv_w2, rwkv_a0, rwkv_a1, rwkv_a2, rwkv_k_k, rwkv_k_a, rwkv_r_k, rwkv_ln_w, rwkv_ln_b, moe_router, moe_bias, moe_w_gate, moe_w_up, moe_w_down, moe_ws_gate, moe_ws_up, moe_ws_down):
    bsz, s, d = x.shape
    lc = ctx.shape[1]
    l = lc + s
    depth = ada_w.shape[0]
    nct = lc // TOKEN_TILE
    assert lc % TOKEN_TILE == 0 and s % TOKEN_TILE == 0 and s >= SWA_BAND and lc % SWA_Q_TILE == 0
    assert lc % (WKV_CHUNK * WKV_CHUNKS_PER_STEP) == 0
    assert d % V7X_MXU_DIM == 0 and WKV_CHUNK * 2 == V7X_LANES
    ngrp = SAMPLE_GROUPS
    bg = bsz // ngrp
    assert bsz % ngrp == 0 and (bg * l) % (8 * V7X_SC_WORKERS) == 0

    assert ngrp == 2
    cos, sin = _rope_table(s, lc)
    bd = _head_block_diag()
    rows = -(-(bsz + 1) // 8) * 8
    cc = jnp.concatenate([c, c_ctx[None, :], jnp.zeros((rows - bsz - 1, d), F32)], axis=0)
    row2 = lambda a: a.reshape(1, -1)
    moe_w = (moe_w_gate, moe_w_up, moe_w_down)
    moe_ws = (moe_ws_gate, moe_ws_up, moe_ws_down)

    shared = {}

    def layer_weights(li):
        if li not in shared:
            i = li // 2
            ada = _ada_mods(cc, ada_w, ada_b, li)
            w = dict(
                mods=jnp.stack([jnp.broadcast_to(ada[bsz].reshape(1, N_MODS, d), (bsz, N_MODS, d)),
                                ada[:bsz].reshape(bsz, N_MODS, d)], axis=1),
                wrt=jnp.concatenate([moe_router[li].T, jnp.zeros((GATE_W - N_EXPERTS, d), F32)], axis=0),
                bias=moe_bias[li].reshape(N_GROUPS, GROUP_SIZE, 1))
            if li % 2 == 0:
                w["win"], w["wuq"], w["wuk"], w["wuvt"] = _layout_attn_weights(attn_w_in[i], attn_w_uq[i], attn_w_ukv[i])
                w["wo"] = attn_w_o[i].astype(BF16)
            else:
                w["w1"], w["w2"] = _lora_pair(rwkv_w1[i], rwkv_w2[i])
                w["a1"], w["a2"] = _lora_pair(rwkv_a1[i], rwkv_a2[i])
                w["wr"], w["wk"], w["wv"], w["wo"] = [x[i].astype(BF16) for x in (rwkv_w_r, rwkv_w_k, rwkv_w_v, rwkv_w_o)]
                w["g1"], w["g2"] = rwkv_g1[i].astype(BF16), rwkv_g2[i].astype(BF16)
            shared[li] = w
        return shared[li]

    groups = [dict(stream=(ctx, x, g * bg, 0), b0=g * bg) for g in range(ngrp)]
    result = [None]

    def run_stage(st, li, name, dep):
        w = layer_weights(li)
        i = li // 2
        with_ctx = li < depth - 1
        mods = w["mods"][st["b0"]:st["b0"] + bg]
        if name == "proj" and li % 2 == 0:
            st["qkv"] = _attn_proj(st["stream"], bg, l, mods, row2(norm_mix[li]), w["win"], row2(attn_q_norm[i]),
                                   row2(attn_kv_norm[i]), w["wuq"], w["wuk"], w["wuvt"], cos, sin, nct, dep=dep)
            return st["qkv"][0]
        if name == "mid" and li % 2 == 0:
            q, k, vt, qs, ks, vs = st.pop("qkv")
            st["a"] = _mla_attention(q, k, vt, lc, 0 if with_ctx else lc // MLA_Q_TILE, dep=dep)
            st["bm"] = _swa_attention(attn_sinks[i], qs, ks, vs, lc, 0 if with_ctx else lc // SWA_Q_TILE, dep=st["a"])
            return st["bm"]
        if name == "proj":
            assert st["stream"][0] is st["stream"][1]
            st["feat"] = _rwkv_proj(st["stream"][0], mods, row2(norm_mix[li]), rwkv_mu[i], w["wr"], w["wk"], w["wv"],
                                    w["g1"], w["g2"], w["w1"], w["w2"], w["a1"], w["a2"], rwkv_w0[i], rwkv_a0[i],
                                    row2(rwkv_k_k[i]), row2(rwkv_k_a[i]), row2(rwkv_r_k[i]), bd, nct, dep=dep)
            return st["feat"][0]
        if name == "mid":
            r, v, kk, gt, km, bv, lw, bonus = st.pop("feat")
            st["y"] = _wkv(r, v, kk, km, bv, lw, lc, dep=dep)
            st["gate"], st["bonus"] = gt, bonus
            return st["y"]
        if name == "out":
            if li % 2 == 0:
                tail = _attn_out(st.pop("a"), st.pop("bm"), st["stream"], mods, w["wo"], row2(norm_ffn[li]),
                                 w["wrt"], w["bias"], nct, dep=dep)
            else:
                tail = _rwkv_out(st.pop("y"), st.pop("bonus"), st.pop("gate"), row2(rwkv_ln_w[i]), row2(rwkv_ln_b[i]),
                                 w["wo"], bd, st["stream"][0], mods, row2(norm_ffn[li]), w["wrt"], w["bias"], nct, dep=dep)
            st["h"], st["n2p"], eid, rank, st["wcols"], counts = tail
            st["xs"], st["dest"], st["tile_expert"], st["n_valid"] = _moe_route_rows(st["n2p"], eid, rank, counts, bg, l)
            return st["h"]
        if name == "experts":
            ys = _moe_experts(st.pop("tile_expert"), st.pop("n_valid"), st.pop("xs"), *moe_w, li, dep=dep)
            st["yg"] = _sc_gather(ys, st.pop("dest"), bg * l).reshape(TOP_K, bg, l, d // 2)
            return ys
        assert name == "combine"
        last = li == depth - 1
        h = _moe_combine(st.pop("yg"), st.pop("wcols"), st.pop("n2p"), *moe_ws, st.pop("h"), mods, row2(norm_final),
                         nct, li, result[0] if last else None, st["b0"] if last else 0, bsz if last else bg,
                         last, last, dep=dep)
        if last:
            result[0] = h
        else:
            st["stream"] = (h, h, 0, nct)
        return h

    order = [(0, 0, "proj"), (0, 0, "mid")]
    for li in range(depth):
        order += [(0, li, "out"), (1, li, "proj"), (0, li, "experts"), (1, li, "mid")]
        if li < depth - 1:
            order += [(0, li, "combine"), (1, li, "out"), (0, li + 1, "proj"), (1, li, "experts"),
                      (0, li + 1, "mid"), (1, li, "combine")]
        else:
            order += [(1, li, "out"), (0, li, "combine"), (1, li, "experts"), (1, li, "combine")]
    dep = None
    for g, li, name in order:
        dep = run_stage(groups[g], li, name, dep)
    return result[0]
```

```python
import functools

import jax
import jax.numpy as jnp
from jax import lax
from jax.experimental import pallas as pl
from jax.experimental.pallas import tpu as pltpu
from jax.experimental.pallas import tpu_sc as plsc

F32 = jnp.float32
BF16 = jnp.bfloat16
HIGHEST = lax.Precision.HIGHEST

GRID_W = 64
NORM_EPS = 1e-6
ROPE_THETA = 10000.0
NEG_INF = -1e30
N_MODS = 6

MLA_HEADS = 4
MLA_Q_RANK = 384
MLA_KV_RANK = 256
MLA_NOPE = 128
MLA_ROPE = 64
MLA_V = 128

SWA_HEADS = 8
SWA_KV_HEADS = 2
SWA_GROUP = SWA_HEADS // SWA_KV_HEADS
SWA_HEAD_DIM = 64
WINDOW = 128

RWKV_HEAD = 64
DECAY_LORA = 64
ICLR_LORA = 64
GATE_LORA = 128
GN_EPS = 64e-5

N_EXPERTS = 64
TOP_K = 6
N_GROUPS = 8
TOPK_GROUPS = 4
GROUP_SIZE = N_EXPERTS // N_GROUPS
ROUTED_SCALE = 2.5
GATE_W = 128

V7X_LANES = 128
V7X_MXU_DIM = 256
V7X_VMEM_BYTES = 64 * 1024 * 1024
V7X_SC_CORES = 2
V7X_SC_SUBCORES = 16
V7X_SC_WORKERS = V7X_SC_CORES * V7X_SC_SUBCORES

TOKEN_TILE = 256
MLA_Q_TILE = 256
MLA_HEADS_PER_STEP = 2
SWA_Q_TILE = 256
SWA_BAND = SWA_Q_TILE + 2 * WINDOW
WKV_CHUNK = 64
WKV_PAIR = 2 * RWKV_HEAD
WKV_CHUNKS_PER_STEP = 4
MOE_ROW_TILE = 512
MOE_ROW_SLOTS = 3
SAMPLE_GROUPS = 2
SC_MAX_CHUNK = 64

LOG2E = 1.4426950408889634
MIB = 1024 * 1024
VMEM_RESERVE_BYTES = 4 * MIB


def _vmem_limit(mib):
    return min(mib * MIB, V7X_VMEM_BYTES - VMEM_RESERVE_BYTES)


def _dot(a, b):
    return jnp.dot(a.astype(BF16), b.astype(BF16), preferred_element_type=F32)


def _dot_nt(a, b):
    return lax.dot_general(a.astype(BF16), b.astype(BF16), (((1,), (1,)), ((), ())),
                           preferred_element_type=F32)


def _dot_tn(a, b):
    return lax.dot_general(a.astype(BF16), b.astype(BF16), (((0,), (0,)), ((), ())),
                           preferred_element_type=F32)


def _sigmoid(x):
    return 1.0 / (1.0 + jnp.exp(-x))


def _silu(x):
    return x * _sigmoid(x)


def _rms(x, g):
    return x * lax.rsqrt(jnp.mean(x * x, axis=-1, keepdims=True) + NORM_EPS) * g


def _norm_mod(x, g, shift, scale):
    return _rms(x, g) * (1.0 + scale) + shift


def _split_dot(x, w):
    hi = x.astype(BF16)
    lo = (x - hi.astype(F32)).astype(BF16)
    return (jnp.dot(hi, w, preferred_element_type=F32) + jnp.dot(lo, w, preferred_element_type=F32))


def _head_sum(x, bd):
    w = bd.shape[0]
    parts = [_split_dot(x[:, c * w:(c + 1) * w], bd) for c in range(x.shape[1] // w)]
    return jnp.concatenate(parts, axis=1)


def _after(dep, kernel, in_specs, args, n_lead=0):
    if dep is None:
        return kernel, list(in_specs), list(args)
    n_in = n_lead + len(in_specs)

    def ordered(*refs):
        return kernel(*refs[:n_in], *refs[n_in + 1:])

    return ordered, list(in_specs) + [pl.BlockSpec(memory_space=pl.ANY)], list(args) + [dep]


def _ada_kernel(c_ref, w_ref, b_ref, o_ref):
    s = _silu(c_ref[...])
    o_ref[...] = jnp.dot(s, w_ref[0], precision=HIGHEST, preferred_element_type=F32) + b_ref[0]


def _ada_mods(cc, w, b, layer):
    rows, d = cc.shape
    depth, _, n = w.shape
    return pl.pallas_call(
        _ada_kernel,
        out_shape=jax.ShapeDtypeStruct((rows, n), F32),
        grid=(n // d,),
        in_specs=[pl.BlockSpec((rows, d), lambda i: (0, 0)),
                  pl.BlockSpec((1, d, d), lambda i: (layer, 0, i)),
                  pl.BlockSpec((1, 1, d), lambda i: (layer, 0, i))],
        out_specs=pl.BlockSpec((rows, d), lambda i: (0, i)),
        compiler_params=pltpu.CompilerParams(dimension_semantics=("parallel",),
                                             vmem_limit_bytes=_vmem_limit(32)),
        name="ada_mods",
    )(cc, w, b.reshape(depth, 1, n))


def _rope128(x, cos, sin, first_half):
    rot = jnp.where(first_half, -pltpu.roll(x, V7X_LANES - 16, axis=1), pltpu.roll(x, 16, axis=1))
    return x * cos + rot * sin


_C_CQ = 0
_C_CKV = _C_CQ + MLA_Q_RANK
_C_QS = _C_CKV + MLA_KV_RANK
_C_KS = _C_QS + SWA_HEADS * SWA_HEAD_DIM
_C_VS = _C_KS + SWA_KV_HEADS * V7X_MXU_DIM
_C_KR = _C_VS + SWA_KV_HEADS * V7X_MXU_DIM
_C_END = _C_KR + V7X_LANES
_SWA_W = SWA_KV_HEADS * V7X_MXU_DIM
_MLA_QK_W = MLA_HEADS * V7X_MXU_DIM


def _stream_specs(stream, nct, tl):
    ctx_arr, lat_arr, b0, lat_off = stream
    d = ctx_arr.shape[2]
    return [pl.BlockSpec((1, tl, d), lambda i, j: (i + b0, jnp.minimum(j, nct - 1), 0)),
            pl.BlockSpec((1, tl, d), lambda i, j: (i + b0, jnp.maximum(j - nct, 0) + lat_off, 0))]


def _stream_tile(c_ref, x_ref, nct):
    rows = c_ref.shape[1]
    take_ctx = lax.broadcasted_iota(jnp.int32, (rows, 1), 0) < jnp.where(pl.program_id(1) < nct, rows, 0)
    return jnp.where(take_ctx, c_ref[0], x_ref[0])


def _attn_proj_kernel(c_ref, x_ref, mods_ref, g_ref, win_ref, qn_ref, kvn_ref, wuq_ref, wuk_ref, wuvt_ref, cos_ref,
                      sin_ref, q_ref, k_ref, vt_ref, qs_ref, ks_ref, vs_ref, *, nct):
    m = mods_ref[0, 0]
    n = _norm_mod(_stream_tile(c_ref, x_ref, nct), g_ref[...], m[0:1], m[1:2])
    u = _dot(n, win_ref[...])
    cos = cos_ref[...]
    sin = sin_ref[...]
    lane = lax.broadcasted_iota(jnp.int32, (1, V7X_LANES), 1)
    first_half = (lane % 32) < 16

    def rope(x):
        return _rope128(x, cos, sin, first_half)

    scale_a = (MLA_NOPE + MLA_ROPE) ** -0.5 * LOG2E
    scale_b = SWA_HEAD_DIM ** -0.5 * LOG2E
    q = _dot(_rms(u[:, _C_CQ:_C_CKV], qn_ref[...]), wuq_ref[...])
    ckv = _rms(u[:, _C_CKV:_C_QS], kvn_ref[...])
    kn = _dot(ckv, wuk_ref[...])
    vt_ref[0] = _dot_nt(wuvt_ref[...], ckv).astype(BF16)
    kr = rope(u[:, _C_KR:_C_END]).astype(BF16)
    for h in range(MLA_HEADS):
        o = h * V7X_MXU_DIM
        q_ref[0, :, o:o + V7X_LANES] = (q[:, o:o + V7X_LANES] * scale_a).astype(BF16)
        q_ref[0, :, o + V7X_LANES:o + V7X_MXU_DIM] = (rope(q[:, o + V7X_LANES:o + V7X_MXU_DIM]) * scale_a).astype(BF16)
        k_ref[0, :, o:o + V7X_LANES] = kn[:, h * MLA_NOPE:(h + 1) * MLA_NOPE].astype(BF16)
        k_ref[0, :, o + V7X_LANES:o + V7X_MXU_DIM] = kr
    for c in range((_C_KS - _C_QS) // V7X_LANES):
        o = c * V7X_LANES
        qs_ref[0, :, o:o + V7X_LANES] = (rope(u[:, _C_QS + o:_C_QS + o + V7X_LANES]) * scale_b).astype(BF16)
    for c in range(_SWA_W // V7X_LANES):
        o = c * V7X_LANES
        ks_ref[0, :, o:o + V7X_LANES] = rope(u[:, _C_KS + o:_C_KS + o + V7X_LANES]).astype(BF16)
    vs_ref[0] = u[:, _C_VS:_C_KR].astype(BF16)


def _attn_proj(stream, b, l, mods, g, win, qn, kvn, wuq, wuk, wuvt, cos, sin, nct, dep=None):
    d = stream[0].shape[2]
    tl = TOKEN_TILE
    tok = lambda w: pl.BlockSpec((1, tl, w), lambda i, j: (i, j, 0))
    full = lambda a: pl.BlockSpec(a.shape, lambda i, j: (0,) * a.ndim)
    sds = jax.ShapeDtypeStruct
    dv = MLA_HEADS * MLA_V
    kern, in_specs, args = _after(
        dep, functools.partial(_attn_proj_kernel, nct=nct),
        _stream_specs(stream, nct, tl) + [
            pl.BlockSpec((1, 1, N_MODS, d), lambda i, j: (i, jnp.where(j < nct, 0, 1), 0, 0)),
            full(g), full(win), full(qn), full(kvn), full(wuq), full(wuk), full(wuvt),
            pl.BlockSpec((tl, V7X_LANES), lambda i, j: (j, 0)),
            pl.BlockSpec((tl, V7X_LANES), lambda i, j: (j, 0))],
        [stream[0], stream[1], mods, g, win, qn, kvn, wuq, wuk, wuvt, cos, sin])
    return pl.pallas_call(
        kern,
        out_shape=[sds((b, l, _MLA_QK_W), BF16), sds((b, l, _MLA_QK_W), BF16), sds((b, dv, l), BF16),
                   sds((b, l, SWA_HEADS * SWA_HEAD_DIM), BF16), sds((b, l, _SWA_W), BF16), sds((b, l, _SWA_W), BF16)],
        grid=(b, l // tl),
        in_specs=in_specs,
        out_specs=[tok(_MLA_QK_W), tok(_MLA_QK_W), pl.BlockSpec((1, dv, tl), lambda i, j: (i, 0, j)),
                   tok(SWA_HEADS * SWA_HEAD_DIM), tok(_SWA_W), tok(_SWA_W)],
        compiler_params=pltpu.CompilerParams(dimension_semantics=("parallel", "parallel"),
                                             vmem_limit_bytes=_vmem_limit(48)),
        name="attn_proj",
    )(*args)


def _mla_kernel(q_ref, k_ref, vt_ref, o_ref, *, nct_q, lc):
    hw = V7X_MXU_DIM

    def attend(nk):
        st = [_dot_nt(k_ref[0, 0:nk, hh * hw:(hh + 1) * hw], q_ref[0, :, hh * hw:(hh + 1) * hw])
              for hh in range(MLA_HEADS_PER_STEP)]
        for hh, s in enumerate(st):
            p = jnp.exp2(s - jnp.max(s, axis=0, keepdims=True))
            den = jnp.sum(p, axis=0, keepdims=True)
            ot = _dot(vt_ref[0, hh * MLA_V:(hh + 1) * MLA_V, 0:nk], p) / den
            o_ref[0, :, hh * MLA_V:(hh + 1) * MLA_V] = ot.T.astype(o_ref.dtype)

    @pl.when(pl.program_id(2) < nct_q)
    def _():
        attend(lc)

    @pl.when(pl.program_id(2) >= nct_q)
    def _():
        attend(k_ref.shape[1])


def _mla_attention(q, k, vt, lc, q_tile0, dep=None):
    b, l, _ = q.shape
    tq = MLA_Q_TILE
    hps = MLA_HEADS_PER_STEP
    kern, in_specs, args = _after(
        dep, functools.partial(_mla_kernel, nct_q=lc // tq - q_tile0, lc=lc),
        [pl.BlockSpec((1, tq, hps * V7X_MXU_DIM), lambda i, h, j: (i, j + q_tile0, h)),
         pl.BlockSpec((1, l, hps * V7X_MXU_DIM), lambda i, h, j: (i, 0, h)),
         pl.BlockSpec((1, hps * MLA_V, l), lambda i, h, j: (i, h, 0))],
        [q, k, vt])
    return pl.pallas_call(
        kern,
        out_shape=jax.ShapeDtypeStruct((b, l, MLA_HEADS * MLA_V), BF16),
        grid=(b, MLA_HEADS // hps, l // tq - q_tile0),
        in_specs=in_specs,
        out_specs=pl.BlockSpec((1, tq, hps * MLA_V), lambda i, h, j: (i, j + q_tile0, h)),
        compiler_params=pltpu.CompilerParams(dimension_semantics=("parallel", "parallel", "parallel"),
                                             vmem_limit_bytes=_vmem_limit(48)),
        name="mla_attention",
    )(*args)


def _swa_kernel(sink_ref, q_ref, k_ref, v_ref, o_ref, *, lc, q_tile0):
    tq = SWA_Q_TILE
    l = k_ref.shape[1]
    r0 = (pl.program_id(1) + q_tile0) * tq
    start = pl.multiple_of(jnp.clip(r0 - WINDOW, lc, l - SWA_BAND), WINDOW)
    rows = SWA_GROUP * tq
    row = lax.broadcasted_iota(jnp.int32, (rows, 1), 0)
    qpos = jnp.where(r0 >= lc, r0, -l) + row % tq
    kpos = start + lax.broadcasted_iota(jnp.int32, (1, SWA_BAND), 1)
    valid = jnp.abs(qpos - kpos) <= WINDOW
    lane = lax.broadcasted_iota(jnp.int32, (1, V7X_MXU_DIM), 1)
    head = [(lane // SWA_HEAD_DIM) == hh for hh in range(SWA_GROUP)]
    groups = range(SWA_KV_HEADS)
    sls = [slice(g * V7X_MXU_DIM, (g + 1) * V7X_MXU_DIM) for g in groups]
    qstack = []
    for sl in sls:
        qg = q_ref[0, :, sl]
        zero = jnp.zeros_like(qg)
        qstack.append(jnp.concatenate([jnp.where(head[hh], qg, zero) for hh in range(SWA_GROUP)], axis=0))
    sc = [_dot_nt(qstack[g], k_ref[0, 0:lc, sls[g]]) for g in groups]
    sb = [_dot_nt(qstack[g], k_ref[0, pl.ds(start, SWA_BAND), sls[g]]) for g in groups]
    for g in groups:
        sl = sls[g]
        sbm = jnp.where(valid, sb[g], NEG_INF)
        sk = jnp.zeros((rows, 1), F32)
        for hh in range(SWA_GROUP):
            sk = jnp.where(row // tq == hh, sink_ref[g * SWA_GROUP + hh] * LOG2E, sk)
        mx = jnp.maximum(jnp.maximum(jnp.max(sc[g], axis=-1, keepdims=True), jnp.max(sbm, axis=-1, keepdims=True)), sk)
        pc = jnp.exp2(sc[g] - mx)
        pb = jnp.exp2(sbm - mx)
        den = jnp.sum(pc, axis=-1, keepdims=True) + jnp.sum(pb, axis=-1, keepdims=True) + jnp.exp2(sk - mx)
        ostack = (_dot(pc, v_ref[0, 0:lc, sl]) + _dot(pb, v_ref[0, pl.ds(start, SWA_BAND), sl])) / den
        o = jnp.zeros((tq, V7X_MXU_DIM), F32)
        for hh in range(SWA_GROUP):
            o = o + jnp.where(head[hh], ostack[hh * tq:(hh + 1) * tq], 0.0)
        o_ref[0, :, sl] = o.astype(o_ref.dtype)


def _swa_attention(sinks, q, k, v, lc, q_tile0, dep=None):
    b, l, _ = q.shape
    tq = SWA_Q_TILE
    kern, in_specs, args = _after(
        dep, functools.partial(_swa_kernel, lc=lc, q_tile0=q_tile0),
        [pl.BlockSpec(memory_space=pltpu.SMEM),
         pl.BlockSpec((1, tq, SWA_HEADS * SWA_HEAD_DIM), lambda i, j: (i, j + q_tile0, 0)),
         pl.BlockSpec((1, l, _SWA_W), lambda i, j: (i, 0, 0)),
         pl.BlockSpec((1, l, _SWA_W), lambda i, j: (i, 0, 0))],
        [sinks, q, k, v])
    return pl.pallas_call(
        kern,
        out_shape=jax.ShapeDtypeStruct((b, l, SWA_HEADS * SWA_HEAD_DIM), BF16),
        grid=(b, l // tq - q_tile0),
        in_specs=in_specs,
        out_specs=pl.BlockSpec((1, tq, SWA_HEADS * SWA_HEAD_DIM), lambda i, j: (i, j + q_tile0, 0)),
        compiler_params=pltpu.CompilerParams(dimension_semantics=("parallel", "parallel"),
                                             vmem_limit_bytes=_vmem_limit(48)),
        name="swa_attention",
    )(*args)


def _pack_bf16_pair(x):
    w = x.shape[1] // 2
    lo = pltpu.bitcast(x[:, :w].astype(BF16).astype(F32), jnp.int32)
    hi = pltpu.bitcast(x[:, w:].astype(BF16).astype(F32), jnp.int32)
    return lax.shift_right_logical(lo, jnp.int32(16)) | (hi & jnp.int32(-65536))


def _unpack_bf16_pair(p):
    return pltpu.bitcast(p << 16, F32), pltpu.bitcast(p & jnp.int32(-65536), F32)


def _route(n2, wrt, bias, run_ref):
    n_hi = n2.astype(BF16)
    n_lo = (n2 - n_hi.astype(F32)).astype(BF16)
    w_hi = wrt.astype(BF16)
    w_lo = (wrt - w_hi.astype(F32)).astype(BF16)
    logits = _dot_nt(w_hi, n_hi) + (_dot_nt(w_hi, n_lo) + _dot_nt(w_lo, n_hi))
    rows = logits.shape[1]
    scores = _sigmoid(logits[0:N_EXPERTS])

    def select(sc2):
        cols = sc2.shape[1]
        shape3 = (N_GROUPS, GROUP_SIZE, cols)
        choice = sc2.reshape(shape3) + bias
        ji = lax.broadcasted_iota(jnp.int32, shape3, 1).astype(F32)
        m1 = jnp.max(choice, axis=1, keepdims=True)
        first = jnp.min(jnp.where(choice == m1, ji, float(GROUP_SIZE)), axis=1, keepdims=True)
        m2 = jnp.max(jnp.where(ji == first, -jnp.inf, choice), axis=1, keepdims=True)
        gs = m1 + m2
        gidx = lax.broadcasted_iota(jnp.int32, gs.shape, 0).astype(F32)
        gsel = jnp.zeros_like(gs)
        for _ in range(TOPK_GROUPS):
            mx = jnp.max(gs, axis=0, keepdims=True)
            pick = gidx == jnp.min(jnp.where(gs == mx, gidx, float(N_GROUPS)), axis=0, keepdims=True)
            gsel = jnp.where(pick, 1.0, gsel)
            gs = jnp.where(pick, -jnp.inf, gs)
        cand = jnp.where(gsel > 0.0, choice, -jnp.inf).reshape(N_EXPERTS, cols)
        eidx = lax.broadcasted_iota(jnp.int32, (N_EXPERTS, cols), 0).astype(F32)
        out = []
        for _ in range(TOP_K):
            mx = jnp.max(cand, axis=0, keepdims=True)
            pick = eidx == jnp.min(jnp.where(cand == mx, eidx, float(N_EXPERTS)), axis=0, keepdims=True)
            out.append(jnp.where(pick, 1.0, 0.0))
            cand = jnp.where(pick, -jnp.inf, cand)
        return out

    blocks = [select(scores[:, o:o + V7X_LANES]) for o in range(0, rows, V7X_LANES)]
    picks = [jnp.concatenate([blk[k] for blk in blocks], axis=1) > 0.0 for k in range(TOP_K)]
    ei = lax.broadcasted_iota(jnp.int32, (N_EXPERTS, rows), 0).astype(F32)
    esel = jnp.zeros((N_EXPERTS, rows), F32)
    for pick in picks:
        esel = jnp.where(pick, 1.0, esel)
    before = jnp.where(lax.broadcasted_iota(jnp.int32, (rows, rows), 0) < lax.broadcasted_iota(jnp.int32, (rows, rows), 1),
                       1.0, 0.0).astype(BF16)
    slot = jnp.dot(esel.astype(BF16), before, preferred_element_type=F32) + run_ref[...]
    run_ref[...] += jnp.sum(esel, axis=1, keepdims=True)
    sc = [jnp.sum(jnp.where(pick, scores, 0.0), axis=0, keepdims=True) for pick in picks]
    tot = sc[0]
    for x in sc[1:]:
        tot = tot + x
    k8 = lax.broadcasted_iota(jnp.int32, (8, rows), 0)
    kw = lax.broadcasted_iota(jnp.int32, (GATE_W, rows), 0)
    eid = jnp.zeros((8, rows), jnp.int32)
    rank = jnp.zeros((8, rows), jnp.int32)
    wk = jnp.zeros((GATE_W, rows), F32)
    for k, pick in enumerate(picks):
        e_k = jnp.sum(jnp.where(pick, ei, 0.0), axis=0, keepdims=True).astype(jnp.int32)
        r_k = jnp.sum(jnp.where(pick, slot, 0.0), axis=0, keepdims=True).astype(jnp.int32)
        eid = jnp.where(k8 == k, e_k, eid)
        rank = jnp.where(k8 == k, r_k, rank)
        wk = jnp.where(kw == k, sc[k] * (ROUTED_SCALE / tot), wk)
    return eid, rank, wk.T


def _mixer_tail(o, h, m, gffn_ref, wrt_ref, bias_ref, hn_ref, n2_ref, eid_ref, rank_ref, w_ref, cnt_ref, run_ref):
    @pl.when((pl.program_id(0) == 0) & (pl.program_id(1) == 0))
    def _():
        run_ref[...] = jnp.zeros_like(run_ref)

    hn = h + m[2:3] * o
    hn_ref[0] = hn
    n2 = _norm_mod(hn, gffn_ref[...], m[3:4], m[4:5])
    n2_ref[0] = _pack_bf16_pair(n2)
    eid, rank, wcols = _route(n2, wrt_ref[...], bias_ref[...], run_ref)
    eid_ref[0] = eid
    rank_ref[0] = rank
    w_ref[0] = wcols
    cnt_ref[...] = run_ref[...]


def _attn_out_kernel(a_ref, b_ref, c_ref, x_ref, mods_ref, wo_ref, gffn_ref, wrt_ref, bias_ref,
                     hn_ref, n2_ref, eid_ref, rank_ref, w_ref, cnt_ref, run_ref, *, nct):
    wa = MLA_HEADS * MLA_V
    o = _dot(a_ref[0], wo_ref[0:wa, :]) + _dot(b_ref[0], wo_ref[wa:, :])
    _mixer_tail(o, _stream_tile(c_ref, x_ref, nct), mods_ref[0, 0], gffn_ref, wrt_ref, bias_ref, hn_ref, n2_ref,
                eid_ref, rank_ref, w_ref, cnt_ref, run_ref)


def _tail_outs(b, l, d):
    tl = TOKEN_TILE
    nt = l // tl
    sds = jax.ShapeDtypeStruct
    tok = lambda w: pl.BlockSpec((1, tl, w), lambda i, j: (i, j, 0))
    blk = pl.BlockSpec((1, 8, tl), lambda i, j: (i * nt + j, 0, 0))
    shapes = [sds((b, l, d), F32), sds((b, l, d // 2), jnp.int32), sds((b * nt, 8, tl), jnp.int32),
              sds((b * nt, 8, tl), jnp.int32), sds((b, l, GATE_W), F32), sds((N_EXPERTS, 1), F32)]
    specs = [tok(d), tok(d // 2), blk, blk, tok(GATE_W), pl.BlockSpec((N_EXPERTS, 1), lambda i, j: (0, 0))]
    return shapes, specs


def _attn_out(a, bm, stream, mods, wo, gffn, wrt, bias, nct, dep=None):
    b, l, _ = a.shape
    d = stream[0].shape[2]
    tl = TOKEN_TILE
    tok = lambda w: pl.BlockSpec((1, tl, w), lambda i, j: (i, j, 0))
    full = lambda x: pl.BlockSpec(x.shape, lambda i, j: (0,) * x.ndim)
    shapes, specs = _tail_outs(b, l, d)
    kern, in_specs, args = _after(
        dep, functools.partial(_attn_out_kernel, nct=nct),
        [tok(a.shape[2]), tok(bm.shape[2])] + _stream_specs(stream, nct, tl) + [
            pl.BlockSpec((1, 1, N_MODS, d), lambda i, j: (i, jnp.where(j < nct, 0, 1), 0, 0)),
            full(wo), full(gffn), full(wrt), full(bias)],
        [a, bm, stream[0], stream[1], mods, wo, gffn, wrt, bias])
    return pl.pallas_call(
        kern,
        out_shape=shapes,
        grid=(b, l // tl),
        in_specs=in_specs,
        out_specs=specs,
        scratch_shapes=[pltpu.VMEM((N_EXPERTS, 1), F32)],
        compiler_params=pltpu.CompilerParams(dimension_semantics=("arbitrary", "arbitrary"),
                                             vmem_limit_bytes=_vmem_limit(40)),
        name="attn_out",
    )(*args)


def _moe_dest_kernel(off_ref, eid_ref, rank_ref, dest_ref):
    eid = eid_ref[...]
    dest = rank_ref[...]
    for e in range(N_EXPERTS):
        dest = dest + jnp.where(eid == e, off_ref[e], 0)
    dest_ref[...] = dest


def _moe_dest(off, eid, rank):
    return pl.pallas_call(
        _moe_dest_kernel,
        out_shape=jax.ShapeDtypeStruct(eid.shape, jnp.int32),
        in_specs=[pl.BlockSpec(memory_space=pltpu.SMEM),
                  pl.BlockSpec(eid.shape, lambda: (0, 0, 0)), pl.BlockSpec(eid.shape, lambda: (0, 0, 0))],
        out_specs=pl.BlockSpec(eid.shape, lambda: (0, 0, 0)),
        name="moe_dest",
    )(off, eid, rank)


def _sc_mesh():
    return plsc.VectorSubcoreMesh(core_axis_name="c", subcore_axis_name="s",
                                  num_cores=V7X_SC_CORES, num_subcores=V7X_SC_SUBCORES)


def _sc_chunk(rows_per_worker):
    return max(c for c in range(8, SC_MAX_CHUNK + 1, 8) if rows_per_worker % c == 0)


def _sc_dispatch(xp, dest, p_rows):
    t, w = xp.shape
    tpw = t // V7X_SC_WORKERS
    ch = _sc_chunk(tpw)

    @functools.partial(
        pl.kernel, mesh=_sc_mesh(), out_type=jax.ShapeDtypeStruct((p_rows, w), xp.dtype),
        scratch_types=[pltpu.VMEM((ch, w), xp.dtype)] + [pltpu.VMEM((ch,), jnp.int32)] * TOP_K
        + [pltpu.SemaphoreType.DMA, pltpu.SemaphoreType.DMA],
        name="moe_dispatch")
    def run(x_hbm, dest_hbm, out_hbm, rows_v, *rest):
        idx, (sem_i, sem_o) = rest[:TOP_K], rest[TOP_K:]
        base = (lax.axis_index("s") * V7X_SC_CORES + lax.axis_index("c")) * tpw

        @pl.loop(0, tpw // ch)
        def _(i):
            t0 = base + i * ch
            loads = [pltpu.async_copy(dest_hbm.at[k, pl.ds(t0, ch)], idx[k], sem_i) for k in range(TOP_K)]
            pltpu.sync_copy(x_hbm.at[pl.ds(t0, ch)], rows_v)
            for c in loads:
                c.wait()
            puts = [pltpu.async_copy(rows_v, out_hbm.at[idx[k]], sem_o) for k in range(TOP_K)]
            for c in puts:
                c.wait()

    return run(xp, dest)


def _sc_gather(ys, dest, t):
    w = ys.shape[1]
    tpw = t // V7X_SC_WORKERS
    ch = _sc_chunk(tpw)

    @functools.partial(
        pl.kernel, mesh=_sc_mesh(), out_type=jax.ShapeDtypeStruct((TOP_K, t, w), ys.dtype),
        scratch_types=[pltpu.VMEM((ch, w), ys.dtype)] * 2 + [pltpu.VMEM((ch,), jnp.int32)] * TOP_K
        + [pltpu.SemaphoreType.DMA] * 5,
        name="moe_gather")
    def run(y_hbm, dest_hbm, out_hbm, rows_a, rows_b, *rest):
        idx, (sem_i, sem_ga, sem_gb, sem_wa, sem_wb) = rest[:TOP_K], rest[TOP_K:]
        rows, sem_g, sem_w = (rows_a, rows_b), (sem_ga, sem_gb), (sem_wa, sem_wb)
        base = (lax.axis_index("s") * V7X_SC_CORES + lax.axis_index("c")) * tpw

        @pl.loop(0, tpw // ch)
        def _(i):
            t0 = base + i * ch
            loads = [pltpu.async_copy(dest_hbm.at[k, pl.ds(t0, ch)], idx[k], sem_i) for k in range(TOP_K)]
            for c in loads:
                c.wait()
            gets, puts = [None] * TOP_K, [None] * TOP_K
            gets[0] = pltpu.async_copy(y_hbm.at[idx[0]], rows[0], sem_g[0])
            for k in range(TOP_K):
                if k + 1 < TOP_K:
                    if k >= 1:
                        puts[k - 1].wait()
                    gets[k + 1] = pltpu.async_copy(y_hbm.at[idx[k + 1]], rows[(k + 1) % 2], sem_g[(k + 1) % 2])
                gets[k].wait()
                puts[k] = pltpu.async_copy(rows[k % 2], out_hbm.at[k, pl.ds(t0, ch)], sem_w[k % 2])
            puts[TOP_K - 2].wait()
            puts[TOP_K - 1].wait()

    return run(ys, dest)


def _cache_mlp_weights(wg, wu, wd, wgu_ref, wdb_ref):
    f = wg.shape[1]
    wgu_ref[:, 0:f] = wg.astype(BF16)
    wgu_ref[:, f:] = wu.astype(BF16)
    wdb_ref[...] = wd.astype(BF16)


def _gated_mlp(xp, wgu_ref, wdb_ref):
    lo, hi = _unpack_bf16_pair(xp)
    x = jnp.concatenate([lo.astype(BF16), hi.astype(BF16)], axis=1)
    gu = jnp.dot(x, wgu_ref[...], preferred_element_type=F32)
    f = gu.shape[1] // 2
    return _dot(_silu(gu[:, :f]) * gu[:, f:], wdb_ref[...])


def _moe_expert_kernel(te_ref, tb_ref, nv_ref, x_hbm, wga_ref, wua_ref, wda_ref, wgb_ref, wub_ref, wdb_ref, y_ref,
                       gu_a, dn_a, gu_b, dn_b, ids_ref, xbuf, sems):
    i = pl.program_id(0)
    tm = MOE_ROW_TILE
    nv = nv_ref[0]
    last = (nv - 1) // 2
    first = 2 * jnp.minimum(i, last)
    ea = te_ref[first]
    eb = te_ref[first + 1]
    two = 2 * i + 1 < nv

    def rows_copy(step):
        slot = step % MOE_ROW_SLOTS
        row0 = step * (2 * tm)
        rows = pl.ds(row0 if isinstance(step, int) else pl.multiple_of(row0, 2 * tm), 2 * tm)
        return pltpu.make_async_copy(x_hbm.at[rows], xbuf.at[slot], sems.at[slot])

    @pl.when(i == 0)
    def _():
        ids_ref[0] = -1
        ids_ref[1] = -1
        for ahead in range(MOE_ROW_SLOTS - 1):
            @pl.when(ahead <= last)
            def _():
                rows_copy(ahead).start()

    @pl.when(i + (MOE_ROW_SLOTS - 1) <= last)
    def _():
        rows_copy(i + (MOE_ROW_SLOTS - 1)).start()

    @pl.when(i <= last)
    def _():
        rows_copy(i).wait()

    x_ref = xbuf.at[i % MOE_ROW_SLOTS]

    @pl.when(ids_ref[0] != ea)
    def _():
        _cache_mlp_weights(wga_ref[0, 0], wua_ref[0, 0], wda_ref[0, 0], gu_a, dn_a)
        ids_ref[0] = ea

    @pl.when(two & (eb != ea) & (ids_ref[1] != eb))
    def _():
        _cache_mlp_weights(wgb_ref[0, 0], wub_ref[0, 0], wdb_ref[0, 0], gu_b, dn_b)
        ids_ref[1] = eb

    @pl.when(two & (eb == ea))
    def _():
        y_ref[...] = _pack_bf16_pair(_gated_mlp(x_ref[...], gu_a, dn_a))

    @pl.when((2 * i < nv) & jnp.logical_not(two & (eb == ea)))
    def _():
        y_ref[0:tm, :] = _pack_bf16_pair(_gated_mlp(x_ref[0:tm, :], gu_a, dn_a))

    @pl.when(two & (eb != ea))
    def _():
        y_ref[tm:, :] = _pack_bf16_pair(_gated_mlp(x_ref[tm:, :], gu_b, dn_b))


def _moe_experts(tile_expert, n_valid, xs, wg, wu, wd, layer, dep=None):
    p_rows, w = xs.shape
    tm = MOE_ROW_TILE
    _, _, d, f = wg.shape
    npair = p_rows // (2 * tm)
    pairs = tile_expert.reshape(npair, 2)
    tile_b = jnp.maximum(lax.cummax(jnp.where(pairs[:, 1] != pairs[:, 0], pairs[:, 1], -1)), 0)
    step = lambda i, nv: jnp.minimum(i, (nv[0] - 1) // 2)
    spec_a = lambda shp: pl.BlockSpec((1, 1) + shp, lambda i, te, tb, nv: (layer, te[2 * step(i, nv)], 0, 0))
    spec_b = lambda shp: pl.BlockSpec((1, 1) + shp, lambda i, te, tb, nv: (layer, tb[step(i, nv)], 0, 0))
    rows = pl.BlockSpec((2 * tm, w), lambda i, te, tb, nv: (step(i, nv), 0))
    kern, in_specs, args = _after(
        dep, _moe_expert_kernel,
        [pl.BlockSpec(memory_space=pl.ANY), spec_a((d, f)), spec_a((d, f)), spec_a((f, d)),
         spec_b((d, f)), spec_b((d, f)), spec_b((f, d))],
        [tile_expert, tile_b, n_valid, xs, wg, wu, wd, wg, wu, wd], n_lead=3)
    return pl.pallas_call(
        kern,
        out_shape=jax.ShapeDtypeStruct((p_rows, w), xs.dtype),
        grid_spec=pltpu.PrefetchScalarGridSpec(
            num_scalar_prefetch=3, grid=(npair,),
            in_specs=in_specs,
            out_specs=rows,
            scratch_shapes=[pltpu.VMEM((d, 2 * f), BF16), pltpu.VMEM((f, d), BF16),
                            pltpu.VMEM((d, 2 * f), BF16), pltpu.VMEM((f, d), BF16), pltpu.SMEM((2,), jnp.int32),
                            pltpu.VMEM((MOE_ROW_SLOTS, 2 * tm, w), xs.dtype),
                            pltpu.SemaphoreType.DMA((MOE_ROW_SLOTS,))]),
        compiler_params=pltpu.CompilerParams(dimension_semantics=("arbitrary",),
                                             vmem_limit_bytes=_vmem_limit(48)),
        name="moe_experts",
    )(*args)


def _moe_combine_kernel(yg_hbm, w_ref, xp_ref, sg_ref, su_ref, sd_ref, h_ref, mods_ref, gfin_ref, *rest, final_norm,
                        tile0):
    o_ref, wgu_ref, wdb_ref, ybuf, sems = rest[-5:]
    tl = ybuf.shape[2]
    nt = pl.num_programs(1)
    step = pl.program_id(0) * nt + pl.program_id(1)
    n_steps = pl.num_programs(0) * nt

    def yg_copy(s):
        rows = pl.ds(pl.multiple_of((s % nt + tile0) * tl, tl), tl)
        return pltpu.make_async_copy(yg_hbm.at[:, s // nt, rows, :], ybuf.at[s % MOE_ROW_SLOTS],
                                     sems.at[s % MOE_ROW_SLOTS])

    @pl.when(step == 0)
    def _():
        _cache_mlp_weights(sg_ref[0], su_ref[0], sd_ref[0], wgu_ref, wdb_ref)
        for ahead in range(MOE_ROW_SLOTS - 1):
            @pl.when(ahead < n_steps)
            def _():
                yg_copy(ahead).start()

    @pl.when(step + (MOE_ROW_SLOTS - 1) < n_steps)
    def _():
        yg_copy(step + (MOE_ROW_SLOTS - 1)).start()

    yg_copy(step).wait()
    yg_ref = ybuf.at[step % MOE_ROW_SLOTS]

    acc = _gated_mlp(xp_ref[0], wgu_ref, wdb_ref)
    half = acc.shape[1] // 2
    lo = acc[:, :half]
    hi = acc[:, half:]
    w = w_ref[0]
    for k in range(TOP_K):
        ylo, yhi = _unpack_bf16_pair(yg_ref[k])
        wk = w[:, k:k + 1]
        lo = lo + wk * ylo
        hi = hi + wk * yhi
    y = h_ref[0] + mods_ref[0, 0, N_MODS - 1:N_MODS, :] * jnp.concatenate([lo, hi], axis=1)
    if final_norm:
        y = _rms(y, gfin_ref[...])
    o_ref[0] = y


def _moe_combine(yg, wcols, xp, sg, su, sd, h, mods, gfin, nct, layer, out_buf, out_b0, out_batch, latent_only,
                 final_norm, dep=None):
    b, l, d = h.shape
    tl = TOKEN_TILE
    tile0 = nct if latent_only else 0
    tok = lambda w: pl.BlockSpec((1, tl, w), lambda i, j: (i, j + tile0, 0))
    lay = lambda x: pl.BlockSpec((1,) + x.shape[1:], lambda i, j: (layer,) + (0,) * (x.ndim - 1))
    args = [yg, wcols, xp, sg, su, sd, h, mods, gfin]
    in_specs = [pl.BlockSpec(memory_space=pl.ANY), tok(GATE_W), tok(d // 2),
                lay(sg), lay(su), lay(sd), tok(d),
                pl.BlockSpec((1, 1, N_MODS, d), lambda i, j: (i, jnp.where(j + tile0 < nct, 0, 1), 0, 0)),
                pl.BlockSpec(gfin.shape, lambda i, j: (0, 0))]
    _, in_specs, args = _after(dep, None, in_specs, args)
    aliases = {}
    if out_buf is not None:
        args.append(out_buf)
        in_specs.append(pl.BlockSpec(memory_space=pl.ANY))
        aliases = {len(args) - 1: 0}
    return pl.pallas_call(
        functools.partial(_moe_combine_kernel, final_norm=final_norm, tile0=tile0),
        out_shape=jax.ShapeDtypeStruct((out_batch, l - tile0 * tl, d), F32),
        grid=(b, l // tl - tile0),
        in_specs=in_specs,
        out_specs=pl.BlockSpec((1, tl, d), lambda i, j: (i + out_b0, j, 0)),
        scratch_shapes=[pltpu.VMEM((d, 2 * sg.shape[2]), BF16), pltpu.VMEM((sg.shape[2], d), BF16),
                        pltpu.VMEM((MOE_ROW_SLOTS, TOP_K, tl, d // 2), yg.dtype),
                        pltpu.SemaphoreType.DMA((MOE_ROW_SLOTS,))],
        input_output_aliases=aliases,
        compiler_params=pltpu.CompilerParams(dimension_semantics=("arbitrary", "arbitrary"),
                                             vmem_limit_bytes=_vmem_limit(40)),
        name="moe_combine",
    )(*args)


def _moe_route_rows(n2p, eid, rank, counts, b, l):
    d2 = n2p.shape[2]
    t = b * l
    tm = MOE_ROW_TILE
    n_tiles = 2 * -(-(TOP_K * t + N_EXPERTS * (tm - 1)) // (2 * tm))
    tiles_e = (counts.reshape(N_EXPERTS).astype(jnp.int32) + (tm - 1)) // tm
    tile_end = jnp.cumsum(tiles_e)
    off = (tile_end - tiles_e) * tm
    n_valid = tile_end[-1:]
    tile_id = jnp.minimum(jnp.arange(n_tiles, dtype=jnp.int32), n_valid - 1)
    tile_expert = jnp.sum((tile_end[None, :] <= tile_id[:, None]).astype(jnp.int32), axis=1)
    dest = _moe_dest(off, eid, rank).transpose(1, 0, 2).reshape(8, t)
    xs = _sc_dispatch(n2p.reshape(t, d2), dest, n_tiles * tm)
    return xs, dest, tile_expert, n_valid


def _rwkv_proj_kernel(h_ref, hp_ref, hx_ref, mods_ref, g_ref, mu_ref, wr_ref, wk_ref, wv_ref, g1_ref, g2_ref,
                      w1_ref, w2_ref, a1_ref, a2_ref, w0_ref, a0_ref, kk_ref, ka_ref, rk_ref, bd_ref,
                      r_out, v_out, kk_out, g_out, km_out, b_out, lw_out, bonus_out, *, nct):
    j = pl.program_id(1)
    nt = pl.num_programs(1)
    m = mods_ref[0, 0]
    g = g_ref[...]
    n = _norm_mod(h_ref[0], g, m[0:1], m[1:2])
    tl, d = n.shape
    seg_first = (j == 0) | (j == nct)
    seg_last = (j == nct - 1) | (j == nt - 1)
    n_prev = _norm_mod(hp_ref[0], g, m[0:1], m[1:2])[7:8] * jnp.where(seg_first, 0.0, 1.0)
    n_next = _norm_mod(hx_ref[0], g, m[0:1], m[1:2])[0:1] * jnp.where(seg_last, 0.0, 1.0)
    row = lax.broadcasted_iota(jnp.int32, (tl, 1), 0)
    prev = jnp.where(row == 0, n_prev, pltpu.roll(n, 1, axis=0))
    nxt = jnp.where(row == tl - 1, n_next, pltpu.roll(n, tl - 1, axis=0))
    lane = lax.broadcasted_iota(jnp.int32, (1, d), 1)
    xx = jnp.where(lane < d // 2, prev, nxt) - n
    mu = mu_ref[...]
    bd = bd_ref[...]
    halves = [slice(0, tl // 2), slice(tl // 2, tl)]
    first = []
    for rs in halves:
        nh, xh = n[rs], xx[rs]
        xr, xw, xk, xv, xa, xg = [nh + xh * mu[i:i + 1] for i in range(6)]
        first.append((_dot(xr, wr_ref[...]), _dot(xk, wk_ref[...]), _dot(xv, wv_ref[...]),
                      _dot(xg, g1_ref[...]), _dot(xw, w1_ref[...]), _dot(xa, a1_ref[...])))
    second = []
    for r, k, v, gq, tq, ta in first:
        tw = jnp.tanh(tq)
        kk = k * kk_ref[...]
        second.append((_dot(_sigmoid(gq), g2_ref[...]), [_dot(tw, w2_ref[dr]) for dr in range(2)],
                       [_dot(ta, a2_ref[dr]) for dr in range(2)], kk, _head_sum(kk * kk, bd)))
    for rs, (r, k, v, _, _, _), (gate, zw, za, kk, kk_sq) in zip(halves, first, second):
        kk = kk / jnp.maximum(jnp.sqrt(kk_sq), 1e-12)
        g_out[0, rs, :] = gate.astype(g_out.dtype)
        r_out[0, rs, :] = r.astype(r_out.dtype)
        v_out[0, rs, :] = v.astype(v_out.dtype)
        kk_out[0, rs, :] = kk.astype(kk_out.dtype)
        bonus = jnp.zeros_like(v)
        for dr in range(2):
            lw_out[dr, 0, rs, :] = -jnp.exp(-0.5) * _sigmoid(w0_ref[dr:dr + 1, :] + zw[dr])
            a = _sigmoid(a0_ref[dr:dr + 1, :] + za[dr])
            km = k * (1.0 + (a - 1.0) * ka_ref[...])
            km_out[dr, 0, rs, :] = km.astype(km_out.dtype)
            b_out[dr, 0, rs, :] = (kk * a).astype(b_out.dtype)
            bonus = bonus + _head_sum(r * km * rk_ref[...], bd) * v
        bonus_out[0, rs, :] = bonus.astype(bonus_out.dtype)


def _rwkv_proj(h, mods, g, mu, wr, wk, wv, g1, g2, w1, w2, a1, a2, w0, a0, kk, ka, rk, bd, nct, dep=None):
    b, l, d = h.shape
    tl = TOKEN_TILE
    nb8 = l // 8
    tok = pl.BlockSpec((1, tl, d), lambda i, j: (i, j, 0))
    tok2 = pl.BlockSpec((2, 1, tl, d), lambda i, j: (0, i, j, 0))
    full = lambda x: pl.BlockSpec(x.shape, lambda i, j: (0,) * x.ndim)
    sds = jax.ShapeDtypeStruct
    kern, in_specs, args = _after(
        dep, functools.partial(_rwkv_proj_kernel, nct=nct),
        [tok,
         pl.BlockSpec((1, 8, d), lambda i, j: (i, jnp.maximum(j * (tl // 8) - 1, 0), 0)),
         pl.BlockSpec((1, 8, d), lambda i, j: (i, jnp.minimum((j + 1) * (tl // 8), nb8 - 1), 0)),
         pl.BlockSpec((1, 1, N_MODS, d), lambda i, j: (i, jnp.where(j < nct, 0, 1), 0, 0)),
         full(g), full(mu), full(wr), full(wk), full(wv), full(g1), full(g2), full(w1), full(w2),
         full(a1), full(a2), full(w0), full(a0), full(kk), full(ka), full(rk), full(bd)],
        [h, h, h, mods, g, mu, wr, wk, wv, g1, g2, w1, w2, a1, a2, w0, a0, kk, ka, rk, bd])
    return pl.pallas_call(
        kern,
        out_shape=[sds((b, l, d), BF16), sds((b, l, d), BF16), sds((b, l, d), BF16), sds((b, l, d), BF16),
                   sds((2, b, l, d), BF16), sds((2, b, l, d), BF16), sds((2, b, l, d), F32), sds((b, l, d), BF16)],
        grid=(b, l // tl),
        in_specs=in_specs,
        out_specs=[tok, tok, tok, tok, tok2, tok2, tok2, tok],
        compiler_params=pltpu.CompilerParams(dimension_semantics=("parallel", "parallel"),
                                             vmem_limit_bytes=_vmem_limit(56)),
        name="rwkv_proj",
    )(*args)


def _wkv_kernel(r_ref, v_ref, kk_ref, km_ref, b_ref, lw_ref, y_ref, st_ref):
    c = WKV_CHUNK
    w = WKV_PAIR
    rev = pl.program_id(0)
    sign = 1 - 2 * rev

    @pl.when(pl.program_id(2) == 0)
    def _():
        st_ref[...] = jnp.zeros_like(st_ref)

    ti = lax.broadcasted_iota(jnp.int32, (c, c), 0)
    si = lax.broadcasted_iota(jnp.int32, (c, c), 1)
    tri = jnp.where((si - ti) * sign <= 0, 1.0, 0.0).astype(F32)
    nsub = WKV_CHUNKS_PER_STEP
    subs = [pl.ds(pl.multiple_of(jnp.where(rev == 0, s, nsub - 1 - s) * c, c), c) for s in range(nsub)]
    rt, kt, kh, bh, v32, e_mid = [], [], [], [], [], []
    for rows in subs:
        lw = lw_ref[0, 0, rows, :]
        l_incl = jnp.dot(tri, lw, precision=HIGHEST, preferred_element_type=F32)
        mid = 0.5 * jnp.sum(lw, axis=0, keepdims=True)
        e_neg = jnp.exp(mid - l_incl)
        e_mid.append(jnp.exp(mid))
        rt.append(r_ref[0, rows, :].astype(F32) * jnp.exp(l_incl - mid))
        kt.append(kk_ref[0, rows, :].astype(F32) * jnp.exp(l_incl - lw - mid))
        kh.append(km_ref[0, 0, rows, :].astype(F32) * e_neg)
        bh.append(b_ref[0, 0, rows, :].astype(F32) * e_neg)
        v32.append(v_ref[0, rows, :].astype(F32))

    ri = lax.broadcasted_iota(jnp.int32, (w, w), 0)
    ci = lax.broadcasted_iota(jnp.int32, (w, w), 1)
    same = (ri // c) == (ci // c)
    eye = jnp.where(ri == ci, 1.0, 0.0).astype(F32)
    tl_ = lax.broadcasted_iota(jnp.int32, (c, w), 0)
    jl_ = lax.broadcasted_iota(jnp.int32, (c, w), 1) % c
    strict = (jl_ - tl_) * sign < 0
    incl = (jl_ - tl_) * sign <= 0
    eye2 = jnp.where(jl_ == tl_, 1.0, 0.0).astype(F32)
    lane = lax.broadcasted_iota(jnp.int32, (1, w), 1)
    h0 = lane < RWKV_HEAD

    def rows2(x):
        return jnp.concatenate([jnp.where(h0, x, 0.0), jnp.where(h0, 0.0, x)], axis=0)

    npair = st_ref.shape[0]
    items = [(s, slice(p * w, (p + 1) * w)) for s in range(nsub) for p in range(npair)]
    n = range(len(items))
    em = [e_mid[s][:, sl] for s, sl in items]
    g = [_dot_nt(jnp.concatenate([kt[s][:, sl], rt[s][:, sl]], axis=0),
                 jnp.concatenate([rows2(kh[s][:, sl]), rows2(bh[s][:, sl])], axis=0)) for s, sl in items]
    a_kk = [jnp.where(strict, x[:c, :w], 0.0) for x in g]
    a_rk = [jnp.where(incl, x[c:, :w], 0.0) for x in g]
    a_rb = [jnp.where(incl, x[c:, w:], 0.0) for x in g]
    vi = [v32[s][:, sl] for s, sl in items]
    v_rows = [rows2(x) for x in vi]
    av = [_dot(jnp.concatenate([a_kk[i], a_rk[i]], axis=0), v_rows[i]) for i in n]
    r_pre = [x[:c] for x in av]
    ark_v = [x[c:] for x in av]
    m = [jnp.where(strict, -x[:c, w:], 0.0) for x in g]
    tinv = [eye2 + x for x in m]
    m = [_dot(x, rows2(x)) for x in m]
    for _ in range(c.bit_length() - 3):
        both = [_dot(jnp.concatenate([tinv[i], m[i]], axis=0), rows2(m[i])) for i in n]
        tinv = [tinv[i] + both[i][:c] for i in n]
        m = [x[c:] for x in both]
    tinv = [tinv[i] + _dot(tinv[i], rows2(m[i])) for i in n]
    sol = [_dot(tinv[i], jnp.concatenate([rows2(r_pre[i]), rows2(kt[s][:, sl] * em[i])], axis=1))
           for i, (s, sl) in enumerate(items)]
    u_pre = [x[:, :w] for x in sol]
    kq = [x[:, w:] for x in sol]
    arb = [_dot(a_rb[i], jnp.concatenate([rows2(u_pre[i]), rows2(kq[i])], axis=1)) for i in n]
    y_pre = [ark_v[i] - arb[i][:, :w] for i in n]
    r_eff = [rt[s][:, sl] * em[i] - arb[i][:, w:] for i, (s, sl) in enumerate(items)]
    bbar = [bh[s][:, sl] * em[i] for i, (s, sl) in enumerate(items)]
    kbar = [kh[s][:, sl] * em[i] for i, (s, sl) in enumerate(items)]
    mmat = [eye * (em[i] * em[i]) - jnp.where(same, _dot_tn(kq[i], bbar[i]), 0.0) for i in n]
    s_pre = [jnp.where(same, _dot_tn(jnp.concatenate([vi[i], -u_pre[i]], axis=0),
                                     jnp.concatenate([kbar[i], bbar[i]], axis=0)), 0.0) for i in n]
    st = [st_ref[p] for p in range(npair)]
    for i, (s, sl) in enumerate(items):
        p = i % npair
        y_ref[0, 0, subs[s], sl] = (_dot_nt(r_eff[i], st[p]) + y_pre[i]).astype(y_ref.dtype)
        hi = st[p].astype(BF16)
        lo = (st[p] - hi.astype(F32)).astype(BF16)
        mb = mmat[i].astype(BF16)
        both = jnp.dot(jnp.concatenate([hi, lo], axis=0), mb, preferred_element_type=F32)
        st[p] = both[:w] + both[w:] + s_pre[i]
    for p in range(npair):
        st_ref[p] = st[p]


def _wkv(r, v, kk, km, bv, lw, lc, dep=None):
    b, l, d = r.shape
    c = WKV_CHUNK * WKV_CHUNKS_PER_STEP
    ncc = lc // c
    nlc = (l - lc) // c

    def chunk(dr, i):
        return jnp.where(dr == 0, i, jnp.where(i < ncc, ncc - 1 - i, nlc + 2 * ncc - 1 - i))

    shared = pl.BlockSpec((1, c, d), lambda dr, bi, i: (bi, chunk(dr, i), 0))
    per_dir = pl.BlockSpec((1, 1, c, d), lambda dr, bi, i: (dr, bi, chunk(dr, i), 0))
    kern, in_specs, args = _after(dep, _wkv_kernel, [shared, shared, shared, per_dir, per_dir, per_dir],
                                  [r, v, kk, km, bv, lw])
    return pl.pallas_call(
        kern,
        out_shape=jax.ShapeDtypeStruct((2, b, l, d), BF16),
        grid=(2, b, l // c),
        in_specs=in_specs,
        out_specs=per_dir,
        scratch_shapes=[pltpu.VMEM((d // WKV_PAIR, WKV_PAIR, WKV_PAIR), F32)],
        compiler_params=pltpu.CompilerParams(dimension_semantics=("parallel", "parallel", "arbitrary"),
                                             vmem_limit_bytes=_vmem_limit(32)),
        name="wkv7_chunked",
    )(*args)


def _rwkv_out_kernel(y_ref, bonus_ref, g_ref, lnw_ref, lnb_ref, wo_ref, bd_ref, h_ref, mods_ref, gffn_ref,
                     wrt_ref, bias_ref, hn_ref, n2_ref, eid_ref, rank_ref, w_ref, cnt_ref, run_ref):
    y = y_ref[0, 0].astype(F32) + y_ref[1, 0].astype(F32)
    bd = bd_ref[...]
    mean = _head_sum(y, bd) * (1.0 / RWKV_HEAD)
    yc = y - mean
    var = _head_sum(yc * yc, bd) * (1.0 / RWKV_HEAD)
    yn = yc * lax.rsqrt(var + GN_EPS) * lnw_ref[...] + lnb_ref[...]
    out = (yn + bonus_ref[0].astype(F32)) * g_ref[0].astype(F32)
    _mixer_tail(_dot(out, wo_ref[...]), h_ref[0], mods_ref[0, 0], gffn_ref, wrt_ref, bias_ref, hn_ref, n2_ref,
                eid_ref, rank_ref, w_ref, cnt_ref, run_ref)


def _rwkv_out(y, bonus, g, lnw, lnb, wo, bd, h, mods, gffn, wrt, bias, nct, dep=None):
    b, l, d = h.shape
    tl = TOKEN_TILE
    tok = lambda w: pl.BlockSpec((1, tl, w), lambda i, j: (i, j, 0))
    full = lambda x: pl.BlockSpec(x.shape, lambda i, j: (0,) * x.ndim)
    shapes, specs = _tail_outs(b, l, d)
    kern, in_specs, args = _after(
        dep, _rwkv_out_kernel,
        [pl.BlockSpec((2, 1, tl, d), lambda i, j: (0, i, j, 0)), tok(d), tok(d),
         full(lnw), full(lnb), full(wo), full(bd), tok(d),
         pl.BlockSpec((1, 1, N_MODS, d), lambda i, j: (i, jnp.where(j < nct, 0, 1), 0, 0)),
         full(gffn), full(wrt), full(bias)],
        [y, bonus, g, lnw, lnb, wo, bd, h, mods, gffn, wrt, bias])
    return pl.pallas_call(
        kern,
        out_shape=shapes,
        grid=(b, l // tl),
        in_specs=in_specs,
        out_specs=specs,
        scratch_shapes=[pltpu.VMEM((N_EXPERTS, 1), F32)],
        compiler_params=pltpu.CompilerParams(dimension_semantics=("arbitrary", "arbitrary"),
                                             vmem_limit_bytes=_vmem_limit(40)),
        name="rwkv_out",
    )(*args)


def _rope_table(n_lat, n_ctx):
    dim = SWA_HEAD_DIM
    nf = dim // 4
    inv = ROPE_THETA ** (-jnp.arange(nf, dtype=F32) / nf)
    row = jnp.repeat(jnp.arange(n_lat // GRID_W, dtype=F32), GRID_W)
    col = jnp.tile(jnp.arange(GRID_W, dtype=F32), n_lat // GRID_W)
    ar = row[:, None] * inv
    ac = col[:, None] * inv
    ang = jnp.concatenate([ar, ar, ac, ac], axis=-1)
    cos = jnp.concatenate([jnp.ones((n_ctx, dim), F32), jnp.cos(ang)], axis=0)
    sin = jnp.concatenate([jnp.zeros((n_ctx, dim), F32), jnp.sin(ang)], axis=0)
    return jnp.tile(cos, (1, 2)), jnp.tile(sin, (1, 2))


def _layout_attn_weights(w_in, w_uq, w_ukv):
    d = w_in.shape[0]
    s0 = MLA_Q_RANK
    s1 = s0 + MLA_KV_RANK
    s2 = s1 + MLA_ROPE
    s3 = s2 + SWA_HEADS * SWA_HEAD_DIM
    s4 = s3 + SWA_KV_HEADS * SWA_HEAD_DIM
    rep = lambda w: jnp.concatenate(
        [jnp.tile(w[:, g * SWA_HEAD_DIM:(g + 1) * SWA_HEAD_DIM], (1, SWA_GROUP)) for g in range(SWA_KV_HEADS)], axis=1)
    win = jnp.concatenate([w_in[:, :s1], w_in[:, s2:s3], rep(w_in[:, s3:s4]), rep(w_in[:, s4:]),
                           w_in[:, s1:s2], jnp.zeros((d, V7X_LANES - MLA_ROPE), w_in.dtype)], axis=1)
    qh = MLA_NOPE + MLA_ROPE
    pad = jnp.zeros((w_uq.shape[0], V7X_MXU_DIM - qh), w_uq.dtype)
    wuq = jnp.concatenate([jnp.concatenate([w_uq[:, h * qh:(h + 1) * qh], pad], axis=1) for h in range(MLA_HEADS)], axis=1)
    kvh = MLA_NOPE + MLA_V
    wuk = jnp.concatenate([w_ukv[:, h * kvh:h * kvh + MLA_NOPE] for h in range(MLA_HEADS)], axis=1)
    wuvt = jnp.concatenate([w_ukv[:, h * kvh + MLA_NOPE:(h + 1) * kvh] for h in range(MLA_HEADS)], axis=1).T
    return win.astype(BF16), wuq.astype(BF16), wuk.astype(BF16), wuvt.astype(BF16)


def _lora_pair(w_down, w_up):
    rank = w_down.shape[2]
    down = jnp.concatenate([w_down[0], w_down[1]], axis=1)
    z = jnp.zeros((rank, w_up.shape[2]), w_up.dtype)
    up = jnp.stack([jnp.concatenate([w_up[0], z], axis=0), jnp.concatenate([z, w_up[1]], axis=0)], axis=0)
    return down.astype(BF16), up.astype(BF16)


def _head_block_diag():
    i = jnp.arange(V7X_MXU_DIM) // RWKV_HEAD
    return (i[:, None] == i[None, :]).astype(BF16)


def kernel(x, c, ctx, c_ctx, ada_w, ada_b, norm_mix, norm_ffn, norm_final, attn_w_in, attn_q_norm, attn_kv_norm, attn_w_uq, attn_w_ukv, attn_sinks, attn_w_o, rwkv_mu, rwkv_w_r, rwkv_w_k, rwkv_w_v, rwkv_w_o, rwkv_g1, rwkv_g2, rwkv_w0, rwkv_w1, rwkv_w2, rwkv_a0, rwkv_a1, rwkv_a2, rwkv_k_k, rwkv_k_a, rwkv_r_k, rwkv_ln_w, rwkv_ln_b, moe_router, moe_bias, moe_w_gate, moe_w_up, moe_w_down, moe_ws_gate, moe_ws_up, moe_ws_down):
    bsz, s, d = x.shape
    lc = ctx.shape[1]
    l = lc + s
    depth = ada_w.shape[0]
    nct = lc // TOKEN_TILE
    assert lc % TOKEN_TILE == 0 and s % TOKEN_TILE == 0 and s >= SWA_BAND and lc % SWA_Q_TILE == 0
    assert lc % (WKV_CHUNK * WKV_CHUNKS_PER_STEP) == 0
    assert d % V7X_MXU_DIM == 0 and WKV_CHUNK * 2 == V7X_LANES
    ngrp = SAMPLE_GROUPS
    bg = bsz // ngrp
    assert bsz % ngrp == 0 and (bg * l) % (8 * V7X_SC_WORKERS) == 0

    assert ngrp == 2
    cos, sin = _rope_table(s, lc)
    bd = _head_block_diag()
    rows = -(-(bsz + 1) // 8) * 8
    cc = jnp.concatenate([c, c_ctx[None, :], jnp.zeros((rows - bsz - 1, d), F32)], axis=0)
    row2 = lambda a: a.reshape(1, -1)
    moe_w = (moe_w_gate, moe_w_up, moe_w_down)
    moe_ws = (moe_ws_gate, moe_ws_up, moe_ws_down)

    shared = {}

    def layer_weights(li):
        if li not in shared:
            i = li // 2
            ada = _ada_mods(cc, ada_w, ada_b, li)
            w = dict(
                mods=jnp.stack([jnp.broadcast_to(ada[bsz].reshape(1, N_MODS, d), (bsz, N_MODS, d)),
                                ada[:bsz].reshape(bsz, N_MODS, d)], axis=1),
                wrt=jnp.concatenate([moe_router[li].T, jnp.zeros((GATE_W - N_EXPERTS, d), F32)], axis=0),
                bias=moe_bias[li].reshape(N_GROUPS, GROUP_SIZE, 1))
            if li % 2 == 0:
                w["win"], w["wuq"], w["wuk"], w["wuvt"] = _layout_attn_weights(attn_w_in[i], attn_w_uq[i], attn_w_ukv[i])
                w["wo"] = attn_w_o[i].astype(BF16)
            else:
                w["w1"], w["w2"] = _lora_pair(rwkv_w1[i], rwkv_w2[i])
                w["a1"], w["a2"] = _lora_pair(rwkv_a1[i], rwkv_a2[i])
                w["wr"], w["wk"], w["wv"], w["wo"] = [x[i].astype(BF16) for x in (rwkv_w_r, rwkv_w_k, rwkv_w_v, rwkv_w_o)]
                w["g1"], w["g2"] = rwkv_g1[i].astype(BF16), rwkv_g2[i].astype(BF16)
            shared[li] = w
        return shared[li]

    groups = [dict(stream=(ctx, x, g * bg, 0), b0=g * bg) for g in range(ngrp)]
    result = [None]

    def run_stage(st, li, name, dep):
        w = layer_weights(li)
        i = li // 2
        with_ctx = li < depth - 1
        mods = w["mods"][st["b0"]:st["b0"] + bg]
        if name == "proj" and li % 2 == 0:
            st["qkv"] = _attn_proj(st["stream"], bg, l, mods, row2(norm_mix[li]), w["win"], row2(attn_q_norm[i]),
                                   row2(attn_kv_norm[i]), w["wuq"], w["wuk"], w["wuvt"], cos, sin, nct, dep=dep)
            return st["qkv"][0]
        if name == "mid" and li % 2 == 0:
            q, k, vt, qs, ks, vs = st.pop("qkv")
            st["a"] = _mla_attention(q, k, vt, lc, 0 if with_ctx else lc // MLA_Q_TILE, dep=dep)
            st["bm"] = _swa_attention(attn_sinks[i], qs, ks, vs, lc, 0 if with_ctx else lc // SWA_Q_TILE, dep=st["a"])
            return st["bm"]
        if name == "proj":
            assert st["stream"][0] is st["stream"][1]
            st["feat"] = _rwkv_proj(st["stream"][0], mods, row2(norm_mix[li]), rwkv_mu[i], w["wr"], w["wk"], w["wv"],
                                    w["g1"], w["g2"], w["w1"], w["w2"], w["a1"], w["a2"], rwkv_w0[i], rwkv_a0[i],
                                    row2(rwkv_k_k[i]), row2(rwkv_k_a[i]), row2(rwkv_r_k[i]), bd, nct, dep=dep)
            return st["feat"][0]
        if name == "mid":
            r, v, kk, gt, km, bv, lw, bonus = st.pop("feat")
            st["y"] = _wkv(r, v, kk, km, bv, lw, lc, dep=dep)
            st["gate"], st["bonus"] = gt, bonus
            return st["y"]
        if name == "out":
            if li % 2 == 0:
                tail = _attn_out(st.pop("a"), st.pop("bm"), st["stream"], mods, w["wo"], row2(norm_ffn[li]),
                                 w["wrt"], w["bias"], nct, dep=dep)
            else:
                tail = _rwkv_out(st.pop("y"), st.pop("bonus"), st.pop("gate"), row2(rwkv_ln_w[i]), row2(rwkv_ln_b[i]),
                                 w["wo"], bd, st["stream"][0], mods, row2(norm_ffn[li]), w["wrt"], w["bias"], nct, dep=dep)
            st["h"], st["n2p"], eid, rank, st["wcols"], counts = tail
            st["xs"], st["dest"], st["tile_expert"], st["n_valid"] = _moe_route_rows(st["n2p"], eid, rank, counts, bg, l)
            return st["h"]
        if name == "experts":
            ys = _moe_experts(st.pop("tile_expert"), st.pop("n_valid"), st.pop("xs"), *moe_w, li, dep=dep)
            st["yg"] = _sc_gather(ys, st.pop("dest"), bg * l).reshape(TOP_K, bg, l, d // 2)
            return ys
        assert name == "combine"
        last = li == depth - 1
        h = _moe_combine(st.pop("yg"), st.pop("wcols"), st.pop("n2p"), *moe_ws, st.pop("h"), mods, row2(norm_final),
                         nct, li, result[0] if last else None, st["b0"] if last else 0, bsz if last else bg,
                         last, last, dep=dep)
        if last:
            result[0] = h
        else:
            st["stream"] = (h, h, 0, nct)
        return h

    order = [(0, 0, "proj"), (0, 0, "mid")]
    for li in range(depth):
        order += [(0, li, "out"), (1, li, "proj"), (0, li, "experts"), (1, li, "mid")]
        if li < depth - 1:
            order += [(0, li, "combine"), (1, li, "out"), (0, li + 1, "proj"), (1, li, "experts"),
                      (0, li + 1, "mid"), (1, li, "combine")]
        else:
            order += [(1, li, "out"), (0, li, "combine"), (1, li, "experts"), (1, li, "combine")]
    dep = None
    for g, li, name in order:
        dep = run_stage(groups[g], li, name, dep)
    return result[0]
```

```python
import functools

import jax
import jax.numpy as jnp
from jax import lax
from jax.experimental import pallas as pl
from jax.experimental.pallas import tpu as pltpu
from jax.experimental.pallas import tpu_sc as plsc

F32 = jnp.float32
BF16 = jnp.bfloat16
HIGHEST = lax.Precision.HIGHEST

GRID_W = 64
NORM_EPS = 1e-6
ROPE_THETA = 10000.0
NEG_INF = -1e30
N_MODS = 6

MLA_HEADS = 4
MLA_Q_RANK = 384
MLA_KV_RANK = 256
MLA_NOPE = 128
MLA_ROPE = 64
MLA_V = 128

SWA_HEADS = 8
SWA_KV_HEADS = 2
SWA_GROUP = SWA_HEADS // SWA_KV_HEADS
SWA_HEAD_DIM = 64
WINDOW = 128

RWKV_HEAD = 64
DECAY_LORA = 64
ICLR_LORA = 64
GATE_LORA = 128
GN_EPS = 64e-5

N_EXPERTS = 64
TOP_K = 6
N_GROUPS = 8
TOPK_GROUPS = 4
GROUP_SIZE = N_EXPERTS // N_GROUPS
ROUTED_SCALE = 2.5
GATE_W = 128

V7X_LANES = 128
V7X_MXU_DIM = 256
V7X_VMEM_BYTES = 64 * 1024 * 1024
V7X_SC_CORES = 2
V7X_SC_SUBCORES = 16
V7X_SC_WORKERS = V7X_SC_CORES * V7X_SC_SUBCORES

TOKEN_TILE = 256
MLA_Q_TILE = 256
MLA_HEADS_PER_STEP = 2
SWA_Q_TILE = 256
SWA_BAND = SWA_Q_TILE + 2 * WINDOW
WKV_CHUNK = 64
WKV_PAIR = 2 * RWKV_HEAD
WKV_CHUNKS_PER_STEP = 4
MOE_ROW_TILE = 512
MOE_ROW_SLOTS = 3
SAMPLE_GROUPS = 2
SC_MAX_CHUNK = 64

LOG2E = 1.4426950408889634
MIB = 1024 * 1024
VMEM_RESERVE_BYTES = 4 * MIB


def _vmem_limit(mib):
    return min(mib * MIB, V7X_VMEM_BYTES - VMEM_RESERVE_BYTES)


def _dot(a, b):
    return jnp.dot(a.astype(BF16), b.astype(BF16), preferred_element_type=F32)


def _dot_nt(a, b):
    return lax.dot_general(a.astype(BF16), b.astype(BF16), (((1,), (1,)), ((), ())),
                           preferred_element_type=F32)


def _dot_tn(a, b):
    return lax.dot_general(a.astype(BF16), b.astype(BF16), (((0,), (0,)), ((), ())),
                           preferred_element_type=F32)


def _sigmoid(x):
    return 1.0 / (1.0 + jnp.exp(-x))


def _silu(x):
    return x * _sigmoid(x)


def _rms(x, g):
    return x * lax.rsqrt(jnp.mean(x * x, axis=-1, keepdims=True) + NORM_EPS) * g


def _norm_mod(x, g, shift, scale):
    return _rms(x, g) * (1.0 + scale) + shift


def _split_dot(x, w):
    hi = x.astype(BF16)
    lo = (x - hi.astype(F32)).astype(BF16)
    return (jnp.dot(hi, w, preferred_element_type=F32) + jnp.dot(lo, w, preferred_element_type=F32))


def _head_sum(x, bd):
    w = bd.shape[0]
    parts = [_split_dot(x[:, c * w:(c + 1) * w], bd) for c in range(x.shape[1] // w)]
    return jnp.concatenate(parts, axis=1)


def _after(dep, kernel, in_specs, args, n_lead=0):
    if dep is None:
        return kernel, list(in_specs), list(args)
    n_in = n_lead + len(in_specs)

    def ordered(*refs):
        return kernel(*refs[:n_in], *refs[n_in + 1:])

    return ordered, list(in_specs) + [pl.BlockSpec(memory_space=pl.ANY)], list(args) + [dep]


def _ada_kernel(c_ref, w_ref, b_ref, o_ref):
    s = _silu(c_ref[...])
    o_ref[...] = jnp.dot(s, w_ref[0], precision=HIGHEST, preferred_element_type=F32) + b_ref[0]


def _ada_mods(cc, w, b, layer):
    rows, d = cc.shape
    depth, _, n = w.shape
    return pl.pallas_call(
        _ada_kernel,
        out_shape=jax.ShapeDtypeStruct((rows, n), F32),
        grid=(n // d,),
        in_specs=[pl.BlockSpec((rows, d), lambda i: (0, 0)),
                  pl.BlockSpec((1, d, d), lambda i: (layer, 0, i)),
                  pl.BlockSpec((1, 1, d), lambda i: (layer, 0, i))],
        out_specs=pl.BlockSpec((rows, d), lambda i: (0, i)),
        compiler_params=pltpu.CompilerParams(dimension_semantics=("parallel",),
                                             vmem_limit_bytes=_vmem_limit(32)),
        name="ada_mods",
    )(cc, w, b.reshape(depth, 1, n))


def _rope128(x, cos, sin, first_half):
    rot = jnp.where(first_half, -pltpu.roll(x, V7X_LANES - 16, axis=1), pltpu.roll(x, 16, axis=1))
    return x * cos + rot * sin


_C_CQ = 0
_C_CKV = _C_CQ + MLA_Q_RANK
_C_QS = _C_CKV + MLA_KV_RANK
_C_KS = _C_QS + SWA_HEADS * SWA_HEAD_DIM
_C_VS = _C_KS + SWA_KV_HEADS * V7X_MXU_DIM
_C_KR = _C_VS + SWA_KV_HEADS * V7X_MXU_DIM
_C_END = _C_KR + V7X_LANES
_SWA_W = SWA_KV_HEADS * V7X_MXU_DIM
_MLA_QK_W = MLA_HEADS * V7X_MXU_DIM


def _stream_specs(stream, nct, tl):
    ctx_arr, lat_arr, b0, lat_off = stream
    d = ctx_arr.shape[2]
    return [pl.BlockSpec((1, tl, d), lambda i, j: (i + b0, jnp.minimum(j, nct - 1), 0)),
            pl.BlockSpec((1, tl, d), lambda i, j: (i + b0, jnp.maximum(j - nct, 0) + lat_off, 0))]


def _stream_tile(c_ref, x_ref, nct):
    rows = c_ref.shape[1]
    take_ctx = lax.broadcasted_iota(jnp.int32, (rows, 1), 0) < jnp.where(pl.program_id(1) < nct, rows, 0)
    return jnp.where(take_ctx, c_ref[0], x_ref[0])


def _attn_proj_kernel(c_ref, x_ref, mods_ref, g_ref, win_ref, qn_ref, kvn_ref, wuq_ref, wuk_ref, wuvt_ref, cos_ref,
                      sin_ref, q_ref, k_ref, vt_ref, qs_ref, ks_ref, vs_ref, *, nct):
    m = mods_ref[0, 0]
    n = _norm_mod(_stream_tile(c_ref, x_ref, nct), g_ref[...], m[0:1], m[1:2])
    u = _dot(n, win_ref[...])
    cos = cos_ref[...]
    sin = sin_ref[...]
    lane = lax.broadcasted_iota(jnp.int32, (1, V7X_LANES), 1)
    first_half = (lane % 32) < 16

    def rope(x):
        return _rope128(x, cos, sin, first_half)

    scale_a = (MLA_NOPE + MLA_ROPE) ** -0.5 * LOG2E
    scale_b = SWA_HEAD_DIM ** -0.5 * LOG2E
    q = _dot(_rms(u[:, _C_CQ:_C_CKV], qn_ref[...]), wuq_ref[...])
    ckv = _rms(u[:, _C_CKV:_C_QS], kvn_ref[...])
    kn = _dot(ckv, wuk_ref[...])
    vt_ref[0] = _dot_nt(wuvt_ref[...], ckv).astype(BF16)
    kr = rope(u[:, _C_KR:_C_END]).astype(BF16)
    for h in range(MLA_HEADS):
        o = h * V7X_MXU_DIM
        q_ref[0, :, o:o + V7X_LANES] = (q[:, o:o + V7X_LANES] * scale_a).astype(BF16)
        q_ref[0, :, o + V7X_LANES:o + V7X_MXU_DIM] = (rope(q[:, o + V7X_LANES:o + V7X_MXU_DIM]) * scale_a).astype(BF16)
        k_ref[0, :, o:o + V7X_LANES] = kn[:, h * MLA_NOPE:(h + 1) * MLA_NOPE].astype(BF16)
        k_ref[0, :, o + V7X_LANES:o + V7X_MXU_DIM] = kr
    for c in range((_C_KS - _C_QS) // V7X_LANES):
        o = c * V7X_LANES
        qs_ref[0, :, o:o + V7X_LANES] = (rope(u[:, _C_QS + o:_C_QS + o + V7X_LANES]) * scale_b).astype(BF16)
    for c in range(_SWA_W // V7X_LANES):
        o = c * V7X_LANES
        ks_ref[0, :, o:o + V7X_LANES] = rope(u[:, _C_KS + o:_C_KS + o + V7X_LANES]).astype(BF16)
    vs_ref[0] = u[:, _C_VS:_C_KR].astype(BF16)


def _attn_proj(stream, b, l, mods, g, win, qn, kvn, wuq, wuk, wuvt, cos, sin, nct, dep=None):
    d = stream[0].shape[2]
    tl = TOKEN_TILE
    tok = lambda w: pl.BlockSpec((1, tl, w), lambda i, j: (i, j, 0))
    full = lambda a: pl.BlockSpec(a.shape, lambda i, j: (0,) * a.ndim)
    sds = jax.ShapeDtypeStruct
    dv = MLA_HEADS * MLA_V
    kern, in_specs, args = _after(
        dep, functools.partial(_attn_proj_kernel, nct=nct),
        _stream_specs(stream, nct, tl) + [
            pl.BlockSpec((1, 1, N_MODS, d), lambda i, j: (i, jnp.where(j < nct, 0, 1), 0, 0)),
            full(g), full(win), full(qn), full(kvn), full(wuq), full(wuk), full(wuvt),
            pl.BlockSpec((tl, V7X_LANES), lambda i, j: (j, 0)),
            pl.BlockSpec((tl, V7X_LANES), lambda i, j: (j, 0))],
        [stream[0], stream[1], mods, g, win, qn, kvn, wuq, wuk, wuvt, cos, sin])
    return pl.pallas_call(
        kern,
        out_shape=[sds((b, l, _MLA_QK_W), BF16), sds((b, l, _MLA_QK_W), BF16), sds((b, dv, l), BF16),
                   sds((b, l, SWA_HEADS * SWA_HEAD_DIM), BF16), sds((b, l, _SWA_W), BF16), sds((b, l, _SWA_W), BF16)],
        grid=(b, l // tl),
        in_specs=in_specs,
        out_specs=[tok(_MLA_QK_W), tok(_MLA_QK_W), pl.BlockSpec((1, dv, tl), lambda i, j: (i, 0, j)),
                   tok(SWA_HEADS * SWA_HEAD_DIM), tok(_SWA_W), tok(_SWA_W)],
        compiler_params=pltpu.CompilerParams(dimension_semantics=("parallel", "parallel"),
                                             vmem_limit_bytes=_vmem_limit(48)),
        name="attn_proj",
    )(*args)


def _mla_kernel(q_ref, k_ref, vt_ref, o_ref, *, nct_q, lc):
    hw = V7X_MXU_DIM

    def attend(nk):
        st = [_dot_nt(k_ref[0, 0:nk, hh * hw:(hh + 1) * hw], q_ref[0, :, hh * hw:(hh + 1) * hw])
              for hh in range(MLA_HEADS_PER_STEP)]
        for hh, s in enumerate(st):
            p = jnp.exp2(s - jnp.max(s, axis=0, keepdims=True))
            den = jnp.sum(p, axis=0, keepdims=True)
            ot = _dot(vt_ref[0, hh * MLA_V:(hh + 1) * MLA_V, 0:nk], p) / den
            o_ref[0, :, hh * MLA_V:(hh + 1) * MLA_V] = ot.T.astype(o_ref.dtype)

    @pl.when(pl.program_id(2) < nct_q)
    def _():
        attend(lc)

    @pl.when(pl.program_id(2) >= nct_q)
    def _():
        attend(k_ref.shape[1])


def _mla_attention(q, k, vt, lc, q_tile0, dep=None):
    b, l, _ = q.shape
    tq = MLA_Q_TILE
    hps = MLA_HEADS_PER_STEP
    kern, in_specs, args = _after(
        dep, functools.partial(_mla_kernel, nct_q=lc // tq - q_tile0, lc=lc),
        [pl.BlockSpec((1, tq, hps * V7X_MXU_DIM), lambda i, h, j: (i, j + q_tile0, h)),
         pl.BlockSpec((1, l, hps * V7X_MXU_DIM), lambda i, h, j: (i, 0, h)),
         pl.BlockSpec((1, hps * MLA_V, l), lambda i, h, j: (i, h, 0))],
        [q, k, vt])
    return pl.pallas_call(
        kern,
        out_shape=jax.ShapeDtypeStruct((b, l, MLA_HEADS * MLA_V), BF16),
        grid=(b, MLA_HEADS // hps, l // tq - q_tile0),
        in_specs=in_specs,
        out_specs=pl.BlockSpec((1, tq, hps * MLA_V), lambda i, h, j: (i, j + q_tile0, h)),
        compiler_params=pltpu.CompilerParams(dimension_semantics=("parallel", "parallel", "parallel"),
                                             vmem_limit_bytes=_vmem_limit(48)),
        name="mla_attention",
    )(*args)


def _swa_kernel(sink_ref, q_ref, k_ref, v_ref, o_ref, *, lc, q_tile0):
    tq = SWA_Q_TILE
    l = k_ref.shape[1]
    r0 = (pl.program_id(1) + q_tile0) * tq
    start = pl.multiple_of(jnp.clip(r0 - WINDOW, lc, l - SWA_BAND), WINDOW)
    rows = SWA_GROUP * tq
    row = lax.broadcasted_iota(jnp.int32, (rows, 1), 0)
    qpos = jnp.where(r0 >= lc, r0, -l) + row % tq
    kpos = start + lax.broadcasted_iota(jnp.int32, (1, SWA_BAND), 1)
    valid = jnp.abs(qpos - kpos) <= WINDOW
    lane = lax.broadcasted_iota(jnp.int32, (1, V7X_MXU_DIM), 1)
    head = [(lane // SWA_HEAD_DIM) == hh for hh in range(SWA_GROUP)]
    groups = range(SWA_KV_HEADS)
    sls = [slice(g * V7X_MXU_DIM, (g + 1) * V7X_MXU_DIM) for g in groups]
    qstack = []
    for sl in sls:
        qg = q_ref[0, :, sl]
        zero = jnp.zeros_like(qg)
        qstack.append(jnp.concatenate([jnp.where(head[hh], qg, zero) for hh in range(SWA_GROUP)], axis=0))
    sc = [_dot_nt(qstack[g], k_ref[0, 0:lc, sls[g]]) for g in groups]
    sb = [_dot_nt(qstack[g], k_ref[0, pl.ds(start, SWA_BAND), sls[g]]) for g in groups]
    for g in groups:
        sl = sls[g]
        sbm = jnp.where(valid, sb[g], NEG_INF)
        sk = jnp.zeros((rows, 1), F32)
        for hh in range(SWA_GROUP):
            sk = jnp.where(row // tq == hh, sink_ref[g * SWA_GROUP + hh] * LOG2E, sk)
        mx = jnp.maximum(jnp.maximum(jnp.max(sc[g], axis=-1, keepdims=True), jnp.max(sbm, axis=-1, keepdims=True)), sk)
        pc = jnp.exp2(sc[g] - mx)
        pb = jnp.exp2(sbm - mx)
        den = jnp.sum(pc, axis=-1, keepdims=True) + jnp.sum(pb, axis=-1, keepdims=True) + jnp.exp2(sk - mx)
        ostack = (_dot(pc, v_ref[0, 0:lc, sl]) + _dot(pb, v_ref[0, pl.ds(start, SWA_BAND), sl])) / den
        o = jnp.zeros((tq, V7X_MXU_DIM), F32)
        for hh in range(SWA_GROUP):
            o = o + jnp.where(head[hh], ostack[hh * tq:(hh + 1) * tq], 0.0)
        o_ref[0, :, sl] = o.astype(o_ref.dtype)


def _swa_attention(sinks, q, k, v, lc, q_tile0, dep=None):
    b, l, _ = q.shape
    tq = SWA_Q_TILE
    kern, in_specs, args = _after(
        dep, functools.partial(_swa_kernel, lc=lc, q_tile0=q_tile0),
        [pl.BlockSpec(memory_space=pltpu.SMEM),
         pl.BlockSpec((1, tq, SWA_HEADS * SWA_HEAD_DIM), lambda i, j: (i, j + q_tile0, 0)),
         pl.BlockSpec((1, l, _SWA_W), lambda i, j: (i, 0, 0)),
         pl.BlockSpec((1, l, _SWA_W), lambda i, j: (i, 0, 0))],
        [sinks, q, k, v])
    return pl.pallas_call(
        kern,
        out_shape=jax.ShapeDtypeStruct((b, l, SWA_HEADS * SWA_HEAD_DIM), BF16),
        grid=(b, l // tq - q_tile0),
        in_specs=in_specs,
        out_specs=pl.BlockSpec((1, tq, SWA_HEADS * SWA_HEAD_DIM), lambda i, j: (i, j + q_tile0, 0)),
        compiler_params=pltpu.CompilerParams(dimension_semantics=("parallel", "parallel"),
                                             vmem_limit_bytes=_vmem_limit(48)),
        name="swa_attention",
    )(*args)


def _pack_bf16_pair(x):
    w = x.shape[1] // 2
    lo = pltpu.bitcast(x[:, :w].astype(BF16).astype(F32), jnp.int32)
    hi = pltpu.bitcast(x[:, w:].astype(BF16).astype(F32), jnp.int32)
    return lax.shift_right_logical(lo, jnp.int32(16)) | (hi & jnp.int32(-65536))


def _unpack_bf16_pair(p):
    return pltpu.bitcast(p << 16, F32), pltpu.bitcast(p & jnp.int32(-65536), F32)


def _route(n2, wrt, bias, run_ref):
    n_hi = n2.astype(BF16)
    n_lo = (n2 - n_hi.astype(F32)).astype(BF16)
    w_hi = wrt.astype(BF16)
    w_lo = (wrt - w_hi.astype(F32)).astype(BF16)
    w_both = jnp.concatenate([w_hi, w_lo], axis=0)
    rows = n2.shape[0]
    score_blocks = []
    for o in range(0, rows, V7X_LANES):
        both = _dot_nt(w_both, n_hi[o:o + V7X_LANES])
        logits = both[:GATE_W] + both[GATE_W:] + _dot_nt(w_hi, n_lo[o:o + V7X_LANES])
        score_blocks.append(_sigmoid(logits[0:N_EXPERTS]))
    scores = jnp.concatenate(score_blocks, axis=1)

    def select(sc2):
        cols = sc2.shape[1]
        shape3 = (N_GROUPS, GROUP_SIZE, cols)
        choice = sc2.reshape(shape3) + bias
        ji = lax.broadcasted_iota(jnp.int32, shape3, 1).astype(F32)
        m1 = jnp.max(choice, axis=1, keepdims=True)
        first = jnp.min(jnp.where(choice == m1, ji, float(GROUP_SIZE)), axis=1, keepdims=True)
        m2 = jnp.max(jnp.where(ji == first, -jnp.inf, choice), axis=1, keepdims=True)
        gs = m1 + m2
        gidx = lax.broadcasted_iota(jnp.int32, gs.shape, 0).astype(F32)
        gsel = jnp.zeros_like(gs)
        for _ in range(TOPK_GROUPS):
            mx = jnp.max(gs, axis=0, keepdims=True)
            pick = gidx == jnp.min(jnp.where(gs == mx, gidx, float(N_GROUPS)), axis=0, keepdims=True)
            gsel = jnp.where(pick, 1.0, gsel)
            gs = jnp.where(pick, -jnp.inf, gs)
        cand = jnp.where(gsel > 0.0, choice, -jnp.inf).reshape(N_EXPERTS, cols)
        eidx = lax.broadcasted_iota(jnp.int32, (N_EXPERTS, cols), 0).astype(F32)
        out = []
        for _ in range(TOP_K):
            mx = jnp.max(cand, axis=0, keepdims=True)
            pick = eidx == jnp.min(jnp.where(cand == mx, eidx, float(N_EXPERTS)), axis=0, keepdims=True)
            out.append(jnp.where(pick, 1.0, 0.0))
            cand = jnp.where(pick, -jnp.inf, cand)
        return out

    blocks = [select(sc2) for sc2 in score_blocks]
    picks = [jnp.concatenate([blk[k] for blk in blocks], axis=1) > 0.0 for k in range(TOP_K)]
    ei = lax.broadcasted_iota(jnp.int32, (N_EXPERTS, rows), 0).astype(F32)
    esel = jnp.zeros((N_EXPERTS, rows), F32)
    for pick in picks:
        esel = jnp.where(pick, 1.0, esel)
    before = jnp.where(lax.broadcasted_iota(jnp.int32, (rows, rows), 0) < lax.broadcasted_iota(jnp.int32, (rows, rows), 1),
                       1.0, 0.0).astype(BF16)
    slot = jnp.dot(esel.astype(BF16), before, preferred_element_type=F32) + run_ref[...]
    run_ref[...] += jnp.sum(esel, axis=1, keepdims=True)
    sc = [jnp.sum(jnp.where(pick, scores, 0.0), axis=0, keepdims=True) for pick in picks]
    tot = sc[0]
    for x in sc[1:]:
        tot = tot + x
    k8 = lax.broadcasted_iota(jnp.int32, (8, rows), 0)
    kw = lax.broadcasted_iota(jnp.int32, (GATE_W, rows), 0)
    eid = jnp.zeros((8, rows), jnp.int32)
    rank = jnp.zeros((8, rows), jnp.int32)
    wk = jnp.zeros((GATE_W, rows), F32)
    for k, pick in enumerate(picks):
        e_k = jnp.sum(jnp.where(pick, ei, 0.0), axis=0, keepdims=True).astype(jnp.int32)
        r_k = jnp.sum(jnp.where(pick, slot, 0.0), axis=0, keepdims=True).astype(jnp.int32)
        eid = jnp.where(k8 == k, e_k, eid)
        rank = jnp.where(k8 == k, r_k, rank)
        wk = jnp.where(kw == k, sc[k] * (ROUTED_SCALE / tot), wk)
    return eid, rank, wk.T


def _mixer_tail(o, h, m, gffn_ref, wrt_ref, bias_ref, hn_ref, n2_ref, eid_ref, rank_ref, w_ref, cnt_ref, run_ref):
    @pl.when((pl.program_id(0) == 0) & (pl.program_id(1) == 0))
    def _():
        run_ref[...] = jnp.zeros_like(run_ref)

    hn = h + m[2:3] * o
    hn_ref[0] = hn
    n2 = _norm_mod(hn, gffn_ref[...], m[3:4], m[4:5])
    n2_ref[0] = _pack_bf16_pair(n2)
    eid, rank, wcols = _route(n2, wrt_ref[...], bias_ref[...], run_ref)
    eid_ref[0] = eid
    rank_ref[0] = rank
    w_ref[0] = wcols
    cnt_ref[...] = run_ref[...]


def _attn_out_kernel(a_ref, b_ref, c_ref, x_ref, mods_ref, wo_ref, gffn_ref, wrt_ref, bias_ref,
                     hn_ref, n2_ref, eid_ref, rank_ref, w_ref, cnt_ref, run_ref, *, nct):
    wa = MLA_HEADS * MLA_V
    o = _dot(a_ref[0], wo_ref[0:wa, :]) + _dot(b_ref[0], wo_ref[wa:, :])
    _mixer_tail(o, _stream_tile(c_ref, x_ref, nct), mods_ref[0, 0], gffn_ref, wrt_ref, bias_ref, hn_ref, n2_ref,
                eid_ref, rank_ref, w_ref, cnt_ref, run_ref)


def _tail_outs(b, l, d):
    tl = TOKEN_TILE
    nt = l // tl
    sds = jax.ShapeDtypeStruct
    tok = lambda w: pl.BlockSpec((1, tl, w), lambda i, j: (i, j, 0))
    blk = pl.BlockSpec((1, 8, tl), lambda i, j: (i * nt + j, 0, 0))
    shapes = [sds((b, l, d), F32), sds((b, l, d // 2), jnp.int32), sds((b * nt, 8, tl), jnp.int32),
              sds((b * nt, 8, tl), jnp.int32), sds((b, l, GATE_W), F32), sds((N_EXPERTS, 1), F32)]
    specs = [tok(d), tok(d // 2), blk, blk, tok(GATE_W), pl.BlockSpec((N_EXPERTS, 1), lambda i, j: (0, 0))]
    return shapes, specs


def _attn_out(a, bm, stream, mods, wo, gffn, wrt, bias, nct, dep=None):
    b, l, _ = a.shape
    d = stream[0].shape[2]
    tl = TOKEN_TILE
    tok = lambda w: pl.BlockSpec((1, tl, w), lambda i, j: (i, j, 0))
    full = lambda x: pl.BlockSpec(x.shape, lambda i, j: (0,) * x.ndim)
    shapes, specs = _tail_outs(b, l, d)
    kern, in_specs, args = _after(
        dep, functools.partial(_attn_out_kernel, nct=nct),
        [tok(a.shape[2]), tok(bm.shape[2])] + _stream_specs(stream, nct, tl) + [
            pl.BlockSpec((1, 1, N_MODS, d), lambda i, j: (i, jnp.where(j < nct, 0, 1), 0, 0)),
            full(wo), full(gffn), full(wrt), full(bias)],
        [a, bm, stream[0], stream[1], mods, wo, gffn, wrt, bias])
    return pl.pallas_call(
        kern,
        out_shape=shapes,
        grid=(b, l // tl),
        in_specs=in_specs,
        out_specs=specs,
        scratch_shapes=[pltpu.VMEM((N_EXPERTS, 1), F32)],
        compiler_params=pltpu.CompilerParams(dimension_semantics=("arbitrary", "arbitrary"),
                                             vmem_limit_bytes=_vmem_limit(40)),
        name="attn_out",
    )(*args)


def _moe_dest_kernel(off_ref, eid_ref, rank_ref, dest_ref):
    eid = eid_ref[...]
    dest = rank_ref[...]
    for e in range(N_EXPERTS):
        dest = dest + jnp.where(eid == e, off_ref[e], 0)
    dest_ref[...] = dest


def _moe_dest(off, eid, rank):
    return pl.pallas_call(
        _moe_dest_kernel,
        out_shape=jax.ShapeDtypeStruct(eid.shape, jnp.int32),
        in_specs=[pl.BlockSpec(memory_space=pltpu.SMEM),
                  pl.BlockSpec(eid.shape, lambda: (0, 0, 0)), pl.BlockSpec(eid.shape, lambda: (0, 0, 0))],
        out_specs=pl.BlockSpec(eid.shape, lambda: (0, 0, 0)),
        name="moe_dest",
    )(off, eid, rank)


def _sc_mesh():
    return plsc.VectorSubcoreMesh(core_axis_name="c", subcore_axis_name="s",
                                  num_cores=V7X_SC_CORES, num_subcores=V7X_SC_SUBCORES)


def _sc_chunk(rows_per_worker):
    return max(c for c in range(8, SC_MAX_CHUNK + 1, 8) if rows_per_worker % c == 0)


def _sc_dispatch(xp, dest, p_rows):
    t, w = xp.shape
    tpw = t // V7X_SC_WORKERS
    ch = _sc_chunk(tpw)

    @functools.partial(
        pl.kernel, mesh=_sc_mesh(), out_type=jax.ShapeDtypeStruct((p_rows, w), xp.dtype),
        scratch_types=[pltpu.VMEM((ch, w), xp.dtype)] + [pltpu.VMEM((ch,), jnp.int32)] * TOP_K
        + [pltpu.SemaphoreType.DMA, pltpu.SemaphoreType.DMA],
        name="moe_dispatch")
    def run(x_hbm, dest_hbm, out_hbm, rows_v, *rest):
        idx, (sem_i, sem_o) = rest[:TOP_K], rest[TOP_K:]
        base = (lax.axis_index("s") * V7X_SC_CORES + lax.axis_index("c")) * tpw

        @pl.loop(0, tpw // ch)
        def _(i):
            t0 = base + i * ch
            loads = [pltpu.async_copy(dest_hbm.at[k, pl.ds(t0, ch)], idx[k], sem_i) for k in range(TOP_K)]
            pltpu.sync_copy(x_hbm.at[pl.ds(t0, ch)], rows_v)
            for c in loads:
                c.wait()
            puts = [pltpu.async_copy(rows_v, out_hbm.at[idx[k]], sem_o) for k in range(TOP_K)]
            for c in puts:
                c.wait()

    return run(xp, dest)


def _sc_gather(ys, dest, t):
    w = ys.shape[1]
    tpw = t // V7X_SC_WORKERS
    ch = _sc_chunk(tpw)

    @functools.partial(
        pl.kernel, mesh=_sc_mesh(), out_type=jax.ShapeDtypeStruct((TOP_K, t, w), ys.dtype),
        scratch_types=[pltpu.VMEM((ch, w), ys.dtype)] * 2 + [pltpu.VMEM((ch,), jnp.int32)] * TOP_K
        + [pltpu.SemaphoreType.DMA] * 5,
        name="moe_gather")
    def run(y_hbm, dest_hbm, out_hbm, rows_a, rows_b, *rest):
        idx, (sem_i, sem_ga, sem_gb, sem_wa, sem_wb) = rest[:TOP_K], rest[TOP_K:]
        rows, sem_g, sem_w = (rows_a, rows_b), (sem_ga, sem_gb), (sem_wa, sem_wb)
        base = (lax.axis_index("s") * V7X_SC_CORES + lax.axis_index("c")) * tpw

        @pl.loop(0, tpw // ch)
        def _(i):
            t0 = base + i * ch
            loads = [pltpu.async_copy(dest_hbm.at[k, pl.ds(t0, ch)], idx[k], sem_i) for k in range(TOP_K)]
            for c in loads:
                c.wait()
            gets, puts = [None] * TOP_K, [None] * TOP_K
            gets[0] = pltpu.async_copy(y_hbm.at[idx[0]], rows[0], sem_g[0])
            for k in range(TOP_K):
                if k + 1 < TOP_K:
                    if k >= 1:
                        puts[k - 1].wait()
                    gets[k + 1] = pltpu.async_copy(y_hbm.at[idx[k + 1]], rows[(k + 1) % 2], sem_g[(k + 1) % 2])
                gets[k].wait()
                puts[k] = pltpu.async_copy(rows[k % 2], out_hbm.at[k, pl.ds(t0, ch)], sem_w[k % 2])
            puts[TOP_K - 2].wait()
            puts[TOP_K - 1].wait()

    return run(ys, dest)


def _cache_mlp_weights(wg, wu, wd, wgu_ref, wdb_ref):
    f = wg.shape[1]
    wgu_ref[:, 0:f] = wg.astype(BF16)
    wgu_ref[:, f:] = wu.astype(BF16)
    wdb_ref[...] = wd.astype(BF16)


def _gated_mlp(xp, wgu_ref, wdb_ref):
    lo, hi = _unpack_bf16_pair(xp)
    x = jnp.concatenate([lo.astype(BF16), hi.astype(BF16)], axis=1)
    gu = jnp.dot(x, wgu_ref[...], preferred_element_type=F32)
    f = gu.shape[1] // 2
    return _dot(_silu(gu[:, :f]) * gu[:, f:], wdb_ref[...])


def _moe_expert_kernel(te_ref, tb_ref, nv_ref, x_hbm, wga_ref, wua_ref, wda_ref, wgb_ref, wub_ref, wdb_ref, y_ref,
                       gu_a, dn_a, gu_b, dn_b, ids_ref, xbuf, sems):
    i = pl.program_id(0)
    tm = MOE_ROW_TILE
    nv = nv_ref[0]
    last = (nv - 1) // 2
    first = 2 * jnp.minimum(i, last)
    ea = te_ref[first]
    eb = te_ref[first + 1]
    two = 2 * i + 1 < nv

    def rows_copy(step):
        slot = step % MOE_ROW_SLOTS
        row0 = step * (2 * tm)
        rows = pl.ds(row0 if isinstance(step, int) else pl.multiple_of(row0, 2 * tm), 2 * tm)
        return pltpu.make_async_copy(x_hbm.at[rows], xbuf.at[slot], sems.at[slot])

    @pl.when(i == 0)
    def _():
        ids_ref[0] = -1
        ids_ref[1] = -1
        for ahead in range(MOE_ROW_SLOTS - 1):
            @pl.when(ahead <= last)
            def _():
                rows_copy(ahead).start()

    @pl.when(i + (MOE_ROW_SLOTS - 1) <= last)
    def _():
        rows_copy(i + (MOE_ROW_SLOTS - 1)).start()

    @pl.when(i <= last)
    def _():
        rows_copy(i).wait()

    x_ref = xbuf.at[i % MOE_ROW_SLOTS]

    @pl.when(ids_ref[0] != ea)
    def _():
        _cache_mlp_weights(wga_ref[0, 0], wua_ref[0, 0], wda_ref[0, 0], gu_a, dn_a)
        ids_ref[0] = ea

    @pl.when(two & (eb != ea) & (ids_ref[1] != eb))
    def _():
        _cache_mlp_weights(wgb_ref[0, 0], wub_ref[0, 0], wdb_ref[0, 0], gu_b, dn_b)
        ids_ref[1] = eb

    @pl.when(two & (eb == ea))
    def _():
        y_ref[...] = _pack_bf16_pair(_gated_mlp(x_ref[...], gu_a, dn_a))

    @pl.when((2 * i < nv) & jnp.logical_not(two & (eb == ea)))
    def _():
        y_ref[0:tm, :] = _pack_bf16_pair(_gated_mlp(x_ref[0:tm, :], gu_a, dn_a))

    @pl.when(two & (eb != ea))
    def _():
        y_ref[tm:, :] = _pack_bf16_pair(_gated_mlp(x_ref[tm:, :], gu_b, dn_b))


def _moe_experts(tile_expert, n_valid, xs, wg, wu, wd, layer, dep=None):
    p_rows, w = xs.shape
    tm = MOE_ROW_TILE
    _, _, d, f = wg.shape
    npair = p_rows // (2 * tm)
    pairs = tile_expert.reshape(npair, 2)
    tile_b = jnp.maximum(lax.cummax(jnp.where(pairs[:, 1] != pairs[:, 0], pairs[:, 1], -1)), 0)
    step = lambda i, nv: jnp.minimum(i, (nv[0] - 1) // 2)
    spec_a = lambda shp: pl.BlockSpec((1, 1) + shp, lambda i, te, tb, nv: (layer, te[2 * step(i, nv)], 0, 0))
    spec_b = lambda shp: pl.BlockSpec((1, 1) + shp, lambda i, te, tb, nv: (layer, tb[step(i, nv)], 0, 0))
    rows = pl.BlockSpec((2 * tm, w), lambda i, te, tb, nv: (step(i, nv), 0))
    kern, in_specs, args = _after(
        dep, _moe_expert_kernel,
        [pl.BlockSpec(memory_space=pl.ANY), spec_a((d, f)), spec_a((d, f)), spec_a((f, d)),
         spec_b((d, f)), spec_b((d, f)), spec_b((f, d))],
        [tile_expert, tile_b, n_valid, xs, wg, wu, wd, wg, wu, wd], n_lead=3)
    return pl.pallas_call(
        kern,
        out_shape=jax.ShapeDtypeStruct((p_rows, w), xs.dtype),
        grid_spec=pltpu.PrefetchScalarGridSpec(
            num_scalar_prefetch=3, grid=(npair,),
            in_specs=in_specs,
            out_specs=rows,
            scratch_shapes=[pltpu.VMEM((d, 2 * f), BF16), pltpu.VMEM((f, d), BF16),
                            pltpu.VMEM((d, 2 * f), BF16), pltpu.VMEM((f, d), BF16), pltpu.SMEM((2,), jnp.int32),
                            pltpu.VMEM((MOE_ROW_SLOTS, 2 * tm, w), xs.dtype),
                            pltpu.SemaphoreType.DMA((MOE_ROW_SLOTS,))]),
        compiler_params=pltpu.CompilerParams(dimension_semantics=("arbitrary",),
                                             vmem_limit_bytes=_vmem_limit(48)),
        name="moe_experts",
    )(*args)


def _moe_combine_kernel(yg_hbm, w_ref, xp_ref, sg_ref, su_ref, sd_ref, h_ref, mods_ref, gfin_ref, *rest, final_norm,
                        tile0):
    o_ref, wgu_ref, wdb_ref, ybuf, sems = rest[-5:]
    tl = ybuf.shape[2]
    nt = pl.num_programs(1)
    step = pl.program_id(0) * nt + pl.program_id(1)
    n_steps = pl.num_programs(0) * nt

    def yg_copy(s):
        rows = pl.ds(pl.multiple_of((s % nt + tile0) * tl, tl), tl)
        return pltpu.make_async_copy(yg_hbm.at[:, s // nt, rows, :], ybuf.at[s % MOE_ROW_SLOTS],
                                     sems.at[s % MOE_ROW_SLOTS])

    @pl.when(step == 0)
    def _():
        _cache_mlp_weights(sg_ref[0], su_ref[0], sd_ref[0], wgu_ref, wdb_ref)
        for ahead in range(MOE_ROW_SLOTS - 1):
            @pl.when(ahead < n_steps)
            def _():
                yg_copy(ahead).start()

    @pl.when(step + (MOE_ROW_SLOTS - 1) < n_steps)
    def _():
        yg_copy(step + (MOE_ROW_SLOTS - 1)).start()

    yg_copy(step).wait()
    yg_ref = ybuf.at[step % MOE_ROW_SLOTS]

    acc = _gated_mlp(xp_ref[0], wgu_ref, wdb_ref)
    half = acc.shape[1] // 2
    lo = acc[:, :half]
    hi = acc[:, half:]
    w = w_ref[0]
    for k in range(TOP_K):
        ylo, yhi = _unpack_bf16_pair(yg_ref[k])
        wk = w[:, k:k + 1]
        lo = lo + wk * ylo
        hi = hi + wk * yhi
    y = h_ref[0] + mods_ref[0, 0, N_MODS - 1:N_MODS, :] * jnp.concatenate([lo, hi], axis=1)
    if final_norm:
        y = _rms(y, gfin_ref[...])
    o_ref[0] = y


def _moe_combine(yg, wcols, xp, sg, su, sd, h, mods, gfin, nct, layer, out_buf, out_b0, out_batch, latent_only,
                 final_norm, dep=None):
    b, l, d = h.shape
    tl = TOKEN_TILE
    tile0 = nct if latent_only else 0
    tok = lambda w: pl.BlockSpec((1, tl, w), lambda i, j: (i, j + tile0, 0))
    lay = lambda x: pl.BlockSpec((1,) + x.shape[1:], lambda i, j: (layer,) + (0,) * (x.ndim - 1))
    args = [yg, wcols, xp, sg, su, sd, h, mods, gfin]
    in_specs = [pl.BlockSpec(memory_space=pl.ANY), tok(GATE_W), tok(d // 2),
                lay(sg), lay(su), lay(sd), tok(d),
                pl.BlockSpec((1, 1, N_MODS, d), lambda i, j: (i, jnp.where(j + tile0 < nct, 0, 1), 0, 0)),
                pl.BlockSpec(gfin.shape, lambda i, j: (0, 0))]
    _, in_specs, args = _after(dep, None, in_specs, args)
    aliases = {}
    if out_buf is not None:
        args.append(out_buf)
        in_specs.append(pl.BlockSpec(memory_space=pl.ANY))
        aliases = {len(args) - 1: 0}
    return pl.pallas_call(
        functools.partial(_moe_combine_kernel, final_norm=final_norm, tile0=tile0),
        out_shape=jax.ShapeDtypeStruct((out_batch, l - tile0 * tl, d), F32),
        grid=(b, l // tl - tile0),
        in_specs=in_specs,
        out_specs=pl.BlockSpec((1, tl, d), lambda i, j: (i + out_b0, j, 0)),
        scratch_shapes=[pltpu.VMEM((d, 2 * sg.shape[2]), BF16), pltpu.VMEM((sg.shape[2], d), BF16),
                        pltpu.VMEM((MOE_ROW_SLOTS, TOP_K, tl, d // 2), yg.dtype),
                        pltpu.SemaphoreType.DMA((MOE_ROW_SLOTS,))],
        input_output_aliases=aliases,
        compiler_params=pltpu.CompilerParams(dimension_semantics=("arbitrary", "arbitrary"),
                                             vmem_limit_bytes=_vmem_limit(40)),
        name="moe_combine",
    )(*args)


def _moe_route_rows(n2p, eid, rank, counts, b, l):
    d2 = n2p.shape[2]
    t = b * l
    tm = MOE_ROW_TILE
    n_tiles = 2 * -(-(TOP_K * t + N_EXPERTS * (tm - 1)) // (2 * tm))
    tiles_e = (counts.reshape(N_EXPERTS).astype(jnp.int32) + (tm - 1)) // tm
    tile_end = jnp.cumsum(tiles_e)
    off = (tile_end - tiles_e) * tm
    n_valid = tile_end[-1:]
    tile_id = jnp.minimum(jnp.arange(n_tiles, dtype=jnp.int32), n_valid - 1)
    tile_expert = jnp.sum((tile_end[None, :] <= tile_id[:, None]).astype(jnp.int32), axis=1)
    dest = _moe_dest(off, eid, rank).transpose(1, 0, 2).reshape(8, t)
    xs = _sc_dispatch(n2p.reshape(t, d2), dest, n_tiles * tm)
    return xs, dest, tile_expert, n_valid


def _rwkv_proj_kernel(h_ref, hp_ref, hx_ref, mods_ref, g_ref, mu_ref, wr_ref, wk_ref, wv_ref, g1_ref, g2_ref,
                      w1_ref, w2_ref, a1_ref, a2_ref, w0_ref, a0_ref, kk_ref, ka_ref, rk_ref, bd_ref,
                      r_out, v_out, kk_out, g_out, km_out, b_out, lw_out, bonus_out, *, nct):
    j = pl.program_id(1)
    nt = pl.num_programs(1)
    m = mods_ref[0, 0]
    g = g_ref[...]
    n = _norm_mod(h_ref[0], g, m[0:1], m[1:2])
    tl, d = n.shape
    seg_first = (j == 0) | (j == nct)
    seg_last = (j == nct - 1) | (j == nt - 1)
    n_prev = _norm_mod(hp_ref[0], g, m[0:1], m[1:2])[7:8] * jnp.where(seg_first, 0.0, 1.0)
    n_next = _norm_mod(hx_ref[0], g, m[0:1], m[1:2])[0:1] * jnp.where(seg_last, 0.0, 1.0)
    row = lax.broadcasted_iota(jnp.int32, (tl, 1), 0)
    prev = jnp.where(row == 0, n_prev, pltpu.roll(n, 1, axis=0))
    nxt = jnp.where(row == tl - 1, n_next, pltpu.roll(n, tl - 1, axis=0))
    lane = lax.broadcasted_iota(jnp.int32, (1, d), 1)
    xx = jnp.where(lane < d // 2, prev, nxt) - n
    mu = mu_ref[...]
    bd = bd_ref[...]
    halves = [slice(0, tl // 2), slice(tl // 2, tl)]
    first = []
    for rs in halves:
        nh, xh = n[rs], xx[rs]
        xr, xw, xk, xv, xa, xg = [nh + xh * mu[i:i + 1] for i in range(6)]
        first.append((_dot(xr, wr_ref[...]), _dot(xk, wk_ref[...]), _dot(xv, wv_ref[...]),
                      _dot(xg, g1_ref[...]), _dot(xw, w1_ref[...]), _dot(xa, a1_ref[...])))
    second = []
    for r, k, v, gq, tq, ta in first:
        tw = jnp.tanh(tq)
        kk = k * kk_ref[...]
        second.append((_dot(_sigmoid(gq), g2_ref[...]), [_dot(tw, w2_ref[dr]) for dr in range(2)],
                       [_dot(ta, a2_ref[dr]) for dr in range(2)], kk, _head_sum(kk * kk, bd)))
    for rs, (r, k, v, _, _, _), (gate, zw, za, kk, kk_sq) in zip(halves, first, second):
        kk = kk / jnp.maximum(jnp.sqrt(kk_sq), 1e-12)
        g_out[0, rs, :] = gate.astype(g_out.dtype)
        r_out[0, rs, :] = r.astype(r_out.dtype)
        v_out[0, rs, :] = v.astype(v_out.dtype)
        kk_out[0, rs, :] = kk.astype(kk_out.dtype)
        bonus = jnp.zeros_like(v)
        for dr in range(2):
            lw_out[dr, 0, rs, :] = -jnp.exp(-0.5) * _sigmoid(w0_ref[dr:dr + 1, :] + zw[dr])
            a = _sigmoid(a0_ref[dr:dr + 1, :] + za[dr])
            km = k * (1.0 + (a - 1.0) * ka_ref[...])
            km_out[dr, 0, rs, :] = km.astype(km_out.dtype)
            b_out[dr, 0, rs, :] = (kk * a).astype(b_out.dtype)
            bonus = bonus + _head_sum(r * km * rk_ref[...], bd) * v
        bonus_out[0, rs, :] = bonus.astype(bonus_out.dtype)


def _rwkv_proj(h, mods, g, mu, wr, wk, wv, g1, g2, w1, w2, a1, a2, w0, a0, kk, ka, rk, bd, nct, dep=None):
    b, l, d = h.shape
    tl = TOKEN_TILE
    nb8 = l // 8
    tok = pl.BlockSpec((1, tl, d), lambda i, j: (i, j, 0))
    tok2 = pl.BlockSpec((2, 1, tl, d), lambda i, j: (0, i, j, 0))
    full = lambda x: pl.BlockSpec(x.shape, lambda i, j: (0,) * x.ndim)
    sds = jax.ShapeDtypeStruct
    kern, in_specs, args = _after(
        dep, functools.partial(_rwkv_proj_kernel, nct=nct),
        [tok,
         pl.BlockSpec((1, 8, d), lambda i, j: (i, jnp.maximum(j * (tl // 8) - 1, 0), 0)),
         pl.BlockSpec((1, 8, d), lambda i, j: (i, jnp.minimum((j + 1) * (tl // 8), nb8 - 1), 0)),
         pl.BlockSpec((1, 1, N_MODS, d), lambda i, j: (i, jnp.where(j < nct, 0, 1), 0, 0)),
         full(g), full(mu), full(wr), full(wk), full(wv), full(g1), full(g2), full(w1), full(w2),
         full(a1), full(a2), full(w0), full(a0), full(kk), full(ka), full(rk), full(bd)],
        [h, h, h, mods, g, mu, wr, wk, wv, g1, g2, w1, w2, a1, a2, w0, a0, kk, ka, rk, bd])
    return pl.pallas_call(
        kern,
        out_shape=[sds((b, l, d), BF16), sds((b, l, d), BF16), sds((b, l, d), BF16), sds((b, l, d), BF16),
                   sds((2, b, l, d), BF16), sds((2, b, l, d), BF16), sds((2, b, l, d), F32), sds((b, l, d), BF16)],
        grid=(b, l // tl),
        in_specs=in_specs,
        out_specs=[tok, tok, tok, tok, tok2, tok2, tok2, tok],
        compiler_params=pltpu.CompilerParams(dimension_semantics=("parallel", "parallel"),
                                             vmem_limit_bytes=_vmem_limit(56)),
        name="rwkv_proj",
    )(*args)


def _wkv_kernel(r_ref, v_ref, kk_ref, km_ref, b_ref, lw_ref, y_ref, st_ref):
    c = WKV_CHUNK
    w = WKV_PAIR
    rev = pl.program_id(0)
    sign = 1 - 2 * rev

    @pl.when(pl.program_id(2) == 0)
    def _():
        st_ref[...] = jnp.zeros_like(st_ref)

    ti = lax.broadcasted_iota(jnp.int32, (c, c), 0)
    si = lax.broadcasted_iota(jnp.int32, (c, c), 1)
    tri = jnp.where((si - ti) * sign <= 0, 1.0, 0.0).astype(F32)
    nsub = WKV_CHUNKS_PER_STEP
    subs = [pl.ds(pl.multiple_of(jnp.where(rev == 0, s, nsub - 1 - s) * c, c), c) for s in range(nsub)]
    rt, kt, kh, bh, v32, e_mid = [], [], [], [], [], []
    for rows in subs:
        lw = lw_ref[0, 0, rows, :]
        l_incl = jnp.dot(tri, lw, precision=HIGHEST, preferred_element_type=F32)
        mid = 0.5 * jnp.sum(lw, axis=0, keepdims=True)
        e_neg = jnp.exp(mid - l_incl)
        e_mid.append(jnp.exp(mid))
        rt.append(r_ref[0, rows, :].astype(F32) * jnp.exp(l_incl - mid))
        kt.append(kk_ref[0, rows, :].astype(F32) * jnp.exp(l_incl - lw - mid))
        kh.append(km_ref[0, 0, rows, :].astype(F32) * e_neg)
        bh.append(b_ref[0, 0, rows, :].astype(F32) * e_neg)
        v32.append(v_ref[0, rows, :].astype(F32))

    ri = lax.broadcasted_iota(jnp.int32, (w, w), 0)
    ci = lax.broadcasted_iota(jnp.int32, (w, w), 1)
    same = (ri // c) == (ci // c)
    eye = jnp.where(ri == ci, 1.0, 0.0).astype(F32)
    tl_ = lax.broadcasted_iota(jnp.int32, (c, w), 0)
    jl_ = lax.broadcasted_iota(jnp.int32, (c, w), 1) % c
    strict = (jl_ - tl_) * sign < 0
    incl = (jl_ - tl_) * sign <= 0
    eye2 = jnp.where(jl_ == tl_, 1.0, 0.0).astype(F32)
    lane = lax.broadcasted_iota(jnp.int32, (1, w), 1)
    h0 = lane < RWKV_HEAD

    def rows2(x):
        return jnp.concatenate([jnp.where(h0, x, 0.0), jnp.where(h0, 0.0, x)], axis=0)

    npair = st_ref.shape[0]
    items = [(s, slice(p * w, (p + 1) * w)) for s in range(nsub) for p in range(npair)]
    n = range(len(items))
    em = [e_mid[s][:, sl] for s, sl in items]
    g = [_dot_nt(jnp.concatenate([kt[s][:, sl], rt[s][:, sl]], axis=0),
                 jnp.concatenate([rows2(kh[s][:, sl]), rows2(bh[s][:, sl])], axis=0)) for s, sl in items]
    a_kk = [jnp.where(strict, x[:c, :w], 0.0) for x in g]
    a_rk = [jnp.where(incl, x[c:, :w], 0.0) for x in g]
    a_rb = [jnp.where(incl, x[c:, w:], 0.0) for x in g]
    vi = [v32[s][:, sl] for s, sl in items]
    v_rows = [rows2(x) for x in vi]
    av = [_dot(jnp.concatenate([a_kk[i], a_rk[i]], axis=0), v_rows[i]) for i in n]
    r_pre = [x[:c] for x in av]
    ark_v = [x[c:] for x in av]
    m = [jnp.where(strict, -x[:c, w:], 0.0) for x in g]
    tinv = [eye2 + x for x in m]
    m = [_dot(x, rows2(x)) for x in m]
    for _ in range(c.bit_length() - 3):
        both = [_dot(jnp.concatenate([tinv[i], m[i]], axis=0), rows2(m[i])) for i in n]
        tinv = [tinv[i] + both[i][:c] for i in n]
        m = [x[c:] for x in both]
    tinv = [tinv[i] + _dot(tinv[i], rows2(m[i])) for i in n]
    sol = [_dot(tinv[i], jnp.concatenate([rows2(r_pre[i]), rows2(kt[s][:, sl] * em[i])], axis=1))
           for i, (s, sl) in enumerate(items)]
    u_pre = [x[:, :w] for x in sol]
    kq = [x[:, w:] for x in sol]
    arb = [_dot(a_rb[i], jnp.concatenate([rows2(u_pre[i]), rows2(kq[i])], axis=1)) for i in n]
    y_pre = [ark_v[i] - arb[i][:, :w] for i in n]
    r_eff = [rt[s][:, sl] * em[i] - arb[i][:, w:] for i, (s, sl) in enumerate(items)]
    bbar = [bh[s][:, sl] * em[i] for i, (s, sl) in enumerate(items)]
    kbar = [kh[s][:, sl] * em[i] for i, (s, sl) in enumerate(items)]
    mmat = [eye * (em[i] * em[i]) - jnp.where(same, _dot_tn(kq[i], bbar[i]), 0.0) for i in n]
    s_pre = [jnp.where(same, _dot_tn(jnp.concatenate([vi[i], -u_pre[i]], axis=0),
                                     jnp.concatenate([kbar[i], bbar[i]], axis=0)), 0.0) for i in n]
    st = [st_ref[p] for p in range(npair)]
    for i, (s, sl) in enumerate(items):
        p = i % npair
        y_ref[0, 0, subs[s], sl] = (_dot_nt(r_eff[i], st[p]) + y_pre[i]).astype(y_ref.dtype)
        hi = st[p].astype(BF16)
        lo = (st[p] - hi.astype(F32)).astype(BF16)
        mb = mmat[i].astype(BF16)
        both = jnp.dot(jnp.concatenate([hi, lo], axis=0), mb, preferred_element_type=F32)
        st[p] = both[:w] + both[w:] + s_pre[i]
    for p in range(npair):
        st_ref[p] = st[p]


def _wkv(r, v, kk, km, bv, lw, lc, dep=None):
    b, l, d = r.shape
    c = WKV_CHUNK * WKV_CHUNKS_PER_STEP
    ncc = lc // c
    nlc = (l - lc) // c

    def chunk(dr, i):
        return jnp.where(dr == 0, i, jnp.where(i < ncc, ncc - 1 - i, nlc + 2 * ncc - 1 - i))

    shared = pl.BlockSpec((1, c, d), lambda dr, bi, i: (bi, chunk(dr, i), 0))
    per_dir = pl.BlockSpec((1, 1, c, d), lambda dr, bi, i: (dr, bi, chunk(dr, i), 0))
    kern, in_specs, args = _after(dep, _wkv_kernel, [shared, shared, shared, per_dir, per_dir, per_dir],
                                  [r, v, kk, km, bv, lw])
    return pl.pallas_call(
        kern,
        out_shape=jax.ShapeDtypeStruct((2, b, l, d), BF16),
        grid=(2, b, l // c),
        in_specs=in_specs,
        out_specs=per_dir,
        scratch_shapes=[pltpu.VMEM((d // WKV_PAIR, WKV_PAIR, WKV_PAIR), F32)],
        compiler_params=pltpu.CompilerParams(dimension_semantics=("parallel", "parallel", "arbitrary"),
                                             vmem_limit_bytes=_vmem_limit(32)),
        name="wkv7_chunked",
    )(*args)


def _rwkv_out_kernel(y_ref, bonus_ref, g_ref, lnw_ref, lnb_ref, wo_ref, bd_ref, h_ref, mods_ref, gffn_ref,
                     wrt_ref, bias_ref, hn_ref, n2_ref, eid_ref, rank_ref, w_ref, cnt_ref, run_ref):
    y = y_ref[0, 0].astype(F32) + y_ref[1, 0].astype(F32)
    bd = bd_ref[...]
    mean = _head_sum(y, bd) * (1.0 / RWKV_HEAD)
    yc = y - mean
    var = _head_sum(yc * yc, bd) * (1.0 / RWKV_HEAD)
    yn = yc * lax.rsqrt(var + GN_EPS) * lnw_ref[...] + lnb_ref[...]
    out = (yn + bonus_ref[0].astype(F32)) * g_ref[0].astype(F32)
    _mixer_tail(_dot(out, wo_ref[...]), h_ref[0], mods_ref[0, 0], gffn_ref, wrt_ref, bias_ref, hn_ref, n2_ref,
                eid_ref, rank_ref, w_ref, cnt_ref, run_ref)


def _rwkv_out(y, bonus, g, lnw, lnb, wo, bd, h, mods, gffn, wrt, bias, nct, dep=None):
    b, l, d = h.shape
    tl = TOKEN_TILE
    tok = lambda w: pl.BlockSpec((1, tl, w), lambda i, j: (i, j, 0))
    full = lambda x: pl.BlockSpec(x.shape, lambda i, j: (0,) * x.ndim)
    shapes, specs = _tail_outs(b, l, d)
    kern, in_specs, args = _after(
        dep, _rwkv_out_kernel,
        [pl.BlockSpec((2, 1, tl, d), lambda i, j: (0, i, j, 0)), tok(d), tok(d),
         full(lnw), full(lnb), full(wo), full(bd), tok(d),
         pl.BlockSpec((1, 1, N_MODS, d), lambda i, j: (i, jnp.where(j < nct, 0, 1), 0, 0)),
         full(gffn), full(wrt), full(bias)],
        [y, bonus, g, lnw, lnb, wo, bd, h, mods, gffn, wrt, bias])
    return pl.pallas_call(
        kern,
        out_shape=shapes,
        grid=(b, l // tl),
        in_specs=in_specs,
        out_specs=specs,
        scratch_shapes=[pltpu.VMEM((N_EXPERTS, 1), F32)],
        compiler_params=pltpu.CompilerParams(dimension_semantics=("arbitrary", "arbitrary"),
                                             vmem_limit_bytes=_vmem_limit(40)),
        name="rwkv_out",
    )(*args)


def _rope_table(n_lat, n_ctx):
    dim = SWA_HEAD_DIM
    nf = dim // 4
    inv = ROPE_THETA ** (-jnp.arange(nf, dtype=F32) / nf)
    row = jnp.repeat(jnp.arange(n_lat // GRID_W, dtype=F32), GRID_W)
    col = jnp.tile(jnp.arange(GRID_W, dtype=F32), n_lat // GRID_W)
    ar = row[:, None] * inv
    ac = col[:, None] * inv
    ang = jnp.concatenate([ar, ar, ac, ac], axis=-1)
    cos = jnp.concatenate([jnp.ones((n_ctx, dim), F32), jnp.cos(ang)], axis=0)
    sin = jnp.concatenate([jnp.zeros((n_ctx, dim), F32), jnp.sin(ang)], axis=0)
    return jnp.tile(cos, (1, 2)), jnp.tile(sin, (1, 2))


def _layout_attn_weights(w_in, w_uq, w_ukv):
    d = w_in.shape[0]
    s0 = MLA_Q_RANK
    s1 = s0 + MLA_KV_RANK
    s2 = s1 + MLA_ROPE
    s3 = s2 + SWA_HEADS * SWA_HEAD_DIM
    s4 = s3 + SWA_KV_HEADS * SWA_HEAD_DIM
    rep = lambda w: jnp.concatenate(
        [jnp.tile(w[:, g * SWA_HEAD_DIM:(g + 1) * SWA_HEAD_DIM], (1, SWA_GROUP)) for g in range(SWA_KV_HEADS)], axis=1)
    win = jnp.concatenate([w_in[:, :s1], w_in[:, s2:s3], rep(w_in[:, s3:s4]), rep(w_in[:, s4:]),
                           w_in[:, s1:s2], jnp.zeros((d, V7X_LANES - MLA_ROPE), w_in.dtype)], axis=1)
    qh = MLA_NOPE + MLA_ROPE
    pad = jnp.zeros((w_uq.shape[0], V7X_MXU_DIM - qh), w_uq.dtype)
    wuq = jnp.concatenate([jnp.concatenate([w_uq[:, h * qh:(h + 1) * qh], pad], axis=1) for h in range(MLA_HEADS)], axis=1)
    kvh = MLA_NOPE + MLA_V
    wuk = jnp.concatenate([w_ukv[:, h * kvh:h * kvh + MLA_NOPE] for h in range(MLA_HEADS)], axis=1)
    wuvt = jnp.concatenate([w_ukv[:, h * kvh + MLA_NOPE:(h + 1) * kvh] for h in range(MLA_HEADS)], axis=1).T
    return win.astype(BF16), wuq.astype(BF16), wuk.astype(BF16), wuvt.astype(BF16)


def _lora_pair(w_down, w_up):
    rank = w_down.shape[2]
    down = jnp.concatenate([w_down[0], w_down[1]], axis=1)
    z = jnp.zeros((rank, w_up.shape[2]), w_up.dtype)
    up = jnp.stack([jnp.concatenate([w_up[0], z], axis=0), jnp.concatenate([z, w_up[1]], axis=0)], axis=0)
    return down.astype(BF16), up.astype(BF16)


def _head_block_diag():
    i = jnp.arange(V7X_MXU_DIM) // RWKV_HEAD
    return (i[:, None] == i[None, :]).astype(BF16)


def kernel(x, c, ctx, c_ctx, ada_w, ada_b, norm_mix, norm_ffn, norm_final, attn_w_in, attn_q_norm, attn_kv_norm, attn_w_uq, attn_w_ukv, attn_sinks, attn_w_o, rwkv_mu, rwkv_w_r, rwkv_w_k, rwkv_w_v, rwkv_w_o, rwkv_g1, rwkv_g2, rwkv_w0, rwkv_w1, rwkv_w2, rwkv_a0, rwkv_a1, rwkv_a2, rwkv_k_k, rwkv_k_a, rwkv_r_k, rwkv_ln_w, rwkv_ln_b, moe_router, moe_bias, moe_w_gate, moe_w_up, moe_w_down, moe_ws_gate, moe_ws_up, moe_ws_down):
    bsz, s, d = x.shape
    lc = ctx.shape[1]
    l = lc + s
    depth = ada_w.shape[0]
    nct = lc // TOKEN_TILE
    assert lc % TOKEN_TILE == 0 and s % TOKEN_TILE == 0 and s >= SWA_BAND and lc % SWA_Q_TILE == 0
    assert lc % (WKV_CHUNK * WKV_CHUNKS_PER_STEP) == 0
    assert d % V7X_MXU_DIM == 0 and WKV_CHUNK * 2 == V7X_LANES
    ngrp = SAMPLE_GROUPS
    bg = bsz // ngrp
    assert bsz % ngrp == 0 and (bg * l) % (8 * V7X_SC_WORKERS) == 0

    assert ngrp == 2
    cos, sin = _rope_table(s, lc)
    bd = _head_block_diag()
    rows = -(-(bsz + 1) // 8) * 8
    cc = jnp.concatenate([c, c_ctx[None, :], jnp.zeros((rows - bsz - 1, d), F32)], axis=0)
    row2 = lambda a: a.reshape(1, -1)
    moe_w = (moe_w_gate, moe_w_up, moe_w_down)
    moe_ws = (moe_ws_gate, moe_ws_up, moe_ws_down)

    shared = {}

    def layer_weights(li):
        if li not in shared:
            i = li // 2
            ada = _ada_mods(cc, ada_w, ada_b, li)
            w = dict(
                mods=jnp.stack([jnp.broadcast_to(ada[bsz].reshape(1, N_MODS, d), (bsz, N_MODS, d)),
                                ada[:bsz].reshape(bsz, N_MODS, d)], axis=1),
                wrt=jnp.concatenate([moe_router[li].T, jnp.zeros((GATE_W - N_EXPERTS, d), F32)], axis=0),
                bias=moe_bias[li].reshape(N_GROUPS, GROUP_SIZE, 1))
            if li % 2 == 0:
                w["win"], w["wuq"], w["wuk"], w["wuvt"] = _layout_attn_weights(attn_w_in[i], attn_w_uq[i], attn_w_ukv[i])
                w["wo"] = attn_w_o[i].astype(BF16)
            else:
                w["w1"], w["w2"] = _lora_pair(rwkv_w1[i], rwkv_w2[i])
                w["a1"], w["a2"] = _lora_pair(rwkv_a1[i], rwkv_a2[i])
                w["wr"], w["wk"], w["wv"], w["wo"] = [x[i].astype(BF16) for x in (rwkv_w_r, rwkv_w_k, rwkv_w_v, rwkv_w_o)]
                w["g1"], w["g2"] = rwkv_g1[i].astype(BF16), rwkv_g2[i].astype(BF16)
            shared[li] = w
        return shared[li]

    groups = [dict(stream=(ctx, x, g * bg, 0), b0=g * bg) for g in range(ngrp)]
    result = [None]

    def run_stage(st, li, name, dep):
        w = layer_weights(li)
        i = li // 2
        with_ctx = li < depth - 1
        mods = w["mods"][st["b0"]:st["b0"] + bg]
        if name == "proj" and li % 2 == 0:
            st["qkv"] = _attn_proj(st["stream"], bg, l, mods, row2(norm_mix[li]), w["win"], row2(attn_q_norm[i]),
                                   row2(attn_kv_norm[i]), w["wuq"], w["wuk"], w["wuvt"], cos, sin, nct, dep=dep)
            return st["qkv"][0]
        if name == "mid" and li % 2 == 0:
            q, k, vt, qs, ks, vs = st.pop("qkv")
            st["a"] = _mla_attention(q, k, vt, lc, 0 if with_ctx else lc // MLA_Q_TILE, dep=dep)
            st["bm"] = _swa_attention(attn_sinks[i], qs, ks, vs, lc, 0 if with_ctx else lc // SWA_Q_TILE, dep=st["a"])
            return st["bm"]
        if name == "proj":
            assert st["stream"][0] is st["stream"][1]
            st["feat"] = _rwkv_proj(st["stream"][0], mods, row2(norm_mix[li]), rwkv_mu[i], w["wr"], w["wk"], w["wv"],
                                    w["g1"], w["g2"], w["w1"], w["w2"], w["a1"], w["a2"], rwkv_w0[i], rwkv_a0[i],
                                    row2(rwkv_k_k[i]), row2(rwkv_k_a[i]), row2(rwkv_r_k[i]), bd, nct, dep=dep)
            return st["feat"][0]
        if name == "mid":
            r, v, kk, gt, km, bv, lw, bonus = st.pop("feat")
            st["y"] = _wkv(r, v, kk, km, bv, lw, lc, dep=dep)
            st["gate"], st["bonus"] = gt, bonus
            return st["y"]
        if name == "out":
            if li % 2 == 0:
                tail = _attn_out(st.pop("a"), st.pop("bm"), st["stream"], mods, w["wo"], row2(norm_ffn[li]),
                                 w["wrt"], w["bias"], nct, dep=dep)
            else:
                tail = _rwkv_out(st.pop("y"), st.pop("bonus"), st.pop("gate"), row2(rwkv_ln_w[i]), row2(rwkv_ln_b[i]),
                                 w["wo"], bd, st["stream"][0], mods, row2(norm_ffn[li]), w["wrt"], w["bias"], nct, dep=dep)
            st["h"], st["n2p"], eid, rank, st["wcols"], counts = tail
            st["xs"], st["dest"], st["tile_expert"], st["n_valid"] = _moe_route_rows(st["n2p"], eid, rank, counts, bg, l)
            return st["h"]
        if name == "experts":
            ys = _moe_experts(st.pop("tile_expert"), st.pop("n_valid"), st.pop("xs"), *moe_w, li, dep=dep)
            st["yg"] = _sc_gather(ys, st.pop("dest"), bg * l).reshape(TOP_K, bg, l, d // 2)
            return ys
        assert name == "combine"
        last = li == depth - 1
        h = _moe_combine(st.pop("yg"), st.pop("wcols"), st.pop("n2p"), *moe_ws, st.pop("h"), mods, row2(norm_final),
                         nct, li, result[0] if last else None, st["b0"] if last else 0, bsz if last else bg,
                         last, last, dep=dep)
        if last:
            result[0] = h
        else:
            st["stream"] = (h, h, 0, nct)
        return h

    order = [(0, 0, "proj"), (0, 0, "mid")]
    for li in range(depth):
        order += [(0, li, "out"), (1, li, "proj"), (0, li, "experts"), (1, li, "mid")]
        if li < depth - 1:
            order += [(0, li, "combine"), (1, li, "out"), (0, li + 1, "proj"), (1, li, "experts"),
                      (0, li + 1, "mid"), (1, li, "combine")]
        else:
            order += [(1, li, "out"), (0, li, "combine"), (1, li, "experts"), (1, li, "combine")]
    dep = None
    for g, li, name in order:
        dep = run_stage(groups[g], li, name, dep)
    return result[0]
```

```python
import functools

import jax
import jax.numpy as jnp
from jax import lax
from jax.experimental import pallas as pl
from jax.experimental.pallas import tpu as pltpu
from jax.experimental.pallas import tpu_sc as plsc

F32 = jnp.float32
BF16 = jnp.bfloat16
HIGHEST = lax.Precision.HIGHEST

GRID_W = 64
NORM_EPS = 1e-6
ROPE_THETA = 10000.0
NEG_INF = -1e30
N_MODS = 6

MLA_HEADS = 4
MLA_Q_RANK = 384
MLA_KV_RANK = 256
MLA_NOPE = 128
MLA_ROPE = 64
MLA_V = 128

SWA_HEADS = 8
SWA_KV_HEADS = 2
SWA_GROUP = SWA_HEADS // SWA_KV_HEADS
SWA_HEAD_DIM = 64
WINDOW = 128

RWKV_HEAD = 64
DECAY_LORA = 64
ICLR_LORA = 64
GATE_LORA = 128
GN_EPS = 64e-5

N_EXPERTS = 64
TOP_K = 6
N_GROUPS = 8
TOPK_GROUPS = 4
GROUP_SIZE = N_EXPERTS // N_GROUPS
ROUTED_SCALE = 2.5
GATE_W = 128

V7X_LANES = 128
V7X_MXU_DIM = 256
V7X_VMEM_BYTES = 64 * 1024 * 1024
V7X_SC_CORES = 2
V7X_SC_SUBCORES = 16
V7X_SC_WORKERS = V7X_SC_CORES * V7X_SC_SUBCORES

TOKEN_TILE = 256
MLA_Q_TILE = 256
MLA_HEADS_PER_STEP = 2
SWA_Q_TILE = 256
SWA_BAND = SWA_Q_TILE + 2 * WINDOW
WKV_CHUNK = 64
WKV_PAIR = 2 * RWKV_HEAD
WKV_CHUNKS_PER_STEP = 4
MOE_ROW_TILE = 512
MOE_ROW_SLOTS = 3
SAMPLE_GROUPS = 2
SC_MAX_CHUNK = 64

LOG2E = 1.4426950408889634
MIB = 1024 * 1024
VMEM_RESERVE_BYTES = 4 * MIB


def _vmem_limit(mib):
    return min(mib * MIB, V7X_VMEM_BYTES - VMEM_RESERVE_BYTES)


def _dot(a, b):
    return jnp.dot(a.astype(BF16), b.astype(BF16), preferred_element_type=F32)


def _dot_nt(a, b):
    return lax.dot_general(a.astype(BF16), b.astype(BF16), (((1,), (1,)), ((), ())),
                           preferred_element_type=F32)


def _dot_tn(a, b):
    return lax.dot_general(a.astype(BF16), b.astype(BF16), (((0,), (0,)), ((), ())),
                           preferred_element_type=F32)


def _sigmoid(x):
    return 1.0 / (1.0 + jnp.exp(-x))


def _silu(x):
    return x * _sigmoid(x)


def _rms(x, g):
    return x * lax.rsqrt(jnp.mean(x * x, axis=-1, keepdims=True) + NORM_EPS) * g


def _norm_mod(x, g, shift, scale):
    return _rms(x, g) * (1.0 + scale) + shift


def _split_dot(x, w):
    hi = x.astype(BF16)
    lo = (x - hi.astype(F32)).astype(BF16)
    return (jnp.dot(hi, w, preferred_element_type=F32) + jnp.dot(lo, w, preferred_element_type=F32))


def _head_sum(x, bd):
    w = bd.shape[0]
    parts = [_split_dot(x[:, c * w:(c + 1) * w], bd) for c in range(x.shape[1] // w)]
    return jnp.concatenate(parts, axis=1)


def _after(dep, kernel, in_specs, args, n_lead=0):
    if dep is None:
        return kernel, list(in_specs), list(args)
    n_in = n_lead + len(in_specs)

    def ordered(*refs):
        return kernel(*refs[:n_in], *refs[n_in + 1:])

    return ordered, list(in_specs) + [pl.BlockSpec(memory_space=pl.ANY)], list(args) + [dep]


def _ada_kernel(c_ref, w_ref, b_ref, o_ref):
    s = _silu(c_ref[...])
    o_ref[...] = jnp.dot(s, w_ref[0], precision=HIGHEST, preferred_element_type=F32) + b_ref[0]


def _ada_mods(cc, w, b, layer):
    rows, d = cc.shape
    depth, _, n = w.shape
    return pl.pallas_call(
        _ada_kernel,
        out_shape=jax.ShapeDtypeStruct((rows, n), F32),
        grid=(n // d,),
        in_specs=[pl.BlockSpec((rows, d), lambda i: (0, 0)),
                  pl.BlockSpec((1, d, d), lambda i: (layer, 0, i)),
                  pl.BlockSpec((1, 1, d), lambda i: (layer, 0, i))],
        out_specs=pl.BlockSpec((rows, d), lambda i: (0, i)),
        compiler_params=pltpu.CompilerParams(dimension_semantics=("parallel",),
                                             vmem_limit_bytes=_vmem_limit(32)),
        name="ada_mods",
    )(cc, w, b.reshape(depth, 1, n))


def _rope128(x, cos, sin, first_half):
    rot = jnp.where(first_half, -pltpu.roll(x, V7X_LANES - 16, axis=1), pltpu.roll(x, 16, axis=1))
    return x * cos + rot * sin


_C_CQ = 0
_C_CKV = _C_CQ + MLA_Q_RANK
_C_QS = _C_CKV + MLA_KV_RANK
_C_KS = _C_QS + SWA_HEADS * SWA_HEAD_DIM
_C_VS = _C_KS + SWA_KV_HEADS * V7X_MXU_DIM
_C_KR = _C_VS + SWA_KV_HEADS * V7X_MXU_DIM
_C_END = _C_KR + V7X_LANES
_SWA_W = SWA_KV_HEADS * V7X_MXU_DIM
_MLA_QK_W = MLA_HEADS * V7X_MXU_DIM


def _stream_specs(stream, nct, tl):
    ctx_arr, lat_arr, b0, lat_off = stream
    d = ctx_arr.shape[2]
    return [pl.BlockSpec((1, tl, d), lambda i, j: (i + b0, jnp.minimum(j, nct - 1), 0)),
            pl.BlockSpec((1, tl, d), lambda i, j: (i + b0, jnp.maximum(j - nct, 0) + lat_off, 0))]


def _stream_tile(c_ref, x_ref, nct):
    rows = c_ref.shape[1]
    take_ctx = lax.broadcasted_iota(jnp.int32, (rows, 1), 0) < jnp.where(pl.program_id(1) < nct, rows, 0)
    return jnp.where(take_ctx, c_ref[0], x_ref[0])


def _attn_proj_kernel(c_ref, x_ref, mods_ref, g_ref, win_ref, qn_ref, kvn_ref, wuq_ref, wuk_ref, wuvt_ref, cos_ref,
                      sin_ref, q_ref, k_ref, vt_ref, qs_ref, ks_ref, vs_ref, *, nct):
    m = mods_ref[0, 0]
    n = _norm_mod(_stream_tile(c_ref, x_ref, nct), g_ref[...], m[0:1], m[1:2])
    u = _dot(n, win_ref[...])
    cos = cos_ref[...]
    sin = sin_ref[...]
    lane = lax.broadcasted_iota(jnp.int32, (1, V7X_LANES), 1)
    first_half = (lane % 32) < 16

    def rope(x):
        return _rope128(x, cos, sin, first_half)

    scale_a = (MLA_NOPE + MLA_ROPE) ** -0.5 * LOG2E
    scale_b = SWA_HEAD_DIM ** -0.5 * LOG2E
    q = _dot(_rms(u[:, _C_CQ:_C_CKV], qn_ref[...]), wuq_ref[...])
    ckv = _rms(u[:, _C_CKV:_C_QS], kvn_ref[...])
    kn = _dot(ckv, wuk_ref[...])
    vt_ref[0] = _dot_nt(wuvt_ref[...], ckv).astype(BF16)
    kr = rope(u[:, _C_KR:_C_END]).astype(BF16)
    for h in range(MLA_HEADS):
        o = h * V7X_MXU_DIM
        q_ref[0, :, o:o + V7X_LANES] = (q[:, o:o + V7X_LANES] * scale_a).astype(BF16)
        q_ref[0, :, o + V7X_LANES:o + V7X_MXU_DIM] = (rope(q[:, o + V7X_LANES:o + V7X_MXU_DIM]) * scale_a).astype(BF16)
        k_ref[0, :, o:o + V7X_LANES] = kn[:, h * MLA_NOPE:(h + 1) * MLA_NOPE].astype(BF16)
        k_ref[0, :, o + V7X_LANES:o + V7X_MXU_DIM] = kr
    for c in range((_C_KS - _C_QS) // V7X_LANES):
        o = c * V7X_LANES
        qs_ref[0, :, o:o + V7X_LANES] = (rope(u[:, _C_QS + o:_C_QS + o + V7X_LANES]) * scale_b).astype(BF16)
    for c in range(_SWA_W // V7X_LANES):
        o = c * V7X_LANES
        ks_ref[0, :, o:o + V7X_LANES] = rope(u[:, _C_KS + o:_C_KS + o + V7X_LANES]).astype(BF16)
    vs_ref[0] = u[:, _C_VS:_C_KR].astype(BF16)


def _attn_proj(stream, b, l, mods, g, win, qn, kvn, wuq, wuk, wuvt, cos, sin, nct, dep=None):
    d = stream[0].shape[2]
    tl = TOKEN_TILE
    tok = lambda w: pl.BlockSpec((1, tl, w), lambda i, j: (i, j, 0))
    full = lambda a: pl.BlockSpec(a.shape, lambda i, j: (0,) * a.ndim)
    sds = jax.ShapeDtypeStruct
    dv = MLA_HEADS * MLA_V
    kern, in_specs, args = _after(
        dep, functools.partial(_attn_proj_kernel, nct=nct),
        _stream_specs(stream, nct, tl) + [
            pl.BlockSpec((1, 1, N_MODS, d), lambda i, j: (i, jnp.where(j < nct, 0, 1), 0, 0)),
            full(g), full(win), full(qn), full(kvn), full(wuq), full(wuk), full(wuvt),
            pl.BlockSpec((tl, V7X_LANES), lambda i, j: (j, 0)),
            pl.BlockSpec((tl, V7X_LANES), lambda i, j: (j, 0))],
        [stream[0], stream[1], mods, g, win, qn, kvn, wuq, wuk, wuvt, cos, sin])
    return pl.pallas_call(
        kern,
        out_shape=[sds((b, l, _MLA_QK_W), BF16), sds((b, l, _MLA_QK_W), BF16), sds((b, dv, l), BF16),
                   sds((b, l, SWA_HEADS * SWA_HEAD_DIM), BF16), sds((b, l, _SWA_W), BF16), sds((b, l, _SWA_W), BF16)],
        grid=(b, l // tl),
        in_specs=in_specs,
        out_specs=[tok(_MLA_QK_W), tok(_MLA_QK_W), pl.BlockSpec((1, dv, tl), lambda i, j: (i, 0, j)),
                   tok(SWA_HEADS * SWA_HEAD_DIM), tok(_SWA_W), tok(_SWA_W)],
        compiler_params=pltpu.CompilerParams(dimension_semantics=("parallel", "parallel"),
                                             vmem_limit_bytes=_vmem_limit(48)),
        name="attn_proj",
    )(*args)


def _mla_kernel(q_ref, k_ref, vt_ref, o_ref, *, nct_q, lc):
    hw = V7X_MXU_DIM

    def attend(nk):
        st = [_dot_nt(k_ref[0, 0:nk, hh * hw:(hh + 1) * hw], q_ref[0, :, hh * hw:(hh + 1) * hw])
              for hh in range(MLA_HEADS_PER_STEP)]
        for hh, s in enumerate(st):
            p = jnp.exp2(s - jnp.max(s, axis=0, keepdims=True))
            den = jnp.sum(p, axis=0, keepdims=True)
            ot = _dot(vt_ref[0, hh * MLA_V:(hh + 1) * MLA_V, 0:nk], p) / den
            o_ref[0, :, hh * MLA_V:(hh + 1) * MLA_V] = ot.T.astype(o_ref.dtype)

    @pl.when(pl.program_id(2) < nct_q)
    def _():
        attend(lc)

    @pl.when(pl.program_id(2) >= nct_q)
    def _():
        attend(k_ref.shape[1])


def _mla_attention(q, k, vt, lc, q_tile0, dep=None):
    b, l, _ = q.shape
    tq = MLA_Q_TILE
    hps = MLA_HEADS_PER_STEP
    kern, in_specs, args = _after(
        dep, functools.partial(_mla_kernel, nct_q=lc // tq - q_tile0, lc=lc),
        [pl.BlockSpec((1, tq, hps * V7X_MXU_DIM), lambda i, h, j: (i, j + q_tile0, h)),
         pl.BlockSpec((1, l, hps * V7X_MXU_DIM), lambda i, h, j: (i, 0, h)),
         pl.BlockSpec((1, hps * MLA_V, l), lambda i, h, j: (i, h, 0))],
        [q, k, vt])
    return pl.pallas_call(
        kern,
        out_shape=jax.ShapeDtypeStruct((b, l, MLA_HEADS * MLA_V), BF16),
        grid=(b, MLA_HEADS // hps, l // tq - q_tile0),
        in_specs=in_specs,
        out_specs=pl.BlockSpec((1, tq, hps * MLA_V), lambda i, h, j: (i, j + q_tile0, h)),
        compiler_params=pltpu.CompilerParams(dimension_semantics=("parallel", "parallel", "parallel"),
                                             vmem_limit_bytes=_vmem_limit(48)),
        name="mla_attention",
    )(*args)


def _swa_kernel(sink_ref, q_ref, k_ref, v_ref, o_ref, *, lc, q_tile0):
    tq = SWA_Q_TILE
    l = k_ref.shape[1]
    r0 = (pl.program_id(1) + q_tile0) * tq
    start = pl.multiple_of(jnp.clip(r0 - WINDOW, lc, l - SWA_BAND), WINDOW)
    rows = SWA_GROUP * tq
    row = lax.broadcasted_iota(jnp.int32, (rows, 1), 0)
    qpos = jnp.where(r0 >= lc, r0, -l) + row % tq
    kpos = start + lax.broadcasted_iota(jnp.int32, (1, SWA_BAND), 1)
    valid = jnp.abs(qpos - kpos) <= WINDOW
    lane = lax.broadcasted_iota(jnp.int32, (1, V7X_MXU_DIM), 1)
    head = [(lane // SWA_HEAD_DIM) == hh for hh in range(SWA_GROUP)]
    groups = range(SWA_KV_HEADS)
    sls = [slice(g * V7X_MXU_DIM, (g + 1) * V7X_MXU_DIM) for g in groups]
    qstack = []
    for sl in sls:
        qg = q_ref[0, :, sl]
        zero = jnp.zeros_like(qg)
        qstack.append(jnp.concatenate([jnp.where(head[hh], qg, zero) for hh in range(SWA_GROUP)], axis=0))
    sc = [_dot_nt(qstack[g], k_ref[0, 0:lc, sls[g]]) for g in groups]
    sb = [_dot_nt(qstack[g], k_ref[0, pl.ds(start, SWA_BAND), sls[g]]) for g in groups]
    for g in groups:
        sl = sls[g]
        sbm = jnp.where(valid, sb[g], NEG_INF)
        sk = jnp.zeros((rows, 1), F32)
        for hh in range(SWA_GROUP):
            sk = jnp.where(row // tq == hh, sink_ref[g * SWA_GROUP + hh] * LOG2E, sk)
        mx = jnp.maximum(jnp.maximum(jnp.max(sc[g], axis=-1, keepdims=True), jnp.max(sbm, axis=-1, keepdims=True)), sk)
        pc = jnp.exp2(sc[g] - mx)
        pb = jnp.exp2(sbm - mx)
        den = jnp.sum(pc, axis=-1, keepdims=True) + jnp.sum(pb, axis=-1, keepdims=True) + jnp.exp2(sk - mx)
        ostack = (_dot(pc, v_ref[0, 0:lc, sl]) + _dot(pb, v_ref[0, pl.ds(start, SWA_BAND), sl])) / den
        o = jnp.zeros((tq, V7X_MXU_DIM), F32)
        for hh in range(SWA_GROUP):
            o = o + jnp.where(head[hh], ostack[hh * tq:(hh + 1) * tq], 0.0)
        o_ref[0, :, sl] = o.astype(o_ref.dtype)


def _swa_attention(sinks, q, k, v, lc, q_tile0, dep=None):
    b, l, _ = q.shape
    tq = SWA_Q_TILE
    kern, in_specs, args = _after(
        dep, functools.partial(_swa_kernel, lc=lc, q_tile0=q_tile0),
        [pl.BlockSpec(memory_space=pltpu.SMEM),
         pl.BlockSpec((1, tq, SWA_HEADS * SWA_HEAD_DIM), lambda i, j: (i, j + q_tile0, 0)),
         pl.BlockSpec((1, l, _SWA_W), lambda i, j: (i, 0, 0)),
         pl.BlockSpec((1, l, _SWA_W), lambda i, j: (i, 0, 0))],
        [sinks, q, k, v])
    return pl.pallas_call(
        kern,
        out_shape=jax.ShapeDtypeStruct((b, l, SWA_HEADS * SWA_HEAD_DIM), BF16),
        grid=(b, l // tq - q_tile0),
        in_specs=in_specs,
        out_specs=pl.BlockSpec((1, tq, SWA_HEADS * SWA_HEAD_DIM), lambda i, j: (i, j + q_tile0, 0)),
        compiler_params=pltpu.CompilerParams(dimension_semantics=("parallel", "parallel"),
                                             vmem_limit_bytes=_vmem_limit(48)),
        name="swa_attention",
    )(*args)


def _pack_bf16_pair(x):
    w = x.shape[1] // 2
    lo = pltpu.bitcast(x[:, :w].astype(BF16).astype(F32), jnp.int32)
    hi = pltpu.bitcast(x[:, w:].astype(BF16).astype(F32), jnp.int32)
    return lax.shift_right_logical(lo, jnp.int32(16)) | (hi & jnp.int32(-65536))


def _unpack_bf16_pair(p):
    return pltpu.bitcast(p << 16, F32), pltpu.bitcast(p & jnp.int32(-65536), F32)


def _route(n2, wrt, bias, run_ref):
    n_hi = n2.astype(BF16)
    n_lo = (n2 - n_hi.astype(F32)).astype(BF16)
    w_hi = wrt.astype(BF16)
    w_lo = (wrt - w_hi.astype(F32)).astype(BF16)
    w_both = jnp.concatenate([w_hi, w_lo], axis=0)
    rows = n2.shape[0]
    score_blocks = []
    for o in range(0, rows, V7X_LANES):
        both = _dot_nt(w_both, n_hi[o:o + V7X_LANES])
        logits = both[:N_EXPERTS] + both[N_EXPERTS:] + _dot_nt(w_hi, n_lo[o:o + V7X_LANES])
        score_blocks.append(_sigmoid(logits))
    scores = jnp.concatenate(score_blocks, axis=1)

    def select(sc2):
        cols = sc2.shape[1]
        shape3 = (N_GROUPS, GROUP_SIZE, cols)
        choice = sc2.reshape(shape3) + bias
        ji = lax.broadcasted_iota(jnp.int32, shape3, 1).astype(F32)
        m1 = jnp.max(choice, axis=1, keepdims=True)
        first = jnp.min(jnp.where(choice == m1, ji, float(GROUP_SIZE)), axis=1, keepdims=True)
        m2 = jnp.max(jnp.where(ji == first, -jnp.inf, choice), axis=1, keepdims=True)
        gs = m1 + m2
        gidx = lax.broadcasted_iota(jnp.int32, gs.shape, 0).astype(F32)
        gsel = jnp.zeros_like(gs)
        for _ in range(TOPK_GROUPS):
            mx = jnp.max(gs, axis=0, keepdims=True)
            pick = gidx == jnp.min(jnp.where(gs == mx, gidx, float(N_GROUPS)), axis=0, keepdims=True)
            gsel = jnp.where(pick, 1.0, gsel)
            gs = jnp.where(pick, -jnp.inf, gs)
        cand = jnp.where(gsel > 0.0, choice, -jnp.inf).reshape(N_EXPERTS, cols)
        eidx = lax.broadcasted_iota(jnp.int32, (N_EXPERTS, cols), 0).astype(F32)
        out = []
        for _ in range(TOP_K):
            mx = jnp.max(cand, axis=0, keepdims=True)
            pick = eidx == jnp.min(jnp.where(cand == mx, eidx, float(N_EXPERTS)), axis=0, keepdims=True)
            out.append(jnp.where(pick, 1.0, 0.0))
            cand = jnp.where(pick, -jnp.inf, cand)
        return out

    blocks = [select(sc2) for sc2 in score_blocks]
    picks = [jnp.concatenate([blk[k] for blk in blocks], axis=1) > 0.0 for k in range(TOP_K)]
    ei = lax.broadcasted_iota(jnp.int32, (N_EXPERTS, rows), 0).astype(F32)
    esel = jnp.zeros((N_EXPERTS, rows), F32)
    for pick in picks:
        esel = jnp.where(pick, 1.0, esel)
    before = jnp.where(lax.broadcasted_iota(jnp.int32, (rows, rows), 0) < lax.broadcasted_iota(jnp.int32, (rows, rows), 1),
                       1.0, 0.0).astype(BF16)
    slot = jnp.dot(esel.astype(BF16), before, preferred_element_type=F32) + run_ref[...]
    run_ref[...] += jnp.sum(esel, axis=1, keepdims=True)
    sc = [jnp.sum(jnp.where(pick, scores, 0.0), axis=0, keepdims=True) for pick in picks]
    tot = sc[0]
    for x in sc[1:]:
        tot = tot + x
    k8 = lax.broadcasted_iota(jnp.int32, (8, rows), 0)
    kw = lax.broadcasted_iota(jnp.int32, (GATE_W, rows), 0)
    eid = jnp.zeros((8, rows), jnp.int32)
    rank = jnp.zeros((8, rows), jnp.int32)
    wk = jnp.zeros((GATE_W, rows), F32)
    for k, pick in enumerate(picks):
        e_k = jnp.sum(jnp.where(pick, ei, 0.0), axis=0, keepdims=True).astype(jnp.int32)
        r_k = jnp.sum(jnp.where(pick, slot, 0.0), axis=0, keepdims=True).astype(jnp.int32)
        eid = jnp.where(k8 == k, e_k, eid)
        rank = jnp.where(k8 == k, r_k, rank)
        wk = jnp.where(kw == k, sc[k] * (ROUTED_SCALE / tot), wk)
    return eid, rank, wk.T


def _mixer_tail(o, h, m, gffn_ref, wrt_ref, bias_ref, hn_ref, n2_ref, eid_ref, rank_ref, w_ref, cnt_ref, run_ref):
    @pl.when((pl.program_id(0) == 0) & (pl.program_id(1) == 0))
    def _():
        run_ref[...] = jnp.zeros_like(run_ref)

    hn = h + m[2:3] * o
    hn_ref[0] = hn
    n2 = _norm_mod(hn, gffn_ref[...], m[3:4], m[4:5])
    n2_ref[0] = _pack_bf16_pair(n2)
    eid, rank, wcols = _route(n2, wrt_ref[...], bias_ref[...], run_ref)
    eid_ref[0] = eid
    rank_ref[0] = rank
    w_ref[0] = wcols
    cnt_ref[...] = run_ref[...]


def _attn_out_kernel(a_ref, b_ref, c_ref, x_ref, mods_ref, wo_ref, gffn_ref, wrt_ref, bias_ref,
                     hn_ref, n2_ref, eid_ref, rank_ref, w_ref, cnt_ref, run_ref, *, nct):
    wa = MLA_HEADS * MLA_V
    o = _dot(a_ref[0], wo_ref[0:wa, :]) + _dot(b_ref[0], wo_ref[wa:, :])
    _mixer_tail(o, _stream_tile(c_ref, x_ref, nct), mods_ref[0, 0], gffn_ref, wrt_ref, bias_ref, hn_ref, n2_ref,
                eid_ref, rank_ref, w_ref, cnt_ref, run_ref)


def _tail_outs(b, l, d):
    tl = TOKEN_TILE
    nt = l // tl
    sds = jax.ShapeDtypeStruct
    tok = lambda w: pl.BlockSpec((1, tl, w), lambda i, j: (i, j, 0))
    blk = pl.BlockSpec((1, 8, tl), lambda i, j: (i * nt + j, 0, 0))
    shapes = [sds((b, l, d), F32), sds((b, l, d // 2), jnp.int32), sds((b * nt, 8, tl), jnp.int32),
              sds((b * nt, 8, tl), jnp.int32), sds((b, l, GATE_W), F32), sds((N_EXPERTS, 1), F32)]
    specs = [tok(d), tok(d // 2), blk, blk, tok(GATE_W), pl.BlockSpec((N_EXPERTS, 1), lambda i, j: (0, 0))]
    return shapes, specs


def _attn_out(a, bm, stream, mods, wo, gffn, wrt, bias, nct, dep=None):
    b, l, _ = a.shape
    d = stream[0].shape[2]
    tl = TOKEN_TILE
    tok = lambda w: pl.BlockSpec((1, tl, w), lambda i, j: (i, j, 0))
    full = lambda x: pl.BlockSpec(x.shape, lambda i, j: (0,) * x.ndim)
    shapes, specs = _tail_outs(b, l, d)
    kern, in_specs, args = _after(
        dep, functools.partial(_attn_out_kernel, nct=nct),
        [tok(a.shape[2]), tok(bm.shape[2])] + _stream_specs(stream, nct, tl) + [
            pl.BlockSpec((1, 1, N_MODS, d), lambda i, j: (i, jnp.where(j < nct, 0, 1), 0, 0)),
            full(wo), full(gffn), full(wrt), full(bias)],
        [a, bm, stream[0], stream[1], mods, wo, gffn, wrt, bias])
    return pl.pallas_call(
        kern,
        out_shape=shapes,
        grid=(b, l // tl),
        in_specs=in_specs,
        out_specs=specs,
        scratch_shapes=[pltpu.VMEM((N_EXPERTS, 1), F32)],
        compiler_params=pltpu.CompilerParams(dimension_semantics=("arbitrary", "arbitrary"),
                                             vmem_limit_bytes=_vmem_limit(40)),
        name="attn_out",
    )(*args)


def _moe_dest_kernel(off_ref, eid_ref, rank_ref, dest_ref):
    eid = eid_ref[...]
    dest = rank_ref[...]
    for e in range(N_EXPERTS):
        dest = dest + jnp.where(eid == e, off_ref[e], 0)
    dest_ref[...] = dest


def _moe_dest(off, eid, rank):
    return pl.pallas_call(
        _moe_dest_kernel,
        out_shape=jax.ShapeDtypeStruct(eid.shape, jnp.int32),
        in_specs=[pl.BlockSpec(memory_space=pltpu.SMEM),
                  pl.BlockSpec(eid.shape, lambda: (0, 0, 0)), pl.BlockSpec(eid.shape, lambda: (0, 0, 0))],
        out_specs=pl.BlockSpec(eid.shape, lambda: (0, 0, 0)),
        name="moe_dest",
    )(off, eid, rank)


def _sc_mesh():
    return plsc.VectorSubcoreMesh(core_axis_name="c", subcore_axis_name="s",
                                  num_cores=V7X_SC_CORES, num_subcores=V7X_SC_SUBCORES)


def _sc_chunk(rows_per_worker):
    return max(c for c in range(8, SC_MAX_CHUNK + 1, 8) if rows_per_worker % c == 0)


def _sc_dispatch(xp, dest, p_rows):
    t, w = xp.shape
    tpw = t // V7X_SC_WORKERS
    ch = _sc_chunk(tpw)

    @functools.partial(
        pl.kernel, mesh=_sc_mesh(), out_type=jax.ShapeDtypeStruct((p_rows, w), xp.dtype),
        scratch_types=[pltpu.VMEM((ch, w), xp.dtype)] + [pltpu.VMEM((ch,), jnp.int32)] * TOP_K
        + [pltpu.SemaphoreType.DMA, pltpu.SemaphoreType.DMA],
        name="moe_dispatch")
    def run(x_hbm, dest_hbm, out_hbm, rows_v, *rest):
        idx, (sem_i, sem_o) = rest[:TOP_K], rest[TOP_K:]
        base = (lax.axis_index("s") * V7X_SC_CORES + lax.axis_index("c")) * tpw

        @pl.loop(0, tpw // ch)
        def _(i):
            t0 = base + i * ch
            loads = [pltpu.async_copy(dest_hbm.at[k, pl.ds(t0, ch)], idx[k], sem_i) for k in range(TOP_K)]
            pltpu.sync_copy(x_hbm.at[pl.ds(t0, ch)], rows_v)
            for c in loads:
                c.wait()
            puts = [pltpu.async_copy(rows_v, out_hbm.at[idx[k]], sem_o) for k in range(TOP_K)]
            for c in puts:
                c.wait()

    return run(xp, dest)


def _sc_gather(ys, dest, t):
    w = ys.shape[1]
    tpw = t // V7X_SC_WORKERS
    ch = _sc_chunk(tpw)

    @functools.partial(
        pl.kernel, mesh=_sc_mesh(), out_type=jax.ShapeDtypeStruct((TOP_K, t, w), ys.dtype),
        scratch_types=[pltpu.VMEM((ch, w), ys.dtype)] * 2 + [pltpu.VMEM((ch,), jnp.int32)] * TOP_K
        + [pltpu.SemaphoreType.DMA] * 5,
        name="moe_gather")
    def run(y_hbm, dest_hbm, out_hbm, rows_a, rows_b, *rest):
        idx, (sem_i, sem_ga, sem_gb, sem_wa, sem_wb) = rest[:TOP_K], rest[TOP_K:]
        rows, sem_g, sem_w = (rows_a, rows_b), (sem_ga, sem_gb), (sem_wa, sem_wb)
        base = (lax.axis_index("s") * V7X_SC_CORES + lax.axis_index("c")) * tpw

        @pl.loop(0, tpw // ch)
        def _(i):
            t0 = base + i * ch
            loads = [pltpu.async_copy(dest_hbm.at[k, pl.ds(t0, ch)], idx[k], sem_i) for k in range(TOP_K)]
            for c in loads:
                c.wait()
            gets, puts = [None] * TOP_K, [None] * TOP_K
            gets[0] = pltpu.async_copy(y_hbm.at[idx[0]], rows[0], sem_g[0])
            for k in range(TOP_K):
                if k + 1 < TOP_K:
                    if k >= 1:
                        puts[k - 1].wait()
                    gets[k + 1] = pltpu.async_copy(y_hbm.at[idx[k + 1]], rows[(k + 1) % 2], sem_g[(k + 1) % 2])
                gets[k].wait()
                puts[k] = pltpu.async_copy(rows[k % 2], out_hbm.at[k, pl.ds(t0, ch)], sem_w[k % 2])
            puts[TOP_K - 2].wait()
            puts[TOP_K - 1].wait()

    return run(ys, dest)


def _cache_mlp_weights(wg, wu, wd, wgu_ref, wdb_ref):
    f = wg.shape[1]
    wgu_ref[:, 0:f] = wg.astype(BF16)
    wgu_ref[:, f:] = wu.astype(BF16)
    wdb_ref[...] = wd.astype(BF16)


def _gated_mlp(xp, wgu_ref, wdb_ref):
    lo, hi = _unpack_bf16_pair(xp)
    x = jnp.concatenate([lo.astype(BF16), hi.astype(BF16)], axis=1)
    gu = jnp.dot(x, wgu_ref[...], preferred_element_type=F32)
    f = gu.shape[1] // 2
    return _dot(_silu(gu[:, :f]) * gu[:, f:], wdb_ref[...])


def _moe_expert_kernel(te_ref, tb_ref, nv_ref, x_hbm, wga_ref, wua_ref, wda_ref, wgb_ref, wub_ref, wdb_ref, y_ref,
                       gu_a, dn_a, gu_b, dn_b, ids_ref, xbuf, sems):
    i = pl.program_id(0)
    tm = MOE_ROW_TILE
    nv = nv_ref[0]
    last = (nv - 1) // 2
    first = 2 * jnp.minimum(i, last)
    ea = te_ref[first]
    eb = te_ref[first + 1]
    two = 2 * i + 1 < nv

    def rows_copy(step):
        slot = step % MOE_ROW_SLOTS
        row0 = step * (2 * tm)
        rows = pl.ds(row0 if isinstance(step, int) else pl.multiple_of(row0, 2 * tm), 2 * tm)
        return pltpu.make_async_copy(x_hbm.at[rows], xbuf.at[slot], sems.at[slot])

    @pl.when(i == 0)
    def _():
        ids_ref[0] = -1
        ids_ref[1] = -1
        for ahead in range(MOE_ROW_SLOTS - 1):
            @pl.when(ahead <= last)
            def _():
                rows_copy(ahead).start()

    @pl.when(i + (MOE_ROW_SLOTS - 1) <= last)
    def _():
        rows_copy(i + (MOE_ROW_SLOTS - 1)).start()

    @pl.when(i <= last)
    def _():
        rows_copy(i).wait()

    x_ref = xbuf.at[i % MOE_ROW_SLOTS]

    @pl.when(ids_ref[0] != ea)
    def _():
        _cache_mlp_weights(wga_ref[0, 0], wua_ref[0, 0], wda_ref[0, 0], gu_a, dn_a)
        ids_ref[0] = ea

    @pl.when(two & (eb != ea) & (ids_ref[1] != eb))
    def _():
        _cache_mlp_weights(wgb_ref[0, 0], wub_ref[0, 0], wdb_ref[0, 0], gu_b, dn_b)
        ids_ref[1] = eb

    @pl.when(two & (eb == ea))
    def _():
        y_ref[...] = _pack_bf16_pair(_gated_mlp(x_ref[...], gu_a, dn_a))

    @pl.when((2 * i < nv) & jnp.logical_not(two & (eb == ea)))
    def _():
        y_ref[0:tm, :] = _pack_bf16_pair(_gated_mlp(x_ref[0:tm, :], gu_a, dn_a))

    @pl.when(two & (eb != ea))
    def _():
        y_ref[tm:, :] = _pack_bf16_pair(_gated_mlp(x_ref[tm:, :], gu_b, dn_b))


def _moe_experts(tile_expert, n_valid, xs, wg, wu, wd, layer, dep=None):
    p_rows, w = xs.shape
    tm = MOE_ROW_TILE
    _, _, d, f = wg.shape
    npair = p_rows // (2 * tm)
    pairs = tile_expert.reshape(npair, 2)
    tile_b = jnp.maximum(lax.cummax(jnp.where(pairs[:, 1] != pairs[:, 0], pairs[:, 1], -1)), 0)
    step = lambda i, nv: jnp.minimum(i, (nv[0] - 1) // 2)
    spec_a = lambda shp: pl.BlockSpec((1, 1) + shp, lambda i, te, tb, nv: (layer, te[2 * step(i, nv)], 0, 0))
    spec_b = lambda shp: pl.BlockSpec((1, 1) + shp, lambda i, te, tb, nv: (layer, tb[step(i, nv)], 0, 0))
    rows = pl.BlockSpec((2 * tm, w), lambda i, te, tb, nv: (step(i, nv), 0))
    kern, in_specs, args = _after(
        dep, _moe_expert_kernel,
        [pl.BlockSpec(memory_space=pl.ANY), spec_a((d, f)), spec_a((d, f)), spec_a((f, d)),
         spec_b((d, f)), spec_b((d, f)), spec_b((f, d))],
        [tile_expert, tile_b, n_valid, xs, wg, wu, wd, wg, wu, wd], n_lead=3)
    return pl.pallas_call(
        kern,
        out_shape=jax.ShapeDtypeStruct((p_rows, w), xs.dtype),
        grid_spec=pltpu.PrefetchScalarGridSpec(
            num_scalar_prefetch=3, grid=(npair,),
            in_specs=in_specs,
            out_specs=rows,
            scratch_shapes=[pltpu.VMEM((d, 2 * f), BF16), pltpu.VMEM((f, d), BF16),
                            pltpu.VMEM((d, 2 * f), BF16), pltpu.VMEM((f, d), BF16), pltpu.SMEM((2,), jnp.int32),
                            pltpu.VMEM((MOE_ROW_SLOTS, 2 * tm, w), xs.dtype),
                            pltpu.SemaphoreType.DMA((MOE_ROW_SLOTS,))]),
        compiler_params=pltpu.CompilerParams(dimension_semantics=("arbitrary",),
                                             vmem_limit_bytes=_vmem_limit(48)),
        name="moe_experts",
    )(*args)


def _moe_combine_kernel(yg_hbm, w_ref, xp_ref, sg_ref, su_ref, sd_ref, h_ref, mods_ref, gfin_ref, *rest, final_norm,
                        tile0):
    o_ref, wgu_ref, wdb_ref, ybuf, sems = rest[-5:]
    tl = ybuf.shape[2]
    nt = pl.num_programs(1)
    step = pl.program_id(0) * nt + pl.program_id(1)
    n_steps = pl.num_programs(0) * nt

    def yg_copy(s):
        rows = pl.ds(pl.multiple_of((s % nt + tile0) * tl, tl), tl)
        return pltpu.make_async_copy(yg_hbm.at[:, s // nt, rows, :], ybuf.at[s % MOE_ROW_SLOTS],
                                     sems.at[s % MOE_ROW_SLOTS])

    @pl.when(step == 0)
    def _():
        _cache_mlp_weights(sg_ref[0], su_ref[0], sd_ref[0], wgu_ref, wdb_ref)
        for ahead in range(MOE_ROW_SLOTS - 1):
            @pl.when(ahead < n_steps)
            def _():
                yg_copy(ahead).start()

    @pl.when(step + (MOE_ROW_SLOTS - 1) < n_steps)
    def _():
        yg_copy(step + (MOE_ROW_SLOTS - 1)).start()

    yg_copy(step).wait()
    yg_ref = ybuf.at[step % MOE_ROW_SLOTS]

    acc = _gated_mlp(xp_ref[0], wgu_ref, wdb_ref)
    half = acc.shape[1] // 2
    lo = acc[:, :half]
    hi = acc[:, half:]
    w = w_ref[0]
    for k in range(TOP_K):
        ylo, yhi = _unpack_bf16_pair(yg_ref[k])
        wk = w[:, k:k + 1]
        lo = lo + wk * ylo
        hi = hi + wk * yhi
    y = h_ref[0] + mods_ref[0, 0, N_MODS - 1:N_MODS, :] * jnp.concatenate([lo, hi], axis=1)
    if final_norm:
        y = _rms(y, gfin_ref[...])
    o_ref[0] = y


def _moe_combine(yg, wcols, xp, sg, su, sd, h, mods, gfin, nct, layer, out_buf, out_b0, out_batch, latent_only,
                 final_norm, dep=None):
    b, l, d = h.shape
    tl = TOKEN_TILE
    tile0 = nct if latent_only else 0
    tok = lambda w: pl.BlockSpec((1, tl, w), lambda i, j: (i, j + tile0, 0))
    lay = lambda x: pl.BlockSpec((1,) + x.shape[1:], lambda i, j: (layer,) + (0,) * (x.ndim - 1))
    args = [yg, wcols, xp, sg, su, sd, h, mods, gfin]
    in_specs = [pl.BlockSpec(memory_space=pl.ANY), tok(GATE_W), tok(d // 2),
                lay(sg), lay(su), lay(sd), tok(d),
                pl.BlockSpec((1, 1, N_MODS, d), lambda i, j: (i, jnp.where(j + tile0 < nct, 0, 1), 0, 0)),
                pl.BlockSpec(gfin.shape, lambda i, j: (0, 0))]
    _, in_specs, args = _after(dep, None, in_specs, args)
    aliases = {}
    if out_buf is not None:
        args.append(out_buf)
        in_specs.append(pl.BlockSpec(memory_space=pl.ANY))
        aliases = {len(args) - 1: 0}
    return pl.pallas_call(
        functools.partial(_moe_combine_kernel, final_norm=final_norm, tile0=tile0),
        out_shape=jax.ShapeDtypeStruct((out_batch, l - tile0 * tl, d), F32),
        grid=(b, l // tl - tile0),
        in_specs=in_specs,
        out_specs=pl.BlockSpec((1, tl, d), lambda i, j: (i + out_b0, j, 0)),
        scratch_shapes=[pltpu.VMEM((d, 2 * sg.shape[2]), BF16), pltpu.VMEM((sg.shape[2], d), BF16),
                        pltpu.VMEM((MOE_ROW_SLOTS, TOP_K, tl, d // 2), yg.dtype),
                        pltpu.SemaphoreType.DMA((MOE_ROW_SLOTS,))],
        input_output_aliases=aliases,
        compiler_params=pltpu.CompilerParams(dimension_semantics=("arbitrary", "arbitrary"),
                                             vmem_limit_bytes=_vmem_limit(40)),
        name="moe_combine",
    )(*args)


def _moe_route_rows(n2p, eid, rank, counts, b, l):
    d2 = n2p.shape[2]
    t = b * l
    tm = MOE_ROW_TILE
    n_tiles = 2 * -(-(TOP_K * t + N_EXPERTS * (tm - 1)) // (2 * tm))
    tiles_e = (counts.reshape(N_EXPERTS).astype(jnp.int32) + (tm - 1)) // tm
    tile_end = jnp.cumsum(tiles_e)
    off = (tile_end - tiles_e) * tm
    n_valid = tile_end[-1:]
    tile_id = jnp.minimum(jnp.arange(n_tiles, dtype=jnp.int32), n_valid - 1)
    tile_expert = jnp.sum((tile_end[None, :] <= tile_id[:, None]).astype(jnp.int32), axis=1)
    dest = _moe_dest(off, eid, rank).transpose(1, 0, 2).reshape(8, t)
    xs = _sc_dispatch(n2p.reshape(t, d2), dest, n_tiles * tm)
    return xs, dest, tile_expert, n_valid


def _rwkv_proj_kernel(h_ref, hp_ref, hx_ref, mods_ref, g_ref, mu_ref, wr_ref, wk_ref, wv_ref, g1_ref, g2_ref,
                      w1_ref, w2_ref, a1_ref, a2_ref, w0_ref, a0_ref, kk_ref, ka_ref, rk_ref, bd_ref,
                      r_out, v_out, kk_out, g_out, km_out, b_out, lw_out, bonus_out, *, nct):
    j = pl.program_id(1)
    nt = pl.num_programs(1)
    m = mods_ref[0, 0]
    g = g_ref[...]
    n = _norm_mod(h_ref[0], g, m[0:1], m[1:2])
    tl, d = n.shape
    seg_first = (j == 0) | (j == nct)
    seg_last = (j == nct - 1) | (j == nt - 1)
    n_prev = _norm_mod(hp_ref[0], g, m[0:1], m[1:2])[7:8] * jnp.where(seg_first, 0.0, 1.0)
    n_next = _norm_mod(hx_ref[0], g, m[0:1], m[1:2])[0:1] * jnp.where(seg_last, 0.0, 1.0)
    row = lax.broadcasted_iota(jnp.int32, (tl, 1), 0)
    prev = jnp.where(row == 0, n_prev, pltpu.roll(n, 1, axis=0))
    nxt = jnp.where(row == tl - 1, n_next, pltpu.roll(n, tl - 1, axis=0))
    lane = lax.broadcasted_iota(jnp.int32, (1, d), 1)
    xx = jnp.where(lane < d // 2, prev, nxt) - n
    mu = mu_ref[...]
    bd = bd_ref[...]
    halves = [slice(0, tl // 2), slice(tl // 2, tl)]
    first = []
    for rs in halves:
        nh, xh = n[rs], xx[rs]
        xr, xw, xk, xv, xa, xg = [nh + xh * mu[i:i + 1] for i in range(6)]
        first.append((_dot(xr, wr_ref[...]), _dot(xk, wk_ref[...]), _dot(xv, wv_ref[...]),
                      _dot(xg, g1_ref[...]), _dot(xw, w1_ref[...]), _dot(xa, a1_ref[...])))
    second = []
    for r, k, v, gq, tq, ta in first:
        tw = jnp.tanh(tq)
        kk = k * kk_ref[...]
        second.append((_dot(_sigmoid(gq), g2_ref[...]), [_dot(tw, w2_ref[dr]) for dr in range(2)],
                       [_dot(ta, a2_ref[dr]) for dr in range(2)], kk, _head_sum(kk * kk, bd)))
    for rs, (r, k, v, _, _, _), (gate, zw, za, kk, kk_sq) in zip(halves, first, second):
        kk = kk / jnp.maximum(jnp.sqrt(kk_sq), 1e-12)
        g_out[0, rs, :] = gate.astype(g_out.dtype)
        r_out[0, rs, :] = r.astype(r_out.dtype)
        v_out[0, rs, :] = v.astype(v_out.dtype)
        kk_out[0, rs, :] = kk.astype(kk_out.dtype)
        bonus = jnp.zeros_like(v)
        for dr in range(2):
            lw_out[dr, 0, rs, :] = -jnp.exp(-0.5) * _sigmoid(w0_ref[dr:dr + 1, :] + zw[dr])
            a = _sigmoid(a0_ref[dr:dr + 1, :] + za[dr])
            km = k * (1.0 + (a - 1.0) * ka_ref[...])
            km_out[dr, 0, rs, :] = km.astype(km_out.dtype)
            b_out[dr, 0, rs, :] = (kk * a).astype(b_out.dtype)
            bonus = bonus + _head_sum(r * km * rk_ref[...], bd) * v
        bonus_out[0, rs, :] = bonus.astype(bonus_out.dtype)


def _rwkv_proj(h, mods, g, mu, wr, wk, wv, g1, g2, w1, w2, a1, a2, w0, a0, kk, ka, rk, bd, nct, dep=None):
    b, l, d = h.shape
    tl = TOKEN_TILE
    nb8 = l // 8
    tok = pl.BlockSpec((1, tl, d), lambda i, j: (i, j, 0))
    tok2 = pl.BlockSpec((2, 1, tl, d), lambda i, j: (0, i, j, 0))
    full = lambda x: pl.BlockSpec(x.shape, lambda i, j: (0,) * x.ndim)
    sds = jax.ShapeDtypeStruct
    kern, in_specs, args = _after(
        dep, functools.partial(_rwkv_proj_kernel, nct=nct),
        [tok,
         pl.BlockSpec((1, 8, d), lambda i, j: (i, jnp.maximum(j * (tl // 8) - 1, 0), 0)),
         pl.BlockSpec((1, 8, d), lambda i, j: (i, jnp.minimum((j + 1) * (tl // 8), nb8 - 1), 0)),
         pl.BlockSpec((1, 1, N_MODS, d), lambda i, j: (i, jnp.where(j < nct, 0, 1), 0, 0)),
         full(g), full(mu), full(wr), full(wk), full(wv), full(g1), full(g2), full(w1), full(w2),
         full(a1), full(a2), full(w0), full(a0), full(kk), full(ka), full(rk), full(bd)],
        [h, h, h, mods, g, mu, wr, wk, wv, g1, g2, w1, w2, a1, a2, w0, a0, kk, ka, rk, bd])
    return pl.pallas_call(
        kern,
        out_shape=[sds((b, l, d), BF16), sds((b, l, d), BF16), sds((b, l, d), BF16), sds((b, l, d), BF16),
                   sds((2, b, l, d), BF16), sds((2, b, l, d), BF16), sds((2, b, l, d), F32), sds((b, l, d), BF16)],
        grid=(b, l // tl),
        in_specs=in_specs,
        out_specs=[tok, tok, tok, tok, tok2, tok2, tok2, tok],
        compiler_params=pltpu.CompilerParams(dimension_semantics=("parallel", "parallel"),
                                             vmem_limit_bytes=_vmem_limit(56)),
        name="rwkv_proj",
    )(*args)


def _wkv_kernel(r_ref, v_ref, kk_ref, km_ref, b_ref, lw_ref, y_ref, st_ref):
    c = WKV_CHUNK
    w = WKV_PAIR
    rev = pl.program_id(0)
    sign = 1 - 2 * rev

    @pl.when(pl.program_id(2) == 0)
    def _():
        st_ref[...] = jnp.zeros_like(st_ref)

    ti = lax.broadcasted_iota(jnp.int32, (c, c), 0)
    si = lax.broadcasted_iota(jnp.int32, (c, c), 1)
    tri = jnp.where((si - ti) * sign <= 0, 1.0, 0.0).astype(F32)
    nsub = WKV_CHUNKS_PER_STEP
    subs = [pl.ds(pl.multiple_of(jnp.where(rev == 0, s, nsub - 1 - s) * c, c), c) for s in range(nsub)]
    rt, kt, kh, bh, v32, e_mid = [], [], [], [], [], []
    for rows in subs:
        lw = lw_ref[0, 0, rows, :]
        l_incl = jnp.dot(tri, lw, precision=HIGHEST, preferred_element_type=F32)
        mid = 0.5 * jnp.sum(lw, axis=0, keepdims=True)
        e_neg = jnp.exp(mid - l_incl)
        e_mid.append(jnp.exp(mid))
        rt.append(r_ref[0, rows, :].astype(F32) * jnp.exp(l_incl - mid))
        kt.append(kk_ref[0, rows, :].astype(F32) * jnp.exp(l_incl - lw - mid))
        kh.append(km_ref[0, 0, rows, :].astype(F32) * e_neg)
        bh.append(b_ref[0, 0, rows, :].astype(F32) * e_neg)
        v32.append(v_ref[0, rows, :].astype(F32))

    ri = lax.broadcasted_iota(jnp.int32, (w, w), 0)
    ci = lax.broadcasted_iota(jnp.int32, (w, w), 1)
    same = (ri // c) == (ci // c)
    eye = jnp.where(ri == ci, 1.0, 0.0).astype(F32)
    tl_ = lax.broadcasted_iota(jnp.int32, (c, w), 0)
    jl_ = lax.broadcasted_iota(jnp.int32, (c, w), 1) % c
    strict = (jl_ - tl_) * sign < 0
    incl = (jl_ - tl_) * sign <= 0
    eye2 = jnp.where(jl_ == tl_, 1.0, 0.0).astype(F32)
    lane = lax.broadcasted_iota(jnp.int32, (1, w), 1)
    h0 = lane < RWKV_HEAD

    def rows2(x):
        return jnp.concatenate([jnp.where(h0, x, 0.0), jnp.where(h0, 0.0, x)], axis=0)

    npair = st_ref.shape[0]
    items = [(s, slice(p * w, (p + 1) * w)) for s in range(nsub) for p in range(npair)]
    n = range(len(items))
    em = [e_mid[s][:, sl] for s, sl in items]
    g = [_dot_nt(jnp.concatenate([kt[s][:, sl], rt[s][:, sl]], axis=0),
                 jnp.concatenate([rows2(kh[s][:, sl]), rows2(bh[s][:, sl])], axis=0)) for s, sl in items]
    a_kk = [jnp.where(strict, x[:c, :w], 0.0) for x in g]
    a_rk = [jnp.where(incl, x[c:, :w], 0.0) for x in g]
    a_rb = [jnp.where(incl, x[c:, w:], 0.0) for x in g]
    vi = [v32[s][:, sl] for s, sl in items]
    v_rows = [rows2(x) for x in vi]
    av = [_dot(jnp.concatenate([a_kk[i], a_rk[i]], axis=0), v_rows[i]) for i in n]
    r_pre = [x[:c] for x in av]
    ark_v = [x[c:] for x in av]
    m = [jnp.where(strict, -x[:c, w:], 0.0) for x in g]
    tinv = [eye2 + x for x in m]
    m = [_dot(x, rows2(x)) for x in m]
    for _ in range(c.bit_length() - 3):
        both = [_dot(jnp.concatenate([tinv[i], m[i]], axis=0), rows2(m[i])) for i in n]
        tinv = [tinv[i] + both[i][:c] for i in n]
        m = [x[c:] for x in both]
    tinv = [tinv[i] + _dot(tinv[i], rows2(m[i])) for i in n]
    sol = [_dot(tinv[i], jnp.concatenate([rows2(r_pre[i]), rows2(kt[s][:, sl] * em[i])], axis=1))
           for i, (s, sl) in enumerate(items)]
    u_pre = [x[:, :w] for x in sol]
    kq = [x[:, w:] for x in sol]
    arb = [_dot(a_rb[i], jnp.concatenate([rows2(u_pre[i]), rows2(kq[i])], axis=1)) for i in n]
    y_pre = [ark_v[i] - arb[i][:, :w] for i in n]
    r_eff = [rt[s][:, sl] * em[i] - arb[i][:, w:] for i, (s, sl) in enumerate(items)]
    bbar = [bh[s][:, sl] * em[i] for i, (s, sl) in enumerate(items)]
    kbar = [kh[s][:, sl] * em[i] for i, (s, sl) in enumerate(items)]
    mmat = [eye * (em[i] * em[i]) - jnp.where(same, _dot_tn(kq[i], bbar[i]), 0.0) for i in n]
    s_pre = [jnp.where(same, _dot_tn(jnp.concatenate([vi[i], -u_pre[i]], axis=0),
                                     jnp.concatenate([kbar[i], bbar[i]], axis=0)), 0.0) for i in n]
    st = [st_ref[p] for p in range(npair)]
    for i, (s, sl) in enumerate(items):
        p = i % npair
        y_ref[0, 0, subs[s], sl] = (_dot_nt(r_eff[i], st[p]) + y_pre[i]).astype(y_ref.dtype)
        hi = st[p].astype(BF16)
        lo = (st[p] - hi.astype(F32)).astype(BF16)
        mb = mmat[i].astype(BF16)
        both = jnp.dot(jnp.concatenate([hi, lo], axis=0), mb, preferred_element_type=F32)
        st[p] = both[:w] + both[w:] + s_pre[i]
    for p in range(npair):
        st_ref[p] = st[p]


def _wkv(r, v, kk, km, bv, lw, lc, dep=None):
    b, l, d = r.shape
    c = WKV_CHUNK * WKV_CHUNKS_PER_STEP
    ncc = lc // c
    nlc = (l - lc) // c

    def chunk(dr, i):
        return jnp.where(dr == 0, i, jnp.where(i < ncc, ncc - 1 - i, nlc + 2 * ncc - 1 - i))

    shared = pl.BlockSpec((1, c, d), lambda dr, bi, i: (bi, chunk(dr, i), 0))
    per_dir = pl.BlockSpec((1, 1, c, d), lambda dr, bi, i: (dr, bi, chunk(dr, i), 0))
    kern, in_specs, args = _after(dep, _wkv_kernel, [shared, shared, shared, per_dir, per_dir, per_dir],
                                  [r, v, kk, km, bv, lw])
    return pl.pallas_call(
        kern,
        out_shape=jax.ShapeDtypeStruct((2, b, l, d), BF16),
        grid=(2, b, l // c),
        in_specs=in_specs,
        out_specs=per_dir,
        scratch_shapes=[pltpu.VMEM((d // WKV_PAIR, WKV_PAIR, WKV_PAIR), F32)],
        compiler_params=pltpu.CompilerParams(dimension_semantics=("parallel", "parallel", "arbitrary"),
                                             vmem_limit_bytes=_vmem_limit(32)),
        name="wkv7_chunked",
    )(*args)


def _rwkv_out_kernel(y_ref, bonus_ref, g_ref, lnw_ref, lnb_ref, wo_ref, bd_ref, h_ref, mods_ref, gffn_ref,
                     wrt_ref, bias_ref, hn_ref, n2_ref, eid_ref, rank_ref, w_ref, cnt_ref, run_ref):
    y = y_ref[0, 0].astype(F32) + y_ref[1, 0].astype(F32)
    bd = bd_ref[...]
    mean = _head_sum(y, bd) * (1.0 / RWKV_HEAD)
    yc = y - mean
    var = _head_sum(yc * yc, bd) * (1.0 / RWKV_HEAD)
    yn = yc * lax.rsqrt(var + GN_EPS) * lnw_ref[...] + lnb_ref[...]
    out = (yn + bonus_ref[0].astype(F32)) * g_ref[0].astype(F32)
    _mixer_tail(_dot(out, wo_ref[...]), h_ref[0], mods_ref[0, 0], gffn_ref, wrt_ref, bias_ref, hn_ref, n2_ref,
                eid_ref, rank_ref, w_ref, cnt_ref, run_ref)


def _rwkv_out(y, bonus, g, lnw, lnb, wo, bd, h, mods, gffn, wrt, bias, nct, dep=None):
    b, l, d = h.shape
    tl = TOKEN_TILE
    tok = lambda w: pl.BlockSpec((1, tl, w), lambda i, j: (i, j, 0))
    full = lambda x: pl.BlockSpec(x.shape, lambda i, j: (0,) * x.ndim)
    shapes, specs = _tail_outs(b, l, d)
    kern, in_specs, args = _after(
        dep, _rwkv_out_kernel,
        [pl.BlockSpec((2, 1, tl, d), lambda i, j: (0, i, j, 0)), tok(d), tok(d),
         full(lnw), full(lnb), full(wo), full(bd), tok(d),
         pl.BlockSpec((1, 1, N_MODS, d), lambda i, j: (i, jnp.where(j < nct, 0, 1), 0, 0)),
         full(gffn), full(wrt), full(bias)],
        [y, bonus, g, lnw, lnb, wo, bd, h, mods, gffn, wrt, bias])
    return pl.pallas_call(
        kern,
        out_shape=shapes,
        grid=(b, l // tl),
        in_specs=in_specs,
        out_specs=specs,
        scratch_shapes=[pltpu.VMEM((N_EXPERTS, 1), F32)],
        compiler_params=pltpu.CompilerParams(dimension_semantics=("arbitrary", "arbitrary"),
                                             vmem_limit_bytes=_vmem_limit(40)),
        name="rwkv_out",
    )(*args)


def _rope_table(n_lat, n_ctx):
    dim = SWA_HEAD_DIM
    nf = dim // 4
    inv = ROPE_THETA ** (-jnp.arange(nf, dtype=F32) / nf)
    row = jnp.repeat(jnp.arange(n_lat // GRID_W, dtype=F32), GRID_W)
    col = jnp.tile(jnp.arange(GRID_W, dtype=F32), n_lat // GRID_W)
    ar = row[:, None] * inv
    ac = col[:, None] * inv
    ang = jnp.concatenate([ar, ar, ac, ac], axis=-1)
    cos = jnp.concatenate([jnp.ones((n_ctx, dim), F32), jnp.cos(ang)], axis=0)
    sin = jnp.concatenate([jnp.zeros((n_ctx, dim), F32), jnp.sin(ang)], axis=0)
    return jnp.tile(cos, (1, 2)), jnp.tile(sin, (1, 2))


def _layout_attn_weights(w_in, w_uq, w_ukv):
    d = w_in.shape[0]
    s0 = MLA_Q_RANK
    s1 = s0 + MLA_KV_RANK
    s2 = s1 + MLA_ROPE
    s3 = s2 + SWA_HEADS * SWA_HEAD_DIM
    s4 = s3 + SWA_KV_HEADS * SWA_HEAD_DIM
    rep = lambda w: jnp.concatenate(
        [jnp.tile(w[:, g * SWA_HEAD_DIM:(g + 1) * SWA_HEAD_DIM], (1, SWA_GROUP)) for g in range(SWA_KV_HEADS)], axis=1)
    win = jnp.concatenate([w_in[:, :s1], w_in[:, s2:s3], rep(w_in[:, s3:s4]), rep(w_in[:, s4:]),
                           w_in[:, s1:s2], jnp.zeros((d, V7X_LANES - MLA_ROPE), w_in.dtype)], axis=1)
    qh = MLA_NOPE + MLA_ROPE
    pad = jnp.zeros((w_uq.shape[0], V7X_MXU_DIM - qh), w_uq.dtype)
    wuq = jnp.concatenate([jnp.concatenate([w_uq[:, h * qh:(h + 1) * qh], pad], axis=1) for h in range(MLA_HEADS)], axis=1)
    kvh = MLA_NOPE + MLA_V
    wuk = jnp.concatenate([w_ukv[:, h * kvh:h * kvh + MLA_NOPE] for h in range(MLA_HEADS)], axis=1)
    wuvt = jnp.concatenate([w_ukv[:, h * kvh + MLA_NOPE:(h + 1) * kvh] for h in range(MLA_HEADS)], axis=1).T
    return win.astype(BF16), wuq.astype(BF16), wuk.astype(BF16), wuvt.astype(BF16)


def _lora_pair(w_down, w_up):
    rank = w_down.shape[2]
    down = jnp.concatenate([w_down[0], w_down[1]], axis=1)
    z = jnp.zeros((rank, w_up.shape[2]), w_up.dtype)
    up = jnp.stack([jnp.concatenate([w_up[0], z], axis=0), jnp.concatenate([z, w_up[1]], axis=0)], axis=0)
    return down.astype(BF16), up.astype(BF16)


def _head_block_diag():
    i = jnp.arange(V7X_MXU_DIM) // RWKV_HEAD
    return (i[:, None] == i[None, :]).astype(BF16)


def kernel(x, c, ctx, c_ctx, ada_w, ada_b, norm_mix, norm_ffn, norm_final, attn_w_in, attn_q_norm, attn_kv_norm, attn_w_uq, attn_w_ukv, attn_sinks, attn_w_o, rwkv_mu, rwkv_w_r, rwkv_w_k, rwkv_w_v, rwkv_w_o, rwkv_g1, rwkv_g2, rwkv_w0, rwkv_w1, rwkv_w2, rwkv_a0, rwkv_a1, rwkv_a2, rwkv_k_k, rwkv_k_a, rwkv_r_k, rwkv_ln_w, rwkv_ln_b, moe_router, moe_bias, moe_w_gate, moe_w_up, moe_w_down, moe_ws_gate, moe_ws_up, moe_ws_down):
    bsz, s, d = x.shape
    lc = ctx.shape[1]
    l = lc + s
    depth = ada_w.shape[0]
    nct = lc // TOKEN_TILE
    assert lc % TOKEN_TILE == 0 and s % TOKEN_TILE == 0 and s >= SWA_BAND and lc % SWA_Q_TILE == 0
    assert lc % (WKV_CHUNK * WKV_CHUNKS_PER_STEP) == 0
    assert d % V7X_MXU_DIM == 0 and WKV_CHUNK * 2 == V7X_LANES
    ngrp = SAMPLE_GROUPS
    bg = bsz // ngrp
    assert bsz % ngrp == 0 and (bg * l) % (8 * V7X_SC_WORKERS) == 0

    assert ngrp == 2
    cos, sin = _rope_table(s, lc)
    bd = _head_block_diag()
    rows = -(-(bsz + 1) // 8) * 8
    cc = jnp.concatenate([c, c_ctx[None, :], jnp.zeros((rows - bsz - 1, d), F32)], axis=0)
    row2 = lambda a: a.reshape(1, -1)
    moe_w = (moe_w_gate, moe_w_up, moe_w_down)
    moe_ws = (moe_ws_gate, moe_ws_up, moe_ws_down)

    shared = {}

    def layer_weights(li):
        if li not in shared:
            i = li // 2
            ada = _ada_mods(cc, ada_w, ada_b, li)
            w = dict(
                mods=jnp.stack([jnp.broadcast_to(ada[bsz].reshape(1, N_MODS, d), (bsz, N_MODS, d)),
                                ada[:bsz].reshape(bsz, N_MODS, d)], axis=1),
                wrt=moe_router[li].T,
                bias=moe_bias[li].reshape(N_GROUPS, GROUP_SIZE, 1))
            if li % 2 == 0:
                w["win"], w["wuq"], w["wuk"], w["wuvt"] = _layout_attn_weights(attn_w_in[i], attn_w_uq[i], attn_w_ukv[i])
                w["wo"] = attn_w_o[i].astype(BF16)
            else:
                w["w1"], w["w2"] = _lora_pair(rwkv_w1[i], rwkv_w2[i])
                w["a1"], w["a2"] = _lora_pair(rwkv_a1[i], rwkv_a2[i])
                w["wr"], w["wk"], w["wv"], w["wo"] = [x[i].astype(BF16) for x in (rwkv_w_r, rwkv_w_k, rwkv_w_v, rwkv_w_o)]
                w["g1"], w["g2"] = rwkv_g1[i].astype(BF16), rwkv_g2[i].astype(BF16)
            shared[li] = w
        return shared[li]

    groups = [dict(stream=(ctx, x, g * bg, 0), b0=g * bg) for g in range(ngrp)]
    result = [None]

    def run_stage(st, li, name, dep):
        w = layer_weights(li)
        i = li // 2
        with_ctx = li < depth - 1
        mods = w["mods"][st["b0"]:st["b0"] + bg]
        if name == "proj" and li % 2 == 0:
            st["qkv"] = _attn_proj(st["stream"], bg, l, mods, row2(norm_mix[li]), w["win"], row2(attn_q_norm[i]),
                                   row2(attn_kv_norm[i]), w["wuq"], w["wuk"], w["wuvt"], cos, sin, nct, dep=dep)
            return st["qkv"][0]
        if name == "mid" and li % 2 == 0:
            q, k, vt, qs, ks, vs = st.pop("qkv")
            st["a"] = _mla_attention(q, k, vt, lc, 0 if with_ctx else lc // MLA_Q_TILE, dep=dep)
            st["bm"] = _swa_attention(attn_sinks[i], qs, ks, vs, lc, 0 if with_ctx else lc // SWA_Q_TILE, dep=st["a"])
            return st["bm"]
        if name == "proj":
            assert st["stream"][0] is st["stream"][1]
            st["feat"] = _rwkv_proj(st["stream"][0], mods, row2(norm_mix[li]), rwkv_mu[i], w["wr"], w["wk"], w["wv"],
                                    w["g1"], w["g2"], w["w1"], w["w2"], w["a1"], w["a2"], rwkv_w0[i], rwkv_a0[i],
                                    row2(rwkv_k_k[i]), row2(rwkv_k_a[i]), row2(rwkv_r_k[i]), bd, nct, dep=dep)
            return st["feat"][0]
        if name == "mid":
            r, v, kk, gt, km, bv, lw, bonus = st.pop("feat")
            st["y"] = _wkv(r, v, kk, km, bv, lw, lc, dep=dep)
            st["gate"], st["bonus"] = gt, bonus
            return st["y"]
        if name == "out":
            if li % 2 == 0:
                tail = _attn_out(st.pop("a"), st.pop("bm"), st["stream"], mods, w["wo"], row2(norm_ffn[li]),
                                 w["wrt"], w["bias"], nct, dep=dep)
            else:
                tail = _rwkv_out(st.pop("y"), st.pop("bonus"), st.pop("gate"), row2(rwkv_ln_w[i]), row2(rwkv_ln_b[i]),
                                 w["wo"], bd, st["stream"][0], mods, row2(norm_ffn[li]), w["wrt"], w["bias"], nct, dep=dep)
            st["h"], st["n2p"], eid, rank, st["wcols"], counts = tail
            st["xs"], st["dest"], st["tile_expert"], st["n_valid"] = _moe_route_rows(st["n2p"], eid, rank, counts, bg, l)
            return st["h"]
        if name == "experts":
            ys = _moe_experts(st.pop("tile_expert"), st.pop("n_valid"), st.pop("xs"), *moe_w, li, dep=dep)
            st["yg"] = _sc_gather(ys, st.pop("dest"), bg * l).reshape(TOP_K, bg, l, d // 2)
            return ys
        assert name == "combine"
        last = li == depth - 1
        h = _moe_combine(st.pop("yg"), st.pop("wcols"), st.pop("n2p"), *moe_ws, st.pop("h"), mods, row2(norm_final),
                         nct, li, result[0] if last else None, st["b0"] if last else 0, bsz if last else bg,
                         last, last, dep=dep)
        if last:
            result[0] = h
        else:
            st["stream"] = (h, h, 0, nct)
        return h

    order = [(0, 0, "proj"), (0, 0, "mid")]
    for li in range(depth):
        order += [(0, li, "out"), (1, li, "proj"), (0, li, "experts"), (1, li, "mid")]
        if li < depth - 1:
            order += [(0, li, "combine"), (1, li, "out"), (0, li + 1, "proj"), (1, li, "experts"),
                      (0, li + 1, "mid"), (1, li, "combine")]
        else:
            order += [(1, li, "out"), (0, li, "combine"), (1, li, "experts"), (1, li, "combine")]
    dep = None
    for g, li, name in order:
        dep = run_stage(groups[g], li, name, dep)
    return result[0]
```

```python
import functools

import jax
import jax.numpy as jnp
from jax import lax
from jax.experimental import pallas as pl
from jax.experimental.pallas import tpu as pltpu
from jax.experimental.pallas import tpu_sc as plsc

F32 = jnp.float32
BF16 = jnp.bfloat16
HIGHEST = lax.Precision.HIGHEST

GRID_W = 64
NORM_EPS = 1e-6
ROPE_THETA = 10000.0
NEG_INF = -1e30
N_MODS = 6

MLA_HEADS = 4
MLA_Q_RANK = 384
MLA_KV_RANK = 256
MLA_NOPE = 128
MLA_ROPE = 64
MLA_V = 128

SWA_HEADS = 8
SWA_KV_HEADS = 2
SWA_GROUP = SWA_HEADS // SWA_KV_HEADS
SWA_HEAD_DIM = 64
WINDOW = 128

RWKV_HEAD = 64
DECAY_LORA = 64
ICLR_LORA = 64
GATE_LORA = 128
GN_EPS = 64e-5

N_EXPERTS = 64
TOP_K = 6
N_GROUPS = 8
TOPK_GROUPS = 4
GROUP_SIZE = N_EXPERTS // N_GROUPS
ROUTED_SCALE = 2.5
GATE_W = 128

V7X_LANES = 128
V7X_MXU_DIM = 256
V7X_VMEM_BYTES = 64 * 1024 * 1024
V7X_SC_CORES = 2
V7X_SC_SUBCORES = 16
V7X_SC_WORKERS = V7X_SC_CORES * V7X_SC_SUBCORES

TOKEN_TILE = 256
MLA_Q_TILE = 256
MLA_HEADS_PER_STEP = 2
SWA_Q_TILE = 256
SWA_BAND = SWA_Q_TILE + 2 * WINDOW
WKV_CHUNK = 64
WKV_PAIR = 2 * RWKV_HEAD
WKV_CHUNKS_PER_STEP = 4
MOE_ROW_TILE = 512
MOE_ROW_SLOTS = 3
SAMPLE_GROUPS = 2
SC_MAX_CHUNK = 64

LOG2E = 1.4426950408889634
MIB = 1024 * 1024
VMEM_RESERVE_BYTES = 4 * MIB


def _vmem_limit(mib):
    return min(mib * MIB, V7X_VMEM_BYTES - VMEM_RESERVE_BYTES)


def _dot(a, b):
    return jnp.dot(a.astype(BF16), b.astype(BF16), preferred_element_type=F32)


def _dot_nt(a, b):
    return lax.dot_general(a.astype(BF16), b.astype(BF16), (((1,), (1,)), ((), ())),
                           preferred_element_type=F32)


def _dot_tn(a, b):
    return lax.dot_general(a.astype(BF16), b.astype(BF16), (((0,), (0,)), ((), ())),
                           preferred_element_type=F32)


def _sigmoid(x):
    return 1.0 / (1.0 + jnp.exp(-x))


def _silu(x):
    return x * _sigmoid(x)


def _rms(x, g):
    return x * lax.rsqrt(jnp.mean(x * x, axis=-1, keepdims=True) + NORM_EPS) * g


def _norm_mod(x, g, shift, scale):
    return _rms(x, g) * (1.0 + scale) + shift


def _split_dot(x, w):
    hi = x.astype(BF16)
    lo = (x - hi.astype(F32)).astype(BF16)
    return (jnp.dot(hi, w, preferred_element_type=F32) + jnp.dot(lo, w, preferred_element_type=F32))


def _head_sum(x, bd):
    w = bd.shape[0]
    parts = [_split_dot(x[:, c * w:(c + 1) * w], bd) for c in range(x.shape[1] // w)]
    return jnp.concatenate(parts, axis=1)


def _after(dep, kernel, in_specs, args, n_lead=0):
    if dep is None:
        return kernel, list(in_specs), list(args)
    n_in = n_lead + len(in_specs)

    def ordered(*refs):
        return kernel(*refs[:n_in], *refs[n_in + 1:])

    return ordered, list(in_specs) + [pl.BlockSpec(memory_space=pl.ANY)], list(args) + [dep]


def _ada_kernel(c_ref, w_ref, b_ref, o_ref):
    s = _silu(c_ref[...])
    o_ref[...] = jnp.dot(s, w_ref[0], precision=HIGHEST, preferred_element_type=F32) + b_ref[0]


def _ada_mods(cc, w, b, layer):
    rows, d = cc.shape
    depth, _, n = w.shape
    return pl.pallas_call(
        _ada_kernel,
        out_shape=jax.ShapeDtypeStruct((rows, n), F32),
        grid=(n // d,),
        in_specs=[pl.BlockSpec((rows, d), lambda i: (0, 0)),
                  pl.BlockSpec((1, d, d), lambda i: (layer, 0, i)),
                  pl.BlockSpec((1, 1, d), lambda i: (layer, 0, i))],
        out_specs=pl.BlockSpec((rows, d), lambda i: (0, i)),
        compiler_params=pltpu.CompilerParams(dimension_semantics=("parallel",),
                                             vmem_limit_bytes=_vmem_limit(32)),
        name="ada_mods",
    )(cc, w, b.reshape(depth, 1, n))


def _rope128(x, cos, sin, first_half):
    rot = jnp.where(first_half, -pltpu.roll(x, V7X_LANES - 16, axis=1), pltpu.roll(x, 16, axis=1))
    return x * cos + rot * sin


_C_CQ = 0
_C_CKV = _C_CQ + MLA_Q_RANK
_C_QS = _C_CKV + MLA_KV_RANK
_C_KS = _C_QS + SWA_HEADS * SWA_HEAD_DIM
_C_VS = _C_KS + SWA_KV_HEADS * V7X_MXU_DIM
_C_KR = _C_VS + SWA_KV_HEADS * V7X_MXU_DIM
_C_END = _C_KR + V7X_LANES
_SWA_W = SWA_KV_HEADS * V7X_MXU_DIM
_MLA_QK_W = MLA_HEADS * V7X_MXU_DIM


def _stream_specs(stream, nct, tl):
    ctx_arr, lat_arr, b0, lat_off = stream
    d = ctx_arr.shape[2]
    return [pl.BlockSpec((1, tl, d), lambda i, j: (i + b0, jnp.minimum(j, nct - 1), 0)),
            pl.BlockSpec((1, tl, d), lambda i, j: (i + b0, jnp.maximum(j - nct, 0) + lat_off, 0))]


def _stream_tile(c_ref, x_ref, nct):
    rows = c_ref.shape[1]
    take_ctx = lax.broadcasted_iota(jnp.int32, (rows, 1), 0) < jnp.where(pl.program_id(1) < nct, rows, 0)
    return jnp.where(take_ctx, c_ref[0], x_ref[0])


def _attn_proj_kernel(c_ref, x_ref, mods_ref, g_ref, win_ref, qn_ref, kvn_ref, wuq_ref, wuk_ref, wuvt_ref, cos_ref,
                      sin_ref, q_ref, k_ref, vt_ref, qs_ref, ks_ref, vs_ref, *, nct):
    m = mods_ref[0, 0]
    n = _norm_mod(_stream_tile(c_ref, x_ref, nct), g_ref[...], m[0:1], m[1:2])
    u = _dot(n, win_ref[...])
    cos = cos_ref[...]
    sin = sin_ref[...]
    lane = lax.broadcasted_iota(jnp.int32, (1, V7X_LANES), 1)
    first_half = (lane % 32) < 16

    def rope(x):
        return _rope128(x, cos, sin, first_half)

    scale_a = (MLA_NOPE + MLA_ROPE) ** -0.5 * LOG2E
    scale_b = SWA_HEAD_DIM ** -0.5 * LOG2E
    q = _dot(_rms(u[:, _C_CQ:_C_CKV], qn_ref[...]), wuq_ref[...])
    ckv = _rms(u[:, _C_CKV:_C_QS], kvn_ref[...])
    kn = _dot(ckv, wuk_ref[...])
    vt_ref[0] = _dot_nt(wuvt_ref[...], ckv).astype(BF16)
    kr = rope(u[:, _C_KR:_C_END]).astype(BF16)
    for h in range(MLA_HEADS):
        o = h * V7X_MXU_DIM
        q_ref[0, :, o:o + V7X_LANES] = (q[:, o:o + V7X_LANES] * scale_a).astype(BF16)
        q_ref[0, :, o + V7X_LANES:o + V7X_MXU_DIM] = (rope(q[:, o + V7X_LANES:o + V7X_MXU_DIM]) * scale_a).astype(BF16)
        k_ref[0, :, o:o + V7X_LANES] = kn[:, h * MLA_NOPE:(h + 1) * MLA_NOPE].astype(BF16)
        k_ref[0, :, o + V7X_LANES:o + V7X_MXU_DIM] = kr
    for c in range((_C_KS - _C_QS) // V7X_LANES):
        o = c * V7X_LANES
        qs_ref[0, :, o:o + V7X_LANES] = (rope(u[:, _C_QS + o:_C_QS + o + V7X_LANES]) * scale_b).astype(BF16)
    for c in range(_SWA_W // V7X_LANES):
        o = c * V7X_LANES
        ks_ref[0, :, o:o + V7X_LANES] = rope(u[:, _C_KS + o:_C_KS + o + V7X_LANES]).astype(BF16)
    vs_ref[0] = u[:, _C_VS:_C_KR].astype(BF16)


def _attn_proj(stream, b, l, mods, g, win, qn, kvn, wuq, wuk, wuvt, cos, sin, nct, dep=None):
    d = stream[0].shape[2]
    tl = TOKEN_TILE
    tok = lambda w: pl.BlockSpec((1, tl, w), lambda i, j: (i, j, 0))
    full = lambda a: pl.BlockSpec(a.shape, lambda i, j: (0,) * a.ndim)
    sds = jax.ShapeDtypeStruct
    dv = MLA_HEADS * MLA_V
    kern, in_specs, args = _after(
        dep, functools.partial(_attn_proj_kernel, nct=nct),
        _stream_specs(stream, nct, tl) + [
            pl.BlockSpec((1, 1, N_MODS, d), lambda i, j: (i, jnp.where(j < nct, 0, 1), 0, 0)),
            full(g), full(win), full(qn), full(kvn), full(wuq), full(wuk), full(wuvt),
            pl.BlockSpec((tl, V7X_LANES), lambda i, j: (j, 0)),
            pl.BlockSpec((tl, V7X_LANES), lambda i, j: (j, 0))],
        [stream[0], stream[1], mods, g, win, qn, kvn, wuq, wuk, wuvt, cos, sin])
    return pl.pallas_call(
        kern,
        out_shape=[sds((b, l, _MLA_QK_W), BF16), sds((b, l, _MLA_QK_W), BF16), sds((b, dv, l), BF16),
                   sds((b, l, SWA_HEADS * SWA_HEAD_DIM), BF16), sds((b, l, _SWA_W), BF16), sds((b, l, _SWA_W), BF16)],
        grid=(b, l // tl),
        in_specs=in_specs,
        out_specs=[tok(_MLA_QK_W), tok(_MLA_QK_W), pl.BlockSpec((1, dv, tl), lambda i, j: (i, 0, j)),
                   tok(SWA_HEADS * SWA_HEAD_DIM), tok(_SWA_W), tok(_SWA_W)],
        compiler_params=pltpu.CompilerParams(dimension_semantics=("parallel", "parallel"),
                                             vmem_limit_bytes=_vmem_limit(48)),
        name="attn_proj",
    )(*args)


def _mla_kernel(q_ref, k_ref, vt_ref, o_ref, *, nct_q, lc):
    hw = V7X_MXU_DIM

    def attend(nk):
        st = [_dot_nt(k_ref[0, 0:nk, hh * hw:(hh + 1) * hw], q_ref[0, :, hh * hw:(hh + 1) * hw])
              for hh in range(MLA_HEADS_PER_STEP)]
        for hh, s in enumerate(st):
            p = jnp.exp2(s - jnp.max(s, axis=0, keepdims=True))
            den = jnp.sum(p, axis=0, keepdims=True)
            ot = _dot(vt_ref[0, hh * MLA_V:(hh + 1) * MLA_V, 0:nk], p) / den
            o_ref[0, :, hh * MLA_V:(hh + 1) * MLA_V] = ot.T.astype(o_ref.dtype)

    @pl.when(pl.program_id(2) < nct_q)
    def _():
        attend(lc)

    @pl.when(pl.program_id(2) >= nct_q)
    def _():
        attend(k_ref.shape[1])


def _mla_attention(q, k, vt, lc, q_tile0, dep=None):
    b, l, _ = q.shape
    tq = MLA_Q_TILE
    hps = MLA_HEADS_PER_STEP
    kern, in_specs, args = _after(
        dep, functools.partial(_mla_kernel, nct_q=lc // tq - q_tile0, lc=lc),
        [pl.BlockSpec((1, tq, hps * V7X_MXU_DIM), lambda i, h, j: (i, j + q_tile0, h)),
         pl.BlockSpec((1, l, hps * V7X_MXU_DIM), lambda i, h, j: (i, 0, h)),
         pl.BlockSpec((1, hps * MLA_V, l), lambda i, h, j: (i, h, 0))],
        [q, k, vt])
    return pl.pallas_call(
        kern,
        out_shape=jax.ShapeDtypeStruct((b, l, MLA_HEADS * MLA_V), BF16),
        grid=(b, MLA_HEADS // hps, l // tq - q_tile0),
        in_specs=in_specs,
        out_specs=pl.BlockSpec((1, tq, hps * MLA_V), lambda i, h, j: (i, j + q_tile0, h)),
        compiler_params=pltpu.CompilerParams(dimension_semantics=("parallel", "parallel", "parallel"),
                                             vmem_limit_bytes=_vmem_limit(48)),
        name="mla_attention",
    )(*args)


def _swa_kernel(sink_ref, q_ref, k_ref, v_ref, o_ref, *, lc, q_tile0):
    tq = SWA_Q_TILE
    l = k_ref.shape[1]
    r0 = (pl.program_id(1) + q_tile0) * tq
    start = pl.multiple_of(jnp.clip(r0 - WINDOW, lc, l - SWA_BAND), WINDOW)
    rows = SWA_GROUP * tq
    row = lax.broadcasted_iota(jnp.int32, (rows, 1), 0)
    qpos = jnp.where(r0 >= lc, r0, -l) + row % tq
    kpos = start + lax.broadcasted_iota(jnp.int32, (1, SWA_BAND), 1)
    valid = jnp.abs(qpos - kpos) <= WINDOW
    lane = lax.broadcasted_iota(jnp.int32, (1, V7X_MXU_DIM), 1)
    head = [(lane // SWA_HEAD_DIM) == hh for hh in range(SWA_GROUP)]
    groups = range(SWA_KV_HEADS)
    sls = [slice(g * V7X_MXU_DIM, (g + 1) * V7X_MXU_DIM) for g in groups]
    qstack = []
    for sl in sls:
        qg = q_ref[0, :, sl]
        zero = jnp.zeros_like(qg)
        qstack.append(jnp.concatenate([jnp.where(head[hh], qg, zero) for hh in range(SWA_GROUP)], axis=0))
    sc = [_dot_nt(qstack[g], k_ref[0, 0:lc, sls[g]]) for g in groups]
    sb = [_dot_nt(qstack[g], k_ref[0, pl.ds(start, SWA_BAND), sls[g]]) for g in groups]
    for g in groups:
        sl = sls[g]
        sbm = jnp.where(valid, sb[g], NEG_INF)
        sk = jnp.zeros((rows, 1), F32)
        for hh in range(SWA_GROUP):
            sk = jnp.where(row // tq == hh, sink_ref[g * SWA_GROUP + hh] * LOG2E, sk)
        mx = jnp.maximum(jnp.maximum(jnp.max(sc[g], axis=-1, keepdims=True), jnp.max(sbm, axis=-1, keepdims=True)), sk)
        pc = jnp.exp2(sc[g] - mx)
        pb = jnp.exp2(sbm - mx)
        den = jnp.sum(pc, axis=-1, keepdims=True) + jnp.sum(pb, axis=-1, keepdims=True) + jnp.exp2(sk - mx)
        ostack = (_dot(pc, v_ref[0, 0:lc, sl]) + _dot(pb, v_ref[0, pl.ds(start, SWA_BAND), sl])) / den
        o = jnp.zeros((tq, V7X_MXU_DIM), F32)
        for hh in range(SWA_GROUP):
            o = o + jnp.where(head[hh], ostack[hh * tq:(hh + 1) * tq], 0.0)
        o_ref[0, :, sl] = o.astype(o_ref.dtype)


def _swa_attention(sinks, q, k, v, lc, q_tile0, dep=None):
    b, l, _ = q.shape
    tq = SWA_Q_TILE
    kern, in_specs, args = _after(
        dep, functools.partial(_swa_kernel, lc=lc, q_tile0=q_tile0),
        [pl.BlockSpec(memory_space=pltpu.SMEM),
         pl.BlockSpec((1, tq, SWA_HEADS * SWA_HEAD_DIM), lambda i, j: (i, j + q_tile0, 0)),
         pl.BlockSpec((1, l, _SWA_W), lambda i, j: (i, 0, 0)),
         pl.BlockSpec((1, l, _SWA_W), lambda i, j: (i, 0, 0))],
        [sinks, q, k, v])
    return pl.pallas_call(
        kern,
        out_shape=jax.ShapeDtypeStruct((b, l, SWA_HEADS * SWA_HEAD_DIM), BF16),
        grid=(b, l // tq - q_tile0),
        in_specs=in_specs,
        out_specs=pl.BlockSpec((1, tq, SWA_HEADS * SWA_HEAD_DIM), lambda i, j: (i, j + q_tile0, 0)),
        compiler_params=pltpu.CompilerParams(dimension_semantics=("parallel", "parallel"),
                                             vmem_limit_bytes=_vmem_limit(48)),
        name="swa_attention",
    )(*args)


def _pack_bf16_pair(x):
    w = x.shape[1] // 2
    lo = pltpu.bitcast(x[:, :w].astype(BF16).astype(F32), jnp.int32)
    hi = pltpu.bitcast(x[:, w:].astype(BF16).astype(F32), jnp.int32)
    return lax.shift_right_logical(lo, jnp.int32(16)) | (hi & jnp.int32(-65536))


def _unpack_bf16_pair(p):
    return pltpu.bitcast(p << 16, F32), pltpu.bitcast(p & jnp.int32(-65536), F32)


def _route(n2, wrt, bias, run_ref):
    n_hi = n2.astype(BF16)
    n_lo = (n2 - n_hi.astype(F32)).astype(BF16)
    w_hi = wrt.astype(BF16)
    w_lo = (wrt - w_hi.astype(F32)).astype(BF16)
    w_both = jnp.concatenate([w_hi, w_lo], axis=0)
    rows = n2.shape[0]
    score_blocks = []
    for o in range(0, rows, V7X_LANES):
        both = _dot_nt(w_both, n_hi[o:o + V7X_LANES])
        logits = both[:N_EXPERTS] + both[N_EXPERTS:] + _dot_nt(w_hi, n_lo[o:o + V7X_LANES])
        score_blocks.append(_sigmoid(logits))
    scores = jnp.concatenate(score_blocks, axis=1)

    def select(sc2):
        cols = sc2.shape[1]
        shape3 = (N_GROUPS, GROUP_SIZE, cols)
        choice = sc2.reshape(shape3) + bias
        ji = lax.broadcasted_iota(jnp.int32, shape3, 1).astype(F32)
        m1 = jnp.max(choice, axis=1, keepdims=True)
        first = jnp.min(jnp.where(choice == m1, ji, float(GROUP_SIZE)), axis=1, keepdims=True)
        m2 = jnp.max(jnp.where(ji == first, -jnp.inf, choice), axis=1, keepdims=True)
        gs = m1 + m2
        gidx = lax.broadcasted_iota(jnp.int32, gs.shape, 0).astype(F32)
        gsel = jnp.zeros_like(gs)
        for _ in range(TOPK_GROUPS):
            mx = jnp.max(gs, axis=0, keepdims=True)
            pick = gidx == jnp.min(jnp.where(gs == mx, gidx, float(N_GROUPS)), axis=0, keepdims=True)
            gsel = jnp.where(pick, 1.0, gsel)
            gs = jnp.where(pick, -jnp.inf, gs)
        cand = jnp.where(gsel > 0.0, choice, -jnp.inf).reshape(N_EXPERTS, cols)
        eidx = lax.broadcasted_iota(jnp.int32, (N_EXPERTS, cols), 0).astype(F32)
        out = []
        for _ in range(TOP_K):
            mx = jnp.max(cand, axis=0, keepdims=True)
            pick = eidx == jnp.min(jnp.where(cand == mx, eidx, float(N_EXPERTS)), axis=0, keepdims=True)
            out.append(jnp.where(pick, 1.0, 0.0))
            cand = jnp.where(pick, -jnp.inf, cand)
        return out

    blocks = [select(sc2) for sc2 in score_blocks]
    picks = [jnp.concatenate([blk[k] for blk in blocks], axis=1) > 0.0 for k in range(TOP_K)]
    ei = lax.broadcasted_iota(jnp.int32, (N_EXPERTS, rows), 0).astype(F32)
    esel = jnp.zeros((N_EXPERTS, rows), F32)
    for pick in picks:
        esel = jnp.where(pick, 1.0, esel)
    before = jnp.where(lax.broadcasted_iota(jnp.int32, (rows, rows), 0) < lax.broadcasted_iota(jnp.int32, (rows, rows), 1),
                       1.0, 0.0).astype(BF16)
    slot = jnp.dot(esel.astype(BF16), before, preferred_element_type=F32) + run_ref[...]
    run_ref[...] += jnp.sum(esel, axis=1, keepdims=True)
    sc = [jnp.sum(jnp.where(pick, scores, 0.0), axis=0, keepdims=True) for pick in picks]
    tot = sc[0]
    for x in sc[1:]:
        tot = tot + x
    k8 = lax.broadcasted_iota(jnp.int32, (8, rows), 0)
    kw = lax.broadcasted_iota(jnp.int32, (GATE_W, rows), 0)
    eid = jnp.zeros((8, rows), jnp.int32)
    rank = jnp.zeros((8, rows), jnp.int32)
    wk = jnp.zeros((GATE_W, rows), F32)
    for k, pick in enumerate(picks):
        e_k = jnp.sum(jnp.where(pick, ei, 0.0), axis=0, keepdims=True).astype(jnp.int32)
        r_k = jnp.sum(jnp.where(pick, slot, 0.0), axis=0, keepdims=True).astype(jnp.int32)
        eid = jnp.where(k8 == k, e_k, eid)
        rank = jnp.where(k8 == k, r_k, rank)
        wk = jnp.where(kw == k, sc[k] * (ROUTED_SCALE / tot), wk)
    return eid, rank, wk.T


def _mixer_tail(o, h, m, gffn_ref, wrt_ref, bias_ref, hn_ref, n2_ref, eid_ref, rank_ref, w_ref, cnt_ref, run_ref):
    @pl.when((pl.program_id(0) == 0) & (pl.program_id(1) == 0))
    def _():
        run_ref[...] = jnp.zeros_like(run_ref)

    hn = h + m[2:3] * o
    hn_ref[0] = hn
    n2 = _norm_mod(hn, gffn_ref[...], m[3:4], m[4:5])
    n2_ref[0] = _pack_bf16_pair(n2)
    eid, rank, wcols = _route(n2, wrt_ref[...], bias_ref[...], run_ref)
    eid_ref[0] = eid
    rank_ref[0] = rank
    w_ref[0] = wcols
    cnt_ref[...] = run_ref[...]


def _attn_out_kernel(a_ref, b_ref, c_ref, x_ref, mods_ref, wo_ref, gffn_ref, wrt_ref, bias_ref,
                     hn_ref, n2_ref, eid_ref, rank_ref, w_ref, cnt_ref, run_ref, *, nct):
    wa = MLA_HEADS * MLA_V
    o = _dot(a_ref[0], wo_ref[0:wa, :]) + _dot(b_ref[0], wo_ref[wa:, :])
    _mixer_tail(o, _stream_tile(c_ref, x_ref, nct), mods_ref[0, 0], gffn_ref, wrt_ref, bias_ref, hn_ref, n2_ref,
                eid_ref, rank_ref, w_ref, cnt_ref, run_ref)


def _tail_outs(b, l, d):
    tl = TOKEN_TILE
    nt = l // tl
    sds = jax.ShapeDtypeStruct
    tok = lambda w: pl.BlockSpec((1, tl, w), lambda i, j: (i, j, 0))
    blk = pl.BlockSpec((1, 8, tl), lambda i, j: (i * nt + j, 0, 0))
    shapes = [sds((b, l, d), F32), sds((b, l, d // 2), jnp.int32), sds((b * nt, 8, tl), jnp.int32),
              sds((b * nt, 8, tl), jnp.int32), sds((b, l, GATE_W), F32), sds((N_EXPERTS, 1), F32)]
    specs = [tok(d), tok(d // 2), blk, blk, tok(GATE_W), pl.BlockSpec((N_EXPERTS, 1), lambda i, j: (0, 0))]
    return shapes, specs


def _attn_out(a, bm, stream, mods, wo, gffn, wrt, bias, nct, dep=None):
    b, l, _ = a.shape
    d = stream[0].shape[2]
    tl = TOKEN_TILE
    tok = lambda w: pl.BlockSpec((1, tl, w), lambda i, j: (i, j, 0))
    full = lambda x: pl.BlockSpec(x.shape, lambda i, j: (0,) * x.ndim)
    shapes, specs = _tail_outs(b, l, d)
    kern, in_specs, args = _after(
        dep, functools.partial(_attn_out_kernel, nct=nct),
        [tok(a.shape[2]), tok(bm.shape[2])] + _stream_specs(stream, nct, tl) + [
            pl.BlockSpec((1, 1, N_MODS, d), lambda i, j: (i, jnp.where(j < nct, 0, 1), 0, 0)),
            full(wo), full(gffn), full(wrt), full(bias)],
        [a, bm, stream[0], stream[1], mods, wo, gffn, wrt, bias])
    return pl.pallas_call(
        kern,
        out_shape=shapes,
        grid=(b, l // tl),
        in_specs=in_specs,
        out_specs=specs,
        scratch_shapes=[pltpu.VMEM((N_EXPERTS, 1), F32)],
        compiler_params=pltpu.CompilerParams(dimension_semantics=("arbitrary", "arbitrary"),
                                             vmem_limit_bytes=_vmem_limit(40)),
        name="attn_out",
    )(*args)


def _moe_dest_kernel(off_ref, eid_ref, rank_ref, dest_ref):
    eid = eid_ref[...]
    dest = rank_ref[...]
    for e in range(N_EXPERTS):
        dest = dest + jnp.where(eid == e, off_ref[e], 0)
    dest_ref[...] = dest


def _moe_dest(off, eid, rank):
    return pl.pallas_call(
        _moe_dest_kernel,
        out_shape=jax.ShapeDtypeStruct(eid.shape, jnp.int32),
        in_specs=[pl.BlockSpec(memory_space=pltpu.SMEM),
                  pl.BlockSpec(eid.shape, lambda: (0, 0, 0)), pl.BlockSpec(eid.shape, lambda: (0, 0, 0))],
        out_specs=pl.BlockSpec(eid.shape, lambda: (0, 0, 0)),
        name="moe_dest",
    )(off, eid, rank)


def _sc_mesh():
    return plsc.VectorSubcoreMesh(core_axis_name="c", subcore_axis_name="s",
                                  num_cores=V7X_SC_CORES, num_subcores=V7X_SC_SUBCORES)


def _sc_chunk(rows_per_worker):
    return max(c for c in range(8, SC_MAX_CHUNK + 1, 8) if rows_per_worker % c == 0)


def _sc_dispatch(xp, dest, p_rows):
    t, w = xp.shape
    tpw = t // V7X_SC_WORKERS
    ch = _sc_chunk(tpw)

    @functools.partial(
        pl.kernel, mesh=_sc_mesh(), out_type=jax.ShapeDtypeStruct((p_rows, w), xp.dtype),
        scratch_types=[pltpu.VMEM((ch, w), xp.dtype)] + [pltpu.VMEM((ch,), jnp.int32)] * TOP_K
        + [pltpu.SemaphoreType.DMA, pltpu.SemaphoreType.DMA],
        name="moe_dispatch")
    def run(x_hbm, dest_hbm, out_hbm, rows_v, *rest):
        idx, (sem_i, sem_o) = rest[:TOP_K], rest[TOP_K:]
        base = (lax.axis_index("s") * V7X_SC_CORES + lax.axis_index("c")) * tpw

        @pl.loop(0, tpw // ch)
        def _(i):
            t0 = base + i * ch
            loads = [pltpu.async_copy(dest_hbm.at[k, pl.ds(t0, ch)], idx[k], sem_i) for k in range(TOP_K)]
            pltpu.sync_copy(x_hbm.at[pl.ds(t0, ch)], rows_v)
            for c in loads:
                c.wait()
            puts = [pltpu.async_copy(rows_v, out_hbm.at[idx[k]], sem_o) for k in range(TOP_K)]
            for c in puts:
                c.wait()

    return run(xp, dest)


def _sc_gather(ys, dest, t):
    w = ys.shape[1]
    tpw = t // V7X_SC_WORKERS
    ch = _sc_chunk(tpw)

    @functools.partial(
        pl.kernel, mesh=_sc_mesh(), out_type=jax.ShapeDtypeStruct((TOP_K, t, w), ys.dtype),
        scratch_types=[pltpu.VMEM((ch, w), ys.dtype)] * 2 + [pltpu.VMEM((ch,), jnp.int32)] * TOP_K
        + [pltpu.SemaphoreType.DMA] * 5,
        name="moe_gather")
    def run(y_hbm, dest_hbm, out_hbm, rows_a, rows_b, *rest):
        idx, (sem_i, sem_ga, sem_gb, sem_wa, sem_wb) = rest[:TOP_K], rest[TOP_K:]
        rows, sem_g, sem_w = (rows_a, rows_b), (sem_ga, sem_gb), (sem_wa, sem_wb)
        base = (lax.axis_index("s") * V7X_SC_CORES + lax.axis_index("c")) * tpw

        @pl.loop(0, tpw // ch)
        def _(i):
            t0 = base + i * ch
            loads = [pltpu.async_copy(dest_hbm.at[k, pl.ds(t0, ch)], idx[k], sem_i) for k in range(TOP_K)]
            for c in loads:
                c.wait()
            gets, puts = [None] * TOP_K, [None] * TOP_K
            gets[0] = pltpu.async_copy(y_hbm.at[idx[0]], rows[0], sem_g[0])
            for k in range(TOP_K):
                if k + 1 < TOP_K:
                    if k >= 1:
                        puts[k - 1].wait()
                    gets[k + 1] = pltpu.async_copy(y_hbm.at[idx[k + 1]], rows[(k + 1) % 2], sem_g[(k + 1) % 2])
                gets[k].wait()
                puts[k] = pltpu.async_copy(rows[k % 2], out_hbm.at[k, pl.ds(t0, ch)], sem_w[k % 2])
            puts[TOP_K - 2].wait()
            puts[TOP_K - 1].wait()

    return run(ys, dest)


def _cache_mlp_weights(wg, wu, wd, wgu_ref, wdb_ref):
    f = wg.shape[1]
    wgu_ref[:, 0:f] = wg.astype(BF16)
    wgu_ref[:, f:] = wu.astype(BF16)
    wdb_ref[...] = wd.astype(BF16)


def _gated_mlp(xp, wgu_ref, wdb_ref):
    lo, hi = _unpack_bf16_pair(xp)
    x = jnp.concatenate([lo.astype(BF16), hi.astype(BF16)], axis=1)
    gu = jnp.dot(x, wgu_ref[...], preferred_element_type=F32)
    f = gu.shape[1] // 2
    return _dot(_silu(gu[:, :f]) * gu[:, f:], wdb_ref[...])


def _moe_expert_kernel(te_ref, tb_ref, nv_ref, x_hbm, wga_ref, wua_ref, wda_ref, wgb_ref, wub_ref, wdb_ref, y_ref,
                       gu_a, dn_a, gu_b, dn_b, ids_ref, xbuf, sems):
    i = pl.program_id(0)
    tm = MOE_ROW_TILE
    nv = nv_ref[0]
    last = (nv - 1) // 2
    first = 2 * jnp.minimum(i, last)
    ea = te_ref[first]
    eb = te_ref[first + 1]
    two = 2 * i + 1 < nv

    def rows_copy(step):
        slot = step % MOE_ROW_SLOTS
        row0 = step * (2 * tm)
        rows = pl.ds(row0 if isinstance(step, int) else pl.multiple_of(row0, 2 * tm), 2 * tm)
        return pltpu.make_async_copy(x_hbm.at[rows], xbuf.at[slot], sems.at[slot])

    @pl.when(i == 0)
    def _():
        ids_ref[0] = -1
        ids_ref[1] = -1
        for ahead in range(MOE_ROW_SLOTS - 1):
            @pl.when(ahead <= last)
            def _():
                rows_copy(ahead).start()

    @pl.when(i + (MOE_ROW_SLOTS - 1) <= last)
    def _():
        rows_copy(i + (MOE_ROW_SLOTS - 1)).start()

    @pl.when(i <= last)
    def _():
        rows_copy(i).wait()

    x_ref = xbuf.at[i % MOE_ROW_SLOTS]

    @pl.when(ids_ref[0] != ea)
    def _():
        _cache_mlp_weights(wga_ref[0, 0], wua_ref[0, 0], wda_ref[0, 0], gu_a, dn_a)
        ids_ref[0] = ea

    @pl.when(two & (eb != ea) & (ids_ref[1] != eb))
    def _():
        _cache_mlp_weights(wgb_ref[0, 0], wub_ref[0, 0], wdb_ref[0, 0], gu_b, dn_b)
        ids_ref[1] = eb

    @pl.when(two & (eb == ea))
    def _():
        y_ref[...] = _pack_bf16_pair(_gated_mlp(x_ref[...], gu_a, dn_a))

    @pl.when((2 * i < nv) & jnp.logical_not(two & (eb == ea)))
    def _():
        y_ref[0:tm, :] = _pack_bf16_pair(_gated_mlp(x_ref[0:tm, :], gu_a, dn_a))

    @pl.when(two & (eb != ea))
    def _():
        y_ref[tm:, :] = _pack_bf16_pair(_gated_mlp(x_ref[tm:, :], gu_b, dn_b))


def _moe_experts(tile_expert, n_valid, xs, wg, wu, wd, layer, dep=None):
    p_rows, w = xs.shape
    tm = MOE_ROW_TILE
    _, _, d, f = wg.shape
    npair = p_rows // (2 * tm)
    pairs = tile_expert.reshape(npair, 2)
    tile_b = jnp.maximum(lax.cummax(jnp.where(pairs[:, 1] != pairs[:, 0], pairs[:, 1], -1)), 0)
    step = lambda i, nv: jnp.minimum(i, (nv[0] - 1) // 2)
    spec_a = lambda shp: pl.BlockSpec((1, 1) + shp, lambda i, te, tb, nv: (layer, te[2 * step(i, nv)], 0, 0))
    spec_b = lambda shp: pl.BlockSpec((1, 1) + shp, lambda i, te, tb, nv: (layer, tb[step(i, nv)], 0, 0))
    rows = pl.BlockSpec((2 * tm, w), lambda i, te, tb, nv: (step(i, nv), 0))
    kern, in_specs, args = _after(
        dep, _moe_expert_kernel,
        [pl.BlockSpec(memory_space=pl.ANY), spec_a((d, f)), spec_a((d, f)), spec_a((f, d)),
         spec_b((d, f)), spec_b((d, f)), spec_b((f, d))],
        [tile_expert, tile_b, n_valid, xs, wg, wu, wd, wg, wu, wd], n_lead=3)
    return pl.pallas_call(
        kern,
        out_shape=jax.ShapeDtypeStruct((p_rows, w), xs.dtype),
        grid_spec=pltpu.PrefetchScalarGridSpec(
            num_scalar_prefetch=3, grid=(npair,),
            in_specs=in_specs,
            out_specs=rows,
            scratch_shapes=[pltpu.VMEM((d, 2 * f), BF16), pltpu.VMEM((f, d), BF16),
                            pltpu.VMEM((d, 2 * f), BF16), pltpu.VMEM((f, d), BF16), pltpu.SMEM((2,), jnp.int32),
                            pltpu.VMEM((MOE_ROW_SLOTS, 2 * tm, w), xs.dtype),
                            pltpu.SemaphoreType.DMA((MOE_ROW_SLOTS,))]),
        compiler_params=pltpu.CompilerParams(dimension_semantics=("arbitrary",),
                                             vmem_limit_bytes=_vmem_limit(48)),
        name="moe_experts",
    )(*args)


def _moe_combine_kernel(yg_hbm, w_ref, xp_ref, sg_ref, su_ref, sd_ref, h_ref, mods_ref, gfin_ref, *rest, final_norm,
                        tile0):
    o_ref, wgu_ref, wdb_ref, ybuf, sems = rest[-5:]
    tl = ybuf.shape[2]
    nt = pl.num_programs(1)
    step = pl.program_id(0) * nt + pl.program_id(1)
    n_steps = pl.num_programs(0) * nt

    def yg_copy(s):
        rows = pl.ds(pl.multiple_of((s % nt + tile0) * tl, tl), tl)
        return pltpu.make_async_copy(yg_hbm.at[:, s // nt, rows, :], ybuf.at[s % MOE_ROW_SLOTS],
                                     sems.at[s % MOE_ROW_SLOTS])

    @pl.when(step == 0)
    def _():
        _cache_mlp_weights(sg_ref[0], su_ref[0], sd_ref[0], wgu_ref, wdb_ref)
        for ahead in range(MOE_ROW_SLOTS - 1):
            @pl.when(ahead < n_steps)
            def _():
                yg_copy(ahead).start()

    @pl.when(step + (MOE_ROW_SLOTS - 1) < n_steps)
    def _():
        yg_copy(step + (MOE_ROW_SLOTS - 1)).start()

    yg_copy(step).wait()
    yg_ref = ybuf.at[step % MOE_ROW_SLOTS]

    acc = _gated_mlp(xp_ref[0], wgu_ref, wdb_ref)
    half = acc.shape[1] // 2
    lo = acc[:, :half]
    hi = acc[:, half:]
    w = w_ref[0]
    for k in range(TOP_K):
        ylo, yhi = _unpack_bf16_pair(yg_ref[k])
        wk = w[:, k:k + 1]
        lo = lo + wk * ylo
        hi = hi + wk * yhi
    y = h_ref[0] + mods_ref[0, 0, N_MODS - 1:N_MODS, :] * jnp.concatenate([lo, hi], axis=1)
    if final_norm:
        y = _rms(y, gfin_ref[...])
    o_ref[0] = y


def _moe_combine(yg, wcols, xp, sg, su, sd, h, mods, gfin, nct, layer, out_buf, out_b0, out_batch, latent_only,
                 final_norm, dep=None):
    b, l, d = h.shape
    tl = TOKEN_TILE
    tile0 = nct if latent_only else 0
    tok = lambda w: pl.BlockSpec((1, tl, w), lambda i, j: (i, j + tile0, 0))
    lay = lambda x: pl.BlockSpec((1,) + x.shape[1:], lambda i, j: (layer,) + (0,) * (x.ndim - 1))
    args = [yg, wcols, xp, sg, su, sd, h, mods, gfin]
    in_specs = [pl.BlockSpec(memory_space=pl.ANY), tok(GATE_W), tok(d // 2),
                lay(sg), lay(su), lay(sd), tok(d),
                pl.BlockSpec((1, 1, N_MODS, d), lambda i, j: (i, jnp.where(j + tile0 < nct, 0, 1), 0, 0)),
                pl.BlockSpec(gfin.shape, lambda i, j: (0, 0))]
    _, in_specs, args = _after(dep, None, in_specs, args)
    aliases = {}
    if out_buf is not None:
        args.append(out_buf)
        in_specs.append(pl.BlockSpec(memory_space=pl.ANY))
        aliases = {len(args) - 1: 0}
    return pl.pallas_call(
        functools.partial(_moe_combine_kernel, final_norm=final_norm, tile0=tile0),
        out_shape=jax.ShapeDtypeStruct((out_batch, l - tile0 * tl, d), F32),
        grid=(b, l // tl - tile0),
        in_specs=in_specs,
        out_specs=pl.BlockSpec((1, tl, d), lambda i, j: (i + out_b0, j, 0)),
        scratch_shapes=[pltpu.VMEM((d, 2 * sg.shape[2]), BF16), pltpu.VMEM((sg.shape[2], d), BF16),
                        pltpu.VMEM((MOE_ROW_SLOTS, TOP_K, tl, d // 2), yg.dtype),
                        pltpu.SemaphoreType.DMA((MOE_ROW_SLOTS,))],
        input_output_aliases=aliases,
        compiler_params=pltpu.CompilerParams(dimension_semantics=("arbitrary", "arbitrary"),
                                             vmem_limit_bytes=_vmem_limit(40)),
        name="moe_combine",
    )(*args)


def _moe_route_rows(n2p, eid, rank, counts, b, l):
    d2 = n2p.shape[2]
    t = b * l
    tm = MOE_ROW_TILE
    n_tiles = 2 * -(-(TOP_K * t + N_EXPERTS * (tm - 1)) // (2 * tm))
    tiles_e = (counts.reshape(N_EXPERTS).astype(jnp.int32) + (tm - 1)) // tm
    tile_end = jnp.cumsum(tiles_e)
    off = (tile_end - tiles_e) * tm
    n_valid = tile_end[-1:]
    tile_id = jnp.minimum(jnp.arange(n_tiles, dtype=jnp.int32), n_valid - 1)
    tile_expert = jnp.sum((tile_end[None, :] <= tile_id[:, None]).astype(jnp.int32), axis=1)
    dest = _moe_dest(off, eid, rank).transpose(1, 0, 2).reshape(8, t)
    xs = _sc_dispatch(n2p.reshape(t, d2), dest, n_tiles * tm)
    return xs, dest, tile_expert, n_valid


def _rwkv_proj_kernel(h_ref, hp_ref, hx_ref, mods_ref, g_ref, mu_ref, wr_ref, wk_ref, wv_ref, g1_ref, g2_ref,
                      w1_ref, w2_ref, a1_ref, a2_ref, w0_ref, a0_ref, kk_ref, ka_ref, rk_ref, bd_ref,
                      r_out, v_out, kk_out, g_out, km_out, b_out, lw_out, bonus_out, *, nct):
    j = pl.program_id(1)
    nt = pl.num_programs(1)
    m = mods_ref[0, 0]
    g = g_ref[...]
    n = _norm_mod(h_ref[0], g, m[0:1], m[1:2])
    tl, d = n.shape
    seg_first = (j == 0) | (j == nct)
    seg_last = (j == nct - 1) | (j == nt - 1)
    n_prev = _norm_mod(hp_ref[0], g, m[0:1], m[1:2])[7:8] * jnp.where(seg_first, 0.0, 1.0)
    n_next = _norm_mod(hx_ref[0], g, m[0:1], m[1:2])[0:1] * jnp.where(seg_last, 0.0, 1.0)
    row = lax.broadcasted_iota(jnp.int32, (tl, 1), 0)
    prev = jnp.where(row == 0, n_prev, pltpu.roll(n, 1, axis=0))
    nxt = jnp.where(row == tl - 1, n_next, pltpu.roll(n, tl - 1, axis=0))
    lane = lax.broadcasted_iota(jnp.int32, (1, d), 1)
    xx = jnp.where(lane < d // 2, prev, nxt) - n
    mu = mu_ref[...]
    bd = bd_ref[...]
    halves = [slice(0, tl // 2), slice(tl // 2, tl)]
    first = []
    for rs in halves:
        nh, xh = n[rs], xx[rs]
        xr, xw, xk, xv, xa, xg = [nh + xh * mu[i:i + 1] for i in range(6)]
        first.append((_dot(xr, wr_ref[...]), _dot(xk, wk_ref[...]), _dot(xv, wv_ref[...]),
                      _dot(xg, g1_ref[...]), _dot(xw, w1_ref[...]), _dot(xa, a1_ref[...])))
    second = []
    for r, k, v, gq, tq, ta in first:
        tw = jnp.tanh(tq)
        kk = k * kk_ref[...]
        second.append((_dot(_sigmoid(gq), g2_ref[...]), [_dot(tw, w2_ref[dr]) for dr in range(2)],
                       [_dot(ta, a2_ref[dr]) for dr in range(2)], kk, _head_sum(kk * kk, bd)))
    for rs, (r, k, v, _, _, _), (gate, zw, za, kk, kk_sq) in zip(halves, first, second):
        kk = kk / jnp.maximum(jnp.sqrt(kk_sq), 1e-12)
        g_out[0, rs, :] = gate.astype(g_out.dtype)
        r_out[0, rs, :] = r.astype(r_out.dtype)
        v_out[0, rs, :] = v.astype(v_out.dtype)
        kk_out[0, rs, :] = kk.astype(kk_out.dtype)
        bonus = jnp.zeros_like(v)
        for dr in range(2):
            lw_out[dr, 0, rs, :] = -jnp.exp(-0.5) * _sigmoid(w0_ref[dr:dr + 1, :] + zw[dr])
            a = _sigmoid(a0_ref[dr:dr + 1, :] + za[dr])
            km = k * (1.0 + (a - 1.0) * ka_ref[...])
            km_out[dr, 0, rs, :] = km.astype(km_out.dtype)
            b_out[dr, 0, rs, :] = (kk * a).astype(b_out.dtype)
            bonus = bonus + _head_sum(r * km * rk_ref[...], bd) * v
        bonus_out[0, rs, :] = bonus.astype(bonus_out.dtype)


def _rwkv_proj(h, mods, g, mu, wr, wk, wv, g1, g2, w1, w2, a1, a2, w0, a0, kk, ka, rk, bd, nct, dep=None):
    b, l, d = h.shape
    tl = TOKEN_TILE
    nb8 = l // 8
    tok = pl.BlockSpec((1, tl, d), lambda i, j: (i, j, 0))
    tok2 = pl.BlockSpec((2, 1, tl, d), lambda i, j: (0, i, j, 0))
    full = lambda x: pl.BlockSpec(x.shape, lambda i, j: (0,) * x.ndim)
    sds = jax.ShapeDtypeStruct
    kern, in_specs, args = _after(
        dep, functools.partial(_rwkv_proj_kernel, nct=nct),
        [tok,
         pl.BlockSpec((1, 8, d), lambda i, j: (i, jnp.maximum(j * (tl // 8) - 1, 0), 0)),
         pl.BlockSpec((1, 8, d), lambda i, j: (i, jnp.minimum((j + 1) * (tl // 8), nb8 - 1), 0)),
         pl.BlockSpec((1, 1, N_MODS, d), lambda i, j: (i, jnp.where(j < nct, 0, 1), 0, 0)),
         full(g), full(mu), full(wr), full(wk), full(wv), full(g1), full(g2), full(w1), full(w2),
         full(a1), full(a2), full(w0), full(a0), full(kk), full(ka), full(rk), full(bd)],
        [h, h, h, mods, g, mu, wr, wk, wv, g1, g2, w1, w2, a1, a2, w0, a0, kk, ka, rk, bd])
    return pl.pallas_call(
        kern,
        out_shape=[sds((b, l, d), BF16), sds((b, l, d), BF16), sds((b, l, d), BF16), sds((b, l, d), BF16),
                   sds((2, b, l, d), BF16), sds((2, b, l, d), BF16), sds((2, b, l, d), F32), sds((b, l, d), BF16)],
        grid=(b, l // tl),
        in_specs=in_specs,
        out_specs=[tok, tok, tok, tok, tok2, tok2, tok2, tok],
        compiler_params=pltpu.CompilerParams(dimension_semantics=("parallel", "parallel"),
                                             vmem_limit_bytes=_vmem_limit(56)),
        name="rwkv_proj",
    )(*args)


def _wkv_kernel(r_ref, v_ref, kk_ref, km_ref, b_ref, lw_ref, y_ref, st_ref):
    c = WKV_CHUNK
    w = WKV_PAIR
    rev = pl.program_id(0)
    sign = 1 - 2 * rev

    @pl.when(pl.program_id(2) == 0)
    def _():
        st_ref[...] = jnp.zeros_like(st_ref)

    ti = lax.broadcasted_iota(jnp.int32, (c, c), 0)
    si = lax.broadcasted_iota(jnp.int32, (c, c), 1)
    tri = jnp.where((si - ti) * sign <= 0, 1.0, 0.0).astype(BF16)
    tri3 = jnp.concatenate([tri, tri, tri], axis=1)
    nsub = WKV_CHUNKS_PER_STEP
    subs = [pl.ds(pl.multiple_of(jnp.where(rev == 0, s, nsub - 1 - s) * c, c), c) for s in range(nsub)]
    rt, kt, kh, bh, v32, e_mid = [], [], [], [], [], []
    for rows in subs:
        lw = lw_ref[0, 0, rows, :]
        t1 = lw.astype(BF16)
        d1 = lw - t1.astype(F32)
        t2 = d1.astype(BF16)
        t3 = (d1 - t2.astype(F32)).astype(BF16)
        l_incl = jnp.dot(tri3, jnp.concatenate([t1, t2, t3], axis=0), preferred_element_type=F32)
        mid = 0.5 * jnp.sum(lw, axis=0, keepdims=True)
        e_neg = jnp.exp(mid - l_incl)
        e_mid.append(jnp.exp(mid))
        rt.append(r_ref[0, rows, :].astype(F32) * jnp.exp(l_incl - mid))
        kt.append(kk_ref[0, rows, :].astype(F32) * jnp.exp(l_incl - lw - mid))
        kh.append(km_ref[0, 0, rows, :].astype(F32) * e_neg)
        bh.append(b_ref[0, 0, rows, :].astype(F32) * e_neg)
        v32.append(v_ref[0, rows, :].astype(F32))

    ri = lax.broadcasted_iota(jnp.int32, (w, w), 0)
    ci = lax.broadcasted_iota(jnp.int32, (w, w), 1)
    same = (ri // c) == (ci // c)
    eye = jnp.where(ri == ci, 1.0, 0.0).astype(F32)
    tl_ = lax.broadcasted_iota(jnp.int32, (c, w), 0)
    jl_ = lax.broadcasted_iota(jnp.int32, (c, w), 1) % c
    strict = (jl_ - tl_) * sign < 0
    incl = (jl_ - tl_) * sign <= 0
    eye2 = jnp.where(jl_ == tl_, 1.0, 0.0).astype(F32)
    lane = lax.broadcasted_iota(jnp.int32, (1, w), 1)
    h0 = lane < RWKV_HEAD

    def rows2(x):
        return jnp.concatenate([jnp.where(h0, x, 0.0), jnp.where(h0, 0.0, x)], axis=0)

    npair = st_ref.shape[0]
    items = [(s, slice(p * w, (p + 1) * w)) for s in range(nsub) for p in range(npair)]
    n = range(len(items))
    em = [e_mid[s][:, sl] for s, sl in items]
    g = [_dot_nt(jnp.concatenate([kt[s][:, sl], rt[s][:, sl]], axis=0),
                 jnp.concatenate([rows2(kh[s][:, sl]), rows2(bh[s][:, sl])], axis=0)) for s, sl in items]
    a_kk = [jnp.where(strict, x[:c, :w], 0.0) for x in g]
    a_rk = [jnp.where(incl, x[c:, :w], 0.0) for x in g]
    a_rb = [jnp.where(incl, x[c:, w:], 0.0) for x in g]
    vi = [v32[s][:, sl] for s, sl in items]
    v_rows = [rows2(x) for x in vi]
    av = [_dot(jnp.concatenate([a_kk[i], a_rk[i]], axis=0), v_rows[i]) for i in n]
    r_pre = [x[:c] for x in av]
    ark_v = [x[c:] for x in av]
    m = [jnp.where(strict, -x[:c, w:], 0.0) for x in g]
    tinv = [eye2 + x for x in m]
    m = [_dot(x, rows2(x)) for x in m]
    for _ in range(c.bit_length() - 3):
        both = [_dot(jnp.concatenate([tinv[i], m[i]], axis=0), rows2(m[i])) for i in n]
        tinv = [tinv[i] + both[i][:c] for i in n]
        m = [x[c:] for x in both]
    tinv = [tinv[i] + _dot(tinv[i], rows2(m[i])) for i in n]
    sol = [_dot(tinv[i], jnp.concatenate([rows2(r_pre[i]), rows2(kt[s][:, sl] * em[i])], axis=1))
           for i, (s, sl) in enumerate(items)]
    u_pre = [x[:, :w] for x in sol]
    kq = [x[:, w:] for x in sol]
    arb = [_dot(a_rb[i], jnp.concatenate([rows2(u_pre[i]), rows2(kq[i])], axis=1)) for i in n]
    y_pre = [ark_v[i] - arb[i][:, :w] for i in n]
    r_eff = [rt[s][:, sl] * em[i] - arb[i][:, w:] for i, (s, sl) in enumerate(items)]
    bbar = [bh[s][:, sl] * em[i] for i, (s, sl) in enumerate(items)]
    kbar = [kh[s][:, sl] * em[i] for i, (s, sl) in enumerate(items)]
    mmat = [eye * (em[i] * em[i]) - jnp.where(same, _dot_tn(kq[i], bbar[i]), 0.0) for i in n]
    s_pre = [jnp.where(same, _dot_tn(jnp.concatenate([vi[i], -u_pre[i]], axis=0),
                                     jnp.concatenate([kbar[i], bbar[i]], axis=0)), 0.0) for i in n]
    st = [st_ref[p] for p in range(npair)]
    for i, (s, sl) in enumerate(items):
        p = i % npair
        y_ref[0, 0, subs[s], sl] = (_dot_nt(r_eff[i], st[p]) + y_pre[i]).astype(y_ref.dtype)
        hi = st[p].astype(BF16)
        lo = (st[p] - hi.astype(F32)).astype(BF16)
        mb = mmat[i].astype(BF16)
        both = jnp.dot(jnp.concatenate([hi, lo], axis=0), mb, preferred_element_type=F32)
        st[p] = both[:w] + both[w:] + s_pre[i]
    for p in range(npair):
        st_ref[p] = st[p]


def _wkv(r, v, kk, km, bv, lw, lc, dep=None):
    b, l, d = r.shape
    c = WKV_CHUNK * WKV_CHUNKS_PER_STEP
    ncc = lc // c
    nlc = (l - lc) // c

    def chunk(dr, i):
        return jnp.where(dr == 0, i, jnp.where(i < ncc, ncc - 1 - i, nlc + 2 * ncc - 1 - i))

    shared = pl.BlockSpec((1, c, d), lambda dr, bi, i: (bi, chunk(dr, i), 0))
    per_dir = pl.BlockSpec((1, 1, c, d), lambda dr, bi, i: (dr, bi, chunk(dr, i), 0))
    kern, in_specs, args = _after(dep, _wkv_kernel, [shared, shared, shared, per_dir, per_dir, per_dir],
                                  [r, v, kk, km, bv, lw])
    return pl.pallas_call(
        kern,
        out_shape=jax.ShapeDtypeStruct((2, b, l, d), BF16),
        grid=(2, b, l // c),
        in_specs=in_specs,
        out_specs=per_dir,
        scratch_shapes=[pltpu.VMEM((d // WKV_PAIR, WKV_PAIR, WKV_PAIR), F32)],
        compiler_params=pltpu.CompilerParams(dimension_semantics=("parallel", "parallel", "arbitrary"),
                                             vmem_limit_bytes=_vmem_limit(32)),
        name="wkv7_chunked",
    )(*args)


def _rwkv_out_kernel(y_ref, bonus_ref, g_ref, lnw_ref, lnb_ref, wo_ref, bd_ref, h_ref, mods_ref, gffn_ref,
                     wrt_ref, bias_ref, hn_ref, n2_ref, eid_ref, rank_ref, w_ref, cnt_ref, run_ref):
    y = y_ref[0, 0].astype(F32) + y_ref[1, 0].astype(F32)
    bd = bd_ref[...]
    mean = _head_sum(y, bd) * (1.0 / RWKV_HEAD)
    yc = y - mean
    var = _head_sum(yc * yc, bd) * (1.0 / RWKV_HEAD)
    yn = yc * lax.rsqrt(var + GN_EPS) * lnw_ref[...] + lnb_ref[...]
    out = (yn + bonus_ref[0].astype(F32)) * g_ref[0].astype(F32)
    _mixer_tail(_dot(out, wo_ref[...]), h_ref[0], mods_ref[0, 0], gffn_ref, wrt_ref, bias_ref, hn_ref, n2_ref,
                eid_ref, rank_ref, w_ref, cnt_ref, run_ref)


def _rwkv_out(y, bonus, g, lnw, lnb, wo, bd, h, mods, gffn, wrt, bias, nct, dep=None):
    b, l, d = h.shape
    tl = TOKEN_TILE
    tok = lambda w: pl.BlockSpec((1, tl, w), lambda i, j: (i, j, 0))
    full = lambda x: pl.BlockSpec(x.shape, lambda i, j: (0,) * x.ndim)
    shapes, specs = _tail_outs(b, l, d)
    kern, in_specs, args = _after(
        dep, _rwkv_out_kernel,
        [pl.BlockSpec((2, 1, tl, d), lambda i, j: (0, i, j, 0)), tok(d), tok(d),
         full(lnw), full(lnb), full(wo), full(bd), tok(d),
         pl.BlockSpec((1, 1, N_MODS, d), lambda i, j: (i, jnp.where(j < nct, 0, 1), 0, 0)),
         full(gffn), full(wrt), full(bias)],
        [y, bonus, g, lnw, lnb, wo, bd, h, mods, gffn, wrt, bias])
    return pl.pallas_call(
        kern,
        out_shape=shapes,
        grid=(b, l // tl),
        in_specs=in_specs,
        out_specs=specs,
        scratch_shapes=[pltpu.VMEM((N_EXPERTS, 1), F32)],
        compiler_params=pltpu.CompilerParams(dimension_semantics=("arbitrary", "arbitrary"),
                                             vmem_limit_bytes=_vmem_limit(40)),
        name="rwkv_out",
    )(*args)


def _rope_table(n_lat, n_ctx):
    dim = SWA_HEAD_DIM
    nf = dim // 4
    inv = ROPE_THETA ** (-jnp.arange(nf, dtype=F32) / nf)
    row = jnp.repeat(jnp.arange(n_lat // GRID_W, dtype=F32), GRID_W)
    col = jnp.tile(jnp.arange(GRID_W, dtype=F32), n_lat // GRID_W)
    ar = row[:, None] * inv
    ac = col[:, None] * inv
    ang = jnp.concatenate([ar, ar, ac, ac], axis=-1)
    cos = jnp.concatenate([jnp.ones((n_ctx, dim), F32), jnp.cos(ang)], axis=0)
    sin = jnp.concatenate([jnp.zeros((n_ctx, dim), F32), jnp.sin(ang)], axis=0)
    return jnp.tile(cos, (1, 2)), jnp.tile(sin, (1, 2))


def _layout_attn_weights(w_in, w_uq, w_ukv):
    d = w_in.shape[0]
    s0 = MLA_Q_RANK
    s1 = s0 + MLA_KV_RANK
    s2 = s1 + MLA_ROPE
    s3 = s2 + SWA_HEADS * SWA_HEAD_DIM
    s4 = s3 + SWA_KV_HEADS * SWA_HEAD_DIM
    rep = lambda w: jnp.concatenate(
        [jnp.tile(w[:, g * SWA_HEAD_DIM:(g + 1) * SWA_HEAD_DIM], (1, SWA_GROUP)) for g in range(SWA_KV_HEADS)], axis=1)
    win = jnp.concatenate([w_in[:, :s1], w_in[:, s2:s3], rep(w_in[:, s3:s4]), rep(w_in[:, s4:]),
                           w_in[:, s1:s2], jnp.zeros((d, V7X_LANES - MLA_ROPE), w_in.dtype)], axis=1)
    qh = MLA_NOPE + MLA_ROPE
    pad = jnp.zeros((w_uq.shape[0], V7X_MXU_DIM - qh), w_uq.dtype)
    wuq = jnp.concatenate([jnp.concatenate([w_uq[:, h * qh:(h + 1) * qh], pad], axis=1) for h in range(MLA_HEADS)], axis=1)
    kvh = MLA_NOPE + MLA_V
    wuk = jnp.concatenate([w_ukv[:, h * kvh:h * kvh + MLA_NOPE] for h in range(MLA_HEADS)], axis=1)
    wuvt = jnp.concatenate([w_ukv[:, h * kvh + MLA_NOPE:(h + 1) * kvh] for h in range(MLA_HEADS)], axis=1).T
    return win.astype(BF16), wuq.astype(BF16), wuk.astype(BF16), wuvt.astype(BF16)


def _lora_pair(w_down, w_up):
    rank = w_down.shape[2]
    down = jnp.concatenate([w_down[0], w_down[1]], axis=1)
    z = jnp.zeros((rank, w_up.shape[2]), w_up.dtype)
    up = jnp.stack([jnp.concatenate([w_up[0], z], axis=0), jnp.concatenate([z, w_up[1]], axis=0)], axis=0)
    return down.astype(BF16), up.astype(BF16)


def _head_block_diag():
    i = jnp.arange(V7X_MXU_DIM) // RWKV_HEAD
    return (i[:, None] == i[None, :]).astype(BF16)


def kernel(x, c, ctx, c_ctx, ada_w, ada_b, norm_mix, norm_ffn, norm_final, attn_w_in, attn_q_norm, attn_kv_norm, attn_w_uq, attn_w_ukv, attn_sinks, attn_w_o, rwkv_mu, rwkv_w_r, rwkv_w_k, rwkv_w_v, rwkv_w_o, rwkv_g1, rwkv_g2, rwkv_w0, rwkv_w1, rwkv_w2, rwkv_a0, rwkv_a1, rwkv_a2, rwkv_k_k, rwkv_k_a, rwkv_r_k, rwkv_ln_w, rwkv_ln_b, moe_router, moe_bias, moe_w_gate, moe_w_up, moe_w_down, moe_ws_gate, moe_ws_up, moe_ws_down):
    bsz, s, d = x.shape
    lc = ctx.shape[1]
    l = lc + s
    depth = ada_w.shape[0]
    nct = lc // TOKEN_TILE
    assert lc % TOKEN_TILE == 0 and s % TOKEN_TILE == 0 and s >= SWA_BAND and lc % SWA_Q_TILE == 0
    assert lc % (WKV_CHUNK * WKV_CHUNKS_PER_STEP) == 0
    assert d % V7X_MXU_DIM == 0 and WKV_CHUNK * 2 == V7X_LANES
    ngrp = SAMPLE_GROUPS
    bg = bsz // ngrp
    assert bsz % ngrp == 0 and (bg * l) % (8 * V7X_SC_WORKERS) == 0

    assert ngrp == 2
    cos, sin = _rope_table(s, lc)
    bd = _head_block_diag()
    rows = -(-(bsz + 1) // 8) * 8
    cc = jnp.concatenate([c, c_ctx[None, :], jnp.zeros((rows - bsz - 1, d), F32)], axis=0)
    row2 = lambda a: a.reshape(1, -1)
    moe_w = (moe_w_gate, moe_w_up, moe_w_down)
    moe_ws = (moe_ws_gate, moe_ws_up, moe_ws_down)

    shared = {}

    def layer_weights(li):
        if li not in shared:
            i = li // 2
            ada = _ada_mods(cc, ada_w, ada_b, li)
            w = dict(
                mods=jnp.stack([jnp.broadcast_to(ada[bsz].reshape(1, N_MODS, d), (bsz, N_MODS, d)),
                                ada[:bsz].reshape(bsz, N_MODS, d)], axis=1),
                wrt=moe_router[li].T,
                bias=moe_bias[li].reshape(N_GROUPS, GROUP_SIZE, 1))
            if li % 2 == 0:
                w["win"], w["wuq"], w["wuk"], w["wuvt"] = _layout_attn_weights(attn_w_in[i], attn_w_uq[i], attn_w_ukv[i])
                w["wo"] = attn_w_o[i].astype(BF16)
            else:
                w["w1"], w["w2"] = _lora_pair(rwkv_w1[i], rwkv_w2[i])
                w["a1"], w["a2"] = _lora_pair(rwkv_a1[i], rwkv_a2[i])
                w["wr"], w["wk"], w["wv"], w["wo"] = [x[i].astype(BF16) for x in (rwkv_w_r, rwkv_w_k, rwkv_w_v, rwkv_w_o)]
                w["g1"], w["g2"] = rwkv_g1[i].astype(BF16), rwkv_g2[i].astype(BF16)
            shared[li] = w
        return shared[li]

    groups = [dict(stream=(ctx, x, g * bg, 0), b0=g * bg) for g in range(ngrp)]
    result = [None]

    def run_stage(st, li, name, dep):
        w = layer_weights(li)
        i = li // 2
        with_ctx = li < depth - 1
        mods = w["mods"][st["b0"]:st["b0"] + bg]
        if name == "proj" and li % 2 == 0:
            st["qkv"] = _attn_proj(st["stream"], bg, l, mods, row2(norm_mix[li]), w["win"], row2(attn_q_norm[i]),
                                   row2(attn_kv_norm[i]), w["wuq"], w["wuk"], w["wuvt"], cos, sin, nct, dep=dep)
            return st["qkv"][0]
        if name == "mid" and li % 2 == 0:
            q, k, vt, qs, ks, vs = st.pop("qkv")
            st["a"] = _mla_attention(q, k, vt, lc, 0 if with_ctx else lc // MLA_Q_TILE, dep=dep)
            st["bm"] = _swa_attention(attn_sinks[i], qs, ks, vs, lc, 0 if with_ctx else lc // SWA_Q_TILE, dep=st["a"])
            return st["bm"]
        if name == "proj":
            assert st["stream"][0] is st["stream"][1]
            st["feat"] = _rwkv_proj(st["stream"][0], mods, row2(norm_mix[li]), rwkv_mu[i], w["wr"], w["wk"], w["wv"],
                                    w["g1"], w["g2"], w["w1"], w["w2"], w["a1"], w["a2"], rwkv_w0[i], rwkv_a0[i],
                                    row2(rwkv_k_k[i]), row2(rwkv_k_a[i]), row2(rwkv_r_k[i]), bd, nct, dep=dep)
            return st["feat"][0]
        if name == "mid":
            r, v, kk, gt, km, bv, lw, bonus = st.pop("feat")
            st["y"] = _wkv(r, v, kk, km, bv, lw, lc, dep=dep)
            st["gate"], st["bonus"] = gt, bonus
            return st["y"]
        if name == "out":
            if li % 2 == 0:
                tail = _attn_out(st.pop("a"), st.pop("bm"), st["stream"], mods, w["wo"], row2(norm_ffn[li]),
                                 w["wrt"], w["bias"], nct, dep=dep)
            else:
                tail = _rwkv_out(st.pop("y"), st.pop("bonus"), st.pop("gate"), row2(rwkv_ln_w[i]), row2(rwkv_ln_b[i]),
                                 w["wo"], bd, st["stream"][0], mods, row2(norm_ffn[li]), w["wrt"], w["bias"], nct, dep=dep)
            st["h"], st["n2p"], eid, rank, st["wcols"], counts = tail
            st["xs"], st["dest"], st["tile_expert"], st["n_valid"] = _moe_route_rows(st["n2p"], eid, rank, counts, bg, l)
            return st["h"]
        if name == "experts":
            ys = _moe_experts(st.pop("tile_expert"), st.pop("n_valid"), st.pop("xs"), *moe_w, li, dep=dep)
            st["yg"] = _sc_gather(ys, st.pop("dest"), bg * l).reshape(TOP_K, bg, l, d // 2)
            return ys
        assert name == "combine"
        last = li == depth - 1
        h = _moe_combine(st.pop("yg"), st.pop("wcols"), st.pop("n2p"), *moe_ws, st.pop("h"), mods, row2(norm_final),
                         nct, li, result[0] if last else None, st["b0"] if last else 0, bsz if last else bg,
                         last, last, dep=dep)
        if last:
            result[0] = h
        else:
            st["stream"] = (h, h, 0, nct)
        return h

    order = [(0, 0, "proj"), (0, 0, "mid")]
    for li in range(depth):
        order += [(0, li, "out"), (1, li, "proj"), (0, li, "experts"), (1, li, "mid")]
        if li < depth - 1:
            order += [(0, li, "combine"), (1, li, "out"), (0, li + 1, "proj"), (1, li, "experts"),
                      (0, li + 1, "mid"), (1, li, "combine")]
        else:
            order += [(1, li, "out"), (0, li, "combine"), (1, li, "experts"), (1, li, "combine")]
    dep = None
    for g, li, name in order:
        dep = run_stage(groups[g], li, name, dep)
    return result[0]
```

```python
import functools

import jax
import jax.numpy as jnp
from jax import lax
from jax.experimental import pallas as pl
from jax.experimental.pallas import tpu as pltpu
from jax.experimental.pallas import tpu_sc as plsc

F32 = jnp.float32
BF16 = jnp.bfloat16
HIGHEST = lax.Precision.HIGHEST

GRID_W = 64
NORM_EPS = 1e-6
ROPE_THETA = 10000.0
NEG_INF = -1e30
N_MODS = 6

MLA_HEADS = 4
MLA_Q_RANK = 384
MLA_KV_RANK = 256
MLA_NOPE = 128
MLA_ROPE = 64
MLA_V = 128

SWA_HEADS = 8
SWA_KV_HEADS = 2
SWA_GROUP = SWA_HEADS // SWA_KV_HEADS
SWA_HEAD_DIM = 64
WINDOW = 128

RWKV_HEAD = 64
DECAY_LORA = 64
ICLR_LORA = 64
GATE_LORA = 128
GN_EPS = 64e-5

N_EXPERTS = 64
TOP_K = 6
N_GROUPS = 8
TOPK_GROUPS = 4
GROUP_SIZE = N_EXPERTS // N_GROUPS
ROUTED_SCALE = 2.5
GATE_W = 128

V7X_LANES = 128
V7X_MXU_DIM = 256
V7X_VMEM_BYTES = 64 * 1024 * 1024
V7X_SC_CORES = 2
V7X_SC_SUBCORES = 16
V7X_SC_WORKERS = V7X_SC_CORES * V7X_SC_SUBCORES

TOKEN_TILE = 256
MLA_Q_TILE = 256
MLA_HEADS_PER_STEP = 2
SWA_Q_TILE = 256
SWA_BAND = SWA_Q_TILE + 2 * WINDOW
WKV_CHUNK = 64
WKV_PAIR = 2 * RWKV_HEAD
WKV_CHUNKS_PER_STEP = 4
MOE_ROW_TILE = 512
MOE_ROW_SLOTS = 3
SAMPLE_GROUPS = 2
SC_MAX_CHUNK = 64

LOG2E = 1.4426950408889634
MIB = 1024 * 1024
VMEM_RESERVE_BYTES = 4 * MIB


def _vmem_limit(mib):
    return min(mib * MIB, V7X_VMEM_BYTES - VMEM_RESERVE_BYTES)


def _dot(a, b):
    return jnp.dot(a.astype(BF16), b.astype(BF16), preferred_element_type=F32)


def _dot_nt(a, b):
    return lax.dot_general(a.astype(BF16), b.astype(BF16), (((1,), (1,)), ((), ())),
                           preferred_element_type=F32)


def _dot_tn(a, b):
    return lax.dot_general(a.astype(BF16), b.astype(BF16), (((0,), (0,)), ((), ())),
                           preferred_element_type=F32)


def _sigmoid(x):
    return 1.0 / (1.0 + jnp.exp(-x))


def _silu(x):
    return x * _sigmoid(x)


def _rms(x, g):
    return x * lax.rsqrt(jnp.mean(x * x, axis=-1, keepdims=True) + NORM_EPS) * g


def _norm_mod(x, g, shift, scale):
    return _rms(x, g) * (1.0 + scale) + shift


def _split_dot(x, w):
    hi = x.astype(BF16)
    lo = (x - hi.astype(F32)).astype(BF16)
    return (jnp.dot(hi, w, preferred_element_type=F32) + jnp.dot(lo, w, preferred_element_type=F32))


def _head_sum(x, bd):
    w = bd.shape[0]
    parts = [_split_dot(x[:, c * w:(c + 1) * w], bd) for c in range(x.shape[1] // w)]
    return jnp.concatenate(parts, axis=1)


def _after(dep, kernel, in_specs, args, n_lead=0):
    if dep is None:
        return kernel, list(in_specs), list(args)
    n_in = n_lead + len(in_specs)

    def ordered(*refs):
        return kernel(*refs[:n_in], *refs[n_in + 1:])

    return ordered, list(in_specs) + [pl.BlockSpec(memory_space=pl.ANY)], list(args) + [dep]


def _ada_kernel(c_ref, w_ref, b_ref, o_ref):
    s = _silu(c_ref[...])
    o_ref[...] = jnp.dot(s, w_ref[0], precision=HIGHEST, preferred_element_type=F32) + b_ref[0]


def _ada_mods(cc, w, b, layer):
    rows, d = cc.shape
    depth, _, n = w.shape
    return pl.pallas_call(
        _ada_kernel,
        out_shape=jax.ShapeDtypeStruct((rows, n), F32),
        grid=(n // d,),
        in_specs=[pl.BlockSpec((rows, d), lambda i: (0, 0)),
                  pl.BlockSpec((1, d, d), lambda i: (layer, 0, i)),
                  pl.BlockSpec((1, 1, d), lambda i: (layer, 0, i))],
        out_specs=pl.BlockSpec((rows, d), lambda i: (0, i)),
        compiler_params=pltpu.CompilerParams(dimension_semantics=("parallel",),
                                             vmem_limit_bytes=_vmem_limit(32)),
        name="ada_mods",
    )(cc, w, b.reshape(depth, 1, n))


def _rope128(x, cos, sin, first_half):
    rot = jnp.where(first_half, -pltpu.roll(x, V7X_LANES - 16, axis=1), pltpu.roll(x, 16, axis=1))
    return x * cos + rot * sin


_C_CQ = 0
_C_CKV = _C_CQ + MLA_Q_RANK
_C_QS = _C_CKV + MLA_KV_RANK
_C_KS = _C_QS + SWA_HEADS * SWA_HEAD_DIM
_C_VS = _C_KS + SWA_KV_HEADS * V7X_LANES
_C_KR = _C_VS + SWA_KV_HEADS * V7X_LANES
_C_END = _C_KR + V7X_LANES
_SWA_W = SWA_KV_HEADS * V7X_MXU_DIM
_MLA_QK_W = MLA_HEADS * V7X_MXU_DIM


def _stream_specs(stream, nct, tl):
    ctx_arr, lat_arr, b0, lat_off = stream
    d = ctx_arr.shape[2]
    return [pl.BlockSpec((1, tl, d), lambda i, j: (i + b0, jnp.minimum(j, nct - 1), 0)),
            pl.BlockSpec((1, tl, d), lambda i, j: (i + b0, jnp.maximum(j - nct, 0) + lat_off, 0))]


def _stream_tile(c_ref, x_ref, nct):
    rows = c_ref.shape[1]
    take_ctx = lax.broadcasted_iota(jnp.int32, (rows, 1), 0) < jnp.where(pl.program_id(1) < nct, rows, 0)
    return jnp.where(take_ctx, c_ref[0], x_ref[0])


def _attn_proj_kernel(c_ref, x_ref, mods_ref, g_ref, win_ref, qn_ref, kvn_ref, wuq_ref, wuk_ref, wuvt_ref, cos_ref,
                      sin_ref, q_ref, k_ref, vt_ref, qs_ref, ks_ref, vs_ref, *, nct):
    m = mods_ref[0, 0]
    n = _norm_mod(_stream_tile(c_ref, x_ref, nct), g_ref[...], m[0:1], m[1:2])
    u = _dot(n, win_ref[...])
    cos = cos_ref[...]
    sin = sin_ref[...]
    lane = lax.broadcasted_iota(jnp.int32, (1, V7X_LANES), 1)
    first_half = (lane % 32) < 16

    def rope(x):
        return _rope128(x, cos, sin, first_half)

    scale_a = (MLA_NOPE + MLA_ROPE) ** -0.5 * LOG2E
    scale_b = SWA_HEAD_DIM ** -0.5 * LOG2E
    q = _dot(_rms(u[:, _C_CQ:_C_CKV], qn_ref[...]), wuq_ref[...])
    ckv = _rms(u[:, _C_CKV:_C_QS], kvn_ref[...])
    kn = _dot(ckv, wuk_ref[...])
    vt_ref[0] = _dot_nt(wuvt_ref[...], ckv).astype(BF16)
    kr = rope(u[:, _C_KR:_C_END]).astype(BF16)
    for h in range(MLA_HEADS):
        o = h * V7X_MXU_DIM
        q_ref[0, :, o:o + V7X_LANES] = (q[:, o:o + V7X_LANES] * scale_a).astype(BF16)
        q_ref[0, :, o + V7X_LANES:o + V7X_MXU_DIM] = (rope(q[:, o + V7X_LANES:o + V7X_MXU_DIM]) * scale_a).astype(BF16)
        k_ref[0, :, o:o + V7X_LANES] = kn[:, h * MLA_NOPE:(h + 1) * MLA_NOPE].astype(BF16)
        k_ref[0, :, o + V7X_LANES:o + V7X_MXU_DIM] = kr
    for c in range((_C_KS - _C_QS) // V7X_LANES):
        o = c * V7X_LANES
        qs_ref[0, :, o:o + V7X_LANES] = (rope(u[:, _C_QS + o:_C_QS + o + V7X_LANES]) * scale_b).astype(BF16)
    for g in range(SWA_KV_HEADS):
        o = g * V7X_LANES
        ks = rope(u[:, _C_KS + o:_C_KS + o + V7X_LANES]).astype(BF16)
        vs = u[:, _C_VS + o:_C_VS + o + V7X_LANES].astype(BF16)
        for c in range(V7X_MXU_DIM // V7X_LANES):
            oo = g * V7X_MXU_DIM + c * V7X_LANES
            ks_ref[0, :, oo:oo + V7X_LANES] = ks
            vs_ref[0, :, oo:oo + V7X_LANES] = vs


def _attn_proj(stream, b, l, mods, g, win, qn, kvn, wuq, wuk, wuvt, cos, sin, nct, dep=None):
    d = stream[0].shape[2]
    tl = TOKEN_TILE
    tok = lambda w: pl.BlockSpec((1, tl, w), lambda i, j: (i, j, 0))
    full = lambda a: pl.BlockSpec(a.shape, lambda i, j: (0,) * a.ndim)
    sds = jax.ShapeDtypeStruct
    dv = MLA_HEADS * MLA_V
    kern, in_specs, args = _after(
        dep, functools.partial(_attn_proj_kernel, nct=nct),
        _stream_specs(stream, nct, tl) + [
            pl.BlockSpec((1, 1, N_MODS, d), lambda i, j: (i, jnp.where(j < nct, 0, 1), 0, 0)),
            full(g), full(win), full(qn), full(kvn), full(wuq), full(wuk), full(wuvt),
            pl.BlockSpec((tl, V7X_LANES), lambda i, j: (j, 0)),
            pl.BlockSpec((tl, V7X_LANES), lambda i, j: (j, 0))],
        [stream[0], stream[1], mods, g, win, qn, kvn, wuq, wuk, wuvt, cos, sin])
    return pl.pallas_call(
        kern,
        out_shape=[sds((b, l, _MLA_QK_W), BF16), sds((b, l, _MLA_QK_W), BF16), sds((b, dv, l), BF16),
                   sds((b, l, SWA_HEADS * SWA_HEAD_DIM), BF16), sds((b, l, _SWA_W), BF16), sds((b, l, _SWA_W), BF16)],
        grid=(b, l // tl),
        in_specs=in_specs,
        out_specs=[tok(_MLA_QK_W), tok(_MLA_QK_W), pl.BlockSpec((1, dv, tl), lambda i, j: (i, 0, j)),
                   tok(SWA_HEADS * SWA_HEAD_DIM), tok(_SWA_W), tok(_SWA_W)],
        compiler_params=pltpu.CompilerParams(dimension_semantics=("parallel", "parallel"),
                                             vmem_limit_bytes=_vmem_limit(48)),
        name="attn_proj",
    )(*args)


def _mla_kernel(q_ref, k_ref, vt_ref, o_ref, *, nct_q, lc):
    hw = V7X_MXU_DIM

    def attend(nk):
        st = [_dot_nt(k_ref[0, 0:nk, hh * hw:(hh + 1) * hw], q_ref[0, :, hh * hw:(hh + 1) * hw])
              for hh in range(MLA_HEADS_PER_STEP)]
        for hh, s in enumerate(st):
            p = jnp.exp2(s - jnp.max(s, axis=0, keepdims=True))
            den = jnp.sum(p, axis=0, keepdims=True)
            ot = _dot(vt_ref[0, hh * MLA_V:(hh + 1) * MLA_V, 0:nk], p) / den
            o_ref[0, :, hh * MLA_V:(hh + 1) * MLA_V] = ot.T.astype(o_ref.dtype)

    @pl.when(pl.program_id(2) < nct_q)
    def _():
        attend(lc)

    @pl.when(pl.program_id(2) >= nct_q)
    def _():
        attend(k_ref.shape[1])


def _mla_attention(q, k, vt, lc, q_tile0, dep=None):
    b, l, _ = q.shape
    tq = MLA_Q_TILE
    hps = MLA_HEADS_PER_STEP
    kern, in_specs, args = _after(
        dep, functools.partial(_mla_kernel, nct_q=lc // tq - q_tile0, lc=lc),
        [pl.BlockSpec((1, tq, hps * V7X_MXU_DIM), lambda i, h, j: (i, j + q_tile0, h)),
         pl.BlockSpec((1, l, hps * V7X_MXU_DIM), lambda i, h, j: (i, 0, h)),
         pl.BlockSpec((1, hps * MLA_V, l), lambda i, h, j: (i, h, 0))],
        [q, k, vt])
    return pl.pallas_call(
        kern,
        out_shape=jax.ShapeDtypeStruct((b, l, MLA_HEADS * MLA_V), BF16),
        grid=(b, MLA_HEADS // hps, l // tq - q_tile0),
        in_specs=in_specs,
        out_specs=pl.BlockSpec((1, tq, hps * MLA_V), lambda i, h, j: (i, j + q_tile0, h)),
        compiler_params=pltpu.CompilerParams(dimension_semantics=("parallel", "parallel", "parallel"),
                                             vmem_limit_bytes=_vmem_limit(48)),
        name="mla_attention",
    )(*args)


def _swa_kernel(sink_ref, q_ref, k_ref, v_ref, o_ref, *, lc, q_tile0):
    tq = SWA_Q_TILE
    l = k_ref.shape[1]
    r0 = (pl.program_id(1) + q_tile0) * tq
    start = pl.multiple_of(jnp.clip(r0 - WINDOW, lc, l - SWA_BAND), WINDOW)
    rows = SWA_GROUP * tq
    row = lax.broadcasted_iota(jnp.int32, (rows, 1), 0)
    qpos = jnp.where(r0 >= lc, r0, -l) + row % tq
    kpos = start + lax.broadcasted_iota(jnp.int32, (1, SWA_BAND), 1)
    valid = jnp.abs(qpos - kpos) <= WINDOW
    lane = lax.broadcasted_iota(jnp.int32, (1, V7X_MXU_DIM), 1)
    head = [(lane // SWA_HEAD_DIM) == hh for hh in range(SWA_GROUP)]
    groups = range(SWA_KV_HEADS)
    sls = [slice(g * V7X_MXU_DIM, (g + 1) * V7X_MXU_DIM) for g in groups]
    qstack = []
    for sl in sls:
        qg = q_ref[0, :, sl]
        zero = jnp.zeros_like(qg)
        qstack.append(jnp.concatenate([jnp.where(head[hh], qg, zero) for hh in range(SWA_GROUP)], axis=0))
    sc = [_dot_nt(qstack[g], k_ref[0, 0:lc, sls[g]]) for g in groups]
    sb = [_dot_nt(qstack[g], k_ref[0, pl.ds(start, SWA_BAND), sls[g]]) for g in groups]
    for g in groups:
        sl = sls[g]
        sbm = jnp.where(valid, sb[g], NEG_INF)
        sk = jnp.zeros((rows, 1), F32)
        for hh in range(SWA_GROUP):
            sk = jnp.where(row // tq == hh, sink_ref[g * SWA_GROUP + hh] * LOG2E, sk)
        mx = jnp.maximum(jnp.maximum(jnp.max(sc[g], axis=-1, keepdims=True), jnp.max(sbm, axis=-1, keepdims=True)), sk)
        pc = jnp.exp2(sc[g] - mx)
        pb = jnp.exp2(sbm - mx)
        den = jnp.sum(pc, axis=-1, keepdims=True) + jnp.sum(pb, axis=-1, keepdims=True) + jnp.exp2(sk - mx)
        ostack = (_dot(pc, v_ref[0, 0:lc, sl]) + _dot(pb, v_ref[0, pl.ds(start, SWA_BAND), sl])) / den
        o = jnp.zeros((tq, V7X_MXU_DIM), F32)
        for hh in range(SWA_GROUP):
            o = o + jnp.where(head[hh], ostack[hh * tq:(hh + 1) * tq], 0.0)
        o_ref[0, :, sl] = o.astype(o_ref.dtype)


def _swa_attention(sinks, q, k, v, lc, q_tile0, dep=None):
    b, l, _ = q.shape
    tq = SWA_Q_TILE
    kern, in_specs, args = _after(
        dep, functools.partial(_swa_kernel, lc=lc, q_tile0=q_tile0),
        [pl.BlockSpec(memory_space=pltpu.SMEM),
         pl.BlockSpec((1, tq, SWA_HEADS * SWA_HEAD_DIM), lambda i, j: (i, j + q_tile0, 0)),
         pl.BlockSpec((1, l, _SWA_W), lambda i, j: (i, 0, 0)),
         pl.BlockSpec((1, l, _SWA_W), lambda i, j: (i, 0, 0))],
        [sinks, q, k, v])
    return pl.pallas_call(
        kern,
        out_shape=jax.ShapeDtypeStruct((b, l, SWA_HEADS * SWA_HEAD_DIM), BF16),
        grid=(b, l // tq - q_tile0),
        in_specs=in_specs,
        out_specs=pl.BlockSpec((1, tq, SWA_HEADS * SWA_HEAD_DIM), lambda i, j: (i, j + q_tile0, 0)),
        compiler_params=pltpu.CompilerParams(dimension_semantics=("parallel", "parallel"),
                                             vmem_limit_bytes=_vmem_limit(48)),
        name="swa_attention",
    )(*args)


def _pack_bf16_pair(x):
    w = x.shape[1] // 2
    lo = pltpu.bitcast(x[:, :w].astype(BF16).astype(F32), jnp.int32)
    hi = pltpu.bitcast(x[:, w:].astype(BF16).astype(F32), jnp.int32)
    return lax.shift_right_logical(lo, jnp.int32(16)) | (hi & jnp.int32(-65536))


def _unpack_bf16_pair(p):
    return pltpu.bitcast(p << 16, F32), pltpu.bitcast(p & jnp.int32(-65536), F32)


def _route(n2, wrt, bias, run_ref):
    n_hi = n2.astype(BF16)
    n_lo = (n2 - n_hi.astype(F32)).astype(BF16)
    w_hi = wrt.astype(BF16)
    w_lo = (wrt - w_hi.astype(F32)).astype(BF16)
    w_both = jnp.concatenate([w_hi, w_lo], axis=0)
    rows = n2.shape[0]
    score_blocks = []
    for o in range(0, rows, V7X_LANES):
        both = _dot_nt(w_both, n_hi[o:o + V7X_LANES])
        logits = both[:N_EXPERTS] + both[N_EXPERTS:] + _dot_nt(w_hi, n_lo[o:o + V7X_LANES])
        score_blocks.append(_sigmoid(logits))
    scores = jnp.concatenate(score_blocks, axis=1)

    def select(sc2):
        cols = sc2.shape[1]
        shape3 = (N_GROUPS, GROUP_SIZE, cols)
        choice = sc2.reshape(shape3) + bias
        ji = lax.broadcasted_iota(jnp.int32, shape3, 1).astype(F32)
        m1 = jnp.max(choice, axis=1, keepdims=True)
        first = jnp.min(jnp.where(choice == m1, ji, float(GROUP_SIZE)), axis=1, keepdims=True)
        m2 = jnp.max(jnp.where(ji == first, -jnp.inf, choice), axis=1, keepdims=True)
        gs = m1 + m2
        gidx = lax.broadcasted_iota(jnp.int32, gs.shape, 0).astype(F32)
        gsel = jnp.zeros_like(gs)
        for _ in range(TOPK_GROUPS):
            mx = jnp.max(gs, axis=0, keepdims=True)
            pick = gidx == jnp.min(jnp.where(gs == mx, gidx, float(N_GROUPS)), axis=0, keepdims=True)
            gsel = jnp.where(pick, 1.0, gsel)
            gs = jnp.where(pick, -jnp.inf, gs)
        cand = jnp.where(gsel > 0.0, choice, -jnp.inf).reshape(N_EXPERTS, cols)
        eidx = lax.broadcasted_iota(jnp.int32, (N_EXPERTS, cols), 0).astype(F32)
        out = []
        for _ in range(TOP_K):
            mx = jnp.max(cand, axis=0, keepdims=True)
            pick = eidx == jnp.min(jnp.where(cand == mx, eidx, float(N_EXPERTS)), axis=0, keepdims=True)
            out.append(jnp.where(pick, 1.0, 0.0))
            cand = jnp.where(pick, -jnp.inf, cand)
        return out

    blocks = [select(sc2) for sc2 in score_blocks]
    picks = [jnp.concatenate([blk[k] for blk in blocks], axis=1) > 0.0 for k in range(TOP_K)]
    ei = lax.broadcasted_iota(jnp.int32, (N_EXPERTS, rows), 0).astype(F32)
    esel = jnp.zeros((N_EXPERTS, rows), F32)
    for pick in picks:
        esel = jnp.where(pick, 1.0, esel)
    before = jnp.where(lax.broadcasted_iota(jnp.int32, (rows, rows), 0) < lax.broadcasted_iota(jnp.int32, (rows, rows), 1),
                       1.0, 0.0).astype(BF16)
    slot = jnp.dot(esel.astype(BF16), before, preferred_element_type=F32) + run_ref[...]
    run_ref[...] += jnp.sum(esel, axis=1, keepdims=True)
    sc = [jnp.sum(jnp.where(pick, scores, 0.0), axis=0, keepdims=True) for pick in picks]
    tot = sc[0]
    for x in sc[1:]:
        tot = tot + x
    k8 = lax.broadcasted_iota(jnp.int32, (8, rows), 0)
    kw = lax.broadcasted_iota(jnp.int32, (GATE_W, rows), 0)
    eid = jnp.zeros((8, rows), jnp.int32)
    rank = jnp.zeros((8, rows), jnp.int32)
    wk = jnp.zeros((GATE_W, rows), F32)
    for k, pick in enumerate(picks):
        e_k = jnp.sum(jnp.where(pick, ei, 0.0), axis=0, keepdims=True).astype(jnp.int32)
        r_k = jnp.sum(jnp.where(pick, slot, 0.0), axis=0, keepdims=True).astype(jnp.int32)
        eid = jnp.where(k8 == k, e_k, eid)
        rank = jnp.where(k8 == k, r_k, rank)
        wk = jnp.where(kw == k, sc[k] * (ROUTED_SCALE / tot), wk)
    return eid, rank, wk.T


def _mixer_tail(o, h, m, gffn_ref, wrt_ref, bias_ref, hn_ref, n2_ref, eid_ref, rank_ref, w_ref, cnt_ref, run_ref):
    @pl.when((pl.program_id(0) == 0) & (pl.program_id(1) == 0))
    def _():
        run_ref[...] = jnp.zeros_like(run_ref)

    hn = h + m[2:3] * o
    hn_ref[0] = hn
    n2 = _norm_mod(hn, gffn_ref[...], m[3:4], m[4:5])
    n2_ref[0] = _pack_bf16_pair(n2)
    eid, rank, wcols = _route(n2, wrt_ref[...], bias_ref[...], run_ref)
    eid_ref[0] = eid
    rank_ref[0] = rank
    w_ref[0] = wcols
    cnt_ref[...] = run_ref[...]


def _attn_out_kernel(a_ref, b_ref, c_ref, x_ref, mods_ref, wo_ref, gffn_ref, wrt_ref, bias_ref,
                     hn_ref, n2_ref, eid_ref, rank_ref, w_ref, cnt_ref, run_ref, *, nct):
    wa = MLA_HEADS * MLA_V
    o = _dot(a_ref[0], wo_ref[0:wa, :]) + _dot(b_ref[0], wo_ref[wa:, :])
    _mixer_tail(o, _stream_tile(c_ref, x_ref, nct), mods_ref[0, 0], gffn_ref, wrt_ref, bias_ref, hn_ref, n2_ref,
                eid_ref, rank_ref, w_ref, cnt_ref, run_ref)


def _tail_outs(b, l, d):
    tl = TOKEN_TILE
    nt = l // tl
    sds = jax.ShapeDtypeStruct
    tok = lambda w: pl.BlockSpec((1, tl, w), lambda i, j: (i, j, 0))
    blk = pl.BlockSpec((1, 8, tl), lambda i, j: (i * nt + j, 0, 0))
    shapes = [sds((b, l, d), F32), sds((b, l, d // 2), jnp.int32), sds((b * nt, 8, tl), jnp.int32),
              sds((b * nt, 8, tl), jnp.int32), sds((b, l, GATE_W), F32), sds((N_EXPERTS, 1), F32)]
    specs = [tok(d), tok(d // 2), blk, blk, tok(GATE_W), pl.BlockSpec((N_EXPERTS, 1), lambda i, j: (0, 0))]
    return shapes, specs


def _attn_out(a, bm, stream, mods, wo, gffn, wrt, bias, nct, dep=None):
    b, l, _ = a.shape
    d = stream[0].shape[2]
    tl = TOKEN_TILE
    tok = lambda w: pl.BlockSpec((1, tl, w), lambda i, j: (i, j, 0))
    full = lambda x: pl.BlockSpec(x.shape, lambda i, j: (0,) * x.ndim)
    shapes, specs = _tail_outs(b, l, d)
    kern, in_specs, args = _after(
        dep, functools.partial(_attn_out_kernel, nct=nct),
        [tok(a.shape[2]), tok(bm.shape[2])] + _stream_specs(stream, nct, tl) + [
            pl.BlockSpec((1, 1, N_MODS, d), lambda i, j: (i, jnp.where(j < nct, 0, 1), 0, 0)),
            full(wo), full(gffn), full(wrt), full(bias)],
        [a, bm, stream[0], stream[1], mods, wo, gffn, wrt, bias])
    return pl.pallas_call(
        kern,
        out_shape=shapes,
        grid=(b, l // tl),
        in_specs=in_specs,
        out_specs=specs,
        scratch_shapes=[pltpu.VMEM((N_EXPERTS, 1), F32)],
        compiler_params=pltpu.CompilerParams(dimension_semantics=("arbitrary", "arbitrary"),
                                             vmem_limit_bytes=_vmem_limit(40)),
        name="attn_out",
    )(*args)


def _moe_dest_kernel(off_ref, eid_ref, rank_ref, dest_ref):
    eid = eid_ref[...]
    dest = rank_ref[...]
    for e in range(N_EXPERTS):
        dest = dest + jnp.where(eid == e, off_ref[e], 0)
    dest_ref[...] = dest


def _moe_dest(off, eid, rank):
    return pl.pallas_call(
        _moe_dest_kernel,
        out_shape=jax.ShapeDtypeStruct(eid.shape, jnp.int32),
        in_specs=[pl.BlockSpec(memory_space=pltpu.SMEM),
                  pl.BlockSpec(eid.shape, lambda: (0, 0, 0)), pl.BlockSpec(eid.shape, lambda: (0, 0, 0))],
        out_specs=pl.BlockSpec(eid.shape, lambda: (0, 0, 0)),
        name="moe_dest",
    )(off, eid, rank)


def _sc_mesh():
    return plsc.VectorSubcoreMesh(core_axis_name="c", subcore_axis_name="s",
                                  num_cores=V7X_SC_CORES, num_subcores=V7X_SC_SUBCORES)


def _sc_chunk(rows_per_worker):
    return max(c for c in range(8, SC_MAX_CHUNK + 1, 8) if rows_per_worker % c == 0)


def _sc_dispatch(xp, dest, p_rows):
    t, w = xp.shape
    tpw = t // V7X_SC_WORKERS
    ch = _sc_chunk(tpw)

    @functools.partial(
        pl.kernel, mesh=_sc_mesh(), out_type=jax.ShapeDtypeStruct((p_rows, w), xp.dtype),
        scratch_types=[pltpu.VMEM((ch, w), xp.dtype)] + [pltpu.VMEM((ch,), jnp.int32)] * TOP_K
        + [pltpu.SemaphoreType.DMA, pltpu.SemaphoreType.DMA],
        name="moe_dispatch")
    def run(x_hbm, dest_hbm, out_hbm, rows_v, *rest):
        idx, (sem_i, sem_o) = rest[:TOP_K], rest[TOP_K:]
        base = (lax.axis_index("s") * V7X_SC_CORES + lax.axis_index("c")) * tpw

        @pl.loop(0, tpw // ch)
        def _(i):
            t0 = base + i * ch
            loads = [pltpu.async_copy(dest_hbm.at[k, pl.ds(t0, ch)], idx[k], sem_i) for k in range(TOP_K)]
            pltpu.sync_copy(x_hbm.at[pl.ds(t0, ch)], rows_v)
            for c in loads:
                c.wait()
            puts = [pltpu.async_copy(rows_v, out_hbm.at[idx[k]], sem_o) for k in range(TOP_K)]
            for c in puts:
                c.wait()

    return run(xp, dest)


def _sc_gather(ys, dest, t):
    w = ys.shape[1]
    tpw = t // V7X_SC_WORKERS
    ch = _sc_chunk(tpw)

    @functools.partial(
        pl.kernel, mesh=_sc_mesh(), out_type=jax.ShapeDtypeStruct((TOP_K, t, w), ys.dtype),
        scratch_types=[pltpu.VMEM((ch, w), ys.dtype)] * 2 + [pltpu.VMEM((ch,), jnp.int32)] * TOP_K
        + [pltpu.SemaphoreType.DMA] * 5,
        name="moe_gather")
    def run(y_hbm, dest_hbm, out_hbm, rows_a, rows_b, *rest):
        idx, (sem_i, sem_ga, sem_gb, sem_wa, sem_wb) = rest[:TOP_K], rest[TOP_K:]
        rows, sem_g, sem_w = (rows_a, rows_b), (sem_ga, sem_gb), (sem_wa, sem_wb)
        base = (lax.axis_index("s") * V7X_SC_CORES + lax.axis_index("c")) * tpw

        @pl.loop(0, tpw // ch)
        def _(i):
            t0 = base + i * ch
            loads = [pltpu.async_copy(dest_hbm.at[k, pl.ds(t0, ch)], idx[k], sem_i) for k in range(TOP_K)]
            for c in loads:
                c.wait()
            gets, puts = [None] * TOP_K, [None] * TOP_K
            gets[0] = pltpu.async_copy(y_hbm.at[idx[0]], rows[0], sem_g[0])
            for k in range(TOP_K):
                if k + 1 < TOP_K:
                    if k >= 1:
                        puts[k - 1].wait()
                    gets[k + 1] = pltpu.async_copy(y_hbm.at[idx[k + 1]], rows[(k + 1) % 2], sem_g[(k + 1) % 2])
                gets[k].wait()
                puts[k] = pltpu.async_copy(rows[k % 2], out_hbm.at[k, pl.ds(t0, ch)], sem_w[k % 2])
            puts[TOP_K - 2].wait()
            puts[TOP_K - 1].wait()

    return run(ys, dest)


def _cache_mlp_weights(wg, wu, wd, wgu_ref, wdb_ref):
    f = wg.shape[1]
    wgu_ref[:, 0:f] = wg.astype(BF16)
    wgu_ref[:, f:] = wu.astype(BF16)
    wdb_ref[...] = wd.astype(BF16)


def _gated_mlp(xp, wgu_ref, wdb_ref):
    lo, hi = _unpack_bf16_pair(xp)
    x = jnp.concatenate([lo.astype(BF16), hi.astype(BF16)], axis=1)
    gu = jnp.dot(x, wgu_ref[...], preferred_element_type=F32)
    f = gu.shape[1] // 2
    return _dot(_silu(gu[:, :f]) * gu[:, f:], wdb_ref[...])


def _moe_expert_kernel(te_ref, tb_ref, nv_ref, x_hbm, wga_ref, wua_ref, wda_ref, wgb_ref, wub_ref, wdb_ref, y_ref,
                       gu_a, dn_a, gu_b, dn_b, ids_ref, xbuf, sems):
    i = pl.program_id(0)
    tm = MOE_ROW_TILE
    nv = nv_ref[0]
    last = (nv - 1) // 2
    first = 2 * jnp.minimum(i, last)
    ea = te_ref[first]
    eb = te_ref[first + 1]
    two = 2 * i + 1 < nv

    def rows_copy(step):
        slot = step % MOE_ROW_SLOTS
        row0 = step * (2 * tm)
        rows = pl.ds(row0 if isinstance(step, int) else pl.multiple_of(row0, 2 * tm), 2 * tm)
        return pltpu.make_async_copy(x_hbm.at[rows], xbuf.at[slot], sems.at[slot])

    @pl.when(i == 0)
    def _():
        ids_ref[0] = -1
        ids_ref[1] = -1
        for ahead in range(MOE_ROW_SLOTS - 1):
            @pl.when(ahead <= last)
            def _():
                rows_copy(ahead).start()

    @pl.when(i + (MOE_ROW_SLOTS - 1) <= last)
    def _():
        rows_copy(i + (MOE_ROW_SLOTS - 1)).start()

    @pl.when(i <= last)
    def _():
        rows_copy(i).wait()

    x_ref = xbuf.at[i % MOE_ROW_SLOTS]

    @pl.when(ids_ref[0] != ea)
    def _():
        _cache_mlp_weights(wga_ref[0, 0], wua_ref[0, 0], wda_ref[0, 0], gu_a, dn_a)
        ids_ref[0] = ea

    @pl.when(two & (eb != ea) & (ids_ref[1] != eb))
    def _():
        _cache_mlp_weights(wgb_ref[0, 0], wub_ref[0, 0], wdb_ref[0, 0], gu_b, dn_b)
        ids_ref[1] = eb

    @pl.when(two & (eb == ea))
    def _():
        y_ref[...] = _pack_bf16_pair(_gated_mlp(x_ref[...], gu_a, dn_a))

    @pl.when((2 * i < nv) & jnp.logical_not(two & (eb == ea)))
    def _():
        y_ref[0:tm, :] = _pack_bf16_pair(_gated_mlp(x_ref[0:tm, :], gu_a, dn_a))

    @pl.when(two & (eb != ea))
    def _():
        y_ref[tm:, :] = _pack_bf16_pair(_gated_mlp(x_ref[tm:, :], gu_b, dn_b))


def _moe_experts(tile_expert, n_valid, xs, wg, wu, wd, layer, dep=None):
    p_rows, w = xs.shape
    tm = MOE_ROW_TILE
    _, _, d, f = wg.shape
    npair = p_rows // (2 * tm)
    pairs = tile_expert.reshape(npair, 2)
    tile_b = jnp.maximum(lax.cummax(jnp.where(pairs[:, 1] != pairs[:, 0], pairs[:, 1], -1)), 0)
    step = lambda i, nv: jnp.minimum(i, (nv[0] - 1) // 2)
    spec_a = lambda shp: pl.BlockSpec((1, 1) + shp, lambda i, te, tb, nv: (layer, te[2 * step(i, nv)], 0, 0))
    spec_b = lambda shp: pl.BlockSpec((1, 1) + shp, lambda i, te, tb, nv: (layer, tb[step(i, nv)], 0, 0))
    rows = pl.BlockSpec((2 * tm, w), lambda i, te, tb, nv: (step(i, nv), 0))
    kern, in_specs, args = _after(
        dep, _moe_expert_kernel,
        [pl.BlockSpec(memory_space=pl.ANY), spec_a((d, f)), spec_a((d, f)), spec_a((f, d)),
         spec_b((d, f)), spec_b((d, f)), spec_b((f, d))],
        [tile_expert, tile_b, n_valid, xs, wg, wu, wd, wg, wu, wd], n_lead=3)
    return pl.pallas_call(
        kern,
        out_shape=jax.ShapeDtypeStruct((p_rows, w), xs.dtype),
        grid_spec=pltpu.PrefetchScalarGridSpec(
            num_scalar_prefetch=3, grid=(npair,),
            in_specs=in_specs,
            out_specs=rows,
            scratch_shapes=[pltpu.VMEM((d, 2 * f), BF16), pltpu.VMEM((f, d), BF16),
                            pltpu.VMEM((d, 2 * f), BF16), pltpu.VMEM((f, d), BF16), pltpu.SMEM((2,), jnp.int32),
                            pltpu.VMEM((MOE_ROW_SLOTS, 2 * tm, w), xs.dtype),
                            pltpu.SemaphoreType.DMA((MOE_ROW_SLOTS,))]),
        compiler_params=pltpu.CompilerParams(dimension_semantics=("arbitrary",),
                                             vmem_limit_bytes=_vmem_limit(48)),
        name="moe_experts",
    )(*args)


def _moe_combine_kernel(yg_hbm, w_ref, xp_ref, sg_ref, su_ref, sd_ref, h_ref, mods_ref, gfin_ref, *rest, final_norm,
                        tile0):
    o_ref, wgu_ref, wdb_ref, ybuf, sems = rest[-5:]
    tl = ybuf.shape[2]
    nt = pl.num_programs(1)
    step = pl.program_id(0) * nt + pl.program_id(1)
    n_steps = pl.num_programs(0) * nt

    def yg_copy(s):
        rows = pl.ds(pl.multiple_of((s % nt + tile0) * tl, tl), tl)
        return pltpu.make_async_copy(yg_hbm.at[:, s // nt, rows, :], ybuf.at[s % MOE_ROW_SLOTS],
                                     sems.at[s % MOE_ROW_SLOTS])

    @pl.when(step == 0)
    def _():
        _cache_mlp_weights(sg_ref[0], su_ref[0], sd_ref[0], wgu_ref, wdb_ref)
        for ahead in range(MOE_ROW_SLOTS - 1):
            @pl.when(ahead < n_steps)
            def _():
                yg_copy(ahead).start()

    @pl.when(step + (MOE_ROW_SLOTS - 1) < n_steps)
    def _():
        yg_copy(step + (MOE_ROW_SLOTS - 1)).start()

    yg_copy(step).wait()
    yg_ref = ybuf.at[step % MOE_ROW_SLOTS]

    acc = _gated_mlp(xp_ref[0], wgu_ref, wdb_ref)
    half = acc.shape[1] // 2
    lo = acc[:, :half]
    hi = acc[:, half:]
    w = w_ref[0]
    for k in range(TOP_K):
        ylo, yhi = _unpack_bf16_pair(yg_ref[k])
        wk = w[:, k:k + 1]
        lo = lo + wk * ylo
        hi = hi + wk * yhi
    y = h_ref[0] + mods_ref[0, 0, N_MODS - 1:N_MODS, :] * jnp.concatenate([lo, hi], axis=1)
    if final_norm:
        y = _rms(y, gfin_ref[...])
    o_ref[0] = y


def _moe_combine(yg, wcols, xp, sg, su, sd, h, mods, gfin, nct, layer, out_buf, out_b0, out_batch, latent_only,
                 final_norm, dep=None):
    b, l, d = h.shape
    tl = TOKEN_TILE
    tile0 = nct if latent_only else 0
    tok = lambda w: pl.BlockSpec((1, tl, w), lambda i, j: (i, j + tile0, 0))
    lay = lambda x: pl.BlockSpec((1,) + x.shape[1:], lambda i, j: (layer,) + (0,) * (x.ndim - 1))
    args = [yg, wcols, xp, sg, su, sd, h, mods, gfin]
    in_specs = [pl.BlockSpec(memory_space=pl.ANY), tok(GATE_W), tok(d // 2),
                lay(sg), lay(su), lay(sd), tok(d),
                pl.BlockSpec((1, 1, N_MODS, d), lambda i, j: (i, jnp.where(j + tile0 < nct, 0, 1), 0, 0)),
                pl.BlockSpec(gfin.shape, lambda i, j: (0, 0))]
    _, in_specs, args = _after(dep, None, in_specs, args)
    aliases = {}
    if out_buf is not None:
        args.append(out_buf)
        in_specs.append(pl.BlockSpec(memory_space=pl.ANY))
        aliases = {len(args) - 1: 0}
    return pl.pallas_call(
        functools.partial(_moe_combine_kernel, final_norm=final_norm, tile0=tile0),
        out_shape=jax.ShapeDtypeStruct((out_batch, l - tile0 * tl, d), F32),
        grid=(b, l // tl - tile0),
        in_specs=in_specs,
        out_specs=pl.BlockSpec((1, tl, d), lambda i, j: (i + out_b0, j, 0)),
        scratch_shapes=[pltpu.VMEM((d, 2 * sg.shape[2]), BF16), pltpu.VMEM((sg.shape[2], d), BF16),
                        pltpu.VMEM((MOE_ROW_SLOTS, TOP_K, tl, d // 2), yg.dtype),
                        pltpu.SemaphoreType.DMA((MOE_ROW_SLOTS,))],
        input_output_aliases=aliases,
        compiler_params=pltpu.CompilerParams(dimension_semantics=("arbitrary", "arbitrary"),
                                             vmem_limit_bytes=_vmem_limit(40)),
        name="moe_combine",
    )(*args)


def _moe_route_rows(n2p, eid, rank, counts, b, l):
    d2 = n2p.shape[2]
    t = b * l
    tm = MOE_ROW_TILE
    n_tiles = 2 * -(-(TOP_K * t + N_EXPERTS * (tm - 1)) // (2 * tm))
    tiles_e = (counts.reshape(N_EXPERTS).astype(jnp.int32) + (tm - 1)) // tm
    tile_end = jnp.cumsum(tiles_e)
    off = (tile_end - tiles_e) * tm
    n_valid = tile_end[-1:]
    tile_id = jnp.minimum(jnp.arange(n_tiles, dtype=jnp.int32), n_valid - 1)
    tile_expert = jnp.sum((tile_end[None, :] <= tile_id[:, None]).astype(jnp.int32), axis=1)
    dest = _moe_dest(off, eid, rank).transpose(1, 0, 2).reshape(8, t)
    xs = _sc_dispatch(n2p.reshape(t, d2), dest, n_tiles * tm)
    return xs, dest, tile_expert, n_valid


def _rwkv_proj_kernel(h_ref, hp_ref, hx_ref, mods_ref, g_ref, mu_ref, wr_ref, wk_ref, wv_ref, g1_ref, g2_ref,
                      w1_ref, w2_ref, a1_ref, a2_ref, w0_ref, a0_ref, kk_ref, ka_ref, rk_ref, bd_ref,
                      r_out, v_out, kk_out, g_out, km_out, b_out, lw_out, bonus_out, *, nct):
    j = pl.program_id(1)
    nt = pl.num_programs(1)
    m = mods_ref[0, 0]
    g = g_ref[...]
    n = _norm_mod(h_ref[0], g, m[0:1], m[1:2])
    tl, d = n.shape
    seg_first = (j == 0) | (j == nct)
    seg_last = (j == nct - 1) | (j == nt - 1)
    n_prev = _norm_mod(hp_ref[0], g, m[0:1], m[1:2])[7:8] * jnp.where(seg_first, 0.0, 1.0)
    n_next = _norm_mod(hx_ref[0], g, m[0:1], m[1:2])[0:1] * jnp.where(seg_last, 0.0, 1.0)
    row = lax.broadcasted_iota(jnp.int32, (tl, 1), 0)
    prev = jnp.where(row == 0, n_prev, pltpu.roll(n, 1, axis=0))
    nxt = jnp.where(row == tl - 1, n_next, pltpu.roll(n, tl - 1, axis=0))
    lane = lax.broadcasted_iota(jnp.int32, (1, d), 1)
    xx = jnp.where(lane < d // 2, prev, nxt) - n
    mu = mu_ref[...]
    bd = bd_ref[...]
    halves = [slice(0, tl // 2), slice(tl // 2, tl)]
    first = []
    for rs in halves:
        nh, xh = n[rs], xx[rs]
        xr, xw, xk, xv, xa, xg = [nh + xh * mu[i:i + 1] for i in range(6)]
        first.append((_dot(xr, wr_ref[...]), _dot(xk, wk_ref[...]), _dot(xv, wv_ref[...]),
                      _dot(xg, g1_ref[...]), _dot(xw, w1_ref[...]), _dot(xa, a1_ref[...])))
    second = []
    for r, k, v, gq, tq, ta in first:
        tw = jnp.tanh(tq)
        kk = k * kk_ref[...]
        second.append((_dot(_sigmoid(gq), g2_ref[...]), [_dot(tw, w2_ref[dr]) for dr in range(2)],
                       [_dot(ta, a2_ref[dr]) for dr in range(2)], kk, _head_sum(kk * kk, bd)))
    for rs, (r, k, v, _, _, _), (gate, zw, za, kk, kk_sq) in zip(halves, first, second):
        kk = kk / jnp.maximum(jnp.sqrt(kk_sq), 1e-12)
        g_out[0, rs, :] = gate.astype(g_out.dtype)
        r_out[0, rs, :] = r.astype(r_out.dtype)
        v_out[0, rs, :] = v.astype(v_out.dtype)
        kk_out[0, rs, :] = kk.astype(kk_out.dtype)
        bonus = jnp.zeros_like(v)
        for dr in range(2):
            lw_out[dr, 0, rs, :] = -jnp.exp(-0.5) * _sigmoid(w0_ref[dr:dr + 1, :] + zw[dr])
            a = _sigmoid(a0_ref[dr:dr + 1, :] + za[dr])
            km = k * (1.0 + (a - 1.0) * ka_ref[...])
            km_out[dr, 0, rs, :] = km.astype(km_out.dtype)
            b_out[dr, 0, rs, :] = (kk * a).astype(b_out.dtype)
            bonus = bonus + _head_sum(r * km * rk_ref[...], bd) * v
        bonus_out[0, rs, :] = bonus.astype(bonus_out.dtype)


def _rwkv_proj(h, mods, g, mu, wr, wk, wv, g1, g2, w1, w2, a1, a2, w0, a0, kk, ka, rk, bd, nct, dep=None):
    b, l, d = h.shape
    tl = TOKEN_TILE
    nb8 = l // 8
    tok = pl.BlockSpec((1, tl, d), lambda i, j: (i, j, 0))
    tok2 = pl.BlockSpec((2, 1, tl, d), lambda i, j: (0, i, j, 0))
    full = lambda x: pl.BlockSpec(x.shape, lambda i, j: (0,) * x.ndim)
    sds = jax.ShapeDtypeStruct
    kern, in_specs, args = _after(
        dep, functools.partial(_rwkv_proj_kernel, nct=nct),
        [tok,
         pl.BlockSpec((1, 8, d), lambda i, j: (i, jnp.maximum(j * (tl // 8) - 1, 0), 0)),
         pl.BlockSpec((1, 8, d), lambda i, j: (i, jnp.minimum((j + 1) * (tl // 8), nb8 - 1), 0)),
         pl.BlockSpec((1, 1, N_MODS, d), lambda i, j: (i, jnp.where(j < nct, 0, 1), 0, 0)),
         full(g), full(mu), full(wr), full(wk), full(wv), full(g1), full(g2), full(w1), full(w2),
         full(a1), full(a2), full(w0), full(a0), full(kk), full(ka), full(rk), full(bd)],
        [h, h, h, mods, g, mu, wr, wk, wv, g1, g2, w1, w2, a1, a2, w0, a0, kk, ka, rk, bd])
    return pl.pallas_call(
        kern,
        out_shape=[sds((b, l, d), BF16), sds((b, l, d), BF16), sds((b, l, d), BF16), sds((b, l, d), BF16),
                   sds((2, b, l, d), BF16), sds((2, b, l, d), BF16), sds((2, b, l, d), F32), sds((b, l, d), BF16)],
        grid=(b, l // tl),
        in_specs=in_specs,
        out_specs=[tok, tok, tok, tok, tok2, tok2, tok2, tok],
        compiler_params=pltpu.CompilerParams(dimension_semantics=("parallel", "parallel"),
                                             vmem_limit_bytes=_vmem_limit(56)),
        name="rwkv_proj",
    )(*args)


def _wkv_kernel(r_ref, v_ref, kk_ref, km_ref, b_ref, lw_ref, y_ref, st_ref):
    c = WKV_CHUNK
    w = WKV_PAIR
    rev = pl.program_id(0)
    sign = 1 - 2 * rev

    @pl.when(pl.program_id(2) == 0)
    def _():
        st_ref[...] = jnp.zeros_like(st_ref)

    ti = lax.broadcasted_iota(jnp.int32, (c, c), 0)
    si = lax.broadcasted_iota(jnp.int32, (c, c), 1)
    tri = jnp.where((si - ti) * sign <= 0, 1.0, 0.0).astype(BF16)
    tri3 = jnp.concatenate([tri, tri, tri], axis=1)
    nsub = WKV_CHUNKS_PER_STEP
    subs = [pl.ds(pl.multiple_of(jnp.where(rev == 0, s, nsub - 1 - s) * c, c), c) for s in range(nsub)]
    rt, kt, kh, bh, v32, e_mid = [], [], [], [], [], []
    for rows in subs:
        lw = lw_ref[0, 0, rows, :]
        t1 = lw.astype(BF16)
        d1 = lw - t1.astype(F32)
        t2 = d1.astype(BF16)
        t3 = (d1 - t2.astype(F32)).astype(BF16)
        l_incl = jnp.dot(tri3, jnp.concatenate([t1, t2, t3], axis=0), preferred_element_type=F32)
        mid = 0.5 * jnp.sum(lw, axis=0, keepdims=True)
        e_neg = jnp.exp(mid - l_incl)
        e_mid.append(jnp.exp(mid))
        rt.append(r_ref[0, rows, :].astype(F32) * jnp.exp(l_incl - mid))
        kt.append(kk_ref[0, rows, :].astype(F32) * jnp.exp(l_incl - lw - mid))
        kh.append(km_ref[0, 0, rows, :].astype(F32) * e_neg)
        bh.append(b_ref[0, 0, rows, :].astype(F32) * e_neg)
        v32.append(v_ref[0, rows, :].astype(F32))

    ri = lax.broadcasted_iota(jnp.int32, (w, w), 0)
    ci = lax.broadcasted_iota(jnp.int32, (w, w), 1)
    same = (ri // c) == (ci // c)
    eye = jnp.where(ri == ci, 1.0, 0.0).astype(F32)
    tl_ = lax.broadcasted_iota(jnp.int32, (c, w), 0)
    jl_ = lax.broadcasted_iota(jnp.int32, (c, w), 1) % c
    strict = (jl_ - tl_) * sign < 0
    incl = (jl_ - tl_) * sign <= 0
    eye2 = jnp.where(jl_ == tl_, 1.0, 0.0).astype(F32)
    lane = lax.broadcasted_iota(jnp.int32, (1, w), 1)
    h0 = lane < RWKV_HEAD

    def rows2(x):
        return jnp.concatenate([jnp.where(h0, x, 0.0), jnp.where(h0, 0.0, x)], axis=0)

    npair = st_ref.shape[0]
    items = [(s, slice(p * w, (p + 1) * w)) for s in range(nsub) for p in range(npair)]
    n = range(len(items))
    em = [e_mid[s][:, sl] for s, sl in items]
    g = [_dot_nt(jnp.concatenate([kt[s][:, sl], rt[s][:, sl]], axis=0),
                 jnp.concatenate([rows2(kh[s][:, sl]), rows2(bh[s][:, sl])], axis=0)) for s, sl in items]
    a_kk = [jnp.where(strict, x[:c, :w], 0.0) for x in g]
    a_rk = [jnp.where(incl, x[c:, :w], 0.0) for x in g]
    a_rb = [jnp.where(incl, x[c:, w:], 0.0) for x in g]
    vi = [v32[s][:, sl] for s, sl in items]
    v_rows = [rows2(x) for x in vi]
    av = [_dot(jnp.concatenate([a_kk[i], a_rk[i]], axis=0), v_rows[i]) for i in n]
    r_pre = [x[:c] for x in av]
    ark_v = [x[c:] for x in av]
    m = [jnp.where(strict, -x[:c, w:], 0.0) for x in g]
    tinv = [eye2 + x for x in m]
    m = [_dot(x, rows2(x)) for x in m]
    for _ in range(c.bit_length() - 3):
        both = [_dot(jnp.concatenate([tinv[i], m[i]], axis=0), rows2(m[i])) for i in n]
        tinv = [tinv[i] + both[i][:c] for i in n]
        m = [x[c:] for x in both]
    tinv = [tinv[i] + _dot(tinv[i], rows2(m[i])) for i in n]
    sol = [_dot(tinv[i], jnp.concatenate([rows2(r_pre[i]), rows2(kt[s][:, sl] * em[i])], axis=1))
           for i, (s, sl) in enumerate(items)]
    u_pre = [x[:, :w] for x in sol]
    kq = [x[:, w:] for x in sol]
    arb = [_dot(a_rb[i], jnp.concatenate([rows2(u_pre[i]), rows2(kq[i])], axis=1)) for i in n]
    y_pre = [ark_v[i] - arb[i][:, :w] for i in n]
    r_eff = [rt[s][:, sl] * em[i] - arb[i][:, w:] for i, (s, sl) in enumerate(items)]
    bbar = [bh[s][:, sl] * em[i] for i, (s, sl) in enumerate(items)]
    kbar = [kh[s][:, sl] * em[i] for i, (s, sl) in enumerate(items)]
    mmat = [eye * (em[i] * em[i]) - jnp.where(same, _dot_tn(kq[i], bbar[i]), 0.0) for i in n]
    s_pre = [jnp.where(same, _dot_tn(jnp.concatenate([vi[i], -u_pre[i]], axis=0),
                                     jnp.concatenate([kbar[i], bbar[i]], axis=0)), 0.0) for i in n]
    st = [st_ref[p] for p in range(npair)]
    for i, (s, sl) in enumerate(items):
        p = i % npair
        y_ref[0, 0, subs[s], sl] = (_dot_nt(r_eff[i], st[p]) + y_pre[i]).astype(y_ref.dtype)
        hi = st[p].astype(BF16)
        lo = (st[p] - hi.astype(F32)).astype(BF16)
        mb = mmat[i].astype(BF16)
        both = jnp.dot(jnp.concatenate([hi, lo], axis=0), mb, preferred_element_type=F32)
        st[p] = both[:w] + both[w:] + s_pre[i]
    for p in range(npair):
        st_ref[p] = st[p]


def _wkv(r, v, kk, km, bv, lw, lc, dep=None):
    b, l, d = r.shape
    c = WKV_CHUNK * WKV_CHUNKS_PER_STEP
    ncc = lc // c
    nlc = (l - lc) // c

    def chunk(dr, i):
        return jnp.where(dr == 0, i, jnp.where(i < ncc, ncc - 1 - i, nlc + 2 * ncc - 1 - i))

    shared = pl.BlockSpec((1, c, d), lambda dr, bi, i: (bi, chunk(dr, i), 0))
    per_dir = pl.BlockSpec((1, 1, c, d), lambda dr, bi, i: (dr, bi, chunk(dr, i), 0))
    kern, in_specs, args = _after(dep, _wkv_kernel, [shared, shared, shared, per_dir, per_dir, per_dir],
                                  [r, v, kk, km, bv, lw])
    return pl.pallas_call(
        kern,
        out_shape=jax.ShapeDtypeStruct((2, b, l, d), BF16),
        grid=(2, b, l // c),
        in_specs=in_specs,
        out_specs=per_dir,
        scratch_shapes=[pltpu.VMEM((d // WKV_PAIR, WKV_PAIR, WKV_PAIR), F32)],
        compiler_params=pltpu.CompilerParams(dimension_semantics=("parallel", "parallel", "arbitrary"),
                                             vmem_limit_bytes=_vmem_limit(32)),
        name="wkv7_chunked",
    )(*args)


def _rwkv_out_kernel(y_ref, bonus_ref, g_ref, lnw_ref, lnb_ref, wo_ref, bd_ref, h_ref, mods_ref, gffn_ref,
                     wrt_ref, bias_ref, hn_ref, n2_ref, eid_ref, rank_ref, w_ref, cnt_ref, run_ref):
    y = y_ref[0, 0].astype(F32) + y_ref[1, 0].astype(F32)
    bd = bd_ref[...]
    mean = _head_sum(y, bd) * (1.0 / RWKV_HEAD)
    yc = y - mean
    var = _head_sum(yc * yc, bd) * (1.0 / RWKV_HEAD)
    yn = yc * lax.rsqrt(var + GN_EPS) * lnw_ref[...] + lnb_ref[...]
    out = (yn + bonus_ref[0].astype(F32)) * g_ref[0].astype(F32)
    _mixer_tail(_dot(out, wo_ref[...]), h_ref[0], mods_ref[0, 0], gffn_ref, wrt_ref, bias_ref, hn_ref, n2_ref,
                eid_ref, rank_ref, w_ref, cnt_ref, run_ref)


def _rwkv_out(y, bonus, g, lnw, lnb, wo, bd, h, mods, gffn, wrt, bias, nct, dep=None):
    b, l, d = h.shape
    tl = TOKEN_TILE
    tok = lambda w: pl.BlockSpec((1, tl, w), lambda i, j: (i, j, 0))
    full = lambda x: pl.BlockSpec(x.shape, lambda i, j: (0,) * x.ndim)
    shapes, specs = _tail_outs(b, l, d)
    kern, in_specs, args = _after(
        dep, _rwkv_out_kernel,
        [pl.BlockSpec((2, 1, tl, d), lambda i, j: (0, i, j, 0)), tok(d), tok(d),
         full(lnw), full(lnb), full(wo), full(bd), tok(d),
         pl.BlockSpec((1, 1, N_MODS, d), lambda i, j: (i, jnp.where(j < nct, 0, 1), 0, 0)),
         full(gffn), full(wrt), full(bias)],
        [y, bonus, g, lnw, lnb, wo, bd, h, mods, gffn, wrt, bias])
    return pl.pallas_call(
        kern,
        out_shape=shapes,
        grid=(b, l // tl),
        in_specs=in_specs,
        out_specs=specs,
        scratch_shapes=[pltpu.VMEM((N_EXPERTS, 1), F32)],
        compiler_params=pltpu.CompilerParams(dimension_semantics=("arbitrary", "arbitrary"),
                                             vmem_limit_bytes=_vmem_limit(40)),
        name="rwkv_out",
    )(*args)


def _rope_table(n_lat, n_ctx):
    dim = SWA_HEAD_DIM
    nf = dim // 4
    inv = ROPE_THETA ** (-jnp.arange(nf, dtype=F32) / nf)
    row = jnp.repeat(jnp.arange(n_lat // GRID_W, dtype=F32), GRID_W)
    col = jnp.tile(jnp.arange(GRID_W, dtype=F32), n_lat // GRID_W)
    ar = row[:, None] * inv
    ac = col[:, None] * inv
    ang = jnp.concatenate([ar, ar, ac, ac], axis=-1)
    cos = jnp.concatenate([jnp.ones((n_ctx, dim), F32), jnp.cos(ang)], axis=0)
    sin = jnp.concatenate([jnp.zeros((n_ctx, dim), F32), jnp.sin(ang)], axis=0)
    return jnp.tile(cos, (1, 2)), jnp.tile(sin, (1, 2))


def _layout_attn_weights(w_in, w_uq, w_ukv):
    d = w_in.shape[0]
    s0 = MLA_Q_RANK
    s1 = s0 + MLA_KV_RANK
    s2 = s1 + MLA_ROPE
    s3 = s2 + SWA_HEADS * SWA_HEAD_DIM
    s4 = s3 + SWA_KV_HEADS * SWA_HEAD_DIM
    rep = lambda w: jnp.concatenate(
        [jnp.tile(w[:, g * SWA_HEAD_DIM:(g + 1) * SWA_HEAD_DIM], (1, V7X_LANES // SWA_HEAD_DIM))
         for g in range(SWA_KV_HEADS)], axis=1)
    win = jnp.concatenate([w_in[:, :s1], w_in[:, s2:s3], rep(w_in[:, s3:s4]), rep(w_in[:, s4:]),
                           w_in[:, s1:s2], jnp.zeros((d, V7X_LANES - MLA_ROPE), w_in.dtype)], axis=1)
    qh = MLA_NOPE + MLA_ROPE
    pad = jnp.zeros((w_uq.shape[0], V7X_MXU_DIM - qh), w_uq.dtype)
    wuq = jnp.concatenate([jnp.concatenate([w_uq[:, h * qh:(h + 1) * qh], pad], axis=1) for h in range(MLA_HEADS)], axis=1)
    kvh = MLA_NOPE + MLA_V
    wuk = jnp.concatenate([w_ukv[:, h * kvh:h * kvh + MLA_NOPE] for h in range(MLA_HEADS)], axis=1)
    wuvt = jnp.concatenate([w_ukv[:, h * kvh + MLA_NOPE:(h + 1) * kvh] for h in range(MLA_HEADS)], axis=1).T
    return win.astype(BF16), wuq.astype(BF16), wuk.astype(BF16), wuvt.astype(BF16)


def _lora_pair(w_down, w_up):
    rank = w_down.shape[2]
    down = jnp.concatenate([w_down[0], w_down[1]], axis=1)
    z = jnp.zeros((rank, w_up.shape[2]), w_up.dtype)
    up = jnp.stack([jnp.concatenate([w_up[0], z], axis=0), jnp.concatenate([z, w_up[1]], axis=0)], axis=0)
    return down.astype(BF16), up.astype(BF16)


def _head_block_diag():
    i = jnp.arange(V7X_MXU_DIM) // RWKV_HEAD
    return (i[:, None] == i[None, :]).astype(BF16)


def kernel(x, c, ctx, c_ctx, ada_w, ada_b, norm_mix, norm_ffn, norm_final, attn_w_in, attn_q_norm, attn_kv_norm, attn_w_uq, attn_w_ukv, attn_sinks, attn_w_o, rwkv_mu, rwkv_w_r, rwkv_w_k, rwkv_w_v, rwkv_w_o, rwkv_g1, rwkv_g2, rwkv_w0, rwkv_w1, rwkv_w2, rwkv_a0, rwkv_a1, rwkv_a2, rwkv_k_k, rwkv_k_a, rwkv_r_k, rwkv_ln_w, rwkv_ln_b, moe_router, moe_bias, moe_w_gate, moe_w_up, moe_w_down, moe_ws_gate, moe_ws_up, moe_ws_down):
    bsz, s, d = x.shape
    lc = ctx.shape[1]
    l = lc + s
    depth = ada_w.shape[0]
    nct = lc // TOKEN_TILE
    assert lc % TOKEN_TILE == 0 and s % TOKEN_TILE == 0 and s >= SWA_BAND and lc % SWA_Q_TILE == 0
    assert lc % (WKV_CHUNK * WKV_CHUNKS_PER_STEP) == 0
    assert d % V7X_MXU_DIM == 0 and WKV_CHUNK * 2 == V7X_LANES
    ngrp = SAMPLE_GROUPS
    bg = bsz // ngrp
    assert bsz % ngrp == 0 and (bg * l) % (8 * V7X_SC_WORKERS) == 0

    assert ngrp == 2
    cos, sin = _rope_table(s, lc)
    bd = _head_block_diag()
    rows = -(-(bsz + 1) // 8) * 8
    cc = jnp.concatenate([c, c_ctx[None, :], jnp.zeros((rows - bsz - 1, d), F32)], axis=0)
    row2 = lambda a: a.reshape(1, -1)
    moe_w = (moe_w_gate, moe_w_up, moe_w_down)
    moe_ws = (moe_ws_gate, moe_ws_up, moe_ws_down)

    shared = {}

    def layer_weights(li):
        if li not in shared:
            i = li // 2
            ada = _ada_mods(cc, ada_w, ada_b, li)
            w = dict(
                mods=jnp.stack([jnp.broadcast_to(ada[bsz].reshape(1, N_MODS, d), (bsz, N_MODS, d)),
                                ada[:bsz].reshape(bsz, N_MODS, d)], axis=1),
                wrt=moe_router[li].T,
                bias=moe_bias[li].reshape(N_GROUPS, GROUP_SIZE, 1))
            if li % 2 == 0:
                w["win"], w["wuq"], w["wuk"], w["wuvt"] = _layout_attn_weights(attn_w_in[i], attn_w_uq[i], attn_w_ukv[i])
                w["wo"] = attn_w_o[i].astype(BF16)
            else:
                w["w1"], w["w2"] = _lora_pair(rwkv_w1[i], rwkv_w2[i])
                w["a1"], w["a2"] = _lora_pair(rwkv_a1[i], rwkv_a2[i])
                w["wr"], w["wk"], w["wv"], w["wo"] = [x[i].astype(BF16) for x in (rwkv_w_r, rwkv_w_k, rwkv_w_v, rwkv_w_o)]
                w["g1"], w["g2"] = rwkv_g1[i].astype(BF16), rwkv_g2[i].astype(BF16)
            shared[li] = w
        return shared[li]

    groups = [dict(stream=(ctx, x, g * bg, 0), b0=g * bg) for g in range(ngrp)]
    result = [None]

    def run_stage(st, li, name, dep):
        w = layer_weights(li)
        i = li // 2
        with_ctx = li < depth - 1
        mods = w["mods"][st["b0"]:st["b0"] + bg]
        if name == "proj" and li % 2 == 0:
            st["qkv"] = _attn_proj(st["stream"], bg, l, mods, row2(norm_mix[li]), w["win"], row2(attn_q_norm[i]),
                                   row2(attn_kv_norm[i]), w["wuq"], w["wuk"], w["wuvt"], cos, sin, nct, dep=dep)
            return st["qkv"][0]
        if name == "mid" and li % 2 == 0:
            q, k, vt, qs, ks, vs = st.pop("qkv")
            st["a"] = _mla_attention(q, k, vt, lc, 0 if with_ctx else lc // MLA_Q_TILE, dep=dep)
            st["bm"] = _swa_attention(attn_sinks[i], qs, ks, vs, lc, 0 if with_ctx else lc // SWA_Q_TILE, dep=st["a"])
            return st["bm"]
        if name == "proj":
            assert st["stream"][0] is st["stream"][1]
            st["feat"] = _rwkv_proj(st["stream"][0], mods, row2(norm_mix[li]), rwkv_mu[i], w["wr"], w["wk"], w["wv"],
                                    w["g1"], w["g2"], w["w1"], w["w2"], w["a1"], w["a2"], rwkv_w0[i], rwkv_a0[i],
                                    row2(rwkv_k_k[i]), row2(rwkv_k_a[i]), row2(rwkv_r_k[i]), bd, nct, dep=dep)
            return st["feat"][0]
        if name == "mid":
            r, v, kk, gt, km, bv, lw, bonus = st.pop("feat")
            st["y"] = _wkv(r, v, kk, km, bv, lw, lc, dep=dep)
            st["gate"], st["bonus"] = gt, bonus
            return st["y"]
        if name == "out":
            if li % 2 == 0:
                tail = _attn_out(st.pop("a"), st.pop("bm"), st["stream"], mods, w["wo"], row2(norm_ffn[li]),
                                 w["wrt"], w["bias"], nct, dep=dep)
            else:
                tail = _rwkv_out(st.pop("y"), st.pop("bonus"), st.pop("gate"), row2(rwkv_ln_w[i]), row2(rwkv_ln_b[i]),
                                 w["wo"], bd, st["stream"][0], mods, row2(norm_ffn[li]), w["wrt"], w["bias"], nct, dep=dep)
            st["h"], st["n2p"], eid, rank, st["wcols"], counts = tail
            st["xs"], st["dest"], st["tile_expert"], st["n_valid"] = _moe_route_rows(st["n2p"], eid, rank, counts, bg, l)
            return st["h"]
        if name == "experts":
            ys = _moe_experts(st.pop("tile_expert"), st.pop("n_valid"), st.pop("xs"), *moe_w, li, dep=dep)
            st["yg"] = _sc_gather(ys, st.pop("dest"), bg * l).reshape(TOP_K, bg, l, d // 2)
            return ys
        assert name == "combine"
        last = li == depth - 1
        h = _moe_combine(st.pop("yg"), st.pop("wcols"), st.pop("n2p"), *moe_ws, st.pop("h"), mods, row2(norm_final),
                         nct, li, result[0] if last else None, st["b0"] if last else 0, bsz if last else bg,
                         last, last, dep=dep)
        if last:
            result[0] = h
        else:
            st["stream"] = (h, h, 0, nct)
        return h

    order = [(0, 0, "proj"), (0, 0, "mid")]
    for li in range(depth):
        order += [(0, li, "out"), (1, li, "proj"), (0, li, "experts"), (1, li, "mid")]
        if li < depth - 1:
            order += [(0, li, "combine"), (1, li, "out"), (0, li + 1, "proj"), (1, li, "experts"),
                      (0, li + 1, "mid"), (1, li, "combine")]
        else:
            order += [(1, li, "out"), (0, li, "combine"), (1, li, "experts"), (1, li, "combine")]
    dep = None
    for g, li, name in order:
        dep = run_stage(groups[g], li, name, dep)
    return result[0]
```

```python
import functools

import jax
import jax.numpy as jnp
from jax import lax
from jax.experimental import pallas as pl
from jax.experimental.pallas import tpu as pltpu
from jax.experimental.pallas import tpu_sc as plsc

F32 = jnp.float32
BF16 = jnp.bfloat16
HIGHEST = lax.Precision.HIGHEST

GRID_W = 64
NORM_EPS = 1e-6
ROPE_THETA = 10000.0
NEG_INF = -1e30
N_MODS = 6

MLA_HEADS = 4
MLA_Q_RANK = 384
MLA_KV_RANK = 256
MLA_NOPE = 128
MLA_ROPE = 64
MLA_V = 128

SWA_HEADS = 8
SWA_KV_HEADS = 2
SWA_GROUP = SWA_HEADS // SWA_KV_HEADS
SWA_HEAD_DIM = 64
WINDOW = 128

RWKV_HEAD = 64
DECAY_LORA = 64
ICLR_LORA = 64
GATE_LORA = 128
GN_EPS = 64e-5

N_EXPERTS = 64
TOP_K = 6
N_GROUPS = 8
TOPK_GROUPS = 4
GROUP_SIZE = N_EXPERTS // N_GROUPS
ROUTED_SCALE = 2.5
GATE_W = 128

V7X_LANES = 128
V7X_MXU_DIM = 256
V7X_VMEM_BYTES = 64 * 1024 * 1024
V7X_SC_CORES = 2
V7X_SC_SUBCORES = 16
V7X_SC_WORKERS = V7X_SC_CORES * V7X_SC_SUBCORES

TOKEN_TILE = 256
MLA_Q_TILE = 256
MLA_HEADS_PER_STEP = 2
SWA_Q_TILE = 256
SWA_BAND = SWA_Q_TILE + 2 * WINDOW
WKV_CHUNK = 64
WKV_PAIR = 2 * RWKV_HEAD
WKV_CHUNKS_PER_STEP = 4
MOE_ROW_TILE = 512
MOE_ROW_SLOTS = 3
SAMPLE_GROUPS = 2
LAST_COMBINE_PARTS = 2
SC_MAX_CHUNK = 64

LOG2E = 1.4426950408889634
MIB = 1024 * 1024
VMEM_RESERVE_BYTES = 4 * MIB


def _vmem_limit(mib):
    return min(mib * MIB, V7X_VMEM_BYTES - VMEM_RESERVE_BYTES)


def _dot(a, b):
    return jnp.dot(a.astype(BF16), b.astype(BF16), preferred_element_type=F32)


def _dot_nt(a, b):
    return lax.dot_general(a.astype(BF16), b.astype(BF16), (((1,), (1,)), ((), ())),
                           preferred_element_type=F32)


def _dot_tn(a, b):
    return lax.dot_general(a.astype(BF16), b.astype(BF16), (((0,), (0,)), ((), ())),
                           preferred_element_type=F32)


def _sigmoid(x):
    return 1.0 / (1.0 + jnp.exp(-x))


def _silu(x):
    return x * _sigmoid(x)


def _rms(x, g):
    return x * lax.rsqrt(jnp.mean(x * x, axis=-1, keepdims=True) + NORM_EPS) * g


def _norm_mod(x, g, shift, scale):
    return _rms(x, g) * (1.0 + scale) + shift


def _split_dot(x, w):
    hi = x.astype(BF16)
    lo = (x - hi.astype(F32)).astype(BF16)
    return (jnp.dot(hi, w, preferred_element_type=F32) + jnp.dot(lo, w, preferred_element_type=F32))


def _head_sum(x, bd):
    w = bd.shape[0]
    parts = [_split_dot(x[:, c * w:(c + 1) * w], bd) for c in range(x.shape[1] // w)]
    return jnp.concatenate(parts, axis=1)


def _after(dep, kernel, in_specs, args, n_lead=0):
    if dep is None:
        return kernel, list(in_specs), list(args)
    n_in = n_lead + len(in_specs)

    def ordered(*refs):
        return kernel(*refs[:n_in], *refs[n_in + 1:])

    return ordered, list(in_specs) + [pl.BlockSpec(memory_space=pl.ANY)], list(args) + [dep]


def _ada_kernel(c_ref, w_ref, b_ref, o_ref):
    s = _silu(c_ref[...])
    o_ref[...] = jnp.dot(s, w_ref[0], precision=HIGHEST, preferred_element_type=F32) + b_ref[0]


def _ada_mods(cc, w, b, layer):
    rows, d = cc.shape
    depth, _, n = w.shape
    return pl.pallas_call(
        _ada_kernel,
        out_shape=jax.ShapeDtypeStruct((rows, n), F32),
        grid=(n // d,),
        in_specs=[pl.BlockSpec((rows, d), lambda i: (0, 0)),
                  pl.BlockSpec((1, d, d), lambda i: (layer, 0, i)),
                  pl.BlockSpec((1, 1, d), lambda i: (layer, 0, i))],
        out_specs=pl.BlockSpec((rows, d), lambda i: (0, i)),
        compiler_params=pltpu.CompilerParams(dimension_semantics=("parallel",),
                                             vmem_limit_bytes=_vmem_limit(32)),
        name="ada_mods",
    )(cc, w, b.reshape(depth, 1, n))


def _rope128(x, cos, sin, first_half):
    rot = jnp.where(first_half, -pltpu.roll(x, V7X_LANES - 16, axis=1), pltpu.roll(x, 16, axis=1))
    return x * cos + rot * sin


_C_CQ = 0
_C_CKV = _C_CQ + MLA_Q_RANK
_C_QS = _C_CKV + MLA_KV_RANK
_C_KS = _C_QS + SWA_HEADS * SWA_HEAD_DIM
_C_VS = _C_KS + SWA_KV_HEADS * V7X_LANES
_C_KR = _C_VS + SWA_KV_HEADS * V7X_LANES
_C_END = _C_KR + V7X_LANES
_SWA_W = SWA_KV_HEADS * V7X_MXU_DIM
_MLA_QK_W = MLA_HEADS * V7X_MXU_DIM


def _stream_specs(stream, nct, tl):
    ctx_arr, lat_arr, b0, lat_off = stream
    d = ctx_arr.shape[2]
    return [pl.BlockSpec((1, tl, d), lambda i, j: (i + b0, jnp.minimum(j, nct - 1), 0)),
            pl.BlockSpec((1, tl, d), lambda i, j: (i + b0, jnp.maximum(j - nct, 0) + lat_off, 0))]


def _stream_tile(c_ref, x_ref, nct):
    rows = c_ref.shape[1]
    take_ctx = lax.broadcasted_iota(jnp.int32, (rows, 1), 0) < jnp.where(pl.program_id(1) < nct, rows, 0)
    return jnp.where(take_ctx, c_ref[0], x_ref[0])


def _attn_proj_kernel(c_ref, x_ref, mods_ref, g_ref, win_ref, qn_ref, kvn_ref, wuq_ref, wuk_ref, wuvt_ref, cos_ref,
                      sin_ref, q_ref, k_ref, vt_ref, qs_ref, ks_ref, vs_ref, *, nct):
    m = mods_ref[0, 0]
    n = _norm_mod(_stream_tile(c_ref, x_ref, nct), g_ref[...], m[0:1], m[1:2])
    u = _dot(n, win_ref[...])
    cos = cos_ref[...]
    sin = sin_ref[...]
    lane = lax.broadcasted_iota(jnp.int32, (1, V7X_LANES), 1)
    first_half = (lane % 32) < 16

    def rope(x):
        return _rope128(x, cos, sin, first_half)

    scale_a = (MLA_NOPE + MLA_ROPE) ** -0.5 * LOG2E
    scale_b = SWA_HEAD_DIM ** -0.5 * LOG2E
    q = _dot(_rms(u[:, _C_CQ:_C_CKV], qn_ref[...]), wuq_ref[...])
    ckv = _rms(u[:, _C_CKV:_C_QS], kvn_ref[...])
    kn = _dot(ckv, wuk_ref[...])
    vt_ref[0] = _dot_nt(wuvt_ref[...], ckv).astype(BF16)
    kr = rope(u[:, _C_KR:_C_END]).astype(BF16)
    for h in range(MLA_HEADS):
        o = h * V7X_MXU_DIM
        q_ref[0, :, o:o + V7X_LANES] = (q[:, o:o + V7X_LANES] * scale_a).astype(BF16)
        q_ref[0, :, o + V7X_LANES:o + V7X_MXU_DIM] = (rope(q[:, o + V7X_LANES:o + V7X_MXU_DIM]) * scale_a).astype(BF16)
        k_ref[0, :, o:o + V7X_LANES] = kn[:, h * MLA_NOPE:(h + 1) * MLA_NOPE].astype(BF16)
        k_ref[0, :, o + V7X_LANES:o + V7X_MXU_DIM] = kr
    for c in range((_C_KS - _C_QS) // V7X_LANES):
        o = c * V7X_LANES
        qs_ref[0, :, o:o + V7X_LANES] = (rope(u[:, _C_QS + o:_C_QS + o + V7X_LANES]) * scale_b).astype(BF16)
    for g in range(SWA_KV_HEADS):
        o = g * V7X_LANES
        ks = rope(u[:, _C_KS + o:_C_KS + o + V7X_LANES]).astype(BF16)
        vs = u[:, _C_VS + o:_C_VS + o + V7X_LANES].astype(BF16)
        for c in range(V7X_MXU_DIM // V7X_LANES):
            oo = g * V7X_MXU_DIM + c * V7X_LANES
            ks_ref[0, :, oo:oo + V7X_LANES] = ks
            vs_ref[0, :, oo:oo + V7X_LANES] = vs


def _attn_proj(stream, b, l, mods, g, win, qn, kvn, wuq, wuk, wuvt, cos, sin, nct, dep=None):
    d = stream[0].shape[2]
    tl = TOKEN_TILE
    tok = lambda w: pl.BlockSpec((1, tl, w), lambda i, j: (i, j, 0))
    full = lambda a: pl.BlockSpec(a.shape, lambda i, j: (0,) * a.ndim)
    sds = jax.ShapeDtypeStruct
    dv = MLA_HEADS * MLA_V
    kern, in_specs, args = _after(
        dep, functools.partial(_attn_proj_kernel, nct=nct),
        _stream_specs(stream, nct, tl) + [
            pl.BlockSpec((1, 1, N_MODS, d), lambda i, j: (i, jnp.where(j < nct, 0, 1), 0, 0)),
            full(g), full(win), full(qn), full(kvn), full(wuq), full(wuk), full(wuvt),
            pl.BlockSpec((tl, V7X_LANES), lambda i, j: (j, 0)),
            pl.BlockSpec((tl, V7X_LANES), lambda i, j: (j, 0))],
        [stream[0], stream[1], mods, g, win, qn, kvn, wuq, wuk, wuvt, cos, sin])
    return pl.pallas_call(
        kern,
        out_shape=[sds((b, l, _MLA_QK_W), BF16), sds((b, l, _MLA_QK_W), BF16), sds((b, dv, l), BF16),
                   sds((b, l, SWA_HEADS * SWA_HEAD_DIM), BF16), sds((b, l, _SWA_W), BF16), sds((b, l, _SWA_W), BF16)],
        grid=(b, l // tl),
        in_specs=in_specs,
        out_specs=[tok(_MLA_QK_W), tok(_MLA_QK_W), pl.BlockSpec((1, dv, tl), lambda i, j: (i, 0, j)),
                   tok(SWA_HEADS * SWA_HEAD_DIM), tok(_SWA_W), tok(_SWA_W)],
        compiler_params=pltpu.CompilerParams(dimension_semantics=("parallel", "parallel"),
                                             vmem_limit_bytes=_vmem_limit(48)),
        name="attn_proj",
    )(*args)


def _mla_kernel(q_ref, k_ref, vt_ref, o_ref, *, nct_q, lc):
    hw = V7X_MXU_DIM

    def attend(nk):
        st = [_dot_nt(k_ref[0, 0:nk, hh * hw:(hh + 1) * hw], q_ref[0, :, hh * hw:(hh + 1) * hw])
              for hh in range(MLA_HEADS_PER_STEP)]
        for hh, s in enumerate(st):
            p = jnp.exp2(s - jnp.max(s, axis=0, keepdims=True))
            den = jnp.sum(p, axis=0, keepdims=True)
            ot = _dot(vt_ref[0, hh * MLA_V:(hh + 1) * MLA_V, 0:nk], p) / den
            o_ref[0, :, hh * MLA_V:(hh + 1) * MLA_V] = ot.T.astype(o_ref.dtype)

    @pl.when(pl.program_id(2) < nct_q)
    def _():
        attend(lc)

    @pl.when(pl.program_id(2) >= nct_q)
    def _():
        attend(k_ref.shape[1])


def _mla_attention(q, k, vt, lc, q_tile0, dep=None):
    b, l, _ = q.shape
    tq = MLA_Q_TILE
    hps = MLA_HEADS_PER_STEP
    kern, in_specs, args = _after(
        dep, functools.partial(_mla_kernel, nct_q=lc // tq - q_tile0, lc=lc),
        [pl.BlockSpec((1, tq, hps * V7X_MXU_DIM), lambda i, h, j: (i, j + q_tile0, h)),
         pl.BlockSpec((1, l, hps * V7X_MXU_DIM), lambda i, h, j: (i, 0, h)),
         pl.BlockSpec((1, hps * MLA_V, l), lambda i, h, j: (i, h, 0))],
        [q, k, vt])
    return pl.pallas_call(
        kern,
        out_shape=jax.ShapeDtypeStruct((b, l, MLA_HEADS * MLA_V), BF16),
        grid=(b, MLA_HEADS // hps, l // tq - q_tile0),
        in_specs=in_specs,
        out_specs=pl.BlockSpec((1, tq, hps * MLA_V), lambda i, h, j: (i, j + q_tile0, h)),
        compiler_params=pltpu.CompilerParams(dimension_semantics=("parallel", "parallel", "parallel"),
                                             vmem_limit_bytes=_vmem_limit(48)),
        name="mla_attention",
    )(*args)


def _swa_kernel(sink_ref, q_ref, k_ref, v_ref, o_ref, *, lc, q_tile0):
    tq = SWA_Q_TILE
    l = k_ref.shape[1]
    r0 = (pl.program_id(1) + q_tile0) * tq
    start = pl.multiple_of(jnp.clip(r0 - WINDOW, lc, l - SWA_BAND), WINDOW)
    rows = SWA_GROUP * tq
    row = lax.broadcasted_iota(jnp.int32, (rows, 1), 0)
    qpos = jnp.where(r0 >= lc, r0, -l) + row % tq
    kpos = start + lax.broadcasted_iota(jnp.int32, (1, SWA_BAND), 1)
    valid = jnp.abs(qpos - kpos) <= WINDOW
    lane = lax.broadcasted_iota(jnp.int32, (1, V7X_MXU_DIM), 1)
    head = [(lane // SWA_HEAD_DIM) == hh for hh in range(SWA_GROUP)]
    groups = range(SWA_KV_HEADS)
    sls = [slice(g * V7X_MXU_DIM, (g + 1) * V7X_MXU_DIM) for g in groups]
    qstack = []
    for sl in sls:
        qg = q_ref[0, :, sl]
        zero = jnp.zeros_like(qg)
        qstack.append(jnp.concatenate([jnp.where(head[hh], qg, zero) for hh in range(SWA_GROUP)], axis=0))
    sc = [_dot_nt(qstack[g], k_ref[0, 0:lc, sls[g]]) for g in groups]
    sb = [_dot_nt(qstack[g], k_ref[0, pl.ds(start, SWA_BAND), sls[g]]) for g in groups]
    for g in groups:
        sl = sls[g]
        sbm = jnp.where(valid, sb[g], NEG_INF)
        sk = jnp.zeros((rows, 1), F32)
        for hh in range(SWA_GROUP):
            sk = jnp.where(row // tq == hh, sink_ref[g * SWA_GROUP + hh] * LOG2E, sk)
        mx = jnp.maximum(jnp.maximum(jnp.max(sc[g], axis=-1, keepdims=True), jnp.max(sbm, axis=-1, keepdims=True)), sk)
        pc = jnp.exp2(sc[g] - mx)
        pb = jnp.exp2(sbm - mx)
        den = jnp.sum(pc, axis=-1, keepdims=True) + jnp.sum(pb, axis=-1, keepdims=True) + jnp.exp2(sk - mx)
        ostack = (_dot(pc, v_ref[0, 0:lc, sl]) + _dot(pb, v_ref[0, pl.ds(start, SWA_BAND), sl])) / den
        o = jnp.zeros((tq, V7X_MXU_DIM), F32)
        for hh in range(SWA_GROUP):
            o = o + jnp.where(head[hh], ostack[hh * tq:(hh + 1) * tq], 0.0)
        o_ref[0, :, sl] = o.astype(o_ref.dtype)


def _swa_attention(sinks, q, k, v, lc, q_tile0, dep=None):
    b, l, _ = q.shape
    tq = SWA_Q_TILE
    kern, in_specs, args = _after(
        dep, functools.partial(_swa_kernel, lc=lc, q_tile0=q_tile0),
        [pl.BlockSpec(memory_space=pltpu.SMEM),
         pl.BlockSpec((1, tq, SWA_HEADS * SWA_HEAD_DIM), lambda i, j: (i, j + q_tile0, 0)),
         pl.BlockSpec((1, l, _SWA_W), lambda i, j: (i, 0, 0)),
         pl.BlockSpec((1, l, _SWA_W), lambda i, j: (i, 0, 0))],
        [sinks, q, k, v])
    return pl.pallas_call(
        kern,
        out_shape=jax.ShapeDtypeStruct((b, l, SWA_HEADS * SWA_HEAD_DIM), BF16),
        grid=(b, l // tq - q_tile0),
        in_specs=in_specs,
        out_specs=pl.BlockSpec((1, tq, SWA_HEADS * SWA_HEAD_DIM), lambda i, j: (i, j + q_tile0, 0)),
        compiler_params=pltpu.CompilerParams(dimension_semantics=("parallel", "parallel"),
                                             vmem_limit_bytes=_vmem_limit(48)),
        name="swa_attention",
    )(*args)


def _pack_bf16_pair(x):
    w = x.shape[1] // 2
    lo = pltpu.bitcast(x[:, :w].astype(BF16).astype(F32), jnp.int32)
    hi = pltpu.bitcast(x[:, w:].astype(BF16).astype(F32), jnp.int32)
    return lax.shift_right_logical(lo, jnp.int32(16)) | (hi & jnp.int32(-65536))


def _unpack_bf16_pair(p):
    return pltpu.bitcast(p << 16, F32), pltpu.bitcast(p & jnp.int32(-65536), F32)


def _route(n2, wrt, bias, run_ref):
    n_hi = n2.astype(BF16)
    n_lo = (n2 - n_hi.astype(F32)).astype(BF16)
    w_hi = wrt.astype(BF16)
    w_lo = (wrt - w_hi.astype(F32)).astype(BF16)
    w_both = jnp.concatenate([w_hi, w_lo], axis=0)
    rows = n2.shape[0]
    score_blocks = []
    for o in range(0, rows, V7X_LANES):
        both = _dot_nt(w_both, n_hi[o:o + V7X_LANES])
        logits = both[:N_EXPERTS] + both[N_EXPERTS:] + _dot_nt(w_hi, n_lo[o:o + V7X_LANES])
        score_blocks.append(_sigmoid(logits))
    scores = jnp.concatenate(score_blocks, axis=1)

    def select(sc2):
        cols = sc2.shape[1]
        shape3 = (N_GROUPS, GROUP_SIZE, cols)
        choice = sc2.reshape(shape3) + bias
        ji = lax.broadcasted_iota(jnp.int32, shape3, 1).astype(F32)
        m1 = jnp.max(choice, axis=1, keepdims=True)
        first = jnp.min(jnp.where(choice == m1, ji, float(GROUP_SIZE)), axis=1, keepdims=True)
        m2 = jnp.max(jnp.where(ji == first, -jnp.inf, choice), axis=1, keepdims=True)
        gs = m1 + m2
        gidx = lax.broadcasted_iota(jnp.int32, gs.shape, 0).astype(F32)
        gsel = jnp.zeros_like(gs)
        for _ in range(TOPK_GROUPS):
            mx = jnp.max(gs, axis=0, keepdims=True)
            pick = gidx == jnp.min(jnp.where(gs == mx, gidx, float(N_GROUPS)), axis=0, keepdims=True)
            gsel = jnp.where(pick, 1.0, gsel)
            gs = jnp.where(pick, -jnp.inf, gs)
        cand = jnp.where(gsel > 0.0, choice, -jnp.inf).reshape(N_EXPERTS, cols)
        eidx = lax.broadcasted_iota(jnp.int32, (N_EXPERTS, cols), 0).astype(F32)
        out = []
        for _ in range(TOP_K):
            mx = jnp.max(cand, axis=0, keepdims=True)
            pick = eidx == jnp.min(jnp.where(cand == mx, eidx, float(N_EXPERTS)), axis=0, keepdims=True)
            out.append(jnp.where(pick, 1.0, 0.0))
            cand = jnp.where(pick, -jnp.inf, cand)
        return out

    blocks = [select(sc2) for sc2 in score_blocks]
    picks = [jnp.concatenate([blk[k] for blk in blocks], axis=1) > 0.0 for k in range(TOP_K)]
    ei = lax.broadcasted_iota(jnp.int32, (N_EXPERTS, rows), 0).astype(F32)
    esel = jnp.zeros((N_EXPERTS, rows), F32)
    for pick in picks:
        esel = jnp.where(pick, 1.0, esel)
    before = jnp.where(lax.broadcasted_iota(jnp.int32, (rows, rows), 0) < lax.broadcasted_iota(jnp.int32, (rows, rows), 1),
                       1.0, 0.0).astype(BF16)
    slot = jnp.dot(esel.astype(BF16), before, preferred_element_type=F32) + run_ref[...]
    run_ref[...] += jnp.sum(esel, axis=1, keepdims=True)
    sc = [jnp.sum(jnp.where(pick, scores, 0.0), axis=0, keepdims=True) for pick in picks]
    tot = sc[0]
    for x in sc[1:]:
        tot = tot + x
    k8 = lax.broadcasted_iota(jnp.int32, (8, rows), 0)
    kw = lax.broadcasted_iota(jnp.int32, (GATE_W, rows), 0)
    eid = jnp.zeros((8, rows), jnp.int32)
    rank = jnp.zeros((8, rows), jnp.int32)
    wk = jnp.zeros((GATE_W, rows), F32)
    for k, pick in enumerate(picks):
        e_k = jnp.sum(jnp.where(pick, ei, 0.0), axis=0, keepdims=True).astype(jnp.int32)
        r_k = jnp.sum(jnp.where(pick, slot, 0.0), axis=0, keepdims=True).astype(jnp.int32)
        eid = jnp.where(k8 == k, e_k, eid)
        rank = jnp.where(k8 == k, r_k, rank)
        wk = jnp.where(kw == k, sc[k] * (ROUTED_SCALE / tot), wk)
    return eid, rank, wk.T


def _mixer_tail(o, h, m, gffn_ref, wrt_ref, bias_ref, hn_ref, n2_ref, eid_ref, rank_ref, w_ref, cnt_ref, run_ref):
    @pl.when((pl.program_id(0) == 0) & (pl.program_id(1) == 0))
    def _():
        run_ref[...] = jnp.zeros_like(run_ref)

    hn = h + m[2:3] * o
    hn_ref[0] = hn
    n2 = _norm_mod(hn, gffn_ref[...], m[3:4], m[4:5])
    n2_ref[0] = _pack_bf16_pair(n2)
    eid, rank, wcols = _route(n2, wrt_ref[...], bias_ref[...], run_ref)
    eid_ref[0] = eid
    rank_ref[0] = rank
    w_ref[0] = wcols
    cnt_ref[...] = run_ref[...]


def _attn_out_kernel(a_ref, b_ref, c_ref, x_ref, mods_ref, wo_ref, gffn_ref, wrt_ref, bias_ref,
                     hn_ref, n2_ref, eid_ref, rank_ref, w_ref, cnt_ref, run_ref, *, nct):
    wa = MLA_HEADS * MLA_V
    o = _dot(a_ref[0], wo_ref[0:wa, :]) + _dot(b_ref[0], wo_ref[wa:, :])
    _mixer_tail(o, _stream_tile(c_ref, x_ref, nct), mods_ref[0, 0], gffn_ref, wrt_ref, bias_ref, hn_ref, n2_ref,
                eid_ref, rank_ref, w_ref, cnt_ref, run_ref)


def _tail_outs(b, l, d):
    tl = TOKEN_TILE
    nt = l // tl
    sds = jax.ShapeDtypeStruct
    tok = lambda w: pl.BlockSpec((1, tl, w), lambda i, j: (i, j, 0))
    blk = pl.BlockSpec((1, 8, tl), lambda i, j: (i * nt + j, 0, 0))
    shapes = [sds((b, l, d), F32), sds((b, l, d // 2), jnp.int32), sds((b * nt, 8, tl), jnp.int32),
              sds((b * nt, 8, tl), jnp.int32), sds((b, l, GATE_W), F32), sds((N_EXPERTS, 1), F32)]
    specs = [tok(d), tok(d // 2), blk, blk, tok(GATE_W), pl.BlockSpec((N_EXPERTS, 1), lambda i, j: (0, 0))]
    return shapes, specs


def _attn_out(a, bm, stream, mods, wo, gffn, wrt, bias, nct, dep=None):
    b, l, _ = a.shape
    d = stream[0].shape[2]
    tl = TOKEN_TILE
    tok = lambda w: pl.BlockSpec((1, tl, w), lambda i, j: (i, j, 0))
    full = lambda x: pl.BlockSpec(x.shape, lambda i, j: (0,) * x.ndim)
    shapes, specs = _tail_outs(b, l, d)
    kern, in_specs, args = _after(
        dep, functools.partial(_attn_out_kernel, nct=nct),
        [tok(a.shape[2]), tok(bm.shape[2])] + _stream_specs(stream, nct, tl) + [
            pl.BlockSpec((1, 1, N_MODS, d), lambda i, j: (i, jnp.where(j < nct, 0, 1), 0, 0)),
            full(wo), full(gffn), full(wrt), full(bias)],
        [a, bm, stream[0], stream[1], mods, wo, gffn, wrt, bias])
    return pl.pallas_call(
        kern,
        out_shape=shapes,
        grid=(b, l // tl),
        in_specs=in_specs,
        out_specs=specs,
        scratch_shapes=[pltpu.VMEM((N_EXPERTS, 1), F32)],
        compiler_params=pltpu.CompilerParams(dimension_semantics=("arbitrary", "arbitrary"),
                                             vmem_limit_bytes=_vmem_limit(40)),
        name="attn_out",
    )(*args)


def _moe_dest_kernel(off_ref, eid_ref, rank_ref, dest_ref):
    eid = eid_ref[...]
    dest = rank_ref[...]
    for e in range(N_EXPERTS):
        dest = dest + jnp.where(eid == e, off_ref[e], 0)
    dest_ref[...] = dest


def _moe_dest(off, eid, rank):
    return pl.pallas_call(
        _moe_dest_kernel,
        out_shape=jax.ShapeDtypeStruct(eid.shape, jnp.int32),
        in_specs=[pl.BlockSpec(memory_space=pltpu.SMEM),
                  pl.BlockSpec(eid.shape, lambda: (0, 0, 0)), pl.BlockSpec(eid.shape, lambda: (0, 0, 0))],
        out_specs=pl.BlockSpec(eid.shape, lambda: (0, 0, 0)),
        name="moe_dest",
    )(off, eid, rank)


def _sc_mesh():
    return plsc.VectorSubcoreMesh(core_axis_name="c", subcore_axis_name="s",
                                  num_cores=V7X_SC_CORES, num_subcores=V7X_SC_SUBCORES)


def _sc_chunk(rows_per_worker):
    return max(c for c in range(8, SC_MAX_CHUNK + 1, 8) if rows_per_worker % c == 0)


def _sc_dispatch(xp, dest, p_rows):
    t, w = xp.shape
    tpw = t // V7X_SC_WORKERS
    ch = _sc_chunk(tpw)

    @functools.partial(
        pl.kernel, mesh=_sc_mesh(), out_type=jax.ShapeDtypeStruct((p_rows, w), xp.dtype),
        scratch_types=[pltpu.VMEM((ch, w), xp.dtype)] + [pltpu.VMEM((ch,), jnp.int32)] * TOP_K
        + [pltpu.SemaphoreType.DMA, pltpu.SemaphoreType.DMA],
        name="moe_dispatch")
    def run(x_hbm, dest_hbm, out_hbm, rows_v, *rest):
        idx, (sem_i, sem_o) = rest[:TOP_K], rest[TOP_K:]
        base = (lax.axis_index("s") * V7X_SC_CORES + lax.axis_index("c")) * tpw

        @pl.loop(0, tpw // ch)
        def _(i):
            t0 = base + i * ch
            loads = [pltpu.async_copy(dest_hbm.at[k, pl.ds(t0, ch)], idx[k], sem_i) for k in range(TOP_K)]
            pltpu.sync_copy(x_hbm.at[pl.ds(t0, ch)], rows_v)
            for c in loads:
                c.wait()
            puts = [pltpu.async_copy(rows_v, out_hbm.at[idx[k]], sem_o) for k in range(TOP_K)]
            for c in puts:
                c.wait()

    return run(xp, dest)


def _sc_gather(ys, dest, t):
    w = ys.shape[1]
    tpw = t // V7X_SC_WORKERS
    ch = _sc_chunk(tpw)

    @functools.partial(
        pl.kernel, mesh=_sc_mesh(), out_type=jax.ShapeDtypeStruct((TOP_K, t, w), ys.dtype),
        scratch_types=[pltpu.VMEM((ch, w), ys.dtype)] * 2 + [pltpu.VMEM((ch,), jnp.int32)] * TOP_K
        + [pltpu.SemaphoreType.DMA] * 5,
        name="moe_gather")
    def run(y_hbm, dest_hbm, out_hbm, rows_a, rows_b, *rest):
        idx, (sem_i, sem_ga, sem_gb, sem_wa, sem_wb) = rest[:TOP_K], rest[TOP_K:]
        rows, sem_g, sem_w = (rows_a, rows_b), (sem_ga, sem_gb), (sem_wa, sem_wb)
        base = (lax.axis_index("s") * V7X_SC_CORES + lax.axis_index("c")) * tpw

        @pl.loop(0, tpw // ch)
        def _(i):
            t0 = base + i * ch
            loads = [pltpu.async_copy(dest_hbm.at[k, pl.ds(t0, ch)], idx[k], sem_i) for k in range(TOP_K)]
            for c in loads:
                c.wait()
            gets, puts = [None] * TOP_K, [None] * TOP_K
            gets[0] = pltpu.async_copy(y_hbm.at[idx[0]], rows[0], sem_g[0])
            for k in range(TOP_K):
                if k + 1 < TOP_K:
                    if k >= 1:
                        puts[k - 1].wait()
                    gets[k + 1] = pltpu.async_copy(y_hbm.at[idx[k + 1]], rows[(k + 1) % 2], sem_g[(k + 1) % 2])
                gets[k].wait()
                puts[k] = pltpu.async_copy(rows[k % 2], out_hbm.at[k, pl.ds(t0, ch)], sem_w[k % 2])
            puts[TOP_K - 2].wait()
            puts[TOP_K - 1].wait()

    return run(ys, dest)


def _cache_mlp_weights(wg, wu, wd, wgu_ref, wdb_ref):
    f = wg.shape[1]
    wgu_ref[:, 0:f] = wg.astype(BF16)
    wgu_ref[:, f:] = wu.astype(BF16)
    wdb_ref[...] = wd.astype(BF16)


def _gated_mlp(xp, wgu_ref, wdb_ref):
    lo, hi = _unpack_bf16_pair(xp)
    x = jnp.concatenate([lo.astype(BF16), hi.astype(BF16)], axis=1)
    gu = jnp.dot(x, wgu_ref[...], preferred_element_type=F32)
    f = gu.shape[1] // 2
    return _dot(_silu(gu[:, :f]) * gu[:, f:], wdb_ref[...])


def _moe_expert_kernel(te_ref, tb_ref, nv_ref, x_hbm, wga_ref, wua_ref, wda_ref, wgb_ref, wub_ref, wdb_ref, y_ref,
                       gu_a, dn_a, gu_b, dn_b, ids_ref, xbuf, sems):
    i = pl.program_id(0)
    tm = MOE_ROW_TILE
    nv = nv_ref[0]
    last = (nv - 1) // 2
    first = 2 * jnp.minimum(i, last)
    ea = te_ref[first]
    eb = te_ref[first + 1]
    two = 2 * i + 1 < nv

    def rows_copy(step):
        slot = step % MOE_ROW_SLOTS
        row0 = step * (2 * tm)
        rows = pl.ds(row0 if isinstance(step, int) else pl.multiple_of(row0, 2 * tm), 2 * tm)
        return pltpu.make_async_copy(x_hbm.at[rows], xbuf.at[slot], sems.at[slot])

    @pl.when(i == 0)
    def _():
        ids_ref[0] = -1
        ids_ref[1] = -1
        for ahead in range(MOE_ROW_SLOTS - 1):
            @pl.when(ahead <= last)
            def _():
                rows_copy(ahead).start()

    @pl.when(i + (MOE_ROW_SLOTS - 1) <= last)
    def _():
        rows_copy(i + (MOE_ROW_SLOTS - 1)).start()

    @pl.when(i <= last)
    def _():
        rows_copy(i).wait()

    x_ref = xbuf.at[i % MOE_ROW_SLOTS]

    @pl.when(ids_ref[0] != ea)
    def _():
        _cache_mlp_weights(wga_ref[0, 0], wua_ref[0, 0], wda_ref[0, 0], gu_a, dn_a)
        ids_ref[0] = ea

    @pl.when(two & (eb != ea) & (ids_ref[1] != eb))
    def _():
        _cache_mlp_weights(wgb_ref[0, 0], wub_ref[0, 0], wdb_ref[0, 0], gu_b, dn_b)
        ids_ref[1] = eb

    @pl.when(two & (eb == ea))
    def _():
        y_ref[...] = _pack_bf16_pair(_gated_mlp(x_ref[...], gu_a, dn_a))

    @pl.when((2 * i < nv) & jnp.logical_not(two & (eb == ea)))
    def _():
        y_ref[0:tm, :] = _pack_bf16_pair(_gated_mlp(x_ref[0:tm, :], gu_a, dn_a))

    @pl.when(two & (eb != ea))
    def _():
        y_ref[tm:, :] = _pack_bf16_pair(_gated_mlp(x_ref[tm:, :], gu_b, dn_b))


def _moe_experts(tile_expert, n_valid, xs, wg, wu, wd, layer, dep=None):
    p_rows, w = xs.shape
    tm = MOE_ROW_TILE
    _, _, d, f = wg.shape
    npair = p_rows // (2 * tm)
    pairs = tile_expert.reshape(npair, 2)
    tile_b = jnp.maximum(lax.cummax(jnp.where(pairs[:, 1] != pairs[:, 0], pairs[:, 1], -1)), 0)
    step = lambda i, nv: jnp.minimum(i, (nv[0] - 1) // 2)
    spec_a = lambda shp: pl.BlockSpec((1, 1) + shp, lambda i, te, tb, nv: (layer, te[2 * step(i, nv)], 0, 0))
    spec_b = lambda shp: pl.BlockSpec((1, 1) + shp, lambda i, te, tb, nv: (layer, tb[step(i, nv)], 0, 0))
    rows = pl.BlockSpec((2 * tm, w), lambda i, te, tb, nv: (step(i, nv), 0))
    kern, in_specs, args = _after(
        dep, _moe_expert_kernel,
        [pl.BlockSpec(memory_space=pl.ANY), spec_a((d, f)), spec_a((d, f)), spec_a((f, d)),
         spec_b((d, f)), spec_b((d, f)), spec_b((f, d))],
        [tile_expert, tile_b, n_valid, xs, wg, wu, wd, wg, wu, wd], n_lead=3)
    return pl.pallas_call(
        kern,
        out_shape=jax.ShapeDtypeStruct((p_rows, w), xs.dtype),
        grid_spec=pltpu.PrefetchScalarGridSpec(
            num_scalar_prefetch=3, grid=(npair,),
            in_specs=in_specs,
            out_specs=rows,
            scratch_shapes=[pltpu.VMEM((d, 2 * f), BF16), pltpu.VMEM((f, d), BF16),
                            pltpu.VMEM((d, 2 * f), BF16), pltpu.VMEM((f, d), BF16), pltpu.SMEM((2,), jnp.int32),
                            pltpu.VMEM((MOE_ROW_SLOTS, 2 * tm, w), xs.dtype),
                            pltpu.SemaphoreType.DMA((MOE_ROW_SLOTS,))]),
        compiler_params=pltpu.CompilerParams(dimension_semantics=("arbitrary",),
                                             vmem_limit_bytes=_vmem_limit(48)),
        name="moe_experts",
    )(*args)


def _moe_combine_kernel(yg_hbm, w_ref, xp_ref, sg_ref, su_ref, sd_ref, h_ref, mods_ref, gfin_ref, *rest, final_norm,
                        tile0):
    o_ref, wgu_ref, wdb_ref, ybuf, sems = rest[-5:]
    tl = ybuf.shape[2]
    nt = pl.num_programs(1)
    step = pl.program_id(0) * nt + pl.program_id(1)
    n_steps = pl.num_programs(0) * nt

    def yg_copy(s):
        rows = pl.ds(pl.multiple_of((s % nt + tile0) * tl, tl), tl)
        return pltpu.make_async_copy(yg_hbm.at[:, s // nt, rows, :], ybuf.at[s % MOE_ROW_SLOTS],
                                     sems.at[s % MOE_ROW_SLOTS])

    @pl.when(step == 0)
    def _():
        _cache_mlp_weights(sg_ref[0], su_ref[0], sd_ref[0], wgu_ref, wdb_ref)
        for ahead in range(MOE_ROW_SLOTS - 1):
            @pl.when(ahead < n_steps)
            def _():
                yg_copy(ahead).start()

    @pl.when(step + (MOE_ROW_SLOTS - 1) < n_steps)
    def _():
        yg_copy(step + (MOE_ROW_SLOTS - 1)).start()

    yg_copy(step).wait()
    yg_ref = ybuf.at[step % MOE_ROW_SLOTS]

    acc = _gated_mlp(xp_ref[0], wgu_ref, wdb_ref)
    half = acc.shape[1] // 2
    lo = acc[:, :half]
    hi = acc[:, half:]
    w = w_ref[0]
    for k in range(TOP_K):
        ylo, yhi = _unpack_bf16_pair(yg_ref[k])
        wk = w[:, k:k + 1]
        lo = lo + wk * ylo
        hi = hi + wk * yhi
    y = h_ref[0] + mods_ref[0, 0, N_MODS - 1:N_MODS, :] * jnp.concatenate([lo, hi], axis=1)
    if final_norm:
        y = _rms(y, gfin_ref[...])
    o_ref[0] = y


def _moe_combine(yg, wcols, xp, sg, su, sd, h, mods, gfin, nct, layer, out_buf, out_b0, out_batch, latent_only,
                 final_norm, in_b0=0, dep=None):
    b = yg.shape[1]
    _, l, d = h.shape
    tl = TOKEN_TILE
    tile0 = nct if latent_only else 0
    tok = lambda w: pl.BlockSpec((1, tl, w), lambda i, j: (i + in_b0, j + tile0, 0))
    lay = lambda x: pl.BlockSpec((1,) + x.shape[1:], lambda i, j: (layer,) + (0,) * (x.ndim - 1))
    args = [yg, wcols, xp, sg, su, sd, h, mods, gfin]
    in_specs = [pl.BlockSpec(memory_space=pl.ANY), tok(GATE_W), tok(d // 2),
                lay(sg), lay(su), lay(sd), tok(d),
                pl.BlockSpec((1, 1, N_MODS, d), lambda i, j: (i + in_b0, jnp.where(j + tile0 < nct, 0, 1), 0, 0)),
                pl.BlockSpec(gfin.shape, lambda i, j: (0, 0))]
    _, in_specs, args = _after(dep, None, in_specs, args)
    aliases = {}
    if out_buf is not None:
        args.append(out_buf)
        in_specs.append(pl.BlockSpec(memory_space=pl.ANY))
        aliases = {len(args) - 1: 0}
    return pl.pallas_call(
        functools.partial(_moe_combine_kernel, final_norm=final_norm, tile0=tile0),
        out_shape=jax.ShapeDtypeStruct((out_batch, l - tile0 * tl, d), F32),
        grid=(b, l // tl - tile0),
        in_specs=in_specs,
        out_specs=pl.BlockSpec((1, tl, d), lambda i, j: (i + out_b0, j, 0)),
        scratch_shapes=[pltpu.VMEM((d, 2 * sg.shape[2]), BF16), pltpu.VMEM((sg.shape[2], d), BF16),
                        pltpu.VMEM((MOE_ROW_SLOTS, TOP_K, tl, d // 2), yg.dtype),
                        pltpu.SemaphoreType.DMA((MOE_ROW_SLOTS,))],
        input_output_aliases=aliases,
        compiler_params=pltpu.CompilerParams(dimension_semantics=("arbitrary", "arbitrary"),
                                             vmem_limit_bytes=_vmem_limit(40)),
        name="moe_combine",
    )(*args)


def _moe_route_rows(n2p, eid, rank, counts, b, l):
    d2 = n2p.shape[2]
    t = b * l
    tm = MOE_ROW_TILE
    n_tiles = 2 * -(-(TOP_K * t + N_EXPERTS * (tm - 1)) // (2 * tm))
    tiles_e = (counts.reshape(N_EXPERTS).astype(jnp.int32) + (tm - 1)) // tm
    tile_end = jnp.cumsum(tiles_e)
    off = (tile_end - tiles_e) * tm
    n_valid = tile_end[-1:]
    tile_id = jnp.minimum(jnp.arange(n_tiles, dtype=jnp.int32), n_valid - 1)
    tile_expert = jnp.sum((tile_end[None, :] <= tile_id[:, None]).astype(jnp.int32), axis=1)
    dest = _moe_dest(off, eid, rank).transpose(1, 0, 2).reshape(8, t)
    xs = _sc_dispatch(n2p.reshape(t, d2), dest, n_tiles * tm)
    return xs, dest, tile_expert, n_valid


def _rwkv_proj_kernel(h_ref, hp_ref, hx_ref, mods_ref, g_ref, mu_ref, wr_ref, wk_ref, wv_ref, g1_ref, g2_ref,
                      w1_ref, w2_ref, a1_ref, a2_ref, w0_ref, a0_ref, kk_ref, ka_ref, rk_ref, bd_ref,
                      r_out, v_out, kk_out, g_out, km_out, b_out, lw_out, bonus_out, *, nct):
    j = pl.program_id(1)
    nt = pl.num_programs(1)
    m = mods_ref[0, 0]
    g = g_ref[...]
    n = _norm_mod(h_ref[0], g, m[0:1], m[1:2])
    tl, d = n.shape
    seg_first = (j == 0) | (j == nct)
    seg_last = (j == nct - 1) | (j == nt - 1)
    n_prev = _norm_mod(hp_ref[0], g, m[0:1], m[1:2])[7:8] * jnp.where(seg_first, 0.0, 1.0)
    n_next = _norm_mod(hx_ref[0], g, m[0:1], m[1:2])[0:1] * jnp.where(seg_last, 0.0, 1.0)
    row = lax.broadcasted_iota(jnp.int32, (tl, 1), 0)
    prev = jnp.where(row == 0, n_prev, pltpu.roll(n, 1, axis=0))
    nxt = jnp.where(row == tl - 1, n_next, pltpu.roll(n, tl - 1, axis=0))
    lane = lax.broadcasted_iota(jnp.int32, (1, d), 1)
    xx = jnp.where(lane < d // 2, prev, nxt) - n
    mu = mu_ref[...]
    bd = bd_ref[...]
    halves = [slice(0, tl // 2), slice(tl // 2, tl)]
    first = []
    for rs in halves:
        nh, xh = n[rs], xx[rs]
        xr, xw, xk, xv, xa, xg = [nh + xh * mu[i:i + 1] for i in range(6)]
        first.append((_dot(xr, wr_ref[...]), _dot(xk, wk_ref[...]), _dot(xv, wv_ref[...]),
                      _dot(xg, g1_ref[...]), _dot(xw, w1_ref[...]), _dot(xa, a1_ref[...])))
    second = []
    for r, k, v, gq, tq, ta in first:
        tw = jnp.tanh(tq)
        kk = k * kk_ref[...]
        second.append((_dot(_sigmoid(gq), g2_ref[...]), [_dot(tw, w2_ref[dr]) for dr in range(2)],
                       [_dot(ta, a2_ref[dr]) for dr in range(2)], kk, _head_sum(kk * kk, bd)))
    for rs, (r, k, v, _, _, _), (gate, zw, za, kk, kk_sq) in zip(halves, first, second):
        kk = kk / jnp.maximum(jnp.sqrt(kk_sq), 1e-12)
        g_out[0, rs, :] = gate.astype(g_out.dtype)
        r_out[0, rs, :] = r.astype(r_out.dtype)
        v_out[0, rs, :] = v.astype(v_out.dtype)
        kk_out[0, rs, :] = kk.astype(kk_out.dtype)
        bonus = jnp.zeros_like(v)
        for dr in range(2):
            lw_out[dr, 0, rs, :] = -jnp.exp(-0.5) * _sigmoid(w0_ref[dr:dr + 1, :] + zw[dr])
            a = _sigmoid(a0_ref[dr:dr + 1, :] + za[dr])
            km = k * (1.0 + (a - 1.0) * ka_ref[...])
            km_out[dr, 0, rs, :] = km.astype(km_out.dtype)
            b_out[dr, 0, rs, :] = (kk * a).astype(b_out.dtype)
            bonus = bonus + _head_sum(r * km * rk_ref[...], bd) * v
        bonus_out[0, rs, :] = bonus.astype(bonus_out.dtype)


def _rwkv_proj(h, mods, g, mu, wr, wk, wv, g1, g2, w1, w2, a1, a2, w0, a0, kk, ka, rk, bd, nct, dep=None):
    b, l, d = h.shape
    tl = TOKEN_TILE
    nb8 = l // 8
    tok = pl.BlockSpec((1, tl, d), lambda i, j: (i, j, 0))
    tok2 = pl.BlockSpec((2, 1, tl, d), lambda i, j: (0, i, j, 0))
    full = lambda x: pl.BlockSpec(x.shape, lambda i, j: (0,) * x.ndim)
    sds = jax.ShapeDtypeStruct
    kern, in_specs, args = _after(
        dep, functools.partial(_rwkv_proj_kernel, nct=nct),
        [tok,
         pl.BlockSpec((1, 8, d), lambda i, j: (i, jnp.maximum(j * (tl // 8) - 1, 0), 0)),
         pl.BlockSpec((1, 8, d), lambda i, j: (i, jnp.minimum((j + 1) * (tl // 8), nb8 - 1), 0)),
         pl.BlockSpec((1, 1, N_MODS, d), lambda i, j: (i, jnp.where(j < nct, 0, 1), 0, 0)),
         full(g), full(mu), full(wr), full(wk), full(wv), full(g1), full(g2), full(w1), full(w2),
         full(a1), full(a2), full(w0), full(a0), full(kk), full(ka), full(rk), full(bd)],
        [h, h, h, mods, g, mu, wr, wk, wv, g1, g2, w1, w2, a1, a2, w0, a0, kk, ka, rk, bd])
    return pl.pallas_call(
        kern,
        out_shape=[sds((b, l, d), BF16), sds((b, l, d), BF16), sds((b, l, d), BF16), sds((b, l, d), BF16),
                   sds((2, b, l, d), BF16), sds((2, b, l, d), BF16), sds((2, b, l, d), F32), sds((b, l, d), BF16)],
        grid=(b, l // tl),
        in_specs=in_specs,
        out_specs=[tok, tok, tok, tok, tok2, tok2, tok2, tok],
        compiler_params=pltpu.CompilerParams(dimension_semantics=("parallel", "parallel"),
                                             vmem_limit_bytes=_vmem_limit(56)),
        name="rwkv_proj",
    )(*args)


def _wkv_kernel(r_ref, v_ref, kk_ref, km_ref, b_ref, lw_ref, y_ref, st_ref):
    c = WKV_CHUNK
    w = WKV_PAIR
    rev = pl.program_id(0)
    sign = 1 - 2 * rev

    @pl.when(pl.program_id(2) == 0)
    def _():
        st_ref[...] = jnp.zeros_like(st_ref)

    ti = lax.broadcasted_iota(jnp.int32, (c, c), 0)
    si = lax.broadcasted_iota(jnp.int32, (c, c), 1)
    tri = jnp.where((si - ti) * sign <= 0, 1.0, 0.0).astype(BF16)
    tri3 = jnp.concatenate([tri, tri, tri], axis=1)
    nsub = WKV_CHUNKS_PER_STEP
    subs = [pl.ds(pl.multiple_of(jnp.where(rev == 0, s, nsub - 1 - s) * c, c), c) for s in range(nsub)]
    rt, kt, kh, bh, v32, e_mid = [], [], [], [], [], []
    for rows in subs:
        lw = lw_ref[0, 0, rows, :]
        t1 = lw.astype(BF16)
        d1 = lw - t1.astype(F32)
        t2 = d1.astype(BF16)
        t3 = (d1 - t2.astype(F32)).astype(BF16)
        l_incl = jnp.dot(tri3, jnp.concatenate([t1, t2, t3], axis=0), preferred_element_type=F32)
        mid = 0.5 * jnp.sum(lw, axis=0, keepdims=True)
        e_neg = jnp.exp(mid - l_incl)
        e_mid.append(jnp.exp(mid))
        rt.append(r_ref[0, rows, :].astype(F32) * jnp.exp(l_incl - mid))
        kt.append(kk_ref[0, rows, :].astype(F32) * jnp.exp(l_incl - lw - mid))
        kh.append(km_ref[0, 0, rows, :].astype(F32) * e_neg)
        bh.append(b_ref[0, 0, rows, :].astype(F32) * e_neg)
        v32.append(v_ref[0, rows, :].astype(F32))

    ri = lax.broadcasted_iota(jnp.int32, (w, w), 0)
    ci = lax.broadcasted_iota(jnp.int32, (w, w), 1)
    same = (ri // c) == (ci // c)
    eye = jnp.where(ri == ci, 1.0, 0.0).astype(F32)
    tl_ = lax.broadcasted_iota(jnp.int32, (c, w), 0)
    jl_ = lax.broadcasted_iota(jnp.int32, (c, w), 1) % c
    strict = (jl_ - tl_) * sign < 0
    incl = (jl_ - tl_) * sign <= 0
    eye2 = jnp.where(jl_ == tl_, 1.0, 0.0).astype(F32)
    lane = lax.broadcasted_iota(jnp.int32, (1, w), 1)
    h0 = lane < RWKV_HEAD

    def rows2(x):
        return jnp.concatenate([jnp.where(h0, x, 0.0), jnp.where(h0, 0.0, x)], axis=0)

    npair = st_ref.shape[0]
    items = [(s, slice(p * w, (p + 1) * w)) for s in range(nsub) for p in range(npair)]
    n = range(len(items))
    em = [e_mid[s][:, sl] for s, sl in items]
    g = [_dot_nt(jnp.concatenate([kt[s][:, sl], rt[s][:, sl]], axis=0),
                 jnp.concatenate([rows2(kh[s][:, sl]), rows2(bh[s][:, sl])], axis=0)) for s, sl in items]
    a_kk = [jnp.where(strict, x[:c, :w], 0.0) for x in g]
    a_rk = [jnp.where(incl, x[c:, :w], 0.0) for x in g]
    a_rb = [jnp.where(incl, x[c:, w:], 0.0) for x in g]
    vi = [v32[s][:, sl] for s, sl in items]
    v_rows = [rows2(x) for x in vi]
    av = [_dot(jnp.concatenate([a_kk[i], a_rk[i]], axis=0), v_rows[i]) for i in n]
    r_pre = [x[:c] for x in av]
    ark_v = [x[c:] for x in av]
    m = [jnp.where(strict, -x[:c, w:], 0.0) for x in g]
    tinv = [eye2 + x for x in m]
    m = [_dot(x, rows2(x)) for x in m]
    for _ in range(c.bit_length() - 3):
        both = [_dot(jnp.concatenate([tinv[i], m[i]], axis=0), rows2(m[i])) for i in n]
        tinv = [tinv[i] + both[i][:c] for i in n]
        m = [x[c:] for x in both]
    tinv = [tinv[i] + _dot(tinv[i], rows2(m[i])) for i in n]
    sol = [_dot(tinv[i], jnp.concatenate([rows2(r_pre[i]), rows2(kt[s][:, sl] * em[i])], axis=1))
           for i, (s, sl) in enumerate(items)]
    u_pre = [x[:, :w] for x in sol]
    kq = [x[:, w:] for x in sol]
    arb = [_dot(a_rb[i], jnp.concatenate([rows2(u_pre[i]), rows2(kq[i])], axis=1)) for i in n]
    y_pre = [ark_v[i] - arb[i][:, :w] for i in n]
    r_eff = [rt[s][:, sl] * em[i] - arb[i][:, w:] for i, (s, sl) in enumerate(items)]
    bbar = [bh[s][:, sl] * em[i] for i, (s, sl) in enumerate(items)]
    kbar = [kh[s][:, sl] * em[i] for i, (s, sl) in enumerate(items)]
    mmat = [eye * (em[i] * em[i]) - jnp.where(same, _dot_tn(kq[i], bbar[i]), 0.0) for i in n]
    s_pre = [jnp.where(same, _dot_tn(jnp.concatenate([vi[i], -u_pre[i]], axis=0),
                                     jnp.concatenate([kbar[i], bbar[i]], axis=0)), 0.0) for i in n]
    st = [st_ref[p] for p in range(npair)]
    for i, (s, sl) in enumerate(items):
        p = i % npair
        y_ref[0, 0, subs[s], sl] = (_dot_nt(r_eff[i], st[p]) + y_pre[i]).astype(y_ref.dtype)
        hi = st[p].astype(BF16)
        lo = (st[p] - hi.astype(F32)).astype(BF16)
        mb = mmat[i].astype(BF16)
        both = jnp.dot(jnp.concatenate([hi, lo], axis=0), mb, preferred_element_type=F32)
        st[p] = both[:w] + both[w:] + s_pre[i]
    for p in range(npair):
        st_ref[p] = st[p]


def _wkv(r, v, kk, km, bv, lw, lc, dep=None):
    b, l, d = r.shape
    c = WKV_CHUNK * WKV_CHUNKS_PER_STEP
    ncc = lc // c
    nlc = (l - lc) // c

    def chunk(dr, i):
        return jnp.where(dr == 0, i, jnp.where(i < ncc, ncc - 1 - i, nlc + 2 * ncc - 1 - i))

    shared = pl.BlockSpec((1, c, d), lambda dr, bi, i: (bi, chunk(dr, i), 0))
    per_dir = pl.BlockSpec((1, 1, c, d), lambda dr, bi, i: (dr, bi, chunk(dr, i), 0))
    kern, in_specs, args = _after(dep, _wkv_kernel, [shared, shared, shared, per_dir, per_dir, per_dir],
                                  [r, v, kk, km, bv, lw])
    return pl.pallas_call(
        kern,
        out_shape=jax.ShapeDtypeStruct((2, b, l, d), BF16),
        grid=(2, b, l // c),
        in_specs=in_specs,
        out_specs=per_dir,
        scratch_shapes=[pltpu.VMEM((d // WKV_PAIR, WKV_PAIR, WKV_PAIR), F32)],
        compiler_params=pltpu.CompilerParams(dimension_semantics=("parallel", "parallel", "arbitrary"),
                                             vmem_limit_bytes=_vmem_limit(32)),
        name="wkv7_chunked",
    )(*args)


def _rwkv_out_kernel(y_ref, bonus_ref, g_ref, lnw_ref, lnb_ref, wo_ref, bd_ref, h_ref, mods_ref, gffn_ref,
                     wrt_ref, bias_ref, hn_ref, n2_ref, eid_ref, rank_ref, w_ref, cnt_ref, run_ref):
    y = y_ref[0, 0].astype(F32) + y_ref[1, 0].astype(F32)
    bd = bd_ref[...]
    mean = _head_sum(y, bd) * (1.0 / RWKV_HEAD)
    yc = y - mean
    var = _head_sum(yc * yc, bd) * (1.0 / RWKV_HEAD)
    yn = yc * lax.rsqrt(var + GN_EPS) * lnw_ref[...] + lnb_ref[...]
    out = (yn + bonus_ref[0].astype(F32)) * g_ref[0].astype(F32)
    _mixer_tail(_dot(out, wo_ref[...]), h_ref[0], mods_ref[0, 0], gffn_ref, wrt_ref, bias_ref, hn_ref, n2_ref,
                eid_ref, rank_ref, w_ref, cnt_ref, run_ref)


def _rwkv_out(y, bonus, g, lnw, lnb, wo, bd, h, mods, gffn, wrt, bias, nct, dep=None):
    b, l, d = h.shape
    tl = TOKEN_TILE
    tok = lambda w: pl.BlockSpec((1, tl, w), lambda i, j: (i, j, 0))
    full = lambda x: pl.BlockSpec(x.shape, lambda i, j: (0,) * x.ndim)
    shapes, specs = _tail_outs(b, l, d)
    kern, in_specs, args = _after(
        dep, _rwkv_out_kernel,
        [pl.BlockSpec((2, 1, tl, d), lambda i, j: (0, i, j, 0)), tok(d), tok(d),
         full(lnw), full(lnb), full(wo), full(bd), tok(d),
         pl.BlockSpec((1, 1, N_MODS, d), lambda i, j: (i, jnp.where(j < nct, 0, 1), 0, 0)),
         full(gffn), full(wrt), full(bias)],
        [y, bonus, g, lnw, lnb, wo, bd, h, mods, gffn, wrt, bias])
    return pl.pallas_call(
        kern,
        out_shape=shapes,
        grid=(b, l // tl),
        in_specs=in_specs,
        out_specs=specs,
        scratch_shapes=[pltpu.VMEM((N_EXPERTS, 1), F32)],
        compiler_params=pltpu.CompilerParams(dimension_semantics=("arbitrary", "arbitrary"),
                                             vmem_limit_bytes=_vmem_limit(40)),
        name="rwkv_out",
    )(*args)


def _rope_table(n_lat, n_ctx):
    dim = SWA_HEAD_DIM
    nf = dim // 4
    inv = ROPE_THETA ** (-jnp.arange(nf, dtype=F32) / nf)
    row = jnp.repeat(jnp.arange(n_lat // GRID_W, dtype=F32), GRID_W)
    col = jnp.tile(jnp.arange(GRID_W, dtype=F32), n_lat // GRID_W)
    ar = row[:, None] * inv
    ac = col[:, None] * inv
    ang = jnp.concatenate([ar, ar, ac, ac], axis=-1)
    cos = jnp.concatenate([jnp.ones((n_ctx, dim), F32), jnp.cos(ang)], axis=0)
    sin = jnp.concatenate([jnp.zeros((n_ctx, dim), F32), jnp.sin(ang)], axis=0)
    return jnp.tile(cos, (1, 2)), jnp.tile(sin, (1, 2))


def _layout_attn_weights(w_in, w_uq, w_ukv):
    d = w_in.shape[0]
    s0 = MLA_Q_RANK
    s1 = s0 + MLA_KV_RANK
    s2 = s1 + MLA_ROPE
    s3 = s2 + SWA_HEADS * SWA_HEAD_DIM
    s4 = s3 + SWA_KV_HEADS * SWA_HEAD_DIM
    rep = lambda w: jnp.concatenate(
        [jnp.tile(w[:, g * SWA_HEAD_DIM:(g + 1) * SWA_HEAD_DIM], (1, V7X_LANES // SWA_HEAD_DIM))
         for g in range(SWA_KV_HEADS)], axis=1)
    win = jnp.concatenate([w_in[:, :s1], w_in[:, s2:s3], rep(w_in[:, s3:s4]), rep(w_in[:, s4:]),
                           w_in[:, s1:s2], jnp.zeros((d, V7X_LANES - MLA_ROPE), w_in.dtype)], axis=1)
    qh = MLA_NOPE + MLA_ROPE
    pad = jnp.zeros((w_uq.shape[0], V7X_MXU_DIM - qh), w_uq.dtype)
    wuq = jnp.concatenate([jnp.concatenate([w_uq[:, h * qh:(h + 1) * qh], pad], axis=1) for h in range(MLA_HEADS)], axis=1)
    kvh = MLA_NOPE + MLA_V
    wuk = jnp.concatenate([w_ukv[:, h * kvh:h * kvh + MLA_NOPE] for h in range(MLA_HEADS)], axis=1)
    wuvt = jnp.concatenate([w_ukv[:, h * kvh + MLA_NOPE:(h + 1) * kvh] for h in range(MLA_HEADS)], axis=1).T
    return win.astype(BF16), wuq.astype(BF16), wuk.astype(BF16), wuvt.astype(BF16)


def _lora_pair(w_down, w_up):
    rank = w_down.shape[2]
    down = jnp.concatenate([w_down[0], w_down[1]], axis=1)
    z = jnp.zeros((rank, w_up.shape[2]), w_up.dtype)
    up = jnp.stack([jnp.concatenate([w_up[0], z], axis=0), jnp.concatenate([z, w_up[1]], axis=0)], axis=0)
    return down.astype(BF16), up.astype(BF16)


def _head_block_diag():
    i = jnp.arange(V7X_MXU_DIM) // RWKV_HEAD
    return (i[:, None] == i[None, :]).astype(BF16)


def kernel(x, c, ctx, c_ctx, ada_w, ada_b, norm_mix, norm_ffn, norm_final, attn_w_in, attn_q_norm, attn_kv_norm, attn_w_uq, attn_w_ukv, attn_sinks, attn_w_o, rwkv_mu, rwkv_w_r, rwkv_w_k, rwkv_w_v, rwkv_w_o, rwkv_g1, rwkv_g2, rwkv_w0, rwkv_w1, rwkv_w2, rwkv_a0, rwkv_a1, rwkv_a2, rwkv_k_k, rwkv_k_a, rwkv_r_k, rwkv_ln_w, rwkv_ln_b, moe_router, moe_bias, moe_w_gate, moe_w_up, moe_w_down, moe_ws_gate, moe_ws_up, moe_ws_down):
    bsz, s, d = x.shape
    lc = ctx.shape[1]
    l = lc + s
    depth = ada_w.shape[0]
    nct = lc // TOKEN_TILE
    assert lc % TOKEN_TILE == 0 and s % TOKEN_TILE == 0 and s >= SWA_BAND and lc % SWA_Q_TILE == 0
    assert lc % (WKV_CHUNK * WKV_CHUNKS_PER_STEP) == 0
    assert d % V7X_MXU_DIM == 0 and WKV_CHUNK * 2 == V7X_LANES
    ngrp = SAMPLE_GROUPS
    bg = bsz // ngrp
    assert bsz % ngrp == 0 and bg % LAST_COMBINE_PARTS == 0
    assert (bg // LAST_COMBINE_PARTS * l) % (8 * V7X_SC_WORKERS) == 0

    assert ngrp == 2
    cos, sin = _rope_table(s, lc)
    bd = _head_block_diag()
    rows = -(-(bsz + 1) // 8) * 8
    cc = jnp.concatenate([c, c_ctx[None, :], jnp.zeros((rows - bsz - 1, d), F32)], axis=0)
    row2 = lambda a: a.reshape(1, -1)
    moe_w = (moe_w_gate, moe_w_up, moe_w_down)
    moe_ws = (moe_ws_gate, moe_ws_up, moe_ws_down)

    shared = {}

    def layer_weights(li):
        if li not in shared:
            i = li // 2
            ada = _ada_mods(cc, ada_w, ada_b, li)
            w = dict(
                mods=jnp.stack([jnp.broadcast_to(ada[bsz].reshape(1, N_MODS, d), (bsz, N_MODS, d)),
                                ada[:bsz].reshape(bsz, N_MODS, d)], axis=1),
                wrt=moe_router[li].T,
                bias=moe_bias[li].reshape(N_GROUPS, GROUP_SIZE, 1))
            if li % 2 == 0:
                w["win"], w["wuq"], w["wuk"], w["wuvt"] = _layout_attn_weights(attn_w_in[i], attn_w_uq[i], attn_w_ukv[i])
                w["wo"] = attn_w_o[i].astype(BF16)
            else:
                w["w1"], w["w2"] = _lora_pair(rwkv_w1[i], rwkv_w2[i])
                w["a1"], w["a2"] = _lora_pair(rwkv_a1[i], rwkv_a2[i])
                w["wr"], w["wk"], w["wv"], w["wo"] = [x[i].astype(BF16) for x in (rwkv_w_r, rwkv_w_k, rwkv_w_v, rwkv_w_o)]
                w["g1"], w["g2"] = rwkv_g1[i].astype(BF16), rwkv_g2[i].astype(BF16)
            shared[li] = w
        return shared[li]

    groups = [dict(stream=(ctx, x, g * bg, 0), b0=g * bg) for g in range(ngrp)]
    result = [None]

    def run_stage(st, li, name, dep):
        w = layer_weights(li)
        i = li // 2
        with_ctx = li < depth - 1
        mods = w["mods"][st["b0"]:st["b0"] + bg]
        if name == "proj" and li % 2 == 0:
            st["qkv"] = _attn_proj(st["stream"], bg, l, mods, row2(norm_mix[li]), w["win"], row2(attn_q_norm[i]),
                                   row2(attn_kv_norm[i]), w["wuq"], w["wuk"], w["wuvt"], cos, sin, nct, dep=dep)
            return st["qkv"][0]
        if name == "mid" and li % 2 == 0:
            q, k, vt, qs, ks, vs = st.pop("qkv")
            st["a"] = _mla_attention(q, k, vt, lc, 0 if with_ctx else lc // MLA_Q_TILE, dep=dep)
            st["bm"] = _swa_attention(attn_sinks[i], qs, ks, vs, lc, 0 if with_ctx else lc // SWA_Q_TILE, dep=st["a"])
            return st["bm"]
        if name == "proj":
            assert st["stream"][0] is st["stream"][1]
            st["feat"] = _rwkv_proj(st["stream"][0], mods, row2(norm_mix[li]), rwkv_mu[i], w["wr"], w["wk"], w["wv"],
                                    w["g1"], w["g2"], w["w1"], w["w2"], w["a1"], w["a2"], rwkv_w0[i], rwkv_a0[i],
                                    row2(rwkv_k_k[i]), row2(rwkv_k_a[i]), row2(rwkv_r_k[i]), bd, nct, dep=dep)
            return st["feat"][0]
        if name == "mid":
            r, v, kk, gt, km, bv, lw, bonus = st.pop("feat")
            st["y"] = _wkv(r, v, kk, km, bv, lw, lc, dep=dep)
            st["gate"], st["bonus"] = gt, bonus
            return st["y"]
        if name == "out":
            if li % 2 == 0:
                tail = _attn_out(st.pop("a"), st.pop("bm"), st["stream"], mods, w["wo"], row2(norm_ffn[li]),
                                 w["wrt"], w["bias"], nct, dep=dep)
            else:
                tail = _rwkv_out(st.pop("y"), st.pop("bonus"), st.pop("gate"), row2(rwkv_ln_w[i]), row2(rwkv_ln_b[i]),
                                 w["wo"], bd, st["stream"][0], mods, row2(norm_ffn[li]), w["wrt"], w["bias"], nct, dep=dep)
            st["h"], st["n2p"], eid, rank, st["wcols"], counts = tail
            st["xs"], st["dest"], st["tile_expert"], st["n_valid"] = _moe_route_rows(st["n2p"], eid, rank, counts, bg, l)
            return st["h"]
        if name == "experts":
            ys = _moe_experts(st.pop("tile_expert"), st.pop("n_valid"), st.pop("xs"), *moe_w, li, dep=dep)
            bp = bg // (LAST_COMBINE_PARTS if li == depth - 1 else 1)
            dest = st.pop("dest")
            st["yg"] = [_sc_gather(ys, dest[:, p * bp * l:(p + 1) * bp * l], bp * l).reshape(TOP_K, bp, l, d // 2)
                        for p in range(bg // bp)]
            return ys
        assert name == "combine"
        last = li == depth - 1
        wcols, n2p, hres = st.pop("wcols"), st.pop("n2p"), st.pop("h")
        for p, yg in enumerate(st.pop("yg")):
            bp = yg.shape[1]
            h = _moe_combine(yg, wcols, n2p, *moe_ws, hres, mods, row2(norm_final), nct, li,
                             result[0] if last else None, st["b0"] + p * bp if last else 0, bsz if last else bg,
                             last, last, in_b0=p * bp, dep=dep)
            dep = h
            if last:
                result[0] = h
        if not last:
            st["stream"] = (h, h, 0, nct)
        return h

    order = [(0, 0, "proj"), (0, 0, "mid")]
    for li in range(depth):
        order += [(0, li, "out"), (1, li, "proj"), (0, li, "experts"), (1, li, "mid")]
        if li < depth - 1:
            order += [(0, li, "combine"), (1, li, "out"), (0, li + 1, "proj"), (1, li, "experts"),
                      (0, li + 1, "mid"), (1, li, "combine")]
        else:
            order += [(1, li, "out"), (0, li, "combine"), (1, li, "experts"), (1, li, "combine")]
    dep = None
    for g, li, name in order:
        dep = run_stage(groups[g], li, name, dep)
    return result[0]
```

```python
import functools

import jax
import jax.numpy as jnp
from jax import lax
from jax.experimental import pallas as pl
from jax.experimental.pallas import tpu as pltpu
from jax.experimental.pallas import tpu_sc as plsc

F32 = jnp.float32
BF16 = jnp.bfloat16
HIGHEST = lax.Precision.HIGHEST

GRID_W = 64
NORM_EPS = 1e-6
ROPE_THETA = 10000.0
NEG_INF = -1e30
N_MODS = 6

MLA_HEADS = 4
MLA_Q_RANK = 384
MLA_KV_RANK = 256
MLA_NOPE = 128
MLA_ROPE = 64
MLA_V = 128

SWA_HEADS = 8
SWA_KV_HEADS = 2
SWA_GROUP = SWA_HEADS // SWA_KV_HEADS
SWA_HEAD_DIM = 64
WINDOW = 128

RWKV_HEAD = 64
DECAY_LORA = 64
ICLR_LORA = 64
GATE_LORA = 128
GN_EPS = 64e-5

N_EXPERTS = 64
TOP_K = 6
N_GROUPS = 8
TOPK_GROUPS = 4
GROUP_SIZE = N_EXPERTS // N_GROUPS
ROUTED_SCALE = 2.5
GATE_W = 128

V7X_LANES = 128
V7X_MXU_DIM = 256
V7X_VMEM_BYTES = 64 * 1024 * 1024
V7X_SC_CORES = 2
V7X_SC_SUBCORES = 16
V7X_SC_WORKERS = V7X_SC_CORES * V7X_SC_SUBCORES

TOKEN_TILE = 256
MLA_Q_TILE = 256
MLA_HEADS_PER_STEP = 2
SWA_Q_TILE = 256
SWA_BAND = SWA_Q_TILE + 2 * WINDOW
WKV_CHUNK = 64
WKV_PAIR = 2 * RWKV_HEAD
WKV_CHUNKS_PER_STEP = 4
MOE_ROW_TILE = 512
MOE_ROW_SLOTS = 3
SAMPLE_GROUPS = 2
LAST_COMBINE_PARTS = 2
SC_MAX_CHUNK = 64

LOG2E = 1.4426950408889634
MIB = 1024 * 1024
VMEM_RESERVE_BYTES = 4 * MIB


def _vmem_limit(mib):
    return min(mib * MIB, V7X_VMEM_BYTES - VMEM_RESERVE_BYTES)


def _dot(a, b):
    return jnp.dot(a.astype(BF16), b.astype(BF16), preferred_element_type=F32)


def _dot_nt(a, b):
    return lax.dot_general(a.astype(BF16), b.astype(BF16), (((1,), (1,)), ((), ())),
                           preferred_element_type=F32)


def _dot_tn(a, b):
    return lax.dot_general(a.astype(BF16), b.astype(BF16), (((0,), (0,)), ((), ())),
                           preferred_element_type=F32)


def _sigmoid(x):
    return 1.0 / (1.0 + jnp.exp(-x))


def _silu(x):
    return x * _sigmoid(x)


def _rms(x, g):
    return x * lax.rsqrt(jnp.mean(x * x, axis=-1, keepdims=True) + NORM_EPS) * g


def _norm_mod(x, g, shift, scale):
    return _rms(x, g) * (1.0 + scale) + shift


def _split_dot(x, w):
    hi = x.astype(BF16)
    lo = (x - hi.astype(F32)).astype(BF16)
    return (jnp.dot(hi, w, preferred_element_type=F32) + jnp.dot(lo, w, preferred_element_type=F32))


def _head_sum(x, bd):
    w = bd.shape[0]
    parts = [_split_dot(x[:, c * w:(c + 1) * w], bd) for c in range(x.shape[1] // w)]
    return jnp.concatenate(parts, axis=1)


def _after(dep, kernel, in_specs, args, n_lead=0):
    if dep is None:
        return kernel, list(in_specs), list(args)
    n_in = n_lead + len(in_specs)

    def ordered(*refs):
        return kernel(*refs[:n_in], *refs[n_in + 1:])

    return ordered, list(in_specs) + [pl.BlockSpec(memory_space=pl.ANY)], list(args) + [dep]


def _ada_kernel(c_ref, w_ref, b_ref, o_ref):
    s = _silu(c_ref[...])
    o_ref[...] = jnp.dot(s, w_ref[0], precision=HIGHEST, preferred_element_type=F32) + b_ref[0]


def _ada_mods(cc, w, b, layer):
    rows, d = cc.shape
    depth, _, n = w.shape
    return pl.pallas_call(
        _ada_kernel,
        out_shape=jax.ShapeDtypeStruct((rows, n), F32),
        grid=(n // d,),
        in_specs=[pl.BlockSpec((rows, d), lambda i: (0, 0)),
                  pl.BlockSpec((1, d, d), lambda i: (layer, 0, i)),
                  pl.BlockSpec((1, 1, d), lambda i: (layer, 0, i))],
        out_specs=pl.BlockSpec((rows, d), lambda i: (0, i)),
        compiler_params=pltpu.CompilerParams(dimension_semantics=("parallel",),
                                             vmem_limit_bytes=_vmem_limit(32)),
        name="ada_mods",
    )(cc, w, b.reshape(depth, 1, n))


def _rope128(x, cos, sin, first_half):
    rot = jnp.where(first_half, -pltpu.roll(x, V7X_LANES - 16, axis=1), pltpu.roll(x, 16, axis=1))
    return x * cos + rot * sin


_C_CQ = 0
_C_CKV = _C_CQ + MLA_Q_RANK
_C_QS = _C_CKV + MLA_KV_RANK
_C_KS = _C_QS + SWA_HEADS * SWA_HEAD_DIM
_C_VS = _C_KS + SWA_KV_HEADS * V7X_LANES
_C_KR = _C_VS + SWA_KV_HEADS * V7X_LANES
_C_END = _C_KR + V7X_LANES
_SWA_W = SWA_KV_HEADS * V7X_MXU_DIM
_MLA_QK_W = MLA_HEADS * V7X_MXU_DIM


def _stream_specs(stream, nct, tl):
    ctx_arr, lat_arr, b0, lat_off = stream
    d = ctx_arr.shape[2]
    return [pl.BlockSpec((1, tl, d), lambda i, j: (i + b0, jnp.minimum(j, nct - 1), 0)),
            pl.BlockSpec((1, tl, d), lambda i, j: (i + b0, jnp.maximum(j - nct, 0) + lat_off, 0))]


def _stream_tile(c_ref, x_ref, nct):
    rows = c_ref.shape[1]
    take_ctx = lax.broadcasted_iota(jnp.int32, (rows, 1), 0) < jnp.where(pl.program_id(1) < nct, rows, 0)
    return jnp.where(take_ctx, c_ref[0], x_ref[0])


def _attn_proj_kernel(c_ref, x_ref, mods_ref, g_ref, win_ref, qn_ref, kvn_ref, wuq_ref, wuk_ref, wuvt_ref, cos_ref,
                      sin_ref, q_ref, k_ref, vt_ref, qs_ref, ks_ref, vs_ref, *, nct):
    m = mods_ref[0, 0]
    n = _norm_mod(_stream_tile(c_ref, x_ref, nct), g_ref[...], m[0:1], m[1:2])
    u = _dot(n, win_ref[...])
    cos = cos_ref[...]
    sin = sin_ref[...]
    lane = lax.broadcasted_iota(jnp.int32, (1, V7X_LANES), 1)
    first_half = (lane % 32) < 16

    def rope(x):
        return _rope128(x, cos, sin, first_half)

    scale_a = (MLA_NOPE + MLA_ROPE) ** -0.5 * LOG2E
    scale_b = SWA_HEAD_DIM ** -0.5 * LOG2E
    q = _dot(_rms(u[:, _C_CQ:_C_CKV], qn_ref[...]), wuq_ref[...])
    ckv = _rms(u[:, _C_CKV:_C_QS], kvn_ref[...])
    kn = _dot(ckv, wuk_ref[...])
    vt_ref[0] = _dot_nt(wuvt_ref[...], ckv).astype(BF16)
    kr = rope(u[:, _C_KR:_C_END]).astype(BF16)
    for h in range(MLA_HEADS):
        o = h * V7X_MXU_DIM
        q_ref[0, :, o:o + V7X_LANES] = (q[:, o:o + V7X_LANES] * scale_a).astype(BF16)
        q_ref[0, :, o + V7X_LANES:o + V7X_MXU_DIM] = (rope(q[:, o + V7X_LANES:o + V7X_MXU_DIM]) * scale_a).astype(BF16)
        k_ref[0, :, o:o + V7X_LANES] = kn[:, h * MLA_NOPE:(h + 1) * MLA_NOPE].astype(BF16)
        k_ref[0, :, o + V7X_LANES:o + V7X_MXU_DIM] = kr
    for c in range((_C_KS - _C_QS) // V7X_LANES):
        o = c * V7X_LANES
        qs_ref[0, :, o:o + V7X_LANES] = (rope(u[:, _C_QS + o:_C_QS + o + V7X_LANES]) * scale_b).astype(BF16)
    for g in range(SWA_KV_HEADS):
        o = g * V7X_LANES
        ks = rope(u[:, _C_KS + o:_C_KS + o + V7X_LANES]).astype(BF16)
        vs = u[:, _C_VS + o:_C_VS + o + V7X_LANES].astype(BF16)
        for c in range(V7X_MXU_DIM // V7X_LANES):
            oo = g * V7X_MXU_DIM + c * V7X_LANES
            ks_ref[0, :, oo:oo + V7X_LANES] = ks
            vs_ref[0, :, oo:oo + V7X_LANES] = vs


def _attn_proj(stream, b, l, mods, g, win, qn, kvn, wuq, wuk, wuvt, cos, sin, nct, dep=None):
    d = stream[0].shape[2]
    tl = TOKEN_TILE
    tok = lambda w: pl.BlockSpec((1, tl, w), lambda i, j: (i, j, 0))
    full = lambda a: pl.BlockSpec(a.shape, lambda i, j: (0,) * a.ndim)
    sds = jax.ShapeDtypeStruct
    dv = MLA_HEADS * MLA_V
    kern, in_specs, args = _after(
        dep, functools.partial(_attn_proj_kernel, nct=nct),
        _stream_specs(stream, nct, tl) + [
            pl.BlockSpec((1, 1, N_MODS, d), lambda i, j: (i, jnp.where(j < nct, 0, 1), 0, 0)),
            full(g), full(win), full(qn), full(kvn), full(wuq), full(wuk), full(wuvt),
            pl.BlockSpec((tl, V7X_LANES), lambda i, j: (j, 0)),
            pl.BlockSpec((tl, V7X_LANES), lambda i, j: (j, 0))],
        [stream[0], stream[1], mods, g, win, qn, kvn, wuq, wuk, wuvt, cos, sin])
    return pl.pallas_call(
        kern,
        out_shape=[sds((b, l, _MLA_QK_W), BF16), sds((b, l, _MLA_QK_W), BF16), sds((b, dv, l), BF16),
                   sds((b, l, SWA_HEADS * SWA_HEAD_DIM), BF16), sds((b, l, _SWA_W), BF16), sds((b, l, _SWA_W), BF16)],
        grid=(b, l // tl),
        in_specs=in_specs,
        out_specs=[tok(_MLA_QK_W), tok(_MLA_QK_W), pl.BlockSpec((1, dv, tl), lambda i, j: (i, 0, j)),
                   tok(SWA_HEADS * SWA_HEAD_DIM), tok(_SWA_W), tok(_SWA_W)],
        compiler_params=pltpu.CompilerParams(dimension_semantics=("parallel", "parallel"),
                                             vmem_limit_bytes=_vmem_limit(48)),
        name="attn_proj",
    )(*args)


def _mla_kernel(q_ref, k_ref, vt_ref, o_ref, *, nct_q, lc):
    hw = V7X_MXU_DIM

    def attend(nk):
        st = [_dot_nt(k_ref[0, 0:nk, hh * hw:(hh + 1) * hw], q_ref[0, :, hh * hw:(hh + 1) * hw])
              for hh in range(MLA_HEADS_PER_STEP)]
        for hh, s in enumerate(st):
            p = jnp.exp2(s - jnp.max(s, axis=0, keepdims=True))
            den = jnp.sum(p, axis=0, keepdims=True)
            ot = _dot(vt_ref[0, hh * MLA_V:(hh + 1) * MLA_V, 0:nk], p) / den
            o_ref[0, :, hh * MLA_V:(hh + 1) * MLA_V] = ot.T.astype(o_ref.dtype)

    @pl.when(pl.program_id(2) < nct_q)
    def _():
        attend(lc)

    @pl.when(pl.program_id(2) >= nct_q)
    def _():
        attend(k_ref.shape[1])


def _mla_attention(q, k, vt, lc, q_tile0, dep=None):
    b, l, _ = q.shape
    tq = MLA_Q_TILE
    hps = MLA_HEADS_PER_STEP
    kern, in_specs, args = _after(
        dep, functools.partial(_mla_kernel, nct_q=lc // tq - q_tile0, lc=lc),
        [pl.BlockSpec((1, tq, hps * V7X_MXU_DIM), lambda i, h, j: (i, j + q_tile0, h)),
         pl.BlockSpec((1, l, hps * V7X_MXU_DIM), lambda i, h, j: (i, 0, h)),
         pl.BlockSpec((1, hps * MLA_V, l), lambda i, h, j: (i, h, 0))],
        [q, k, vt])
    return pl.pallas_call(
        kern,
        out_shape=jax.ShapeDtypeStruct((b, l, MLA_HEADS * MLA_V), BF16),
        grid=(b, MLA_HEADS // hps, l // tq - q_tile0),
        in_specs=in_specs,
        out_specs=pl.BlockSpec((1, tq, hps * MLA_V), lambda i, h, j: (i, j + q_tile0, h)),
        compiler_params=pltpu.CompilerParams(dimension_semantics=("parallel", "parallel", "parallel"),
                                             vmem_limit_bytes=_vmem_limit(48)),
        name="mla_attention",
    )(*args)


def _swa_kernel(sink_ref, q_ref, k_ref, v_ref, o_ref, *, lc, q_tile0):
    tq = SWA_Q_TILE
    l = k_ref.shape[1]
    r0 = (pl.program_id(1) + q_tile0) * tq
    start = pl.multiple_of(jnp.clip(r0 - WINDOW, lc, l - SWA_BAND), WINDOW)
    rows = SWA_GROUP * tq
    row = lax.broadcasted_iota(jnp.int32, (rows, 1), 0)
    qpos = jnp.where(r0 >= lc, r0, -l) + row % tq
    kpos = start + lax.broadcasted_iota(jnp.int32, (1, SWA_BAND), 1)
    valid = jnp.abs(qpos - kpos) <= WINDOW
    lane = lax.broadcasted_iota(jnp.int32, (1, V7X_MXU_DIM), 1)
    head = [(lane // SWA_HEAD_DIM) == hh for hh in range(SWA_GROUP)]
    groups = range(SWA_KV_HEADS)
    sls = [slice(g * V7X_MXU_DIM, (g + 1) * V7X_MXU_DIM) for g in groups]
    qstack = []
    for sl in sls:
        qg = q_ref[0, :, sl]
        zero = jnp.zeros_like(qg)
        qstack.append(jnp.concatenate([jnp.where(head[hh], qg, zero) for hh in range(SWA_GROUP)], axis=0))
    sc = [_dot_nt(qstack[g], k_ref[0, 0:lc, sls[g]]) for g in groups]
    sb = [_dot_nt(qstack[g], k_ref[0, pl.ds(start, SWA_BAND), sls[g]]) for g in groups]
    for g in groups:
        sl = sls[g]
        sbm = jnp.where(valid, sb[g], NEG_INF)
        sk = jnp.zeros((rows, 1), F32)
        for hh in range(SWA_GROUP):
            sk = jnp.where(row // tq == hh, sink_ref[g * SWA_GROUP + hh] * LOG2E, sk)
        mx = jnp.maximum(jnp.maximum(jnp.max(sc[g], axis=-1, keepdims=True), jnp.max(sbm, axis=-1, keepdims=True)), sk)
        pc = jnp.exp2(sc[g] - mx)
        pb = jnp.exp2(sbm - mx)
        den = jnp.sum(pc, axis=-1, keepdims=True) + jnp.sum(pb, axis=-1, keepdims=True) + jnp.exp2(sk - mx)
        ostack = (_dot(pc, v_ref[0, 0:lc, sl]) + _dot(pb, v_ref[0, pl.ds(start, SWA_BAND), sl])) / den
        o = jnp.zeros((tq, V7X_MXU_DIM), F32)
        for hh in range(SWA_GROUP):
            o = o + jnp.where(head[hh], ostack[hh * tq:(hh + 1) * tq], 0.0)
        o_ref[0, :, sl] = o.astype(o_ref.dtype)


def _swa_attention(sinks, q, k, v, lc, q_tile0, dep=None):
    b, l, _ = q.shape
    tq = SWA_Q_TILE
    kern, in_specs, args = _after(
        dep, functools.partial(_swa_kernel, lc=lc, q_tile0=q_tile0),
        [pl.BlockSpec(memory_space=pltpu.SMEM),
         pl.BlockSpec((1, tq, SWA_HEADS * SWA_HEAD_DIM), lambda i, j: (i, j + q_tile0, 0)),
         pl.BlockSpec((1, l, _SWA_W), lambda i, j: (i, 0, 0)),
         pl.BlockSpec((1, l, _SWA_W), lambda i, j: (i, 0, 0))],
        [sinks, q, k, v])
    return pl.pallas_call(
        kern,
        out_shape=jax.ShapeDtypeStruct((b, l, SWA_HEADS * SWA_HEAD_DIM), BF16),
        grid=(b, l // tq - q_tile0),
        in_specs=in_specs,
        out_specs=pl.BlockSpec((1, tq, SWA_HEADS * SWA_HEAD_DIM), lambda i, j: (i, j + q_tile0, 0)),
        compiler_params=pltpu.CompilerParams(dimension_semantics=("parallel", "parallel"),
                                             vmem_limit_bytes=_vmem_limit(48)),
        name="swa_attention",
    )(*args)


def _pack_bf16_pair(x):
    w = x.shape[1] // 2
    lo = pltpu.bitcast(x[:, :w].astype(BF16).astype(F32), jnp.int32)
    hi = pltpu.bitcast(x[:, w:].astype(BF16).astype(F32), jnp.int32)
    return lax.shift_right_logical(lo, jnp.int32(16)) | (hi & jnp.int32(-65536))


def _unpack_bf16_pair(p):
    return pltpu.bitcast(p << 16, F32), pltpu.bitcast(p & jnp.int32(-65536), F32)


def _route(n2, wrt, bias, run_ref):
    n_hi = n2.astype(BF16)
    n_lo = (n2 - n_hi.astype(F32)).astype(BF16)
    w_hi = wrt.astype(BF16)
    w_lo = (wrt - w_hi.astype(F32)).astype(BF16)
    w_both = jnp.concatenate([w_hi, w_lo], axis=0)
    rows = n2.shape[0]
    score_blocks = []
    for o in range(0, rows, V7X_LANES):
        both = _dot_nt(w_both, n_hi[o:o + V7X_LANES])
        logits = both[:N_EXPERTS] + both[N_EXPERTS:] + _dot_nt(w_hi, n_lo[o:o + V7X_LANES])
        score_blocks.append(_sigmoid(logits))
    scores = jnp.concatenate(score_blocks, axis=1)

    def select(sc2):
        cols = sc2.shape[1]
        shape3 = (N_GROUPS, GROUP_SIZE, cols)
        choice = sc2.reshape(shape3) + bias
        ji = lax.broadcasted_iota(jnp.int32, shape3, 1).astype(F32)
        m1 = jnp.max(choice, axis=1, keepdims=True)
        first = jnp.min(jnp.where(choice == m1, ji, float(GROUP_SIZE)), axis=1, keepdims=True)
        m2 = jnp.max(jnp.where(ji == first, -jnp.inf, choice), axis=1, keepdims=True)
        gs = m1 + m2
        gidx = lax.broadcasted_iota(jnp.int32, gs.shape, 0).astype(F32)
        gsel = jnp.zeros_like(gs)
        for _ in range(TOPK_GROUPS):
            mx = jnp.max(gs, axis=0, keepdims=True)
            pick = gidx == jnp.min(jnp.where(gs == mx, gidx, float(N_GROUPS)), axis=0, keepdims=True)
            gsel = jnp.where(pick, 1.0, gsel)
            gs = jnp.where(pick, -jnp.inf, gs)
        cand = jnp.where(gsel > 0.0, choice, -jnp.inf).reshape(N_EXPERTS, cols)
        eidx = lax.broadcasted_iota(jnp.int32, (N_EXPERTS, cols), 0).astype(F32)
        out = []
        for _ in range(TOP_K):
            mx = jnp.max(cand, axis=0, keepdims=True)
            pick = eidx == jnp.min(jnp.where(cand == mx, eidx, float(N_EXPERTS)), axis=0, keepdims=True)
            out.append(jnp.where(pick, 1.0, 0.0))
            cand = jnp.where(pick, -jnp.inf, cand)
        return out

    blocks = [select(sc2) for sc2 in score_blocks]
    picks = [jnp.concatenate([blk[k] for blk in blocks], axis=1) > 0.0 for k in range(TOP_K)]
    ei = lax.broadcasted_iota(jnp.int32, (N_EXPERTS, rows), 0).astype(F32)
    esel = jnp.zeros((N_EXPERTS, rows), F32)
    for pick in picks:
        esel = jnp.where(pick, 1.0, esel)
    before = jnp.where(lax.broadcasted_iota(jnp.int32, (rows, rows), 0) < lax.broadcasted_iota(jnp.int32, (rows, rows), 1),
                       1.0, 0.0).astype(BF16)
    slot = jnp.dot(esel.astype(BF16), before, preferred_element_type=F32) + run_ref[...]
    run_ref[...] += jnp.sum(esel, axis=1, keepdims=True)
    sc = [jnp.sum(jnp.where(pick, scores, 0.0), axis=0, keepdims=True) for pick in picks]
    tot = sc[0]
    for x in sc[1:]:
        tot = tot + x
    k8 = lax.broadcasted_iota(jnp.int32, (8, rows), 0)
    kw = lax.broadcasted_iota(jnp.int32, (GATE_W, rows), 0)
    eid = jnp.zeros((8, rows), jnp.int32)
    rank = jnp.zeros((8, rows), jnp.int32)
    wk = jnp.zeros((GATE_W, rows), F32)
    for k, pick in enumerate(picks):
        e_k = jnp.sum(jnp.where(pick, ei, 0.0), axis=0, keepdims=True).astype(jnp.int32)
        r_k = jnp.sum(jnp.where(pick, slot, 0.0), axis=0, keepdims=True).astype(jnp.int32)
        eid = jnp.where(k8 == k, e_k, eid)
        rank = jnp.where(k8 == k, r_k, rank)
        wk = jnp.where(kw == k, sc[k] * (ROUTED_SCALE / tot), wk)
    return eid, rank, wk.T


def _mixer_tail(o, h, m, gffn_ref, wrt_ref, bias_ref, hn_ref, n2_ref, eid_ref, rank_ref, w_ref, cnt_ref, run_ref):
    @pl.when((pl.program_id(0) == 0) & (pl.program_id(1) == 0))
    def _():
        run_ref[...] = jnp.zeros_like(run_ref)

    hn = h + m[2:3] * o
    hn_ref[0] = hn
    n2 = _norm_mod(hn, gffn_ref[...], m[3:4], m[4:5])
    n2_ref[0] = _pack_bf16_pair(n2)
    eid, rank, wcols = _route(n2, wrt_ref[...], bias_ref[...], run_ref)
    eid_ref[0] = eid
    rank_ref[0] = rank
    w_ref[0] = wcols
    cnt_ref[...] = run_ref[...]


def _attn_out_kernel(a_ref, b_ref, c_ref, x_ref, mods_ref, wo_ref, gffn_ref, wrt_ref, bias_ref,
                     hn_ref, n2_ref, eid_ref, rank_ref, w_ref, cnt_ref, run_ref, *, nct):
    wa = MLA_HEADS * MLA_V
    o = _dot(a_ref[0], wo_ref[0:wa, :]) + _dot(b_ref[0], wo_ref[wa:, :])
    _mixer_tail(o, _stream_tile(c_ref, x_ref, nct), mods_ref[0, 0], gffn_ref, wrt_ref, bias_ref, hn_ref, n2_ref,
                eid_ref, rank_ref, w_ref, cnt_ref, run_ref)


def _tail_outs(b, l, d):
    tl = TOKEN_TILE
    nt = l // tl
    sds = jax.ShapeDtypeStruct
    tok = lambda w: pl.BlockSpec((1, tl, w), lambda i, j: (i, j, 0))
    blk = pl.BlockSpec((1, 8, tl), lambda i, j: (i * nt + j, 0, 0))
    shapes = [sds((b, l, d), F32), sds((b, l, d // 2), jnp.int32), sds((b * nt, 8, tl), jnp.int32),
              sds((b * nt, 8, tl), jnp.int32), sds((b, l, GATE_W), F32), sds((N_EXPERTS, 1), F32)]
    specs = [tok(d), tok(d // 2), blk, blk, tok(GATE_W), pl.BlockSpec((N_EXPERTS, 1), lambda i, j: (0, 0))]
    return shapes, specs


def _attn_out(a, bm, stream, mods, wo, gffn, wrt, bias, nct, dep=None):
    b, l, _ = a.shape
    d = stream[0].shape[2]
    tl = TOKEN_TILE
    tok = lambda w: pl.BlockSpec((1, tl, w), lambda i, j: (i, j, 0))
    full = lambda x: pl.BlockSpec(x.shape, lambda i, j: (0,) * x.ndim)
    shapes, specs = _tail_outs(b, l, d)
    kern, in_specs, args = _after(
        dep, functools.partial(_attn_out_kernel, nct=nct),
        [tok(a.shape[2]), tok(bm.shape[2])] + _stream_specs(stream, nct, tl) + [
            pl.BlockSpec((1, 1, N_MODS, d), lambda i, j: (i, jnp.where(j < nct, 0, 1), 0, 0)),
            full(wo), full(gffn), full(wrt), full(bias)],
        [a, bm, stream[0], stream[1], mods, wo, gffn, wrt, bias])
    return pl.pallas_call(
        kern,
        out_shape=shapes,
        grid=(b, l // tl),
        in_specs=in_specs,
        out_specs=specs,
        scratch_shapes=[pltpu.VMEM((N_EXPERTS, 1), F32)],
        compiler_params=pltpu.CompilerParams(dimension_semantics=("arbitrary", "arbitrary"),
                                             vmem_limit_bytes=_vmem_limit(40)),
        name="attn_out",
    )(*args)


def _moe_dest_kernel(off_ref, eid_ref, rank_ref, dest_ref):
    eid = eid_ref[...]
    dest = rank_ref[...]
    for e in range(N_EXPERTS):
        dest = dest + jnp.where(eid == e, off_ref[e], 0)
    dest_ref[...] = dest


def _moe_dest(off, eid, rank):
    return pl.pallas_call(
        _moe_dest_kernel,
        out_shape=jax.ShapeDtypeStruct(eid.shape, jnp.int32),
        in_specs=[pl.BlockSpec(memory_space=pltpu.SMEM),
                  pl.BlockSpec(eid.shape, lambda: (0, 0, 0)), pl.BlockSpec(eid.shape, lambda: (0, 0, 0))],
        out_specs=pl.BlockSpec(eid.shape, lambda: (0, 0, 0)),
        name="moe_dest",
    )(off, eid, rank)


def _sc_mesh():
    return plsc.VectorSubcoreMesh(core_axis_name="c", subcore_axis_name="s",
                                  num_cores=V7X_SC_CORES, num_subcores=V7X_SC_SUBCORES)


def _sc_chunk(rows_per_worker):
    return max(c for c in range(8, SC_MAX_CHUNK + 1, 8) if rows_per_worker % c == 0)


def _sc_dispatch(xp, dest, p_rows):
    t, w = xp.shape
    tpw = t // V7X_SC_WORKERS
    ch = _sc_chunk(tpw)

    @functools.partial(
        pl.kernel, mesh=_sc_mesh(), out_type=jax.ShapeDtypeStruct((p_rows, w), xp.dtype),
        scratch_types=[pltpu.VMEM((ch, w), xp.dtype)] + [pltpu.VMEM((ch,), jnp.int32)] * TOP_K
        + [pltpu.SemaphoreType.DMA, pltpu.SemaphoreType.DMA],
        name="moe_dispatch")
    def run(x_hbm, dest_hbm, out_hbm, rows_v, *rest):
        idx, (sem_i, sem_o) = rest[:TOP_K], rest[TOP_K:]
        base = (lax.axis_index("s") * V7X_SC_CORES + lax.axis_index("c")) * tpw

        @pl.loop(0, tpw // ch)
        def _(i):
            t0 = base + i * ch
            loads = [pltpu.async_copy(dest_hbm.at[k, pl.ds(t0, ch)], idx[k], sem_i) for k in range(TOP_K)]
            pltpu.sync_copy(x_hbm.at[pl.ds(t0, ch)], rows_v)
            for c in loads:
                c.wait()
            puts = [pltpu.async_copy(rows_v, out_hbm.at[idx[k]], sem_o) for k in range(TOP_K)]
            for c in puts:
                c.wait()

    return run(xp, dest)


def _sc_gather(ys, dest, t):
    w = ys.shape[1]
    tpw = t // V7X_SC_WORKERS
    ch = _sc_chunk(tpw)

    @functools.partial(
        pl.kernel, mesh=_sc_mesh(), out_type=jax.ShapeDtypeStruct((TOP_K, t, w), ys.dtype),
        scratch_types=[pltpu.VMEM((ch, w), ys.dtype)] * 2 + [pltpu.VMEM((ch,), jnp.int32)] * TOP_K
        + [pltpu.SemaphoreType.DMA] * 5,
        name="moe_gather")
    def run(y_hbm, dest_hbm, out_hbm, rows_a, rows_b, *rest):
        idx, (sem_i, sem_ga, sem_gb, sem_wa, sem_wb) = rest[:TOP_K], rest[TOP_K:]
        rows, sem_g, sem_w = (rows_a, rows_b), (sem_ga, sem_gb), (sem_wa, sem_wb)
        base = (lax.axis_index("s") * V7X_SC_CORES + lax.axis_index("c")) * tpw

        @pl.loop(0, tpw // ch)
        def _(i):
            t0 = base + i * ch
            loads = [pltpu.async_copy(dest_hbm.at[k, pl.ds(t0, ch)], idx[k], sem_i) for k in range(TOP_K)]
            for c in loads:
                c.wait()
            gets, puts = [None] * TOP_K, [None] * TOP_K
            gets[0] = pltpu.async_copy(y_hbm.at[idx[0]], rows[0], sem_g[0])
            for k in range(TOP_K):
                if k + 1 < TOP_K:
                    if k >= 1:
                        puts[k - 1].wait()
                    gets[k + 1] = pltpu.async_copy(y_hbm.at[idx[k + 1]], rows[(k + 1) % 2], sem_g[(k + 1) % 2])
                gets[k].wait()
                puts[k] = pltpu.async_copy(rows[k % 2], out_hbm.at[k, pl.ds(t0, ch)], sem_w[k % 2])
            puts[TOP_K - 2].wait()
            puts[TOP_K - 1].wait()

    return run(ys, dest)


def _cache_mlp_weights(wg, wu, wd, wgu_ref, wdb_ref):
    f = wg.shape[1]
    wgu_ref[:, 0:f] = wg.astype(BF16)
    wgu_ref[:, f:] = wu.astype(BF16)
    wdb_ref[...] = wd.astype(BF16)


def _gated_mlp(xp, wgu_ref, wdb_ref):
    lo, hi = _unpack_bf16_pair(xp)
    x = jnp.concatenate([lo.astype(BF16), hi.astype(BF16)], axis=1)
    gu = jnp.dot(x, wgu_ref[...], preferred_element_type=F32)
    f = gu.shape[1] // 2
    return _dot(_silu(gu[:, :f]) * gu[:, f:], wdb_ref[...])


def _moe_expert_kernel(te_ref, tb_ref, nv_ref, x_hbm, wga_ref, wua_ref, wda_ref, wgb_ref, wub_ref, wdb_ref, y_ref,
                       gu_a, dn_a, gu_b, dn_b, ids_ref, xbuf, sems):
    i = pl.program_id(0)
    tm = MOE_ROW_TILE
    nv = nv_ref[0]
    last = (nv - 1) // 2
    first = 2 * jnp.minimum(i, last)
    ea = te_ref[first]
    eb = te_ref[first + 1]
    two = 2 * i + 1 < nv

    def rows_copy(step):
        slot = step % MOE_ROW_SLOTS
        row0 = step * (2 * tm)
        rows = pl.ds(row0 if isinstance(step, int) else pl.multiple_of(row0, 2 * tm), 2 * tm)
        return pltpu.make_async_copy(x_hbm.at[rows], xbuf.at[slot], sems.at[slot])

    @pl.when(i == 0)
    def _():
        ids_ref[0] = -1
        ids_ref[1] = -1
        for ahead in range(MOE_ROW_SLOTS - 1):
            @pl.when(ahead <= last)
            def _():
                rows_copy(ahead).start()

    @pl.when(i + (MOE_ROW_SLOTS - 1) <= last)
    def _():
        rows_copy(i + (MOE_ROW_SLOTS - 1)).start()

    @pl.when(i <= last)
    def _():
        rows_copy(i).wait()

    x_ref = xbuf.at[i % MOE_ROW_SLOTS]

    @pl.when(ids_ref[0] != ea)
    def _():
        _cache_mlp_weights(wga_ref[0, 0], wua_ref[0, 0], wda_ref[0, 0], gu_a, dn_a)
        ids_ref[0] = ea

    @pl.when(two & (eb != ea) & (ids_ref[1] != eb))
    def _():
        _cache_mlp_weights(wgb_ref[0, 0], wub_ref[0, 0], wdb_ref[0, 0], gu_b, dn_b)
        ids_ref[1] = eb

    @pl.when(two & (eb == ea))
    def _():
        y_ref[...] = _pack_bf16_pair(_gated_mlp(x_ref[...], gu_a, dn_a))

    @pl.when((2 * i < nv) & jnp.logical_not(two & (eb == ea)))
    def _():
        y_ref[0:tm, :] = _pack_bf16_pair(_gated_mlp(x_ref[0:tm, :], gu_a, dn_a))

    @pl.when(two & (eb != ea))
    def _():
        y_ref[tm:, :] = _pack_bf16_pair(_gated_mlp(x_ref[tm:, :], gu_b, dn_b))


def _moe_experts(tile_expert, n_valid, xs, wg, wu, wd, layer, dep=None):
    p_rows, w = xs.shape
    tm = MOE_ROW_TILE
    _, _, d, f = wg.shape
    npair = p_rows // (2 * tm)
    pairs = tile_expert.reshape(npair, 2)
    tile_b = jnp.maximum(lax.cummax(jnp.where(pairs[:, 1] != pairs[:, 0], pairs[:, 1], -1)), 0)
    step = lambda i, nv: jnp.minimum(i, (nv[0] - 1) // 2)
    spec_a = lambda shp: pl.BlockSpec((1, 1) + shp, lambda i, te, tb, nv: (layer, te[2 * step(i, nv)], 0, 0))
    spec_b = lambda shp: pl.BlockSpec((1, 1) + shp, lambda i, te, tb, nv: (layer, tb[step(i, nv)], 0, 0))
    rows = pl.BlockSpec((2 * tm, w), lambda i, te, tb, nv: (step(i, nv), 0))
    kern, in_specs, args = _after(
        dep, _moe_expert_kernel,
        [pl.BlockSpec(memory_space=pl.ANY), spec_a((d, f)), spec_a((d, f)), spec_a((f, d)),
         spec_b((d, f)), spec_b((d, f)), spec_b((f, d))],
        [tile_expert, tile_b, n_valid, xs, wg, wu, wd, wg, wu, wd], n_lead=3)
    return pl.pallas_call(
        kern,
        out_shape=jax.ShapeDtypeStruct((p_rows, w), xs.dtype),
        grid_spec=pltpu.PrefetchScalarGridSpec(
            num_scalar_prefetch=3, grid=(npair,),
            in_specs=in_specs,
            out_specs=rows,
            scratch_shapes=[pltpu.VMEM((d, 2 * f), BF16), pltpu.VMEM((f, d), BF16),
                            pltpu.VMEM((d, 2 * f), BF16), pltpu.VMEM((f, d), BF16), pltpu.SMEM((2,), jnp.int32),
                            pltpu.VMEM((MOE_ROW_SLOTS, 2 * tm, w), xs.dtype),
                            pltpu.SemaphoreType.DMA((MOE_ROW_SLOTS,))]),
        compiler_params=pltpu.CompilerParams(dimension_semantics=("arbitrary",),
                                             vmem_limit_bytes=_vmem_limit(48)),
        name="moe_experts",
    )(*args)


def _moe_combine_kernel(yg_hbm, w_ref, xp_ref, sg_ref, su_ref, sd_ref, h_ref, mods_ref, gfin_ref, *rest, final_norm,
                        tile0):
    o_ref, wgu_ref, wdb_ref, ybuf, sems = rest[-5:]
    tl = ybuf.shape[2]
    nt = pl.num_programs(1)
    step = pl.program_id(0) * nt + pl.program_id(1)
    n_steps = pl.num_programs(0) * nt

    def yg_copy(s):
        rows = pl.ds(pl.multiple_of((s % nt + tile0) * tl, tl), tl)
        return pltpu.make_async_copy(yg_hbm.at[:, s // nt, rows, :], ybuf.at[s % MOE_ROW_SLOTS],
                                     sems.at[s % MOE_ROW_SLOTS])

    @pl.when(step == 0)
    def _():
        _cache_mlp_weights(sg_ref[0], su_ref[0], sd_ref[0], wgu_ref, wdb_ref)
        for ahead in range(MOE_ROW_SLOTS - 1):
            @pl.when(ahead < n_steps)
            def _():
                yg_copy(ahead).start()

    @pl.when(step + (MOE_ROW_SLOTS - 1) < n_steps)
    def _():
        yg_copy(step + (MOE_ROW_SLOTS - 1)).start()

    yg_copy(step).wait()
    yg_ref = ybuf.at[step % MOE_ROW_SLOTS]

    acc = _gated_mlp(xp_ref[0], wgu_ref, wdb_ref)
    half = acc.shape[1] // 2
    lo = acc[:, :half]
    hi = acc[:, half:]
    w = w_ref[0]
    for k in range(TOP_K):
        ylo, yhi = _unpack_bf16_pair(yg_ref[k])
        wk = w[:, k:k + 1]
        lo = lo + wk * ylo
        hi = hi + wk * yhi
    y = h_ref[0] + mods_ref[0, 0, N_MODS - 1:N_MODS, :] * jnp.concatenate([lo, hi], axis=1)
    if final_norm:
        y = _rms(y, gfin_ref[...])
    o_ref[0] = y


def _moe_combine(yg, wcols, xp, sg, su, sd, h, mods, gfin, nct, layer, out_buf, out_b0, out_batch, latent_only,
                 final_norm, in_b0=0, dep=None):
    b = yg.shape[1]
    _, l, d = h.shape
    tl = TOKEN_TILE
    tile0 = nct if latent_only else 0
    tok = lambda w: pl.BlockSpec((1, tl, w), lambda i, j: (i + in_b0, j + tile0, 0))
    lay = lambda x: pl.BlockSpec((1,) + x.shape[1:], lambda i, j: (layer,) + (0,) * (x.ndim - 1))
    args = [yg, wcols, xp, sg, su, sd, h, mods, gfin]
    in_specs = [pl.BlockSpec(memory_space=pl.ANY), tok(GATE_W), tok(d // 2),
                lay(sg), lay(su), lay(sd), tok(d),
                pl.BlockSpec((1, 1, N_MODS, d), lambda i, j: (i + in_b0, jnp.where(j + tile0 < nct, 0, 1), 0, 0)),
                pl.BlockSpec(gfin.shape, lambda i, j: (0, 0))]
    _, in_specs, args = _after(dep, None, in_specs, args)
    aliases = {}
    if out_buf is not None:
        args.append(out_buf)
        in_specs.append(pl.BlockSpec(memory_space=pl.ANY))
        aliases = {len(args) - 1: 0}
    return pl.pallas_call(
        functools.partial(_moe_combine_kernel, final_norm=final_norm, tile0=tile0),
        out_shape=jax.ShapeDtypeStruct((out_batch, l - tile0 * tl, d), F32),
        grid=(b, l // tl - tile0),
        in_specs=in_specs,
        out_specs=pl.BlockSpec((1, tl, d), lambda i, j: (i + out_b0, j, 0)),
        scratch_shapes=[pltpu.VMEM((d, 2 * sg.shape[2]), BF16), pltpu.VMEM((sg.shape[2], d), BF16),
                        pltpu.VMEM((MOE_ROW_SLOTS, TOP_K, tl, d // 2), yg.dtype),
                        pltpu.SemaphoreType.DMA((MOE_ROW_SLOTS,))],
        input_output_aliases=aliases,
        compiler_params=pltpu.CompilerParams(dimension_semantics=("arbitrary", "arbitrary"),
                                             vmem_limit_bytes=_vmem_limit(40)),
        name="moe_combine",
    )(*args)


def _moe_route_rows(n2p, eid, rank, counts, b, l):
    d2 = n2p.shape[2]
    t = b * l
    tm = MOE_ROW_TILE
    n_tiles = 2 * -(-(TOP_K * t + N_EXPERTS * (tm - 1)) // (2 * tm))
    tiles_e = (counts.reshape(N_EXPERTS).astype(jnp.int32) + (tm - 1)) // tm
    tile_end = jnp.cumsum(tiles_e)
    off = (tile_end - tiles_e) * tm
    n_valid = tile_end[-1:]
    tile_id = jnp.minimum(jnp.arange(n_tiles, dtype=jnp.int32), n_valid - 1)
    tile_expert = jnp.sum((tile_end[None, :] <= tile_id[:, None]).astype(jnp.int32), axis=1)
    dest = _moe_dest(off, eid, rank).transpose(1, 0, 2).reshape(8, t)
    xs = _sc_dispatch(n2p.reshape(t, d2), dest, n_tiles * tm)
    return xs, dest, tile_expert, n_valid


def _rwkv_proj_kernel(h_ref, hp_ref, hx_ref, mods_ref, g_ref, mu_ref, wr_ref, wk_ref, wv_ref, g1_ref, g2_ref,
                      w1_ref, w2_ref, a1_ref, a2_ref, w0_ref, a0_ref, kk_ref, ka_ref, rk_ref, bd_ref,
                      r_out, v_out, kk_out, g_out, km_out, b_out, lw_out, bonus_out, *, nct):
    j = pl.program_id(1)
    nt = pl.num_programs(1)
    m = mods_ref[0, 0]
    g = g_ref[...]
    n = _norm_mod(h_ref[0], g, m[0:1], m[1:2])
    tl, d = n.shape
    seg_first = (j == 0) | (j == nct)
    seg_last = (j == nct - 1) | (j == nt - 1)
    n_prev = _norm_mod(hp_ref[0], g, m[0:1], m[1:2])[7:8] * jnp.where(seg_first, 0.0, 1.0)
    n_next = _norm_mod(hx_ref[0], g, m[0:1], m[1:2])[0:1] * jnp.where(seg_last, 0.0, 1.0)
    row = lax.broadcasted_iota(jnp.int32, (tl, 1), 0)
    prev = jnp.where(row == 0, n_prev, pltpu.roll(n, 1, axis=0))
    nxt = jnp.where(row == tl - 1, n_next, pltpu.roll(n, tl - 1, axis=0))
    lane = lax.broadcasted_iota(jnp.int32, (1, d), 1)
    xx = jnp.where(lane < d // 2, prev, nxt) - n
    mu = mu_ref[...]
    bd = bd_ref[...]
    halves = [slice(0, tl // 2), slice(tl // 2, tl)]
    first = []
    for rs in halves:
        nh, xh = n[rs], xx[rs]
        xr, xw, xk, xv, xa, xg = [nh + xh * mu[i:i + 1] for i in range(6)]
        first.append((_dot(xr, wr_ref[...]), _dot(xk, wk_ref[...]), _dot(xv, wv_ref[...]),
                      _dot(xg, g1_ref[...]), _dot(xw, w1_ref[...]), _dot(xa, a1_ref[...])))
    second = []
    for r, k, v, gq, tq, ta in first:
        tw = jnp.tanh(tq)
        kk = k * kk_ref[...]
        second.append((_dot(_sigmoid(gq), g2_ref[...]), [_dot(tw, w2_ref[dr]) for dr in range(2)],
                       [_dot(ta, a2_ref[dr]) for dr in range(2)], kk, _head_sum(kk * kk, bd)))
    for rs, (r, k, v, _, _, _), (gate, zw, za, kk, kk_sq) in zip(halves, first, second):
        kk = kk / jnp.maximum(jnp.sqrt(kk_sq), 1e-12)
        g_out[0, rs, :] = gate.astype(g_out.dtype)
        r_out[0, rs, :] = r.astype(r_out.dtype)
        v_out[0, rs, :] = v.astype(v_out.dtype)
        kk_out[0, rs, :] = kk.astype(kk_out.dtype)
        bonus = jnp.zeros_like(v)
        for dr in range(2):
            lw_out[dr, 0, rs, :] = -jnp.exp(-0.5) * _sigmoid(w0_ref[dr:dr + 1, :] + zw[dr])
            a = _sigmoid(a0_ref[dr:dr + 1, :] + za[dr])
            km = k * (1.0 + (a - 1.0) * ka_ref[...])
            km_out[dr, 0, rs, :] = km.astype(km_out.dtype)
            b_out[dr, 0, rs, :] = (kk * a).astype(b_out.dtype)
            bonus = bonus + _head_sum(r * km * rk_ref[...], bd) * v
        bonus_out[0, rs, :] = bonus.astype(bonus_out.dtype)


def _rwkv_proj(h, mods, g, mu, wr, wk, wv, g1, g2, w1, w2, a1, a2, w0, a0, kk, ka, rk, bd, nct, dep=None):
    b, l, d = h.shape
    tl = TOKEN_TILE
    nb8 = l // 8
    tok = pl.BlockSpec((1, tl, d), lambda i, j: (i, j, 0))
    tok2 = pl.BlockSpec((2, 1, tl, d), lambda i, j: (0, i, j, 0))
    full = lambda x: pl.BlockSpec(x.shape, lambda i, j: (0,) * x.ndim)
    sds = jax.ShapeDtypeStruct
    kern, in_specs, args = _after(
        dep, functools.partial(_rwkv_proj_kernel, nct=nct),
        [tok,
         pl.BlockSpec((1, 8, d), lambda i, j: (i, jnp.maximum(j * (tl // 8) - 1, 0), 0)),
         pl.BlockSpec((1, 8, d), lambda i, j: (i, jnp.minimum((j + 1) * (tl // 8), nb8 - 1), 0)),
         pl.BlockSpec((1, 1, N_MODS, d), lambda i, j: (i, jnp.where(j < nct, 0, 1), 0, 0)),
         full(g), full(mu), full(wr), full(wk), full(wv), full(g1), full(g2), full(w1), full(w2),
         full(a1), full(a2), full(w0), full(a0), full(kk), full(ka), full(rk), full(bd)],
        [h, h, h, mods, g, mu, wr, wk, wv, g1, g2, w1, w2, a1, a2, w0, a0, kk, ka, rk, bd])
    return pl.pallas_call(
        kern,
        out_shape=[sds((b, l, d), BF16), sds((b, l, d), BF16), sds((b, l, d), BF16), sds((b, l, d), BF16),
                   sds((2, b, l, d), BF16), sds((2, b, l, d), BF16), sds((2, b, l, d), F32), sds((b, l, d), BF16)],
        grid=(b, l // tl),
        in_specs=in_specs,
        out_specs=[tok, tok, tok, tok, tok2, tok2, tok2, tok],
        compiler_params=pltpu.CompilerParams(dimension_semantics=("parallel", "parallel"),
                                             vmem_limit_bytes=_vmem_limit(56)),
        name="rwkv_proj",
    )(*args)


def _wkv_kernel(r_ref, v_ref, kk_ref, km_ref, b_ref, lw_ref, y_ref, st_ref):
    c = WKV_CHUNK
    w = WKV_PAIR
    rev = pl.program_id(0)
    sign = 1 - 2 * rev

    @pl.when(pl.program_id(2) == 0)
    def _():
        st_ref[...] = jnp.zeros_like(st_ref)

    ti = lax.broadcasted_iota(jnp.int32, (c, c), 0)
    si = lax.broadcasted_iota(jnp.int32, (c, c), 1)
    tri = jnp.where((si - ti) * sign <= 0, 1.0, 0.0).astype(BF16)
    tri3 = jnp.concatenate([tri, tri, tri], axis=1)
    nsub = WKV_CHUNKS_PER_STEP
    subs = [pl.ds(pl.multiple_of(jnp.where(rev == 0, s, nsub - 1 - s) * c, c), c) for s in range(nsub)]
    rt, kt, kh, bh, v32, e_mid = [], [], [], [], [], []
    for rows in subs:
        lw = lw_ref[0, 0, rows, :]
        t1 = lw.astype(BF16)
        d1 = lw - t1.astype(F32)
        t2 = d1.astype(BF16)
        t3 = (d1 - t2.astype(F32)).astype(BF16)
        l_incl = jnp.dot(tri3, jnp.concatenate([t1, t2, t3], axis=0), preferred_element_type=F32)
        mid = 0.5 * jnp.sum(lw, axis=0, keepdims=True)
        e_neg = jnp.exp(mid - l_incl)
        e_mid.append(jnp.exp(mid))
        rt.append(r_ref[0, rows, :].astype(F32) * jnp.exp(l_incl - mid))
        kt.append(kk_ref[0, rows, :].astype(F32) * jnp.exp(l_incl - lw - mid))
        kh.append(km_ref[0, 0, rows, :].astype(F32) * e_neg)
        bh.append(b_ref[0, 0, rows, :].astype(F32) * e_neg)
        v32.append(v_ref[0, rows, :].astype(F32))

    ri = lax.broadcasted_iota(jnp.int32, (w, w), 0)
    ci = lax.broadcasted_iota(jnp.int32, (w, w), 1)
    same = (ri // c) == (ci // c)
    eye = jnp.where(ri == ci, 1.0, 0.0).astype(F32)
    tl_ = lax.broadcasted_iota(jnp.int32, (c, w), 0)
    jl_ = lax.broadcasted_iota(jnp.int32, (c, w), 1) % c
    strict = (jl_ - tl_) * sign < 0
    incl = (jl_ - tl_) * sign <= 0
    eye2 = jnp.where(jl_ == tl_, 1.0, 0.0).astype(F32)
    lane = lax.broadcasted_iota(jnp.int32, (1, w), 1)
    h0 = lane < RWKV_HEAD

    def rows2(x):
        return jnp.concatenate([jnp.where(h0, x, 0.0), jnp.where(h0, 0.0, x)], axis=0)

    npair = st_ref.shape[0]
    items = [(s, slice(p * w, (p + 1) * w)) for s in range(nsub) for p in range(npair)]
    n = range(len(items))
    em = [e_mid[s][:, sl] for s, sl in items]
    g = [_dot_nt(jnp.concatenate([kt[s][:, sl], rt[s][:, sl]], axis=0),
                 jnp.concatenate([rows2(kh[s][:, sl]), rows2(bh[s][:, sl])], axis=0)) for s, sl in items]
    a_kk = [jnp.where(strict, x[:c, :w], 0.0) for x in g]
    a_rk = [jnp.where(incl, x[c:, :w], 0.0) for x in g]
    a_rb = [jnp.where(incl, x[c:, w:], 0.0) for x in g]
    vi = [v32[s][:, sl] for s, sl in items]
    v_rows = [rows2(x) for x in vi]
    av = [_dot(jnp.concatenate([a_kk[i], a_rk[i]], axis=0), v_rows[i]) for i in n]
    r_pre = [x[:c] for x in av]
    ark_v = [x[c:] for x in av]
    m = [jnp.where(strict, -x[:c, w:], 0.0) for x in g]
    tinv = [eye2 + x for x in m]
    m = [_dot(x, rows2(x)) for x in m]
    for _ in range(c.bit_length() - 3):
        both = [_dot(jnp.concatenate([tinv[i], m[i]], axis=0), rows2(m[i])) for i in n]
        tinv = [tinv[i] + both[i][:c] for i in n]
        m = [x[c:] for x in both]
    tinv = [tinv[i] + _dot(tinv[i], rows2(m[i])) for i in n]
    sol = [_dot(tinv[i], jnp.concatenate([rows2(r_pre[i]), rows2(kt[s][:, sl] * em[i])], axis=1))
           for i, (s, sl) in enumerate(items)]
    u_pre = [x[:, :w] for x in sol]
    kq = [x[:, w:] for x in sol]
    arb = [_dot(a_rb[i], jnp.concatenate([rows2(u_pre[i]), rows2(kq[i])], axis=1)) for i in n]
    y_pre = [ark_v[i] - arb[i][:, :w] for i in n]
    r_eff = [rt[s][:, sl] * em[i] - arb[i][:, w:] for i, (s, sl) in enumerate(items)]
    bbar = [bh[s][:, sl] * em[i] for i, (s, sl) in enumerate(items)]
    kbar = [kh[s][:, sl] * em[i] for i, (s, sl) in enumerate(items)]
    mmat = [eye * (em[i] * em[i]) - jnp.where(same, _dot_tn(kq[i], bbar[i]), 0.0) for i in n]
    s_pre = [jnp.where(same, _dot_tn(jnp.concatenate([vi[i], -u_pre[i]], axis=0),
                                     jnp.concatenate([kbar[i], bbar[i]], axis=0)), 0.0) for i in n]
    st = [st_ref[p] for p in range(npair)]
    for i, (s, sl) in enumerate(items):
        p = i % npair
        y_ref[0, 0, subs[s], sl] = (_dot_nt(r_eff[i], st[p]) + y_pre[i]).astype(y_ref.dtype)
        hi = st[p].astype(BF16)
        lo = (st[p] - hi.astype(F32)).astype(BF16)
        mb = mmat[i].astype(BF16)
        both = jnp.dot(jnp.concatenate([hi, lo], axis=0), mb, preferred_element_type=F32)
        st[p] = both[:w] + both[w:] + s_pre[i]
    for p in range(npair):
        st_ref[p] = st[p]


def _wkv(r, v, kk, km, bv, lw, lc, dep=None):
    b, l, d = r.shape
    c = WKV_CHUNK * WKV_CHUNKS_PER_STEP
    ncc = lc // c
    nlc = (l - lc) // c

    def chunk(dr, i):
        return jnp.where(dr == 0, i, jnp.where(i < ncc, ncc - 1 - i, nlc + 2 * ncc - 1 - i))

    shared = pl.BlockSpec((1, c, d), lambda dr, bi, i: (bi, chunk(dr, i), 0))
    per_dir = pl.BlockSpec((1, 1, c, d), lambda dr, bi, i: (dr, bi, chunk(dr, i), 0))
    kern, in_specs, args = _after(dep, _wkv_kernel, [shared, shared, shared, per_dir, per_dir, per_dir],
                                  [r, v, kk, km, bv, lw])
    return pl.pallas_call(
        kern,
        out_shape=jax.ShapeDtypeStruct((2, b, l, d), BF16),
        grid=(2, b, l // c),
        in_specs=in_specs,
        out_specs=per_dir,
        scratch_shapes=[pltpu.VMEM((d // WKV_PAIR, WKV_PAIR, WKV_PAIR), F32)],
        compiler_params=pltpu.CompilerParams(dimension_semantics=("parallel", "parallel", "arbitrary"),
                                             vmem_limit_bytes=_vmem_limit(32)),
        name="wkv7_chunked",
    )(*args)


def _rwkv_out_kernel(y_ref, bonus_ref, g_ref, lnw_ref, lnb_ref, wo_ref, bd_ref, h_ref, mods_ref, gffn_ref,
                     wrt_ref, bias_ref, hn_ref, n2_ref, eid_ref, rank_ref, w_ref, cnt_ref, run_ref):
    y = y_ref[0, 0].astype(F32) + y_ref[1, 0].astype(F32)
    bd = bd_ref[...]
    mean = _head_sum(y, bd) * (1.0 / RWKV_HEAD)
    yc = y - mean
    var = _head_sum(yc * yc, bd) * (1.0 / RWKV_HEAD)
    yn = yc * lax.rsqrt(var + GN_EPS) * lnw_ref[...] + lnb_ref[...]
    out = (yn + bonus_ref[0].astype(F32)) * g_ref[0].astype(F32)
    _mixer_tail(_dot(out, wo_ref[...]), h_ref[0], mods_ref[0, 0], gffn_ref, wrt_ref, bias_ref, hn_ref, n2_ref,
                eid_ref, rank_ref, w_ref, cnt_ref, run_ref)


def _rwkv_out(y, bonus, g, lnw, lnb, wo, bd, h, mods, gffn, wrt, bias, nct, dep=None):
    b, l, d = h.shape
    tl = TOKEN_TILE
    tok = lambda w: pl.BlockSpec((1, tl, w), lambda i, j: (i, j, 0))
    full = lambda x: pl.BlockSpec(x.shape, lambda i, j: (0,) * x.ndim)
    shapes, specs = _tail_outs(b, l, d)
    kern, in_specs, args = _after(
        dep, _rwkv_out_kernel,
        [pl.BlockSpec((2, 1, tl, d), lambda i, j: (0, i, j, 0)), tok(d), tok(d),
         full(lnw), full(lnb), full(wo), full(bd), tok(d),
         pl.BlockSpec((1, 1, N_MODS, d), lambda i, j: (i, jnp.where(j < nct, 0, 1), 0, 0)),
         full(gffn), full(wrt), full(bias)],
        [y, bonus, g, lnw, lnb, wo, bd, h, mods, gffn, wrt, bias])
    return pl.pallas_call(
        kern,
        out_shape=shapes,
        grid=(b, l // tl),
        in_specs=in_specs,
        out_specs=specs,
        scratch_shapes=[pltpu.VMEM((N_EXPERTS, 1), F32)],
        compiler_params=pltpu.CompilerParams(dimension_semantics=("arbitrary", "arbitrary"),
                                             vmem_limit_bytes=_vmem_limit(40)),
        name="rwkv_out",
    )(*args)


def _rope_table(n_lat, n_ctx):
    dim = SWA_HEAD_DIM
    nf = dim // 4
    inv = ROPE_THETA ** (-jnp.arange(nf, dtype=F32) / nf)
    row = jnp.repeat(jnp.arange(n_lat // GRID_W, dtype=F32), GRID_W)
    col = jnp.tile(jnp.arange(GRID_W, dtype=F32), n_lat // GRID_W)
    ar = row[:, None] * inv
    ac = col[:, None] * inv
    ang = jnp.concatenate([ar, ar, ac, ac], axis=-1)
    cos = jnp.concatenate([jnp.ones((n_ctx, dim), F32), jnp.cos(ang)], axis=0)
    sin = jnp.concatenate([jnp.zeros((n_ctx, dim), F32), jnp.sin(ang)], axis=0)
    return jnp.tile(cos, (1, 2)), jnp.tile(sin, (1, 2))


def _layout_attn_weights(w_in, w_uq, w_ukv):
    d = w_in.shape[0]
    s0 = MLA_Q_RANK
    s1 = s0 + MLA_KV_RANK
    s2 = s1 + MLA_ROPE
    s3 = s2 + SWA_HEADS * SWA_HEAD_DIM
    s4 = s3 + SWA_KV_HEADS * SWA_HEAD_DIM
    rep = lambda w: jnp.concatenate(
        [jnp.tile(w[:, g * SWA_HEAD_DIM:(g + 1) * SWA_HEAD_DIM], (1, V7X_LANES // SWA_HEAD_DIM))
         for g in range(SWA_KV_HEADS)], axis=1)
    win = jnp.concatenate([w_in[:, :s1], w_in[:, s2:s3], rep(w_in[:, s3:s4]), rep(w_in[:, s4:]),
                           w_in[:, s1:s2], jnp.zeros((d, V7X_LANES - MLA_ROPE), w_in.dtype)], axis=1)
    qh = MLA_NOPE + MLA_ROPE
    pad = jnp.zeros((w_uq.shape[0], V7X_MXU_DIM - qh), w_uq.dtype)
    wuq = jnp.concatenate([jnp.concatenate([w_uq[:, h * qh:(h + 1) * qh], pad], axis=1) for h in range(MLA_HEADS)], axis=1)
    kvh = MLA_NOPE + MLA_V
    wuk = jnp.concatenate([w_ukv[:, h * kvh:h * kvh + MLA_NOPE] for h in range(MLA_HEADS)], axis=1)
    wuvt = jnp.concatenate([w_ukv[:, h * kvh + MLA_NOPE:(h + 1) * kvh] for h in range(MLA_HEADS)], axis=1).T
    return win.astype(BF16), wuq.astype(BF16), wuk.astype(BF16), wuvt.astype(BF16)


def _lora_pair(w_down, w_up):
    rank = w_down.shape[2]
    down = jnp.concatenate([w_down[0], w_down[1]], axis=1)
    z = jnp.zeros((rank, w_up.shape[2]), w_up.dtype)
    up = jnp.stack([jnp.concatenate([w_up[0], z], axis=0), jnp.concatenate([z, w_up[1]], axis=0)], axis=0)
    return down.astype(BF16), up.astype(BF16)


def _head_block_diag():
    i = jnp.arange(V7X_MXU_DIM) // RWKV_HEAD
    return (i[:, None] == i[None, :]).astype(BF16)


def kernel(x, c, ctx, c_ctx, ada_w, ada_b, norm_mix, norm_ffn, norm_final, attn_w_in, attn_q_norm, attn_kv_norm, attn_w_uq, attn_w_ukv, attn_sinks, attn_w_o, rwkv_mu, rwkv_w_r, rwkv_w_k, rwkv_w_v, rwkv_w_o, rwkv_g1, rwkv_g2, rwkv_w0, rwkv_w1, rwkv_w2, rwkv_a0, rwkv_a1, rwkv_a2, rwkv_k_k, rwkv_k_a, rwkv_r_k, rwkv_ln_w, rwkv_ln_b, moe_router, moe_bias, moe_w_gate, moe_w_up, moe_w_down, moe_ws_gate, moe_ws_up, moe_ws_down):
    bsz, s, d = x.shape
    lc = ctx.shape[1]
    l = lc + s
    depth = ada_w.shape[0]
    nct = lc // TOKEN_TILE
    assert lc % TOKEN_TILE == 0 and s % TOKEN_TILE == 0 and s >= SWA_BAND and lc % SWA_Q_TILE == 0
    assert lc % (WKV_CHUNK * WKV_CHUNKS_PER_STEP) == 0
    assert d % V7X_MXU_DIM == 0 and WKV_CHUNK * 2 == V7X_LANES
    ngrp = SAMPLE_GROUPS
    bg = bsz // ngrp
    assert bsz % ngrp == 0 and bg % LAST_COMBINE_PARTS == 0
    assert (bg // LAST_COMBINE_PARTS * l) % (8 * V7X_SC_WORKERS) == 0

    assert ngrp == 2
    cos, sin = _rope_table(s, lc)
    bd = _head_block_diag()
    rows = -(-(bsz + 1) // 8) * 8
    cc = jnp.concatenate([c, c_ctx[None, :], jnp.zeros((rows - bsz - 1, d), F32)], axis=0)
    row2 = lambda a: a.reshape(1, -1)
    moe_w = (moe_w_gate, moe_w_up, moe_w_down)
    moe_ws = (moe_ws_gate, moe_ws_up, moe_ws_down)

    shared = {}

    def layer_weights(li):
        if li not in shared:
            i = li // 2
            ada = _ada_mods(cc, ada_w, ada_b, li)
            w = dict(
                mods=jnp.stack([jnp.broadcast_to(ada[bsz].reshape(1, N_MODS, d), (bsz, N_MODS, d)),
                                ada[:bsz].reshape(bsz, N_MODS, d)], axis=1),
                wrt=moe_router[li].T,
                bias=moe_bias[li].reshape(N_GROUPS, GROUP_SIZE, 1))
            if li % 2 == 0:
                w["win"], w["wuq"], w["wuk"], w["wuvt"] = _layout_attn_weights(attn_w_in[i], attn_w_uq[i], attn_w_ukv[i])
                w["wo"] = attn_w_o[i].astype(BF16)
            else:
                w["w1"], w["w2"] = _lora_pair(rwkv_w1[i], rwkv_w2[i])
                w["a1"], w["a2"] = _lora_pair(rwkv_a1[i], rwkv_a2[i])
                w["wr"], w["wk"], w["wv"], w["wo"] = [x[i].astype(BF16) for x in (rwkv_w_r, rwkv_w_k, rwkv_w_v, rwkv_w_o)]
                w["g1"], w["g2"] = rwkv_g1[i].astype(BF16), rwkv_g2[i].astype(BF16)
            shared[li] = w
        return shared[li]

    groups = [dict(stream=(ctx, x, g * bg, 0), b0=g * bg) for g in range(ngrp)]
    result = [None]

    def run_stage(st, li, name, dep):
        w = layer_weights(li)
        i = li // 2
        with_ctx = li < depth - 1
        mods = w["mods"][st["b0"]:st["b0"] + bg]
        if name == "proj" and li % 2 == 0:
            st["qkv"] = _attn_proj(st["stream"], bg, l, mods, row2(norm_mix[li]), w["win"], row2(attn_q_norm[i]),
                                   row2(attn_kv_norm[i]), w["wuq"], w["wuk"], w["wuvt"], cos, sin, nct, dep=dep)
            return st["qkv"][0]
        if name == "mid" and li % 2 == 0:
            q, k, vt, qs, ks, vs = st.pop("qkv")
            st["a"] = _mla_attention(q, k, vt, lc, 0 if with_ctx else lc // MLA_Q_TILE, dep=dep)
            st["bm"] = _swa_attention(attn_sinks[i], qs, ks, vs, lc, 0 if with_ctx else lc // SWA_Q_TILE, dep=st["a"])
            return st["bm"]
        if name == "proj":
            assert st["stream"][0] is st["stream"][1]
            st["feat"] = _rwkv_proj(st["stream"][0], mods, row2(norm_mix[li]), rwkv_mu[i], w["wr"], w["wk"], w["wv"],
                                    w["g1"], w["g2"], w["w1"], w["w2"], w["a1"], w["a2"], rwkv_w0[i], rwkv_a0[i],
                                    row2(rwkv_k_k[i]), row2(rwkv_k_a[i]), row2(rwkv_r_k[i]), bd, nct, dep=dep)
            return st["feat"][0]
        if name == "mid":
            r, v, kk, gt, km, bv, lw, bonus = st.pop("feat")
            st["y"] = _wkv(r, v, kk, km, bv, lw, lc, dep=dep)
            st["gate"], st["bonus"] = gt, bonus
            return st["y"]
        if name == "out":
            if li % 2 == 0:
                tail = _attn_out(st.pop("a"), st.pop("bm"), st["stream"], mods, w["wo"], row2(norm_ffn[li]),
                                 w["wrt"], w["bias"], nct, dep=dep)
            else:
                tail = _rwkv_out(st.pop("y"), st.pop("bonus"), st.pop("gate"), row2(rwkv_ln_w[i]), row2(rwkv_ln_b[i]),
                                 w["wo"], bd, st["stream"][0], mods, row2(norm_ffn[li]), w["wrt"], w["bias"], nct, dep=dep)
            st["h"], st["n2p"], eid, rank, st["wcols"], counts = tail
            st["xs"], st["dest"], st["tile_expert"], st["n_valid"] = _moe_route_rows(st["n2p"], eid, rank, counts, bg, l)
            return st["h"]
        if name == "experts":
            ys = _moe_experts(st.pop("tile_expert"), st.pop("n_valid"), st.pop("xs"), *moe_w, li, dep=dep)
            bp = bg // (LAST_COMBINE_PARTS if li == depth - 1 else 1)
            dest = st.pop("dest")
            st["yg"] = [_sc_gather(ys, dest[:, p * bp * l:(p + 1) * bp * l], bp * l).reshape(TOP_K, bp, l, d // 2)
                        for p in range(bg // bp)]
            return ys
        assert name == "combine"
        last = li == depth - 1
        wcols, n2p, hres = st.pop("wcols"), st.pop("n2p"), st.pop("h")
        for p, yg in enumerate(st.pop("yg")):
            bp = yg.shape[1]
            h = _moe_combine(yg, wcols, n2p, *moe_ws, hres, mods, row2(norm_final), nct, li,
                             result[0] if last else None, st["b0"] + p * bp if last else 0, bsz if last else bg,
                             last, last, in_b0=p * bp, dep=dep)
            if last:
                result[0] = h
        if not last:
            st["stream"] = (h, h, 0, nct)
        return h

    order = [(0, 0, "proj"), (0, 0, "mid")]
    for li in range(depth):
        order += [(0, li, "out"), (1, li, "proj"), (0, li, "experts"), (1, li, "mid")]
        if li < depth - 1:
            order += [(0, li, "combine"), (1, li, "out"), (0, li + 1, "proj"), (1, li, "experts"),
                      (0, li + 1, "mid"), (1, li, "combine")]
        else:
            order += [(1, li, "out"), (0, li, "combine"), (1, li, "experts"), (1, li, "combine")]
    dep = None
    for g, li, name in order:
        dep = run_stage(groups[g], li, name, dep)
    return result[0]
```
